```python
import math
import jax
import jax.numpy as jnp
from jax import lax
import numpy as np

D_MODEL = 1024
BATCH = 4
SEQ = 4096
DEPTH = 1

POOL_WINDOWS = (2, 4, 8, 16)
POOL_GROUPS = 4
POOL_GROUP_DIM = D_MODEL // 8
POOL_WIDTH = POOL_GROUPS * POOL_GROUP_DIM
HEAD_DIM = 64
ATT_PATTERNS = ((128, 1), (512, 4), (2048, 16))
HEADS_PER_GROUP = 4
N_ATT_HEADS = HEADS_PER_GROUP * len(ATT_PATTERNS)
ATT_WIDTH = N_ATT_HEADS * HEAD_DIM
ATT_OUT_WIDTH = HEADS_PER_GROUP * HEAD_DIM
ALIBI_MAX_BIAS = 8.0
ATT_BLOCK = 128
IN_WIDTH = POOL_WIDTH + 3 * ATT_WIDTH + 2 * D_MODEL
N_EXPERT_GROUPS = 4
EXPERTS_PER_GROUP = 8
N_EXPERTS = N_EXPERT_GROUPS * EXPERTS_PER_GROUP
EXPERT_TOPK = 2
D_EXPERT = D_MODEL // 2
MOE_BLOCK = 128
RMS_EPS = 1e-6

kernel_name = 'hybrid_pool_dilated_attn_hmoe_block'


def _rmsnorm(x, g):
    xf = x.astype(jnp.float32)
    y = xf * lax.rsqrt(jnp.mean(xf * xf, axis=-1, keepdims=True) + RMS_EPS)
    return (y * g.astype(jnp.float32)).astype(x.dtype)


def _alibi_slopes():
    h = jnp.arange(1, N_ATT_HEADS + 1, dtype=jnp.float32)
    return jnp.exp2(-ALIBI_MAX_BIAS * h / N_ATT_HEADS)


def _multiscale_pool(u):
    b, s, _ = u.shape
    ug = u.astype(jnp.float32).reshape(b, s, POOL_GROUPS, POOL_GROUP_DIM)
    cs = jnp.cumsum(ug, axis=1)
    t = jnp.arange(s)
    outs = []
    for g, w in enumerate(POOL_WINDOWS):
        csg = cs[:, :, g]
        lagged = jnp.pad(csg[:, : s - w], ((0, 0), (w, 0), (0, 0)))
        count = jnp.minimum(t + 1, w).astype(jnp.float32)[None, :, None]
        outs.append((csg - lagged) / count - ug[:, :, g])
    return jnp.stack(outs, axis=2).astype(u.dtype)


def _dilated_window_attention(q, k, v, window, dilation, slopes):
    b, s, h, hd = q.shape
    sub_len = s // dilation
    nb = -(-sub_len // ATT_BLOCK)
    padded = nb * ATT_BLOCK
    reach = window // dilation

    def to_sub(t):
        t = t.reshape(b, sub_len, dilation, h, hd).transpose(0, 2, 3, 1, 4)
        return jnp.pad(t, ((0, 0), (0, 0), (0, 0), (0, padded - sub_len), (0, 0)))

    def key_blocks(t):
        t = jnp.pad(t, ((0, 0), (0, 0), (0, 0), (ATT_BLOCK, 0), (0, 0)))
        t = t.reshape(b, dilation, h, nb + 1, ATT_BLOCK, hd)
        return jnp.concatenate([t[:, :, :, :-1], t[:, :, :, 1:]], axis=4)

    qb = to_sub(q).reshape(b, dilation, h, nb, ATT_BLOCK, hd)
    kb = key_blocks(to_sub(k))
    vb = key_blocks(to_sub(v))
    scores = jnp.einsum('brhnqd,brhnkd->brhnqk', qb, kb,
                        preferred_element_type=jnp.float32) * (hd ** -0.5)
    qi = jnp.arange(ATT_BLOCK)[:, None]
    ki = jnp.arange(2 * ATT_BLOCK)[None, :]
    delta = ATT_BLOCK + qi - ki
    key_sub = (jnp.arange(nb)[:, None, None] - 1) * ATT_BLOCK + ki[None]
    valid = (delta >= 0) & (delta <= reach) & (key_sub >= 0)
    bias = -slopes[:, None, None] * (delta * dilation).astype(jnp.float32)
    scores = jnp.where(valid[None, None, None], scores + bias[None, None, :, None], -jnp.inf)
    mx = jnp.max(scores, axis=-1, keepdims=True)
    p = jnp.exp(scores - mx)
    den = jnp.sum(p, axis=-1, keepdims=True)
    o = jnp.einsum('brhnqk,brhnkd->brhnqd', p, vb.astype(jnp.float32)) / den
    lse = (mx + jnp.log(den))[..., 0]
    o = o.reshape(b, dilation, h, padded, hd)[:, :, :, :sub_len]
    o = o.transpose(0, 3, 1, 2, 4).reshape(b, s, h, hd)
    lse = lse.reshape(b, dilation, h, padded)[..., :sub_len]
    lse = lse.transpose(0, 3, 1, 2).reshape(b, s, h)
    return o, lse


def _token_mixer(h, w_in, w_pool_group, pool_scale, w_branch_pool, w_branch_att, w_out):
    b, s, _ = h.shape
    proj = h @ w_in
    splits = [POOL_WIDTH, POOL_WIDTH + ATT_WIDTH, POOL_WIDTH + 2 * ATT_WIDTH,
              POOL_WIDTH + 3 * ATT_WIDTH, POOL_WIDTH + 3 * ATT_WIDTH + D_MODEL]
    u, q, k, v, gate_pool, gate_att = jnp.split(proj, splits, axis=-1)

    pooled = _multiscale_pool(u)
    pooled = jnp.einsum('bsgc,gcd->bsgd', pooled, w_pool_group).reshape(b, s, POOL_WIDTH)
    y_pool = (pooled * pool_scale) @ w_branch_pool

    q = q.reshape(b, s, N_ATT_HEADS, HEAD_DIM)
    k = k.reshape(b, s, N_ATT_HEADS, HEAD_DIM)
    v = v.reshape(b, s, N_ATT_HEADS, HEAD_DIM)
    slopes = _alibi_slopes()
    outs, lses = [], []
    for g, (window, dilation) in enumerate(ATT_PATTERNS):
        hs = slice(g * HEADS_PER_GROUP, (g + 1) * HEADS_PER_GROUP)
        o, l = _dilated_window_attention(q[:, :, hs], k[:, :, hs], v[:, :, hs],
                                         window, dilation, slopes[hs])
        outs.append(o)
        lses.append(l)
    weights = jax.nn.softmax(jnp.stack(lses, axis=0), axis=0)
    o = jnp.sum(weights[..., None] * jnp.stack(outs, axis=0), axis=0)
    y_att = o.reshape(b, s, ATT_OUT_WIDTH).astype(h.dtype) @ w_branch_att

    merged = jax.nn.sigmoid(gate_pool) * y_pool + jax.nn.sigmoid(gate_att) * y_att
    return merged @ w_out


def _hier_moe(h, w_group_router, b_group_router, w_expert_router, b_expert_router,
              w_gate, w_up, w_down):
    t_count = h.shape[0]
    hf = h.astype(jnp.float32)
    g_logits = hf @ w_group_router.astype(jnp.float32) + b_group_router.astype(jnp.float32)
    g_prob = jax.nn.softmax(g_logits, axis=-1)
    g_sel = jnp.argmax(g_logits, axis=-1)
    p_group = jnp.take_along_axis(g_prob, g_sel[:, None], axis=1)[:, 0]
    e_logits = (hf @ w_expert_router.astype(jnp.float32)
                + b_expert_router.astype(jnp.float32)).reshape(t_count, N_EXPERT_GROUPS, EXPERTS_PER_GROUP)
    e_logits = jnp.take_along_axis(e_logits, g_sel[:, None, None], axis=1)[:, 0]
    top_v, top_i = lax.top_k(e_logits, EXPERT_TOPK)
    gate_w = p_group[:, None] * jax.nn.softmax(top_v, axis=-1)
    expert_id = g_sel[:, None] * EXPERTS_PER_GROUP + top_i

    n_assign = t_count * EXPERT_TOPK
    e_a = expert_id.reshape(-1).astype(jnp.int32)
    tok_a = jnp.repeat(jnp.arange(t_count, dtype=jnp.int32), EXPERT_TOPK)
    w_a = gate_w.reshape(-1)
    order = jnp.argsort(e_a)
    e_s, tok_s, w_s = e_a[order], tok_a[order], w_a[order]
    counts = jnp.zeros((N_EXPERTS,), jnp.int32).at[e_a].add(1)
    pad_counts = ((counts + MOE_BLOCK - 1) // MOE_BLOCK) * MOE_BLOCK
    pad_end = jnp.cumsum(pad_counts)
    pad_start = pad_end - pad_counts
    raw_start = jnp.cumsum(counts) - counts
    dest = pad_start[e_s] + jnp.arange(n_assign, dtype=jnp.int32) - raw_start[e_s]
    n_blocks = -(-n_assign // MOE_BLOCK) + N_EXPERTS
    n_slots = n_blocks * MOE_BLOCK
    slot_tok = jnp.zeros((n_slots,), jnp.int32).at[dest].set(tok_s)
    slot_w = jnp.zeros((n_slots,), h.dtype).at[dest].set(w_s.astype(h.dtype))
    block_start = jnp.arange(n_blocks, dtype=jnp.int32) * MOE_BLOCK
    block_expert = jnp.minimum(jnp.searchsorted(pad_end, block_start, side='right'),
                               N_EXPERTS - 1).astype(jnp.int32)

    def run_block(args):
        tok, e = args
        xb = h[tok]
        a = xb @ w_gate[e]
        u = xb @ w_up[e]
        return (jax.nn.silu(a) * u) @ w_down[e]

    y = lax.map(run_block, (slot_tok.reshape(n_blocks, MOE_BLOCK), block_expert))
    y = y.reshape(n_slots, D_MODEL) * slot_w[:, None]
    return jnp.zeros((t_count, D_MODEL), h.dtype).at[slot_tok].add(y)


def setup_inputs(seed: int = 0) -> dict:
    key = jax.random.key(seed)
    ks = jax.random.split(key, 24)
    f32 = jnp.float32
    L, D = DEPTH, D_MODEL

    def nrm(k, shape, scale):
        return jax.random.normal(k, shape, f32) * scale

    return {
        'x': nrm(ks[0], (BATCH, SEQ, D), 1.0),
        'c': nrm(ks[1], (BATCH, D), 1.0),
        'w_ada': nrm(ks[2], (L, D, 6 * D), 0.5 * D ** -0.5),
        'b_ada': nrm(ks[3], (L, 6 * D), 0.02),
        'g_pre_mix': 1.0 + nrm(ks[4], (L, D), 0.02),
        'g_post_mix': 1.0 + nrm(ks[5], (L, D), 0.02),
        'g_pre_ffn': 1.0 + nrm(ks[6], (L, D), 0.02),
        'g_post_ffn': 1.0 + nrm(ks[7], (L, D), 0.02),
        'w_in': nrm(ks[8], (L, D, IN_WIDTH), D ** -0.5),
        'w_pool_group': nrm(ks[9], (L, POOL_GROUPS, POOL_GROUP_DIM, POOL_GROUP_DIM), POOL_GROUP_DIM ** -0.5),
        'pool_scale': 1.0 + nrm(ks[10], (L, POOL_WIDTH), 0.02),
        'w_branch_pool': nrm(ks[11], (L, POOL_WIDTH, D), POOL_WIDTH ** -0.5),
        'w_branch_att': nrm(ks[12], (L, ATT_OUT_WIDTH, D), ATT_OUT_WIDTH ** -0.5),
        'w_out': nrm(ks[13], (L, D, D), D ** -0.5),
        'w_group_router': nrm(ks[14], (L, D, N_EXPERT_GROUPS), D ** -0.5),
        'b_group_router': nrm(ks[15], (L, N_EXPERT_GROUPS), 0.01),
        'w_expert_router': nrm(ks[16], (L, D, N_EXPERTS), D ** -0.5),
        'b_expert_router': nrm(ks[17], (L, N_EXPERTS), 0.01),
        'w_exp_gate': nrm(ks[18], (L, N_EXPERTS, D, D_EXPERT), D ** -0.5),
        'w_exp_up': nrm(ks[19], (L, N_EXPERTS, D, D_EXPERT), D ** -0.5),
        'w_exp_down': nrm(ks[20], (L, N_EXPERTS, D_EXPERT, D), D_EXPERT ** -0.5),
    }


def reference(x, c, w_ada, b_ada, g_pre_mix, g_post_mix, g_pre_ffn, g_post_ffn, w_in,
              w_pool_group, pool_scale, w_branch_pool, w_branch_att, w_out,
              w_group_router, b_group_router, w_expert_router, b_expert_router,
              w_exp_gate, w_exp_up, w_exp_down):
    b, s, d = x.shape
    for layer in range(DEPTH):
        mod = jax.nn.silu(c) @ w_ada[layer] + b_ada[layer]
        sh_m, sc_m, gt_m, sh_f, sc_f, gt_f = jnp.split(mod, 6, axis=-1)
        h = _rmsnorm(x, g_pre_mix[layer]) * (1.0 + sc_m[:, None]) + sh_m[:, None]
        y = _token_mixer(h, w_in[layer], w_pool_group[layer], pool_scale[layer],
                         w_branch_pool[layer], w_branch_att[layer], w_out[layer])
        x = x + gt_m[:, None] * _rmsnorm(y, g_post_mix[layer])
        h = _rmsnorm(x, g_pre_ffn[layer]) * (1.0 + sc_f[:, None]) + sh_f[:, None]
        y = _hier_moe(h.reshape(b * s, d), w_group_router[layer], b_group_router[layer],
                      w_expert_router[layer], b_expert_router[layer],
                      w_exp_gate[layer], w_exp_up[layer], w_exp_down[layer]).reshape(b, s, d)
        x = x + gt_f[:, None] * _rmsnorm(y, g_post_ffn[layer])
    return x
```

```python
import functools

import jax
import jax.numpy as jnp
from jax import lax
from jax.experimental import pallas as pl
from jax.experimental.pallas import tpu as pltpu

F32 = jnp.float32
BF16 = jnp.bfloat16
HIGHEST = lax.Precision.HIGHEST

D_MODEL = 1024
LANES = 128
SUBLANES = 8
ROW_TILES = D_MODEL // LANES

POOL_WINDOWS = (2, 4, 8, 16)
POOL_GROUP_DIM = 128
POOL_WIDTH = 512
POOL_HALO = 16

HEAD_DIM = 64
ATT_DILATIONS = (1, 4, 16)
ATT_REACH = 128
ATT_BLOCK = 128
HEADS_PER_GROUP = 4
N_ATT_HEADS = 12
GROUP_QKV = 3 * HEADS_PER_GROUP * HEAD_DIM
ATT_OUT_WIDTH = 256
ALIBI_MAX_BIAS = 8.0
IN_WIDTH = POOL_WIDTH + 3 * GROUP_QKV + 2 * D_MODEL
MASKED = -1e30

N_EXPERT_GROUPS = 4
EXPERTS_PER_GROUP = 8
N_EXPERTS = 32
D_EXPERT = 512
RMS_EPS = 1e-6

ROW_TILE = 512
MOE_BLOCK = 256
COMBINE_TILE = 256
VMEM_LIMIT = 52 * 1024 * 1024


def _sigmoid(x):
    return 1.0 / (1.0 + jnp.exp(-x))


def _rmsnorm(x, g):
    return x * lax.rsqrt(jnp.mean(x * x, axis=-1, keepdims=True) + RMS_EPS) * g


def _dot(a, b):
    return jnp.dot(a, b, preferred_element_type=F32)


def _rows_to_tiles(ref, val, n):
    for s in range(ROW_TILES):
        ref[pl.ds(s, n, stride=ROW_TILES), :] = val[:, s * LANES:(s + 1) * LANES]


def _tiles_to_rows(ref, n):
    return jnp.concatenate(
        [ref[pl.ds(s, n, stride=ROW_TILES), :] for s in range(ROW_TILES)], axis=1)


def _adaln_kernel(c_ref, w_ref, b_ref, o_ref):
    c = c_ref[...]
    a = c * _sigmoid(c)
    o_ref[...] = jnp.dot(a, w_ref[...], preferred_element_type=F32, precision=HIGHEST) + b_ref[...]


def _adaln(c, w_ada, b_ada):
    b, d = c.shape
    n = w_ada.shape[1]
    rows = -(-b // SUBLANES) * SUBLANES
    cp = jnp.pad(c, ((0, rows - b), (0, 0)))
    nt = 1536
    out = pl.pallas_call(
        _adaln_kernel,
        grid=(n // nt,),
        in_specs=[pl.BlockSpec((rows, d), lambda j: (0, 0)),
                  pl.BlockSpec((d, nt), lambda j: (0, j)),
                  pl.BlockSpec((1, nt), lambda j: (0, j))],
        out_specs=pl.BlockSpec((rows, nt), lambda j: (0, j)),
        out_shape=jax.ShapeDtypeStruct((rows, n), F32),
        compiler_params=pltpu.CompilerParams(vmem_limit_bytes=VMEM_LIMIT),
        name="adaln",
    )(cp, w_ada, b_ada.reshape(1, n))
    return out[:b]


def _inproj_kernel(x_ref, g_ref, sc_ref, sh_ref, w_ref,
                   u_ref, qkv0_ref, qkv1_ref, qkv2_ref, gate_ref, h_scr, p_scr):
    tm = x_ref.shape[1]
    h = _rmsnorm(x_ref[0], g_ref[...]) * (1.0 + sc_ref[0]) + sh_ref[0]
    h_scr[...] = h.astype(BF16)

    u_ref[0] = _dot(h_scr[...], w_ref[:, 0:POOL_WIDTH]).astype(BF16)

    col = POOL_WIDTH
    qkv0_ref[0, 0] = _dot(h_scr[...], w_ref[:, col:col + GROUP_QKV]).astype(BF16)
    for out_ref, d in ((qkv1_ref, ATT_DILATIONS[1]), (qkv2_ref, ATT_DILATIONS[2])):
        col += GROUP_QKV
        proj = _dot(h_scr[...], w_ref[:, col:col + GROUP_QKV])
        for cb in range(GROUP_QKV // LANES):
            p_scr[cb] = proj[:, cb * LANES:(cb + 1) * LANES]
        for r in range(d):
            out_ref[0, r] = jnp.concatenate(
                [p_scr[cb, pl.ds(r, tm // d, stride=d), :] for cb in range(GROUP_QKV // LANES)],
                axis=1).astype(BF16)
    col += GROUP_QKV

    chunk = 512
    for j in range(2 * D_MODEL // chunk):
        g = _dot(h_scr[...], w_ref[:, col + j * chunk:col + (j + 1) * chunk])
        gate_ref[0, :, j * chunk:(j + 1) * chunk] = _sigmoid(g).astype(BF16)


def _inproj(x, g_pre, sc, sh, w_perm):
    b, s, d = x.shape
    tm = ROW_TILE
    d1, d2 = ATT_DILATIONS[1], ATT_DILATIONS[2]
    grid = (b, s // tm)
    const = lambda bi, i: (0, 0)
    per_b = lambda bi, i: (bi, 0, 0)
    return pl.pallas_call(
        _inproj_kernel,
        grid=grid,
        in_specs=[pl.BlockSpec((1, tm, d), lambda bi, i: (bi, i, 0)),
                  pl.BlockSpec((1, d), const),
                  pl.BlockSpec((1, 1, d), per_b),
                  pl.BlockSpec((1, 1, d), per_b),
                  pl.BlockSpec((d, IN_WIDTH), const, pipeline_mode=pl.Buffered(1))],
        out_specs=[pl.BlockSpec((1, tm, POOL_WIDTH), lambda bi, i: (bi, i, 0)),
                   pl.BlockSpec((1, 1, tm, GROUP_QKV), lambda bi, i: (bi, 0, i, 0)),
                   pl.BlockSpec((1, d1, tm // d1, GROUP_QKV), lambda bi, i: (bi, 0, i, 0)),
                   pl.BlockSpec((1, d2, tm // d2, GROUP_QKV), lambda bi, i: (bi, 0, i, 0)),
                   pl.BlockSpec((1, tm, 2 * D_MODEL), lambda bi, i: (bi, i, 0))],
        out_shape=[jax.ShapeDtypeStruct((b, s, POOL_WIDTH), BF16),
                   jax.ShapeDtypeStruct((b, 1, s, GROUP_QKV), BF16),
                   jax.ShapeDtypeStruct((b, d1, s // d1, GROUP_QKV), BF16),
                   jax.ShapeDtypeStruct((b, d2, s // d2, GROUP_QKV), BF16),
                   jax.ShapeDtypeStruct((b, s, 2 * D_MODEL), BF16)],
        scratch_shapes=[pltpu.VMEM((tm, d), BF16), pltpu.VMEM((GROUP_QKV // LANES, tm, LANES), F32)],
        compiler_params=pltpu.CompilerParams(
            dimension_semantics=("arbitrary", "arbitrary"), vmem_limit_bytes=VMEM_LIMIT),
        name="inproj",
    )(x, g_pre, sc, sh, w_perm)


def _attn_kernel(slopes_ref, q0, k0, v0, q1, k1, v1, q2, k2, v2, o_ref,
                 bias_scr, acc, mrun, lrun):
    pair = pl.program_id(1)
    seq = o_ref.shape[1]
    nblk = seq // ATT_BLOCK
    lane = lax.broadcasted_iota(jnp.int32, (ATT_BLOCK, LANES), 1)
    first_head = lane < HEAD_DIM

    qi = lax.broadcasted_iota(jnp.int32, (ATT_BLOCK, 2 * ATT_BLOCK), 0)
    kj = lax.broadcasted_iota(jnp.int32, (ATT_BLOCK, 2 * ATT_BLOCK), 1)
    delta = ATT_BLOCK + qi - kj
    valid = (delta >= 0) & (delta <= ATT_REACH)
    delta0 = qi - kj
    valid0 = delta0 >= 0
    for g, d in enumerate(ATT_DILATIONS):
        for j in range(2):
            slope = slopes_ref[g * HEADS_PER_GROUP + 2 * pair + j]
            bias_scr[g, j, 0] = jnp.where(valid0, -slope * (delta0 * d).astype(F32), MASKED)
            bias_scr[g, j, 1] = jnp.where(valid, -slope * (delta * d).astype(F32), MASKED)

    def block_softmax(g, q_ref, k_ref, v_ref, r, i):
        lo = jnp.maximum(i - 1, 0)
        variant = jnp.minimum(i, 1)
        q = q_ref[0, r, pl.ds(pl.multiple_of(i * ATT_BLOCK, ATT_BLOCK), ATT_BLOCK), :]
        kw = k_ref[0, r, pl.ds(pl.multiple_of(lo * ATT_BLOCK, ATT_BLOCK), 2 * ATT_BLOCK), :]
        vw = v_ref[0, r, pl.ds(pl.multiple_of(lo * ATT_BLOCK, ATT_BLOCK), 2 * ATT_BLOCK), :]
        qf = q.astype(F32)
        outs = []
        for j in range(2):
            head = first_head if j == 0 else jnp.logical_not(first_head)
            qh = jnp.where(head, qf, 0.0).astype(BF16)
            sc = lax.dot_general(qh, kw, (((1,), (1,)), ((), ())), preferred_element_type=F32)
            sc = sc + bias_scr[g, j, variant]
            mx = jnp.max(sc, axis=-1, keepdims=True)
            p = jnp.exp(sc - mx)
            den = jnp.sum(p, axis=-1, keepdims=True)
            num = _dot(p.astype(BF16), vw)
            outs.append((num, mx, den))
        (n0, m0, l0), (n1, m1, l1) = outs
        return (jnp.where(first_head, n0, n1),
                jnp.where(first_head, m0, m1),
                jnp.where(first_head, l0, l1))

    def first_group(n, carry):
        num, mx, den = block_softmax(0, q0, k0, v0, 0, n)
        rows = pl.ds(pl.multiple_of(n * ATT_BLOCK, ATT_BLOCK), ATT_BLOCK)
        acc[rows, :] = num
        mrun[rows, :] = mx
        lrun[rows, :] = den
        return carry

    lax.fori_loop(0, nblk, first_group, 0)

    def merge_group(g, q_ref, k_ref, v_ref):
        d = ATT_DILATIONS[g]
        per_res = nblk // d

        def body(n, carry):
            r = n // per_res
            i = n % per_res
            num, mx, den = block_softmax(g, q_ref, k_ref, v_ref, r, i)
            rows = pl.ds(i * (ATT_BLOCK * d) + r, ATT_BLOCK, stride=d)
            m_old = mrun[rows, :]
            m_new = jnp.maximum(m_old, mx)
            a = jnp.exp(m_old - m_new)
            c = jnp.exp(mx - m_new)
            acc[rows, :] = a * acc[rows, :] + c * num
            lrun[rows, :] = a * lrun[rows, :] + c * den
            mrun[rows, :] = m_new
            return carry

        lax.fori_loop(0, nblk, body, 0)

    merge_group(1, q1, k1, v1)
    merge_group(2, q2, k2, v2)

    def finish(n, carry):
        rows = pl.ds(pl.multiple_of(n * ATT_BLOCK, ATT_BLOCK), ATT_BLOCK)
        o_ref[0, rows, :] = (acc[rows, :] / lrun[rows, :]).astype(BF16)
        return carry

    lax.fori_loop(0, nblk, finish, 0)


def _attention(qkv0, qkv1, qkv2, slopes):
    b, _, s, _ = qkv0.shape
    pairs = HEADS_PER_GROUP // 2
    col_blocks = HEADS_PER_GROUP * HEAD_DIM // LANES

    def specs(arr):
        _, d, sub, _ = arr.shape
        return [pl.BlockSpec((1, d, sub, LANES),
                             functools.partial(lambda bi, p, sec: (bi, 0, 0, sec * col_blocks + p), sec=sec))
                for sec in range(3)]

    return pl.pallas_call(
        _attn_kernel,
        grid=(b, pairs),
        in_specs=[pl.BlockSpec(memory_space=pltpu.SMEM)] + specs(qkv0) + specs(qkv1) + specs(qkv2),
        out_specs=pl.BlockSpec((1, s, LANES), lambda bi, p: (bi, 0, p)),
        out_shape=jax.ShapeDtypeStruct((b, s, ATT_OUT_WIDTH), BF16),
        scratch_shapes=[pltpu.VMEM((3, 2, 2, ATT_BLOCK, 2 * ATT_BLOCK), F32),
                        pltpu.VMEM((s, LANES), F32),
                        pltpu.VMEM((s, LANES), F32),
                        pltpu.VMEM((s, LANES), F32)],
        compiler_params=pltpu.CompilerParams(
            dimension_semantics=("arbitrary", "arbitrary"), vmem_limit_bytes=VMEM_LIMIT),
        name="attention",
    )(slopes, qkv0, qkv0, qkv0, qkv1, qkv1, qkv1, qkv2, qkv2, qkv2)


def _mixtail_kernel(u_ref, halo_ref, att_ref, gate_ref, x_ref,
                    gt_m_ref, sc_f_ref, sh_f_ref, g_post_ref, g_pre_ref,
                    wpg_ref, pscale_ref, wbp_ref, wba_ref, wout_ref, wr_ref, br_ref,
                    xmid_ref, h2_ref, route_ref, pu):
    i = pl.program_id(1)
    tm = x_ref.shape[1]

    halo = halo_ref[0].astype(F32)
    pu[0:POOL_HALO, :] = jnp.where(i > 0, halo, jnp.zeros_like(halo))
    pu[POOL_HALO:POOL_HALO + tm, :] = u_ref[0].astype(F32)
    t = i * tm + lax.broadcasted_iota(jnp.int32, (tm, 1), 0)
    y_pool = jnp.zeros((tm, D_MODEL), F32)
    for g, w in enumerate(POOL_WINDOWS):
        cols = slice(g * POOL_GROUP_DIM, (g + 1) * POOL_GROUP_DIM)
        win = pu[pl.ds(POOL_HALO, tm), cols]
        for j in range(1, w):
            win = win + pu[pl.ds(POOL_HALO - j, tm), cols]
        count = jnp.minimum(t + 1, w).astype(F32)
        pooled = win / count - pu[pl.ds(POOL_HALO, tm), cols]
        mixed = _dot(pooled.astype(BF16), wpg_ref[g]) * pscale_ref[:, cols]
        y_pool = y_pool + _dot(mixed.astype(BF16), wbp_ref[cols, :])

    y_att = _dot(att_ref[0], wba_ref[...])
    merged = (gate_ref[0, :, 0:D_MODEL].astype(F32) * y_pool
              + gate_ref[0, :, D_MODEL:2 * D_MODEL].astype(F32) * y_att)
    y = _dot(merged.astype(BF16), wout_ref[...])
    x_mid = x_ref[0] + gt_m_ref[0] * _rmsnorm(y, g_post_ref[...])
    xmid_ref[0] = x_mid

    h2 = _rmsnorm(x_mid, g_pre_ref[...]) * (1.0 + sc_f_ref[0]) + sh_f_ref[0]
    _rows_to_tiles(h2_ref, h2, tm)

    logits = jnp.dot(h2, wr_ref[...], preferred_element_type=F32, precision=HIGHEST) + br_ref[...]
    lane = lax.broadcasted_iota(jnp.int32, (tm, LANES), 1)
    lane_f = lane.astype(F32)
    neg_inf = -jnp.inf
    far = float(LANES)
    gmask = (lane >= N_EXPERTS) & (lane < N_EXPERTS + N_EXPERT_GROUPS)
    gl = jnp.where(gmask, logits, neg_inf)
    gmax = jnp.max(gl, axis=-1, keepdims=True)
    gsel = jnp.min(jnp.where(gl == gmax, lane_f, far), axis=-1, keepdims=True) - float(N_EXPERTS)
    p_group = 1.0 / jnp.sum(jnp.where(gmask, jnp.exp(gl - gmax), 0.0), axis=-1, keepdims=True)
    e_lo = gsel * float(EXPERTS_PER_GROUP)
    emask = (lane_f >= e_lo) & (lane_f < e_lo + float(EXPERTS_PER_GROUP))
    el = jnp.where(emask, logits, neg_inf)
    v1 = jnp.max(el, axis=-1, keepdims=True)
    i1 = jnp.min(jnp.where(el == v1, lane_f, far), axis=-1, keepdims=True)
    el2 = jnp.where(lane_f == i1, neg_inf, el)
    v2 = jnp.max(el2, axis=-1, keepdims=True)
    i2 = jnp.min(jnp.where(el2 == v2, lane_f, far), axis=-1, keepdims=True)
    e21 = jnp.exp(v2 - v1)
    w1 = p_group / (1.0 + e21)
    w2 = p_group * e21 / (1.0 + e21)
    route_ref[...] = jnp.where(lane == 0, i1, jnp.where(lane == 1, i2,
                               jnp.where(lane == 2, w1, jnp.where(lane == 3, w2, 0.0))))


def _mixtail(u, att, gates, x, gt_m, sc_f, sh_f, g_post, g_pre,
             wpg, pscale, wbp, wba, wout, wr, br):
    b, s, d = x.shape
    tm = ROW_TILE
    tiles = s // tm
    halo_blocks = tm // POOL_HALO
    const2 = lambda bi, i: (0, 0)
    const3 = lambda bi, i: (0, 0, 0)
    per_b = lambda bi, i: (bi, 0, 0)
    tile = lambda bi, i: (bi, i, 0)
    single = dict(pipeline_mode=pl.Buffered(1))
    return pl.pallas_call(
        _mixtail_kernel,
        grid=(b, tiles),
        in_specs=[pl.BlockSpec((1, tm, POOL_WIDTH), tile),
                  pl.BlockSpec((1, POOL_HALO, POOL_WIDTH),
                               lambda bi, i: (bi, jnp.maximum(i * halo_blocks - 1, 0), 0)),
                  pl.BlockSpec((1, tm, ATT_OUT_WIDTH), tile),
                  pl.BlockSpec((1, tm, 2 * D_MODEL), tile),
                  pl.BlockSpec((1, tm, d), tile),
                  pl.BlockSpec((1, 1, d), per_b),
                  pl.BlockSpec((1, 1, d), per_b),
                  pl.BlockSpec((1, 1, d), per_b),
                  pl.BlockSpec((1, d), const2),
                  pl.BlockSpec((1, d), const2),
                  pl.BlockSpec(wpg.shape, const3, **single),
                  pl.BlockSpec((1, POOL_WIDTH), const2),
                  pl.BlockSpec(wbp.shape, const2, **single),
                  pl.BlockSpec(wba.shape, const2, **single),
                  pl.BlockSpec(wout.shape, const2, **single),
                  pl.BlockSpec(wr.shape, const2, **single),
                  pl.BlockSpec((1, LANES), const2)],
        out_specs=[pl.BlockSpec((1, tm, d), tile),
                   pl.BlockSpec((tm * ROW_TILES, LANES), lambda bi, i: (bi * tiles + i, 0)),
                   pl.BlockSpec((tm, LANES), lambda bi, i: (bi * tiles + i, 0))],
        out_shape=[jax.ShapeDtypeStruct((b, s, d), F32),
                   jax.ShapeDtypeStruct((b * s * ROW_TILES, LANES), F32),
                   jax.ShapeDtypeStruct((b * s, LANES), F32)],
        scratch_shapes=[pltpu.VMEM((POOL_HALO + tm, POOL_WIDTH), F32)],
        compiler_params=pltpu.CompilerParams(
            dimension_semantics=("arbitrary", "arbitrary"), vmem_limit_bytes=VMEM_LIMIT),
        name="mixtail",
    )(u, u, att, gates, x, gt_m, sc_f, sh_f, g_post, g_pre,
      wpg, pscale, wbp, wba, wout, wr, br)


def _expert_kernel(bexp_ref, nused_ref, stok_ref, h_hbm, wg_ref, wu_ref, wd_ref, y_ref, xbuf, sem):
    del bexp_ref
    blk = pl.program_id(0)
    bm = MOE_BLOCK

    def row_copy(tok, j):
        return pltpu.make_async_copy(h_hbm.at[tok], xbuf.at[pl.ds(j * ROW_TILES, ROW_TILES), :], sem)

    @pl.when(blk < nused_ref[0])
    def _():
        def issue(j, carry):
            row_copy(stok_ref[blk * bm + j], j).start()
            return carry

        lax.fori_loop(0, bm, issue, 0)

        def drain(j, carry):
            row_copy(0, j).wait()
            return carry

        lax.fori_loop(0, bm, drain, 0)

        x = _tiles_to_rows(xbuf, bm).astype(BF16)
        a = _dot(x, wg_ref[...].astype(BF16))
        u = _dot(x, wu_ref[...].astype(BF16))
        mid = (a * _sigmoid(a)) * u
        y = _dot(mid.astype(BF16), wd_ref[...].astype(BF16))
        _rows_to_tiles(y_ref, y, bm)

    @pl.when(blk >= nused_ref[0])
    def _():
        y_ref[...] = jnp.zeros_like(y_ref)


def _experts(h2_tiles, block_expert, n_used, slot_tok, w_gate, w_up, w_down):
    bm = MOE_BLOCK
    n_blocks = block_expert.shape[0]
    wspec_in = pl.BlockSpec((None, D_MODEL, D_EXPERT), lambda blk, be, nu, st: (be[blk], 0, 0))
    wspec_out = pl.BlockSpec((None, D_EXPERT, D_MODEL), lambda blk, be, nu, st: (be[blk], 0, 0))
    grid_spec = pltpu.PrefetchScalarGridSpec(
        num_scalar_prefetch=3,
        grid=(n_blocks,),
        in_specs=[pl.BlockSpec(memory_space=pl.ANY), wspec_in, wspec_in, wspec_out],
        out_specs=pl.BlockSpec((bm * ROW_TILES, LANES), lambda blk, be, nu, st: (blk, 0)),
        scratch_shapes=[pltpu.VMEM((bm * ROW_TILES, LANES), F32), pltpu.SemaphoreType.DMA(())],
    )
    return pl.pallas_call(
        _expert_kernel,
        grid_spec=grid_spec,
        out_shape=jax.ShapeDtypeStruct((n_blocks * bm * ROW_TILES, LANES), F32),
        compiler_params=pltpu.CompilerParams(
            dimension_semantics=("arbitrary",), vmem_limit_bytes=VMEM_LIMIT),
        name="experts",
    )(block_expert, n_used, slot_tok, h2_tiles, w_gate, w_up, w_down)


def _combine_kernel(dest_ref, y_hbm, route_ref, xmid_ref, gt_ref, g_ref, o_ref, ybuf0, ybuf1, sem):
    tile = pl.program_id(0)
    tm = COMBINE_TILE

    def row_copy(slot, buf, j):
        return pltpu.make_async_copy(y_hbm.at[slot], buf.at[pl.ds(j * ROW_TILES, ROW_TILES), :], sem)

    def issue(j, carry):
        a = 2 * (tile * tm + j)
        row_copy(dest_ref[a], ybuf0, j).start()
        row_copy(dest_ref[a + 1], ybuf1, j).start()
        return carry

    lax.fori_loop(0, tm, issue, 0)

    def drain(j, carry):
        row_copy(0, ybuf0, j).wait()
        row_copy(0, ybuf1, j).wait()
        return carry

    lax.fori_loop(0, tm, drain, 0)

    route = route_ref[...]
    y = route[:, 2:3] * _tiles_to_rows(ybuf0, tm) + route[:, 3:4] * _tiles_to_rows(ybuf1, tm)
    o_ref[...] = xmid_ref[...] + gt_ref[0] * _rmsnorm(y, g_ref[...])


def _combine(dest, y_tiles, route, x_mid, gt_f, g_post, seq):
    t, d = x_mid.shape
    tm = COMBINE_TILE
    tiles_per_seq = seq // tm
    grid_spec = pltpu.PrefetchScalarGridSpec(
        num_scalar_prefetch=1,
        grid=(t // tm,),
        in_specs=[pl.BlockSpec(memory_space=pl.ANY),
                  pl.BlockSpec((tm, LANES), lambda i, dst: (i, 0)),
                  pl.BlockSpec((tm, d), lambda i, dst: (i, 0)),
                  pl.BlockSpec((1, 1, d), lambda i, dst: (i // tiles_per_seq, 0, 0)),
                  pl.BlockSpec((1, d), lambda i, dst: (0, 0))],
        out_specs=pl.BlockSpec((tm, d), lambda i, dst: (i, 0)),
        scratch_shapes=[pltpu.VMEM((tm * ROW_TILES, LANES), F32),
                        pltpu.VMEM((tm * ROW_TILES, LANES), F32),
                        pltpu.SemaphoreType.DMA(())],
    )
    return pl.pallas_call(
        _combine_kernel,
        grid_spec=grid_spec,
        out_shape=jax.ShapeDtypeStruct((t, d), F32),
        compiler_params=pltpu.CompilerParams(
            dimension_semantics=("arbitrary",), vmem_limit_bytes=VMEM_LIMIT),
        name="combine",
    )(dest, y_tiles, route, x_mid, gt_f, g_post)


def _slot_tables(route, n_tokens):
    bm = MOE_BLOCK
    n_assign = 2 * n_tokens
    n_blocks = -(-n_assign // bm) + N_EXPERTS
    expert = route[:, 0:2].astype(jnp.int32).reshape(n_assign)
    onehot = (expert[:, None] == jnp.arange(N_EXPERTS, dtype=jnp.int32)[None, :]).astype(jnp.int32)
    running = jnp.cumsum(onehot, axis=0)
    counts = running[-1]
    rank = jnp.sum(running * onehot, axis=1) - 1
    pad_counts = ((counts + bm - 1) // bm) * bm
    pad_end = jnp.cumsum(pad_counts)
    pad_start = pad_end - pad_counts
    dest = (pad_start[expert] + rank).astype(jnp.int32)
    token = jnp.arange(n_assign, dtype=jnp.int32) // 2
    slot_tok = jnp.zeros((n_blocks * bm,), jnp.int32).at[dest].set(token)
    n_used = (pad_end[-1] // bm).astype(jnp.int32)
    block_start = jnp.minimum(jnp.arange(n_blocks, dtype=jnp.int32), n_used - 1) * bm
    block_expert = jnp.minimum(jnp.searchsorted(pad_end, block_start, side='right'),
                               N_EXPERTS - 1).astype(jnp.int32)
    return dest, slot_tok, block_expert, n_used.reshape(1)


def kernel(x, c, w_ada, b_ada, g_pre_mix, g_post_mix, g_pre_ffn, g_post_ffn, w_in, w_pool_group, pool_scale, w_branch_pool, w_branch_att, w_out, w_group_router, b_group_router, w_expert_router, b_expert_router, w_exp_gate, w_exp_up, w_exp_down):
    b, s, d = x.shape
    assert d == D_MODEL and s % (ATT_BLOCK * 2 * ATT_DILATIONS[2]) == 0 and s % ROW_TILE == 0
    depth = w_ada.shape[0]
    slopes = jnp.exp2(-ALIBI_MAX_BIAS * jnp.arange(1, N_ATT_HEADS + 1, dtype=F32) / N_ATT_HEADS)
    q_scale = HEAD_DIM ** -0.5

    for layer in range(depth):
        mod = _adaln(c, w_ada[layer], b_ada[layer]).reshape(b, 6, 1, d)
        sh_m, sc_m, gt_m, sh_f, sc_f, gt_f = [mod[:, j] for j in range(6)]

        wl = w_in[layer]
        q_lo, k_lo, v_lo = POOL_WIDTH, POOL_WIDTH + 768, POOL_WIDTH + 2 * 768
        group_cols = []
        for g in range(3):
            sl = slice(g * 256, (g + 1) * 256)
            group_cols += [wl[:, q_lo:k_lo][:, sl] * q_scale, wl[:, k_lo:v_lo][:, sl],
                           wl[:, v_lo:v_lo + 768][:, sl]]
        w_perm = jnp.concatenate([wl[:, :POOL_WIDTH]] + group_cols + [wl[:, v_lo + 768:]],
                                 axis=1).astype(BF16)

        u, qkv0, qkv1, qkv2, gates = _inproj(x, g_pre_mix[layer].reshape(1, d), sc_m, sh_m, w_perm)
        att = _attention(qkv0, qkv1, qkv2, slopes)

        wr = jnp.zeros((d, LANES), F32)
        wr = wr.at[:, :N_EXPERTS].set(w_expert_router[layer])
        wr = wr.at[:, N_EXPERTS:N_EXPERTS + N_EXPERT_GROUPS].set(w_group_router[layer])
        br = jnp.zeros((1, LANES), F32)
        br = br.at[0, :N_EXPERTS].set(b_expert_router[layer])
        br = br.at[0, N_EXPERTS:N_EXPERTS + N_EXPERT_GROUPS].set(b_group_router[layer])

        x_mid, h2_tiles, route = _mixtail(
            u, att, gates, x, gt_m, sc_f, sh_f,
            g_post_mix[layer].reshape(1, d), g_pre_ffn[layer].reshape(1, d),
            w_pool_group[layer].astype(BF16), pool_scale[layer].reshape(1, POOL_WIDTH),
            w_branch_pool[layer].astype(BF16), w_branch_att[layer].astype(BF16),
            w_out[layer].astype(BF16), wr, br)

        dest, slot_tok, block_expert, n_used = _slot_tables(route, b * s)
        y_tiles = _experts(h2_tiles.reshape(b * s, ROW_TILES, LANES), block_expert, n_used, slot_tok,
                           w_exp_gate[layer], w_exp_up[layer], w_exp_down[layer])
        n_slots = block_expert.shape[0] * MOE_BLOCK
        x = _combine(dest, y_tiles.reshape(n_slots, ROW_TILES, LANES), route,
                     x_mid.reshape(b * s, d), gt_f, g_post_ffn[layer].reshape(1, d), s).reshape(b, s, d)
    return x
```

```python
import functools

import jax
import jax.numpy as jnp
from jax import lax
from jax.experimental import pallas as pl
from jax.experimental.pallas import tpu as pltpu

F32 = jnp.float32
BF16 = jnp.bfloat16
I32 = jnp.int32
HIGHEST = lax.Precision.HIGHEST

D_MODEL = 1024
LANES = 128
SUBLANES = 8
ROW_TILES = D_MODEL // LANES

POOL_WINDOWS = (2, 4, 8, 16)
POOL_GROUP_DIM = 128
POOL_WIDTH = 512
POOL_HALO = 16

HEAD_DIM = 64
ATT_DILATIONS = (1, 4, 16)
ATT_REACH = 128
ATT_BLOCK = 128
HEADS_PER_GROUP = 4
N_ATT_HEADS = 12
GROUP_QKV = 3 * HEADS_PER_GROUP * HEAD_DIM
ATT_OUT_WIDTH = 256
ALIBI_MAX_BIAS = 8.0
IN_WIDTH = POOL_WIDTH + 3 * GROUP_QKV + 2 * D_MODEL
MASKED = -1e30

N_EXPERT_GROUPS = 4
EXPERTS_PER_GROUP = 8
N_EXPERTS = 32
ROUTER_ROWS = 40
D_EXPERT = 512
RMS_EPS = 1e-6

ROW_TILE = 512
SORT_ROWS = 2 * ROW_TILE
RUN_BITS = ROW_TILE.bit_length()
MOE_BLOCK = 256
VMEM_LIMIT = 52 * 1024 * 1024


def _sigmoid(x):
    return 1.0 / (1.0 + jnp.exp(-x))


def _rmsnorm(x, g):
    return x * lax.rsqrt(jnp.mean(x * x, axis=-1, keepdims=True) + RMS_EPS) * g


def _dot(a, b):
    return jnp.dot(a, b, preferred_element_type=F32)


def _dot_nt(a, b, **kw):
    return lax.dot_general(a, b, (((1,), (1,)), ((), ())), preferred_element_type=F32, **kw)


def _tiles_to_rows(ref, n):
    return jnp.concatenate(
        [ref[pl.ds(s, n, stride=ROW_TILES), :] for s in range(ROW_TILES)], axis=1)


def _region_rows(n_tokens):
    return n_tokens + MOE_BLOCK


def _run_copies(n, src, src_row, dst, dst_row, sem, action):
    for bit in range(RUN_BITS):
        size = 1 << bit
        above = n & ~(2 * size - 1)

        @pl.when((n & size) != 0)
        def _():
            action(pltpu.make_async_copy(
                src.at[pl.ds((src_row + above) * ROW_TILES, size * ROW_TILES), :],
                dst.at[pl.ds((dst_row + above) * ROW_TILES, size * ROW_TILES), :], sem))


def _adaln_kernel(c_ref, w_ref, b_ref, o_ref):
    c = c_ref[...]
    a = c * _sigmoid(c)
    o_ref[...] = jnp.dot(a, w_ref[...], preferred_element_type=F32, precision=HIGHEST) + b_ref[...]


def _adaln(c, w_ada, b_ada):
    b, d = c.shape
    n = w_ada.shape[1]
    rows = -(-b // SUBLANES) * SUBLANES
    cp = jnp.pad(c, ((0, rows - b), (0, 0)))
    nt = 1536
    out = pl.pallas_call(
        _adaln_kernel,
        grid=(n // nt,),
        in_specs=[pl.BlockSpec((rows, d), lambda j: (0, 0)),
                  pl.BlockSpec((d, nt), lambda j: (0, j)),
                  pl.BlockSpec((1, nt), lambda j: (0, j))],
        out_specs=pl.BlockSpec((rows, nt), lambda j: (0, j)),
        out_shape=jax.ShapeDtypeStruct((rows, n), F32),
        compiler_params=pltpu.CompilerParams(vmem_limit_bytes=VMEM_LIMIT),
        name="adaln",
    )(cp, w_ada, b_ada.reshape(1, n))
    return out[:b]


def _inproj_kernel(x_ref, g_ref, sc_ref, sh_ref, w_ref,
                   u_ref, qkv0_ref, qkv1_ref, qkv2_ref, gate_ref, h_scr, p_scr):
    tm = x_ref.shape[1]
    h = _rmsnorm(x_ref[0], g_ref[...]) * (1.0 + sc_ref[0]) + sh_ref[0]
    h_scr[...] = h.astype(BF16)

    u_ref[0] = _dot(h_scr[...], w_ref[:, 0:POOL_WIDTH]).astype(BF16)

    col = POOL_WIDTH
    qkv0_ref[0, 0] = _dot(h_scr[...], w_ref[:, col:col + GROUP_QKV]).astype(BF16)
    for out_ref, d in ((qkv1_ref, ATT_DILATIONS[1]), (qkv2_ref, ATT_DILATIONS[2])):
        col += GROUP_QKV
        proj = _dot(h_scr[...], w_ref[:, col:col + GROUP_QKV])
        for cb in range(GROUP_QKV // LANES):
            p_scr[cb] = proj[:, cb * LANES:(cb + 1) * LANES]
        for r in range(d):
            out_ref[0, r] = jnp.concatenate(
                [p_scr[cb, pl.ds(r, tm // d, stride=d), :] for cb in range(GROUP_QKV // LANES)],
                axis=1).astype(BF16)
    col += GROUP_QKV

    chunk = 512
    for j in range(2 * D_MODEL // chunk):
        g = _dot(h_scr[...], w_ref[:, col + j * chunk:col + (j + 1) * chunk])
        gate_ref[0, :, j * chunk:(j + 1) * chunk] = _sigmoid(g).astype(BF16)


def _inproj(x, g_pre, sc, sh, w_perm):
    b, s, d = x.shape
    tm = ROW_TILE
    d1, d2 = ATT_DILATIONS[1], ATT_DILATIONS[2]
    grid = (b, s // tm)
    const = lambda bi, i: (0, 0)
    per_b = lambda bi, i: (bi, 0, 0)
    return pl.pallas_call(
        _inproj_kernel,
        grid=grid,
        in_specs=[pl.BlockSpec((1, tm, d), lambda bi, i: (bi, i, 0)),
                  pl.BlockSpec((1, d), const),
                  pl.BlockSpec((1, 1, d), per_b),
                  pl.BlockSpec((1, 1, d), per_b),
                  pl.BlockSpec((d, IN_WIDTH), const, pipeline_mode=pl.Buffered(1))],
        out_specs=[pl.BlockSpec((1, tm, POOL_WIDTH), lambda bi, i: (bi, i, 0)),
                   pl.BlockSpec((1, 1, tm, GROUP_QKV), lambda bi, i: (bi, 0, i, 0)),
                   pl.BlockSpec((1, d1, tm // d1, GROUP_QKV), lambda bi, i: (bi, 0, i, 0)),
                   pl.BlockSpec((1, d2, tm // d2, GROUP_QKV), lambda bi, i: (bi, 0, i, 0)),
                   pl.BlockSpec((1, tm, 2 * D_MODEL), lambda bi, i: (bi, i, 0))],
        out_shape=[jax.ShapeDtypeStruct((b, s, POOL_WIDTH), BF16),
                   jax.ShapeDtypeStruct((b, 1, s, GROUP_QKV), BF16),
                   jax.ShapeDtypeStruct((b, d1, s // d1, GROUP_QKV), BF16),
                   jax.ShapeDtypeStruct((b, d2, s // d2, GROUP_QKV), BF16),
                   jax.ShapeDtypeStruct((b, s, 2 * D_MODEL), BF16)],
        scratch_shapes=[pltpu.VMEM((tm, d), BF16), pltpu.VMEM((GROUP_QKV // LANES, tm, LANES), F32)],
        compiler_params=pltpu.CompilerParams(
            dimension_semantics=("arbitrary", "arbitrary"), vmem_limit_bytes=VMEM_LIMIT),
        name="inproj",
    )(x, g_pre, sc, sh, w_perm)


def _attn_kernel(slopes_ref, q0, k0, v0, q1, k1, v1, q2, k2, v2, o_ref,
                 bias_scr, acc, mrun, lrun):
    pair = pl.program_id(1)
    seq = o_ref.shape[1]
    nblk = seq // ATT_BLOCK
    lane = lax.broadcasted_iota(I32, (ATT_BLOCK, LANES), 1)
    first_head = lane < HEAD_DIM

    qi = lax.broadcasted_iota(I32, (ATT_BLOCK, 2 * ATT_BLOCK), 0)
    kj = lax.broadcasted_iota(I32, (ATT_BLOCK, 2 * ATT_BLOCK), 1)
    delta = ATT_BLOCK + qi - kj
    valid = (delta >= 0) & (delta <= ATT_REACH)
    delta0 = qi - kj
    valid0 = delta0 >= 0
    for g, d in enumerate(ATT_DILATIONS):
        for j in range(2):
            slope = slopes_ref[g * HEADS_PER_GROUP + 2 * pair + j]
            bias_scr[g, j, 0] = jnp.where(valid0, -slope * (delta0 * d).astype(F32), MASKED)
            bias_scr[g, j, 1] = jnp.where(valid, -slope * (delta * d).astype(F32), MASKED)

    def block_softmax(g, q_ref, k_ref, v_ref, r, i):
        lo = jnp.maximum(i - 1, 0)
        variant = jnp.minimum(i, 1)
        q = q_ref[0, r, pl.ds(pl.multiple_of(i * ATT_BLOCK, ATT_BLOCK), ATT_BLOCK), :]
        kw = k_ref[0, r, pl.ds(pl.multiple_of(lo * ATT_BLOCK, ATT_BLOCK), 2 * ATT_BLOCK), :]
        vw = v_ref[0, r, pl.ds(pl.multiple_of(lo * ATT_BLOCK, ATT_BLOCK), 2 * ATT_BLOCK), :]
        qf = q.astype(F32)
        outs = []
        for j in range(2):
            head = first_head if j == 0 else jnp.logical_not(first_head)
            qh = jnp.where(head, qf, 0.0).astype(BF16)
            sc = _dot_nt(qh, kw) + bias_scr[g, j, variant]
            mx = jnp.max(sc, axis=-1, keepdims=True)
            p = jnp.exp(sc - mx)
            den = jnp.sum(p, axis=-1, keepdims=True)
            num = _dot(p.astype(BF16), vw)
            outs.append((num, mx, den))
        (n0, m0, l0), (n1, m1, l1) = outs
        return (jnp.where(first_head, n0, n1),
                jnp.where(first_head, m0, m1),
                jnp.where(first_head, l0, l1))

    def first_group(n, carry):
        num, mx, den = block_softmax(0, q0, k0, v0, 0, n)
        rows = pl.ds(pl.multiple_of(n * ATT_BLOCK, ATT_BLOCK), ATT_BLOCK)
        acc[rows, :] = num
        mrun[rows, :] = mx
        lrun[rows, :] = den
        return carry

    lax.fori_loop(0, nblk, first_group, 0)

    def merge_group(g, q_ref, k_ref, v_ref):
        d = ATT_DILATIONS[g]
        per_res = nblk // d

        def body(n, carry):
            r = n // per_res
            i = n % per_res
            num, mx, den = block_softmax(g, q_ref, k_ref, v_ref, r, i)
            rows = pl.ds(i * (ATT_BLOCK * d) + r, ATT_BLOCK, stride=d)
            m_old = mrun[rows, :]
            m_new = jnp.maximum(m_old, mx)
            a = jnp.exp(m_old - m_new)
            c = jnp.exp(mx - m_new)
            acc[rows, :] = a * acc[rows, :] + c * num
            lrun[rows, :] = a * lrun[rows, :] + c * den
            mrun[rows, :] = m_new
            return carry

        lax.fori_loop(0, nblk, body, 0)

    merge_group(1, q1, k1, v1)
    merge_group(2, q2, k2, v2)

    def finish(n, carry):
        rows = pl.ds(pl.multiple_of(n * ATT_BLOCK, ATT_BLOCK), ATT_BLOCK)
        o_ref[0, rows, :] = (acc[rows, :] / lrun[rows, :]).astype(BF16)
        return carry

    lax.fori_loop(0, nblk, finish, 0)


def _attention(qkv0, qkv1, qkv2, slopes):
    b, _, s, _ = qkv0.shape
    pairs = HEADS_PER_GROUP // 2
    col_blocks = HEADS_PER_GROUP * HEAD_DIM // LANES

    def specs(arr):
        _, d, sub, _ = arr.shape
        return [pl.BlockSpec((1, d, sub, LANES),
                             functools.partial(lambda bi, p, sec: (bi, 0, 0, sec * col_blocks + p), sec=sec))
                for sec in range(3)]

    return pl.pallas_call(
        _attn_kernel,
        grid=(b, pairs),
        in_specs=[pl.BlockSpec(memory_space=pltpu.SMEM)] + specs(qkv0) + specs(qkv1) + specs(qkv2),
        out_specs=pl.BlockSpec((1, s, LANES), lambda bi, p: (bi, 0, p)),
        out_shape=jax.ShapeDtypeStruct((b, s, ATT_OUT_WIDTH), BF16),
        scratch_shapes=[pltpu.VMEM((3, 2, 2, ATT_BLOCK, 2 * ATT_BLOCK), F32),
                        pltpu.VMEM((s, LANES), F32),
                        pltpu.VMEM((s, LANES), F32),
                        pltpu.VMEM((s, LANES), F32)],
        compiler_params=pltpu.CompilerParams(
            dimension_semantics=("arbitrary", "arbitrary"), vmem_limit_bytes=VMEM_LIMIT),
        name="attention",
    )(slopes, qkv0, qkv0, qkv0, qkv1, qkv1, qkv1, qkv2, qkv2, qkv2)


def _mixtail_kernel(u_ref, halo_ref, att_ref, gate_ref, x_ref,
                    gt_m_ref, sc_f_ref, sh_f_ref, g_post_ref, g_pre_ref,
                    wpg_ref, pscale_ref, wbp_ref, wba_ref, wout_ref, wrt_ref, brt_ref,
                    xmid_ref, route_ref, meta_ref, counts_ref, xs_hbm,
                    pu, xbuf, zbuf, fill, meta_s, sem, sem_s):
    i = pl.program_id(1)
    tm = x_ref.shape[1]
    step = pl.program_id(0) * pl.num_programs(1) + i
    last = pl.num_programs(0) * pl.num_programs(1) - 1
    slot = step % 2
    region = _region_rows(pl.num_programs(0) * pl.num_programs(1) * tm)

    @pl.when(step == 0)
    def _():
        fill[...] = jnp.zeros_like(fill)

    halo = halo_ref[0].astype(F32)
    pu[0:POOL_HALO, :] = jnp.where(i > 0, halo, jnp.zeros_like(halo))
    pu[POOL_HALO:POOL_HALO + tm, :] = u_ref[0].astype(F32)
    t = i * tm + lax.broadcasted_iota(I32, (tm, 1), 0)
    y_pool = jnp.zeros((tm, D_MODEL), F32)
    for g, w in enumerate(POOL_WINDOWS):
        cols = slice(g * POOL_GROUP_DIM, (g + 1) * POOL_GROUP_DIM)
        win = pu[pl.ds(POOL_HALO, tm), cols]
        for j in range(1, w):
            win = win + pu[pl.ds(POOL_HALO - j, tm), cols]
        count = jnp.minimum(t + 1, w).astype(F32)
        pooled = win / count - pu[pl.ds(POOL_HALO, tm), cols]
        mixed = _dot(pooled.astype(BF16), wpg_ref[g]) * pscale_ref[:, cols]
        y_pool = y_pool + _dot(mixed.astype(BF16), wbp_ref[cols, :])

    y_att = _dot(att_ref[0], wba_ref[...])
    merged = (gate_ref[0, :, 0:D_MODEL].astype(F32) * y_pool
              + gate_ref[0, :, D_MODEL:2 * D_MODEL].astype(F32) * y_att)
    y = _dot(merged.astype(BF16), wout_ref[...])
    x_mid = x_ref[0] + gt_m_ref[0] * _rmsnorm(y, g_post_ref[...])
    xmid_ref[0] = x_mid

    h2 = _rmsnorm(x_mid, g_pre_ref[...]) * (1.0 + sc_f_ref[0]) + sh_f_ref[0]

    logits = _dot_nt(wrt_ref[...], h2, precision=HIGHEST) + brt_ref[...]
    neg_inf = -jnp.inf
    far = float(LANES)
    gl = logits[N_EXPERTS:N_EXPERTS + N_EXPERT_GROUPS, :]
    grow = lax.broadcasted_iota(I32, gl.shape, 0).astype(F32)
    gmax = jnp.max(gl, axis=0, keepdims=True)
    gsel = jnp.min(jnp.where(gl == gmax, grow, far), axis=0, keepdims=True)
    p_group = 1.0 / jnp.sum(jnp.exp(gl - gmax), axis=0, keepdims=True)
    erow = lax.broadcasted_iota(I32, (N_EXPERTS, tm), 0).astype(F32)
    e_lo = gsel * float(EXPERTS_PER_GROUP)
    el = jnp.where((erow >= e_lo) & (erow < e_lo + float(EXPERTS_PER_GROUP)), logits[0:N_EXPERTS, :], neg_inf)
    v1 = jnp.max(el, axis=0, keepdims=True)
    i1 = jnp.min(jnp.where(el == v1, erow, far), axis=0, keepdims=True)
    el2 = jnp.where(erow == i1, neg_inf, el)
    v2 = jnp.max(el2, axis=0, keepdims=True)
    i2 = jnp.min(jnp.where(el2 == v2, erow, far), axis=0, keepdims=True)
    e21 = jnp.exp(v2 - v1)
    w1 = p_group / (1.0 + e21)
    w2 = p_group * e21 / (1.0 + e21)

    pick1 = erow == i1
    pick2 = erow == i2
    assign = jnp.where(pick1 | pick2, 1.0, 0.0)
    before = (lax.broadcasted_iota(I32, (tm, tm), 0) < lax.broadcasted_iota(I32, (tm, tm), 1))
    rank = _dot(assign.astype(BF16), jnp.where(before, 1.0, 0.0).astype(BF16))
    assign_pad = jnp.concatenate([assign, jnp.zeros((LANES - N_EXPERTS, tm), F32)], axis=0).astype(BF16)
    cnt_row = _dot_nt(jnp.ones((SUBLANES, tm), BF16), assign_pad)
    e_col = lax.broadcasted_iota(I32, (N_EXPERTS, LANES), 0)
    e_lane = lax.broadcasted_iota(I32, (N_EXPERTS, LANES), 1)
    run_start = jnp.sum(jnp.where(e_lane < e_col, cnt_row[0:1, :], 0.0), axis=1, keepdims=True)
    pos = rank + run_start
    key1 = jnp.sum(jnp.where(pick1, pos, 0.0), axis=0, keepdims=True)
    key2 = jnp.sum(jnp.where(pick2, pos, 0.0), axis=0, keepdims=True)

    row8 = lax.broadcasted_iota(I32, (SUBLANES, tm), 0)
    route_ref[...] = jnp.where(row8 == 0, key1, jnp.where(row8 == 1, key2,
                               jnp.where(row8 == 2, w1, jnp.where(row8 == 3, w2, 0.0))))

    filled = fill[...]
    mrow = lax.broadcasted_iota(I32, (SUBLANES, LANES), 0)
    meta_ref[0] = jnp.where(mrow == 0, filled, jnp.where(mrow == 1, cnt_row, 0.0)).astype(I32)
    fill[...] = filled + cnt_row
    counts_ref[...] = (filled + cnt_row).astype(I32)

    srow = lax.broadcasted_iota(I32, (SORT_ROWS, tm), 0).astype(F32)
    perm = jnp.where((srow == key1) | (srow == key2), 1.0, 0.0).astype(BF16)
    h2b = h2.astype(BF16)
    xslot = xbuf.at[slot]
    for s in range(ROW_TILES):
        xslot[pl.ds(s, SORT_ROWS, stride=ROW_TILES), :] = _dot(perm, h2b[:, s * LANES:(s + 1) * LANES])

    def tile_runs(which, action):
        def per_expert(e, local):
            n = meta_s[which, 1, e]
            _run_copies(n, xbuf.at[which], local, xs_hbm, e * region + meta_s[which, 0, e], sem, action)
            return local + n
        lax.fori_loop(0, N_EXPERTS, per_expert, 0)

    @pl.when(step > 0)
    def _():
        tile_runs(1 - slot, lambda cp: cp.wait())

    meta_copy = pltpu.make_async_copy(meta_ref.at[0], meta_s.at[slot], sem_s)
    meta_copy.start()
    meta_copy.wait()
    tile_runs(slot, lambda cp: cp.start())

    @pl.when(step == last)
    def _():
        tile_runs(slot, lambda cp: cp.wait())
        zbuf[...] = jnp.zeros_like(zbuf)

        def pad_copy(e):
            end = e * region + meta_s[slot, 0, e] + meta_s[slot, 1, e]
            return pltpu.make_async_copy(
                zbuf, xs_hbm.at[pl.ds(end * ROW_TILES, MOE_BLOCK * ROW_TILES), :], sem)

        def start_pad(e, carry):
            pad_copy(e).start()
            return carry

        def wait_pad(e, carry):
            pad_copy(e).wait()
            return carry

        lax.fori_loop(0, N_EXPERTS, start_pad, 0)
        lax.fori_loop(0, N_EXPERTS, wait_pad, 0)


def _mixtail(u, att, gates, x, gt_m, sc_f, sh_f, g_post, g_pre,
             wpg, pscale, wbp, wba, wout, wrt, brt):
    b, s, d = x.shape
    tm = ROW_TILE
    tiles = s // tm
    n_tiles = b * tiles
    halo_blocks = tm // POOL_HALO
    region = _region_rows(b * s)
    const2 = lambda bi, i: (0, 0)
    const3 = lambda bi, i: (0, 0, 0)
    per_b = lambda bi, i: (bi, 0, 0)
    tile = lambda bi, i: (bi, i, 0)
    single = dict(pipeline_mode=pl.Buffered(1))
    return pl.pallas_call(
        _mixtail_kernel,
        grid=(b, tiles),
        in_specs=[pl.BlockSpec((1, tm, POOL_WIDTH), tile),
                  pl.BlockSpec((1, POOL_HALO, POOL_WIDTH),
                               lambda bi, i: (bi, jnp.maximum(i * halo_blocks - 1, 0), 0)),
                  pl.BlockSpec((1, tm, ATT_OUT_WIDTH), tile),
                  pl.BlockSpec((1, tm, 2 * D_MODEL), tile),
                  pl.BlockSpec((1, tm, d), tile),
                  pl.BlockSpec((1, 1, d), per_b),
                  pl.BlockSpec((1, 1, d), per_b),
                  pl.BlockSpec((1, 1, d), per_b),
                  pl.BlockSpec((1, d), const2),
                  pl.BlockSpec((1, d), const2),
                  pl.BlockSpec(wpg.shape, const3, **single),
                  pl.BlockSpec((1, POOL_WIDTH), const2),
                  pl.BlockSpec(wbp.shape, const2, **single),
                  pl.BlockSpec(wba.shape, const2, **single),
                  pl.BlockSpec(wout.shape, const2, **single),
                  pl.BlockSpec(wrt.shape, const2, **single),
                  pl.BlockSpec(brt.shape, const2)],
        out_specs=[pl.BlockSpec((1, tm, d), tile),
                   pl.BlockSpec((SUBLANES, tm), lambda bi, i: (0, bi * tiles + i)),
                   pl.BlockSpec((1, SUBLANES, LANES), lambda bi, i: (bi * tiles + i, 0, 0)),
                   pl.BlockSpec((SUBLANES, LANES), const2),
                   pl.BlockSpec(memory_space=pl.ANY)],
        out_shape=[jax.ShapeDtypeStruct((b, s, d), F32),
                   jax.ShapeDtypeStruct((SUBLANES, b * s), F32),
                   jax.ShapeDtypeStruct((n_tiles, SUBLANES, LANES), I32),
                   jax.ShapeDtypeStruct((SUBLANES, LANES), I32),
                   jax.ShapeDtypeStruct((N_EXPERTS * region * ROW_TILES, LANES), F32)],
        scratch_shapes=[pltpu.VMEM((POOL_HALO + tm, POOL_WIDTH), F32),
                        pltpu.VMEM((2, SORT_ROWS * ROW_TILES, LANES), F32),
                        pltpu.VMEM((MOE_BLOCK * ROW_TILES, LANES), F32),
                        pltpu.VMEM((SUBLANES, LANES), F32),
                        pltpu.SMEM((2, SUBLANES, LANES), I32),
                        pltpu.SemaphoreType.DMA(()),
                        pltpu.SemaphoreType.DMA(())],
        compiler_params=pltpu.CompilerParams(
            dimension_semantics=("arbitrary", "arbitrary"), vmem_limit_bytes=VMEM_LIMIT),
        name="mixtail",
    )(u, u, att, gates, x, gt_m, sc_f, sh_f, g_post, g_pre,
      wpg, pscale, wbp, wba, wout, wrt, brt)


def _expert_kernel(brow_ref, first_ref, nused_ref, x_ref, wg_ref, wu_ref, wd_ref, y_ref,
                   wg_bf, wu_bf, wd_bf):
    del brow_ref
    blk = pl.program_id(0)
    bm = MOE_BLOCK

    @pl.when(first_ref[blk] == 1)
    def _():
        wg_bf[...] = wg_ref[...].astype(BF16)
        wu_bf[...] = wu_ref[...].astype(BF16)
        wd_bf[...] = wd_ref[...].astype(BF16)

    @pl.when(blk < nused_ref[0])
    def _():
        x = _tiles_to_rows(x_ref, bm).astype(BF16)
        a = _dot(x, wg_bf[...])
        u = _dot(x, wu_bf[...])
        mid = ((a * _sigmoid(a)) * u).astype(BF16)
        for s in range(ROW_TILES):
            y_ref[pl.ds(s, bm, stride=ROW_TILES), :] = _dot(mid, wd_bf[:, s * LANES:(s + 1) * LANES])


def _experts(xs, block_row, block_expert, first, n_used, w_gate, w_up, w_down):
    bm = MOE_BLOCK
    n_blocks = block_row.shape[0]
    rows = pl.BlockSpec((bm * ROW_TILES, LANES), lambda blk, br, be, fi, nu: (br[blk], 0))
    w_in = pl.BlockSpec((None, D_MODEL, D_EXPERT), lambda blk, br, be, fi, nu: (be[blk], 0, 0))
    w_out = pl.BlockSpec((None, D_EXPERT, D_MODEL), lambda blk, br, be, fi, nu: (be[blk], 0, 0))

    def body(brow_ref, bexp_ref, first_ref, nused_ref, *refs):
        del bexp_ref
        _expert_kernel(brow_ref, first_ref, nused_ref, *refs)

    grid_spec = pltpu.PrefetchScalarGridSpec(
        num_scalar_prefetch=4,
        grid=(n_blocks,),
        in_specs=[rows, w_in, w_in, w_out],
        out_specs=rows,
        scratch_shapes=[pltpu.VMEM((D_MODEL, D_EXPERT), BF16),
                        pltpu.VMEM((D_MODEL, D_EXPERT), BF16),
                        pltpu.VMEM((D_EXPERT, D_MODEL), BF16)],
    )
    return pl.pallas_call(
        body,
        grid_spec=grid_spec,
        out_shape=jax.ShapeDtypeStruct(xs.shape, F32),
        compiler_params=pltpu.CompilerParams(
            dimension_semantics=("arbitrary",), vmem_limit_bytes=VMEM_LIMIT),
        name="experts",
    )(block_row, block_expert, first, n_used, xs, w_gate, w_up, w_down)


def _block_tables(counts, n_tokens):
    bm = MOE_BLOCK
    n_blocks = 2 * n_tokens // bm + N_EXPERTS
    blocks_per_region = _region_rows(n_tokens) // bm
    nb = (counts + bm - 1) // bm
    nb_end = jnp.cumsum(nb)
    n_used = nb_end[-1]
    blk = jnp.arange(n_blocks, dtype=I32)
    live = jnp.minimum(blk, n_used - 1)
    expert = jnp.minimum(jnp.searchsorted(nb_end, live, side='right'), N_EXPERTS - 1).astype(I32)
    k = live - (nb_end[expert] - nb[expert])
    block_row = (expert * blocks_per_region + k).astype(I32)
    first = ((k == 0) & (blk < n_used)).astype(I32)
    return block_row, expert, first, n_used.reshape(1).astype(I32)


def _combine_kernel(meta_ref, y_hbm, route_ref, xmid_ref, gt_ref, g_ref, o_ref, ybuf, rt_scr, sem):
    step = pl.program_id(0)
    n_steps = pl.num_programs(0)
    tm = ROW_TILE
    slot = step % 2
    region = _region_rows(n_steps * tm)

    def tile_runs(tile, which, action):
        def per_expert(e, local):
            filled = meta_ref[(tile * 2) * N_EXPERTS + e]
            n = meta_ref[(tile * 2 + 1) * N_EXPERTS + e]
            _run_copies(n, y_hbm, e * region + filled, ybuf.at[which], local, sem, action)
            return local + n
        lax.fori_loop(0, N_EXPERTS, per_expert, 0)

    @pl.when(step == 0)
    def _():
        tile_runs(0, 0, lambda cp: cp.start())

    tile_runs(step, slot, lambda cp: cp.wait())

    @pl.when(step + 1 < n_steps)
    def _():
        tile_runs(step + 1, 1 - slot, lambda cp: cp.start())

    rt_scr[...] = jnp.zeros_like(rt_scr)
    rt_scr[0:SUBLANES, :] = route_ref[...]
    cols = jnp.concatenate([rt_scr[:, c * LANES:(c + 1) * LANES].T for c in range(tm // LANES)], axis=0)
    key1, key2, w1, w2 = cols[:, 0:1], cols[:, 1:2], cols[:, 2:3], cols[:, 3:4]

    yb = _tiles_to_rows(ybuf.at[slot], SORT_ROWS).astype(BF16)
    spos = lax.broadcasted_iota(I32, (tm, SORT_ROWS), 1).astype(F32)
    y1 = _dot(jnp.where(spos == key1, 1.0, 0.0).astype(BF16), yb)
    y2 = _dot(jnp.where(spos == key2, 1.0, 0.0).astype(BF16), yb)
    y = w1 * y1 + w2 * y2
    o_ref[...] = xmid_ref[...] + gt_ref[0] * _rmsnorm(y, g_ref[...])


def _combine(meta, ys, route, x_mid, gt_f, g_post, seq):
    t, d = x_mid.shape
    tm = ROW_TILE
    tiles_per_seq = seq // tm
    grid_spec = pltpu.PrefetchScalarGridSpec(
        num_scalar_prefetch=1,
        grid=(t // tm,),
        in_specs=[pl.BlockSpec(memory_space=pl.ANY),
                  pl.BlockSpec((SUBLANES, tm), lambda i, m: (0, i)),
                  pl.BlockSpec((tm, d), lambda i, m: (i, 0)),
                  pl.BlockSpec((1, 1, d), lambda i, m: (i // tiles_per_seq, 0, 0)),
                  pl.BlockSpec((1, d), lambda i, m: (0, 0))],
        out_specs=pl.BlockSpec((tm, d), lambda i, m: (i, 0)),
        scratch_shapes=[pltpu.VMEM((2, SORT_ROWS * ROW_TILES, LANES), F32),
                        pltpu.VMEM((LANES, tm), F32),
                        pltpu.SemaphoreType.DMA(())],
    )
    return pl.pallas_call(
        _combine_kernel,
        grid_spec=grid_spec,
        out_shape=jax.ShapeDtypeStruct((t, d), F32),
        compiler_params=pltpu.CompilerParams(
            dimension_semantics=("arbitrary",), vmem_limit_bytes=VMEM_LIMIT),
        name="combine",
    )(meta, ys, route, x_mid, gt_f, g_post)


def kernel(x, c, w_ada, b_ada, g_pre_mix, g_post_mix, g_pre_ffn, g_post_ffn, w_in, w_pool_group, pool_scale, w_branch_pool, w_branch_att, w_out, w_group_router, b_group_router, w_expert_router, b_expert_router, w_exp_gate, w_exp_up, w_exp_down):
    b, s, d = x.shape
    assert d == D_MODEL and s % (ATT_BLOCK * 2 * ATT_DILATIONS[2]) == 0 and s % ROW_TILE == 0
    assert (2 * b * s) % MOE_BLOCK == 0
    depth = w_ada.shape[0]
    slopes = jnp.exp2(-ALIBI_MAX_BIAS * jnp.arange(1, N_ATT_HEADS + 1, dtype=F32) / N_ATT_HEADS)
    q_scale = HEAD_DIM ** -0.5

    for layer in range(depth):
        mod = _adaln(c, w_ada[layer], b_ada[layer]).reshape(b, 6, 1, d)
        sh_m, sc_m, gt_m, sh_f, sc_f, gt_f = [mod[:, j] for j in range(6)]

        wl = w_in[layer]
        q_lo, k_lo, v_lo = POOL_WIDTH, POOL_WIDTH + 768, POOL_WIDTH + 2 * 768
        group_cols = []
        for g in range(3):
            sl = slice(g * 256, (g + 1) * 256)
            group_cols += [wl[:, q_lo:k_lo][:, sl] * q_scale, wl[:, k_lo:v_lo][:, sl],
                           wl[:, v_lo:v_lo + 768][:, sl]]
        w_perm = jnp.concatenate([wl[:, :POOL_WIDTH]] + group_cols + [wl[:, v_lo + 768:]],
                                 axis=1).astype(BF16)

        u, qkv0, qkv1, qkv2, gates = _inproj(x, g_pre_mix[layer].reshape(1, d), sc_m, sh_m, w_perm)
        att = _attention(qkv0, qkv1, qkv2, slopes)

        pad_rows = ROUTER_ROWS - N_EXPERTS - N_EXPERT_GROUPS
        wrt = jnp.concatenate([w_expert_router[layer].T, w_group_router[layer].T,
                               jnp.zeros((pad_rows, d), F32)], axis=0)
        brt = jnp.concatenate([b_expert_router[layer], b_group_router[layer],
                               jnp.zeros((pad_rows,), F32)]).reshape(ROUTER_ROWS, 1)

        x_mid, route, meta, counts, xs = _mixtail(
            u, att, gates, x, gt_m, sc_f, sh_f,
            g_post_mix[layer].reshape(1, d), g_pre_ffn[layer].reshape(1, d),
            w_pool_group[layer].astype(BF16), pool_scale[layer].reshape(1, POOL_WIDTH),
            w_branch_pool[layer].astype(BF16), w_branch_att[layer].astype(BF16),
            w_out[layer].astype(BF16), wrt, brt)

        block_row, block_expert, first, n_used = _block_tables(counts[0, :N_EXPERTS], b * s)
        ys = _experts(xs, block_row, block_expert, first, n_used,
                      w_exp_gate[layer], w_exp_up[layer], w_exp_down[layer])
        run_meta = meta[:, 0:2, 0:N_EXPERTS].reshape(-1)
        x = _combine(run_meta, ys, route, x_mid.reshape(b * s, d), gt_f,
                     g_post_ffn[layer].reshape(1, d), s).reshape(b, s, d)
    return x
```

```python
import functools

import jax
import jax.numpy as jnp
from jax import lax
from jax.experimental import pallas as pl
from jax.experimental.pallas import tpu as pltpu

F32 = jnp.float32
BF16 = jnp.bfloat16
I32 = jnp.int32
HIGHEST = lax.Precision.HIGHEST

D_MODEL = 1024
LANES = 128
SUBLANES = 8
ROW_TILES = D_MODEL // LANES

POOL_WINDOWS = (2, 4, 8, 16)
POOL_GROUP_DIM = 128
POOL_WIDTH = 512
POOL_HALO = 16

HEAD_DIM = 64
ATT_DILATIONS = (1, 4, 16)
ATT_REACH = 128
ATT_BLOCK = 128
ATT_UNROLL = 4
HEADS_PER_GROUP = 4
N_ATT_HEADS = 12
GROUP_QKV = 3 * HEADS_PER_GROUP * HEAD_DIM
ATT_OUT_WIDTH = 256
ALIBI_MAX_BIAS = 8.0
IN_WIDTH = POOL_WIDTH + 3 * GROUP_QKV + 2 * D_MODEL
MASKED = -1e30

N_EXPERT_GROUPS = 4
EXPERTS_PER_GROUP = 8
N_EXPERTS = 32
ROUTER_ROWS = 40
D_EXPERT = 512
RMS_EPS = 1e-6

ROW_TILE = 512
SORT_ROWS = 2 * ROW_TILE
RUN_BITS = ROW_TILE.bit_length()
MOE_BLOCK = 256
VMEM_LIMIT = 52 * 1024 * 1024


def _sigmoid(x):
    return 1.0 / (1.0 + jnp.exp(-x))


def _rmsnorm(x, g):
    return x * lax.rsqrt(jnp.mean(x * x, axis=-1, keepdims=True) + RMS_EPS) * g


def _dot(a, b):
    return jnp.dot(a, b, preferred_element_type=F32)


def _dot_nt(a, b, **kw):
    return lax.dot_general(a, b, (((1,), (1,)), ((), ())), preferred_element_type=F32, **kw)


def _tiles_to_rows(ref, n):
    return jnp.concatenate(
        [ref[pl.ds(s, n, stride=ROW_TILES), :] for s in range(ROW_TILES)], axis=1)


def _region_rows(n_tokens):
    return n_tokens + MOE_BLOCK


def _run_copies(n, src, src_row, dst, dst_row, sem, action):
    for bit in range(RUN_BITS):
        size = 1 << bit
        above = n & ~(2 * size - 1)

        @pl.when((n & size) != 0)
        def _():
            action(pltpu.make_async_copy(
                src.at[pl.ds((src_row + above) * ROW_TILES, size * ROW_TILES), :],
                dst.at[pl.ds((dst_row + above) * ROW_TILES, size * ROW_TILES), :], sem))


def _adaln_kernel(c_ref, w_ref, b_ref, o_ref):
    c = c_ref[...]
    a = c * _sigmoid(c)
    o_ref[...] = jnp.dot(a, w_ref[...], preferred_element_type=F32, precision=HIGHEST) + b_ref[...]


def _adaln(c, w_ada, b_ada):
    b, d = c.shape
    n = w_ada.shape[1]
    rows = -(-b // SUBLANES) * SUBLANES
    cp = jnp.pad(c, ((0, rows - b), (0, 0)))
    nt = 1536
    out = pl.pallas_call(
        _adaln_kernel,
        grid=(n // nt,),
        in_specs=[pl.BlockSpec((rows, d), lambda j: (0, 0)),
                  pl.BlockSpec((d, nt), lambda j: (0, j)),
                  pl.BlockSpec((1, nt), lambda j: (0, j))],
        out_specs=pl.BlockSpec((rows, nt), lambda j: (0, j)),
        out_shape=jax.ShapeDtypeStruct((rows, n), F32),
        compiler_params=pltpu.CompilerParams(vmem_limit_bytes=VMEM_LIMIT),
        name="adaln",
    )(cp, w_ada, b_ada.reshape(1, n))
    return out[:b]


def _inproj_kernel(x_ref, g_ref, sc_ref, sh_ref, w_ref,
                   u_ref, qkv0_ref, qkv1_ref, qkv2_ref, gate_ref, h_scr, p_scr):
    tm = x_ref.shape[1]
    h = _rmsnorm(x_ref[0], g_ref[...]) * (1.0 + sc_ref[0]) + sh_ref[0]
    h_scr[...] = h.astype(BF16)

    u_ref[0] = _dot(h_scr[...], w_ref[:, 0:POOL_WIDTH]).astype(BF16)

    col = POOL_WIDTH
    qkv0_ref[0, 0] = _dot(h_scr[...], w_ref[:, col:col + GROUP_QKV]).astype(BF16)
    for out_ref, d in ((qkv1_ref, ATT_DILATIONS[1]), (qkv2_ref, ATT_DILATIONS[2])):
        col += GROUP_QKV
        proj = _dot(h_scr[...], w_ref[:, col:col + GROUP_QKV])
        for cb in range(GROUP_QKV // LANES):
            p_scr[cb] = proj[:, cb * LANES:(cb + 1) * LANES]
        for r in range(d):
            out_ref[0, r] = jnp.concatenate(
                [p_scr[cb, pl.ds(r, tm // d, stride=d), :] for cb in range(GROUP_QKV // LANES)],
                axis=1).astype(BF16)
    col += GROUP_QKV

    chunk = 512
    for j in range(2 * D_MODEL // chunk):
        g = _dot(h_scr[...], w_ref[:, col + j * chunk:col + (j + 1) * chunk])
        gate_ref[0, :, j * chunk:(j + 1) * chunk] = _sigmoid(g).astype(BF16)


def _inproj(x, g_pre, sc, sh, w_perm):
    b, s, d = x.shape
    tm = ROW_TILE
    d1, d2 = ATT_DILATIONS[1], ATT_DILATIONS[2]
    grid = (b, s // tm)
    const = lambda bi, i: (0, 0)
    per_b = lambda bi, i: (bi, 0, 0)
    return pl.pallas_call(
        _inproj_kernel,
        grid=grid,
        in_specs=[pl.BlockSpec((1, tm, d), lambda bi, i: (bi, i, 0)),
                  pl.BlockSpec((1, d), const),
                  pl.BlockSpec((1, 1, d), per_b),
                  pl.BlockSpec((1, 1, d), per_b),
                  pl.BlockSpec((d, IN_WIDTH), const, pipeline_mode=pl.Buffered(1))],
        out_specs=[pl.BlockSpec((1, tm, POOL_WIDTH), lambda bi, i: (bi, i, 0)),
                   pl.BlockSpec((1, 1, tm, GROUP_QKV), lambda bi, i: (bi, 0, i, 0)),
                   pl.BlockSpec((1, d1, tm // d1, GROUP_QKV), lambda bi, i: (bi, 0, i, 0)),
                   pl.BlockSpec((1, d2, tm // d2, GROUP_QKV), lambda bi, i: (bi, 0, i, 0)),
                   pl.BlockSpec((1, tm, 2 * D_MODEL), lambda bi, i: (bi, i, 0))],
        out_shape=[jax.ShapeDtypeStruct((b, s, POOL_WIDTH), BF16),
                   jax.ShapeDtypeStruct((b, 1, s, GROUP_QKV), BF16),
                   jax.ShapeDtypeStruct((b, d1, s // d1, GROUP_QKV), BF16),
                   jax.ShapeDtypeStruct((b, d2, s // d2, GROUP_QKV), BF16),
                   jax.ShapeDtypeStruct((b, s, 2 * D_MODEL), BF16)],
        scratch_shapes=[pltpu.VMEM((tm, d), BF16), pltpu.VMEM((GROUP_QKV // LANES, tm, LANES), F32)],
        compiler_params=pltpu.CompilerParams(
            dimension_semantics=("arbitrary", "arbitrary"), vmem_limit_bytes=VMEM_LIMIT),
        name="inproj",
    )(x, g_pre, sc, sh, w_perm)


def _attn_kernel(slopes_ref, q0, k0, v0, q1, k1, v1, q2, k2, v2, o_ref,
                 bias_scr, acc, mrun, lrun):
    pair = pl.program_id(1)
    seq = o_ref.shape[1]
    nblk = seq // ATT_BLOCK
    lane = lax.broadcasted_iota(I32, (ATT_BLOCK, LANES), 1)
    first_head = lane < HEAD_DIM
    half_lane = lax.broadcasted_iota(I32, (ATT_BLOCK // 2, LANES), 1)
    head_bits = (jnp.where(half_lane < HEAD_DIM, -1, 0), jnp.where(half_lane < HEAD_DIM, 0, -1))

    qi = lax.broadcasted_iota(I32, (ATT_BLOCK, 2 * ATT_BLOCK), 0)
    kj = lax.broadcasted_iota(I32, (ATT_BLOCK, 2 * ATT_BLOCK), 1)
    delta = ATT_BLOCK + qi - kj
    valid = (delta >= 0) & (delta <= ATT_REACH)
    delta0 = qi - kj
    valid0 = delta0 >= 0
    for g, d in enumerate(ATT_DILATIONS):
        for j in range(2):
            slope = slopes_ref[g * HEADS_PER_GROUP + 2 * pair + j]
            bias_scr[g, j, 0] = jnp.where(valid0, -slope * (delta0 * d).astype(F32), MASKED)
            bias_scr[g, j, 1] = jnp.where(valid, -slope * (delta * d).astype(F32), MASKED)

    def block_softmax(g, q_ref, k_ref, v_ref, r, i):
        lo = jnp.maximum(i - 1, 0)
        variant = jnp.minimum(i, 1)
        q = q_ref[0, r, pl.ds(pl.multiple_of(i * ATT_BLOCK, ATT_BLOCK), ATT_BLOCK), :]
        kw = k_ref[0, r, pl.ds(pl.multiple_of(lo * ATT_BLOCK, ATT_BLOCK), 2 * ATT_BLOCK), :]
        vw = v_ref[0, r, pl.ds(pl.multiple_of(lo * ATT_BLOCK, ATT_BLOCK), 2 * ATT_BLOCK), :]
        qbits = pltpu.bitcast(q, I32)
        outs = []
        for j in range(2):
            qh = pltpu.bitcast(qbits & head_bits[j], BF16)
            sc = _dot_nt(qh, kw) + bias_scr[g, j, variant]
            mx = jnp.max(sc, axis=-1, keepdims=True)
            p = jnp.exp(sc - mx)
            den = jnp.sum(p, axis=-1, keepdims=True)
            num = _dot(p.astype(BF16), vw)
            outs.append((num, mx, den))
        (n0, m0, l0), (n1, m1, l1) = outs
        return (jnp.where(first_head, n0, n1),
                jnp.where(first_head, m0, m1),
                jnp.where(first_head, l0, l1))

    def first_group(it, carry):
        blocks = [it * ATT_UNROLL + k for k in range(ATT_UNROLL)]
        results = [block_softmax(0, q0, k0, v0, 0, n) for n in blocks]
        for n, (num, mx, den) in zip(blocks, results):
            rows = pl.ds(pl.multiple_of(n * ATT_BLOCK, ATT_BLOCK), ATT_BLOCK)
            acc[rows, :] = num
            mrun[rows, :] = mx
            lrun[rows, :] = den
        return carry

    lax.fori_loop(0, nblk // ATT_UNROLL, first_group, 0)

    def merge_group(g, q_ref, k_ref, v_ref):
        d = ATT_DILATIONS[g]
        per_res = nblk // d

        def body(it, carry):
            blocks = [it * ATT_UNROLL + k for k in range(ATT_UNROLL)]
            results = [block_softmax(g, q_ref, k_ref, v_ref, n // per_res, n % per_res) for n in blocks]
            for n, (num, mx, den) in zip(blocks, results):
                r = n // per_res
                i = n % per_res
                rows = pl.ds(i * (ATT_BLOCK * d) + r, ATT_BLOCK, stride=d)
                m_old = mrun[rows, :]
                m_new = jnp.maximum(m_old, mx)
                a = jnp.exp(m_old - m_new)
                c = jnp.exp(mx - m_new)
                acc[rows, :] = a * acc[rows, :] + c * num
                lrun[rows, :] = a * lrun[rows, :] + c * den
                mrun[rows, :] = m_new
            return carry

        lax.fori_loop(0, nblk // ATT_UNROLL, body, 0)

    merge_group(1, q1, k1, v1)
    merge_group(2, q2, k2, v2)

    def finish(n, carry):
        rows = pl.ds(pl.multiple_of(n * ATT_BLOCK, ATT_BLOCK), ATT_BLOCK)
        o_ref[0, rows, :] = (acc[rows, :] / lrun[rows, :]).astype(BF16)
        return carry

    lax.fori_loop(0, nblk, finish, 0)


def _attention(qkv0, qkv1, qkv2, slopes):
    b, _, s, _ = qkv0.shape
    pairs = HEADS_PER_GROUP // 2
    col_blocks = HEADS_PER_GROUP * HEAD_DIM // LANES

    def specs(arr):
        _, d, sub, _ = arr.shape
        return [pl.BlockSpec((1, d, sub, LANES),
                             functools.partial(lambda bi, p, sec: (bi, 0, 0, sec * col_blocks + p), sec=sec))
                for sec in range(3)]

    return pl.pallas_call(
        _attn_kernel,
        grid=(b, pairs),
        in_specs=[pl.BlockSpec(memory_space=pltpu.SMEM)] + specs(qkv0) + specs(qkv1) + specs(qkv2),
        out_specs=pl.BlockSpec((1, s, LANES), lambda bi, p: (bi, 0, p)),
        out_shape=jax.ShapeDtypeStruct((b, s, ATT_OUT_WIDTH), BF16),
        scratch_shapes=[pltpu.VMEM((3, 2, 2, ATT_BLOCK, 2 * ATT_BLOCK), F32),
                        pltpu.VMEM((s, LANES), F32),
                        pltpu.VMEM((s, LANES), F32),
                        pltpu.VMEM((s, LANES), F32)],
        compiler_params=pltpu.CompilerParams(
            dimension_semantics=("arbitrary", "arbitrary"), vmem_limit_bytes=VMEM_LIMIT),
        name="attention",
    )(slopes, qkv0, qkv0, qkv0, qkv1, qkv1, qkv1, qkv2, qkv2, qkv2)


def _mixtail_kernel(u_ref, halo_ref, att_ref, gate_ref, x_ref,
                    gt_m_ref, sc_f_ref, sh_f_ref, g_post_ref, g_pre_ref,
                    wpg_ref, pscale_ref, wbp_ref, wba_ref, wout_ref, wrt_ref, brt_ref,
                    xmid_ref, route_ref, meta_ref, counts_ref, xs_hbm,
                    pu, xbuf, zbuf, fill, meta_s, sem, sem_s):
    i = pl.program_id(1)
    tm = x_ref.shape[1]
    step = pl.program_id(0) * pl.num_programs(1) + i
    last = pl.num_programs(0) * pl.num_programs(1) - 1
    slot = step % 2
    region = _region_rows(pl.num_programs(0) * pl.num_programs(1) * tm)

    @pl.when(step == 0)
    def _():
        fill[...] = jnp.zeros_like(fill)

    halo = halo_ref[0].astype(F32)
    pu[0:POOL_HALO, :] = jnp.where(i > 0, halo, jnp.zeros_like(halo))
    pu[POOL_HALO:POOL_HALO + tm, :] = u_ref[0].astype(F32)
    t = i * tm + lax.broadcasted_iota(I32, (tm, 1), 0)
    y_pool = jnp.zeros((tm, D_MODEL), F32)
    for g, w in enumerate(POOL_WINDOWS):
        cols = slice(g * POOL_GROUP_DIM, (g + 1) * POOL_GROUP_DIM)
        win = pu[pl.ds(POOL_HALO, tm), cols]
        for j in range(1, w):
            win = win + pu[pl.ds(POOL_HALO - j, tm), cols]
        count = jnp.minimum(t + 1, w).astype(F32)
        pooled = win / count - pu[pl.ds(POOL_HALO, tm), cols]
        mixed = _dot(pooled.astype(BF16), wpg_ref[g]) * pscale_ref[:, cols]
        y_pool = y_pool + _dot(mixed.astype(BF16), wbp_ref[cols, :])

    y_att = _dot(att_ref[0], wba_ref[...])
    merged = (gate_ref[0, :, 0:D_MODEL].astype(F32) * y_pool
              + gate_ref[0, :, D_MODEL:2 * D_MODEL].astype(F32) * y_att)
    y = _dot(merged.astype(BF16), wout_ref[...])
    x_mid = x_ref[0] + gt_m_ref[0] * _rmsnorm(y, g_post_ref[...])
    xmid_ref[0] = x_mid

    h2 = _rmsnorm(x_mid, g_pre_ref[...]) * (1.0 + sc_f_ref[0]) + sh_f_ref[0]

    logits = _dot_nt(wrt_ref[...], h2, precision=HIGHEST) + brt_ref[...]
    neg_inf = -jnp.inf
    far = float(LANES)
    gl = logits[N_EXPERTS:N_EXPERTS + N_EXPERT_GROUPS, :]
    grow = lax.broadcasted_iota(I32, gl.shape, 0).astype(F32)
    gmax = jnp.max(gl, axis=0, keepdims=True)
    gsel = jnp.min(jnp.where(gl == gmax, grow, far), axis=0, keepdims=True)
    p_group = 1.0 / jnp.sum(jnp.exp(gl - gmax), axis=0, keepdims=True)
    erow = lax.broadcasted_iota(I32, (N_EXPERTS, tm), 0).astype(F32)
    e_lo = gsel * float(EXPERTS_PER_GROUP)
    el = jnp.where((erow >= e_lo) & (erow < e_lo + float(EXPERTS_PER_GROUP)), logits[0:N_EXPERTS, :], neg_inf)
    v1 = jnp.max(el, axis=0, keepdims=True)
    i1 = jnp.min(jnp.where(el == v1, erow, far), axis=0, keepdims=True)
    el2 = jnp.where(erow == i1, neg_inf, el)
    v2 = jnp.max(el2, axis=0, keepdims=True)
    i2 = jnp.min(jnp.where(el2 == v2, erow, far), axis=0, keepdims=True)
    e21 = jnp.exp(v2 - v1)
    w1 = p_group / (1.0 + e21)
    w2 = p_group * e21 / (1.0 + e21)

    pick1 = erow == i1
    pick2 = erow == i2
    assign = jnp.where(pick1 | pick2, 1.0, 0.0)
    before = (lax.broadcasted_iota(I32, (tm, tm), 0) < lax.broadcasted_iota(I32, (tm, tm), 1))
    rank = _dot(assign.astype(BF16), jnp.where(before, 1.0, 0.0).astype(BF16))
    assign_pad = jnp.concatenate([assign, jnp.zeros((LANES - N_EXPERTS, tm), F32)], axis=0).astype(BF16)
    cnt_row = _dot_nt(jnp.ones((SUBLANES, tm), BF16), assign_pad)
    e_col = lax.broadcasted_iota(I32, (N_EXPERTS, LANES), 0)
    e_lane = lax.broadcasted_iota(I32, (N_EXPERTS, LANES), 1)
    run_start = jnp.sum(jnp.where(e_lane < e_col, cnt_row[0:1, :], 0.0), axis=1, keepdims=True)
    pos = rank + run_start
    key1 = jnp.sum(jnp.where(pick1, pos, 0.0), axis=0, keepdims=True)
    key2 = jnp.sum(jnp.where(pick2, pos, 0.0), axis=0, keepdims=True)

    row8 = lax.broadcasted_iota(I32, (SUBLANES, tm), 0)
    route_ref[...] = jnp.where(row8 == 0, key1, jnp.where(row8 == 1, key2,
                               jnp.where(row8 == 2, w1, jnp.where(row8 == 3, w2, 0.0))))

    filled = fill[...]
    mrow = lax.broadcasted_iota(I32, (SUBLANES, LANES), 0)
    meta_ref[0] = jnp.where(mrow == 0, filled, jnp.where(mrow == 1, cnt_row, 0.0)).astype(I32)
    fill[...] = filled + cnt_row
    counts_ref[...] = (filled + cnt_row).astype(I32)

    srow = lax.broadcasted_iota(I32, (SORT_ROWS, tm), 0).astype(F32)
    perm = jnp.where((srow == key1) | (srow == key2), 1.0, 0.0).astype(BF16)
    h2b = h2.astype(BF16)
    xslot = xbuf.at[slot]
    for s in range(ROW_TILES):
        xslot[pl.ds(s, SORT_ROWS, stride=ROW_TILES), :] = _dot(perm, h2b[:, s * LANES:(s + 1) * LANES])

    def tile_runs(which, action):
        def per_expert(e, local):
            n = meta_s[which, 1, e]
            _run_copies(n, xbuf.at[which], local, xs_hbm, e * region + meta_s[which, 0, e], sem, action)
            return local + n
        lax.fori_loop(0, N_EXPERTS, per_expert, 0)

    @pl.when(step > 0)
    def _():
        tile_runs(1 - slot, lambda cp: cp.wait())

    meta_copy = pltpu.make_async_copy(meta_ref.at[0], meta_s.at[slot], sem_s)
    meta_copy.start()
    meta_copy.wait()
    tile_runs(slot, lambda cp: cp.start())

    @pl.when(step == last)
    def _():
        tile_runs(slot, lambda cp: cp.wait())
        zbuf[...] = jnp.zeros_like(zbuf)

        def pad_copy(e):
            end = e * region + meta_s[slot, 0, e] + meta_s[slot, 1, e]
            return pltpu.make_async_copy(
                zbuf, xs_hbm.at[pl.ds(end * ROW_TILES, MOE_BLOCK * ROW_TILES), :], sem)

        def start_pad(e, carry):
            pad_copy(e).start()
            return carry

        def wait_pad(e, carry):
            pad_copy(e).wait()
            return carry

        lax.fori_loop(0, N_EXPERTS, start_pad, 0)
        lax.fori_loop(0, N_EXPERTS, wait_pad, 0)


def _mixtail(u, att, gates, x, gt_m, sc_f, sh_f, g_post, g_pre,
             wpg, pscale, wbp, wba, wout, wrt, brt):
    b, s, d = x.shape
    tm = ROW_TILE
    tiles = s // tm
    n_tiles = b * tiles
    halo_blocks = tm // POOL_HALO
    region = _region_rows(b * s)
    const2 = lambda bi, i: (0, 0)
    const3 = lambda bi, i: (0, 0, 0)
    per_b = lambda bi, i: (bi, 0, 0)
    tile = lambda bi, i: (bi, i, 0)
    single = dict(pipeline_mode=pl.Buffered(1))
    return pl.pallas_call(
        _mixtail_kernel,
        grid=(b, tiles),
        in_specs=[pl.BlockSpec((1, tm, POOL_WIDTH), tile),
                  pl.BlockSpec((1, POOL_HALO, POOL_WIDTH),
                               lambda bi, i: (bi, jnp.maximum(i * halo_blocks - 1, 0), 0)),
                  pl.BlockSpec((1, tm, ATT_OUT_WIDTH), tile),
                  pl.BlockSpec((1, tm, 2 * D_MODEL), tile),
                  pl.BlockSpec((1, tm, d), tile),
                  pl.BlockSpec((1, 1, d), per_b),
                  pl.BlockSpec((1, 1, d), per_b),
                  pl.BlockSpec((1, 1, d), per_b),
                  pl.BlockSpec((1, d), const2),
                  pl.BlockSpec((1, d), const2),
                  pl.BlockSpec(wpg.shape, const3, **single),
                  pl.BlockSpec((1, POOL_WIDTH), const2),
                  pl.BlockSpec(wbp.shape, const2, **single),
                  pl.BlockSpec(wba.shape, const2, **single),
                  pl.BlockSpec(wout.shape, const2, **single),
                  pl.BlockSpec(wrt.shape, const2, **single),
                  pl.BlockSpec(brt.shape, const2)],
        out_specs=[pl.BlockSpec((1, tm, d), tile),
                   pl.BlockSpec((SUBLANES, tm), lambda bi, i: (0, bi * tiles + i)),
                   pl.BlockSpec((1, SUBLANES, LANES), lambda bi, i: (bi * tiles + i, 0, 0)),
                   pl.BlockSpec((SUBLANES, LANES), const2),
                   pl.BlockSpec(memory_space=pl.ANY)],
        out_shape=[jax.ShapeDtypeStruct((b, s, d), F32),
                   jax.ShapeDtypeStruct((SUBLANES, b * s), F32),
                   jax.ShapeDtypeStruct((n_tiles, SUBLANES, LANES), I32),
                   jax.ShapeDtypeStruct((SUBLANES, LANES), I32),
                   jax.ShapeDtypeStruct((N_EXPERTS * region * ROW_TILES, LANES), F32)],
        scratch_shapes=[pltpu.VMEM((POOL_HALO + tm, POOL_WIDTH), F32),
                        pltpu.VMEM((2, SORT_ROWS * ROW_TILES, LANES), F32),
                        pltpu.VMEM((MOE_BLOCK * ROW_TILES, LANES), F32),
                        pltpu.VMEM((SUBLANES, LANES), F32),
                        pltpu.SMEM((2, SUBLANES, LANES), I32),
                        pltpu.SemaphoreType.DMA(()),
                        pltpu.SemaphoreType.DMA(())],
        compiler_params=pltpu.CompilerParams(
            dimension_semantics=("arbitrary", "arbitrary"), vmem_limit_bytes=VMEM_LIMIT),
        name="mixtail",
    )(u, u, att, gates, x, gt_m, sc_f, sh_f, g_post, g_pre,
      wpg, pscale, wbp, wba, wout, wrt, brt)


def _expert_kernel(brow_ref, first_ref, nused_ref, x_ref, wg_ref, wu_ref, wd_ref, y_ref,
                   wg_bf, wu_bf, wd_bf):
    del brow_ref
    blk = pl.program_id(0)
    bm = MOE_BLOCK

    @pl.when(first_ref[blk] == 1)
    def _():
        wg_bf[...] = wg_ref[...].astype(BF16)
        wu_bf[...] = wu_ref[...].astype(BF16)
        wd_bf[...] = wd_ref[...].astype(BF16)

    @pl.when(blk < nused_ref[0])
    def _():
        x = _tiles_to_rows(x_ref, bm).astype(BF16)
        a = _dot(x, wg_bf[...])
        u = _dot(x, wu_bf[...])
        mid = ((a * _sigmoid(a)) * u).astype(BF16)
        for s in range(ROW_TILES):
            y_ref[pl.ds(s, bm, stride=ROW_TILES), :] = _dot(mid, wd_bf[:, s * LANES:(s + 1) * LANES])


def _experts(xs, block_row, block_expert, first, n_used, w_gate, w_up, w_down):
    bm = MOE_BLOCK
    n_blocks = block_row.shape[0]
    rows = pl.BlockSpec((bm * ROW_TILES, LANES), lambda blk, br, be, fi, nu: (br[blk], 0))
    w_in = pl.BlockSpec((None, D_MODEL, D_EXPERT), lambda blk, br, be, fi, nu: (be[blk], 0, 0))
    w_out = pl.BlockSpec((None, D_EXPERT, D_MODEL), lambda blk, br, be, fi, nu: (be[blk], 0, 0))

    def body(brow_ref, bexp_ref, first_ref, nused_ref, *refs):
        del bexp_ref
        _expert_kernel(brow_ref, first_ref, nused_ref, *refs)

    grid_spec = pltpu.PrefetchScalarGridSpec(
        num_scalar_prefetch=4,
        grid=(n_blocks,),
        in_specs=[rows, w_in, w_in, w_out],
        out_specs=rows,
        scratch_shapes=[pltpu.VMEM((D_MODEL, D_EXPERT), BF16),
                        pltpu.VMEM((D_MODEL, D_EXPERT), BF16),
                        pltpu.VMEM((D_EXPERT, D_MODEL), BF16)],
    )
    return pl.pallas_call(
        body,
        grid_spec=grid_spec,
        out_shape=jax.ShapeDtypeStruct(xs.shape, F32),
        compiler_params=pltpu.CompilerParams(
            dimension_semantics=("arbitrary",), vmem_limit_bytes=VMEM_LIMIT),
        name="experts",
    )(block_row, block_expert, first, n_used, xs, w_gate, w_up, w_down)


def _block_tables(counts, n_tokens):
    bm = MOE_BLOCK
    n_blocks = 2 * n_tokens // bm + N_EXPERTS
    blocks_per_region = _region_rows(n_tokens) // bm
    nb = (counts + bm - 1) // bm
    nb_end = jnp.cumsum(nb)
    n_used = nb_end[-1]
    blk = jnp.arange(n_blocks, dtype=I32)
    live = jnp.minimum(blk, n_used - 1)
    expert = jnp.minimum(jnp.sum((live[:, None] >= nb_end[None, :]).astype(I32), axis=1), N_EXPERTS - 1)
    k = live - (nb_end[expert] - nb[expert])
    block_row = (expert * blocks_per_region + k).astype(I32)
    first = ((k == 0) & (blk < n_used)).astype(I32)
    return block_row, expert, first, n_used.reshape(1).astype(I32)


def _combine_kernel(meta_ref, y_hbm, route_ref, xmid_ref, gt_ref, g_ref, o_ref, ybuf, rt_scr, sem):
    step = pl.program_id(0)
    n_steps = pl.num_programs(0)
    tm = ROW_TILE
    slot = step % 2
    region = _region_rows(n_steps * tm)

    def tile_runs(tile, which, action):
        def per_expert(e, local):
            filled = meta_ref[(tile * 2) * N_EXPERTS + e]
            n = meta_ref[(tile * 2 + 1) * N_EXPERTS + e]
            _run_copies(n, y_hbm, e * region + filled, ybuf.at[which], local, sem, action)
            return local + n
        lax.fori_loop(0, N_EXPERTS, per_expert, 0)

    @pl.when(step == 0)
    def _():
        tile_runs(0, 0, lambda cp: cp.start())

    tile_runs(step, slot, lambda cp: cp.wait())

    @pl.when(step + 1 < n_steps)
    def _():
        tile_runs(step + 1, 1 - slot, lambda cp: cp.start())

    rt_scr[...] = jnp.zeros_like(rt_scr)
    rt_scr[0:SUBLANES, :] = route_ref[...]
    cols = jnp.concatenate([rt_scr[:, c * LANES:(c + 1) * LANES].T for c in range(tm // LANES)], axis=0)
    key1, key2, w1, w2 = cols[:, 0:1], cols[:, 1:2], cols[:, 2:3], cols[:, 3:4]

    yb = _tiles_to_rows(ybuf.at[slot], SORT_ROWS).astype(BF16)
    spos = lax.broadcasted_iota(I32, (tm, SORT_ROWS), 1).astype(F32)
    y1 = _dot(jnp.where(spos == key1, 1.0, 0.0).astype(BF16), yb)
    y2 = _dot(jnp.where(spos == key2, 1.0, 0.0).astype(BF16), yb)
    y = w1 * y1 + w2 * y2
    o_ref[...] = xmid_ref[...] + gt_ref[0] * _rmsnorm(y, g_ref[...])


def _combine(meta, ys, route, x_mid, gt_f, g_post, seq):
    t, d = x_mid.shape
    tm = ROW_TILE
    tiles_per_seq = seq // tm
    grid_spec = pltpu.PrefetchScalarGridSpec(
        num_scalar_prefetch=1,
        grid=(t // tm,),
        in_specs=[pl.BlockSpec(memory_space=pl.ANY),
                  pl.BlockSpec((SUBLANES, tm), lambda i, m: (0, i)),
                  pl.BlockSpec((tm, d), lambda i, m: (i, 0)),
                  pl.BlockSpec((1, 1, d), lambda i, m: (i // tiles_per_seq, 0, 0)),
                  pl.BlockSpec((1, d), lambda i, m: (0, 0))],
        out_specs=pl.BlockSpec((tm, d), lambda i, m: (i, 0)),
        scratch_shapes=[pltpu.VMEM((2, SORT_ROWS * ROW_TILES, LANES), F32),
                        pltpu.VMEM((LANES, tm), F32),
                        pltpu.SemaphoreType.DMA(())],
    )
    return pl.pallas_call(
        _combine_kernel,
        grid_spec=grid_spec,
        out_shape=jax.ShapeDtypeStruct((t, d), F32),
        compiler_params=pltpu.CompilerParams(
            dimension_semantics=("arbitrary",), vmem_limit_bytes=VMEM_LIMIT),
        name="combine",
    )(meta, ys, route, x_mid, gt_f, g_post)


def kernel(x, c, w_ada, b_ada, g_pre_mix, g_post_mix, g_pre_ffn, g_post_ffn, w_in, w_pool_group, pool_scale, w_branch_pool, w_branch_att, w_out, w_group_router, b_group_router, w_expert_router, b_expert_router, w_exp_gate, w_exp_up, w_exp_down):
    b, s, d = x.shape
    assert d == D_MODEL and s % (ATT_BLOCK * 2 * ATT_DILATIONS[2]) == 0 and s % ROW_TILE == 0
    assert (2 * b * s) % MOE_BLOCK == 0
    depth = w_ada.shape[0]
    slopes = jnp.exp2(-ALIBI_MAX_BIAS * jnp.arange(1, N_ATT_HEADS + 1, dtype=F32) / N_ATT_HEADS)
    q_scale = HEAD_DIM ** -0.5

    for layer in range(depth):
        mod = _adaln(c, w_ada[layer], b_ada[layer]).reshape(b, 6, 1, d)
        sh_m, sc_m, gt_m, sh_f, sc_f, gt_f = [mod[:, j] for j in range(6)]

        wl = w_in[layer]
        q_lo, k_lo, v_lo = POOL_WIDTH, POOL_WIDTH + 768, POOL_WIDTH + 2 * 768
        group_cols = []
        for g in range(3):
            sl = slice(g * 256, (g + 1) * 256)
            group_cols += [wl[:, q_lo:k_lo][:, sl] * q_scale, wl[:, k_lo:v_lo][:, sl],
                           wl[:, v_lo:v_lo + 768][:, sl]]
        w_perm = jnp.concatenate([wl[:, :POOL_WIDTH]] + group_cols + [wl[:, v_lo + 768:]],
                                 axis=1).astype(BF16)

        u, qkv0, qkv1, qkv2, gates = _inproj(x, g_pre_mix[layer].reshape(1, d), sc_m, sh_m, w_perm)
        att = _attention(qkv0, qkv1, qkv2, slopes)

        pad_rows = ROUTER_ROWS - N_EXPERTS - N_EXPERT_GROUPS
        wrt = jnp.concatenate([w_expert_router[layer].T, w_group_router[layer].T,
                               jnp.zeros((pad_rows, d), F32)], axis=0)
        brt = jnp.concatenate([b_expert_router[layer], b_group_router[layer],
                               jnp.zeros((pad_rows,), F32)]).reshape(ROUTER_ROWS, 1)

        x_mid, route, meta, counts, xs = _mixtail(
            u, att, gates, x, gt_m, sc_f, sh_f,
            g_post_mix[layer].reshape(1, d), g_pre_ffn[layer].reshape(1, d),
            w_pool_group[layer].astype(BF16), pool_scale[layer].reshape(1, POOL_WIDTH),
            w_branch_pool[layer].astype(BF16), w_branch_att[layer].astype(BF16),
            w_out[layer].astype(BF16), wrt, brt)

        block_row, block_expert, first, n_used = _block_tables(counts[0, :N_EXPERTS], b * s)
        ys = _experts(xs, block_row, block_expert, first, n_used,
                      w_exp_gate[layer], w_exp_up[layer], w_exp_down[layer])
        run_meta = meta[:, 0:2, 0:N_EXPERTS].reshape(-1)
        x = _combine(run_meta, ys, route, x_mid.reshape(b * s, d), gt_f,
                     g_post_ffn[layer].reshape(1, d), s).reshape(b, s, d)
    return x
```

```python
import functools

import jax
import jax.numpy as jnp
from jax import lax
from jax.experimental import pallas as pl
from jax.experimental.pallas import tpu as pltpu

F32 = jnp.float32
BF16 = jnp.bfloat16
I32 = jnp.int32
HIGHEST = lax.Precision.HIGHEST

D_MODEL = 1024
LANES = 128
SUBLANES = 8
ROW_WORDS = D_MODEL // (2 * LANES)
HI_HALF = -65536

POOL_WINDOWS = (2, 4, 8, 16)
POOL_GROUP_DIM = 128
POOL_WIDTH = 512
POOL_HALO = 16

HEAD_DIM = 64
ATT_DILATIONS = (1, 4, 16)
ATT_REACH = 128
ATT_BLOCK = 128
ATT_UNROLL = 4
HEADS_PER_GROUP = 4
N_ATT_HEADS = 12
GROUP_QKV = 3 * HEADS_PER_GROUP * HEAD_DIM
ATT_OUT_WIDTH = 256
ALIBI_MAX_BIAS = 8.0
IN_WIDTH = POOL_WIDTH + 3 * GROUP_QKV + 2 * D_MODEL
MASKED = -1e30

N_EXPERT_GROUPS = 4
EXPERTS_PER_GROUP = 8
N_EXPERTS = 32
D_EXPERT = 512
RMS_EPS = 1e-6

ROW_TILE = 512
SORT_ROWS = 2 * ROW_TILE
RUN_CHUNK = 64
MOE_BLOCK = 256
VMEM_LIMIT = 52 * 1024 * 1024


def _sigmoid(x):
    return 0.5 * jnp.tanh(0.5 * x) + 0.5


def _rmsnorm(x, g):
    return x * lax.rsqrt(jnp.mean(x * x, axis=-1, keepdims=True) + RMS_EPS) * g


def _dot(a, b):
    return jnp.dot(a, b, preferred_element_type=F32)


def _dot_nt(a, b, **kw):
    return lax.dot_general(a, b, (((1,), (1,)), ((), ())), preferred_element_type=F32, **kw)


def _pack_rows(ref, w, n, lo, hi):
    word = (lax.shift_right_logical(pltpu.bitcast(lo, I32), 16) | (pltpu.bitcast(hi, I32) & HI_HALF))
    ref[pl.ds(w, n, stride=ROW_WORDS), :] = word


def _unpack_rows(ref, n):
    cols = []
    for w in range(ROW_WORDS):
        word = ref[pl.ds(w, n, stride=ROW_WORDS), :]
        cols += [pltpu.bitcast(word << 16, F32), pltpu.bitcast(word & HI_HALF, F32)]
    return jnp.concatenate(cols, axis=1)


def _bf16_exact(x):
    return x.astype(BF16).astype(F32)


def _region_rows(n_tokens):
    return n_tokens + MOE_BLOCK


def _run_copies(n, src, src_row, dst, dst_row, sem):
    def piece(offset, size):
        return pltpu.make_async_copy(
            src.at[pl.ds((src_row + offset) * ROW_WORDS, size * ROW_WORDS), :],
            dst.at[pl.ds((dst_row + offset) * ROW_WORDS, size * ROW_WORDS), :], sem)

    def whole_chunk(c, carry):
        piece(c * RUN_CHUNK, RUN_CHUNK).start()
        return carry

    lax.fori_loop(0, n // RUN_CHUNK, whole_chunk, 0)
    for bit in range(RUN_CHUNK.bit_length() - 1):
        size = 1 << bit
        above = n & ~(2 * size - 1)

        @pl.when((n & size) != 0)
        def _():
            piece(above, size).start()


def _adaln_kernel(c_ref, w_ref, b_ref, o_ref):
    c = c_ref[...]
    a = c * _sigmoid(c)
    o_ref[...] = jnp.dot(a, w_ref[...], preferred_element_type=F32, precision=HIGHEST) + b_ref[...]


def _adaln(c, w_ada, b_ada):
    b, d = c.shape
    n = w_ada.shape[1]
    rows = -(-b // SUBLANES) * SUBLANES
    cp = jnp.pad(c, ((0, rows - b), (0, 0)))
    nt = 1536
    out = pl.pallas_call(
        _adaln_kernel,
        grid=(n // nt,),
        in_specs=[pl.BlockSpec((rows, d), lambda j: (0, 0)),
                  pl.BlockSpec((d, nt), lambda j: (0, j)),
                  pl.BlockSpec((1, nt), lambda j: (0, j))],
        out_specs=pl.BlockSpec((rows, nt), lambda j: (0, j)),
        out_shape=jax.ShapeDtypeStruct((rows, n), F32),
        compiler_params=pltpu.CompilerParams(vmem_limit_bytes=VMEM_LIMIT),
        name="adaln",
    )(cp, w_ada, b_ada.reshape(1, n))
    return out[:b]


def _inproj_kernel(x_ref, g_ref, sc_ref, sh_ref, w_ref,
                   u_ref, qkv0_ref, qkv1_ref, qkv2_ref, gate_ref, h_scr, p_scr):
    tm = x_ref.shape[1]
    h = _rmsnorm(x_ref[0], g_ref[...]) * (1.0 + sc_ref[0]) + sh_ref[0]
    h_scr[...] = h.astype(BF16)

    u_ref[0] = _dot(h_scr[...], w_ref[:, 0:POOL_WIDTH]).astype(BF16)

    col = POOL_WIDTH
    qkv0_ref[0, 0] = _dot(h_scr[...], w_ref[:, col:col + GROUP_QKV]).astype(BF16)
    for out_ref, d in ((qkv1_ref, ATT_DILATIONS[1]), (qkv2_ref, ATT_DILATIONS[2])):
        col += GROUP_QKV
        proj = _dot(h_scr[...], w_ref[:, col:col + GROUP_QKV])
        for cb in range(GROUP_QKV // LANES):
            p_scr[cb] = proj[:, cb * LANES:(cb + 1) * LANES]
        for r in range(d):
            out_ref[0, r] = jnp.concatenate(
                [p_scr[cb, pl.ds(r, tm // d, stride=d), :] for cb in range(GROUP_QKV // LANES)],
                axis=1).astype(BF16)
    col += GROUP_QKV

    chunk = 512
    for j in range(2 * D_MODEL // chunk):
        g = _dot(h_scr[...], w_ref[:, col + j * chunk:col + (j + 1) * chunk])
        gate_ref[0, :, j * chunk:(j + 1) * chunk] = _sigmoid(g).astype(BF16)


def _inproj(x, g_pre, sc, sh, w_perm):
    b, s, d = x.shape
    tm = ROW_TILE
    d1, d2 = ATT_DILATIONS[1], ATT_DILATIONS[2]
    grid = (b, s // tm)
    const = lambda bi, i: (0, 0)
    per_b = lambda bi, i: (bi, 0, 0)
    return pl.pallas_call(
        _inproj_kernel,
        grid=grid,
        in_specs=[pl.BlockSpec((1, tm, d), lambda bi, i: (bi, i, 0)),
                  pl.BlockSpec((1, d), const),
                  pl.BlockSpec((1, 1, d), per_b),
                  pl.BlockSpec((1, 1, d), per_b),
                  pl.BlockSpec((d, IN_WIDTH), const, pipeline_mode=pl.Buffered(1))],
        out_specs=[pl.BlockSpec((1, tm, POOL_WIDTH), lambda bi, i: (bi, i, 0)),
                   pl.BlockSpec((1, 1, tm, GROUP_QKV), lambda bi, i: (bi, 0, i, 0)),
                   pl.BlockSpec((1, d1, tm // d1, GROUP_QKV), lambda bi, i: (bi, 0, i, 0)),
                   pl.BlockSpec((1, d2, tm // d2, GROUP_QKV), lambda bi, i: (bi, 0, i, 0)),
                   pl.BlockSpec((1, tm, 2 * D_MODEL), lambda bi, i: (bi, i, 0))],
        out_shape=[jax.ShapeDtypeStruct((b, s, POOL_WIDTH), BF16),
                   jax.ShapeDtypeStruct((b, 1, s, GROUP_QKV), BF16),
                   jax.ShapeDtypeStruct((b, d1, s // d1, GROUP_QKV), BF16),
                   jax.ShapeDtypeStruct((b, d2, s // d2, GROUP_QKV), BF16),
                   jax.ShapeDtypeStruct((b, s, 2 * D_MODEL), BF16)],
        scratch_shapes=[pltpu.VMEM((tm, d), BF16), pltpu.VMEM((GROUP_QKV // LANES, tm, LANES), F32)],
        compiler_params=pltpu.CompilerParams(
            dimension_semantics=("arbitrary", "arbitrary"), vmem_limit_bytes=VMEM_LIMIT),
        name="inproj",
    )(x, g_pre, sc, sh, w_perm)


def _attn_kernel(slopes_ref, q0, k0, v0, q1, k1, v1, q2, k2, v2, o_ref,
                 bias_scr, acc, mrun, lrun):
    pair = pl.program_id(1)
    seq = o_ref.shape[1]
    nblk = seq // ATT_BLOCK
    lane = lax.broadcasted_iota(I32, (ATT_BLOCK, LANES), 1)
    first_head = lane < HEAD_DIM
    half_lane = lax.broadcasted_iota(I32, (ATT_BLOCK // 2, LANES), 1)
    head_bits = (jnp.where(half_lane < HEAD_DIM, -1, 0), jnp.where(half_lane < HEAD_DIM, 0, -1))

    qi = lax.broadcasted_iota(I32, (ATT_BLOCK, 2 * ATT_BLOCK), 0)
    kj = lax.broadcasted_iota(I32, (ATT_BLOCK, 2 * ATT_BLOCK), 1)
    delta = ATT_BLOCK + qi - kj
    valid = (delta >= 0) & (delta <= ATT_REACH)
    delta0 = qi - kj
    valid0 = delta0 >= 0
    for g, d in enumerate(ATT_DILATIONS):
        for j in range(2):
            slope = slopes_ref[g * HEADS_PER_GROUP + 2 * pair + j]
            bias_scr[g, j, 0] = jnp.where(valid0, -slope * (delta0 * d).astype(F32), MASKED)
            bias_scr[g, j, 1] = jnp.where(valid, -slope * (delta * d).astype(F32), MASKED)

    def block_softmax(g, q_ref, k_ref, v_ref, r, i):
        lo = jnp.maximum(i - 1, 0)
        variant = jnp.minimum(i, 1)
        q = q_ref[0, r, pl.ds(pl.multiple_of(i * ATT_BLOCK, ATT_BLOCK), ATT_BLOCK), :]
        kw = k_ref[0, r, pl.ds(pl.multiple_of(lo * ATT_BLOCK, ATT_BLOCK), 2 * ATT_BLOCK), :]
        vw = v_ref[0, r, pl.ds(pl.multiple_of(lo * ATT_BLOCK, ATT_BLOCK), 2 * ATT_BLOCK), :]
        qbits = pltpu.bitcast(q, I32)
        outs = []
        for j in range(2):
            qh = pltpu.bitcast(qbits & head_bits[j], BF16)
            sc = _dot_nt(qh, kw) + bias_scr[g, j, variant]
            mx = jnp.max(sc, axis=-1, keepdims=True)
            p = jnp.exp(sc - mx)
            den = jnp.sum(p, axis=-1, keepdims=True)
            num = _dot(p.astype(BF16), vw)
            outs.append((num, mx, den))
        (n0, m0, l0), (n1, m1, l1) = outs
        return (jnp.where(first_head, n0, n1),
                jnp.where(first_head, m0, m1),
                jnp.where(first_head, l0, l1))

    def first_group(it, carry):
        blocks = [it * ATT_UNROLL + k for k in range(ATT_UNROLL)]
        results = [block_softmax(0, q0, k0, v0, 0, n) for n in blocks]
        for n, (num, mx, den) in zip(blocks, results):
            rows = pl.ds(pl.multiple_of(n * ATT_BLOCK, ATT_BLOCK), ATT_BLOCK)
            acc[rows, :] = num
            mrun[rows, :] = mx
            lrun[rows, :] = den
        return carry

    lax.fori_loop(0, nblk // ATT_UNROLL, first_group, 0)

    def merge_group(g, q_ref, k_ref, v_ref):
        d = ATT_DILATIONS[g]
        per_res = nblk // d

        def body(it, carry):
            blocks = [it * ATT_UNROLL + k for k in range(ATT_UNROLL)]
            results = [block_softmax(g, q_ref, k_ref, v_ref, n // per_res, n % per_res) for n in blocks]
            for n, (num, mx, den) in zip(blocks, results):
                r = n // per_res
                i = n % per_res
                rows = pl.ds(i * (ATT_BLOCK * d) + r, ATT_BLOCK, stride=d)
                m_old = mrun[rows, :]
                m_new = jnp.maximum(m_old, mx)
                a = jnp.exp(m_old - m_new)
                c = jnp.exp(mx - m_new)
                acc[rows, :] = a * acc[rows, :] + c * num
                lrun[rows, :] = a * lrun[rows, :] + c * den
                mrun[rows, :] = m_new
            return carry

        lax.fori_loop(0, nblk // ATT_UNROLL, body, 0)

    merge_group(1, q1, k1, v1)
    merge_group(2, q2, k2, v2)

    def finish(n, carry):
        rows = pl.ds(pl.multiple_of(n * ATT_BLOCK, ATT_BLOCK), ATT_BLOCK)
        o_ref[0, rows, :] = (acc[rows, :] / lrun[rows, :]).astype(BF16)
        return carry

    lax.fori_loop(0, nblk, finish, 0)


def _attention(qkv0, qkv1, qkv2, slopes):
    b, _, s, _ = qkv0.shape
    pairs = HEADS_PER_GROUP // 2
    col_blocks = HEADS_PER_GROUP * HEAD_DIM // LANES

    def specs(arr):
        _, d, sub, _ = arr.shape
        return [pl.BlockSpec((1, d, sub, LANES),
                             functools.partial(lambda bi, p, sec: (bi, 0, 0, sec * col_blocks + p), sec=sec))
                for sec in range(3)]

    return pl.pallas_call(
        _attn_kernel,
        grid=(b, pairs),
        in_specs=[pl.BlockSpec(memory_space=pltpu.SMEM)] + specs(qkv0) + specs(qkv1) + specs(qkv2),
        out_specs=pl.BlockSpec((1, s, LANES), lambda bi, p: (bi, 0, p)),
        out_shape=jax.ShapeDtypeStruct((b, s, ATT_OUT_WIDTH), BF16),
        scratch_shapes=[pltpu.VMEM((3, 2, 2, ATT_BLOCK, 2 * ATT_BLOCK), F32),
                        pltpu.VMEM((s, LANES), F32),
                        pltpu.VMEM((s, LANES), F32),
                        pltpu.VMEM((s, LANES), F32)],
        compiler_params=pltpu.CompilerParams(
            dimension_semantics=("arbitrary", "arbitrary"), vmem_limit_bytes=VMEM_LIMIT),
        name="attention",
    )(slopes, qkv0, qkv0, qkv0, qkv1, qkv1, qkv1, qkv2, qkv2, qkv2)


def _mixtail_kernel(u_ref, halo_ref, att_ref, gate_ref, x_ref,
                    gt_m_ref, sc_f_ref, sh_f_ref, g_post_ref, g_pre_ref,
                    wpg_ref, pscale_ref, wbp_ref, wba_ref, wout_ref, wrh_ref, wrl_ref, br_ref,
                    xmid_ref, route_ref, meta_ref, counts_ref, xs_hbm,
                    pu, xbuf, zbuf, fill, meta_s, sem, sem_s):
    i = pl.program_id(1)
    tm = x_ref.shape[1]
    step = pl.program_id(0) * pl.num_programs(1) + i
    last = pl.num_programs(0) * pl.num_programs(1) - 1
    slot = step % 2
    region = _region_rows(pl.num_programs(0) * pl.num_programs(1) * tm)

    @pl.when(step == 0)
    def _():
        fill[...] = jnp.zeros_like(fill)

    halo = halo_ref[0].astype(F32)
    pu[0:POOL_HALO, :] = jnp.where(i > 0, halo, jnp.zeros_like(halo))
    pu[POOL_HALO:POOL_HALO + tm, :] = u_ref[0].astype(F32)
    t = i * tm + lax.broadcasted_iota(I32, (tm, 1), 0)
    mixed = []
    for g, w in enumerate(POOL_WINDOWS):
        cols = slice(g * POOL_GROUP_DIM, (g + 1) * POOL_GROUP_DIM)
        win = pu[pl.ds(POOL_HALO, tm), cols]
        for j in range(1, w):
            win = win + pu[pl.ds(POOL_HALO - j, tm), cols]
        count = jnp.minimum(t + 1, w).astype(F32)
        pooled = win / count - pu[pl.ds(POOL_HALO, tm), cols]
        mixed.append((_dot(pooled.astype(BF16), wpg_ref[g]) * pscale_ref[:, cols]).astype(BF16))
    y_pool = _dot(jnp.concatenate(mixed, axis=1), wbp_ref[...])

    y_att = _dot(att_ref[0], wba_ref[...])
    merged = (gate_ref[0, :, 0:D_MODEL].astype(F32) * y_pool
              + gate_ref[0, :, D_MODEL:2 * D_MODEL].astype(F32) * y_att)
    y = _dot(merged.astype(BF16), wout_ref[...])
    x_mid = x_ref[0] + gt_m_ref[0] * _rmsnorm(y, g_post_ref[...])
    xmid_ref[0] = x_mid

    h2 = _rmsnorm(x_mid, g_pre_ref[...]) * (1.0 + sc_f_ref[0]) + sh_f_ref[0]

    h2b = h2.astype(BF16)
    h2lo = (h2 - h2b.astype(F32)).astype(BF16)
    by_token = (_dot(h2b, wrh_ref[...]) + (_dot(h2b, wrl_ref[...]) + _dot(h2lo, wrh_ref[...]))
                + br_ref[...])
    logits = jnp.concatenate([by_token[c * LANES:(c + 1) * LANES, :].T for c in range(tm // LANES)],
                             axis=1)
    neg_inf = -jnp.inf
    far = float(LANES)
    gl = logits[N_EXPERTS:N_EXPERTS + N_EXPERT_GROUPS, :]
    grow = lax.broadcasted_iota(I32, gl.shape, 0).astype(F32)
    gmax = jnp.max(gl, axis=0, keepdims=True)
    gsel = jnp.min(jnp.where(gl == gmax, grow, far), axis=0, keepdims=True)
    p_group = 1.0 / jnp.sum(jnp.exp(gl - gmax), axis=0, keepdims=True)
    erow = lax.broadcasted_iota(I32, (N_EXPERTS, tm), 0).astype(F32)
    e_lo = gsel * float(EXPERTS_PER_GROUP)
    el = jnp.where((erow >= e_lo) & (erow < e_lo + float(EXPERTS_PER_GROUP)), logits[0:N_EXPERTS, :], neg_inf)
    v1 = jnp.max(el, axis=0, keepdims=True)
    i1 = jnp.min(jnp.where(el == v1, erow, far), axis=0, keepdims=True)
    el2 = jnp.where(erow == i1, neg_inf, el)
    v2 = jnp.max(el2, axis=0, keepdims=True)
    i2 = jnp.min(jnp.where(el2 == v2, erow, far), axis=0, keepdims=True)
    e21 = jnp.exp(v2 - v1)
    w1 = p_group / (1.0 + e21)
    w2 = p_group * e21 / (1.0 + e21)

    pick1 = erow == i1
    pick2 = erow == i2
    assign = jnp.where(pick1 | pick2, 1.0, 0.0)
    before = (lax.broadcasted_iota(I32, (tm, tm), 0) < lax.broadcasted_iota(I32, (tm, tm), 1))
    rank = _dot(assign.astype(BF16), jnp.where(before, 1.0, 0.0).astype(BF16))
    assign_pad = jnp.concatenate([assign, jnp.zeros((LANES - N_EXPERTS, tm), F32)], axis=0).astype(BF16)
    cnt_row = _dot_nt(jnp.ones((SUBLANES, tm), BF16), assign_pad)
    e_col = lax.broadcasted_iota(I32, (N_EXPERTS, LANES), 0)
    e_lane = lax.broadcasted_iota(I32, (N_EXPERTS, LANES), 1)
    run_start = jnp.sum(jnp.where(e_lane < e_col, cnt_row[0:1, :], 0.0), axis=1, keepdims=True)
    pos = rank + run_start
    key1 = jnp.sum(jnp.where(pick1, pos, 0.0), axis=0, keepdims=True)
    key2 = jnp.sum(jnp.where(pick2, pos, 0.0), axis=0, keepdims=True)

    row8 = lax.broadcasted_iota(I32, (SUBLANES, tm), 0)
    route_ref[...] = jnp.where(row8 == 0, key1, jnp.where(row8 == 1, key2,
                               jnp.where(row8 == 2, w1, jnp.where(row8 == 3, w2, 0.0))))

    filled = fill[...]
    mrow = lax.broadcasted_iota(I32, (SUBLANES, LANES), 0)
    meta_ref[0] = jnp.where(mrow == 0, filled, jnp.where(mrow == 1, cnt_row, 0.0)).astype(I32)
    fill[...] = filled + cnt_row
    counts_ref[...] = (filled + cnt_row).astype(I32)

    srow = lax.broadcasted_iota(I32, (SORT_ROWS, tm), 0).astype(F32)
    perm = jnp.where((srow == key1) | (srow == key2), 1.0, 0.0).astype(BF16)
    xslot = xbuf.at[slot]
    for w in range(ROW_WORDS):
        pair = _dot(perm, h2b[:, 2 * w * LANES:(2 * w + 2) * LANES])
        _pack_rows(xslot, w, SORT_ROWS, pair[:, :LANES], pair[:, LANES:])

    def wait_tile(which):
        pltpu.make_async_copy(xbuf.at[which], xs_hbm.at[pl.ds(0, SORT_ROWS * ROW_WORDS), :], sem).wait()

    @pl.when(step > 0)
    def _():
        wait_tile(1 - slot)

    meta_copy = pltpu.make_async_copy(meta_ref.at[0], meta_s, sem_s)
    meta_copy.start()
    meta_copy.wait()

    def per_expert(e, local):
        n = meta_s[1, e]
        _run_copies(n, xslot, local, xs_hbm, e * region + meta_s[0, e], sem)
        return local + n

    lax.fori_loop(0, N_EXPERTS, per_expert, 0)

    @pl.when(step == last)
    def _():
        wait_tile(slot)
        zbuf[...] = jnp.zeros_like(zbuf)

        def pad_copy(e):
            end = e * region + meta_s[0, e] + meta_s[1, e]
            return pltpu.make_async_copy(
                zbuf, xs_hbm.at[pl.ds(end * ROW_WORDS, MOE_BLOCK * ROW_WORDS), :], sem)

        def start_pad(e, carry):
            pad_copy(e).start()
            return carry

        def wait_pad(e, carry):
            pad_copy(e).wait()
            return carry

        lax.fori_loop(0, N_EXPERTS, start_pad, 0)
        lax.fori_loop(0, N_EXPERTS, wait_pad, 0)


def _mixtail(u, att, gates, x, gt_m, sc_f, sh_f, g_post, g_pre,
             wpg, pscale, wbp, wba, wout, wr_hi, wr_lo, br):
    b, s, d = x.shape
    tm = ROW_TILE
    tiles = s // tm
    n_tiles = b * tiles
    halo_blocks = tm // POOL_HALO
    region = _region_rows(b * s)
    const2 = lambda bi, i: (0, 0)
    const3 = lambda bi, i: (0, 0, 0)
    per_b = lambda bi, i: (bi, 0, 0)
    tile = lambda bi, i: (bi, i, 0)
    single = dict(pipeline_mode=pl.Buffered(1))
    return pl.pallas_call(
        _mixtail_kernel,
        grid=(b, tiles),
        in_specs=[pl.BlockSpec((1, tm, POOL_WIDTH), tile),
                  pl.BlockSpec((1, POOL_HALO, POOL_WIDTH),
                               lambda bi, i: (bi, jnp.maximum(i * halo_blocks - 1, 0), 0)),
                  pl.BlockSpec((1, tm, ATT_OUT_WIDTH), tile),
                  pl.BlockSpec((1, tm, 2 * D_MODEL), tile),
                  pl.BlockSpec((1, tm, d), tile),
                  pl.BlockSpec((1, 1, d), per_b),
                  pl.BlockSpec((1, 1, d), per_b),
                  pl.BlockSpec((1, 1, d), per_b),
                  pl.BlockSpec((1, d), const2),
                  pl.BlockSpec((1, d), const2),
                  pl.BlockSpec(wpg.shape, const3, **single),
                  pl.BlockSpec((1, POOL_WIDTH), const2),
                  pl.BlockSpec(wbp.shape, const2, **single),
                  pl.BlockSpec(wba.shape, const2, **single),
                  pl.BlockSpec(wout.shape, const2, **single),
                  pl.BlockSpec(wr_hi.shape, const2, **single),
                  pl.BlockSpec(wr_lo.shape, const2, **single),
                  pl.BlockSpec(br.shape, const2)],
        out_specs=[pl.BlockSpec((1, tm, d), tile),
                   pl.BlockSpec((SUBLANES, tm), lambda bi, i: (0, bi * tiles + i)),
                   pl.BlockSpec((1, SUBLANES, LANES), lambda bi, i: (bi * tiles + i, 0, 0)),
                   pl.BlockSpec((SUBLANES, LANES), const2),
                   pl.BlockSpec(memory_space=pl.ANY)],
        out_shape=[jax.ShapeDtypeStruct((b, s, d), F32),
                   jax.ShapeDtypeStruct((SUBLANES, b * s), F32),
                   jax.ShapeDtypeStruct((n_tiles, SUBLANES, LANES), I32),
                   jax.ShapeDtypeStruct((SUBLANES, LANES), I32),
                   jax.ShapeDtypeStruct((N_EXPERTS * region * ROW_WORDS, LANES), I32)],
        scratch_shapes=[pltpu.VMEM((POOL_HALO + tm, POOL_WIDTH), F32),
                        pltpu.VMEM((2, SORT_ROWS * ROW_WORDS, LANES), I32),
                        pltpu.VMEM((MOE_BLOCK * ROW_WORDS, LANES), I32),
                        pltpu.VMEM((SUBLANES, LANES), F32),
                        pltpu.SMEM((SUBLANES, LANES), I32),
                        pltpu.SemaphoreType.DMA(()),
                        pltpu.SemaphoreType.DMA(())],
        compiler_params=pltpu.CompilerParams(
            dimension_semantics=("arbitrary", "arbitrary"), vmem_limit_bytes=VMEM_LIMIT),
        name="mixtail",
    )(u, u, att, gates, x, gt_m, sc_f, sh_f, g_post, g_pre,
      wpg, pscale, wbp, wba, wout, wr_hi, wr_lo, br)


def _expert_kernel(brow_ref, first_ref, nused_ref, x_ref, wg_ref, wu_ref, wd_ref, y_ref,
                   wg_bf, wu_bf, wd_bf):
    del brow_ref
    blk = pl.program_id(0)
    bm = MOE_BLOCK

    @pl.when(first_ref[blk] == 1)
    def _():
        wg_bf[...] = wg_ref[...].astype(BF16)
        wu_bf[...] = wu_ref[...].astype(BF16)
        wd_bf[...] = wd_ref[...].astype(BF16)

    @pl.when(blk < nused_ref[0])
    def _():
        x = _unpack_rows(x_ref, bm).astype(BF16)
        a = _dot(x, wg_bf[...])
        u = _dot(x, wu_bf[...])
        mid = ((a * _sigmoid(a)) * u).astype(BF16)
        for w in range(ROW_WORDS):
            pair = _bf16_exact(_dot(mid, wd_bf[:, 2 * w * LANES:(2 * w + 2) * LANES]))
            _pack_rows(y_ref, w, bm, pair[:, :LANES], pair[:, LANES:])


def _experts(xs, block_row, block_expert, first, n_used, w_gate, w_up, w_down):
    bm = MOE_BLOCK
    n_blocks = block_row.shape[0]
    rows = pl.BlockSpec((bm * ROW_WORDS, LANES), lambda blk, br, be, fi, nu: (br[blk], 0))
    w_in = pl.BlockSpec((None, D_MODEL, D_EXPERT), lambda blk, br, be, fi, nu: (be[blk], 0, 0))
    w_out = pl.BlockSpec((None, D_EXPERT, D_MODEL), lambda blk, br, be, fi, nu: (be[blk], 0, 0))

    def body(brow_ref, bexp_ref, first_ref, nused_ref, *refs):
        del bexp_ref
        _expert_kernel(brow_ref, first_ref, nused_ref, *refs)

    grid_spec = pltpu.PrefetchScalarGridSpec(
        num_scalar_prefetch=4,
        grid=(n_blocks,),
        in_specs=[rows, w_in, w_in, w_out],
        out_specs=rows,
        scratch_shapes=[pltpu.VMEM((D_MODEL, D_EXPERT), BF16),
                        pltpu.VMEM((D_MODEL, D_EXPERT), BF16),
                        pltpu.VMEM((D_EXPERT, D_MODEL), BF16)],
    )
    return pl.pallas_call(
        body,
        grid_spec=grid_spec,
        out_shape=jax.ShapeDtypeStruct(xs.shape, I32),
        compiler_params=pltpu.CompilerParams(
            dimension_semantics=("arbitrary",), vmem_limit_bytes=VMEM_LIMIT),
        name="experts",
    )(block_row, block_expert, first, n_used, xs, w_gate, w_up, w_down)


def _block_tables(counts, n_tokens):
    bm = MOE_BLOCK
    n_blocks = 2 * n_tokens // bm + N_EXPERTS
    blocks_per_region = _region_rows(n_tokens) // bm
    nb = (counts + bm - 1) // bm
    nb_end = jnp.cumsum(nb)
    n_used = nb_end[-1]
    blk = jnp.arange(n_blocks, dtype=I32)
    live = jnp.minimum(blk, n_used - 1)
    expert = jnp.minimum(jnp.sum((live[:, None] >= nb_end[None, :]).astype(I32), axis=1), N_EXPERTS - 1)
    k = live - (nb_end[expert] - nb[expert])
    block_row = (expert * blocks_per_region + k).astype(I32)
    first = ((k == 0) & (blk < n_used)).astype(I32)
    return block_row, expert, first, n_used.reshape(1).astype(I32)


def _combine_kernel(meta_ref, y_hbm, route_ref, xmid_ref, gt_ref, g_ref, o_ref, ybuf, rt_scr, sem):
    step = pl.program_id(0)
    n_steps = pl.num_programs(0)
    tm = ROW_TILE
    slot = step % 2
    region = _region_rows(n_steps * tm)

    def fetch_tile(tile, which):
        def per_expert(e, local):
            filled = meta_ref[(tile * 2) * N_EXPERTS + e]
            n = meta_ref[(tile * 2 + 1) * N_EXPERTS + e]
            _run_copies(n, y_hbm, e * region + filled, ybuf.at[which], local, sem)
            return local + n
        lax.fori_loop(0, N_EXPERTS, per_expert, 0)

    @pl.when(step == 0)
    def _():
        fetch_tile(0, 0)

    pltpu.make_async_copy(y_hbm.at[pl.ds(0, SORT_ROWS * ROW_WORDS), :], ybuf.at[slot], sem).wait()

    @pl.when(step + 1 < n_steps)
    def _():
        fetch_tile(step + 1, 1 - slot)

    rt_scr[...] = jnp.zeros_like(rt_scr)
    rt_scr[0:SUBLANES, :] = route_ref[...]
    cols = jnp.concatenate([rt_scr[:, c * LANES:(c + 1) * LANES].T for c in range(tm // LANES)], axis=0)
    key1, key2, w1, w2 = cols[:, 0:1], cols[:, 1:2], cols[:, 2:3], cols[:, 3:4]

    yb = _unpack_rows(ybuf.at[slot], SORT_ROWS).astype(BF16)
    spos = lax.broadcasted_iota(I32, (tm, SORT_ROWS), 1).astype(F32)
    y1 = _dot(jnp.where(spos == key1, 1.0, 0.0).astype(BF16), yb)
    y2 = _dot(jnp.where(spos == key2, 1.0, 0.0).astype(BF16), yb)
    y = w1 * y1 + w2 * y2
    o_ref[...] = xmid_ref[...] + gt_ref[0] * _rmsnorm(y, g_ref[...])


def _combine(meta, ys, route, x_mid, gt_f, g_post, seq):
    t, d = x_mid.shape
    tm = ROW_TILE
    tiles_per_seq = seq // tm
    grid_spec = pltpu.PrefetchScalarGridSpec(
        num_scalar_prefetch=1,
        grid=(t // tm,),
        in_specs=[pl.BlockSpec(memory_space=pl.ANY),
                  pl.BlockSpec((SUBLANES, tm), lambda i, m: (0, i)),
                  pl.BlockSpec((tm, d), lambda i, m: (i, 0)),
                  pl.BlockSpec((1, 1, d), lambda i, m: (i // tiles_per_seq, 0, 0)),
                  pl.BlockSpec((1, d), lambda i, m: (0, 0))],
        out_specs=pl.BlockSpec((tm, d), lambda i, m: (i, 0)),
        scratch_shapes=[pltpu.VMEM((2, SORT_ROWS * ROW_WORDS, LANES), I32),
                        pltpu.VMEM((LANES, tm), F32),
                        pltpu.SemaphoreType.DMA(())],
    )
    return pl.pallas_call(
        _combine_kernel,
        grid_spec=grid_spec,
        out_shape=jax.ShapeDtypeStruct((t, d), F32),
        compiler_params=pltpu.CompilerParams(
            dimension_semantics=("arbitrary",), vmem_limit_bytes=VMEM_LIMIT),
        name="combine",
    )(meta, ys, route, x_mid, gt_f, g_post)


def kernel(x, c, w_ada, b_ada, g_pre_mix, g_post_mix, g_pre_ffn, g_post_ffn, w_in, w_pool_group, pool_scale, w_branch_pool, w_branch_att, w_out, w_group_router, b_group_router, w_expert_router, b_expert_router, w_exp_gate, w_exp_up, w_exp_down):
    b, s, d = x.shape
    assert d == D_MODEL and s % (ATT_BLOCK * 2 * ATT_DILATIONS[2]) == 0 and s % ROW_TILE == 0
    assert (2 * b * s) % MOE_BLOCK == 0
    depth = w_ada.shape[0]
    slopes = jnp.exp2(-ALIBI_MAX_BIAS * jnp.arange(1, N_ATT_HEADS + 1, dtype=F32) / N_ATT_HEADS)
    q_scale = HEAD_DIM ** -0.5

    for layer in range(depth):
        mod = _adaln(c, w_ada[layer], b_ada[layer]).reshape(b, 6, 1, d)
        sh_m, sc_m, gt_m, sh_f, sc_f, gt_f = [mod[:, j] for j in range(6)]

        wl = w_in[layer]
        q_lo, k_lo, v_lo = POOL_WIDTH, POOL_WIDTH + 768, POOL_WIDTH + 2 * 768
        group_cols = []
        for g in range(3):
            sl = slice(g * 256, (g + 1) * 256)
            group_cols += [wl[:, q_lo:k_lo][:, sl] * q_scale, wl[:, k_lo:v_lo][:, sl],
                           wl[:, v_lo:v_lo + 768][:, sl]]
        w_perm = jnp.concatenate([wl[:, :POOL_WIDTH]] + group_cols + [wl[:, v_lo + 768:]],
                                 axis=1).astype(BF16)

        u, qkv0, qkv1, qkv2, gates = _inproj(x, g_pre_mix[layer].reshape(1, d), sc_m, sh_m, w_perm)
        att = _attention(qkv0, qkv1, qkv2, slopes)

        pad_cols = LANES - N_EXPERTS - N_EXPERT_GROUPS
        wr = jnp.concatenate([w_expert_router[layer], w_group_router[layer],
                              jnp.zeros((d, pad_cols), F32)], axis=1)
        wr_hi = wr.astype(BF16)
        wr_lo = (wr - wr_hi.astype(F32)).astype(BF16)
        br = jnp.concatenate([b_expert_router[layer], b_group_router[layer],
                              jnp.zeros((pad_cols,), F32)]).reshape(1, LANES)

        x_mid, route, meta, counts, xs = _mixtail(
            u, att, gates, x, gt_m, sc_f, sh_f,
            g_post_mix[layer].reshape(1, d), g_pre_ffn[layer].reshape(1, d),
            w_pool_group[layer].astype(BF16), pool_scale[layer].reshape(1, POOL_WIDTH),
            w_branch_pool[layer].astype(BF16), w_branch_att[layer].astype(BF16),
            w_out[layer].astype(BF16), wr_hi, wr_lo, br)

        block_row, block_expert, first, n_used = _block_tables(counts[0, :N_EXPERTS], b * s)
        ys = _experts(xs, block_row, block_expert, first, n_used,
                      w_exp_gate[layer], w_exp_up[layer], w_exp_down[layer])
        run_meta = meta[:, 0:2, 0:N_EXPERTS].reshape(-1)
        x = _combine(run_meta, ys, route, x_mid.reshape(b * s, d), gt_f,
                     g_post_ffn[layer].reshape(1, d), s).reshape(b, s, d)
    return x
```

```python
import functools

import jax
import jax.numpy as jnp
from jax import lax
from jax.experimental import pallas as pl
from jax.experimental.pallas import tpu as pltpu

F32 = jnp.float32
BF16 = jnp.bfloat16
I32 = jnp.int32
HIGHEST = lax.Precision.HIGHEST

D_MODEL = 1024
LANES = 128
SUBLANES = 8
ROW_WORDS = D_MODEL // (2 * LANES)
HI_HALF = -65536

POOL_WINDOWS = (2, 4, 8, 16)
POOL_GROUP_DIM = 128
POOL_WIDTH = 512
POOL_HALO = 16

HEAD_DIM = 64
ATT_DILATIONS = (1, 4, 16)
ATT_REACH = 128
ATT_BLOCK = 128
ATT_UNROLL = 4
HEADS_PER_GROUP = 4
N_ATT_HEADS = 12
GROUP_QKV = 3 * HEADS_PER_GROUP * HEAD_DIM
ATT_OUT_WIDTH = 256
ALIBI_MAX_BIAS = 8.0
IN_WIDTH = POOL_WIDTH + 3 * GROUP_QKV + 2 * D_MODEL
MASKED = -1e30

N_EXPERT_GROUPS = 4
EXPERTS_PER_GROUP = 8
N_EXPERTS = 32
D_EXPERT = 512
RMS_EPS = 1e-6

ROW_TILE = 512
SORT_ROWS = 2 * ROW_TILE
RUN_CHUNK = 64
MOE_BLOCK = 256
VMEM_LIMIT = 52 * 1024 * 1024


def _sigmoid(x):
    return 0.5 * jnp.tanh(0.5 * x) + 0.5


def _rmsnorm(x, g):
    return x * lax.rsqrt(jnp.mean(x * x, axis=-1, keepdims=True) + RMS_EPS) * g


def _dot(a, b):
    return jnp.dot(a, b, preferred_element_type=F32)


def _dot_nt(a, b, **kw):
    return lax.dot_general(a, b, (((1,), (1,)), ((), ())), preferred_element_type=F32, **kw)


def _pack_rows(ref, w, n, lo, hi):
    word = (lax.shift_right_logical(pltpu.bitcast(lo, I32), 16) | (pltpu.bitcast(hi, I32) & HI_HALF))
    ref[pl.ds(w, n, stride=ROW_WORDS), :] = word


def _unpack_rows(ref, n):
    cols = []
    for w in range(ROW_WORDS):
        word = ref[pl.ds(w, n, stride=ROW_WORDS), :]
        cols += [pltpu.bitcast(word << 16, F32), pltpu.bitcast(word & HI_HALF, F32)]
    return jnp.concatenate(cols, axis=1)


def _bf16_exact(x):
    return x.astype(BF16).astype(F32)


def _region_rows(n_tokens):
    return n_tokens + MOE_BLOCK


def _run_copies(n, src, src_row, dst, dst_row, sem):
    def piece(offset, size):
        return pltpu.make_async_copy(
            src.at[pl.ds((src_row + offset) * ROW_WORDS, size * ROW_WORDS), :],
            dst.at[pl.ds((dst_row + offset) * ROW_WORDS, size * ROW_WORDS), :], sem)

    def whole_chunk(c, carry):
        piece(c * RUN_CHUNK, RUN_CHUNK).start()
        return carry

    lax.fori_loop(0, n // RUN_CHUNK, whole_chunk, 0)
    for bit in range(RUN_CHUNK.bit_length() - 1):
        size = 1 << bit
        above = n & ~(2 * size - 1)

        @pl.when((n & size) != 0)
        def _():
            piece(above, size).start()


def _adaln_kernel(c_ref, w_ref, b_ref, o_ref):
    c = c_ref[...]
    a = c * _sigmoid(c)
    o_ref[...] = jnp.dot(a, w_ref[...], preferred_element_type=F32, precision=HIGHEST) + b_ref[...]


def _adaln(c, w_ada, b_ada):
    b, d = c.shape
    n = w_ada.shape[1]
    rows = -(-b // SUBLANES) * SUBLANES
    cp = jnp.pad(c, ((0, rows - b), (0, 0)))
    nt = 1536
    out = pl.pallas_call(
        _adaln_kernel,
        grid=(n // nt,),
        in_specs=[pl.BlockSpec((rows, d), lambda j: (0, 0)),
                  pl.BlockSpec((d, nt), lambda j: (0, j)),
                  pl.BlockSpec((1, nt), lambda j: (0, j))],
        out_specs=pl.BlockSpec((rows, nt), lambda j: (0, j)),
        out_shape=jax.ShapeDtypeStruct((rows, n), F32),
        compiler_params=pltpu.CompilerParams(vmem_limit_bytes=VMEM_LIMIT),
        name="adaln",
    )(cp, w_ada, b_ada.reshape(1, n))
    return out[:b]


def _inproj_kernel(x_ref, g_ref, sc_ref, sh_ref, w_ref,
                   u_ref, qkv0_ref, qkv1_ref, qkv2_ref, gate_ref, h_scr, p_scr):
    tm = x_ref.shape[1]
    h = _rmsnorm(x_ref[0], g_ref[...]) * (1.0 + sc_ref[0]) + sh_ref[0]
    h_scr[...] = h.astype(BF16)

    u_ref[0] = _dot(h_scr[...], w_ref[:, 0:POOL_WIDTH]).astype(BF16)

    col = POOL_WIDTH
    qkv0_ref[0, 0] = _dot(h_scr[...], w_ref[:, col:col + GROUP_QKV]).astype(BF16)
    for out_ref, d in ((qkv1_ref, ATT_DILATIONS[1]), (qkv2_ref, ATT_DILATIONS[2])):
        col += GROUP_QKV
        proj = _dot(h_scr[...], w_ref[:, col:col + GROUP_QKV])
        for cb in range(GROUP_QKV // LANES):
            p_scr[cb] = proj[:, cb * LANES:(cb + 1) * LANES]
        for r in range(d):
            out_ref[0, r] = jnp.concatenate(
                [p_scr[cb, pl.ds(r, tm // d, stride=d), :] for cb in range(GROUP_QKV // LANES)],
                axis=1).astype(BF16)
    col += GROUP_QKV

    chunk = 512
    for j in range(2 * D_MODEL // chunk):
        g = _dot(h_scr[...], w_ref[:, col + j * chunk:col + (j + 1) * chunk])
        gate_ref[0, :, j * chunk:(j + 1) * chunk] = _sigmoid(g).astype(BF16)


def _inproj(x, g_pre, sc, sh, w_perm):
    b, s, d = x.shape
    tm = ROW_TILE
    d1, d2 = ATT_DILATIONS[1], ATT_DILATIONS[2]
    grid = (b, s // tm)
    const = lambda bi, i: (0, 0)
    per_b = lambda bi, i: (bi, 0, 0)
    return pl.pallas_call(
        _inproj_kernel,
        grid=grid,
        in_specs=[pl.BlockSpec((1, tm, d), lambda bi, i: (bi, i, 0)),
                  pl.BlockSpec((1, d), const),
                  pl.BlockSpec((1, 1, d), per_b),
                  pl.BlockSpec((1, 1, d), per_b),
                  pl.BlockSpec((d, IN_WIDTH), const, pipeline_mode=pl.Buffered(1))],
        out_specs=[pl.BlockSpec((1, tm, POOL_WIDTH), lambda bi, i: (bi, i, 0)),
                   pl.BlockSpec((1, 1, tm, GROUP_QKV), lambda bi, i: (bi, 0, i, 0)),
                   pl.BlockSpec((1, d1, tm // d1, GROUP_QKV), lambda bi, i: (bi, 0, i, 0)),
                   pl.BlockSpec((1, d2, tm // d2, GROUP_QKV), lambda bi, i: (bi, 0, i, 0)),
                   pl.BlockSpec((1, tm, 2 * D_MODEL), lambda bi, i: (bi, i, 0))],
        out_shape=[jax.ShapeDtypeStruct((b, s, POOL_WIDTH), BF16),
                   jax.ShapeDtypeStruct((b, 1, s, GROUP_QKV), BF16),
                   jax.ShapeDtypeStruct((b, d1, s // d1, GROUP_QKV), BF16),
                   jax.ShapeDtypeStruct((b, d2, s // d2, GROUP_QKV), BF16),
                   jax.ShapeDtypeStruct((b, s, 2 * D_MODEL), BF16)],
        scratch_shapes=[pltpu.VMEM((tm, d), BF16), pltpu.VMEM((GROUP_QKV // LANES, tm, LANES), F32)],
        compiler_params=pltpu.CompilerParams(
            dimension_semantics=("arbitrary", "arbitrary"), vmem_limit_bytes=VMEM_LIMIT),
        name="inproj",
    )(x, g_pre, sc, sh, w_perm)


def _attn_kernel(slopes_ref, q0, k0, v0, q1, k1, v1, q2, k2, v2, o_ref,
                 bias_scr, acc, mrun, lrun):
    pair = pl.program_id(1)
    seq = o_ref.shape[1]
    nblk = seq // ATT_BLOCK
    lane = lax.broadcasted_iota(I32, (ATT_BLOCK, LANES), 1)
    first_head = lane < HEAD_DIM
    half_lane = lax.broadcasted_iota(I32, (ATT_BLOCK // 2, LANES), 1)
    head_bits = (jnp.where(half_lane < HEAD_DIM, -1, 0), jnp.where(half_lane < HEAD_DIM, 0, -1))

    qi = lax.broadcasted_iota(I32, (ATT_BLOCK, 2 * ATT_BLOCK), 0)
    kj = lax.broadcasted_iota(I32, (ATT_BLOCK, 2 * ATT_BLOCK), 1)
    delta = ATT_BLOCK + qi - kj
    valid = (delta >= 0) & (delta <= ATT_REACH)
    delta0 = qi - kj
    valid0 = delta0 >= 0
    for g, d in enumerate(ATT_DILATIONS):
        for j in range(2):
            slope = slopes_ref[g * HEADS_PER_GROUP + 2 * pair + j]
            bias_scr[g, j, 0] = jnp.where(valid0, -slope * (delta0 * d).astype(F32), MASKED)
            bias_scr[g, j, 1] = jnp.where(valid, -slope * (delta * d).astype(F32), MASKED)

    def block_softmax(g, q_ref, k_ref, v_ref, r, i):
        lo = jnp.maximum(i - 1, 0)
        variant = jnp.minimum(i, 1)
        q = q_ref[0, r, pl.ds(pl.multiple_of(i * ATT_BLOCK, ATT_BLOCK), ATT_BLOCK), :]
        kw = k_ref[0, r, pl.ds(pl.multiple_of(lo * ATT_BLOCK, ATT_BLOCK), 2 * ATT_BLOCK), :]
        vw = v_ref[0, r, pl.ds(pl.multiple_of(lo * ATT_BLOCK, ATT_BLOCK), 2 * ATT_BLOCK), :]
        qbits = pltpu.bitcast(q, I32)
        outs = []
        for j in range(2):
            qh = pltpu.bitcast(qbits & head_bits[j], BF16)
            sc = _dot_nt(qh, kw) + bias_scr[g, j, variant]
            mx = jnp.max(sc, axis=-1, keepdims=True)
            p = jnp.exp(sc - mx)
            den = jnp.sum(p, axis=-1, keepdims=True)
            num = _dot(p.astype(BF16), vw)
            outs.append((num, mx, den))
        (n0, m0, l0), (n1, m1, l1) = outs
        return (jnp.where(first_head, n0, n1),
                jnp.where(first_head, m0, m1),
                jnp.where(first_head, l0, l1))

    def first_group(it, carry):
        blocks = [it * ATT_UNROLL + k for k in range(ATT_UNROLL)]
        results = [block_softmax(0, q0, k0, v0, 0, n) for n in blocks]
        for n, (num, mx, den) in zip(blocks, results):
            rows = pl.ds(pl.multiple_of(n * ATT_BLOCK, ATT_BLOCK), ATT_BLOCK)
            acc[rows, :] = num
            mrun[rows, :] = mx
            lrun[rows, :] = den
        return carry

    lax.fori_loop(0, nblk // ATT_UNROLL, first_group, 0)

    def merge_group(g, q_ref, k_ref, v_ref):
        d = ATT_DILATIONS[g]
        per_res = nblk // d

        def body(it, carry):
            blocks = [it * ATT_UNROLL + k for k in range(ATT_UNROLL)]
            results = [block_softmax(g, q_ref, k_ref, v_ref, n // per_res, n % per_res) for n in blocks]
            for n, (num, mx, den) in zip(blocks, results):
                r = n // per_res
                i = n % per_res
                rows = pl.ds(i * (ATT_BLOCK * d) + r, ATT_BLOCK, stride=d)
                m_old = mrun[rows, :]
                m_new = jnp.maximum(m_old, mx)
                a = jnp.exp(m_old - m_new)
                c = jnp.exp(mx - m_new)
                acc[rows, :] = a * acc[rows, :] + c * num
                lrun[rows, :] = a * lrun[rows, :] + c * den
                mrun[rows, :] = m_new
            return carry

        lax.fori_loop(0, nblk // ATT_UNROLL, body, 0)

    merge_group(1, q1, k1, v1)
    merge_group(2, q2, k2, v2)

    def finish(n, carry):
        rows = pl.ds(pl.multiple_of(n * ATT_BLOCK, ATT_BLOCK), ATT_BLOCK)
        o_ref[0, rows, :] = (acc[rows, :] / lrun[rows, :]).astype(BF16)
        return carry

    lax.fori_loop(0, nblk, finish, 0)


def _attention(qkv0, qkv1, qkv2, slopes):
    b, _, s, _ = qkv0.shape
    pairs = HEADS_PER_GROUP // 2
    col_blocks = HEADS_PER_GROUP * HEAD_DIM // LANES

    def specs(arr):
        _, d, sub, _ = arr.shape
        return [pl.BlockSpec((1, d, sub, LANES),
                             functools.partial(lambda bi, p, sec: (bi, 0, 0, sec * col_blocks + p), sec=sec))
                for sec in range(3)]

    return pl.pallas_call(
        _attn_kernel,
        grid=(b, pairs),
        in_specs=[pl.BlockSpec(memory_space=pltpu.SMEM)] + specs(qkv0) + specs(qkv1) + specs(qkv2),
        out_specs=pl.BlockSpec((1, s, LANES), lambda bi, p: (bi, 0, p)),
        out_shape=jax.ShapeDtypeStruct((b, s, ATT_OUT_WIDTH), BF16),
        scratch_shapes=[pltpu.VMEM((3, 2, 2, ATT_BLOCK, 2 * ATT_BLOCK), F32),
                        pltpu.VMEM((s, LANES), F32),
                        pltpu.VMEM((s, LANES), F32),
                        pltpu.VMEM((s, LANES), F32)],
        compiler_params=pltpu.CompilerParams(
            dimension_semantics=("arbitrary", "arbitrary"), vmem_limit_bytes=VMEM_LIMIT),
        name="attention",
    )(slopes, qkv0, qkv0, qkv0, qkv1, qkv1, qkv1, qkv2, qkv2, qkv2)


def _mixtail_kernel(u_ref, halo_ref, att_ref, gate_ref, x_ref,
                    gt_m_ref, sc_f_ref, sh_f_ref, g_post_ref, g_pre_ref,
                    wpg_ref, pscale_ref, wbp_ref, wba_ref, wout_ref, wrh_ref, wrl_ref, br_ref,
                    xmid_ref, route_ref, meta_ref, counts_ref, xs_hbm,
                    pu, xbuf, zbuf, fill, meta_s, sem, sem_s):
    i = pl.program_id(1)
    tm = x_ref.shape[1]
    step = pl.program_id(0) * pl.num_programs(1) + i
    last = pl.num_programs(0) * pl.num_programs(1) - 1
    slot = step % 2
    region = _region_rows(pl.num_programs(0) * pl.num_programs(1) * tm)

    @pl.when(step == 0)
    def _():
        fill[...] = jnp.zeros_like(fill)

    halo = halo_ref[0].astype(F32)
    pu[0:POOL_HALO, :] = jnp.where(i > 0, halo, jnp.zeros_like(halo))
    pu[POOL_HALO:POOL_HALO + tm, :] = u_ref[0].astype(F32)
    t = i * tm + lax.broadcasted_iota(I32, (tm, 1), 0)
    mixed = []
    for g, w in enumerate(POOL_WINDOWS):
        cols = slice(g * POOL_GROUP_DIM, (g + 1) * POOL_GROUP_DIM)
        win = pu[pl.ds(POOL_HALO, tm), cols]
        for j in range(1, w):
            win = win + pu[pl.ds(POOL_HALO - j, tm), cols]
        count = jnp.minimum(t + 1, w).astype(F32)
        pooled = win / count - pu[pl.ds(POOL_HALO, tm), cols]
        mixed.append((_dot(pooled.astype(BF16), wpg_ref[g]) * pscale_ref[:, cols]).astype(BF16))
    y_pool = _dot(jnp.concatenate(mixed, axis=1), wbp_ref[...])

    y_att = _dot(att_ref[0], wba_ref[...])
    merged = (gate_ref[0, :, 0:D_MODEL].astype(F32) * y_pool
              + gate_ref[0, :, D_MODEL:2 * D_MODEL].astype(F32) * y_att)
    y = _dot(merged.astype(BF16), wout_ref[...])
    x_mid = x_ref[0] + gt_m_ref[0] * _rmsnorm(y, g_post_ref[...])
    xmid_ref[0] = x_mid

    h2 = _rmsnorm(x_mid, g_pre_ref[...]) * (1.0 + sc_f_ref[0]) + sh_f_ref[0]

    h2b = h2.astype(BF16)
    h2lo = (h2 - h2b.astype(F32)).astype(BF16)
    by_token = (_dot(h2b, wrh_ref[...]) + (_dot(h2b, wrl_ref[...]) + _dot(h2lo, wrh_ref[...]))
                + br_ref[...])
    logits = jnp.concatenate([by_token[c * LANES:(c + 1) * LANES, :].T for c in range(tm // LANES)],
                             axis=1)
    neg_inf = -jnp.inf
    far = float(LANES)
    gl = logits[N_EXPERTS:N_EXPERTS + N_EXPERT_GROUPS, :]
    grow = lax.broadcasted_iota(I32, gl.shape, 0).astype(F32)
    gmax = jnp.max(gl, axis=0, keepdims=True)
    gsel = jnp.min(jnp.where(gl == gmax, grow, far), axis=0, keepdims=True)
    p_group = 1.0 / jnp.sum(jnp.exp(gl - gmax), axis=0, keepdims=True)
    erow = lax.broadcasted_iota(I32, (N_EXPERTS, tm), 0).astype(F32)
    e_lo = gsel * float(EXPERTS_PER_GROUP)
    el = jnp.where((erow >= e_lo) & (erow < e_lo + float(EXPERTS_PER_GROUP)), logits[0:N_EXPERTS, :], neg_inf)
    v1 = jnp.max(el, axis=0, keepdims=True)
    i1 = jnp.min(jnp.where(el == v1, erow, far), axis=0, keepdims=True)
    el2 = jnp.where(erow == i1, neg_inf, el)
    v2 = jnp.max(el2, axis=0, keepdims=True)
    i2 = jnp.min(jnp.where(el2 == v2, erow, far), axis=0, keepdims=True)
    e21 = jnp.exp(v2 - v1)
    w1 = p_group / (1.0 + e21)
    w2 = p_group * e21 / (1.0 + e21)

    pick1 = erow == i1
    pick2 = erow == i2
    assign = jnp.where(pick1 | pick2, 1.0, 0.0)
    before = (lax.broadcasted_iota(I32, (tm, tm), 0) < lax.broadcasted_iota(I32, (tm, tm), 1))
    rank = _dot(assign.astype(BF16), jnp.where(before, 1.0, 0.0).astype(BF16))
    assign_pad = jnp.concatenate([assign, jnp.zeros((LANES - N_EXPERTS, tm), F32)], axis=0).astype(BF16)
    cnt_row = _dot_nt(jnp.ones((SUBLANES, tm), BF16), assign_pad)
    e_col = lax.broadcasted_iota(I32, (N_EXPERTS, LANES), 0)
    e_lane = lax.broadcasted_iota(I32, (N_EXPERTS, LANES), 1)
    run_start = jnp.sum(jnp.where(e_lane < e_col, cnt_row[0:1, :], 0.0), axis=1, keepdims=True)
    pos = rank + run_start
    key1 = jnp.sum(jnp.where(pick1, pos, 0.0), axis=0, keepdims=True)
    key2 = jnp.sum(jnp.where(pick2, pos, 0.0), axis=0, keepdims=True)

    row8 = lax.broadcasted_iota(I32, (SUBLANES, tm), 0)
    route_ref[...] = jnp.where(row8 == 0, key1, jnp.where(row8 == 1, key2,
                               jnp.where(row8 == 2, w1, jnp.where(row8 == 3, w2, 0.0))))

    filled = fill[...]
    mrow = lax.broadcasted_iota(I32, (SUBLANES, LANES), 0)
    meta_ref[0] = jnp.where(mrow == 0, filled, jnp.where(mrow == 1, cnt_row, 0.0)).astype(I32)
    fill[...] = filled + cnt_row
    counts_ref[...] = (filled + cnt_row).astype(I32)

    srow = lax.broadcasted_iota(I32, (SORT_ROWS, tm), 0).astype(F32)
    perm = jnp.where((srow == key1) | (srow == key2), 1.0, 0.0).astype(BF16)
    xslot = xbuf.at[slot]
    for w in range(ROW_WORDS):
        pair = _dot(perm, h2b[:, 2 * w * LANES:(2 * w + 2) * LANES])
        _pack_rows(xslot, w, SORT_ROWS, pair[:, :LANES], pair[:, LANES:])

    def wait_tile(which):
        pltpu.make_async_copy(xbuf.at[which], xs_hbm.at[pl.ds(0, SORT_ROWS * ROW_WORDS), :], sem).wait()

    @pl.when(step > 0)
    def _():
        wait_tile(1 - slot)

    meta_copy = pltpu.make_async_copy(meta_ref.at[0], meta_s, sem_s)
    meta_copy.start()
    meta_copy.wait()

    def per_expert(e, local):
        n = meta_s[1, e]
        _run_copies(n, xslot, local, xs_hbm, e * region + meta_s[0, e], sem)
        return local + n

    lax.fori_loop(0, N_EXPERTS, per_expert, 0)

    @pl.when(step == last)
    def _():
        wait_tile(slot)
        zbuf[...] = jnp.zeros_like(zbuf)

        def pad_copy(e):
            end = e * region + meta_s[0, e] + meta_s[1, e]
            return pltpu.make_async_copy(
                zbuf, xs_hbm.at[pl.ds(end * ROW_WORDS, MOE_BLOCK * ROW_WORDS), :], sem)

        def start_pad(e, carry):
            pad_copy(e).start()
            return carry

        def wait_pad(e, carry):
            pad_copy(e).wait()
            return carry

        lax.fori_loop(0, N_EXPERTS, start_pad, 0)
        lax.fori_loop(0, N_EXPERTS, wait_pad, 0)


def _mixtail(u, att, gates, x, gt_m, sc_f, sh_f, g_post, g_pre,
             wpg, pscale, wbp, wba, wout, wr_hi, wr_lo, br):
    b, s, d = x.shape
    tm = ROW_TILE
    tiles = s // tm
    n_tiles = b * tiles
    halo_blocks = tm // POOL_HALO
    region = _region_rows(b * s)
    const2 = lambda bi, i: (0, 0)
    const3 = lambda bi, i: (0, 0, 0)
    per_b = lambda bi, i: (bi, 0, 0)
    tile = lambda bi, i: (bi, i, 0)
    single = dict(pipeline_mode=pl.Buffered(1))
    return pl.pallas_call(
        _mixtail_kernel,
        grid=(b, tiles),
        in_specs=[pl.BlockSpec((1, tm, POOL_WIDTH), tile),
                  pl.BlockSpec((1, POOL_HALO, POOL_WIDTH),
                               lambda bi, i: (bi, jnp.maximum(i * halo_blocks - 1, 0), 0)),
                  pl.BlockSpec((1, tm, ATT_OUT_WIDTH), tile),
                  pl.BlockSpec((1, tm, 2 * D_MODEL), tile),
                  pl.BlockSpec((1, tm, d), tile),
                  pl.BlockSpec((1, 1, d), per_b),
                  pl.BlockSpec((1, 1, d), per_b),
                  pl.BlockSpec((1, 1, d), per_b),
                  pl.BlockSpec((1, d), const2),
                  pl.BlockSpec((1, d), const2),
                  pl.BlockSpec(wpg.shape, const3, **single),
                  pl.BlockSpec((1, POOL_WIDTH), const2),
                  pl.BlockSpec(wbp.shape, const2, **single),
                  pl.BlockSpec(wba.shape, const2, **single),
                  pl.BlockSpec(wout.shape, const2, **single),
                  pl.BlockSpec(wr_hi.shape, const2, **single),
                  pl.BlockSpec(wr_lo.shape, const2, **single),
                  pl.BlockSpec(br.shape, const2)],
        out_specs=[pl.BlockSpec((1, tm, d), tile),
                   pl.BlockSpec((SUBLANES, tm), lambda bi, i: (0, bi * tiles + i)),
                   pl.BlockSpec((1, SUBLANES, LANES), lambda bi, i: (bi * tiles + i, 0, 0)),
                   pl.BlockSpec((SUBLANES, LANES), const2),
                   pl.BlockSpec(memory_space=pl.ANY)],
        out_shape=[jax.ShapeDtypeStruct((b, s, d), F32),
                   jax.ShapeDtypeStruct((SUBLANES, b * s), F32),
                   jax.ShapeDtypeStruct((n_tiles, SUBLANES, LANES), I32),
                   jax.ShapeDtypeStruct((SUBLANES, LANES), I32),
                   jax.ShapeDtypeStruct((N_EXPERTS * region * ROW_WORDS, LANES), I32)],
        scratch_shapes=[pltpu.VMEM((POOL_HALO + tm, POOL_WIDTH), F32),
                        pltpu.VMEM((2, SORT_ROWS * ROW_WORDS, LANES), I32),
                        pltpu.VMEM((MOE_BLOCK * ROW_WORDS, LANES), I32),
                        pltpu.VMEM((SUBLANES, LANES), F32),
                        pltpu.SMEM((SUBLANES, LANES), I32),
                        pltpu.SemaphoreType.DMA(()),
                        pltpu.SemaphoreType.DMA(())],
        compiler_params=pltpu.CompilerParams(
            dimension_semantics=("arbitrary", "arbitrary"), vmem_limit_bytes=VMEM_LIMIT),
        name="mixtail",
    )(u, u, att, gates, x, gt_m, sc_f, sh_f, g_post, g_pre,
      wpg, pscale, wbp, wba, wout, wr_hi, wr_lo, br)


def _expert_kernel(counts_ref, xs_hbm, wg_ref, wu_ref, wd_ref, ys_hbm,
                   wg_bf, wu_bf, wd_bf, xbuf, ybuf, state, semx, semy):
    e = pl.program_id(0)
    bm = MOE_BLOCK
    block_words = bm * ROW_WORDS
    region = xs_hbm.shape[0] // (N_EXPERTS * ROW_WORDS)

    def n_blocks(ex):
        return (counts_ref[ex] + (bm - 1)) // bm

    def x_copy(ex, k, s):
        start = pl.multiple_of((ex * region + k * bm) * ROW_WORDS, block_words)
        return pltpu.make_async_copy(xs_hbm.at[pl.ds(start, block_words), :], xbuf.at[s], semx.at[s])

    def y_copy(k, s):
        start = pl.multiple_of((e * region + k * bm) * ROW_WORDS, block_words)
        return pltpu.make_async_copy(ybuf.at[s], ys_hbm.at[pl.ds(start, block_words), :], semy.at[s])

    def fetch_first_block(ex, s):
        @pl.when(n_blocks(ex) > 0)
        def _():
            x_copy(ex, 0, s).start()

    def drain_y(s):
        @pl.when(state[1 + s] == 1)
        def _():
            y_copy(0, s).wait()
            state[1 + s] = 0

    @pl.when(e == 0)
    def _():
        state[0] = 0
        state[1] = 0
        state[2] = 0
        fetch_first_block(0, 0)

    wg_bf[...] = wg_ref[...].astype(BF16)
    wu_bf[...] = wu_ref[...].astype(BF16)
    wd_bf[...] = wd_ref[...].astype(BF16)

    nb = n_blocks(e)
    first = state[0]
    after = (first + nb) % 2

    def fetch_next_expert():
        @pl.when(e + 1 < N_EXPERTS)
        def _():
            fetch_first_block(jnp.minimum(e + 1, N_EXPERTS - 1), after)

    @pl.when(nb == 0)
    def _():
        fetch_next_expert()

    def block(k, carry):
        s = (first + k) % 2
        x_copy(e, k, s).wait()

        @pl.when(k + 1 < nb)
        def _():
            x_copy(e, k + 1, 1 - s).start()

        @pl.when(k + 1 == nb)
        def _():
            fetch_next_expert()

        x = _unpack_rows(xbuf.at[s], bm).astype(BF16)
        a = _dot(x, wg_bf[...])
        u = _dot(x, wu_bf[...])
        mid = ((a * _sigmoid(a)) * u).astype(BF16)
        drain_y(s)
        for w in range(ROW_WORDS):
            pair = _bf16_exact(_dot(mid, wd_bf[:, 2 * w * LANES:(2 * w + 2) * LANES]))
            _pack_rows(ybuf.at[s], w, bm, pair[:, :LANES], pair[:, LANES:])
        y_copy(k, s).start()
        state[1 + s] = 1
        return carry

    lax.fori_loop(0, nb, block, 0)
    state[0] = after

    @pl.when(e == N_EXPERTS - 1)
    def _():
        drain_y(0)
        drain_y(1)


def _experts(xs, counts, w_gate, w_up, w_down):
    bm = MOE_BLOCK
    w_in = pl.BlockSpec((None, D_MODEL, D_EXPERT), lambda e, cnt: (e, 0, 0))
    w_out = pl.BlockSpec((None, D_EXPERT, D_MODEL), lambda e, cnt: (e, 0, 0))
    grid_spec = pltpu.PrefetchScalarGridSpec(
        num_scalar_prefetch=1,
        grid=(N_EXPERTS,),
        in_specs=[pl.BlockSpec(memory_space=pl.ANY), w_in, w_in, w_out],
        out_specs=pl.BlockSpec(memory_space=pl.ANY),
        scratch_shapes=[pltpu.VMEM((D_MODEL, D_EXPERT), BF16),
                        pltpu.VMEM((D_MODEL, D_EXPERT), BF16),
                        pltpu.VMEM((D_EXPERT, D_MODEL), BF16),
                        pltpu.VMEM((2, bm * ROW_WORDS, LANES), I32),
                        pltpu.VMEM((2, bm * ROW_WORDS, LANES), I32),
                        pltpu.SMEM((3,), I32),
                        pltpu.SemaphoreType.DMA((2,)),
                        pltpu.SemaphoreType.DMA((2,))],
    )
    return pl.pallas_call(
        _expert_kernel,
        grid_spec=grid_spec,
        out_shape=jax.ShapeDtypeStruct(xs.shape, I32),
        compiler_params=pltpu.CompilerParams(
            dimension_semantics=("arbitrary",), vmem_limit_bytes=VMEM_LIMIT),
        name="experts",
    )(counts, xs, w_gate, w_up, w_down)


def _combine_kernel(meta_ref, y_hbm, route_ref, xmid_ref, gt_ref, g_ref, o_ref, ybuf, rt_scr, sem):
    step = pl.program_id(0)
    n_steps = pl.num_programs(0)
    tm = ROW_TILE
    slot = step % 2
    region = _region_rows(n_steps * tm)

    def fetch_tile(tile, which):
        def per_expert(e, local):
            filled = meta_ref[(tile * 2) * N_EXPERTS + e]
            n = meta_ref[(tile * 2 + 1) * N_EXPERTS + e]
            _run_copies(n, y_hbm, e * region + filled, ybuf.at[which], local, sem)
            return local + n
        lax.fori_loop(0, N_EXPERTS, per_expert, 0)

    @pl.when(step == 0)
    def _():
        fetch_tile(0, 0)

    pltpu.make_async_copy(y_hbm.at[pl.ds(0, SORT_ROWS * ROW_WORDS), :], ybuf.at[slot], sem).wait()

    @pl.when(step + 1 < n_steps)
    def _():
        fetch_tile(step + 1, 1 - slot)

    rt_scr[...] = jnp.zeros_like(rt_scr)
    rt_scr[0:SUBLANES, :] = route_ref[...]
    cols = jnp.concatenate([rt_scr[:, c * LANES:(c + 1) * LANES].T for c in range(tm // LANES)], axis=0)
    key1, key2, w1, w2 = cols[:, 0:1], cols[:, 1:2], cols[:, 2:3], cols[:, 3:4]

    yb = _unpack_rows(ybuf.at[slot], SORT_ROWS).astype(BF16)
    spos = lax.broadcasted_iota(I32, (tm, SORT_ROWS), 1).astype(F32)
    y1 = _dot(jnp.where(spos == key1, 1.0, 0.0).astype(BF16), yb)
    y2 = _dot(jnp.where(spos == key2, 1.0, 0.0).astype(BF16), yb)
    y = w1 * y1 + w2 * y2
    o_ref[...] = xmid_ref[...] + gt_ref[0] * _rmsnorm(y, g_ref[...])


def _combine(meta, ys, route, x_mid, gt_f, g_post, seq):
    t, d = x_mid.shape
    tm = ROW_TILE
    tiles_per_seq = seq // tm
    grid_spec = pltpu.PrefetchScalarGridSpec(
        num_scalar_prefetch=1,
        grid=(t // tm,),
        in_specs=[pl.BlockSpec(memory_space=pl.ANY),
                  pl.BlockSpec((SUBLANES, tm), lambda i, m: (0, i)),
                  pl.BlockSpec((tm, d), lambda i, m: (i, 0)),
                  pl.BlockSpec((1, 1, d), lambda i, m: (i // tiles_per_seq, 0, 0)),
                  pl.BlockSpec((1, d), lambda i, m: (0, 0))],
        out_specs=pl.BlockSpec((tm, d), lambda i, m: (i, 0)),
        scratch_shapes=[pltpu.VMEM((2, SORT_ROWS * ROW_WORDS, LANES), I32),
                        pltpu.VMEM((LANES, tm), F32),
                        pltpu.SemaphoreType.DMA(())],
    )
    return pl.pallas_call(
        _combine_kernel,
        grid_spec=grid_spec,
        out_shape=jax.ShapeDtypeStruct((t, d), F32),
        compiler_params=pltpu.CompilerParams(
            dimension_semantics=("arbitrary",), vmem_limit_bytes=VMEM_LIMIT),
        name="combine",
    )(meta, ys, route, x_mid, gt_f, g_post)


def kernel(x, c, w_ada, b_ada, g_pre_mix, g_post_mix, g_pre_ffn, g_post_ffn, w_in, w_pool_group, pool_scale, w_branch_pool, w_branch_att, w_out, w_group_router, b_group_router, w_expert_router, b_expert_router, w_exp_gate, w_exp_up, w_exp_down):
    b, s, d = x.shape
    assert d == D_MODEL and s % (ATT_BLOCK * 2 * ATT_DILATIONS[2]) == 0 and s % ROW_TILE == 0
    assert (b * s) % MOE_BLOCK == 0
    depth = w_ada.shape[0]
    slopes = jnp.exp2(-ALIBI_MAX_BIAS * jnp.arange(1, N_ATT_HEADS + 1, dtype=F32) / N_ATT_HEADS)
    q_scale = HEAD_DIM ** -0.5

    for layer in range(depth):
        mod = _adaln(c, w_ada[layer], b_ada[layer]).reshape(b, 6, 1, d)
        sh_m, sc_m, gt_m, sh_f, sc_f, gt_f = [mod[:, j] for j in range(6)]

        wl = w_in[layer]
        q_lo, k_lo, v_lo = POOL_WIDTH, POOL_WIDTH + 768, POOL_WIDTH + 2 * 768
        group_cols = []
        for g in range(3):
            sl = slice(g * 256, (g + 1) * 256)
            group_cols += [wl[:, q_lo:k_lo][:, sl] * q_scale, wl[:, k_lo:v_lo][:, sl],
                           wl[:, v_lo:v_lo + 768][:, sl]]
        w_perm = jnp.concatenate([wl[:, :POOL_WIDTH]] + group_cols + [wl[:, v_lo + 768:]],
                                 axis=1).astype(BF16)

        u, qkv0, qkv1, qkv2, gates = _inproj(x, g_pre_mix[layer].reshape(1, d), sc_m, sh_m, w_perm)
        att = _attention(qkv0, qkv1, qkv2, slopes)

        pad_cols = LANES - N_EXPERTS - N_EXPERT_GROUPS
        wr = jnp.concatenate([w_expert_router[layer], w_group_router[layer],
                              jnp.zeros((d, pad_cols), F32)], axis=1)
        wr_hi = wr.astype(BF16)
        wr_lo = (wr - wr_hi.astype(F32)).astype(BF16)
        br = jnp.concatenate([b_expert_router[layer], b_group_router[layer],
                              jnp.zeros((pad_cols,), F32)]).reshape(1, LANES)

        x_mid, route, meta, counts, xs = _mixtail(
            u, att, gates, x, gt_m, sc_f, sh_f,
            g_post_mix[layer].reshape(1, d), g_pre_ffn[layer].reshape(1, d),
            w_pool_group[layer].astype(BF16), pool_scale[layer].reshape(1, POOL_WIDTH),
            w_branch_pool[layer].astype(BF16), w_branch_att[layer].astype(BF16),
            w_out[layer].astype(BF16), wr_hi, wr_lo, br)

        ys = _experts(xs, counts[0, :N_EXPERTS], w_exp_gate[layer], w_exp_up[layer], w_exp_down[layer])
        run_meta = meta[:, 0:2, 0:N_EXPERTS].reshape(-1)
        x = _combine(run_meta, ys, route, x_mid.reshape(b * s, d), gt_f,
                     g_post_ffn[layer].reshape(1, d), s).reshape(b, s, d)
    return x
```

```python
import functools

import jax
import jax.numpy as jnp
from jax import lax
from jax.experimental import pallas as pl
from jax.experimental.pallas import tpu as pltpu

F32 = jnp.float32
BF16 = jnp.bfloat16
I32 = jnp.int32
HIGHEST = lax.Precision.HIGHEST

D_MODEL = 1024
LANES = 128
SUBLANES = 8
ROW_WORDS = D_MODEL // (2 * LANES)
HI_HALF = -65536

POOL_WINDOWS = (2, 4, 8, 16)
POOL_GROUP_DIM = 128
POOL_WIDTH = 512
POOL_HALO = 16

HEAD_DIM = 64
ATT_DILATIONS = (1, 4, 16)
ATT_REACH = 128
ATT_BLOCK = 128
ATT_UNROLL = 4
HEADS_PER_GROUP = 4
N_ATT_HEADS = 12
GROUP_QKV = 3 * HEADS_PER_GROUP * HEAD_DIM
ATT_OUT_WIDTH = 256
ALIBI_MAX_BIAS = 8.0
IN_WIDTH = POOL_WIDTH + 3 * GROUP_QKV + 2 * D_MODEL
MASKED = -1e30

N_EXPERT_GROUPS = 4
EXPERTS_PER_GROUP = 8
N_EXPERTS = 32
D_EXPERT = 512
RMS_EPS = 1e-6

ROW_TILE = 512
SORT_ROWS = 2 * ROW_TILE
RUN_CHUNK = 64
MOE_BLOCK = 256
BLOCK_DMA_PRIORITY = 1
VMEM_LIMIT = 52 * 1024 * 1024


def _sigmoid(x):
    return 0.5 * jnp.tanh(0.5 * x) + 0.5


def _rmsnorm(x, g):
    return x * lax.rsqrt(jnp.mean(x * x, axis=-1, keepdims=True) + RMS_EPS) * g


def _dot(a, b):
    return jnp.dot(a, b, preferred_element_type=F32)


def _dot_nt(a, b, **kw):
    return lax.dot_general(a, b, (((1,), (1,)), ((), ())), preferred_element_type=F32, **kw)


def _pack_rows(ref, w, n, lo, hi):
    word = (lax.shift_right_logical(pltpu.bitcast(lo, I32), 16) | (pltpu.bitcast(hi, I32) & HI_HALF))
    ref[pl.ds(w, n, stride=ROW_WORDS), :] = word


def _unpack_rows(ref, n):
    cols = []
    for w in range(ROW_WORDS):
        word = ref[pl.ds(w, n, stride=ROW_WORDS), :]
        cols += [pltpu.bitcast(word << 16, F32), pltpu.bitcast(word & HI_HALF, F32)]
    return jnp.concatenate(cols, axis=1)


def _bf16_exact(x):
    return x.astype(BF16).astype(F32)


def _region_rows(n_tokens):
    return n_tokens + MOE_BLOCK


def _run_copies(n, src, src_row, dst, dst_row, sem):
    def piece(offset, size):
        return pltpu.make_async_copy(
            src.at[pl.ds((src_row + offset) * ROW_WORDS, size * ROW_WORDS), :],
            dst.at[pl.ds((dst_row + offset) * ROW_WORDS, size * ROW_WORDS), :], sem)

    def whole_chunk(c, carry):
        piece(c * RUN_CHUNK, RUN_CHUNK).start()
        return carry

    lax.fori_loop(0, n // RUN_CHUNK, whole_chunk, 0)
    for bit in range(RUN_CHUNK.bit_length() - 1):
        size = 1 << bit
        above = n & ~(2 * size - 1)

        @pl.when((n & size) != 0)
        def _():
            piece(above, size).start()


def _adaln_kernel(c_ref, w_ref, b_ref, o_ref):
    c = c_ref[...]
    a = c * _sigmoid(c)
    o_ref[...] = jnp.dot(a, w_ref[...], preferred_element_type=F32, precision=HIGHEST) + b_ref[...]


def _adaln(c, w_ada, b_ada):
    b, d = c.shape
    n = w_ada.shape[1]
    rows = -(-b // SUBLANES) * SUBLANES
    cp = jnp.pad(c, ((0, rows - b), (0, 0)))
    nt = 1536
    out = pl.pallas_call(
        _adaln_kernel,
        grid=(n // nt,),
        in_specs=[pl.BlockSpec((rows, d), lambda j: (0, 0)),
                  pl.BlockSpec((d, nt), lambda j: (0, j)),
                  pl.BlockSpec((1, nt), lambda j: (0, j))],
        out_specs=pl.BlockSpec((rows, nt), lambda j: (0, j)),
        out_shape=jax.ShapeDtypeStruct((rows, n), F32),
        compiler_params=pltpu.CompilerParams(vmem_limit_bytes=VMEM_LIMIT),
        name="adaln",
    )(cp, w_ada, b_ada.reshape(1, n))
    return out[:b]


def _inproj_kernel(x_ref, g_ref, sc_ref, sh_ref, w_ref,
                   u_ref, qkv0_ref, qkv1_ref, qkv2_ref, gate_ref, h_scr, p_scr):
    tm = x_ref.shape[1]
    h = _rmsnorm(x_ref[0], g_ref[...]) * (1.0 + sc_ref[0]) + sh_ref[0]
    h_scr[...] = h.astype(BF16)

    u_ref[0] = _dot(h_scr[...], w_ref[:, 0:POOL_WIDTH]).astype(BF16)

    col = POOL_WIDTH
    qkv0_ref[0, 0] = _dot(h_scr[...], w_ref[:, col:col + GROUP_QKV]).astype(BF16)
    for out_ref, d in ((qkv1_ref, ATT_DILATIONS[1]), (qkv2_ref, ATT_DILATIONS[2])):
        col += GROUP_QKV
        proj = _dot(h_scr[...], w_ref[:, col:col + GROUP_QKV])
        for cb in range(GROUP_QKV // LANES):
            p_scr[cb] = proj[:, cb * LANES:(cb + 1) * LANES]
        for r in range(d):
            out_ref[0, r] = jnp.concatenate(
                [p_scr[cb, pl.ds(r, tm // d, stride=d), :] for cb in range(GROUP_QKV // LANES)],
                axis=1).astype(BF16)
    col += GROUP_QKV

    chunk = 512
    for j in range(2 * D_MODEL // chunk):
        g = _dot(h_scr[...], w_ref[:, col + j * chunk:col + (j + 1) * chunk])
        gate_ref[0, :, j * chunk:(j + 1) * chunk] = _sigmoid(g).astype(BF16)


def _inproj(x, g_pre, sc, sh, w_perm):
    b, s, d = x.shape
    tm = ROW_TILE
    d1, d2 = ATT_DILATIONS[1], ATT_DILATIONS[2]
    grid = (b, s // tm)
    const = lambda bi, i: (0, 0)
    per_b = lambda bi, i: (bi, 0, 0)
    return pl.pallas_call(
        _inproj_kernel,
        grid=grid,
        in_specs=[pl.BlockSpec((1, tm, d), lambda bi, i: (bi, i, 0)),
                  pl.BlockSpec((1, d), const),
                  pl.BlockSpec((1, 1, d), per_b),
                  pl.BlockSpec((1, 1, d), per_b),
                  pl.BlockSpec((d, IN_WIDTH), const, pipeline_mode=pl.Buffered(1))],
        out_specs=[pl.BlockSpec((1, tm, POOL_WIDTH), lambda bi, i: (bi, i, 0)),
                   pl.BlockSpec((1, 1, tm, GROUP_QKV), lambda bi, i: (bi, 0, i, 0)),
                   pl.BlockSpec((1, d1, tm // d1, GROUP_QKV), lambda bi, i: (bi, 0, i, 0)),
                   pl.BlockSpec((1, d2, tm // d2, GROUP_QKV), lambda bi, i: (bi, 0, i, 0)),
                   pl.BlockSpec((1, tm, 2 * D_MODEL), lambda bi, i: (bi, i, 0))],
        out_shape=[jax.ShapeDtypeStruct((b, s, POOL_WIDTH), BF16),
                   jax.ShapeDtypeStruct((b, 1, s, GROUP_QKV), BF16),
                   jax.ShapeDtypeStruct((b, d1, s // d1, GROUP_QKV), BF16),
                   jax.ShapeDtypeStruct((b, d2, s // d2, GROUP_QKV), BF16),
                   jax.ShapeDtypeStruct((b, s, 2 * D_MODEL), BF16)],
        scratch_shapes=[pltpu.VMEM((tm, d), BF16), pltpu.VMEM((GROUP_QKV // LANES, tm, LANES), F32)],
        compiler_params=pltpu.CompilerParams(
            dimension_semantics=("arbitrary", "arbitrary"), vmem_limit_bytes=VMEM_LIMIT),
        name="inproj",
    )(x, g_pre, sc, sh, w_perm)


def _attn_kernel(slopes_ref, q0, k0, v0, q1, k1, v1, q2, k2, v2, o_ref,
                 bias_scr, acc, mrun, lrun):
    pair = pl.program_id(1)
    seq = o_ref.shape[1]
    nblk = seq // ATT_BLOCK
    lane = lax.broadcasted_iota(I32, (ATT_BLOCK, LANES), 1)
    first_head = lane < HEAD_DIM
    half_lane = lax.broadcasted_iota(I32, (ATT_BLOCK // 2, LANES), 1)
    head_bits = (jnp.where(half_lane < HEAD_DIM, -1, 0), jnp.where(half_lane < HEAD_DIM, 0, -1))

    qi = lax.broadcasted_iota(I32, (ATT_BLOCK, 2 * ATT_BLOCK), 0)
    kj = lax.broadcasted_iota(I32, (ATT_BLOCK, 2 * ATT_BLOCK), 1)
    delta = ATT_BLOCK + qi - kj
    valid = (delta >= 0) & (delta <= ATT_REACH)
    delta0 = qi - kj
    valid0 = delta0 >= 0
    for g, d in enumerate(ATT_DILATIONS):
        for j in range(2):
            slope = slopes_ref[g * HEADS_PER_GROUP + 2 * pair + j]
            bias_scr[g, j, 0] = jnp.where(valid0, -slope * (delta0 * d).astype(F32), MASKED)
            bias_scr[g, j, 1] = jnp.where(valid, -slope * (delta * d).astype(F32), MASKED)

    def block_softmax(g, q_ref, k_ref, v_ref, r, i):
        lo = jnp.maximum(i - 1, 0)
        variant = jnp.minimum(i, 1)
        q = q_ref[0, r, pl.ds(pl.multiple_of(i * ATT_BLOCK, ATT_BLOCK), ATT_BLOCK), :]
        kw = k_ref[0, r, pl.ds(pl.multiple_of(lo * ATT_BLOCK, ATT_BLOCK), 2 * ATT_BLOCK), :]
        vw = v_ref[0, r, pl.ds(pl.multiple_of(lo * ATT_BLOCK, ATT_BLOCK), 2 * ATT_BLOCK), :]
        qbits = pltpu.bitcast(q, I32)
        outs = []
        for j in range(2):
            qh = pltpu.bitcast(qbits & head_bits[j], BF16)
            sc = _dot_nt(qh, kw) + bias_scr[g, j, variant]
            mx = jnp.max(sc, axis=-1, keepdims=True)
            p = jnp.exp(sc - mx)
            den = jnp.sum(p, axis=-1, keepdims=True)
            num = _dot(p.astype(BF16), vw)
            outs.append((num, mx, den))
        (n0, m0, l0), (n1, m1, l1) = outs
        return (jnp.where(first_head, n0, n1),
                jnp.where(first_head, m0, m1),
                jnp.where(first_head, l0, l1))

    def first_group(it, carry):
        blocks = [it * ATT_UNROLL + k for k in range(ATT_UNROLL)]
        results = [block_softmax(0, q0, k0, v0, 0, n) for n in blocks]
        for n, (num, mx, den) in zip(blocks, results):
            rows = pl.ds(pl.multiple_of(n * ATT_BLOCK, ATT_BLOCK), ATT_BLOCK)
            acc[rows, :] = num
            mrun[rows, :] = mx
            lrun[rows, :] = den
        return carry

    lax.fori_loop(0, nblk // ATT_UNROLL, first_group, 0)

    def merge_group(g, q_ref, k_ref, v_ref):
        d = ATT_DILATIONS[g]
        per_res = nblk // d

        def body(it, carry):
            blocks = [it * ATT_UNROLL + k for k in range(ATT_UNROLL)]
            results = [block_softmax(g, q_ref, k_ref, v_ref, n // per_res, n % per_res) for n in blocks]
            for n, (num, mx, den) in zip(blocks, results):
                r = n // per_res
                i = n % per_res
                rows = pl.ds(i * (ATT_BLOCK * d) + r, ATT_BLOCK, stride=d)
                m_old = mrun[rows, :]
                m_new = jnp.maximum(m_old, mx)
                a = jnp.exp(m_old - m_new)
                c = jnp.exp(mx - m_new)
                acc[rows, :] = a * acc[rows, :] + c * num
                lrun[rows, :] = a * lrun[rows, :] + c * den
                mrun[rows, :] = m_new
            return carry

        lax.fori_loop(0, nblk // ATT_UNROLL, body, 0)

    merge_group(1, q1, k1, v1)
    merge_group(2, q2, k2, v2)

    def finish(n, carry):
        rows = pl.ds(pl.multiple_of(n * ATT_BLOCK, ATT_BLOCK), ATT_BLOCK)
        o_ref[0, rows, :] = (acc[rows, :] / lrun[rows, :]).astype(BF16)
        return carry

    lax.fori_loop(0, nblk, finish, 0)


def _attention(qkv0, qkv1, qkv2, slopes):
    b, _, s, _ = qkv0.shape
    pairs = HEADS_PER_GROUP // 2
    col_blocks = HEADS_PER_GROUP * HEAD_DIM // LANES

    def specs(arr):
        _, d, sub, _ = arr.shape
        return [pl.BlockSpec((1, d, sub, LANES),
                             functools.partial(lambda bi, p, sec: (bi, 0, 0, sec * col_blocks + p), sec=sec))
                for sec in range(3)]

    return pl.pallas_call(
        _attn_kernel,
        grid=(b, pairs),
        in_specs=[pl.BlockSpec(memory_space=pltpu.SMEM)] + specs(qkv0) + specs(qkv1) + specs(qkv2),
        out_specs=pl.BlockSpec((1, s, LANES), lambda bi, p: (bi, 0, p)),
        out_shape=jax.ShapeDtypeStruct((b, s, ATT_OUT_WIDTH), BF16),
        scratch_shapes=[pltpu.VMEM((3, 2, 2, ATT_BLOCK, 2 * ATT_BLOCK), F32),
                        pltpu.VMEM((s, LANES), F32),
                        pltpu.VMEM((s, LANES), F32),
                        pltpu.VMEM((s, LANES), F32)],
        compiler_params=pltpu.CompilerParams(
            dimension_semantics=("arbitrary", "arbitrary"), vmem_limit_bytes=VMEM_LIMIT),
        name="attention",
    )(slopes, qkv0, qkv0, qkv0, qkv1, qkv1, qkv1, qkv2, qkv2, qkv2)


def _mixtail_kernel(u_ref, halo_ref, att_ref, gate_ref, x_ref,
                    gt_m_ref, sc_f_ref, sh_f_ref, g_post_ref, g_pre_ref,
                    wpg_ref, pscale_ref, wbp_ref, wba_ref, wout_ref, wrh_ref, wrl_ref, br_ref,
                    xmid_ref, route_ref, meta_ref, counts_ref, xs_hbm,
                    pu, xbuf, zbuf, fill, meta_s, sem, sem_s):
    i = pl.program_id(1)
    tm = x_ref.shape[1]
    step = pl.program_id(0) * pl.num_programs(1) + i
    last = pl.num_programs(0) * pl.num_programs(1) - 1
    slot = step % 2
    region = _region_rows(pl.num_programs(0) * pl.num_programs(1) * tm)

    @pl.when(step == 0)
    def _():
        fill[...] = jnp.zeros_like(fill)

    halo = halo_ref[0].astype(F32)
    pu[0:POOL_HALO, :] = jnp.where(i > 0, halo, jnp.zeros_like(halo))
    pu[POOL_HALO:POOL_HALO + tm, :] = u_ref[0].astype(F32)
    t = i * tm + lax.broadcasted_iota(I32, (tm, 1), 0)
    mixed = []
    for g, w in enumerate(POOL_WINDOWS):
        cols = slice(g * POOL_GROUP_DIM, (g + 1) * POOL_GROUP_DIM)
        win = pu[pl.ds(POOL_HALO, tm), cols]
        for j in range(1, w):
            win = win + pu[pl.ds(POOL_HALO - j, tm), cols]
        count = jnp.minimum(t + 1, w).astype(F32)
        pooled = win / count - pu[pl.ds(POOL_HALO, tm), cols]
        mixed.append((_dot(pooled.astype(BF16), wpg_ref[g]) * pscale_ref[:, cols]).astype(BF16))
    y_pool = _dot(jnp.concatenate(mixed, axis=1), wbp_ref[...])

    y_att = _dot(att_ref[0], wba_ref[...])
    merged = (gate_ref[0, :, 0:D_MODEL].astype(F32) * y_pool
              + gate_ref[0, :, D_MODEL:2 * D_MODEL].astype(F32) * y_att)
    y = _dot(merged.astype(BF16), wout_ref[...])
    x_mid = x_ref[0] + gt_m_ref[0] * _rmsnorm(y, g_post_ref[...])
    xmid_ref[0] = x_mid

    h2 = _rmsnorm(x_mid, g_pre_ref[...]) * (1.0 + sc_f_ref[0]) + sh_f_ref[0]

    h2b = h2.astype(BF16)
    h2lo = (h2 - h2b.astype(F32)).astype(BF16)
    by_token = (_dot(h2b, wrh_ref[...]) + (_dot(h2b, wrl_ref[...]) + _dot(h2lo, wrh_ref[...]))
                + br_ref[...])
    logits = jnp.concatenate([by_token[c * LANES:(c + 1) * LANES, :].T for c in range(tm // LANES)],
                             axis=1)
    neg_inf = -jnp.inf
    far = float(LANES)
    gl = logits[N_EXPERTS:N_EXPERTS + N_EXPERT_GROUPS, :]
    grow = lax.broadcasted_iota(I32, gl.shape, 0).astype(F32)
    gmax = jnp.max(gl, axis=0, keepdims=True)
    gsel = jnp.min(jnp.where(gl == gmax, grow, far), axis=0, keepdims=True)
    p_group = 1.0 / jnp.sum(jnp.exp(gl - gmax), axis=0, keepdims=True)
    erow = lax.broadcasted_iota(I32, (N_EXPERTS, tm), 0).astype(F32)
    e_lo = gsel * float(EXPERTS_PER_GROUP)
    el = jnp.where((erow >= e_lo) & (erow < e_lo + float(EXPERTS_PER_GROUP)), logits[0:N_EXPERTS, :], neg_inf)
    v1 = jnp.max(el, axis=0, keepdims=True)
    i1 = jnp.min(jnp.where(el == v1, erow, far), axis=0, keepdims=True)
    el2 = jnp.where(erow == i1, neg_inf, el)
    v2 = jnp.max(el2, axis=0, keepdims=True)
    i2 = jnp.min(jnp.where(el2 == v2, erow, far), axis=0, keepdims=True)
    e21 = jnp.exp(v2 - v1)
    w1 = p_group / (1.0 + e21)
    w2 = p_group * e21 / (1.0 + e21)

    pick1 = erow == i1
    pick2 = erow == i2
    assign = jnp.where(pick1 | pick2, 1.0, 0.0)
    before = (lax.broadcasted_iota(I32, (tm, tm), 0) < lax.broadcasted_iota(I32, (tm, tm), 1))
    rank = _dot(assign.astype(BF16), jnp.where(before, 1.0, 0.0).astype(BF16))
    assign_pad = jnp.concatenate([assign, jnp.zeros((LANES - N_EXPERTS, tm), F32)], axis=0).astype(BF16)
    cnt_row = _dot_nt(jnp.ones((SUBLANES, tm), BF16), assign_pad)
    e_col = lax.broadcasted_iota(I32, (N_EXPERTS, LANES), 0)
    e_lane = lax.broadcasted_iota(I32, (N_EXPERTS, LANES), 1)
    run_start = jnp.sum(jnp.where(e_lane < e_col, cnt_row[0:1, :], 0.0), axis=1, keepdims=True)
    pos = rank + run_start
    key1 = jnp.sum(jnp.where(pick1, pos, 0.0), axis=0, keepdims=True)
    key2 = jnp.sum(jnp.where(pick2, pos, 0.0), axis=0, keepdims=True)

    row8 = lax.broadcasted_iota(I32, (SUBLANES, tm), 0)
    route_ref[...] = jnp.where(row8 == 0, key1, jnp.where(row8 == 1, key2,
                               jnp.where(row8 == 2, w1, jnp.where(row8 == 3, w2, 0.0))))

    filled = fill[...]
    mrow = lax.broadcasted_iota(I32, (SUBLANES, LANES), 0)
    meta_ref[0] = jnp.where(mrow == 0, filled, jnp.where(mrow == 1, cnt_row, 0.0)).astype(I32)
    fill[...] = filled + cnt_row
    counts_ref[...] = (filled + cnt_row).astype(I32)

    srow = lax.broadcasted_iota(I32, (SORT_ROWS, tm), 0).astype(F32)
    perm = jnp.where((srow == key1) | (srow == key2), 1.0, 0.0).astype(BF16)
    xslot = xbuf.at[slot]
    for w in range(ROW_WORDS):
        pair = _dot(perm, h2b[:, 2 * w * LANES:(2 * w + 2) * LANES])
        _pack_rows(xslot, w, SORT_ROWS, pair[:, :LANES], pair[:, LANES:])

    def wait_tile(which):
        pltpu.make_async_copy(xbuf.at[which], xs_hbm.at[pl.ds(0, SORT_ROWS * ROW_WORDS), :], sem).wait()

    @pl.when(step > 0)
    def _():
        wait_tile(1 - slot)

    meta_copy = pltpu.make_async_copy(meta_ref.at[0], meta_s, sem_s)
    meta_copy.start()
    meta_copy.wait()

    def per_expert(e, local):
        n = meta_s[1, e]
        _run_copies(n, xslot, local, xs_hbm, e * region + meta_s[0, e], sem)
        return local + n

    lax.fori_loop(0, N_EXPERTS, per_expert, 0)

    @pl.when(step == last)
    def _():
        wait_tile(slot)
        zbuf[...] = jnp.zeros_like(zbuf)

        def pad_copy(e):
            end = e * region + meta_s[0, e] + meta_s[1, e]
            return pltpu.make_async_copy(
                zbuf, xs_hbm.at[pl.ds(end * ROW_WORDS, MOE_BLOCK * ROW_WORDS), :], sem)

        def start_pad(e, carry):
            pad_copy(e).start()
            return carry

        def wait_pad(e, carry):
            pad_copy(e).wait()
            return carry

        lax.fori_loop(0, N_EXPERTS, start_pad, 0)
        lax.fori_loop(0, N_EXPERTS, wait_pad, 0)


def _mixtail(u, att, gates, x, gt_m, sc_f, sh_f, g_post, g_pre,
             wpg, pscale, wbp, wba, wout, wr_hi, wr_lo, br):
    b, s, d = x.shape
    tm = ROW_TILE
    tiles = s // tm
    n_tiles = b * tiles
    halo_blocks = tm // POOL_HALO
    region = _region_rows(b * s)
    const2 = lambda bi, i: (0, 0)
    const3 = lambda bi, i: (0, 0, 0)
    per_b = lambda bi, i: (bi, 0, 0)
    tile = lambda bi, i: (bi, i, 0)
    single = dict(pipeline_mode=pl.Buffered(1))
    return pl.pallas_call(
        _mixtail_kernel,
        grid=(b, tiles),
        in_specs=[pl.BlockSpec((1, tm, POOL_WIDTH), tile),
                  pl.BlockSpec((1, POOL_HALO, POOL_WIDTH),
                               lambda bi, i: (bi, jnp.maximum(i * halo_blocks - 1, 0), 0)),
                  pl.BlockSpec((1, tm, ATT_OUT_WIDTH), tile),
                  pl.BlockSpec((1, tm, 2 * D_MODEL), tile),
                  pl.BlockSpec((1, tm, d), tile),
                  pl.BlockSpec((1, 1, d), per_b),
                  pl.BlockSpec((1, 1, d), per_b),
                  pl.BlockSpec((1, 1, d), per_b),
                  pl.BlockSpec((1, d), const2),
                  pl.BlockSpec((1, d), const2),
                  pl.BlockSpec(wpg.shape, const3, **single),
                  pl.BlockSpec((1, POOL_WIDTH), const2),
                  pl.BlockSpec(wbp.shape, const2, **single),
                  pl.BlockSpec(wba.shape, const2, **single),
                  pl.BlockSpec(wout.shape, const2, **single),
                  pl.BlockSpec(wr_hi.shape, const2, **single),
                  pl.BlockSpec(wr_lo.shape, const2, **single),
                  pl.BlockSpec(br.shape, const2)],
        out_specs=[pl.BlockSpec((1, tm, d), tile),
                   pl.BlockSpec((SUBLANES, tm), lambda bi, i: (0, bi * tiles + i)),
                   pl.BlockSpec((1, SUBLANES, LANES), lambda bi, i: (bi * tiles + i, 0, 0)),
                   pl.BlockSpec((SUBLANES, LANES), const2),
                   pl.BlockSpec(memory_space=pl.ANY)],
        out_shape=[jax.ShapeDtypeStruct((b, s, d), F32),
                   jax.ShapeDtypeStruct((SUBLANES, b * s), F32),
                   jax.ShapeDtypeStruct((n_tiles, SUBLANES, LANES), I32),
                   jax.ShapeDtypeStruct((SUBLANES, LANES), I32),
                   jax.ShapeDtypeStruct((N_EXPERTS * region * ROW_WORDS, LANES), I32)],
        scratch_shapes=[pltpu.VMEM((POOL_HALO + tm, POOL_WIDTH), F32),
                        pltpu.VMEM((2, SORT_ROWS * ROW_WORDS, LANES), I32),
                        pltpu.VMEM((MOE_BLOCK * ROW_WORDS, LANES), I32),
                        pltpu.VMEM((SUBLANES, LANES), F32),
                        pltpu.SMEM((SUBLANES, LANES), I32),
                        pltpu.SemaphoreType.DMA(()),
                        pltpu.SemaphoreType.DMA(())],
        compiler_params=pltpu.CompilerParams(
            dimension_semantics=("arbitrary", "arbitrary"), vmem_limit_bytes=VMEM_LIMIT),
        name="mixtail",
    )(u, u, att, gates, x, gt_m, sc_f, sh_f, g_post, g_pre,
      wpg, pscale, wbp, wba, wout, wr_hi, wr_lo, br)


def _expert_kernel(counts_ref, xs_hbm, wg_ref, wu_ref, wd_ref, ys_hbm,
                   wg_bf, wu_bf, wd_bf, xbuf, ybuf, state, semx, semy):
    e = pl.program_id(0)
    bm = MOE_BLOCK
    block_words = bm * ROW_WORDS
    region = xs_hbm.shape[0] // (N_EXPERTS * ROW_WORDS)

    def n_blocks(ex):
        return (counts_ref[ex] + (bm - 1)) // bm

    def x_copy(ex, k, s):
        start = pl.multiple_of((ex * region + k * bm) * ROW_WORDS, block_words)
        return pltpu.make_async_copy(xs_hbm.at[pl.ds(start, block_words), :], xbuf.at[s], semx.at[s])

    def y_copy(k, s):
        start = pl.multiple_of((e * region + k * bm) * ROW_WORDS, block_words)
        return pltpu.make_async_copy(ybuf.at[s], ys_hbm.at[pl.ds(start, block_words), :], semy.at[s])

    def fetch_first_block(ex, s):
        @pl.when(n_blocks(ex) > 0)
        def _():
            x_copy(ex, 0, s).start(priority=BLOCK_DMA_PRIORITY)

    def drain_y(s):
        @pl.when(state[1 + s] == 1)
        def _():
            y_copy(0, s).wait()
            state[1 + s] = 0

    @pl.when(e == 0)
    def _():
        state[0] = 0
        state[1] = 0
        state[2] = 0
        fetch_first_block(0, 0)

    wg_bf[...] = wg_ref[...].astype(BF16)
    wu_bf[...] = wu_ref[...].astype(BF16)
    wd_bf[...] = wd_ref[...].astype(BF16)

    nb = n_blocks(e)
    first = state[0]
    after = (first + nb) % 2

    def fetch_next_expert():
        @pl.when(e + 1 < N_EXPERTS)
        def _():
            fetch_first_block(jnp.minimum(e + 1, N_EXPERTS - 1), after)

    @pl.when(nb == 0)
    def _():
        fetch_next_expert()

    def block(k, carry):
        s = (first + k) % 2
        x_copy(e, k, s).wait()

        @pl.when(k + 1 < nb)
        def _():
            x_copy(e, k + 1, 1 - s).start(priority=BLOCK_DMA_PRIORITY)

        @pl.when(k + 1 == nb)
        def _():
            fetch_next_expert()

        x = _unpack_rows(xbuf.at[s], bm).astype(BF16)
        a = _dot(x, wg_bf[...])
        u = _dot(x, wu_bf[...])
        mid = ((a * _sigmoid(a)) * u).astype(BF16)
        drain_y(s)
        for w in range(ROW_WORDS):
            pair = _bf16_exact(_dot(mid, wd_bf[:, 2 * w * LANES:(2 * w + 2) * LANES]))
            _pack_rows(ybuf.at[s], w, bm, pair[:, :LANES], pair[:, LANES:])
        y_copy(k, s).start(priority=BLOCK_DMA_PRIORITY)
        state[1 + s] = 1
        return carry

    lax.fori_loop(0, nb, block, 0)
    state[0] = after

    @pl.when(e == N_EXPERTS - 1)
    def _():
        drain_y(0)
        drain_y(1)


def _experts(xs, counts, w_gate, w_up, w_down):
    bm = MOE_BLOCK
    w_in = pl.BlockSpec((None, D_MODEL, D_EXPERT), lambda e, cnt: (e, 0, 0))
    w_out = pl.BlockSpec((None, D_EXPERT, D_MODEL), lambda e, cnt: (e, 0, 0))
    grid_spec = pltpu.PrefetchScalarGridSpec(
        num_scalar_prefetch=1,
        grid=(N_EXPERTS,),
        in_specs=[pl.BlockSpec(memory_space=pl.ANY), w_in, w_in, w_out],
        out_specs=pl.BlockSpec(memory_space=pl.ANY),
        scratch_shapes=[pltpu.VMEM((D_MODEL, D_EXPERT), BF16),
                        pltpu.VMEM((D_MODEL, D_EXPERT), BF16),
                        pltpu.VMEM((D_EXPERT, D_MODEL), BF16),
                        pltpu.VMEM((2, bm * ROW_WORDS, LANES), I32),
                        pltpu.VMEM((2, bm * ROW_WORDS, LANES), I32),
                        pltpu.SMEM((3,), I32),
                        pltpu.SemaphoreType.DMA((2,)),
                        pltpu.SemaphoreType.DMA((2,))],
    )
    return pl.pallas_call(
        _expert_kernel,
        grid_spec=grid_spec,
        out_shape=jax.ShapeDtypeStruct(xs.shape, I32),
        compiler_params=pltpu.CompilerParams(
            dimension_semantics=("arbitrary",), vmem_limit_bytes=VMEM_LIMIT),
        name="experts",
    )(counts, xs, w_gate, w_up, w_down)


def _combine_kernel(meta_ref, y_hbm, route_ref, xmid_ref, gt_ref, g_ref, o_ref, ybuf, rt_scr, sem):
    step = pl.program_id(0)
    n_steps = pl.num_programs(0)
    tm = ROW_TILE
    slot = step % 2
    region = _region_rows(n_steps * tm)

    def fetch_tile(tile, which):
        def per_expert(e, local):
            filled = meta_ref[(tile * 2) * N_EXPERTS + e]
            n = meta_ref[(tile * 2 + 1) * N_EXPERTS + e]
            _run_copies(n, y_hbm, e * region + filled, ybuf.at[which], local, sem)
            return local + n
        lax.fori_loop(0, N_EXPERTS, per_expert, 0)

    @pl.when(step == 0)
    def _():
        fetch_tile(0, 0)

    pltpu.make_async_copy(y_hbm.at[pl.ds(0, SORT_ROWS * ROW_WORDS), :], ybuf.at[slot], sem).wait()

    @pl.when(step + 1 < n_steps)
    def _():
        fetch_tile(step + 1, 1 - slot)

    rt_scr[...] = jnp.zeros_like(rt_scr)
    rt_scr[0:SUBLANES, :] = route_ref[...]
    cols = jnp.concatenate([rt_scr[:, c * LANES:(c + 1) * LANES].T for c in range(tm // LANES)], axis=0)
    key1, key2, w1, w2 = cols[:, 0:1], cols[:, 1:2], cols[:, 2:3], cols[:, 3:4]

    yb = _unpack_rows(ybuf.at[slot], SORT_ROWS).astype(BF16)
    spos = lax.broadcasted_iota(I32, (tm, SORT_ROWS), 1).astype(F32)
    y1 = _dot(jnp.where(spos == key1, 1.0, 0.0).astype(BF16), yb)
    y2 = _dot(jnp.where(spos == key2, 1.0, 0.0).astype(BF16), yb)
    y = w1 * y1 + w2 * y2
    o_ref[...] = xmid_ref[...] + gt_ref[0] * _rmsnorm(y, g_ref[...])


def _combine(meta, ys, route, x_mid, gt_f, g_post, seq):
    t, d = x_mid.shape
    tm = ROW_TILE
    tiles_per_seq = seq // tm
    grid_spec = pltpu.PrefetchScalarGridSpec(
        num_scalar_prefetch=1,
        grid=(t // tm,),
        in_specs=[pl.BlockSpec(memory_space=pl.ANY),
                  pl.BlockSpec((SUBLANES, tm), lambda i, m: (0, i)),
                  pl.BlockSpec((tm, d), lambda i, m: (i, 0)),
                  pl.BlockSpec((1, 1, d), lambda i, m: (i // tiles_per_seq, 0, 0)),
                  pl.BlockSpec((1, d), lambda i, m: (0, 0))],
        out_specs=pl.BlockSpec((tm, d), lambda i, m: (i, 0)),
        scratch_shapes=[pltpu.VMEM((2, SORT_ROWS * ROW_WORDS, LANES), I32),
                        pltpu.VMEM((LANES, tm), F32),
                        pltpu.SemaphoreType.DMA(())],
    )
    return pl.pallas_call(
        _combine_kernel,
        grid_spec=grid_spec,
        out_shape=jax.ShapeDtypeStruct((t, d), F32),
        compiler_params=pltpu.CompilerParams(
            dimension_semantics=("arbitrary",), vmem_limit_bytes=VMEM_LIMIT),
        name="combine",
    )(meta, ys, route, x_mid, gt_f, g_post)


def kernel(x, c, w_ada, b_ada, g_pre_mix, g_post_mix, g_pre_ffn, g_post_ffn, w_in, w_pool_group, pool_scale, w_branch_pool, w_branch_att, w_out, w_group_router, b_group_router, w_expert_router, b_expert_router, w_exp_gate, w_exp_up, w_exp_down):
    b, s, d = x.shape
    assert d == D_MODEL and s % (ATT_BLOCK * 2 * ATT_DILATIONS[2]) == 0 and s % ROW_TILE == 0
    assert (b * s) % MOE_BLOCK == 0
    depth = w_ada.shape[0]
    slopes = jnp.exp2(-ALIBI_MAX_BIAS * jnp.arange(1, N_ATT_HEADS + 1, dtype=F32) / N_ATT_HEADS)
    q_scale = HEAD_DIM ** -0.5

    for layer in range(depth):
        mod = _adaln(c, w_ada[layer], b_ada[layer]).reshape(b, 6, 1, d)
        sh_m, sc_m, gt_m, sh_f, sc_f, gt_f = [mod[:, j] for j in range(6)]

        wl = w_in[layer]
        q_lo, k_lo, v_lo = POOL_WIDTH, POOL_WIDTH + 768, POOL_WIDTH + 2 * 768
        group_cols = []
        for g in range(3):
            sl = slice(g * 256, (g + 1) * 256)
            group_cols += [wl[:, q_lo:k_lo][:, sl] * q_scale, wl[:, k_lo:v_lo][:, sl],
                           wl[:, v_lo:v_lo + 768][:, sl]]
        w_perm = jnp.concatenate([wl[:, :POOL_WIDTH]] + group_cols + [wl[:, v_lo + 768:]],
                                 axis=1).astype(BF16)

        u, qkv0, qkv1, qkv2, gates = _inproj(x, g_pre_mix[layer].reshape(1, d), sc_m, sh_m, w_perm)
        att = _attention(qkv0, qkv1, qkv2, slopes)

        pad_cols = LANES - N_EXPERTS - N_EXPERT_GROUPS
        wr = jnp.concatenate([w_expert_router[layer], w_group_router[layer],
                              jnp.zeros((d, pad_cols), F32)], axis=1)
        wr_hi = wr.astype(BF16)
        wr_lo = (wr - wr_hi.astype(F32)).astype(BF16)
        br = jnp.concatenate([b_expert_router[layer], b_group_router[layer],
                              jnp.zeros((pad_cols,), F32)]).reshape(1, LANES)

        x_mid, route, meta, counts, xs = _mixtail(
            u, att, gates, x, gt_m, sc_f, sh_f,
            g_post_mix[layer].reshape(1, d), g_pre_ffn[layer].reshape(1, d),
            w_pool_group[layer].astype(BF16), pool_scale[layer].reshape(1, POOL_WIDTH),
            w_branch_pool[layer].astype(BF16), w_branch_att[layer].astype(BF16),
            w_out[layer].astype(BF16), wr_hi, wr_lo, br)

        ys = _experts(xs, counts[0, :N_EXPERTS], w_exp_gate[layer], w_exp_up[layer], w_exp_down[layer])
        run_meta = meta[:, 0:2, 0:N_EXPERTS].reshape(-1)
        x = _combine(run_meta, ys, route, x_mid.reshape(b * s, d), gt_f,
                     g_post_ffn[layer].reshape(1, d), s).reshape(b, s, d)
    return x
```

```python
import functools

import jax
import jax.numpy as jnp
from jax import lax
from jax.experimental import pallas as pl
from jax.experimental.pallas import tpu as pltpu

F32 = jnp.float32
BF16 = jnp.bfloat16
I32 = jnp.int32
HIGHEST = lax.Precision.HIGHEST

D_MODEL = 1024
LANES = 128
SUBLANES = 8
ROW_WORDS = D_MODEL // (2 * LANES)
HI_HALF = -65536

POOL_WINDOWS = (2, 4, 8, 16)
POOL_GROUP_DIM = 128
POOL_WIDTH = 512
POOL_HALO = 16

HEAD_DIM = 64
ATT_DILATIONS = (1, 4, 16)
ATT_REACH = 128
ATT_BLOCK = 128
ATT_UNROLL = 4
HEADS_PER_GROUP = 4
N_ATT_HEADS = 12
GROUP_QKV = 3 * HEADS_PER_GROUP * HEAD_DIM
ATT_OUT_WIDTH = 256
ALIBI_MAX_BIAS = 8.0
IN_WIDTH = POOL_WIDTH + 3 * GROUP_QKV + 2 * D_MODEL
MASKED = -1e30

N_EXPERT_GROUPS = 4
EXPERTS_PER_GROUP = 8
N_EXPERTS = 32
D_EXPERT = 512
RMS_EPS = 1e-6

ROW_TILE = 512
SORT_ROWS = 2 * ROW_TILE
RUN_CHUNK = 64
MOE_BLOCK = 256
BLOCK_DMA_PRIORITY = 1
EXPERT_X_BUFFERS = 4
EXPERT_Y_BUFFERS = 3
VMEM_LIMIT = 52 * 1024 * 1024


def _sigmoid(x):
    return 0.5 * jnp.tanh(0.5 * x) + 0.5


def _rmsnorm(x, g):
    return x * lax.rsqrt(jnp.mean(x * x, axis=-1, keepdims=True) + RMS_EPS) * g


def _dot(a, b):
    return jnp.dot(a, b, preferred_element_type=F32)


def _dot_nt(a, b, **kw):
    return lax.dot_general(a, b, (((1,), (1,)), ((), ())), preferred_element_type=F32, **kw)


def _pack_rows(ref, w, n, lo, hi):
    word = (lax.shift_right_logical(pltpu.bitcast(lo, I32), 16) | (pltpu.bitcast(hi, I32) & HI_HALF))
    ref[pl.ds(w, n, stride=ROW_WORDS), :] = word


def _unpack_rows(ref, n):
    cols = []
    for w in range(ROW_WORDS):
        word = ref[pl.ds(w, n, stride=ROW_WORDS), :]
        cols += [pltpu.bitcast(word << 16, F32), pltpu.bitcast(word & HI_HALF, F32)]
    return jnp.concatenate(cols, axis=1)


def _bf16_exact(x):
    return x.astype(BF16).astype(F32)


def _region_rows(n_tokens):
    return n_tokens + MOE_BLOCK


def _run_copies(n, src, src_row, dst, dst_row, sem):
    def piece(offset, size):
        return pltpu.make_async_copy(
            src.at[pl.ds((src_row + offset) * ROW_WORDS, size * ROW_WORDS), :],
            dst.at[pl.ds((dst_row + offset) * ROW_WORDS, size * ROW_WORDS), :], sem)

    def whole_chunk(c, carry):
        piece(c * RUN_CHUNK, RUN_CHUNK).start()
        return carry

    lax.fori_loop(0, n // RUN_CHUNK, whole_chunk, 0)
    for bit in range(RUN_CHUNK.bit_length() - 1):
        size = 1 << bit
        above = n & ~(2 * size - 1)

        @pl.when((n & size) != 0)
        def _():
            piece(above, size).start()


def _adaln_kernel(c_ref, w_ref, b_ref, o_ref):
    c = c_ref[...]
    a = c * _sigmoid(c)
    o_ref[...] = jnp.dot(a, w_ref[...], preferred_element_type=F32, precision=HIGHEST) + b_ref[...]


def _adaln(c, w_ada, b_ada):
    b, d = c.shape
    n = w_ada.shape[1]
    rows = -(-b // SUBLANES) * SUBLANES
    cp = jnp.pad(c, ((0, rows - b), (0, 0)))
    nt = 1536
    out = pl.pallas_call(
        _adaln_kernel,
        grid=(n // nt,),
        in_specs=[pl.BlockSpec((rows, d), lambda j: (0, 0)),
                  pl.BlockSpec((d, nt), lambda j: (0, j)),
                  pl.BlockSpec((1, nt), lambda j: (0, j))],
        out_specs=pl.BlockSpec((rows, nt), lambda j: (0, j)),
        out_shape=jax.ShapeDtypeStruct((rows, n), F32),
        compiler_params=pltpu.CompilerParams(vmem_limit_bytes=VMEM_LIMIT),
        name="adaln",
    )(cp, w_ada, b_ada.reshape(1, n))
    return out[:b]


def _inproj_kernel(x_ref, g_ref, sc_ref, sh_ref, w_ref,
                   u_ref, qkv0_ref, qkv1_ref, qkv2_ref, gate_ref, h_scr, p_scr):
    tm = x_ref.shape[1]
    h = _rmsnorm(x_ref[0], g_ref[...]) * (1.0 + sc_ref[0]) + sh_ref[0]
    h_scr[...] = h.astype(BF16)

    u_ref[0] = _dot(h_scr[...], w_ref[:, 0:POOL_WIDTH]).astype(BF16)

    col = POOL_WIDTH
    qkv0_ref[0, 0] = _dot(h_scr[...], w_ref[:, col:col + GROUP_QKV]).astype(BF16)
    for out_ref, d in ((qkv1_ref, ATT_DILATIONS[1]), (qkv2_ref, ATT_DILATIONS[2])):
        col += GROUP_QKV
        proj = _dot(h_scr[...], w_ref[:, col:col + GROUP_QKV])
        for cb in range(GROUP_QKV // LANES):
            p_scr[cb] = proj[:, cb * LANES:(cb + 1) * LANES]
        for r in range(d):
            out_ref[0, r] = jnp.concatenate(
                [p_scr[cb, pl.ds(r, tm // d, stride=d), :] for cb in range(GROUP_QKV // LANES)],
                axis=1).astype(BF16)
    col += GROUP_QKV

    chunk = 512
    for j in range(2 * D_MODEL // chunk):
        g = _dot(h_scr[...], w_ref[:, col + j * chunk:col + (j + 1) * chunk])
        gate_ref[0, :, j * chunk:(j + 1) * chunk] = _sigmoid(g).astype(BF16)


def _inproj(x, g_pre, sc, sh, w_perm):
    b, s, d = x.shape
    tm = ROW_TILE
    d1, d2 = ATT_DILATIONS[1], ATT_DILATIONS[2]
    grid = (b, s // tm)
    const = lambda bi, i: (0, 0)
    per_b = lambda bi, i: (bi, 0, 0)
    return pl.pallas_call(
        _inproj_kernel,
        grid=grid,
        in_specs=[pl.BlockSpec((1, tm, d), lambda bi, i: (bi, i, 0)),
                  pl.BlockSpec((1, d), const),
                  pl.BlockSpec((1, 1, d), per_b),
                  pl.BlockSpec((1, 1, d), per_b),
                  pl.BlockSpec((d, IN_WIDTH), const, pipeline_mode=pl.Buffered(1))],
        out_specs=[pl.BlockSpec((1, tm, POOL_WIDTH), lambda bi, i: (bi, i, 0)),
                   pl.BlockSpec((1, 1, tm, GROUP_QKV), lambda bi, i: (bi, 0, i, 0)),
                   pl.BlockSpec((1, d1, tm // d1, GROUP_QKV), lambda bi, i: (bi, 0, i, 0)),
                   pl.BlockSpec((1, d2, tm // d2, GROUP_QKV), lambda bi, i: (bi, 0, i, 0)),
                   pl.BlockSpec((1, tm, 2 * D_MODEL), lambda bi, i: (bi, i, 0))],
        out_shape=[jax.ShapeDtypeStruct((b, s, POOL_WIDTH), BF16),
                   jax.ShapeDtypeStruct((b, 1, s, GROUP_QKV), BF16),
                   jax.ShapeDtypeStruct((b, d1, s // d1, GROUP_QKV), BF16),
                   jax.ShapeDtypeStruct((b, d2, s // d2, GROUP_QKV), BF16),
                   jax.ShapeDtypeStruct((b, s, 2 * D_MODEL), BF16)],
        scratch_shapes=[pltpu.VMEM((tm, d), BF16), pltpu.VMEM((GROUP_QKV // LANES, tm, LANES), F32)],
        compiler_params=pltpu.CompilerParams(
            dimension_semantics=("arbitrary", "arbitrary"), vmem_limit_bytes=VMEM_LIMIT),
        name="inproj",
    )(x, g_pre, sc, sh, w_perm)


def _attn_kernel(slopes_ref, q0, k0, v0, q1, k1, v1, q2, k2, v2, o_ref,
                 bias_scr, acc, mrun, lrun):
    pair = pl.program_id(1)
    seq = o_ref.shape[1]
    nblk = seq // ATT_BLOCK
    lane = lax.broadcasted_iota(I32, (ATT_BLOCK, LANES), 1)
    first_head = lane < HEAD_DIM
    half_lane = lax.broadcasted_iota(I32, (ATT_BLOCK // 2, LANES), 1)
    head_bits = (jnp.where(half_lane < HEAD_DIM, -1, 0), jnp.where(half_lane < HEAD_DIM, 0, -1))

    qi = lax.broadcasted_iota(I32, (ATT_BLOCK, 2 * ATT_BLOCK), 0)
    kj = lax.broadcasted_iota(I32, (ATT_BLOCK, 2 * ATT_BLOCK), 1)
    delta = ATT_BLOCK + qi - kj
    valid = (delta >= 0) & (delta <= ATT_REACH)
    delta0 = qi - kj
    valid0 = delta0 >= 0
    for g, d in enumerate(ATT_DILATIONS):
        for j in range(2):
            slope = slopes_ref[g * HEADS_PER_GROUP + 2 * pair + j]
            bias_scr[g, j, 0] = jnp.where(valid0, -slope * (delta0 * d).astype(F32), MASKED)
            bias_scr[g, j, 1] = jnp.where(valid, -slope * (delta * d).astype(F32), MASKED)

    def block_softmax(g, q_ref, k_ref, v_ref, r, i):
        lo = jnp.maximum(i - 1, 0)
        variant = jnp.minimum(i, 1)
        q = q_ref[0, r, pl.ds(pl.multiple_of(i * ATT_BLOCK, ATT_BLOCK), ATT_BLOCK), :]
        kw = k_ref[0, r, pl.ds(pl.multiple_of(lo * ATT_BLOCK, ATT_BLOCK), 2 * ATT_BLOCK), :]
        vw = v_ref[0, r, pl.ds(pl.multiple_of(lo * ATT_BLOCK, ATT_BLOCK), 2 * ATT_BLOCK), :]
        qbits = pltpu.bitcast(q, I32)
        outs = []
        for j in range(2):
            qh = pltpu.bitcast(qbits & head_bits[j], BF16)
            sc = _dot_nt(qh, kw) + bias_scr[g, j, variant]
            mx = jnp.max(sc, axis=-1, keepdims=True)
            p = jnp.exp(sc - mx)
            den = jnp.sum(p, axis=-1, keepdims=True)
            num = _dot(p.astype(BF16), vw)
            outs.append((num, mx, den))
        (n0, m0, l0), (n1, m1, l1) = outs
        return (jnp.where(first_head, n0, n1),
                jnp.where(first_head, m0, m1),
                jnp.where(first_head, l0, l1))

    def first_group(it, carry):
        blocks = [it * ATT_UNROLL + k for k in range(ATT_UNROLL)]
        results = [block_softmax(0, q0, k0, v0, 0, n) for n in blocks]
        for n, (num, mx, den) in zip(blocks, results):
            rows = pl.ds(pl.multiple_of(n * ATT_BLOCK, ATT_BLOCK), ATT_BLOCK)
            acc[rows, :] = num
            mrun[rows, :] = mx
            lrun[rows, :] = den
        return carry

    lax.fori_loop(0, nblk // ATT_UNROLL, first_group, 0)

    def merge_group(g, q_ref, k_ref, v_ref):
        d = ATT_DILATIONS[g]
        per_res = nblk // d

        def body(it, carry):
            blocks = [it * ATT_UNROLL + k for k in range(ATT_UNROLL)]
            results = [block_softmax(g, q_ref, k_ref, v_ref, n // per_res, n % per_res) for n in blocks]
            for n, (num, mx, den) in zip(blocks, results):
                r = n // per_res
                i = n % per_res
                rows = pl.ds(i * (ATT_BLOCK * d) + r, ATT_BLOCK, stride=d)
                m_old = mrun[rows, :]
                m_new = jnp.maximum(m_old, mx)
                a = jnp.exp(m_old - m_new)
                c = jnp.exp(mx - m_new)
                acc[rows, :] = a * acc[rows, :] + c * num
                lrun[rows, :] = a * lrun[rows, :] + c * den
                mrun[rows, :] = m_new
            return carry

        lax.fori_loop(0, nblk // ATT_UNROLL, body, 0)

    merge_group(1, q1, k1, v1)
    merge_group(2, q2, k2, v2)

    def finish(n, carry):
        rows = pl.ds(pl.multiple_of(n * ATT_BLOCK, ATT_BLOCK), ATT_BLOCK)
        o_ref[0, rows, :] = (acc[rows, :] / lrun[rows, :]).astype(BF16)
        return carry

    lax.fori_loop(0, nblk, finish, 0)


def _attention(qkv0, qkv1, qkv2, slopes):
    b, _, s, _ = qkv0.shape
    pairs = HEADS_PER_GROUP // 2
    col_blocks = HEADS_PER_GROUP * HEAD_DIM // LANES

    def specs(arr):
        _, d, sub, _ = arr.shape
        return [pl.BlockSpec((1, d, sub, LANES),
                             functools.partial(lambda bi, p, sec: (bi, 0, 0, sec * col_blocks + p), sec=sec))
                for sec in range(3)]

    return pl.pallas_call(
        _attn_kernel,
        grid=(b, pairs),
        in_specs=[pl.BlockSpec(memory_space=pltpu.SMEM)] + specs(qkv0) + specs(qkv1) + specs(qkv2),
        out_specs=pl.BlockSpec((1, s, LANES), lambda bi, p: (bi, 0, p)),
        out_shape=jax.ShapeDtypeStruct((b, s, ATT_OUT_WIDTH), BF16),
        scratch_shapes=[pltpu.VMEM((3, 2, 2, ATT_BLOCK, 2 * ATT_BLOCK), F32),
                        pltpu.VMEM((s, LANES), F32),
                        pltpu.VMEM((s, LANES), F32),
                        pltpu.VMEM((s, LANES), F32)],
        compiler_params=pltpu.CompilerParams(
            dimension_semantics=("arbitrary", "arbitrary"), vmem_limit_bytes=VMEM_LIMIT),
        name="attention",
    )(slopes, qkv0, qkv0, qkv0, qkv1, qkv1, qkv1, qkv2, qkv2, qkv2)


def _mixtail_kernel(u_ref, halo_ref, att_ref, gate_ref, x_ref,
                    gt_m_ref, sc_f_ref, sh_f_ref, g_post_ref, g_pre_ref,
                    wpg_ref, pscale_ref, wbp_ref, wba_ref, wout_ref, wrh_ref, wrl_ref, br_ref,
                    xmid_ref, route_ref, meta_ref, counts_ref, xs_hbm,
                    pu, xbuf, zbuf, fill, meta_s, sem, sem_s):
    i = pl.program_id(1)
    tm = x_ref.shape[1]
    step = pl.program_id(0) * pl.num_programs(1) + i
    last = pl.num_programs(0) * pl.num_programs(1) - 1
    slot = step % 2
    region = _region_rows(pl.num_programs(0) * pl.num_programs(1) * tm)

    @pl.when(step == 0)
    def _():
        fill[...] = jnp.zeros_like(fill)

    halo = halo_ref[0].astype(F32)
    pu[0:POOL_HALO, :] = jnp.where(i > 0, halo, jnp.zeros_like(halo))
    pu[POOL_HALO:POOL_HALO + tm, :] = u_ref[0].astype(F32)
    t = i * tm + lax.broadcasted_iota(I32, (tm, 1), 0)
    mixed = []
    for g, w in enumerate(POOL_WINDOWS):
        cols = slice(g * POOL_GROUP_DIM, (g + 1) * POOL_GROUP_DIM)
        win = pu[pl.ds(POOL_HALO, tm), cols]
        for j in range(1, w):
            win = win + pu[pl.ds(POOL_HALO - j, tm), cols]
        count = jnp.minimum(t + 1, w).astype(F32)
        pooled = win / count - pu[pl.ds(POOL_HALO, tm), cols]
        mixed.append((_dot(pooled.astype(BF16), wpg_ref[g]) * pscale_ref[:, cols]).astype(BF16))
    y_pool = _dot(jnp.concatenate(mixed, axis=1), wbp_ref[...])

    y_att = _dot(att_ref[0], wba_ref[...])
    merged = (gate_ref[0, :, 0:D_MODEL].astype(F32) * y_pool
              + gate_ref[0, :, D_MODEL:2 * D_MODEL].astype(F32) * y_att)
    y = _dot(merged.astype(BF16), wout_ref[...])
    x_mid = x_ref[0] + gt_m_ref[0] * _rmsnorm(y, g_post_ref[...])
    xmid_ref[0] = x_mid

    h2 = _rmsnorm(x_mid, g_pre_ref[...]) * (1.0 + sc_f_ref[0]) + sh_f_ref[0]

    h2b = h2.astype(BF16)
    h2lo = (h2 - h2b.astype(F32)).astype(BF16)
    by_token = (_dot(h2b, wrh_ref[...]) + (_dot(h2b, wrl_ref[...]) + _dot(h2lo, wrh_ref[...]))
                + br_ref[...])
    logits = jnp.concatenate([by_token[c * LANES:(c + 1) * LANES, :].T for c in range(tm // LANES)],
                             axis=1)
    neg_inf = -jnp.inf
    far = float(LANES)
    gl = logits[N_EXPERTS:N_EXPERTS + N_EXPERT_GROUPS, :]
    grow = lax.broadcasted_iota(I32, gl.shape, 0).astype(F32)
    gmax = jnp.max(gl, axis=0, keepdims=True)
    gsel = jnp.min(jnp.where(gl == gmax, grow, far), axis=0, keepdims=True)
    p_group = 1.0 / jnp.sum(jnp.exp(gl - gmax), axis=0, keepdims=True)
    erow = lax.broadcasted_iota(I32, (N_EXPERTS, tm), 0).astype(F32)
    e_lo = gsel * float(EXPERTS_PER_GROUP)
    el = jnp.where((erow >= e_lo) & (erow < e_lo + float(EXPERTS_PER_GROUP)), logits[0:N_EXPERTS, :], neg_inf)
    v1 = jnp.max(el, axis=0, keepdims=True)
    i1 = jnp.min(jnp.where(el == v1, erow, far), axis=0, keepdims=True)
    el2 = jnp.where(erow == i1, neg_inf, el)
    v2 = jnp.max(el2, axis=0, keepdims=True)
    i2 = jnp.min(jnp.where(el2 == v2, erow, far), axis=0, keepdims=True)
    e21 = jnp.exp(v2 - v1)
    w1 = p_group / (1.0 + e21)
    w2 = p_group * e21 / (1.0 + e21)

    pick1 = erow == i1
    pick2 = erow == i2
    assign = jnp.where(pick1 | pick2, 1.0, 0.0)
    before = (lax.broadcasted_iota(I32, (tm, tm), 0) < lax.broadcasted_iota(I32, (tm, tm), 1))
    rank = _dot(assign.astype(BF16), jnp.where(before, 1.0, 0.0).astype(BF16))
    assign_pad = jnp.concatenate([assign, jnp.zeros((LANES - N_EXPERTS, tm), F32)], axis=0).astype(BF16)
    cnt_row = _dot_nt(jnp.ones((SUBLANES, tm), BF16), assign_pad)
    e_col = lax.broadcasted_iota(I32, (N_EXPERTS, LANES), 0)
    e_lane = lax.broadcasted_iota(I32, (N_EXPERTS, LANES), 1)
    run_start = jnp.sum(jnp.where(e_lane < e_col, cnt_row[0:1, :], 0.0), axis=1, keepdims=True)
    pos = rank + run_start
    key1 = jnp.sum(jnp.where(pick1, pos, 0.0), axis=0, keepdims=True)
    key2 = jnp.sum(jnp.where(pick2, pos, 0.0), axis=0, keepdims=True)

    row8 = lax.broadcasted_iota(I32, (SUBLANES, tm), 0)
    route_ref[...] = jnp.where(row8 == 0, key1, jnp.where(row8 == 1, key2,
                               jnp.where(row8 == 2, w1, jnp.where(row8 == 3, w2, 0.0))))

    filled = fill[...]
    mrow = lax.broadcasted_iota(I32, (SUBLANES, LANES), 0)
    meta_ref[0] = jnp.where(mrow == 0, filled, jnp.where(mrow == 1, cnt_row, 0.0)).astype(I32)
    fill[...] = filled + cnt_row
    counts_ref[...] = (filled + cnt_row).astype(I32)

    srow = lax.broadcasted_iota(I32, (SORT_ROWS, tm), 0).astype(F32)
    perm = jnp.where((srow == key1) | (srow == key2), 1.0, 0.0).astype(BF16)
    xslot = xbuf.at[slot]
    for w in range(ROW_WORDS):
        pair = _dot(perm, h2b[:, 2 * w * LANES:(2 * w + 2) * LANES])
        _pack_rows(xslot, w, SORT_ROWS, pair[:, :LANES], pair[:, LANES:])

    def wait_tile(which):
        pltpu.make_async_copy(xbuf.at[which], xs_hbm.at[pl.ds(0, SORT_ROWS * ROW_WORDS), :], sem).wait()

    @pl.when(step > 0)
    def _():
        wait_tile(1 - slot)

    meta_copy = pltpu.make_async_copy(meta_ref.at[0], meta_s, sem_s)
    meta_copy.start()
    meta_copy.wait()

    def per_expert(e, local):
        n = meta_s[1, e]
        _run_copies(n, xslot, local, xs_hbm, e * region + meta_s[0, e], sem)
        return local + n

    lax.fori_loop(0, N_EXPERTS, per_expert, 0)

    @pl.when(step == last)
    def _():
        wait_tile(slot)
        zbuf[...] = jnp.zeros_like(zbuf)

        def pad_copy(e):
            end = e * region + meta_s[0, e] + meta_s[1, e]
            return pltpu.make_async_copy(
                zbuf, xs_hbm.at[pl.ds(end * ROW_WORDS, MOE_BLOCK * ROW_WORDS), :], sem)

        def start_pad(e, carry):
            pad_copy(e).start()
            return carry

        def wait_pad(e, carry):
            pad_copy(e).wait()
            return carry

        lax.fori_loop(0, N_EXPERTS, start_pad, 0)
        lax.fori_loop(0, N_EXPERTS, wait_pad, 0)


def _mixtail(u, att, gates, x, gt_m, sc_f, sh_f, g_post, g_pre,
             wpg, pscale, wbp, wba, wout, wr_hi, wr_lo, br):
    b, s, d = x.shape
    tm = ROW_TILE
    tiles = s // tm
    n_tiles = b * tiles
    halo_blocks = tm // POOL_HALO
    region = _region_rows(b * s)
    const2 = lambda bi, i: (0, 0)
    const3 = lambda bi, i: (0, 0, 0)
    per_b = lambda bi, i: (bi, 0, 0)
    tile = lambda bi, i: (bi, i, 0)
    single = dict(pipeline_mode=pl.Buffered(1))
    return pl.pallas_call(
        _mixtail_kernel,
        grid=(b, tiles),
        in_specs=[pl.BlockSpec((1, tm, POOL_WIDTH), tile),
                  pl.BlockSpec((1, POOL_HALO, POOL_WIDTH),
                               lambda bi, i: (bi, jnp.maximum(i * halo_blocks - 1, 0), 0)),
                  pl.BlockSpec((1, tm, ATT_OUT_WIDTH), tile),
                  pl.BlockSpec((1, tm, 2 * D_MODEL), tile),
                  pl.BlockSpec((1, tm, d), tile),
                  pl.BlockSpec((1, 1, d), per_b),
                  pl.BlockSpec((1, 1, d), per_b),
                  pl.BlockSpec((1, 1, d), per_b),
                  pl.BlockSpec((1, d), const2),
                  pl.BlockSpec((1, d), const2),
                  pl.BlockSpec(wpg.shape, const3, **single),
                  pl.BlockSpec((1, POOL_WIDTH), const2),
                  pl.BlockSpec(wbp.shape, const2, **single),
                  pl.BlockSpec(wba.shape, const2, **single),
                  pl.BlockSpec(wout.shape, const2, **single),
                  pl.BlockSpec(wr_hi.shape, const2, **single),
                  pl.BlockSpec(wr_lo.shape, const2, **single),
                  pl.BlockSpec(br.shape, const2)],
        out_specs=[pl.BlockSpec((1, tm, d), tile),
                   pl.BlockSpec((SUBLANES, tm), lambda bi, i: (0, bi * tiles + i)),
                   pl.BlockSpec((1, SUBLANES, LANES), lambda bi, i: (bi * tiles + i, 0, 0)),
                   pl.BlockSpec((SUBLANES, LANES), const2),
                   pl.BlockSpec(memory_space=pl.ANY)],
        out_shape=[jax.ShapeDtypeStruct((b, s, d), F32),
                   jax.ShapeDtypeStruct((SUBLANES, b * s), F32),
                   jax.ShapeDtypeStruct((n_tiles, SUBLANES, LANES), I32),
                   jax.ShapeDtypeStruct((SUBLANES, LANES), I32),
                   jax.ShapeDtypeStruct((N_EXPERTS * region * ROW_WORDS, LANES), I32)],
        scratch_shapes=[pltpu.VMEM((POOL_HALO + tm, POOL_WIDTH), F32),
                        pltpu.VMEM((2, SORT_ROWS * ROW_WORDS, LANES), I32),
                        pltpu.VMEM((MOE_BLOCK * ROW_WORDS, LANES), I32),
                        pltpu.VMEM((SUBLANES, LANES), F32),
                        pltpu.SMEM((SUBLANES, LANES), I32),
                        pltpu.SemaphoreType.DMA(()),
                        pltpu.SemaphoreType.DMA(())],
        compiler_params=pltpu.CompilerParams(
            dimension_semantics=("arbitrary", "arbitrary"), vmem_limit_bytes=VMEM_LIMIT),
        name="mixtail",
    )(u, u, att, gates, x, gt_m, sc_f, sh_f, g_post, g_pre,
      wpg, pscale, wbp, wba, wout, wr_hi, wr_lo, br)


def _expert_kernel(counts_ref, xs_hbm, wg_ref, wu_ref, wd_ref, ys_hbm,
                   wg_bf, wu_bf, wd_bf, xbuf, ybuf, state, semx, semy):
    e = pl.program_id(0)
    bm = MOE_BLOCK
    nx, ny = EXPERT_X_BUFFERS, EXPERT_Y_BUFFERS
    block_words = bm * ROW_WORDS
    region = xs_hbm.shape[0] // (N_EXPERTS * ROW_WORDS)

    def n_blocks(ex):
        return (counts_ref[ex] + (bm - 1)) // bm

    def block_rows(ref, ex, k):
        start = pl.multiple_of((ex * region + k * bm) * ROW_WORDS, block_words)
        return ref.at[pl.ds(start, block_words), :]

    def x_copy(ex, k, s):
        return pltpu.make_async_copy(block_rows(xs_hbm, ex, k), xbuf.at[s], semx.at[s])

    def y_copy(k, s):
        return pltpu.make_async_copy(ybuf.at[s], block_rows(ys_hbm, e, k), semy.at[s])

    @pl.when(e == 0)
    def _():
        for j in range(4):
            state[j] = 0

    def fetch_through(target):
        def more(c):
            pe, _, pg = c
            return (pg < target) & (pe < N_EXPERTS)

        def step(c):
            pe, pk, pg = c
            has = pk < n_blocks(pe)

            @pl.when(has)
            def _():
                x_copy(pe, pk, pg % nx).start(priority=BLOCK_DMA_PRIORITY)

            return (jnp.where(has, pe, pe + 1), jnp.where(has, pk + 1, 0), pg + has.astype(I32))

        pe, pk, pg = lax.while_loop(more, step, (state[1], state[2], state[3]))
        state[1] = pe
        state[2] = pk
        state[3] = pg

    wg_bf[...] = wg_ref[...].astype(BF16)
    wu_bf[...] = wu_ref[...].astype(BF16)
    wd_bf[...] = wd_ref[...].astype(BF16)

    nb = n_blocks(e)
    done = state[0]

    def block(k, carry):
        g = done + k
        fetch_through(g + nx)
        sx = g % nx
        sy = g % ny
        x_copy(e, k, sx).wait()
        x = _unpack_rows(xbuf.at[sx], bm).astype(BF16)
        a = _dot(x, wg_bf[...])
        u = _dot(x, wu_bf[...])
        mid = ((a * _sigmoid(a)) * u).astype(BF16)

        @pl.when(g >= ny)
        def _():
            y_copy(k, sy).wait()

        for w in range(ROW_WORDS):
            pair = _bf16_exact(_dot(mid, wd_bf[:, 2 * w * LANES:(2 * w + 2) * LANES]))
            _pack_rows(ybuf.at[sy], w, bm, pair[:, :LANES], pair[:, LANES:])
        y_copy(k, sy).start(priority=BLOCK_DMA_PRIORITY)
        return carry

    lax.fori_loop(0, nb, block, 0)
    state[0] = done + nb

    @pl.when(e == N_EXPERTS - 1)
    def _():
        total = done + nb
        for j in range(ny):
            @pl.when(total > j)
            def _():
                y_copy(0, (total - 1 - j) % ny).wait()


def _experts(xs, counts, w_gate, w_up, w_down):
    bm = MOE_BLOCK
    w_in = pl.BlockSpec((None, D_MODEL, D_EXPERT), lambda e, cnt: (e, 0, 0))
    w_out = pl.BlockSpec((None, D_EXPERT, D_MODEL), lambda e, cnt: (e, 0, 0))
    grid_spec = pltpu.PrefetchScalarGridSpec(
        num_scalar_prefetch=1,
        grid=(N_EXPERTS,),
        in_specs=[pl.BlockSpec(memory_space=pl.ANY), w_in, w_in, w_out],
        out_specs=pl.BlockSpec(memory_space=pl.ANY),
        scratch_shapes=[pltpu.VMEM((D_MODEL, D_EXPERT), BF16),
                        pltpu.VMEM((D_MODEL, D_EXPERT), BF16),
                        pltpu.VMEM((D_EXPERT, D_MODEL), BF16),
                        pltpu.VMEM((EXPERT_X_BUFFERS, bm * ROW_WORDS, LANES), I32),
                        pltpu.VMEM((EXPERT_Y_BUFFERS, bm * ROW_WORDS, LANES), I32),
                        pltpu.SMEM((4,), I32),
                        pltpu.SemaphoreType.DMA((EXPERT_X_BUFFERS,)),
                        pltpu.SemaphoreType.DMA((EXPERT_Y_BUFFERS,))],
    )
    return pl.pallas_call(
        _expert_kernel,
        grid_spec=grid_spec,
        out_shape=jax.ShapeDtypeStruct(xs.shape, I32),
        compiler_params=pltpu.CompilerParams(
            dimension_semantics=("arbitrary",), vmem_limit_bytes=VMEM_LIMIT),
        name="experts",
    )(counts, xs, w_gate, w_up, w_down)


def _combine_kernel(meta_ref, y_hbm, route_ref, xmid_ref, gt_ref, g_ref, o_ref, ybuf, rt_scr, sem):
    step = pl.program_id(0)
    n_steps = pl.num_programs(0)
    tm = ROW_TILE
    slot = step % 2
    region = _region_rows(n_steps * tm)

    def fetch_tile(tile, which):
        def per_expert(e, local):
            filled = meta_ref[(tile * 2) * N_EXPERTS + e]
            n = meta_ref[(tile * 2 + 1) * N_EXPERTS + e]
            _run_copies(n, y_hbm, e * region + filled, ybuf.at[which], local, sem)
            return local + n
        lax.fori_loop(0, N_EXPERTS, per_expert, 0)

    @pl.when(step == 0)
    def _():
        fetch_tile(0, 0)

    pltpu.make_async_copy(y_hbm.at[pl.ds(0, SORT_ROWS * ROW_WORDS), :], ybuf.at[slot], sem).wait()

    @pl.when(step + 1 < n_steps)
    def _():
        fetch_tile(step + 1, 1 - slot)

    rt_scr[...] = jnp.zeros_like(rt_scr)
    rt_scr[0:SUBLANES, :] = route_ref[...]
    cols = jnp.concatenate([rt_scr[:, c * LANES:(c + 1) * LANES].T for c in range(tm // LANES)], axis=0)
    key1, key2, w1, w2 = cols[:, 0:1], cols[:, 1:2], cols[:, 2:3], cols[:, 3:4]

    def split3(w):
        hi = _bf16_exact(w)
        mid = _bf16_exact(w - hi)
        return hi, mid, w - hi - mid

    lane = lax.broadcasted_iota(I32, (tm, LANES), 1)

    def parts_in_lanes(w):
        hi, mid, lo = split3(w)
        return jnp.where(lane == 0, hi, jnp.where(lane == 1, mid, jnp.where(lane == 2, lo, 0.0))).astype(BF16)

    srow = lax.broadcasted_iota(I32, (SORT_ROWS, tm), 0).astype(F32)
    route = route_ref[...]
    slot1 = jnp.where(srow == route[0:1, :], 1.0, 0.0).astype(BF16)
    slot2 = jnp.where(srow == route[1:2, :], 1.0, 0.0).astype(BF16)
    parts = _dot(slot1, parts_in_lanes(w1)) + _dot(slot2, parts_in_lanes(w2))
    w_sorted = parts[:, 0:1] + parts[:, 1:2] + parts[:, 2:3]

    yw = (_unpack_rows(ybuf.at[slot], SORT_ROWS) * w_sorted).astype(BF16)
    spos = lax.broadcasted_iota(I32, (tm, SORT_ROWS), 1).astype(F32)
    unsort = jnp.where((spos == key1) | (spos == key2), 1.0, 0.0).astype(BF16)
    y = _dot(unsort, yw)
    o_ref[...] = xmid_ref[...] + gt_ref[0] * _rmsnorm(y, g_ref[...])


def _combine(meta, ys, route, x_mid, gt_f, g_post, seq):
    t, d = x_mid.shape
    tm = ROW_TILE
    tiles_per_seq = seq // tm
    grid_spec = pltpu.PrefetchScalarGridSpec(
        num_scalar_prefetch=1,
        grid=(t // tm,),
        in_specs=[pl.BlockSpec(memory_space=pl.ANY),
                  pl.BlockSpec((SUBLANES, tm), lambda i, m: (0, i)),
                  pl.BlockSpec((tm, d), lambda i, m: (i, 0)),
                  pl.BlockSpec((1, 1, d), lambda i, m: (i // tiles_per_seq, 0, 0)),
                  pl.BlockSpec((1, d), lambda i, m: (0, 0))],
        out_specs=pl.BlockSpec((tm, d), lambda i, m: (i, 0)),
        scratch_shapes=[pltpu.VMEM((2, SORT_ROWS * ROW_WORDS, LANES), I32),
                        pltpu.VMEM((LANES, tm), F32),
                        pltpu.SemaphoreType.DMA(())],
    )
    return pl.pallas_call(
        _combine_kernel,
        grid_spec=grid_spec,
        out_shape=jax.ShapeDtypeStruct((t, d), F32),
        compiler_params=pltpu.CompilerParams(
            dimension_semantics=("arbitrary",), vmem_limit_bytes=VMEM_LIMIT),
        name="combine",
    )(meta, ys, route, x_mid, gt_f, g_post)


def kernel(x, c, w_ada, b_ada, g_pre_mix, g_post_mix, g_pre_ffn, g_post_ffn, w_in, w_pool_group, pool_scale, w_branch_pool, w_branch_att, w_out, w_group_router, b_group_router, w_expert_router, b_expert_router, w_exp_gate, w_exp_up, w_exp_down):
    b, s, d = x.shape
    assert d == D_MODEL and s % (ATT_BLOCK * 2 * ATT_DILATIONS[2]) == 0 and s % ROW_TILE == 0
    assert (b * s) % MOE_BLOCK == 0
    depth = w_ada.shape[0]
    slopes = jnp.exp2(-ALIBI_MAX_BIAS * jnp.arange(1, N_ATT_HEADS + 1, dtype=F32) / N_ATT_HEADS)
    q_scale = HEAD_DIM ** -0.5

    for layer in range(depth):
        mod = _adaln(c, w_ada[layer], b_ada[layer]).reshape(b, 6, 1, d)
        sh_m, sc_m, gt_m, sh_f, sc_f, gt_f = [mod[:, j] for j in range(6)]

        wl = w_in[layer]
        q_lo, k_lo, v_lo = POOL_WIDTH, POOL_WIDTH + 768, POOL_WIDTH + 2 * 768
        group_cols = []
        for g in range(3):
            sl = slice(g * 256, (g + 1) * 256)
            group_cols += [wl[:, q_lo:k_lo][:, sl] * q_scale, wl[:, k_lo:v_lo][:, sl],
                           wl[:, v_lo:v_lo + 768][:, sl]]
        w_perm = jnp.concatenate([wl[:, :POOL_WIDTH]] + group_cols + [wl[:, v_lo + 768:]],
                                 axis=1).astype(BF16)

        u, qkv0, qkv1, qkv2, gates = _inproj(x, g_pre_mix[layer].reshape(1, d), sc_m, sh_m, w_perm)
        att = _attention(qkv0, qkv1, qkv2, slopes)

        pad_cols = LANES - N_EXPERTS - N_EXPERT_GROUPS
        wr = jnp.concatenate([w_expert_router[layer], w_group_router[layer],
                              jnp.zeros((d, pad_cols), F32)], axis=1)
        wr_hi = wr.astype(BF16)
        wr_lo = (wr - wr_hi.astype(F32)).astype(BF16)
        br = jnp.concatenate([b_expert_router[layer], b_group_router[layer],
                              jnp.zeros((pad_cols,), F32)]).reshape(1, LANES)

        x_mid, route, meta, counts, xs = _mixtail(
            u, att, gates, x, gt_m, sc_f, sh_f,
            g_post_mix[layer].reshape(1, d), g_pre_ffn[layer].reshape(1, d),
            w_pool_group[layer].astype(BF16), pool_scale[layer].reshape(1, POOL_WIDTH),
            w_branch_pool[layer].astype(BF16), w_branch_att[layer].astype(BF16),
            w_out[layer].astype(BF16), wr_hi, wr_lo, br)

        ys = _experts(xs, counts[0, :N_EXPERTS], w_exp_gate[layer], w_exp_up[layer], w_exp_down[layer])
        run_meta = meta[:, 0:2, 0:N_EXPERTS].reshape(-1)
        x = _combine(run_meta, ys, route, x_mid.reshape(b * s, d), gt_f,
                     g_post_ffn[layer].reshape(1, d), s).reshape(b, s, d)
    return x
```

```python
import functools

import jax
import jax.numpy as jnp
from jax import lax
from jax.experimental import pallas as pl
from jax.experimental.pallas import tpu as pltpu

F32 = jnp.float32
BF16 = jnp.bfloat16
I32 = jnp.int32
HIGHEST = lax.Precision.HIGHEST

D_MODEL = 1024
LANES = 128
SUBLANES = 8
ROW_WORDS = D_MODEL // (2 * LANES)
HI_HALF = -65536

POOL_WINDOWS = (2, 4, 8, 16)
POOL_GROUP_DIM = 128
POOL_WIDTH = 512
POOL_HALO = 16

HEAD_DIM = 64
ATT_DILATIONS = (1, 4, 16)
ATT_REACH = 128
ATT_BLOCK = 128
ATT_UNROLL = 4
HEADS_PER_GROUP = 4
N_ATT_HEADS = 12
GROUP_QKV = 3 * HEADS_PER_GROUP * HEAD_DIM
ATT_OUT_WIDTH = 256
ALIBI_MAX_BIAS = 8.0
IN_WIDTH = POOL_WIDTH + 3 * GROUP_QKV + 2 * D_MODEL
MASKED = -1e30

N_EXPERT_GROUPS = 4
EXPERTS_PER_GROUP = 8
N_EXPERTS = 32
D_EXPERT = 512
RMS_EPS = 1e-6

ROW_TILE = 512
SORT_ROWS = 2 * ROW_TILE
MOE_BLOCK = 256
BLOCK_DMA_PRIORITY = 1
EXPERT_X_BUFFERS = 6
EXPERT_Y_BUFFERS = 4
COMBINE_BUFFERS = 3
VMEM_LIMIT = 52 * 1024 * 1024


def _sigmoid(x):
    return 0.5 * jnp.tanh(0.5 * x) + 0.5


def _rmsnorm(x, g):
    return x * lax.rsqrt(jnp.mean(x * x, axis=-1, keepdims=True) + RMS_EPS) * g


def _dot(a, b):
    return jnp.dot(a, b, preferred_element_type=F32)


def _dot_nt(a, b, **kw):
    return lax.dot_general(a, b, (((1,), (1,)), ((), ())), preferred_element_type=F32, **kw)


def _pack_rows(ref, w, n, lo, hi):
    word = (lax.shift_right_logical(pltpu.bitcast(lo, I32), 16) | (pltpu.bitcast(hi, I32) & HI_HALF))
    ref[pl.ds(w, n, stride=ROW_WORDS), :] = word


def _unpack_rows(ref, n):
    cols = []
    for w in range(ROW_WORDS):
        word = ref[pl.ds(w, n, stride=ROW_WORDS), :]
        cols += [pltpu.bitcast(word << 16, F32), pltpu.bitcast(word & HI_HALF, F32)]
    return jnp.concatenate(cols, axis=1)


def _bf16_exact(x):
    return x.astype(BF16).astype(F32)


def _region_rows(n_tokens):
    return n_tokens + MOE_BLOCK


def _run_copies(n, src, src_row, dst, dst_row, sem):
    for bit in range(ROW_TILE.bit_length()):
        size = 1 << bit
        above = n & ~(2 * size - 1)

        @pl.when((n & size) != 0)
        def _():
            pltpu.make_async_copy(
                src.at[pl.ds((src_row + above) * ROW_WORDS, size * ROW_WORDS), :],
                dst.at[pl.ds((dst_row + above) * ROW_WORDS, size * ROW_WORDS), :], sem).start()


def _tile_run_copies(counts, src, src_rows, dst, dst_rows, sem):
    local = 0
    for e in range(N_EXPERTS):
        n = counts(e)
        _run_copies(n, src, local if src_rows is None else src_rows(e),
                    dst, local if dst_rows is None else dst_rows(e), sem)
        local = local + n


def _adaln_kernel(c_ref, w_ref, b_ref, o_ref):
    c = c_ref[...]
    a = c * _sigmoid(c)
    o_ref[...] = jnp.dot(a, w_ref[...], preferred_element_type=F32, precision=HIGHEST) + b_ref[...]


def _adaln(c, w_ada, b_ada):
    b, d = c.shape
    n = w_ada.shape[1]
    rows = -(-b // SUBLANES) * SUBLANES
    cp = jnp.pad(c, ((0, rows - b), (0, 0)))
    nt = 1536
    out = pl.pallas_call(
        _adaln_kernel,
        grid=(n // nt,),
        in_specs=[pl.BlockSpec((rows, d), lambda j: (0, 0)),
                  pl.BlockSpec((d, nt), lambda j: (0, j)),
                  pl.BlockSpec((1, nt), lambda j: (0, j))],
        out_specs=pl.BlockSpec((rows, nt), lambda j: (0, j)),
        out_shape=jax.ShapeDtypeStruct((rows, n), F32),
        compiler_params=pltpu.CompilerParams(vmem_limit_bytes=VMEM_LIMIT),
        name="adaln",
    )(cp, w_ada, b_ada.reshape(1, n))
    return out[:b]


def _inproj_kernel(x_ref, g_ref, sc_ref, sh_ref, w_ref,
                   u_ref, qkv0_ref, qkv1_ref, qkv2_ref, gate_ref, h_scr, p_scr):
    tm = x_ref.shape[1]
    h = _rmsnorm(x_ref[0], g_ref[...]) * (1.0 + sc_ref[0]) + sh_ref[0]
    h_scr[...] = h.astype(BF16)

    u_ref[0] = _dot(h_scr[...], w_ref[:, 0:POOL_WIDTH]).astype(BF16)

    col = POOL_WIDTH
    qkv0_ref[0, 0] = _dot(h_scr[...], w_ref[:, col:col + GROUP_QKV]).astype(BF16)
    for out_ref, d in ((qkv1_ref, ATT_DILATIONS[1]), (qkv2_ref, ATT_DILATIONS[2])):
        col += GROUP_QKV
        proj = _dot(h_scr[...], w_ref[:, col:col + GROUP_QKV])
        for cb in range(GROUP_QKV // LANES):
            p_scr[cb] = proj[:, cb * LANES:(cb + 1) * LANES]
        for r in range(d):
            out_ref[0, r] = jnp.concatenate(
                [p_scr[cb, pl.ds(r, tm // d, stride=d), :] for cb in range(GROUP_QKV // LANES)],
                axis=1).astype(BF16)
    col += GROUP_QKV

    chunk = 512
    for j in range(2 * D_MODEL // chunk):
        g = _dot(h_scr[...], w_ref[:, col + j * chunk:col + (j + 1) * chunk])
        gate_ref[0, :, j * chunk:(j + 1) * chunk] = _sigmoid(g).astype(BF16)


def _inproj(x, g_pre, sc, sh, w_perm):
    b, s, d = x.shape
    tm = ROW_TILE
    d1, d2 = ATT_DILATIONS[1], ATT_DILATIONS[2]
    grid = (b, s // tm)
    const = lambda bi, i: (0, 0)
    per_b = lambda bi, i: (bi, 0, 0)
    return pl.pallas_call(
        _inproj_kernel,
        grid=grid,
        in_specs=[pl.BlockSpec((1, tm, d), lambda bi, i: (bi, i, 0)),
                  pl.BlockSpec((1, d), const),
                  pl.BlockSpec((1, 1, d), per_b),
                  pl.BlockSpec((1, 1, d), per_b),
                  pl.BlockSpec((d, IN_WIDTH), const, pipeline_mode=pl.Buffered(1))],
        out_specs=[pl.BlockSpec((1, tm, POOL_WIDTH), lambda bi, i: (bi, i, 0)),
                   pl.BlockSpec((1, 1, tm, GROUP_QKV), lambda bi, i: (bi, 0, i, 0)),
                   pl.BlockSpec((1, d1, tm // d1, GROUP_QKV), lambda bi, i: (bi, 0, i, 0)),
                   pl.BlockSpec((1, d2, tm // d2, GROUP_QKV), lambda bi, i: (bi, 0, i, 0)),
                   pl.BlockSpec((1, tm, 2 * D_MODEL), lambda bi, i: (bi, i, 0))],
        out_shape=[jax.ShapeDtypeStruct((b, s, POOL_WIDTH), BF16),
                   jax.ShapeDtypeStruct((b, 1, s, GROUP_QKV), BF16),
                   jax.ShapeDtypeStruct((b, d1, s // d1, GROUP_QKV), BF16),
                   jax.ShapeDtypeStruct((b, d2, s // d2, GROUP_QKV), BF16),
                   jax.ShapeDtypeStruct((b, s, 2 * D_MODEL), BF16)],
        scratch_shapes=[pltpu.VMEM((tm, d), BF16), pltpu.VMEM((GROUP_QKV // LANES, tm, LANES), F32)],
        compiler_params=pltpu.CompilerParams(
            dimension_semantics=("arbitrary", "arbitrary"), vmem_limit_bytes=VMEM_LIMIT),
        name="inproj",
    )(x, g_pre, sc, sh, w_perm)


def _attn_kernel(slopes_ref, q0, k0, v0, q1, k1, v1, q2, k2, v2, o_ref,
                 bias_scr, acc, mrun, lrun):
    pair = pl.program_id(1)
    seq = o_ref.shape[1]
    nblk = seq // ATT_BLOCK
    lane = lax.broadcasted_iota(I32, (ATT_BLOCK, LANES), 1)
    first_head = lane < HEAD_DIM
    half_lane = lax.broadcasted_iota(I32, (ATT_BLOCK // 2, LANES), 1)
    head_bits = (jnp.where(half_lane < HEAD_DIM, -1, 0), jnp.where(half_lane < HEAD_DIM, 0, -1))

    qi = lax.broadcasted_iota(I32, (ATT_BLOCK, 2 * ATT_BLOCK), 0)
    kj = lax.broadcasted_iota(I32, (ATT_BLOCK, 2 * ATT_BLOCK), 1)
    delta = ATT_BLOCK + qi - kj
    valid = (delta >= 0) & (delta <= ATT_REACH)
    delta0 = qi - kj
    valid0 = delta0 >= 0
    for g, d in enumerate(ATT_DILATIONS):
        for j in range(2):
            slope = slopes_ref[g * HEADS_PER_GROUP + 2 * pair + j]
            bias_scr[g, j, 0] = jnp.where(valid0, -slope * (delta0 * d).astype(F32), MASKED)
            bias_scr[g, j, 1] = jnp.where(valid, -slope * (delta * d).astype(F32), MASKED)

    def block_softmax(g, q_ref, k_ref, v_ref, r, i):
        lo = jnp.maximum(i - 1, 0)
        variant = jnp.minimum(i, 1)
        q = q_ref[0, r, pl.ds(pl.multiple_of(i * ATT_BLOCK, ATT_BLOCK), ATT_BLOCK), :]
        kw = k_ref[0, r, pl.ds(pl.multiple_of(lo * ATT_BLOCK, ATT_BLOCK), 2 * ATT_BLOCK), :]
        vw = v_ref[0, r, pl.ds(pl.multiple_of(lo * ATT_BLOCK, ATT_BLOCK), 2 * ATT_BLOCK), :]
        qbits = pltpu.bitcast(q, I32)
        outs = []
        for j in range(2):
            qh = pltpu.bitcast(qbits & head_bits[j], BF16)
            sc = _dot_nt(qh, kw) + bias_scr[g, j, variant]
            mx = jnp.max(sc, axis=-1, keepdims=True)
            p = jnp.exp(sc - mx)
            den = jnp.sum(p, axis=-1, keepdims=True)
            num = _dot(p.astype(BF16), vw)
            outs.append((num, mx, den))
        (n0, m0, l0), (n1, m1, l1) = outs
        return (jnp.where(first_head, n0, n1),
                jnp.where(first_head, m0, m1),
                jnp.where(first_head, l0, l1))

    def first_group(it, carry):
        blocks = [it * ATT_UNROLL + k for k in range(ATT_UNROLL)]
        results = [block_softmax(0, q0, k0, v0, 0, n) for n in blocks]
        for n, (num, mx, den) in zip(blocks, results):
            rows = pl.ds(pl.multiple_of(n * ATT_BLOCK, ATT_BLOCK), ATT_BLOCK)
            acc[rows, :] = num
            mrun[rows, :] = mx
            lrun[rows, :] = den
        return carry

    lax.fori_loop(0, nblk // ATT_UNROLL, first_group, 0)

    def merge_group(g, q_ref, k_ref, v_ref):
        d = ATT_DILATIONS[g]
        per_res = nblk // d

        def body(it, carry):
            blocks = [it * ATT_UNROLL + k for k in range(ATT_UNROLL)]
            results = [block_softmax(g, q_ref, k_ref, v_ref, n // per_res, n % per_res) for n in blocks]
            for n, (num, mx, den) in zip(blocks, results):
                r = n // per_res
                i = n % per_res
                rows = pl.ds(i * (ATT_BLOCK * d) + r, ATT_BLOCK, stride=d)
                m_old = mrun[rows, :]
                m_new = jnp.maximum(m_old, mx)
                a = jnp.exp(m_old - m_new)
                c = jnp.exp(mx - m_new)
                acc[rows, :] = a * acc[rows, :] + c * num
                lrun[rows, :] = a * lrun[rows, :] + c * den
                mrun[rows, :] = m_new
            return carry

        lax.fori_loop(0, nblk // ATT_UNROLL, body, 0)

    merge_group(1, q1, k1, v1)
    merge_group(2, q2, k2, v2)

    def finish(n, carry):
        rows = pl.ds(pl.multiple_of(n * ATT_BLOCK, ATT_BLOCK), ATT_BLOCK)
        o_ref[0, rows, :] = (acc[rows, :] / lrun[rows, :]).astype(BF16)
        return carry

    lax.fori_loop(0, nblk, finish, 0)


def _attention(qkv0, qkv1, qkv2, slopes):
    b, _, s, _ = qkv0.shape
    pairs = HEADS_PER_GROUP // 2
    col_blocks = HEADS_PER_GROUP * HEAD_DIM // LANES

    def specs(arr):
        _, d, sub, _ = arr.shape
        return [pl.BlockSpec((1, d, sub, LANES),
                             functools.partial(lambda bi, p, sec: (bi, 0, 0, sec * col_blocks + p), sec=sec))
                for sec in range(3)]

    return pl.pallas_call(
        _attn_kernel,
        grid=(b, pairs),
        in_specs=[pl.BlockSpec(memory_space=pltpu.SMEM)] + specs(qkv0) + specs(qkv1) + specs(qkv2),
        out_specs=pl.BlockSpec((1, s, LANES), lambda bi, p: (bi, 0, p)),
        out_shape=jax.ShapeDtypeStruct((b, s, ATT_OUT_WIDTH), BF16),
        scratch_shapes=[pltpu.VMEM((3, 2, 2, ATT_BLOCK, 2 * ATT_BLOCK), F32),
                        pltpu.VMEM((s, LANES), F32),
                        pltpu.VMEM((s, LANES), F32),
                        pltpu.VMEM((s, LANES), F32)],
        compiler_params=pltpu.CompilerParams(
            dimension_semantics=("arbitrary", "arbitrary"), vmem_limit_bytes=VMEM_LIMIT),
        name="attention",
    )(slopes, qkv0, qkv0, qkv0, qkv1, qkv1, qkv1, qkv2, qkv2, qkv2)


def _mixtail_kernel(u_ref, halo_ref, att_ref, gate_ref, x_ref,
                    gt_m_ref, sc_f_ref, sh_f_ref, g_post_ref, g_pre_ref,
                    wpg_ref, pscale_ref, wbp_ref, wba_ref, wout_ref, wrh_ref, wrl_ref, br_ref,
                    xmid_ref, route_ref, meta_ref, counts_ref, xs_hbm,
                    pu, xbuf, zbuf, fill, meta_s, sem, sem_s):
    i = pl.program_id(1)
    tm = x_ref.shape[1]
    step = pl.program_id(0) * pl.num_programs(1) + i
    last = pl.num_programs(0) * pl.num_programs(1) - 1
    slot = step % 2
    region = _region_rows(pl.num_programs(0) * pl.num_programs(1) * tm)

    @pl.when(step == 0)
    def _():
        fill[...] = jnp.zeros_like(fill)
        for e in range(N_EXPERTS):
            meta_s[1, 1, e] = 0
            meta_s[1, 0, e] = 0

    def send_tile(which):
        _tile_run_copies(lambda e: meta_s[which, 1, e], xbuf.at[which], None,
                         xs_hbm, lambda e: e * region + meta_s[which, 0, e], sem.at[which])

    send_tile(1 - slot)

    halo = halo_ref[0].astype(F32)
    pu[0:POOL_HALO, :] = jnp.where(i > 0, halo, jnp.zeros_like(halo))
    pu[POOL_HALO:POOL_HALO + tm, :] = u_ref[0].astype(F32)
    t = i * tm + lax.broadcasted_iota(I32, (tm, 1), 0)
    mixed = []
    for g, w in enumerate(POOL_WINDOWS):
        cols = slice(g * POOL_GROUP_DIM, (g + 1) * POOL_GROUP_DIM)
        win = pu[pl.ds(POOL_HALO, tm), cols]
        for j in range(1, w):
            win = win + pu[pl.ds(POOL_HALO - j, tm), cols]
        count = jnp.minimum(t + 1, w).astype(F32)
        pooled = win / count - pu[pl.ds(POOL_HALO, tm), cols]
        mixed.append((_dot(pooled.astype(BF16), wpg_ref[g]) * pscale_ref[:, cols]).astype(BF16))
    y_pool = _dot(jnp.concatenate(mixed, axis=1), wbp_ref[...])

    y_att = _dot(att_ref[0], wba_ref[...])
    merged = (gate_ref[0, :, 0:D_MODEL].astype(F32) * y_pool
              + gate_ref[0, :, D_MODEL:2 * D_MODEL].astype(F32) * y_att)
    y = _dot(merged.astype(BF16), wout_ref[...])
    x_mid = x_ref[0] + gt_m_ref[0] * _rmsnorm(y, g_post_ref[...])
    xmid_ref[0] = x_mid

    h2 = _rmsnorm(x_mid, g_pre_ref[...]) * (1.0 + sc_f_ref[0]) + sh_f_ref[0]

    h2b = h2.astype(BF16)
    h2lo = (h2 - h2b.astype(F32)).astype(BF16)
    by_token = (_dot(h2b, wrh_ref[...]) + (_dot(h2b, wrl_ref[...]) + _dot(h2lo, wrh_ref[...]))
                + br_ref[...])
    logits = jnp.concatenate([by_token[c * LANES:(c + 1) * LANES, :].T for c in range(tm // LANES)],
                             axis=1)
    neg_inf = -jnp.inf
    far = float(LANES)
    gl = logits[N_EXPERTS:N_EXPERTS + N_EXPERT_GROUPS, :]
    grow = lax.broadcasted_iota(I32, gl.shape, 0).astype(F32)
    gmax = jnp.max(gl, axis=0, keepdims=True)
    gsel = jnp.min(jnp.where(gl == gmax, grow, far), axis=0, keepdims=True)
    p_group = 1.0 / jnp.sum(jnp.exp(gl - gmax), axis=0, keepdims=True)
    erow = lax.broadcasted_iota(I32, (N_EXPERTS, tm), 0).astype(F32)
    e_lo = gsel * float(EXPERTS_PER_GROUP)
    el = jnp.where((erow >= e_lo) & (erow < e_lo + float(EXPERTS_PER_GROUP)), logits[0:N_EXPERTS, :], neg_inf)
    v1 = jnp.max(el, axis=0, keepdims=True)
    i1 = jnp.min(jnp.where(el == v1, erow, far), axis=0, keepdims=True)
    el2 = jnp.where(erow == i1, neg_inf, el)
    v2 = jnp.max(el2, axis=0, keepdims=True)
    i2 = jnp.min(jnp.where(el2 == v2, erow, far), axis=0, keepdims=True)
    e21 = jnp.exp(v2 - v1)
    w1 = p_group / (1.0 + e21)
    w2 = p_group * e21 / (1.0 + e21)

    pick1 = erow == i1
    pick2 = erow == i2
    assign = jnp.where(pick1 | pick2, 1.0, 0.0)
    before = (lax.broadcasted_iota(I32, (tm, tm), 0) < lax.broadcasted_iota(I32, (tm, tm), 1))
    rank = _dot(assign.astype(BF16), jnp.where(before, 1.0, 0.0).astype(BF16))
    assign_pad = jnp.concatenate([assign, jnp.zeros((LANES - N_EXPERTS, tm), F32)], axis=0).astype(BF16)
    cnt_row = _dot_nt(jnp.ones((SUBLANES, tm), BF16), assign_pad)
    e_col = lax.broadcasted_iota(I32, (N_EXPERTS, LANES), 0)
    e_lane = lax.broadcasted_iota(I32, (N_EXPERTS, LANES), 1)
    run_start = jnp.sum(jnp.where(e_lane < e_col, cnt_row[0:1, :], 0.0), axis=1, keepdims=True)
    pos = rank + run_start
    key1 = jnp.sum(jnp.where(pick1, pos, 0.0), axis=0, keepdims=True)
    key2 = jnp.sum(jnp.where(pick2, pos, 0.0), axis=0, keepdims=True)

    row8 = lax.broadcasted_iota(I32, (SUBLANES, tm), 0)
    route_ref[...] = jnp.where(row8 == 0, key1, jnp.where(row8 == 1, key2,
                               jnp.where(row8 == 2, w1, jnp.where(row8 == 3, w2, 0.0))))

    filled = fill[...]
    mrow = lax.broadcasted_iota(I32, (SUBLANES, LANES), 0)
    meta_ref[0] = jnp.where(mrow == 0, filled, jnp.where(mrow == 1, cnt_row, 0.0)).astype(I32)
    fill[...] = filled + cnt_row
    counts_ref[...] = (filled + cnt_row).astype(I32)

    srow = lax.broadcasted_iota(I32, (SORT_ROWS, tm), 0).astype(F32)
    perm = jnp.where((srow == key1) | (srow == key2), 1.0, 0.0).astype(BF16)
    def wait_tile(which):
        pltpu.make_async_copy(xbuf.at[which], xs_hbm.at[pl.ds(0, SORT_ROWS * ROW_WORDS), :],
                              sem.at[which]).wait()

    @pl.when(step > 1)
    def _():
        wait_tile(slot)

    xslot = xbuf.at[slot]
    for w in range(ROW_WORDS):
        pair = _dot(perm, h2b[:, 2 * w * LANES:(2 * w + 2) * LANES])
        _pack_rows(xslot, w, SORT_ROWS, pair[:, :LANES], pair[:, LANES:])

    meta_copy = pltpu.make_async_copy(meta_ref.at[0], meta_s.at[slot], sem_s)
    meta_copy.start()
    meta_copy.wait()

    @pl.when(step == last)
    def _():
        send_tile(slot)

        @pl.when(step > 0)
        def _():
            wait_tile(1 - slot)

        wait_tile(slot)
        zbuf[...] = jnp.zeros_like(zbuf)

        def pad_copy(e):
            end = e * region + meta_s[slot, 0, e] + meta_s[slot, 1, e]
            return pltpu.make_async_copy(
                zbuf, xs_hbm.at[pl.ds(end * ROW_WORDS, MOE_BLOCK * ROW_WORDS), :], sem_s)

        def start_pad(e, carry):
            pad_copy(e).start()
            return carry

        def wait_pad(e, carry):
            pad_copy(e).wait()
            return carry

        lax.fori_loop(0, N_EXPERTS, start_pad, 0)
        lax.fori_loop(0, N_EXPERTS, wait_pad, 0)


def _mixtail(u, att, gates, x, gt_m, sc_f, sh_f, g_post, g_pre,
             wpg, pscale, wbp, wba, wout, wr_hi, wr_lo, br):
    b, s, d = x.shape
    tm = ROW_TILE
    tiles = s // tm
    n_tiles = b * tiles
    halo_blocks = tm // POOL_HALO
    region = _region_rows(b * s)
    const2 = lambda bi, i: (0, 0)
    const3 = lambda bi, i: (0, 0, 0)
    per_b = lambda bi, i: (bi, 0, 0)
    tile = lambda bi, i: (bi, i, 0)
    single = dict(pipeline_mode=pl.Buffered(1))
    return pl.pallas_call(
        _mixtail_kernel,
        grid=(b, tiles),
        in_specs=[pl.BlockSpec((1, tm, POOL_WIDTH), tile),
                  pl.BlockSpec((1, POOL_HALO, POOL_WIDTH),
                               lambda bi, i: (bi, jnp.maximum(i * halo_blocks - 1, 0), 0)),
                  pl.BlockSpec((1, tm, ATT_OUT_WIDTH), tile),
                  pl.BlockSpec((1, tm, 2 * D_MODEL), tile),
                  pl.BlockSpec((1, tm, d), tile),
                  pl.BlockSpec((1, 1, d), per_b),
                  pl.BlockSpec((1, 1, d), per_b),
                  pl.BlockSpec((1, 1, d), per_b),
                  pl.BlockSpec((1, d), const2),
                  pl.BlockSpec((1, d), const2),
                  pl.BlockSpec(wpg.shape, const3, **single),
                  pl.BlockSpec((1, POOL_WIDTH), const2),
                  pl.BlockSpec(wbp.shape, const2, **single),
                  pl.BlockSpec(wba.shape, const2, **single),
                  pl.BlockSpec(wout.shape, const2, **single),
                  pl.BlockSpec(wr_hi.shape, const2, **single),
                  pl.BlockSpec(wr_lo.shape, const2, **single),
                  pl.BlockSpec(br.shape, const2)],
        out_specs=[pl.BlockSpec((1, tm, d), tile),
                   pl.BlockSpec((SUBLANES, tm), lambda bi, i: (0, bi * tiles + i)),
                   pl.BlockSpec((1, SUBLANES, LANES), lambda bi, i: (bi * tiles + i, 0, 0)),
                   pl.BlockSpec((SUBLANES, LANES), const2),
                   pl.BlockSpec(memory_space=pl.ANY)],
        out_shape=[jax.ShapeDtypeStruct((b, s, d), F32),
                   jax.ShapeDtypeStruct((SUBLANES, b * s), F32),
                   jax.ShapeDtypeStruct((n_tiles, SUBLANES, LANES), I32),
                   jax.ShapeDtypeStruct((SUBLANES, LANES), I32),
                   jax.ShapeDtypeStruct((N_EXPERTS * region * ROW_WORDS, LANES), I32)],
        scratch_shapes=[pltpu.VMEM((POOL_HALO + tm, POOL_WIDTH), F32),
                        pltpu.VMEM((2, SORT_ROWS * ROW_WORDS, LANES), I32),
                        pltpu.VMEM((MOE_BLOCK * ROW_WORDS, LANES), I32),
                        pltpu.VMEM((SUBLANES, LANES), F32),
                        pltpu.SMEM((2, SUBLANES, LANES), I32),
                        pltpu.SemaphoreType.DMA((2,)),
                        pltpu.SemaphoreType.DMA(())],
        compiler_params=pltpu.CompilerParams(
            dimension_semantics=("arbitrary", "arbitrary"), vmem_limit_bytes=VMEM_LIMIT),
        name="mixtail",
    )(u, u, att, gates, x, gt_m, sc_f, sh_f, g_post, g_pre,
      wpg, pscale, wbp, wba, wout, wr_hi, wr_lo, br)


def _expert_kernel(counts_ref, xs_hbm, wg_ref, wu_ref, wd_ref, ys_hbm,
                   wg_bf, wu_bf, wd_bf, xbuf, ybuf, state, semx, semy):
    e = pl.program_id(0)
    bm = MOE_BLOCK
    nx, ny = EXPERT_X_BUFFERS, EXPERT_Y_BUFFERS
    block_words = bm * ROW_WORDS
    region = xs_hbm.shape[0] // (N_EXPERTS * ROW_WORDS)

    def n_blocks(ex):
        return (counts_ref[ex] + (bm - 1)) // bm

    def block_rows(ref, ex, k):
        start = pl.multiple_of((ex * region + k * bm) * ROW_WORDS, block_words)
        return ref.at[pl.ds(start, block_words), :]

    def x_copy(ex, k, s):
        return pltpu.make_async_copy(block_rows(xs_hbm, ex, k), xbuf.at[s], semx.at[s])

    def y_copy(k, s):
        return pltpu.make_async_copy(ybuf.at[s], block_rows(ys_hbm, e, k), semy.at[s])

    @pl.when(e == 0)
    def _():
        for j in range(4):
            state[j] = 0

    def fetch_through(target):
        def more(c):
            pe, _, pg = c
            return (pg < target) & (pe < N_EXPERTS)

        def step(c):
            pe, pk, pg = c
            has = pk < n_blocks(pe)

            @pl.when(has)
            def _():
                x_copy(pe, pk, pg % nx).start(priority=BLOCK_DMA_PRIORITY)

            return (jnp.where(has, pe, pe + 1), jnp.where(has, pk + 1, 0), pg + has.astype(I32))

        pe, pk, pg = lax.while_loop(more, step, (state[1], state[2], state[3]))
        state[1] = pe
        state[2] = pk
        state[3] = pg

    wg_bf[...] = wg_ref[...].astype(BF16)
    wu_bf[...] = wu_ref[...].astype(BF16)
    wd_bf[...] = wd_ref[...].astype(BF16)

    nb = n_blocks(e)
    done = state[0]

    def block(k, carry):
        g = done + k
        fetch_through(g + nx)
        sx = g % nx
        sy = g % ny
        x_copy(e, k, sx).wait()
        x = _unpack_rows(xbuf.at[sx], bm).astype(BF16)
        a = _dot(x, wg_bf[...])
        u = _dot(x, wu_bf[...])
        mid = ((a * _sigmoid(a)) * u).astype(BF16)

        @pl.when(g >= ny)
        def _():
            y_copy(k, sy).wait()

        for w in range(ROW_WORDS):
            pair = _bf16_exact(_dot(mid, wd_bf[:, 2 * w * LANES:(2 * w + 2) * LANES]))
            _pack_rows(ybuf.at[sy], w, bm, pair[:, :LANES], pair[:, LANES:])
        y_copy(k, sy).start(priority=BLOCK_DMA_PRIORITY)
        return carry

    lax.fori_loop(0, nb, block, 0)
    state[0] = done + nb

    @pl.when(e == N_EXPERTS - 1)
    def _():
        total = done + nb
        for j in range(ny):
            @pl.when(total > j)
            def _():
                y_copy(0, (total - 1 - j) % ny).wait()


def _experts(xs, counts, w_gate, w_up, w_down):
    bm = MOE_BLOCK
    w_in = pl.BlockSpec((None, D_MODEL, D_EXPERT), lambda e, cnt: (e, 0, 0))
    w_out = pl.BlockSpec((None, D_EXPERT, D_MODEL), lambda e, cnt: (e, 0, 0))
    grid_spec = pltpu.PrefetchScalarGridSpec(
        num_scalar_prefetch=1,
        grid=(N_EXPERTS,),
        in_specs=[pl.BlockSpec(memory_space=pl.ANY), w_in, w_in, w_out],
        out_specs=pl.BlockSpec(memory_space=pl.ANY),
        scratch_shapes=[pltpu.VMEM((D_MODEL, D_EXPERT), BF16),
                        pltpu.VMEM((D_MODEL, D_EXPERT), BF16),
                        pltpu.VMEM((D_EXPERT, D_MODEL), BF16),
                        pltpu.VMEM((EXPERT_X_BUFFERS, bm * ROW_WORDS, LANES), I32),
                        pltpu.VMEM((EXPERT_Y_BUFFERS, bm * ROW_WORDS, LANES), I32),
                        pltpu.SMEM((4,), I32),
                        pltpu.SemaphoreType.DMA((EXPERT_X_BUFFERS,)),
                        pltpu.SemaphoreType.DMA((EXPERT_Y_BUFFERS,))],
    )
    return pl.pallas_call(
        _expert_kernel,
        grid_spec=grid_spec,
        out_shape=jax.ShapeDtypeStruct(xs.shape, I32),
        compiler_params=pltpu.CompilerParams(
            dimension_semantics=("arbitrary",), vmem_limit_bytes=VMEM_LIMIT),
        name="experts",
    )(counts, xs, w_gate, w_up, w_down)


def _combine_kernel(meta_ref, y_hbm, route_ref, xmid_ref, gt_ref, g_ref, o_ref, ybuf, rt_scr, sem):
    step = pl.program_id(0)
    n_steps = pl.num_programs(0)
    tm = ROW_TILE
    slot = step % COMBINE_BUFFERS
    region = _region_rows(n_steps * tm)

    def fetch_tile(tile, live):
        tile = jnp.minimum(tile, n_steps - 1)
        which = tile % COMBINE_BUFFERS
        _tile_run_copies(lambda e: jnp.where(live, meta_ref[(tile * 2 + 1) * N_EXPERTS + e], 0),
                         y_hbm, lambda e: e * region + meta_ref[(tile * 2) * N_EXPERTS + e],
                         ybuf.at[which], None, sem.at[which])

    @pl.when(step == 0)
    def _():
        for ahead in range(COMBINE_BUFFERS - 1):
            fetch_tile(ahead, ahead < n_steps)

    pltpu.make_async_copy(y_hbm.at[pl.ds(0, SORT_ROWS * ROW_WORDS), :], ybuf.at[slot], sem.at[slot]).wait()
    yb = _unpack_rows(ybuf.at[slot], SORT_ROWS).astype(BF16)

    fetch_tile(step + COMBINE_BUFFERS - 1, step + COMBINE_BUFFERS - 1 < n_steps)

    rt_scr[...] = jnp.zeros_like(rt_scr)
    rt_scr[0:SUBLANES, :] = route_ref[...]
    cols = jnp.concatenate([rt_scr[:, c * LANES:(c + 1) * LANES].T for c in range(tm // LANES)], axis=0)
    key1, key2, w1, w2 = cols[:, 0:1], cols[:, 1:2], cols[:, 2:3], cols[:, 3:4]

    spos = lax.broadcasted_iota(I32, (tm, SORT_ROWS), 1).astype(F32)
    y1 = _dot(jnp.where(spos == key1, 1.0, 0.0).astype(BF16), yb)
    y2 = _dot(jnp.where(spos == key2, 1.0, 0.0).astype(BF16), yb)
    y = w1 * y1 + w2 * y2
    o_ref[...] = xmid_ref[...] + gt_ref[0] * _rmsnorm(y, g_ref[...])


def _combine(meta, ys, route, x_mid, gt_f, g_post, seq):
    t, d = x_mid.shape
    tm = ROW_TILE
    tiles_per_seq = seq // tm
    grid_spec = pltpu.PrefetchScalarGridSpec(
        num_scalar_prefetch=1,
        grid=(t // tm,),
        in_specs=[pl.BlockSpec(memory_space=pl.ANY),
                  pl.BlockSpec((SUBLANES, tm), lambda i, m: (0, i)),
                  pl.BlockSpec((tm, d), lambda i, m: (i, 0)),
                  pl.BlockSpec((1, 1, d), lambda i, m: (i // tiles_per_seq, 0, 0)),
                  pl.BlockSpec((1, d), lambda i, m: (0, 0))],
        out_specs=pl.BlockSpec((tm, d), lambda i, m: (i, 0)),
        scratch_shapes=[pltpu.VMEM((COMBINE_BUFFERS, SORT_ROWS * ROW_WORDS, LANES), I32),
                        pltpu.VMEM((LANES, tm), F32),
                        pltpu.SemaphoreType.DMA((COMBINE_BUFFERS,))],
    )
    return pl.pallas_call(
        _combine_kernel,
        grid_spec=grid_spec,
        out_shape=jax.ShapeDtypeStruct((t, d), F32),
        compiler_params=pltpu.CompilerParams(
            dimension_semantics=("arbitrary",), vmem_limit_bytes=VMEM_LIMIT),
        name="combine",
    )(meta, ys, route, x_mid, gt_f, g_post)


def kernel(x, c, w_ada, b_ada, g_pre_mix, g_post_mix, g_pre_ffn, g_post_ffn, w_in, w_pool_group, pool_scale, w_branch_pool, w_branch_att, w_out, w_group_router, b_group_router, w_expert_router, b_expert_router, w_exp_gate, w_exp_up, w_exp_down):
    b, s, d = x.shape
    assert d == D_MODEL and s % (ATT_BLOCK * 2 * ATT_DILATIONS[2]) == 0 and s % ROW_TILE == 0
    assert (b * s) % MOE_BLOCK == 0
    depth = w_ada.shape[0]
    slopes = jnp.exp2(-ALIBI_MAX_BIAS * jnp.arange(1, N_ATT_HEADS + 1, dtype=F32) / N_ATT_HEADS)
    q_scale = HEAD_DIM ** -0.5

    for layer in range(depth):
        mod = _adaln(c, w_ada[layer], b_ada[layer]).reshape(b, 6, 1, d)
        sh_m, sc_m, gt_m, sh_f, sc_f, gt_f = [mod[:, j] for j in range(6)]

        wl = w_in[layer]
        q_lo, k_lo, v_lo = POOL_WIDTH, POOL_WIDTH + 768, POOL_WIDTH + 2 * 768
        group_cols = []
        for g in range(3):
            sl = slice(g * 256, (g + 1) * 256)
            group_cols += [wl[:, q_lo:k_lo][:, sl] * q_scale, wl[:, k_lo:v_lo][:, sl],
                           wl[:, v_lo:v_lo + 768][:, sl]]
        w_perm = jnp.concatenate([wl[:, :POOL_WIDTH]] + group_cols + [wl[:, v_lo + 768:]],
                                 axis=1).astype(BF16)

        u, qkv0, qkv1, qkv2, gates = _inproj(x, g_pre_mix[layer].reshape(1, d), sc_m, sh_m, w_perm)
        att = _attention(qkv0, qkv1, qkv2, slopes)

        pad_cols = LANES - N_EXPERTS - N_EXPERT_GROUPS
        wr = jnp.concatenate([w_expert_router[layer], w_group_router[layer],
                              jnp.zeros((d, pad_cols), F32)], axis=1)
        wr_hi = wr.astype(BF16)
        wr_lo = (wr - wr_hi.astype(F32)).astype(BF16)
        br = jnp.concatenate([b_expert_router[layer], b_group_router[layer],
                              jnp.zeros((pad_cols,), F32)]).reshape(1, LANES)

        x_mid, route, meta, counts, xs = _mixtail(
            u, att, gates, x, gt_m, sc_f, sh_f,
            g_post_mix[layer].reshape(1, d), g_pre_ffn[layer].reshape(1, d),
            w_pool_group[layer].astype(BF16), pool_scale[layer].reshape(1, POOL_WIDTH),
            w_branch_pool[layer].astype(BF16), w_branch_att[layer].astype(BF16),
            w_out[layer].astype(BF16), wr_hi, wr_lo, br)

        ys = _experts(xs, counts[0, :N_EXPERTS], w_exp_gate[layer], w_exp_up[layer], w_exp_down[layer])
        run_meta = meta[:, 0:2, 0:N_EXPERTS].reshape(-1)
        x = _combine(run_meta, ys, route, x_mid.reshape(b * s, d), gt_f,
                     g_post_ffn[layer].reshape(1, d), s).reshape(b, s, d)
    return x
```

```python
import functools

import jax
import jax.numpy as jnp
from jax import lax
from jax.experimental import pallas as pl
from jax.experimental.pallas import tpu as pltpu

F32 = jnp.float32
BF16 = jnp.bfloat16
I32 = jnp.int32
HIGHEST = lax.Precision.HIGHEST

D_MODEL = 1024
LANES = 128
SUBLANES = 8
ROW_WORDS = D_MODEL // (2 * LANES)
HI_HALF = -65536

POOL_WINDOWS = (2, 4, 8, 16)
POOL_GROUP_DIM = 128
POOL_WIDTH = 512
POOL_HALO = 16

HEAD_DIM = 64
ATT_DILATIONS = (1, 4, 16)
ATT_REACH = 128
ATT_BLOCK = 128
ATT_UNROLL = 4
HEADS_PER_GROUP = 4
N_ATT_HEADS = 12
GROUP_QKV = 3 * HEADS_PER_GROUP * HEAD_DIM
ATT_OUT_WIDTH = 256
ALIBI_MAX_BIAS = 8.0
IN_WIDTH = POOL_WIDTH + 3 * GROUP_QKV + 2 * D_MODEL
MASKED = -1e30
LOG2_E = 1.4426950408889634

N_EXPERT_GROUPS = 4
EXPERTS_PER_GROUP = 8
N_EXPERTS = 32
D_EXPERT = 512
RMS_EPS = 1e-6

ROW_TILE = 512
SORT_ROWS = 2 * ROW_TILE
MOE_BLOCK = 256
BLOCK_DMA_PRIORITY = 1
EXPERT_X_BUFFERS = 6
EXPERT_Y_BUFFERS = 4
COMBINE_BUFFERS = 3
VMEM_LIMIT = 52 * 1024 * 1024


def _sigmoid(x):
    return 0.5 * jnp.tanh(0.5 * x) + 0.5


def _rmsnorm(x, g):
    return x * lax.rsqrt(jnp.mean(x * x, axis=-1, keepdims=True) + RMS_EPS) * g


def _dot(a, b):
    return jnp.dot(a, b, preferred_element_type=F32)


def _dot_nt(a, b, **kw):
    return lax.dot_general(a, b, (((1,), (1,)), ((), ())), preferred_element_type=F32, **kw)


def _pack_rows(ref, w, n, lo, hi):
    word = (lax.shift_right_logical(pltpu.bitcast(lo, I32), 16) | (pltpu.bitcast(hi, I32) & HI_HALF))
    ref[pl.ds(w, n, stride=ROW_WORDS), :] = word


def _unpack_rows(ref, n):
    cols = []
    for w in range(ROW_WORDS):
        word = ref[pl.ds(w, n, stride=ROW_WORDS), :]
        cols += [pltpu.bitcast(word << 16, F32), pltpu.bitcast(word & HI_HALF, F32)]
    return jnp.concatenate(cols, axis=1)


def _bf16_exact(x):
    return x.astype(BF16).astype(F32)


def _region_rows(n_tokens):
    return n_tokens + MOE_BLOCK


def _run_copies(n, src, src_row, dst, dst_row, sem):
    for bit in range(ROW_TILE.bit_length()):
        size = 1 << bit
        above = n & ~(2 * size - 1)

        @pl.when((n & size) != 0)
        def _():
            pltpu.make_async_copy(
                src.at[pl.ds((src_row + above) * ROW_WORDS, size * ROW_WORDS), :],
                dst.at[pl.ds((dst_row + above) * ROW_WORDS, size * ROW_WORDS), :], sem).start()


def _tile_run_copies(counts, src, src_rows, dst, dst_rows, sem):
    local = 0
    for e in range(N_EXPERTS):
        n = counts(e)
        _run_copies(n, src, local if src_rows is None else src_rows(e),
                    dst, local if dst_rows is None else dst_rows(e), sem)
        local = local + n


def _adaln_kernel(c_ref, w_ref, b_ref, o_ref):
    c = c_ref[...]
    a = c * _sigmoid(c)
    o_ref[...] = jnp.dot(a, w_ref[...], preferred_element_type=F32, precision=HIGHEST) + b_ref[...]


def _adaln(c, w_ada, b_ada):
    b, d = c.shape
    n = w_ada.shape[1]
    rows = -(-b // SUBLANES) * SUBLANES
    cp = jnp.pad(c, ((0, rows - b), (0, 0)))
    nt = 1536
    out = pl.pallas_call(
        _adaln_kernel,
        grid=(n // nt,),
        in_specs=[pl.BlockSpec((rows, d), lambda j: (0, 0)),
                  pl.BlockSpec((d, nt), lambda j: (0, j)),
                  pl.BlockSpec((1, nt), lambda j: (0, j))],
        out_specs=pl.BlockSpec((rows, nt), lambda j: (0, j)),
        out_shape=jax.ShapeDtypeStruct((rows, n), F32),
        compiler_params=pltpu.CompilerParams(vmem_limit_bytes=VMEM_LIMIT),
        name="adaln",
    )(cp, w_ada, b_ada.reshape(1, n))
    return out[:b]


def _inproj_kernel(x_ref, g_ref, sc_ref, sh_ref, w_ref,
                   u_ref, qkv0_ref, qkv1_ref, qkv2_ref, gate_ref, h_scr, p_scr):
    tm = x_ref.shape[1]
    h = _rmsnorm(x_ref[0], g_ref[...]) * (1.0 + sc_ref[0]) + sh_ref[0]
    h_scr[...] = h.astype(BF16)

    u_ref[0] = _dot(h_scr[...], w_ref[:, 0:POOL_WIDTH]).astype(BF16)

    col = POOL_WIDTH
    qkv0_ref[0, 0] = _dot(h_scr[...], w_ref[:, col:col + GROUP_QKV]).astype(BF16)
    for out_ref, d in ((qkv1_ref, ATT_DILATIONS[1]), (qkv2_ref, ATT_DILATIONS[2])):
        col += GROUP_QKV
        proj = _dot(h_scr[...], w_ref[:, col:col + GROUP_QKV])
        for cb in range(GROUP_QKV // LANES):
            p_scr[cb] = proj[:, cb * LANES:(cb + 1) * LANES]
        for r in range(d):
            out_ref[0, r] = jnp.concatenate(
                [p_scr[cb, pl.ds(r, tm // d, stride=d), :] for cb in range(GROUP_QKV // LANES)],
                axis=1).astype(BF16)
    col += GROUP_QKV

    chunk = 512
    for j in range(2 * D_MODEL // chunk):
        g = _dot(h_scr[...], w_ref[:, col + j * chunk:col + (j + 1) * chunk])
        gate_ref[0, :, j * chunk:(j + 1) * chunk] = _sigmoid(g).astype(BF16)


def _inproj(x, g_pre, sc, sh, w_perm):
    b, s, d = x.shape
    tm = ROW_TILE
    d1, d2 = ATT_DILATIONS[1], ATT_DILATIONS[2]
    grid = (b, s // tm)
    const = lambda bi, i: (0, 0)
    per_b = lambda bi, i: (bi, 0, 0)
    return pl.pallas_call(
        _inproj_kernel,
        grid=grid,
        in_specs=[pl.BlockSpec((1, tm, d), lambda bi, i: (bi, i, 0)),
                  pl.BlockSpec((1, d), const),
                  pl.BlockSpec((1, 1, d), per_b),
                  pl.BlockSpec((1, 1, d), per_b),
                  pl.BlockSpec((d, IN_WIDTH), const, pipeline_mode=pl.Buffered(1))],
        out_specs=[pl.BlockSpec((1, tm, POOL_WIDTH), lambda bi, i: (bi, i, 0)),
                   pl.BlockSpec((1, 1, tm, GROUP_QKV), lambda bi, i: (bi, 0, i, 0)),
                   pl.BlockSpec((1, d1, tm // d1, GROUP_QKV), lambda bi, i: (bi, 0, i, 0)),
                   pl.BlockSpec((1, d2, tm // d2, GROUP_QKV), lambda bi, i: (bi, 0, i, 0)),
                   pl.BlockSpec((1, tm, 2 * D_MODEL), lambda bi, i: (bi, i, 0))],
        out_shape=[jax.ShapeDtypeStruct((b, s, POOL_WIDTH), BF16),
                   jax.ShapeDtypeStruct((b, 1, s, GROUP_QKV), BF16),
                   jax.ShapeDtypeStruct((b, d1, s // d1, GROUP_QKV), BF16),
                   jax.ShapeDtypeStruct((b, d2, s // d2, GROUP_QKV), BF16),
                   jax.ShapeDtypeStruct((b, s, 2 * D_MODEL), BF16)],
        scratch_shapes=[pltpu.VMEM((tm, d), BF16), pltpu.VMEM((GROUP_QKV // LANES, tm, LANES), F32)],
        compiler_params=pltpu.CompilerParams(
            dimension_semantics=("arbitrary", "arbitrary"), vmem_limit_bytes=VMEM_LIMIT),
        name="inproj",
    )(x, g_pre, sc, sh, w_perm)


def _attn_kernel(slopes_ref, q0, k0, v0, q1, k1, v1, q2, k2, v2, o_ref,
                 bias_scr, acc, mst, lst, s_even, s_odd):
    pair = pl.program_id(1)
    seq = o_ref.shape[1]
    nblk = seq // ATT_BLOCK
    lane = lax.broadcasted_iota(I32, (ATT_BLOCK, LANES), 1)
    first_head = lane < HEAD_DIM
    half_lane = lax.broadcasted_iota(I32, (ATT_BLOCK // 2, LANES), 1)
    head_bits = (jnp.where(half_lane < HEAD_DIM, -1, 0), jnp.where(half_lane < HEAD_DIM, 0, -1))

    qi = lax.broadcasted_iota(I32, (ATT_BLOCK, 2 * ATT_BLOCK), 0)
    kj = lax.broadcasted_iota(I32, (ATT_BLOCK, 2 * ATT_BLOCK), 1)
    delta = ATT_BLOCK + qi - kj
    valid = (delta >= 0) & (delta <= ATT_REACH)
    delta0 = qi - kj
    valid0 = delta0 >= 0
    for g, d in enumerate(ATT_DILATIONS):
        for j in range(2):
            slope = slopes_ref[g * HEADS_PER_GROUP + 2 * pair + j]
            slope = slope * LOG2_E
            bias_scr[g, j, 0] = jnp.where(valid0, -slope * (delta0 * d).astype(F32), MASKED)
            bias_scr[g, j, 1] = jnp.where(valid, -slope * (delta * d).astype(F32), MASKED)

    def run_group(g, q_ref, k_ref, v_ref):
        d = ATT_DILATIONS[g]
        per_res = nblk // d
        n_iter = nblk // ATT_UNROLL

        def block_index(it, k):
            n = it * ATT_UNROLL + k
            return n // per_res, n % per_res

        def key_rows(i):
            lo = jnp.maximum(i - 1, 0)
            return pl.ds(pl.multiple_of(lo * ATT_BLOCK, ATT_BLOCK), 2 * ATT_BLOCK)

        def scores(it, s_ref):
            for k in range(ATT_UNROLL):
                r, i = block_index(it, k)
                q = q_ref[0, r, pl.ds(pl.multiple_of(i * ATT_BLOCK, ATT_BLOCK), ATT_BLOCK), :]
                kw = k_ref[0, r, key_rows(i), :]
                qbits = pltpu.bitcast(q, I32)
                for j in range(2):
                    qh = pltpu.bitcast(qbits & head_bits[j], BF16)
                    s_ref[2 * k + j] = _dot_nt(qh, kw) + bias_scr[g, j, jnp.minimum(i, 1)]

        def weighted_values(it, s_ref):
            for k in range(ATT_UNROLL):
                r, i = block_index(it, k)
                vw = v_ref[0, r, key_rows(i), :]
                outs = []
                for j in range(2):
                    sc = s_ref[2 * k + j]
                    mx = jnp.max(sc, axis=-1, keepdims=True)
                    p = jnp.exp2(sc - mx)
                    den = jnp.sum(p, axis=-1, keepdims=True)
                    outs.append((_dot(p.astype(BF16), vw), mx, den))
                (n0, m0, l0), (n1, m1, l1) = outs
                if d == 1:
                    rows = pl.ds(pl.multiple_of(i * ATT_BLOCK, ATT_BLOCK), ATT_BLOCK)
                else:
                    rows = pl.ds(i * (ATT_BLOCK * d) + r, ATT_BLOCK, stride=d)
                acc[g, rows, :] = jnp.where(first_head, n0, n1)
                mst[g, rows, :] = jnp.where(first_head, m0, m1)
                lst[g, rows, :] = jnp.where(first_head, l0, l1)

        scores(0, s_even)

        def body(h, carry):
            it = 2 * h
            scores(it + 1, s_odd)
            weighted_values(it, s_even)
            scores(jnp.minimum(it + 2, n_iter - 1), s_even)
            weighted_values(it + 1, s_odd)
            return carry

        lax.fori_loop(0, n_iter // 2, body, 0)

    run_group(0, q0, k0, v0)
    run_group(1, q1, k1, v1)
    run_group(2, q2, k2, v2)

    def finish(n, carry):
        rows = pl.ds(pl.multiple_of(n * ATT_BLOCK, ATT_BLOCK), ATT_BLOCK)
        ms = [mst[g, rows, :] for g in range(3)]
        top = jnp.maximum(jnp.maximum(ms[0], ms[1]), ms[2])
        scale = [jnp.exp2(m - top) for m in ms]
        num = scale[0] * acc[0, rows, :] + scale[1] * acc[1, rows, :] + scale[2] * acc[2, rows, :]
        den = scale[0] * lst[0, rows, :] + scale[1] * lst[1, rows, :] + scale[2] * lst[2, rows, :]
        o_ref[0, rows, :] = (num / den).astype(BF16)
        return carry

    lax.fori_loop(0, nblk, finish, 0)


def _attention(qkv0, qkv1, qkv2, slopes):
    b, _, s, _ = qkv0.shape
    pairs = HEADS_PER_GROUP // 2
    col_blocks = HEADS_PER_GROUP * HEAD_DIM // LANES

    def specs(arr):
        _, d, sub, _ = arr.shape
        return [pl.BlockSpec((1, d, sub, LANES),
                             functools.partial(lambda bi, p, sec: (bi, 0, 0, sec * col_blocks + p), sec=sec))
                for sec in range(3)]

    return pl.pallas_call(
        _attn_kernel,
        grid=(b, pairs),
        in_specs=[pl.BlockSpec(memory_space=pltpu.SMEM)] + specs(qkv0) + specs(qkv1) + specs(qkv2),
        out_specs=pl.BlockSpec((1, s, LANES), lambda bi, p: (bi, 0, p)),
        out_shape=jax.ShapeDtypeStruct((b, s, ATT_OUT_WIDTH), BF16),
        scratch_shapes=[pltpu.VMEM((3, 2, 2, ATT_BLOCK, 2 * ATT_BLOCK), F32),
                        pltpu.VMEM((3, s, LANES), F32),
                        pltpu.VMEM((3, s, LANES), F32),
                        pltpu.VMEM((3, s, LANES), F32),
                        pltpu.VMEM((2 * ATT_UNROLL, ATT_BLOCK, 2 * ATT_BLOCK), F32),
                        pltpu.VMEM((2 * ATT_UNROLL, ATT_BLOCK, 2 * ATT_BLOCK), F32)],
        compiler_params=pltpu.CompilerParams(
            dimension_semantics=("arbitrary", "arbitrary"), vmem_limit_bytes=VMEM_LIMIT),
        name="attention",
    )(slopes, qkv0, qkv0, qkv0, qkv1, qkv1, qkv1, qkv2, qkv2, qkv2)


def _mixtail_kernel(u_ref, halo_ref, att_ref, gate_ref, x_ref,
                    gt_m_ref, sc_f_ref, sh_f_ref, g_post_ref, g_pre_ref,
                    wpg_ref, pscale_ref, wbp_ref, wba_ref, wout_ref, wrh_ref, wrl_ref, br_ref,
                    xmid_ref, route_ref, meta_ref, counts_ref, xs_hbm,
                    pu, xbuf, zbuf, fill, meta_s, sem, sem_s):
    i = pl.program_id(1)
    tm = x_ref.shape[1]
    step = pl.program_id(0) * pl.num_programs(1) + i
    last = pl.num_programs(0) * pl.num_programs(1) - 1
    slot = step % 2
    region = _region_rows(pl.num_programs(0) * pl.num_programs(1) * tm)

    @pl.when(step == 0)
    def _():
        fill[...] = jnp.zeros_like(fill)
        for e in range(N_EXPERTS):
            meta_s[1, 1, e] = 0
            meta_s[1, 0, e] = 0

    def send_tile(which):
        _tile_run_copies(lambda e: meta_s[which, 1, e], xbuf.at[which], None,
                         xs_hbm, lambda e: e * region + meta_s[which, 0, e], sem.at[which])

    send_tile(1 - slot)

    halo = halo_ref[0].astype(F32)
    pu[0:POOL_HALO, :] = jnp.where(i > 0, halo, jnp.zeros_like(halo))
    pu[POOL_HALO:POOL_HALO + tm, :] = u_ref[0].astype(F32)
    t = i * tm + lax.broadcasted_iota(I32, (tm, 1), 0)
    mixed = []
    for g, w in enumerate(POOL_WINDOWS):
        cols = slice(g * POOL_GROUP_DIM, (g + 1) * POOL_GROUP_DIM)
        win = pu[pl.ds(POOL_HALO, tm), cols]
        for j in range(1, w):
            win = win + pu[pl.ds(POOL_HALO - j, tm), cols]
        count = jnp.minimum(t + 1, w).astype(F32)
        pooled = win / count - pu[pl.ds(POOL_HALO, tm), cols]
        mixed.append((_dot(pooled.astype(BF16), wpg_ref[g]) * pscale_ref[:, cols]).astype(BF16))
    y_pool = _dot(jnp.concatenate(mixed, axis=1), wbp_ref[...])

    y_att = _dot(att_ref[0], wba_ref[...])
    merged = (gate_ref[0, :, 0:D_MODEL].astype(F32) * y_pool
              + gate_ref[0, :, D_MODEL:2 * D_MODEL].astype(F32) * y_att)
    y = _dot(merged.astype(BF16), wout_ref[...])
    x_mid = x_ref[0] + gt_m_ref[0] * _rmsnorm(y, g_post_ref[...])
    xmid_ref[0] = x_mid

    h2 = _rmsnorm(x_mid, g_pre_ref[...]) * (1.0 + sc_f_ref[0]) + sh_f_ref[0]

    h2b = h2.astype(BF16)
    h2lo = (h2 - h2b.astype(F32)).astype(BF16)
    by_token = (_dot(h2b, wrh_ref[...]) + (_dot(h2b, wrl_ref[...]) + _dot(h2lo, wrh_ref[...]))
                + br_ref[...])
    logits = jnp.concatenate([by_token[c * LANES:(c + 1) * LANES, :].T for c in range(tm // LANES)],
                             axis=1)
    neg_inf = -jnp.inf
    far = float(LANES)
    gl = logits[N_EXPERTS:N_EXPERTS + N_EXPERT_GROUPS, :]
    grow = lax.broadcasted_iota(I32, gl.shape, 0).astype(F32)
    gmax = jnp.max(gl, axis=0, keepdims=True)
    gsel = jnp.min(jnp.where(gl == gmax, grow, far), axis=0, keepdims=True)
    p_group = 1.0 / jnp.sum(jnp.exp(gl - gmax), axis=0, keepdims=True)
    erow = lax.broadcasted_iota(I32, (N_EXPERTS, tm), 0).astype(F32)
    e_lo = gsel * float(EXPERTS_PER_GROUP)
    el = jnp.where((erow >= e_lo) & (erow < e_lo + float(EXPERTS_PER_GROUP)), logits[0:N_EXPERTS, :], neg_inf)
    v1 = jnp.max(el, axis=0, keepdims=True)
    i1 = jnp.min(jnp.where(el == v1, erow, far), axis=0, keepdims=True)
    el2 = jnp.where(erow == i1, neg_inf, el)
    v2 = jnp.max(el2, axis=0, keepdims=True)
    i2 = jnp.min(jnp.where(el2 == v2, erow, far), axis=0, keepdims=True)
    e21 = jnp.exp(v2 - v1)
    w1 = p_group / (1.0 + e21)
    w2 = p_group * e21 / (1.0 + e21)

    pick1 = erow == i1
    pick2 = erow == i2
    assign = jnp.where(pick1 | pick2, 1.0, 0.0)
    before = (lax.broadcasted_iota(I32, (tm, tm), 0) < lax.broadcasted_iota(I32, (tm, tm), 1))
    rank = _dot(assign.astype(BF16), jnp.where(before, 1.0, 0.0).astype(BF16))
    assign_pad = jnp.concatenate([assign, jnp.zeros((LANES - N_EXPERTS, tm), F32)], axis=0).astype(BF16)
    cnt_row = _dot_nt(jnp.ones((SUBLANES, tm), BF16), assign_pad)
    e_col = lax.broadcasted_iota(I32, (N_EXPERTS, LANES), 0)
    e_lane = lax.broadcasted_iota(I32, (N_EXPERTS, LANES), 1)
    run_start = jnp.sum(jnp.where(e_lane < e_col, cnt_row[0:1, :], 0.0), axis=1, keepdims=True)
    pos = rank + run_start
    key1 = jnp.sum(jnp.where(pick1, pos, 0.0), axis=0, keepdims=True)
    key2 = jnp.sum(jnp.where(pick2, pos, 0.0), axis=0, keepdims=True)

    row8 = lax.broadcasted_iota(I32, (SUBLANES, tm), 0)
    route_ref[...] = jnp.where(row8 == 0, key1, jnp.where(row8 == 1, key2,
                               jnp.where(row8 == 2, w1, jnp.where(row8 == 3, w2, 0.0))))

    filled = fill[...]
    mrow = lax.broadcasted_iota(I32, (SUBLANES, LANES), 0)
    meta_ref[0] = jnp.where(mrow == 0, filled, jnp.where(mrow == 1, cnt_row, 0.0)).astype(I32)
    fill[...] = filled + cnt_row
    counts_ref[...] = (filled + cnt_row).astype(I32)

    srow = lax.broadcasted_iota(I32, (SORT_ROWS, tm), 0).astype(F32)
    perm = jnp.where((srow == key1) | (srow == key2), 1.0, 0.0).astype(BF16)
    def wait_tile(which):
        pltpu.make_async_copy(xbuf.at[which], xs_hbm.at[pl.ds(0, SORT_ROWS * ROW_WORDS), :],
                              sem.at[which]).wait()

    @pl.when(step > 1)
    def _():
        wait_tile(slot)

    xslot = xbuf.at[slot]
    for w in range(ROW_WORDS):
        pair = _dot(perm, h2b[:, 2 * w * LANES:(2 * w + 2) * LANES])
        _pack_rows(xslot, w, SORT_ROWS, pair[:, :LANES], pair[:, LANES:])

    meta_copy = pltpu.make_async_copy(meta_ref.at[0], meta_s.at[slot], sem_s)
    meta_copy.start()
    meta_copy.wait()

    @pl.when(step == last)
    def _():
        send_tile(slot)

        @pl.when(step > 0)
        def _():
            wait_tile(1 - slot)

        wait_tile(slot)
        zbuf[...] = jnp.zeros_like(zbuf)

        def pad_copy(e):
            end = e * region + meta_s[slot, 0, e] + meta_s[slot, 1, e]
            return pltpu.make_async_copy(
                zbuf, xs_hbm.at[pl.ds(end * ROW_WORDS, MOE_BLOCK * ROW_WORDS), :], sem_s)

        def start_pad(e, carry):
            pad_copy(e).start()
            return carry

        def wait_pad(e, carry):
            pad_copy(e).wait()
            return carry

        lax.fori_loop(0, N_EXPERTS, start_pad, 0)
        lax.fori_loop(0, N_EXPERTS, wait_pad, 0)


def _mixtail(u, att, gates, x, gt_m, sc_f, sh_f, g_post, g_pre,
             wpg, pscale, wbp, wba, wout, wr_hi, wr_lo, br):
    b, s, d = x.shape
    tm = ROW_TILE
    tiles = s // tm
    n_tiles = b * tiles
    halo_blocks = tm // POOL_HALO
    region = _region_rows(b * s)
    const2 = lambda bi, i: (0, 0)
    const3 = lambda bi, i: (0, 0, 0)
    per_b = lambda bi, i: (bi, 0, 0)
    tile = lambda bi, i: (bi, i, 0)
    single = dict(pipeline_mode=pl.Buffered(1))
    return pl.pallas_call(
        _mixtail_kernel,
        grid=(b, tiles),
        in_specs=[pl.BlockSpec((1, tm, POOL_WIDTH), tile),
                  pl.BlockSpec((1, POOL_HALO, POOL_WIDTH),
                               lambda bi, i: (bi, jnp.maximum(i * halo_blocks - 1, 0), 0)),
                  pl.BlockSpec((1, tm, ATT_OUT_WIDTH), tile),
                  pl.BlockSpec((1, tm, 2 * D_MODEL), tile),
                  pl.BlockSpec((1, tm, d), tile),
                  pl.BlockSpec((1, 1, d), per_b),
                  pl.BlockSpec((1, 1, d), per_b),
                  pl.BlockSpec((1, 1, d), per_b),
                  pl.BlockSpec((1, d), const2),
                  pl.BlockSpec((1, d), const2),
                  pl.BlockSpec(wpg.shape, const3, **single),
                  pl.BlockSpec((1, POOL_WIDTH), const2),
                  pl.BlockSpec(wbp.shape, const2, **single),
                  pl.BlockSpec(wba.shape, const2, **single),
                  pl.BlockSpec(wout.shape, const2, **single),
                  pl.BlockSpec(wr_hi.shape, const2, **single),
                  pl.BlockSpec(wr_lo.shape, const2, **single),
                  pl.BlockSpec(br.shape, const2)],
        out_specs=[pl.BlockSpec((1, tm, d), tile),
                   pl.BlockSpec((SUBLANES, tm), lambda bi, i: (0, bi * tiles + i)),
                   pl.BlockSpec((1, SUBLANES, LANES), lambda bi, i: (bi * tiles + i, 0, 0)),
                   pl.BlockSpec((SUBLANES, LANES), const2),
                   pl.BlockSpec(memory_space=pl.ANY)],
        out_shape=[jax.ShapeDtypeStruct((b, s, d), F32),
                   jax.ShapeDtypeStruct((SUBLANES, b * s), F32),
                   jax.ShapeDtypeStruct((n_tiles, SUBLANES, LANES), I32),
                   jax.ShapeDtypeStruct((SUBLANES, LANES), I32),
                   jax.ShapeDtypeStruct((N_EXPERTS * region * ROW_WORDS, LANES), I32)],
        scratch_shapes=[pltpu.VMEM((POOL_HALO + tm, POOL_WIDTH), F32),
                        pltpu.VMEM((2, SORT_ROWS * ROW_WORDS, LANES), I32),
                        pltpu.VMEM((MOE_BLOCK * ROW_WORDS, LANES), I32),
                        pltpu.VMEM((SUBLANES, LANES), F32),
                        pltpu.SMEM((2, SUBLANES, LANES), I32),
                        pltpu.SemaphoreType.DMA((2,)),
                        pltpu.SemaphoreType.DMA(())],
        compiler_params=pltpu.CompilerParams(
            dimension_semantics=("arbitrary", "arbitrary"), vmem_limit_bytes=VMEM_LIMIT),
        name="mixtail",
    )(u, u, att, gates, x, gt_m, sc_f, sh_f, g_post, g_pre,
      wpg, pscale, wbp, wba, wout, wr_hi, wr_lo, br)


def _expert_kernel(counts_ref, xs_hbm, wg_ref, wu_ref, wd_ref, ys_hbm,
                   wg_bf, wu_bf, wd_bf, xbuf, ybuf, state, semx, semy):
    e = pl.program_id(0)
    bm = MOE_BLOCK
    nx, ny = EXPERT_X_BUFFERS, EXPERT_Y_BUFFERS
    block_words = bm * ROW_WORDS
    region = xs_hbm.shape[0] // (N_EXPERTS * ROW_WORDS)

    def n_blocks(ex):
        return (counts_ref[ex] + (bm - 1)) // bm

    def block_rows(ref, ex, k):
        start = pl.multiple_of((ex * region + k * bm) * ROW_WORDS, block_words)
        return ref.at[pl.ds(start, block_words), :]

    def x_copy(ex, k, s):
        return pltpu.make_async_copy(block_rows(xs_hbm, ex, k), xbuf.at[s], semx.at[s])

    def y_copy(k, s):
        return pltpu.make_async_copy(ybuf.at[s], block_rows(ys_hbm, e, k), semy.at[s])

    @pl.when(e == 0)
    def _():
        for j in range(4):
            state[j] = 0

    def fetch_through(target):
        def more(c):
            pe, _, pg = c
            return (pg < target) & (pe < N_EXPERTS)

        def step(c):
            pe, pk, pg = c
            has = pk < n_blocks(pe)

            @pl.when(has)
            def _():
                x_copy(pe, pk, pg % nx).start(priority=BLOCK_DMA_PRIORITY)

            return (jnp.where(has, pe, pe + 1), jnp.where(has, pk + 1, 0), pg + has.astype(I32))

        pe, pk, pg = lax.while_loop(more, step, (state[1], state[2], state[3]))
        state[1] = pe
        state[2] = pk
        state[3] = pg

    wg_bf[...] = wg_ref[...].astype(BF16)
    wu_bf[...] = wu_ref[...].astype(BF16)
    wd_bf[...] = wd_ref[...].astype(BF16)

    nb = n_blocks(e)
    done = state[0]

    def block(k, carry):
        g = done + k
        fetch_through(g + nx)
        sx = g % nx
        sy = g % ny
        x_copy(e, k, sx).wait()
        x = _unpack_rows(xbuf.at[sx], bm).astype(BF16)
        a = _dot(x, wg_bf[...])
        u = _dot(x, wu_bf[...])
        mid = ((a * _sigmoid(a)) * u).astype(BF16)

        @pl.when(g >= ny)
        def _():
            y_copy(k, sy).wait()

        for w in range(ROW_WORDS):
            pair = _bf16_exact(_dot(mid, wd_bf[:, 2 * w * LANES:(2 * w + 2) * LANES]))
            _pack_rows(ybuf.at[sy], w, bm, pair[:, :LANES], pair[:, LANES:])
        y_copy(k, sy).start(priority=BLOCK_DMA_PRIORITY)
        return carry

    lax.fori_loop(0, nb, block, 0)
    state[0] = done + nb

    @pl.when(e == N_EXPERTS - 1)
    def _():
        total = done + nb
        for j in range(ny):
            @pl.when(total > j)
            def _():
                y_copy(0, (total - 1 - j) % ny).wait()


def _experts(xs, counts, w_gate, w_up, w_down):
    bm = MOE_BLOCK
    w_in = pl.BlockSpec((None, D_MODEL, D_EXPERT), lambda e, cnt: (e, 0, 0))
    w_out = pl.BlockSpec((None, D_EXPERT, D_MODEL), lambda e, cnt: (e, 0, 0))
    grid_spec = pltpu.PrefetchScalarGridSpec(
        num_scalar_prefetch=1,
        grid=(N_EXPERTS,),
        in_specs=[pl.BlockSpec(memory_space=pl.ANY), w_in, w_in, w_out],
        out_specs=pl.BlockSpec(memory_space=pl.ANY),
        scratch_shapes=[pltpu.VMEM((D_MODEL, D_EXPERT), BF16),
                        pltpu.VMEM((D_MODEL, D_EXPERT), BF16),
                        pltpu.VMEM((D_EXPERT, D_MODEL), BF16),
                        pltpu.VMEM((EXPERT_X_BUFFERS, bm * ROW_WORDS, LANES), I32),
                        pltpu.VMEM((EXPERT_Y_BUFFERS, bm * ROW_WORDS, LANES), I32),
                        pltpu.SMEM((4,), I32),
                        pltpu.SemaphoreType.DMA((EXPERT_X_BUFFERS,)),
                        pltpu.SemaphoreType.DMA((EXPERT_Y_BUFFERS,))],
    )
    return pl.pallas_call(
        _expert_kernel,
        grid_spec=grid_spec,
        out_shape=jax.ShapeDtypeStruct(xs.shape, I32),
        compiler_params=pltpu.CompilerParams(
            dimension_semantics=("arbitrary",), vmem_limit_bytes=VMEM_LIMIT),
        name="experts",
    )(counts, xs, w_gate, w_up, w_down)


def _combine_kernel(meta_ref, y_hbm, route_ref, xmid_ref, gt_ref, g_ref, o_ref, ybuf, rt_scr, sem):
    step = pl.program_id(0)
    n_steps = pl.num_programs(0)
    tm = ROW_TILE
    slot = step % COMBINE_BUFFERS
    region = _region_rows(n_steps * tm)

    def fetch_tile(tile, live):
        tile = jnp.minimum(tile, n_steps - 1)
        which = tile % COMBINE_BUFFERS
        _tile_run_copies(lambda e: jnp.where(live, meta_ref[(tile * 2 + 1) * N_EXPERTS + e], 0),
                         y_hbm, lambda e: e * region + meta_ref[(tile * 2) * N_EXPERTS + e],
                         ybuf.at[which], None, sem.at[which])

    @pl.when(step == 0)
    def _():
        for ahead in range(COMBINE_BUFFERS - 1):
            fetch_tile(ahead, ahead < n_steps)

    pltpu.make_async_copy(y_hbm.at[pl.ds(0, SORT_ROWS * ROW_WORDS), :], ybuf.at[slot], sem.at[slot]).wait()
    yb = _unpack_rows(ybuf.at[slot], SORT_ROWS).astype(BF16)

    fetch_tile(step + COMBINE_BUFFERS - 1, step + COMBINE_BUFFERS - 1 < n_steps)

    rt_scr[...] = jnp.zeros_like(rt_scr)
    rt_scr[0:SUBLANES, :] = route_ref[...]
    cols = jnp.concatenate([rt_scr[:, c * LANES:(c + 1) * LANES].T for c in range(tm // LANES)], axis=0)
    key1, key2, w1, w2 = cols[:, 0:1], cols[:, 1:2], cols[:, 2:3], cols[:, 3:4]

    spos = lax.broadcasted_iota(I32, (tm, SORT_ROWS), 1).astype(F32)
    y1 = _dot(jnp.where(spos == key1, 1.0, 0.0).astype(BF16), yb)
    y2 = _dot(jnp.where(spos == key2, 1.0, 0.0).astype(BF16), yb)
    y = w1 * y1 + w2 * y2
    o_ref[...] = xmid_ref[...] + gt_ref[0] * _rmsnorm(y, g_ref[...])


def _combine(meta, ys, route, x_mid, gt_f, g_post, seq):
    t, d = x_mid.shape
    tm = ROW_TILE
    tiles_per_seq = seq // tm
    grid_spec = pltpu.PrefetchScalarGridSpec(
        num_scalar_prefetch=1,
        grid=(t // tm,),
        in_specs=[pl.BlockSpec(memory_space=pl.ANY),
                  pl.BlockSpec((SUBLANES, tm), lambda i, m: (0, i)),
                  pl.BlockSpec((tm, d), lambda i, m: (i, 0)),
                  pl.BlockSpec((1, 1, d), lambda i, m: (i // tiles_per_seq, 0, 0)),
                  pl.BlockSpec((1, d), lambda i, m: (0, 0))],
        out_specs=pl.BlockSpec((tm, d), lambda i, m: (i, 0)),
        scratch_shapes=[pltpu.VMEM((COMBINE_BUFFERS, SORT_ROWS * ROW_WORDS, LANES), I32),
                        pltpu.VMEM((LANES, tm), F32),
                        pltpu.SemaphoreType.DMA((COMBINE_BUFFERS,))],
    )
    return pl.pallas_call(
        _combine_kernel,
        grid_spec=grid_spec,
        out_shape=jax.ShapeDtypeStruct((t, d), F32),
        compiler_params=pltpu.CompilerParams(
            dimension_semantics=("arbitrary",), vmem_limit_bytes=VMEM_LIMIT),
        name="combine",
    )(meta, ys, route, x_mid, gt_f, g_post)


def kernel(x, c, w_ada, b_ada, g_pre_mix, g_post_mix, g_pre_ffn, g_post_ffn, w_in, w_pool_group, pool_scale, w_branch_pool, w_branch_att, w_out, w_group_router, b_group_router, w_expert_router, b_expert_router, w_exp_gate, w_exp_up, w_exp_down):
    b, s, d = x.shape
    assert d == D_MODEL and s % (ATT_BLOCK * 2 * ATT_DILATIONS[2]) == 0 and s % ROW_TILE == 0
    assert (b * s) % MOE_BLOCK == 0
    depth = w_ada.shape[0]
    slopes = jnp.exp2(-ALIBI_MAX_BIAS * jnp.arange(1, N_ATT_HEADS + 1, dtype=F32) / N_ATT_HEADS)
    q_scale = HEAD_DIM ** -0.5 * LOG2_E

    for layer in range(depth):
        mod = _adaln(c, w_ada[layer], b_ada[layer]).reshape(b, 6, 1, d)
        sh_m, sc_m, gt_m, sh_f, sc_f, gt_f = [mod[:, j] for j in range(6)]

        wl = w_in[layer]
        q_lo, k_lo, v_lo = POOL_WIDTH, POOL_WIDTH + 768, POOL_WIDTH + 2 * 768
        group_cols = []
        for g in range(3):
            sl = slice(g * 256, (g + 1) * 256)
            group_cols += [wl[:, q_lo:k_lo][:, sl] * q_scale, wl[:, k_lo:v_lo][:, sl],
                           wl[:, v_lo:v_lo + 768][:, sl]]
        w_perm = jnp.concatenate([wl[:, :POOL_WIDTH]] + group_cols + [wl[:, v_lo + 768:]],
                                 axis=1).astype(BF16)

        u, qkv0, qkv1, qkv2, gates = _inproj(x, g_pre_mix[layer].reshape(1, d), sc_m, sh_m, w_perm)
        att = _attention(qkv0, qkv1, qkv2, slopes)

        pad_cols = LANES - N_EXPERTS - N_EXPERT_GROUPS
        wr = jnp.concatenate([w_expert_router[layer], w_group_router[layer],
                              jnp.zeros((d, pad_cols), F32)], axis=1)
        wr_hi = wr.astype(BF16)
        wr_lo = (wr - wr_hi.astype(F32)).astype(BF16)
        br = jnp.concatenate([b_expert_router[layer], b_group_router[layer],
                              jnp.zeros((pad_cols,), F32)]).reshape(1, LANES)

        x_mid, route, meta, counts, xs = _mixtail(
            u, att, gates, x, gt_m, sc_f, sh_f,
            g_post_mix[layer].reshape(1, d), g_pre_ffn[layer].reshape(1, d),
            w_pool_group[layer].astype(BF16), pool_scale[layer].reshape(1, POOL_WIDTH),
            w_branch_pool[layer].astype(BF16), w_branch_att[layer].astype(BF16),
            w_out[layer].astype(BF16), wr_hi, wr_lo, br)

        ys = _experts(xs, counts[0, :N_EXPERTS], w_exp_gate[layer], w_exp_up[layer], w_exp_down[layer])
        run_meta = meta[:, 0:2, 0:N_EXPERTS].reshape(-1)
        x = _combine(run_meta, ys, route, x_mid.reshape(b * s, d), gt_f,
                     g_post_ffn[layer].reshape(1, d), s).reshape(b, s, d)
    return x
```

```python
import functools

import jax
import jax.numpy as jnp
from jax import lax
from jax.experimental import pallas as pl
from jax.experimental.pallas import tpu as pltpu

F32 = jnp.float32
BF16 = jnp.bfloat16
I32 = jnp.int32
HIGHEST = lax.Precision.HIGHEST

D_MODEL = 1024
LANES = 128
SUBLANES = 8
ROW_WORDS = D_MODEL // (2 * LANES)
HI_HALF = -65536

POOL_WINDOWS = (2, 4, 8, 16)
POOL_GROUP_DIM = 128
POOL_WIDTH = 512
POOL_HALO = 16
POOL_PAD = POOL_HALO + 8

HEAD_DIM = 64
ATT_DILATIONS = (1, 4, 16)
ATT_REACH = 128
ATT_BLOCK = 128
ATT_UNROLL = 4
HEADS_PER_GROUP = 4
N_ATT_HEADS = 12
GROUP_QKV = 3 * HEADS_PER_GROUP * HEAD_DIM
ATT_OUT_WIDTH = 256
ALIBI_MAX_BIAS = 8.0
IN_WIDTH = POOL_WIDTH + 3 * GROUP_QKV + 2 * D_MODEL
MASKED = -1e30
LOG2_E = 1.4426950408889634

N_EXPERT_GROUPS = 4
EXPERTS_PER_GROUP = 8
N_EXPERTS = 32
D_EXPERT = 512
RMS_EPS = 1e-6

ROW_TILE = 512
SORT_ROWS = 2 * ROW_TILE
MOE_BLOCK = 256
BLOCK_DMA_PRIORITY = 1
EXPERT_X_BUFFERS = 6
EXPERT_Y_BUFFERS = 4
COMBINE_BUFFERS = 3
VMEM_LIMIT = 52 * 1024 * 1024


def _sigmoid(x):
    return 0.5 * jnp.tanh(0.5 * x) + 0.5


def _rmsnorm(x, g):
    return x * lax.rsqrt(jnp.mean(x * x, axis=-1, keepdims=True) + RMS_EPS) * g


def _dot(a, b):
    return jnp.dot(a, b, preferred_element_type=F32)


def _dot_nt(a, b, **kw):
    return lax.dot_general(a, b, (((1,), (1,)), ((), ())), preferred_element_type=F32, **kw)


def _pack_rows(ref, w, n, lo, hi):
    word = (lax.shift_right_logical(pltpu.bitcast(lo, I32), 16) | (pltpu.bitcast(hi, I32) & HI_HALF))
    ref[pl.ds(w, n, stride=ROW_WORDS), :] = word


def _unpack_rows(ref, n):
    cols = []
    for w in range(ROW_WORDS):
        word = ref[pl.ds(w, n, stride=ROW_WORDS), :]
        cols += [pltpu.bitcast(word << 16, F32), pltpu.bitcast(word & HI_HALF, F32)]
    return jnp.concatenate(cols, axis=1)


def _bf16_exact(x):
    return x.astype(BF16).astype(F32)


def _region_rows(n_tokens):
    return n_tokens + MOE_BLOCK


def _run_copies(n, src, src_row, dst, dst_row, sem):
    for bit in range(ROW_TILE.bit_length()):
        size = 1 << bit
        above = n & ~(2 * size - 1)

        @pl.when((n & size) != 0)
        def _():
            pltpu.make_async_copy(
                src.at[pl.ds((src_row + above) * ROW_WORDS, size * ROW_WORDS), :],
                dst.at[pl.ds((dst_row + above) * ROW_WORDS, size * ROW_WORDS), :], sem).start()


def _tile_run_copies(counts, src, src_rows, dst, dst_rows, sem):
    local = 0
    for e in range(N_EXPERTS):
        n = counts(e)
        _run_copies(n, src, local if src_rows is None else src_rows(e),
                    dst, local if dst_rows is None else dst_rows(e), sem)
        local = local + n


def _adaln_kernel(c_ref, w_ref, b_ref, o_ref):
    c = c_ref[...]
    a = c * _sigmoid(c)
    o_ref[...] = jnp.dot(a, w_ref[...], preferred_element_type=F32, precision=HIGHEST) + b_ref[...]


def _adaln(c, w_ada, b_ada):
    b, d = c.shape
    n = w_ada.shape[1]
    rows = -(-b // SUBLANES) * SUBLANES
    cp = jnp.pad(c, ((0, rows - b), (0, 0)))
    nt = 1536
    out = pl.pallas_call(
        _adaln_kernel,
        grid=(n // nt,),
        in_specs=[pl.BlockSpec((rows, d), lambda j: (0, 0)),
                  pl.BlockSpec((d, nt), lambda j: (0, j)),
                  pl.BlockSpec((1, nt), lambda j: (0, j))],
        out_specs=pl.BlockSpec((rows, nt), lambda j: (0, j)),
        out_shape=jax.ShapeDtypeStruct((rows, n), F32),
        compiler_params=pltpu.CompilerParams(vmem_limit_bytes=VMEM_LIMIT),
        name="adaln",
    )(cp, w_ada, b_ada.reshape(1, n))
    return out[:b]


def _inproj_kernel(x_ref, g_ref, sc_ref, sh_ref, w_ref,
                   u_ref, qkv0_ref, qkv1_ref, qkv2_ref, gate_ref, h_scr, p_scr):
    tm = x_ref.shape[1]
    h = _rmsnorm(x_ref[0], g_ref[...]) * (1.0 + sc_ref[0]) + sh_ref[0]
    h_scr[...] = h.astype(BF16)

    u_ref[0] = _dot(h_scr[...], w_ref[:, 0:POOL_WIDTH]).astype(BF16)

    col = POOL_WIDTH
    qkv0_ref[0, 0] = _dot(h_scr[...], w_ref[:, col:col + GROUP_QKV]).astype(BF16)
    for out_ref, d in ((qkv1_ref, ATT_DILATIONS[1]), (qkv2_ref, ATT_DILATIONS[2])):
        col += GROUP_QKV
        proj = _dot(h_scr[...], w_ref[:, col:col + GROUP_QKV])
        for cb in range(GROUP_QKV // LANES):
            p_scr[cb] = proj[:, cb * LANES:(cb + 1) * LANES]
        for r in range(d):
            out_ref[0, r] = jnp.concatenate(
                [p_scr[cb, pl.ds(r, tm // d, stride=d), :] for cb in range(GROUP_QKV // LANES)],
                axis=1).astype(BF16)
    col += GROUP_QKV

    chunk = 512
    for j in range(2 * D_MODEL // chunk):
        g = _dot(h_scr[...], w_ref[:, col + j * chunk:col + (j + 1) * chunk])
        gate_ref[0, :, j * chunk:(j + 1) * chunk] = _sigmoid(g).astype(BF16)


def _inproj(x, g_pre, sc, sh, w_perm):
    b, s, d = x.shape
    tm = ROW_TILE
    d1, d2 = ATT_DILATIONS[1], ATT_DILATIONS[2]
    grid = (b, s // tm)
    const = lambda bi, i: (0, 0)
    per_b = lambda bi, i: (bi, 0, 0)
    return pl.pallas_call(
        _inproj_kernel,
        grid=grid,
        in_specs=[pl.BlockSpec((1, tm, d), lambda bi, i: (bi, i, 0)),
                  pl.BlockSpec((1, d), const),
                  pl.BlockSpec((1, 1, d), per_b),
                  pl.BlockSpec((1, 1, d), per_b),
                  pl.BlockSpec((d, IN_WIDTH), const, pipeline_mode=pl.Buffered(1))],
        out_specs=[pl.BlockSpec((1, tm, POOL_WIDTH), lambda bi, i: (bi, i, 0)),
                   pl.BlockSpec((1, 1, tm, GROUP_QKV), lambda bi, i: (bi, 0, i, 0)),
                   pl.BlockSpec((1, d1, tm // d1, GROUP_QKV), lambda bi, i: (bi, 0, i, 0)),
                   pl.BlockSpec((1, d2, tm // d2, GROUP_QKV), lambda bi, i: (bi, 0, i, 0)),
                   pl.BlockSpec((1, tm, 2 * D_MODEL), lambda bi, i: (bi, i, 0))],
        out_shape=[jax.ShapeDtypeStruct((b, s, POOL_WIDTH), BF16),
                   jax.ShapeDtypeStruct((b, 1, s, GROUP_QKV), BF16),
                   jax.ShapeDtypeStruct((b, d1, s // d1, GROUP_QKV), BF16),
                   jax.ShapeDtypeStruct((b, d2, s // d2, GROUP_QKV), BF16),
                   jax.ShapeDtypeStruct((b, s, 2 * D_MODEL), BF16)],
        scratch_shapes=[pltpu.VMEM((tm, d), BF16), pltpu.VMEM((GROUP_QKV // LANES, tm, LANES), F32)],
        compiler_params=pltpu.CompilerParams(
            dimension_semantics=("arbitrary", "arbitrary"), vmem_limit_bytes=VMEM_LIMIT),
        name="inproj",
    )(x, g_pre, sc, sh, w_perm)


def _attn_kernel(slopes_ref, q0, k0, v0, q1, k1, v1, q2, k2, v2, o_ref,
                 bias_scr, acc, mst, lst, s_even, s_odd):
    pair = pl.program_id(1)
    seq = o_ref.shape[1]
    nblk = seq // ATT_BLOCK
    lane = lax.broadcasted_iota(I32, (ATT_BLOCK, LANES), 1)
    first_head = lane < HEAD_DIM
    half_lane = lax.broadcasted_iota(I32, (ATT_BLOCK // 2, LANES), 1)
    head_bits = (jnp.where(half_lane < HEAD_DIM, -1, 0), jnp.where(half_lane < HEAD_DIM, 0, -1))

    qi = lax.broadcasted_iota(I32, (ATT_BLOCK, 2 * ATT_BLOCK), 0)
    kj = lax.broadcasted_iota(I32, (ATT_BLOCK, 2 * ATT_BLOCK), 1)
    delta = ATT_BLOCK + qi - kj
    valid = (delta >= 0) & (delta <= ATT_REACH)
    delta0 = qi - kj
    valid0 = delta0 >= 0
    for g, d in enumerate(ATT_DILATIONS):
        for j in range(2):
            slope = slopes_ref[g * HEADS_PER_GROUP + 2 * pair + j]
            slope = slope * LOG2_E
            bias_scr[g, j, 0] = jnp.where(valid0, -slope * (delta0 * d).astype(F32), MASKED)
            bias_scr[g, j, 1] = jnp.where(valid, -slope * (delta * d).astype(F32), MASKED)

    def run_group(g, q_ref, k_ref, v_ref):
        d = ATT_DILATIONS[g]
        per_res = nblk // d
        n_iter = nblk // ATT_UNROLL

        def block_index(it, k):
            n = it * ATT_UNROLL + k
            return n // per_res, n % per_res

        def key_rows(i):
            lo = jnp.maximum(i - 1, 0)
            return pl.ds(pl.multiple_of(lo * ATT_BLOCK, ATT_BLOCK), 2 * ATT_BLOCK)

        def scores(it, s_ref):
            for k in range(ATT_UNROLL):
                r, i = block_index(it, k)
                q = q_ref[0, r, pl.ds(pl.multiple_of(i * ATT_BLOCK, ATT_BLOCK), ATT_BLOCK), :]
                kw = k_ref[0, r, key_rows(i), :]
                qbits = pltpu.bitcast(q, I32)
                for j in range(2):
                    qh = pltpu.bitcast(qbits & head_bits[j], BF16)
                    s_ref[2 * k + j] = _dot_nt(qh, kw) + bias_scr[g, j, jnp.minimum(i, 1)]

        def weighted_values(it, s_ref):
            for k in range(ATT_UNROLL):
                r, i = block_index(it, k)
                vw = v_ref[0, r, key_rows(i), :]
                outs = []
                for j in range(2):
                    sc = s_ref[2 * k + j]
                    mx = jnp.max(sc, axis=-1, keepdims=True)
                    p = jnp.exp2(sc - mx)
                    den = jnp.sum(p, axis=-1, keepdims=True)
                    outs.append((_dot(p.astype(BF16), vw), mx, den))
                (n0, m0, l0), (n1, m1, l1) = outs
                if d == 1:
                    rows = pl.ds(pl.multiple_of(i * ATT_BLOCK, ATT_BLOCK), ATT_BLOCK)
                else:
                    rows = pl.ds(i * (ATT_BLOCK * d) + r, ATT_BLOCK, stride=d)
                acc[g, rows, :] = jnp.where(first_head, n0, n1)
                mst[g, rows, :] = jnp.where(first_head, m0, m1)
                lst[g, rows, :] = jnp.where(first_head, l0, l1)

        scores(0, s_even)

        def body(h, carry):
            it = 2 * h
            scores(it + 1, s_odd)
            weighted_values(it, s_even)
            scores(jnp.minimum(it + 2, n_iter - 1), s_even)
            weighted_values(it + 1, s_odd)
            return carry

        lax.fori_loop(0, n_iter // 2, body, 0)

    run_group(0, q0, k0, v0)
    run_group(1, q1, k1, v1)
    run_group(2, q2, k2, v2)

    def finish(n, carry):
        rows = pl.ds(pl.multiple_of(n * ATT_BLOCK, ATT_BLOCK), ATT_BLOCK)
        ms = [mst[g, rows, :] for g in range(3)]
        top = jnp.maximum(jnp.maximum(ms[0], ms[1]), ms[2])
        scale = [jnp.exp2(m - top) for m in ms]
        num = scale[0] * acc[0, rows, :] + scale[1] * acc[1, rows, :] + scale[2] * acc[2, rows, :]
        den = scale[0] * lst[0, rows, :] + scale[1] * lst[1, rows, :] + scale[2] * lst[2, rows, :]
        o_ref[0, rows, :] = (num / den).astype(BF16)
        return carry

    lax.fori_loop(0, nblk, finish, 0)


def _attention(qkv0, qkv1, qkv2, slopes):
    b, _, s, _ = qkv0.shape
    pairs = HEADS_PER_GROUP // 2
    col_blocks = HEADS_PER_GROUP * HEAD_DIM // LANES

    def specs(arr):
        _, d, sub, _ = arr.shape
        return [pl.BlockSpec((1, d, sub, LANES),
                             functools.partial(lambda bi, p, sec: (bi, 0, 0, sec * col_blocks + p), sec=sec))
                for sec in range(3)]

    return pl.pallas_call(
        _attn_kernel,
        grid=(b, pairs),
        in_specs=[pl.BlockSpec(memory_space=pltpu.SMEM)] + specs(qkv0) + specs(qkv1) + specs(qkv2),
        out_specs=pl.BlockSpec((1, s, LANES), lambda bi, p: (bi, 0, p)),
        out_shape=jax.ShapeDtypeStruct((b, s, ATT_OUT_WIDTH), BF16),
        scratch_shapes=[pltpu.VMEM((3, 2, 2, ATT_BLOCK, 2 * ATT_BLOCK), F32),
                        pltpu.VMEM((3, s, LANES), F32),
                        pltpu.VMEM((3, s, LANES), F32),
                        pltpu.VMEM((3, s, LANES), F32),
                        pltpu.VMEM((2 * ATT_UNROLL, ATT_BLOCK, 2 * ATT_BLOCK), F32),
                        pltpu.VMEM((2 * ATT_UNROLL, ATT_BLOCK, 2 * ATT_BLOCK), F32)],
        compiler_params=pltpu.CompilerParams(
            dimension_semantics=("arbitrary", "arbitrary"), vmem_limit_bytes=VMEM_LIMIT),
        name="attention",
    )(slopes, qkv0, qkv0, qkv0, qkv1, qkv1, qkv1, qkv2, qkv2, qkv2)


def _mixtail_kernel(u_ref, halo_ref, att_ref, gate_ref, x_ref,
                    gt_m_ref, sc_f_ref, sh_f_ref, g_post_ref, g_pre_ref,
                    wpg_ref, pscale_ref, wbp_ref, wba_ref, wout_ref, wrh_ref, wrl_ref, br_ref,
                    xmid_ref, route_ref, meta_ref, counts_ref, xs_hbm,
                    pu, lv, xbuf, zbuf, fill, meta_s, sem, sem_s):
    i = pl.program_id(1)
    tm = x_ref.shape[1]
    step = pl.program_id(0) * pl.num_programs(1) + i
    last = pl.num_programs(0) * pl.num_programs(1) - 1
    slot = step % 2
    region = _region_rows(pl.num_programs(0) * pl.num_programs(1) * tm)

    @pl.when(step == 0)
    def _():
        fill[...] = jnp.zeros_like(fill)
        pu[0:POOL_PAD - POOL_HALO, :] = jnp.zeros((POOL_PAD - POOL_HALO, POOL_WIDTH), F32)
        lv[:, 0:POOL_PAD - POOL_HALO, :] = jnp.zeros((2, POOL_PAD - POOL_HALO, POOL_GROUP_DIM), F32)
        for e in range(N_EXPERTS):
            meta_s[1, 1, e] = 0
            meta_s[1, 0, e] = 0

    def send_tile(which):
        _tile_run_copies(lambda e: meta_s[which, 1, e], xbuf.at[which], None,
                         xs_hbm, lambda e: e * region + meta_s[which, 0, e], sem.at[which])

    send_tile(1 - slot)

    head = POOL_PAD - POOL_HALO
    halo = halo_ref[0].astype(F32)
    pu[head:POOL_PAD, :] = jnp.where(i > 0, halo, jnp.zeros_like(halo))
    pu[POOL_PAD:POOL_PAD + tm, :] = u_ref[0].astype(F32)
    t = i * tm + lax.broadcasted_iota(I32, (tm, 1), 0)
    mixed = []
    for g, w in enumerate(POOL_WINDOWS):
        cols = slice(g * POOL_GROUP_DIM, (g + 1) * POOL_GROUP_DIM)
        read = lambda start, n: pu[pl.ds(start, n), cols]
        shift, level = 1, 0
        while 2 * shift < w:
            partial = read(head, tm + POOL_HALO) + read(head - shift, tm + POOL_HALO)
            buf = lv.at[level % 2]
            buf[pl.ds(head, tm + POOL_HALO), :] = partial
            read = lambda start, n, buf=buf: buf[pl.ds(start, n), :]
            shift, level = 2 * shift, level + 1
        win = read(POOL_PAD, tm) + read(POOL_PAD - shift, tm)
        count = jnp.minimum(t + 1, w).astype(F32)
        pooled = win / count - pu[pl.ds(POOL_PAD, tm), cols]
        mixed.append((_dot(pooled.astype(BF16), wpg_ref[g]) * pscale_ref[:, cols]).astype(BF16))
    y_pool = _dot(jnp.concatenate(mixed, axis=1), wbp_ref[...])

    y_att = _dot(att_ref[0], wba_ref[...])
    merged = (gate_ref[0, :, 0:D_MODEL] * y_pool.astype(BF16)
              + gate_ref[0, :, D_MODEL:2 * D_MODEL] * y_att.astype(BF16))
    y = _dot(merged, wout_ref[...])
    x_mid = x_ref[0] + gt_m_ref[0] * _rmsnorm(y, g_post_ref[...])
    xmid_ref[0] = x_mid

    h2 = _rmsnorm(x_mid, g_pre_ref[...]) * (1.0 + sc_f_ref[0]) + sh_f_ref[0]

    h2b = h2.astype(BF16)
    h2lo = (h2 - h2b.astype(F32)).astype(BF16)
    by_token = (_dot(h2b, wrh_ref[...]) + (_dot(h2b, wrl_ref[...]) + _dot(h2lo, wrh_ref[...]))
                + br_ref[...])
    logits = jnp.concatenate([by_token[c * LANES:(c + 1) * LANES, :].T for c in range(tm // LANES)],
                             axis=1)
    neg_inf = -jnp.inf
    far = float(LANES)
    gl = logits[N_EXPERTS:N_EXPERTS + N_EXPERT_GROUPS, :]
    grow = lax.broadcasted_iota(I32, gl.shape, 0).astype(F32)
    gmax = jnp.max(gl, axis=0, keepdims=True)
    gsel = jnp.min(jnp.where(gl == gmax, grow, far), axis=0, keepdims=True)
    p_group = 1.0 / jnp.sum(jnp.exp(gl - gmax), axis=0, keepdims=True)
    erow = lax.broadcasted_iota(I32, (N_EXPERTS, tm), 0).astype(F32)
    e_lo = gsel * float(EXPERTS_PER_GROUP)
    el = jnp.where((erow >= e_lo) & (erow < e_lo + float(EXPERTS_PER_GROUP)), logits[0:N_EXPERTS, :], neg_inf)
    v1 = jnp.max(el, axis=0, keepdims=True)
    i1 = jnp.min(jnp.where(el == v1, erow, far), axis=0, keepdims=True)
    el2 = jnp.where(erow == i1, neg_inf, el)
    v2 = jnp.max(el2, axis=0, keepdims=True)
    i2 = jnp.min(jnp.where(el2 == v2, erow, far), axis=0, keepdims=True)
    e21 = jnp.exp(v2 - v1)
    w1 = p_group / (1.0 + e21)
    w2 = p_group * e21 / (1.0 + e21)

    pick1 = erow == i1
    pick2 = erow == i2
    assign = jnp.where(pick1 | pick2, 1.0, 0.0)
    before = (lax.broadcasted_iota(I32, (tm, tm), 0) < lax.broadcasted_iota(I32, (tm, tm), 1))
    rank = _dot(assign.astype(BF16), jnp.where(before, 1.0, 0.0).astype(BF16))
    assign_pad = jnp.concatenate([assign, jnp.zeros((LANES - N_EXPERTS, tm), F32)], axis=0).astype(BF16)
    cnt_row = _dot_nt(jnp.ones((SUBLANES, tm), BF16), assign_pad)
    e_col = lax.broadcasted_iota(I32, (N_EXPERTS, LANES), 0)
    e_lane = lax.broadcasted_iota(I32, (N_EXPERTS, LANES), 1)
    run_start = jnp.sum(jnp.where(e_lane < e_col, cnt_row[0:1, :], 0.0), axis=1, keepdims=True)
    pos = rank + run_start
    key1 = jnp.sum(jnp.where(pick1, pos, 0.0), axis=0, keepdims=True)
    key2 = jnp.sum(jnp.where(pick2, pos, 0.0), axis=0, keepdims=True)

    row8 = lax.broadcasted_iota(I32, (SUBLANES, tm), 0)
    route_ref[...] = jnp.where(row8 == 0, key1, jnp.where(row8 == 1, key2,
                               jnp.where(row8 == 2, w1, jnp.where(row8 == 3, w2, 0.0))))

    filled = fill[...]
    mrow = lax.broadcasted_iota(I32, (SUBLANES, LANES), 0)
    meta_ref[0] = jnp.where(mrow == 0, filled, jnp.where(mrow == 1, cnt_row, 0.0)).astype(I32)
    fill[...] = filled + cnt_row
    counts_ref[...] = (filled + cnt_row).astype(I32)

    srow = lax.broadcasted_iota(I32, (SORT_ROWS, tm), 0).astype(F32)
    perm = jnp.where((srow == key1) | (srow == key2), 1.0, 0.0).astype(BF16)
    def wait_tile(which):
        pltpu.make_async_copy(xbuf.at[which], xs_hbm.at[pl.ds(0, SORT_ROWS * ROW_WORDS), :],
                              sem.at[which]).wait()

    @pl.when(step > 1)
    def _():
        wait_tile(slot)

    xslot = xbuf.at[slot]
    for w in range(ROW_WORDS):
        pair = _dot(perm, h2b[:, 2 * w * LANES:(2 * w + 2) * LANES])
        _pack_rows(xslot, w, SORT_ROWS, pair[:, :LANES], pair[:, LANES:])

    meta_copy = pltpu.make_async_copy(meta_ref.at[0], meta_s.at[slot], sem_s)
    meta_copy.start()
    meta_copy.wait()

    @pl.when(step == last)
    def _():
        send_tile(slot)

        @pl.when(step > 0)
        def _():
            wait_tile(1 - slot)

        wait_tile(slot)
        zbuf[...] = jnp.zeros_like(zbuf)

        def pad_copy(e):
            end = e * region + meta_s[slot, 0, e] + meta_s[slot, 1, e]
            return pltpu.make_async_copy(
                zbuf, xs_hbm.at[pl.ds(end * ROW_WORDS, MOE_BLOCK * ROW_WORDS), :], sem_s)

        def start_pad(e, carry):
            pad_copy(e).start()
            return carry

        def wait_pad(e, carry):
            pad_copy(e).wait()
            return carry

        lax.fori_loop(0, N_EXPERTS, start_pad, 0)
        lax.fori_loop(0, N_EXPERTS, wait_pad, 0)


def _mixtail(u, att, gates, x, gt_m, sc_f, sh_f, g_post, g_pre,
             wpg, pscale, wbp, wba, wout, wr_hi, wr_lo, br):
    b, s, d = x.shape
    tm = ROW_TILE
    tiles = s // tm
    n_tiles = b * tiles
    halo_blocks = tm // POOL_HALO
    region = _region_rows(b * s)
    const2 = lambda bi, i: (0, 0)
    const3 = lambda bi, i: (0, 0, 0)
    per_b = lambda bi, i: (bi, 0, 0)
    tile = lambda bi, i: (bi, i, 0)
    single = dict(pipeline_mode=pl.Buffered(1))
    return pl.pallas_call(
        _mixtail_kernel,
        grid=(b, tiles),
        in_specs=[pl.BlockSpec((1, tm, POOL_WIDTH), tile),
                  pl.BlockSpec((1, POOL_HALO, POOL_WIDTH),
                               lambda bi, i: (bi, jnp.maximum(i * halo_blocks - 1, 0), 0)),
                  pl.BlockSpec((1, tm, ATT_OUT_WIDTH), tile),
                  pl.BlockSpec((1, tm, 2 * D_MODEL), tile),
                  pl.BlockSpec((1, tm, d), tile),
                  pl.BlockSpec((1, 1, d), per_b),
                  pl.BlockSpec((1, 1, d), per_b),
                  pl.BlockSpec((1, 1, d), per_b),
                  pl.BlockSpec((1, d), const2),
                  pl.BlockSpec((1, d), const2),
                  pl.BlockSpec(wpg.shape, const3, **single),
                  pl.BlockSpec((1, POOL_WIDTH), const2),
                  pl.BlockSpec(wbp.shape, const2, **single),
                  pl.BlockSpec(wba.shape, const2, **single),
                  pl.BlockSpec(wout.shape, const2, **single),
                  pl.BlockSpec(wr_hi.shape, const2, **single),
                  pl.BlockSpec(wr_lo.shape, const2, **single),
                  pl.BlockSpec(br.shape, const2)],
        out_specs=[pl.BlockSpec((1, tm, d), tile),
                   pl.BlockSpec((SUBLANES, tm), lambda bi, i: (0, bi * tiles + i)),
                   pl.BlockSpec((1, SUBLANES, LANES), lambda bi, i: (bi * tiles + i, 0, 0)),
                   pl.BlockSpec((SUBLANES, LANES), const2),
                   pl.BlockSpec(memory_space=pl.ANY)],
        out_shape=[jax.ShapeDtypeStruct((b, s, d), F32),
                   jax.ShapeDtypeStruct((SUBLANES, b * s), F32),
                   jax.ShapeDtypeStruct((n_tiles, SUBLANES, LANES), I32),
                   jax.ShapeDtypeStruct((SUBLANES, LANES), I32),
                   jax.ShapeDtypeStruct((N_EXPERTS * region * ROW_WORDS, LANES), I32)],
        scratch_shapes=[pltpu.VMEM((POOL_PAD + tm, POOL_WIDTH), F32),
                        pltpu.VMEM((2, POOL_PAD + tm, POOL_GROUP_DIM), F32),
                        pltpu.VMEM((2, SORT_ROWS * ROW_WORDS, LANES), I32),
                        pltpu.VMEM((MOE_BLOCK * ROW_WORDS, LANES), I32),
                        pltpu.VMEM((SUBLANES, LANES), F32),
                        pltpu.SMEM((2, SUBLANES, LANES), I32),
                        pltpu.SemaphoreType.DMA((2,)),
                        pltpu.SemaphoreType.DMA(())],
        compiler_params=pltpu.CompilerParams(
            dimension_semantics=("arbitrary", "arbitrary"), vmem_limit_bytes=VMEM_LIMIT),
        name="mixtail",
    )(u, u, att, gates, x, gt_m, sc_f, sh_f, g_post, g_pre,
      wpg, pscale, wbp, wba, wout, wr_hi, wr_lo, br)


def _expert_kernel(counts_ref, xs_hbm, wg_ref, wu_ref, wd_ref, ys_hbm,
                   wg_bf, wu_bf, wd_bf, xbuf, ybuf, state, semx, semy):
    e = pl.program_id(0)
    bm = MOE_BLOCK
    nx, ny = EXPERT_X_BUFFERS, EXPERT_Y_BUFFERS
    block_words = bm * ROW_WORDS
    region = xs_hbm.shape[0] // (N_EXPERTS * ROW_WORDS)

    def n_blocks(ex):
        return (counts_ref[ex] + (bm - 1)) // bm

    def block_rows(ref, ex, k):
        start = pl.multiple_of((ex * region + k * bm) * ROW_WORDS, block_words)
        return ref.at[pl.ds(start, block_words), :]

    def x_copy(ex, k, s):
        return pltpu.make_async_copy(block_rows(xs_hbm, ex, k), xbuf.at[s], semx.at[s])

    def y_copy(k, s):
        return pltpu.make_async_copy(ybuf.at[s], block_rows(ys_hbm, e, k), semy.at[s])

    @pl.when(e == 0)
    def _():
        for j in range(4):
            state[j] = 0

    def fetch_through(target):
        def more(c):
            pe, _, pg = c
            return (pg < target) & (pe < N_EXPERTS)

        def step(c):
            pe, pk, pg = c
            has = pk < n_blocks(pe)

            @pl.when(has)
            def _():
                x_copy(pe, pk, pg % nx).start(priority=BLOCK_DMA_PRIORITY)

            return (jnp.where(has, pe, pe + 1), jnp.where(has, pk + 1, 0), pg + has.astype(I32))

        pe, pk, pg = lax.while_loop(more, step, (state[1], state[2], state[3]))
        state[1] = pe
        state[2] = pk
        state[3] = pg

    wg_bf[...] = wg_ref[...].astype(BF16)
    wu_bf[...] = wu_ref[...].astype(BF16)
    wd_bf[...] = wd_ref[...].astype(BF16)

    nb = n_blocks(e)
    done = state[0]

    def blocks(k0, count):
        ks = [k0 + c for c in range(count)]
        gs = [done + k for k in ks]
        fetch_through(gs[0] + nx)
        for k, g in zip(ks, gs):
            x_copy(e, k, g % nx).wait()

            @pl.when(g >= ny)
            def _():
                y_copy(k, g % ny).wait()

        mids = []
        for g in gs:
            x = _unpack_rows(xbuf.at[g % nx], bm).astype(BF16)
            a = _dot(x, wg_bf[...])
            u = _dot(x, wu_bf[...])
            mids.append(((a * _sigmoid(a)) * u).astype(BF16))
        for k, g, mid in zip(ks, gs, mids):
            for w in range(ROW_WORDS):
                pair = _bf16_exact(_dot(mid, wd_bf[:, 2 * w * LANES:(2 * w + 2) * LANES]))
                _pack_rows(ybuf.at[g % ny], w, bm, pair[:, :LANES], pair[:, LANES:])
            y_copy(k, g % ny).start(priority=BLOCK_DMA_PRIORITY)

    def pair_of_blocks(p, carry):
        blocks(2 * p, 2)
        return carry

    lax.fori_loop(0, nb // 2, pair_of_blocks, 0)

    @pl.when(nb % 2 == 1)
    def _():
        blocks(nb - 1, 1)

    state[0] = done + nb

    @pl.when(e == N_EXPERTS - 1)
    def _():
        total = done + nb
        for j in range(ny):
            @pl.when(total > j)
            def _():
                y_copy(0, (total - 1 - j) % ny).wait()


def _experts(xs, counts, w_gate, w_up, w_down):
    bm = MOE_BLOCK
    w_in = pl.BlockSpec((None, D_MODEL, D_EXPERT), lambda e, cnt: (e, 0, 0))
    w_out = pl.BlockSpec((None, D_EXPERT, D_MODEL), lambda e, cnt: (e, 0, 0))
    grid_spec = pltpu.PrefetchScalarGridSpec(
        num_scalar_prefetch=1,
        grid=(N_EXPERTS,),
        in_specs=[pl.BlockSpec(memory_space=pl.ANY), w_in, w_in, w_out],
        out_specs=pl.BlockSpec(memory_space=pl.ANY),
        scratch_shapes=[pltpu.VMEM((D_MODEL, D_EXPERT), BF16),
                        pltpu.VMEM((D_MODEL, D_EXPERT), BF16),
                        pltpu.VMEM((D_EXPERT, D_MODEL), BF16),
                        pltpu.VMEM((EXPERT_X_BUFFERS, bm * ROW_WORDS, LANES), I32),
                        pltpu.VMEM((EXPERT_Y_BUFFERS, bm * ROW_WORDS, LANES), I32),
                        pltpu.SMEM((4,), I32),
                        pltpu.SemaphoreType.DMA((EXPERT_X_BUFFERS,)),
                        pltpu.SemaphoreType.DMA((EXPERT_Y_BUFFERS,))],
    )
    return pl.pallas_call(
        _expert_kernel,
        grid_spec=grid_spec,
        out_shape=jax.ShapeDtypeStruct(xs.shape, I32),
        compiler_params=pltpu.CompilerParams(
            dimension_semantics=("arbitrary",), vmem_limit_bytes=VMEM_LIMIT),
        name="experts",
    )(counts, xs, w_gate, w_up, w_down)


def _combine_kernel(meta_ref, y_hbm, route_ref, xmid_ref, gt_ref, g_ref, o_ref, ybuf, rt_scr, sem):
    step = pl.program_id(0)
    n_steps = pl.num_programs(0)
    tm = ROW_TILE
    slot = step % COMBINE_BUFFERS
    region = _region_rows(n_steps * tm)

    def fetch_tile(tile, live):
        tile = jnp.minimum(tile, n_steps - 1)
        which = tile % COMBINE_BUFFERS
        _tile_run_copies(lambda e: jnp.where(live, meta_ref[(tile * 2 + 1) * N_EXPERTS + e], 0),
                         y_hbm, lambda e: e * region + meta_ref[(tile * 2) * N_EXPERTS + e],
                         ybuf.at[which], None, sem.at[which])

    @pl.when(step == 0)
    def _():
        for ahead in range(COMBINE_BUFFERS - 1):
            fetch_tile(ahead, ahead < n_steps)

    pltpu.make_async_copy(y_hbm.at[pl.ds(0, SORT_ROWS * ROW_WORDS), :], ybuf.at[slot], sem.at[slot]).wait()
    yb = _unpack_rows(ybuf.at[slot], SORT_ROWS).astype(BF16)

    fetch_tile(step + COMBINE_BUFFERS - 1, step + COMBINE_BUFFERS - 1 < n_steps)

    rt_scr[...] = jnp.zeros_like(rt_scr)
    rt_scr[0:SUBLANES, :] = route_ref[...]
    cols = jnp.concatenate([rt_scr[:, c * LANES:(c + 1) * LANES].T for c in range(tm // LANES)], axis=0)
    key1, key2, w1, w2 = cols[:, 0:1], cols[:, 1:2], cols[:, 2:3], cols[:, 3:4]

    spos = lax.broadcasted_iota(I32, (tm, SORT_ROWS), 1).astype(F32)
    y1 = _dot(jnp.where(spos == key1, 1.0, 0.0).astype(BF16), yb)
    y2 = _dot(jnp.where(spos == key2, 1.0, 0.0).astype(BF16), yb)
    y = w1 * y1 + w2 * y2
    o_ref[...] = xmid_ref[...] + gt_ref[0] * _rmsnorm(y, g_ref[...])


def _combine(meta, ys, route, x_mid, gt_f, g_post, seq):
    t, d = x_mid.shape
    tm = ROW_TILE
    tiles_per_seq = seq // tm
    grid_spec = pltpu.PrefetchScalarGridSpec(
        num_scalar_prefetch=1,
        grid=(t // tm,),
        in_specs=[pl.BlockSpec(memory_space=pl.ANY),
                  pl.BlockSpec((SUBLANES, tm), lambda i, m: (0, i)),
                  pl.BlockSpec((tm, d), lambda i, m: (i, 0)),
                  pl.BlockSpec((1, 1, d), lambda i, m: (i // tiles_per_seq, 0, 0)),
                  pl.BlockSpec((1, d), lambda i, m: (0, 0))],
        out_specs=pl.BlockSpec((tm, d), lambda i, m: (i, 0)),
        scratch_shapes=[pltpu.VMEM((COMBINE_BUFFERS, SORT_ROWS * ROW_WORDS, LANES), I32),
                        pltpu.VMEM((LANES, tm), F32),
                        pltpu.SemaphoreType.DMA((COMBINE_BUFFERS,))],
    )
    return pl.pallas_call(
        _combine_kernel,
        grid_spec=grid_spec,
        out_shape=jax.ShapeDtypeStruct((t, d), F32),
        compiler_params=pltpu.CompilerParams(
            dimension_semantics=("arbitrary",), vmem_limit_bytes=VMEM_LIMIT),
        name="combine",
    )(meta, ys, route, x_mid, gt_f, g_post)


def kernel(x, c, w_ada, b_ada, g_pre_mix, g_post_mix, g_pre_ffn, g_post_ffn, w_in, w_pool_group, pool_scale, w_branch_pool, w_branch_att, w_out, w_group_router, b_group_router, w_expert_router, b_expert_router, w_exp_gate, w_exp_up, w_exp_down):
    b, s, d = x.shape
    assert d == D_MODEL and s % (ATT_BLOCK * 2 * ATT_DILATIONS[2]) == 0 and s % ROW_TILE == 0
    assert (b * s) % MOE_BLOCK == 0
    depth = w_ada.shape[0]
    slopes = jnp.exp2(-ALIBI_MAX_BIAS * jnp.arange(1, N_ATT_HEADS + 1, dtype=F32) / N_ATT_HEADS)
    q_scale = HEAD_DIM ** -0.5 * LOG2_E

    for layer in range(depth):
        mod = _adaln(c, w_ada[layer], b_ada[layer]).reshape(b, 6, 1, d)
        sh_m, sc_m, gt_m, sh_f, sc_f, gt_f = [mod[:, j] for j in range(6)]

        wl = w_in[layer]
        q_lo, k_lo, v_lo = POOL_WIDTH, POOL_WIDTH + 768, POOL_WIDTH + 2 * 768
        group_cols = []
        for g in range(3):
            sl = slice(g * 256, (g + 1) * 256)
            group_cols += [wl[:, q_lo:k_lo][:, sl] * q_scale, wl[:, k_lo:v_lo][:, sl],
                           wl[:, v_lo:v_lo + 768][:, sl]]
        w_perm = jnp.concatenate([wl[:, :POOL_WIDTH]] + group_cols + [wl[:, v_lo + 768:]],
                                 axis=1).astype(BF16)

        u, qkv0, qkv1, qkv2, gates = _inproj(x, g_pre_mix[layer].reshape(1, d), sc_m, sh_m, w_perm)
        att = _attention(qkv0, qkv1, qkv2, slopes)

        pad_cols = LANES - N_EXPERTS - N_EXPERT_GROUPS
        wr = jnp.concatenate([w_expert_router[layer], w_group_router[layer],
                              jnp.zeros((d, pad_cols), F32)], axis=1)
        wr_hi = wr.astype(BF16)
        wr_lo = (wr - wr_hi.astype(F32)).astype(BF16)
        br = jnp.concatenate([b_expert_router[layer], b_group_router[layer],
                              jnp.zeros((pad_cols,), F32)]).reshape(1, LANES)

        x_mid, route, meta, counts, xs = _mixtail(
            u, att, gates, x, gt_m, sc_f, sh_f,
            g_post_mix[layer].reshape(1, d), g_pre_ffn[layer].reshape(1, d),
            w_pool_group[layer].astype(BF16), pool_scale[layer].reshape(1, POOL_WIDTH),
            w_branch_pool[layer].astype(BF16), w_branch_att[layer].astype(BF16),
            w_out[layer].astype(BF16), wr_hi, wr_lo, br)

        ys = _experts(xs, counts[0, :N_EXPERTS], w_exp_gate[layer], w_exp_up[layer], w_exp_down[layer])
        run_meta = meta[:, 0:2, 0:N_EXPERTS].reshape(-1)
        x = _combine(run_meta, ys, route, x_mid.reshape(b * s, d), gt_f,
                     g_post_ffn[layer].reshape(1, d), s).reshape(b, s, d)
    return x
```

```python
import functools

import jax
import jax.numpy as jnp
from jax import lax
from jax.experimental import pallas as pl
from jax.experimental.pallas import tpu as pltpu

F32 = jnp.float32
BF16 = jnp.bfloat16
I32 = jnp.int32
HIGHEST = lax.Precision.HIGHEST

D_MODEL = 1024
LANES = 128
SUBLANES = 8
ROW_WORDS = D_MODEL // (2 * LANES)
HI_HALF = -65536

POOL_WINDOWS = (2, 4, 8, 16)
POOL_GROUP_DIM = 128
POOL_WIDTH = 512
POOL_HALO = 16
POOL_PAD = POOL_HALO + 8

HEAD_DIM = 64
ATT_DILATIONS = (1, 4, 16)
ATT_REACH = 128
ATT_BLOCK = 128
ATT_UNROLL = 4
HEADS_PER_GROUP = 4
N_ATT_HEADS = 12
GROUP_QKV = 3 * HEADS_PER_GROUP * HEAD_DIM
ATT_OUT_WIDTH = 256
ALIBI_MAX_BIAS = 8.0
IN_WIDTH = POOL_WIDTH + 3 * GROUP_QKV + 2 * D_MODEL
MASKED = -1e30
LOG2_E = 1.4426950408889634

N_EXPERT_GROUPS = 4
EXPERTS_PER_GROUP = 8
N_EXPERTS = 32
ROUTER_ROWS = 48
D_EXPERT = 512
RMS_EPS = 1e-6

ROW_TILE = 512
ROW_SPLIT = 2
SORT_ROWS = 2 * ROW_TILE
MOE_BLOCK = 256
BLOCK_DMA_PRIORITY = 1
EXPERT_X_BUFFERS = 6
EXPERT_Y_BUFFERS = 4
COMBINE_BUFFERS = 3
VMEM_LIMIT = 52 * 1024 * 1024


def _sigmoid(x):
    return 0.5 * jnp.tanh(0.5 * x) + 0.5


def _rmsnorm(x, g):
    return x * lax.rsqrt(jnp.mean(x * x, axis=-1, keepdims=True) + RMS_EPS) * g


def _dot(a, b):
    return jnp.dot(a, b, preferred_element_type=F32)


def _dot_nt(a, b, **kw):
    return lax.dot_general(a, b, (((1,), (1,)), ((), ())), preferred_element_type=F32, **kw)


def _pack_rows(ref, w, n, lo, hi):
    word = (lax.shift_right_logical(pltpu.bitcast(lo, I32), 16) | (pltpu.bitcast(hi, I32) & HI_HALF))
    ref[pl.ds(w, n, stride=ROW_WORDS), :] = word


def _unpack_rows(ref, n):
    cols = []
    for w in range(ROW_WORDS):
        word = ref[pl.ds(w, n, stride=ROW_WORDS), :]
        cols += [pltpu.bitcast(word << 16, F32), pltpu.bitcast(word & HI_HALF, F32)]
    return jnp.concatenate(cols, axis=1)


def _bf16_exact(x):
    return x.astype(BF16).astype(F32)


def _region_rows(n_tokens):
    return n_tokens + MOE_BLOCK


def _run_copies(n, src, src_row, dst, dst_row, sem):
    for bit in range(ROW_TILE.bit_length()):
        size = 1 << bit
        above = n & ~(2 * size - 1)

        @pl.when((n & size) != 0)
        def _():
            pltpu.make_async_copy(
                src.at[pl.ds((src_row + above) * ROW_WORDS, size * ROW_WORDS), :],
                dst.at[pl.ds((dst_row + above) * ROW_WORDS, size * ROW_WORDS), :], sem).start()


def _tile_run_copies(counts, src, src_rows, dst, dst_rows, sem):
    local = 0
    for e in range(N_EXPERTS):
        n = counts(e)
        _run_copies(n, src, local if src_rows is None else src_rows(e),
                    dst, local if dst_rows is None else dst_rows(e), sem)
        local = local + n


def _adaln_kernel(c_ref, w_ref, b_ref, o_ref):
    c = c_ref[...]
    a = c * _sigmoid(c)
    o_ref[...] = jnp.dot(a, w_ref[...], preferred_element_type=F32, precision=HIGHEST) + b_ref[...]


def _adaln(c, w_ada, b_ada):
    b, d = c.shape
    n = w_ada.shape[1]
    rows = -(-b // SUBLANES) * SUBLANES
    cp = jnp.pad(c, ((0, rows - b), (0, 0)))
    nt = 1536
    out = pl.pallas_call(
        _adaln_kernel,
        grid=(n // nt,),
        in_specs=[pl.BlockSpec((rows, d), lambda j: (0, 0)),
                  pl.BlockSpec((d, nt), lambda j: (0, j)),
                  pl.BlockSpec((1, nt), lambda j: (0, j))],
        out_specs=pl.BlockSpec((rows, nt), lambda j: (0, j)),
        out_shape=jax.ShapeDtypeStruct((rows, n), F32),
        compiler_params=pltpu.CompilerParams(vmem_limit_bytes=VMEM_LIMIT),
        name="adaln",
    )(cp, w_ada, b_ada.reshape(1, n))
    return out[:b]


def _inproj_kernel(x_ref, g_ref, sc_ref, sh_ref, w_ref,
                   u_ref, qkv0_ref, qkv1_ref, qkv2_ref, gate_ref, h_scr, p_scr):
    tm = x_ref.shape[1]
    hm = tm // ROW_SPLIT
    for part in range(ROW_SPLIT):
        rows = slice(part * hm, (part + 1) * hm)
        h = _rmsnorm(x_ref[0, rows, :], g_ref[...]) * (1.0 + sc_ref[0]) + sh_ref[0]
        h_scr[rows, :] = h.astype(BF16)
        hb = h_scr[rows, :]

        u_ref[0, rows, :] = _dot(hb, w_ref[:, 0:POOL_WIDTH]).astype(BF16)

        col = POOL_WIDTH
        qkv0_ref[0, 0, rows, :] = _dot(hb, w_ref[:, col:col + GROUP_QKV]).astype(BF16)
        for gi, (out_ref, d) in enumerate(((qkv1_ref, ATT_DILATIONS[1]), (qkv2_ref, ATT_DILATIONS[2]))):
            col += GROUP_QKV
            proj = _dot(hb, w_ref[:, col:col + GROUP_QKV])
            stage = p_scr.at[part, gi]
            for cb in range(GROUP_QKV // LANES):
                stage[cb] = proj[:, cb * LANES:(cb + 1) * LANES]
            sub = hm // d
            for r in range(d):
                out_ref[0, r, part * sub:(part + 1) * sub, :] = jnp.concatenate(
                    [stage[cb, pl.ds(r, sub, stride=d), :] for cb in range(GROUP_QKV // LANES)],
                    axis=1).astype(BF16)
        col += GROUP_QKV

        chunk = 512
        for j in range(2 * D_MODEL // chunk):
            g = _dot(hb, w_ref[:, col + j * chunk:col + (j + 1) * chunk])
            gate_ref[0, rows, j * chunk:(j + 1) * chunk] = _sigmoid(g).astype(BF16)


def _inproj(x, g_pre, sc, sh, w_perm):
    b, s, d = x.shape
    tm = ROW_TILE
    d1, d2 = ATT_DILATIONS[1], ATT_DILATIONS[2]
    grid = (b, s // tm)
    const = lambda bi, i: (0, 0)
    per_b = lambda bi, i: (bi, 0, 0)
    return pl.pallas_call(
        _inproj_kernel,
        grid=grid,
        in_specs=[pl.BlockSpec((1, tm, d), lambda bi, i: (bi, i, 0)),
                  pl.BlockSpec((1, d), const),
                  pl.BlockSpec((1, 1, d), per_b),
                  pl.BlockSpec((1, 1, d), per_b),
                  pl.BlockSpec((d, IN_WIDTH), const, pipeline_mode=pl.Buffered(1))],
        out_specs=[pl.BlockSpec((1, tm, POOL_WIDTH), lambda bi, i: (bi, i, 0)),
                   pl.BlockSpec((1, 1, tm, GROUP_QKV), lambda bi, i: (bi, 0, i, 0)),
                   pl.BlockSpec((1, d1, tm // d1, GROUP_QKV), lambda bi, i: (bi, 0, i, 0)),
                   pl.BlockSpec((1, d2, tm // d2, GROUP_QKV), lambda bi, i: (bi, 0, i, 0)),
                   pl.BlockSpec((1, tm, 2 * D_MODEL), lambda bi, i: (bi, i, 0))],
        out_shape=[jax.ShapeDtypeStruct((b, s, POOL_WIDTH), BF16),
                   jax.ShapeDtypeStruct((b, 1, s, GROUP_QKV), BF16),
                   jax.ShapeDtypeStruct((b, d1, s // d1, GROUP_QKV), BF16),
                   jax.ShapeDtypeStruct((b, d2, s // d2, GROUP_QKV), BF16),
                   jax.ShapeDtypeStruct((b, s, 2 * D_MODEL), BF16)],
        scratch_shapes=[pltpu.VMEM((tm, d), BF16),
                        pltpu.VMEM((ROW_SPLIT, 2, GROUP_QKV // LANES, tm // ROW_SPLIT, LANES), F32)],
        compiler_params=pltpu.CompilerParams(
            dimension_semantics=("arbitrary", "arbitrary"), vmem_limit_bytes=VMEM_LIMIT),
        name="inproj",
    )(x, g_pre, sc, sh, w_perm)


def _attn_kernel(slopes_ref, q0, k0, v0, q1, k1, v1, q2, k2, v2, o_ref,
                 bias_scr, acc, mst, lst, s_even, s_odd):
    pair = pl.program_id(1)
    seq = o_ref.shape[1]
    nblk = seq // ATT_BLOCK
    lane = lax.broadcasted_iota(I32, (ATT_BLOCK, LANES), 1)
    first_head = lane < HEAD_DIM
    half_lane = lax.broadcasted_iota(I32, (ATT_BLOCK // 2, LANES), 1)
    head_bits = (jnp.where(half_lane < HEAD_DIM, -1, 0), jnp.where(half_lane < HEAD_DIM, 0, -1))

    qi = lax.broadcasted_iota(I32, (ATT_BLOCK, 2 * ATT_BLOCK), 0)
    kj = lax.broadcasted_iota(I32, (ATT_BLOCK, 2 * ATT_BLOCK), 1)
    delta = ATT_BLOCK + qi - kj
    valid = (delta >= 0) & (delta <= ATT_REACH)
    delta0 = qi - kj
    valid0 = delta0 >= 0
    for g, d in enumerate(ATT_DILATIONS):
        for j in range(2):
            slope = slopes_ref[g * HEADS_PER_GROUP + 2 * pair + j]
            slope = slope * LOG2_E
            bias_scr[g, j, 0] = jnp.where(valid0, -slope * (delta0 * d).astype(F32), MASKED)
            bias_scr[g, j, 1] = jnp.where(valid, -slope * (delta * d).astype(F32), MASKED)

    def run_group(g, q_ref, k_ref, v_ref):
        d = ATT_DILATIONS[g]
        per_res = nblk // d
        n_iter = nblk // ATT_UNROLL

        def block_index(it, k):
            n = it * ATT_UNROLL + k
            return n // per_res, n % per_res

        def key_rows(i):
            lo = jnp.maximum(i - 1, 0)
            return pl.ds(pl.multiple_of(lo * ATT_BLOCK, ATT_BLOCK), 2 * ATT_BLOCK)

        def scores(it, s_ref):
            for k in range(ATT_UNROLL):
                r, i = block_index(it, k)
                q = q_ref[0, r, pl.ds(pl.multiple_of(i * ATT_BLOCK, ATT_BLOCK), ATT_BLOCK), :]
                kw = k_ref[0, r, key_rows(i), :]
                qbits = pltpu.bitcast(q, I32)
                for j in range(2):
                    qh = pltpu.bitcast(qbits & head_bits[j], BF16)
                    s_ref[2 * k + j] = _dot_nt(qh, kw) + bias_scr[g, j, jnp.minimum(i, 1)]

        def weighted_values(it, s_ref):
            for k in range(ATT_UNROLL):
                r, i = block_index(it, k)
                vw = v_ref[0, r, key_rows(i), :]
                outs = []
                for j in range(2):
                    sc = s_ref[2 * k + j]
                    mx = jnp.max(sc, axis=-1, keepdims=True)
                    p = jnp.exp2(sc - mx)
                    den = jnp.sum(p, axis=-1, keepdims=True)
                    outs.append((_dot(p.astype(BF16), vw), mx, den))
                (n0, m0, l0), (n1, m1, l1) = outs
                if d == 1:
                    rows = pl.ds(pl.multiple_of(i * ATT_BLOCK, ATT_BLOCK), ATT_BLOCK)
                else:
                    rows = pl.ds(i * (ATT_BLOCK * d) + r, ATT_BLOCK, stride=d)
                acc[g, rows, :] = jnp.where(first_head, n0, n1)
                mst[g, rows, :] = jnp.where(first_head, m0, m1)
                lst[g, rows, :] = jnp.where(first_head, l0, l1)

        scores(0, s_even)

        def body(h, carry):
            it = 2 * h
            scores(it + 1, s_odd)
            weighted_values(it, s_even)
            scores(jnp.minimum(it + 2, n_iter - 1), s_even)
            weighted_values(it + 1, s_odd)
            return carry

        lax.fori_loop(0, n_iter // 2, body, 0)

    run_group(0, q0, k0, v0)
    run_group(1, q1, k1, v1)
    run_group(2, q2, k2, v2)

    def finish(n, carry):
        rows = pl.ds(pl.multiple_of(n * ATT_BLOCK, ATT_BLOCK), ATT_BLOCK)
        ms = [mst[g, rows, :] for g in range(3)]
        top = jnp.maximum(jnp.maximum(ms[0], ms[1]), ms[2])
        scale = [jnp.exp2(m - top) for m in ms]
        num = scale[0] * acc[0, rows, :] + scale[1] * acc[1, rows, :] + scale[2] * acc[2, rows, :]
        den = scale[0] * lst[0, rows, :] + scale[1] * lst[1, rows, :] + scale[2] * lst[2, rows, :]
        o_ref[0, rows, :] = (num / den).astype(BF16)
        return carry

    lax.fori_loop(0, nblk, finish, 0)


def _attention(qkv0, qkv1, qkv2, slopes):
    b, _, s, _ = qkv0.shape
    pairs = HEADS_PER_GROUP // 2
    col_blocks = HEADS_PER_GROUP * HEAD_DIM // LANES

    def specs(arr):
        _, d, sub, _ = arr.shape
        return [pl.BlockSpec((1, d, sub, LANES),
                             functools.partial(lambda bi, p, sec: (bi, 0, 0, sec * col_blocks + p), sec=sec))
                for sec in range(3)]

    return pl.pallas_call(
        _attn_kernel,
        grid=(b, pairs),
        in_specs=[pl.BlockSpec(memory_space=pltpu.SMEM)] + specs(qkv0) + specs(qkv1) + specs(qkv2),
        out_specs=pl.BlockSpec((1, s, LANES), lambda bi, p: (bi, 0, p)),
        out_shape=jax.ShapeDtypeStruct((b, s, ATT_OUT_WIDTH), BF16),
        scratch_shapes=[pltpu.VMEM((3, 2, 2, ATT_BLOCK, 2 * ATT_BLOCK), F32),
                        pltpu.VMEM((3, s, LANES), F32),
                        pltpu.VMEM((3, s, LANES), F32),
                        pltpu.VMEM((3, s, LANES), F32),
                        pltpu.VMEM((2 * ATT_UNROLL, ATT_BLOCK, 2 * ATT_BLOCK), F32),
                        pltpu.VMEM((2 * ATT_UNROLL, ATT_BLOCK, 2 * ATT_BLOCK), F32)],
        compiler_params=pltpu.CompilerParams(
            dimension_semantics=("arbitrary", "arbitrary"), vmem_limit_bytes=VMEM_LIMIT),
        name="attention",
    )(slopes, qkv0, qkv0, qkv0, qkv1, qkv1, qkv1, qkv2, qkv2, qkv2)


def _mixtail_kernel(u_ref, halo_ref, att_ref, gate_ref, x_ref,
                    gt_m_ref, sc_f_ref, sh_f_ref, g_post_ref, g_pre_ref,
                    wpg_ref, pscale_ref, wbp_ref, wba_ref, wout_ref, wr_ref, br_ref,
                    xmid_ref, route_ref, meta_ref, counts_ref, xs_hbm,
                    pu, lv, xbuf, zbuf, fill, meta_s, sem, sem_s):
    i = pl.program_id(1)
    tm = x_ref.shape[1]
    step = pl.program_id(0) * pl.num_programs(1) + i
    last = pl.num_programs(0) * pl.num_programs(1) - 1
    slot = step % 2
    region = _region_rows(pl.num_programs(0) * pl.num_programs(1) * tm)

    @pl.when(step == 0)
    def _():
        fill[...] = jnp.zeros_like(fill)
        pu[0:POOL_PAD - POOL_HALO, :] = jnp.zeros((POOL_PAD - POOL_HALO, POOL_WIDTH), F32)
        lv[:, 0:POOL_PAD - POOL_HALO, :] = jnp.zeros((2, POOL_PAD - POOL_HALO, POOL_GROUP_DIM), F32)
        for e in range(N_EXPERTS):
            meta_s[1, 1, e] = 0
            meta_s[1, 0, e] = 0

    def send_tile(which):
        _tile_run_copies(lambda e: meta_s[which, 1, e], xbuf.at[which], None,
                         xs_hbm, lambda e: e * region + meta_s[which, 0, e], sem.at[which])

    send_tile(1 - slot)

    head = POOL_PAD - POOL_HALO
    halo = halo_ref[0].astype(F32)
    pu[head:POOL_PAD, :] = jnp.where(i > 0, halo, jnp.zeros_like(halo))
    pu[POOL_PAD:POOL_PAD + tm, :] = u_ref[0].astype(F32)
    t = i * tm + lax.broadcasted_iota(I32, (tm, 1), 0)
    pooled_groups = []
    for g, w in enumerate(POOL_WINDOWS):
        cols = slice(g * POOL_GROUP_DIM, (g + 1) * POOL_GROUP_DIM)
        read = lambda start, n: pu[pl.ds(start, n), cols]
        shift, level = 1, 0
        while 2 * shift < w:
            partial = read(head, tm + POOL_HALO) + read(head - shift, tm + POOL_HALO)
            buf = lv.at[level % 2]
            buf[pl.ds(head, tm + POOL_HALO), :] = partial
            read = lambda start, n, buf=buf: buf[pl.ds(start, n), :]
            shift, level = 2 * shift, level + 1
        win = read(POOL_PAD, tm) + read(POOL_PAD - shift, tm)
        count = jnp.minimum(t + 1, w).astype(F32)
        pooled_groups.append((win / count - pu[pl.ds(POOL_PAD, tm), cols]).astype(BF16))
    mixed = _dot(jnp.concatenate(pooled_groups, axis=1), wpg_ref[...]) * pscale_ref[...]
    y_pool = _dot(mixed.astype(BF16), wbp_ref[...])

    y_att = _dot(att_ref[0], wba_ref[...])
    merged = (gate_ref[0, :, 0:D_MODEL] * y_pool.astype(BF16)
              + gate_ref[0, :, D_MODEL:2 * D_MODEL] * y_att.astype(BF16))
    y = _dot(merged, wout_ref[...])
    x_mid = x_ref[0] + gt_m_ref[0] * _rmsnorm(y, g_post_ref[...])
    xmid_ref[0] = x_mid

    h2 = _rmsnorm(x_mid, g_pre_ref[...]) * (1.0 + sc_f_ref[0]) + sh_f_ref[0]

    h2b = h2.astype(BF16)
    h2lo = (h2 - h2b.astype(F32)).astype(BF16)
    by_hi = _dot_nt(wr_ref[...], h2b)
    logits = (by_hi[0:ROUTER_ROWS] + (by_hi[ROUTER_ROWS:2 * ROUTER_ROWS]
                                      + _dot_nt(wr_ref[0:ROUTER_ROWS, :], h2lo)) + br_ref[...])
    neg_inf = -jnp.inf
    far = float(LANES)
    gl = logits[N_EXPERTS:N_EXPERTS + N_EXPERT_GROUPS, :]
    grow = lax.broadcasted_iota(I32, gl.shape, 0).astype(F32)
    gmax = jnp.max(gl, axis=0, keepdims=True)
    gsel = jnp.min(jnp.where(gl == gmax, grow, far), axis=0, keepdims=True)
    p_group = 1.0 / jnp.sum(jnp.exp(gl - gmax), axis=0, keepdims=True)
    erow = lax.broadcasted_iota(I32, (N_EXPERTS, tm), 0).astype(F32)
    e_lo = gsel * float(EXPERTS_PER_GROUP)
    el = jnp.where((erow >= e_lo) & (erow < e_lo + float(EXPERTS_PER_GROUP)), logits[0:N_EXPERTS, :], neg_inf)
    v1 = jnp.max(el, axis=0, keepdims=True)
    i1 = jnp.min(jnp.where(el == v1, erow, far), axis=0, keepdims=True)
    el2 = jnp.where(erow == i1, neg_inf, el)
    v2 = jnp.max(el2, axis=0, keepdims=True)
    i2 = jnp.min(jnp.where(el2 == v2, erow, far), axis=0, keepdims=True)
    e21 = jnp.exp(v2 - v1)
    w1 = p_group / (1.0 + e21)
    w2 = p_group * e21 / (1.0 + e21)

    pick1 = erow == i1
    pick2 = erow == i2
    assign = jnp.where(pick1 | pick2, 1.0, 0.0)
    before = (lax.broadcasted_iota(I32, (tm, tm), 0) < lax.broadcasted_iota(I32, (tm, tm), 1))
    rank = _dot(assign.astype(BF16), jnp.where(before, 1.0, 0.0).astype(BF16))
    assign_pad = jnp.concatenate([assign, jnp.zeros((LANES - N_EXPERTS, tm), F32)], axis=0).astype(BF16)
    cnt_row = _dot_nt(jnp.ones((SUBLANES, tm), BF16), assign_pad)
    e_col = lax.broadcasted_iota(I32, (N_EXPERTS, LANES), 0)
    e_lane = lax.broadcasted_iota(I32, (N_EXPERTS, LANES), 1)
    run_start = jnp.sum(jnp.where(e_lane < e_col, cnt_row[0:1, :], 0.0), axis=1, keepdims=True)
    pos = rank + run_start
    key1 = jnp.sum(jnp.where(pick1, pos, 0.0), axis=0, keepdims=True)
    key2 = jnp.sum(jnp.where(pick2, pos, 0.0), axis=0, keepdims=True)

    row8 = lax.broadcasted_iota(I32, (SUBLANES, tm), 0)
    route_ref[...] = jnp.where(row8 == 0, key1, jnp.where(row8 == 1, key2,
                               jnp.where(row8 == 2, w1, jnp.where(row8 == 3, w2, 0.0))))

    filled = fill[...]
    mrow = lax.broadcasted_iota(I32, (SUBLANES, LANES), 0)
    meta_ref[0] = jnp.where(mrow == 0, filled, jnp.where(mrow == 1, cnt_row, 0.0)).astype(I32)
    fill[...] = filled + cnt_row
    counts_ref[...] = (filled + cnt_row).astype(I32)

    srow = lax.broadcasted_iota(I32, (SORT_ROWS, tm), 0).astype(F32)
    perm = jnp.where((srow == key1) | (srow == key2), 1.0, 0.0).astype(BF16)
    def wait_tile(which):
        pltpu.make_async_copy(xbuf.at[which], xs_hbm.at[pl.ds(0, SORT_ROWS * ROW_WORDS), :],
                              sem.at[which]).wait()

    @pl.when(step > 1)
    def _():
        wait_tile(slot)

    xslot = xbuf.at[slot]
    for w in range(ROW_WORDS):
        pair = _dot(perm, h2b[:, 2 * w * LANES:(2 * w + 2) * LANES])
        _pack_rows(xslot, w, SORT_ROWS, pair[:, :LANES], pair[:, LANES:])

    meta_copy = pltpu.make_async_copy(meta_ref.at[0], meta_s.at[slot], sem_s)
    meta_copy.start()
    meta_copy.wait()

    @pl.when(step == last)
    def _():
        send_tile(slot)

        @pl.when(step > 0)
        def _():
            wait_tile(1 - slot)

        wait_tile(slot)
        zbuf[...] = jnp.zeros_like(zbuf)

        def pad_copy(e):
            end = e * region + meta_s[slot, 0, e] + meta_s[slot, 1, e]
            return pltpu.make_async_copy(
                zbuf, xs_hbm.at[pl.ds(end * ROW_WORDS, MOE_BLOCK * ROW_WORDS), :], sem_s)

        def start_pad(e, carry):
            pad_copy(e).start()
            return carry

        def wait_pad(e, carry):
            pad_copy(e).wait()
            return carry

        lax.fori_loop(0, N_EXPERTS, start_pad, 0)
        lax.fori_loop(0, N_EXPERTS, wait_pad, 0)


def _mixtail(u, att, gates, x, gt_m, sc_f, sh_f, g_post, g_pre,
             wpg, pscale, wbp, wba, wout, wr, br):
    b, s, d = x.shape
    tm = ROW_TILE
    tiles = s // tm
    n_tiles = b * tiles
    halo_blocks = tm // POOL_HALO
    region = _region_rows(b * s)
    const2 = lambda bi, i: (0, 0)
    const3 = lambda bi, i: (0, 0, 0)
    per_b = lambda bi, i: (bi, 0, 0)
    tile = lambda bi, i: (bi, i, 0)
    single = dict(pipeline_mode=pl.Buffered(1))
    return pl.pallas_call(
        _mixtail_kernel,
        grid=(b, tiles),
        in_specs=[pl.BlockSpec((1, tm, POOL_WIDTH), tile),
                  pl.BlockSpec((1, POOL_HALO, POOL_WIDTH),
                               lambda bi, i: (bi, jnp.maximum(i * halo_blocks - 1, 0), 0)),
                  pl.BlockSpec((1, tm, ATT_OUT_WIDTH), tile),
                  pl.BlockSpec((1, tm, 2 * D_MODEL), tile),
                  pl.BlockSpec((1, tm, d), tile),
                  pl.BlockSpec((1, 1, d), per_b),
                  pl.BlockSpec((1, 1, d), per_b),
                  pl.BlockSpec((1, 1, d), per_b),
                  pl.BlockSpec((1, d), const2),
                  pl.BlockSpec((1, d), const2),
                  pl.BlockSpec(wpg.shape, const2, **single),
                  pl.BlockSpec((1, POOL_WIDTH), const2),
                  pl.BlockSpec(wbp.shape, const2, **single),
                  pl.BlockSpec(wba.shape, const2, **single),
                  pl.BlockSpec(wout.shape, const2, **single),
                  pl.BlockSpec(wr.shape, const2, **single),
                  pl.BlockSpec(br.shape, const2)],
        out_specs=[pl.BlockSpec((1, tm, d), tile),
                   pl.BlockSpec((SUBLANES, tm), lambda bi, i: (0, bi * tiles + i)),
                   pl.BlockSpec((1, SUBLANES, LANES), lambda bi, i: (bi * tiles + i, 0, 0)),
                   pl.BlockSpec((SUBLANES, LANES), const2),
                   pl.BlockSpec(memory_space=pl.ANY)],
        out_shape=[jax.ShapeDtypeStruct((b, s, d), F32),
                   jax.ShapeDtypeStruct((SUBLANES, b * s), F32),
                   jax.ShapeDtypeStruct((n_tiles, SUBLANES, LANES), I32),
                   jax.ShapeDtypeStruct((SUBLANES, LANES), I32),
                   jax.ShapeDtypeStruct((N_EXPERTS * region * ROW_WORDS, LANES), I32)],
        scratch_shapes=[pltpu.VMEM((POOL_PAD + tm, POOL_WIDTH), F32),
                        pltpu.VMEM((2, POOL_PAD + tm, POOL_GROUP_DIM), F32),
                        pltpu.VMEM((2, SORT_ROWS * ROW_WORDS, LANES), I32),
                        pltpu.VMEM((MOE_BLOCK * ROW_WORDS, LANES), I32),
                        pltpu.VMEM((SUBLANES, LANES), F32),
                        pltpu.SMEM((2, SUBLANES, LANES), I32),
                        pltpu.SemaphoreType.DMA((2,)),
                        pltpu.SemaphoreType.DMA(())],
        compiler_params=pltpu.CompilerParams(
            dimension_semantics=("arbitrary", "arbitrary"), vmem_limit_bytes=VMEM_LIMIT),
        name="mixtail",
    )(u, u, att, gates, x, gt_m, sc_f, sh_f, g_post, g_pre,
      wpg, pscale, wbp, wba, wout, wr, br)


def _expert_kernel(counts_ref, xs_hbm, wg_ref, wu_ref, wd_ref, ys_hbm,
                   wg_bf, wu_bf, wd_bf, xbuf, ybuf, state, semx, semy):
    e = pl.program_id(0)
    bm = MOE_BLOCK
    nx, ny = EXPERT_X_BUFFERS, EXPERT_Y_BUFFERS
    block_words = bm * ROW_WORDS
    region = xs_hbm.shape[0] // (N_EXPERTS * ROW_WORDS)

    def n_blocks(ex):
        return (counts_ref[ex] + (bm - 1)) // bm

    def block_rows(ref, ex, k):
        start = pl.multiple_of((ex * region + k * bm) * ROW_WORDS, block_words)
        return ref.at[pl.ds(start, block_words), :]

    def x_copy(ex, k, s):
        return pltpu.make_async_copy(block_rows(xs_hbm, ex, k), xbuf.at[s], semx.at[s])

    def y_copy(k, s):
        return pltpu.make_async_copy(ybuf.at[s], block_rows(ys_hbm, e, k), semy.at[s])

    @pl.when(e == 0)
    def _():
        for j in range(4):
            state[j] = 0

    def fetch_through(target):
        def more(c):
            pe, _, pg = c
            return (pg < target) & (pe < N_EXPERTS)

        def step(c):
            pe, pk, pg = c
            has = pk < n_blocks(pe)

            @pl.when(has)
            def _():
                x_copy(pe, pk, pg % nx).start(priority=BLOCK_DMA_PRIORITY)

            return (jnp.where(has, pe, pe + 1), jnp.where(has, pk + 1, 0), pg + has.astype(I32))

        pe, pk, pg = lax.while_loop(more, step, (state[1], state[2], state[3]))
        state[1] = pe
        state[2] = pk
        state[3] = pg

    wg_bf[...] = wg_ref[...].astype(BF16)
    wu_bf[...] = wu_ref[...].astype(BF16)
    wd_bf[...] = wd_ref[...].astype(BF16)

    nb = n_blocks(e)
    done = state[0]

    def blocks(k0, count):
        ks = [k0 + c for c in range(count)]
        gs = [done + k for k in ks]
        fetch_through(gs[0] + nx)
        for k, g in zip(ks, gs):
            x_copy(e, k, g % nx).wait()

            @pl.when(g >= ny)
            def _():
                y_copy(k, g % ny).wait()

        mids = []
        for g in gs:
            x = _unpack_rows(xbuf.at[g % nx], bm).astype(BF16)
            a = _dot(x, wg_bf[...])
            u = _dot(x, wu_bf[...])
            mids.append(((a * _sigmoid(a)) * u).astype(BF16))
        for k, g, mid in zip(ks, gs, mids):
            for w in range(ROW_WORDS):
                pair = _bf16_exact(_dot(mid, wd_bf[:, 2 * w * LANES:(2 * w + 2) * LANES]))
                _pack_rows(ybuf.at[g % ny], w, bm, pair[:, :LANES], pair[:, LANES:])
            y_copy(k, g % ny).start(priority=BLOCK_DMA_PRIORITY)

    def pair_of_blocks(p, carry):
        blocks(2 * p, 2)
        return carry

    lax.fori_loop(0, nb // 2, pair_of_blocks, 0)

    @pl.when(nb % 2 == 1)
    def _():
        blocks(nb - 1, 1)

    state[0] = done + nb

    @pl.when(e == N_EXPERTS - 1)
    def _():
        total = done + nb
        for j in range(ny):
            @pl.when(total > j)
            def _():
                y_copy(0, (total - 1 - j) % ny).wait()


def _experts(xs, counts, w_gate, w_up, w_down):
    bm = MOE_BLOCK
    w_in = pl.BlockSpec((None, D_MODEL, D_EXPERT), lambda e, cnt: (e, 0, 0))
    w_out = pl.BlockSpec((None, D_EXPERT, D_MODEL), lambda e, cnt: (e, 0, 0))
    grid_spec = pltpu.PrefetchScalarGridSpec(
        num_scalar_prefetch=1,
        grid=(N_EXPERTS,),
        in_specs=[pl.BlockSpec(memory_space=pl.ANY), w_in, w_in, w_out],
        out_specs=pl.BlockSpec(memory_space=pl.ANY),
        scratch_shapes=[pltpu.VMEM((D_MODEL, D_EXPERT), BF16),
                        pltpu.VMEM((D_MODEL, D_EXPERT), BF16),
                        pltpu.VMEM((D_EXPERT, D_MODEL), BF16),
                        pltpu.VMEM((EXPERT_X_BUFFERS, bm * ROW_WORDS, LANES), I32),
                        pltpu.VMEM((EXPERT_Y_BUFFERS, bm * ROW_WORDS, LANES), I32),
                        pltpu.SMEM((4,), I32),
                        pltpu.SemaphoreType.DMA((EXPERT_X_BUFFERS,)),
                        pltpu.SemaphoreType.DMA((EXPERT_Y_BUFFERS,))],
    )
    return pl.pallas_call(
        _expert_kernel,
        grid_spec=grid_spec,
        out_shape=jax.ShapeDtypeStruct(xs.shape, I32),
        compiler_params=pltpu.CompilerParams(
            dimension_semantics=("arbitrary",), vmem_limit_bytes=VMEM_LIMIT),
        name="experts",
    )(counts, xs, w_gate, w_up, w_down)


def _combine_kernel(meta_ref, y_hbm, route_ref, xmid_ref, gt_ref, g_ref, o_ref, ybuf, rt_scr, sem):
    step = pl.program_id(0)
    n_steps = pl.num_programs(0)
    tm = ROW_TILE
    slot = step % COMBINE_BUFFERS
    region = _region_rows(n_steps * tm)

    def fetch_tile(tile, live):
        tile = jnp.minimum(tile, n_steps - 1)
        which = tile % COMBINE_BUFFERS
        _tile_run_copies(lambda e: jnp.where(live, meta_ref[(tile * 2 + 1) * N_EXPERTS + e], 0),
                         y_hbm, lambda e: e * region + meta_ref[(tile * 2) * N_EXPERTS + e],
                         ybuf.at[which], None, sem.at[which])

    @pl.when(step == 0)
    def _():
        for ahead in range(COMBINE_BUFFERS - 1):
            fetch_tile(ahead, ahead < n_steps)

    pltpu.make_async_copy(y_hbm.at[pl.ds(0, SORT_ROWS * ROW_WORDS), :], ybuf.at[slot], sem.at[slot]).wait()
    yb = _unpack_rows(ybuf.at[slot], SORT_ROWS).astype(BF16)

    fetch_tile(step + COMBINE_BUFFERS - 1, step + COMBINE_BUFFERS - 1 < n_steps)

    rt_scr[...] = jnp.zeros_like(rt_scr)
    rt_scr[0:SUBLANES, :] = route_ref[...]
    cols = jnp.concatenate([rt_scr[:, c * LANES:(c + 1) * LANES].T for c in range(tm // LANES)], axis=0)
    key1, key2, w1, w2 = cols[:, 0:1], cols[:, 1:2], cols[:, 2:3], cols[:, 3:4]

    spos = lax.broadcasted_iota(I32, (tm, SORT_ROWS), 1).astype(F32)
    y1 = _dot(jnp.where(spos == key1, 1.0, 0.0).astype(BF16), yb)
    y2 = _dot(jnp.where(spos == key2, 1.0, 0.0).astype(BF16), yb)
    y = w1 * y1 + w2 * y2
    o_ref[...] = xmid_ref[...] + gt_ref[0] * _rmsnorm(y, g_ref[...])


def _combine(meta, ys, route, x_mid, gt_f, g_post, seq):
    t, d = x_mid.shape
    tm = ROW_TILE
    tiles_per_seq = seq // tm
    grid_spec = pltpu.PrefetchScalarGridSpec(
        num_scalar_prefetch=1,
        grid=(t // tm,),
        in_specs=[pl.BlockSpec(memory_space=pl.ANY),
                  pl.BlockSpec((SUBLANES, tm), lambda i, m: (0, i)),
                  pl.BlockSpec((tm, d), lambda i, m: (i, 0)),
                  pl.BlockSpec((1, 1, d), lambda i, m: (i // tiles_per_seq, 0, 0)),
                  pl.BlockSpec((1, d), lambda i, m: (0, 0))],
        out_specs=pl.BlockSpec((tm, d), lambda i, m: (i, 0)),
        scratch_shapes=[pltpu.VMEM((COMBINE_BUFFERS, SORT_ROWS * ROW_WORDS, LANES), I32),
                        pltpu.VMEM((LANES, tm), F32),
                        pltpu.SemaphoreType.DMA((COMBINE_BUFFERS,))],
    )
    return pl.pallas_call(
        _combine_kernel,
        grid_spec=grid_spec,
        out_shape=jax.ShapeDtypeStruct((t, d), F32),
        compiler_params=pltpu.CompilerParams(
            dimension_semantics=("arbitrary",), vmem_limit_bytes=VMEM_LIMIT),
        name="combine",
    )(meta, ys, route, x_mid, gt_f, g_post)


def kernel(x, c, w_ada, b_ada, g_pre_mix, g_post_mix, g_pre_ffn, g_post_ffn, w_in, w_pool_group, pool_scale, w_branch_pool, w_branch_att, w_out, w_group_router, b_group_router, w_expert_router, b_expert_router, w_exp_gate, w_exp_up, w_exp_down):
    b, s, d = x.shape
    assert d == D_MODEL and s % (ATT_BLOCK * 2 * ATT_DILATIONS[2]) == 0 and s % ROW_TILE == 0
    assert (b * s) % MOE_BLOCK == 0
    depth = w_ada.shape[0]
    slopes = jnp.exp2(-ALIBI_MAX_BIAS * jnp.arange(1, N_ATT_HEADS + 1, dtype=F32) / N_ATT_HEADS)
    q_scale = HEAD_DIM ** -0.5 * LOG2_E

    for layer in range(depth):
        mod = _adaln(c, w_ada[layer], b_ada[layer]).reshape(b, 6, 1, d)
        sh_m, sc_m, gt_m, sh_f, sc_f, gt_f = [mod[:, j] for j in range(6)]

        wl = w_in[layer]
        q_lo, k_lo, v_lo = POOL_WIDTH, POOL_WIDTH + 768, POOL_WIDTH + 2 * 768
        group_cols = []
        for g in range(3):
            sl = slice(g * 256, (g + 1) * 256)
            group_cols += [wl[:, q_lo:k_lo][:, sl] * q_scale, wl[:, k_lo:v_lo][:, sl],
                           wl[:, v_lo:v_lo + 768][:, sl]]
        w_perm = jnp.concatenate([wl[:, :POOL_WIDTH]] + group_cols + [wl[:, v_lo + 768:]],
                                 axis=1).astype(BF16)

        u, qkv0, qkv1, qkv2, gates = _inproj(x, g_pre_mix[layer].reshape(1, d), sc_m, sh_m, w_perm)
        att = _attention(qkv0, qkv1, qkv2, slopes)

        zero_block = jnp.zeros((POOL_GROUP_DIM, POOL_GROUP_DIM), F32)
        wpg_diag = jnp.block([[w_pool_group[layer, g] if g == h else zero_block
                               for h in range(len(POOL_WINDOWS))] for g in range(len(POOL_WINDOWS))])

        pad_rows = ROUTER_ROWS - N_EXPERTS - N_EXPERT_GROUPS
        wrt = jnp.concatenate([w_expert_router[layer].T, w_group_router[layer].T,
                               jnp.zeros((pad_rows, d), F32)], axis=0)
        wrt_hi = wrt.astype(BF16)
        wr = jnp.concatenate([wrt_hi, (wrt - wrt_hi.astype(F32)).astype(BF16)], axis=0)
        br = jnp.concatenate([b_expert_router[layer], b_group_router[layer],
                              jnp.zeros((pad_rows,), F32)]).reshape(ROUTER_ROWS, 1)

        x_mid, route, meta, counts, xs = _mixtail(
            u, att, gates, x, gt_m, sc_f, sh_f,
            g_post_mix[layer].reshape(1, d), g_pre_ffn[layer].reshape(1, d),
            wpg_diag.astype(BF16), pool_scale[layer].reshape(1, POOL_WIDTH),
            w_branch_pool[layer].astype(BF16), w_branch_att[layer].astype(BF16),
            w_out[layer].astype(BF16), wr, br)

        ys = _experts(xs, counts[0, :N_EXPERTS], w_exp_gate[layer], w_exp_up[layer], w_exp_down[layer])
        run_meta = meta[:, 0:2, 0:N_EXPERTS].reshape(-1)
        x = _combine(run_meta, ys, route, x_mid.reshape(b * s, d), gt_f,
                     g_post_ffn[layer].reshape(1, d), s).reshape(b, s, d)
    return x
```

```python
import functools

import jax
import jax.numpy as jnp
from jax import lax
from jax.experimental import pallas as pl
from jax.experimental.pallas import tpu as pltpu

F32 = jnp.float32
BF16 = jnp.bfloat16
I32 = jnp.int32
HIGHEST = lax.Precision.HIGHEST

D_MODEL = 1024
LANES = 128
SUBLANES = 8
ROW_WORDS = D_MODEL // (2 * LANES)
HI_HALF = -65536

POOL_WINDOWS = (2, 4, 8, 16)
POOL_GROUP_DIM = 128
POOL_WIDTH = 512
POOL_HALO = 16
POOL_PAD = POOL_HALO + 8

HEAD_DIM = 64
ATT_DILATIONS = (1, 4, 16)
ATT_REACH = 128
ATT_BLOCK = 128
ATT_UNROLL = 4
HEADS_PER_GROUP = 4
N_ATT_HEADS = 12
GROUP_QKV = 3 * HEADS_PER_GROUP * HEAD_DIM
ATT_OUT_WIDTH = 256
ALIBI_MAX_BIAS = 8.0
IN_WIDTH = POOL_WIDTH + 3 * GROUP_QKV + 2 * D_MODEL
MASKED = -1e30
LOG2_E = 1.4426950408889634

N_EXPERT_GROUPS = 4
EXPERTS_PER_GROUP = 8
N_EXPERTS = 32
ROUTER_ROWS = 48
D_EXPERT = 512
RMS_EPS = 1e-6

ROW_TILE = 512
ROW_SPLIT = 2
SORT_ROWS = 2 * ROW_TILE
MOE_BLOCK = 256
BLOCK_DMA_PRIORITY = 1
EXPERT_X_BUFFERS = 6
EXPERT_Y_BUFFERS = 4
COMBINE_BUFFERS = 3
VMEM_LIMIT = 52 * 1024 * 1024


def _sigmoid(x):
    return 0.5 * jnp.tanh(0.5 * x) + 0.5


def _rmsnorm(x, g):
    return x * lax.rsqrt(jnp.mean(x * x, axis=-1, keepdims=True) + RMS_EPS) * g


def _dot(a, b):
    return jnp.dot(a, b, preferred_element_type=F32)


def _dot_nt(a, b, **kw):
    return lax.dot_general(a, b, (((1,), (1,)), ((), ())), preferred_element_type=F32, **kw)


def _pack_rows(ref, w, n, lo, hi):
    word = (lax.shift_right_logical(pltpu.bitcast(lo, I32), 16) | (pltpu.bitcast(hi, I32) & HI_HALF))
    ref[pl.ds(w, n, stride=ROW_WORDS), :] = word


def _unpack_rows(ref, n):
    cols = []
    for w in range(ROW_WORDS):
        word = ref[pl.ds(w, n, stride=ROW_WORDS), :]
        cols += [pltpu.bitcast(word << 16, F32), pltpu.bitcast(word & HI_HALF, F32)]
    return jnp.concatenate(cols, axis=1)


def _bf16_exact(x):
    return x.astype(BF16).astype(F32)


def _region_rows(n_tokens):
    return n_tokens + MOE_BLOCK


def _run_copies(n, src, src_row, dst, dst_row, sem):
    @pl.when(n > 0)
    def _():
        pltpu.make_async_copy(
            src.at[pl.ds(src_row * ROW_WORDS, n * ROW_WORDS), :],
            dst.at[pl.ds(dst_row * ROW_WORDS, n * ROW_WORDS), :], sem).start()


def _tile_run_copies(counts, src, src_rows, dst, dst_rows, sem):
    local = 0
    for e in range(N_EXPERTS):
        n = counts(e)
        _run_copies(n, src, local if src_rows is None else src_rows(e),
                    dst, local if dst_rows is None else dst_rows(e), sem)
        local = local + n


def _adaln_kernel(c_ref, w_ref, b_ref, o_ref):
    c = c_ref[...]
    a = c * _sigmoid(c)
    o_ref[...] = jnp.dot(a, w_ref[...], preferred_element_type=F32, precision=HIGHEST) + b_ref[...]


def _adaln(c, w_ada, b_ada):
    b, d = c.shape
    n = w_ada.shape[1]
    rows = -(-b // SUBLANES) * SUBLANES
    cp = jnp.pad(c, ((0, rows - b), (0, 0)))
    nt = 1536
    out = pl.pallas_call(
        _adaln_kernel,
        grid=(n // nt,),
        in_specs=[pl.BlockSpec((rows, d), lambda j: (0, 0)),
                  pl.BlockSpec((d, nt), lambda j: (0, j)),
                  pl.BlockSpec((1, nt), lambda j: (0, j))],
        out_specs=pl.BlockSpec((rows, nt), lambda j: (0, j)),
        out_shape=jax.ShapeDtypeStruct((rows, n), F32),
        compiler_params=pltpu.CompilerParams(vmem_limit_bytes=VMEM_LIMIT),
        name="adaln",
    )(cp, w_ada, b_ada.reshape(1, n))
    return out[:b]


def _inproj_kernel(x_ref, g_ref, sc_ref, sh_ref, w_ref,
                   u_ref, qkv0_ref, qkv1_ref, qkv2_ref, gate_ref, h_scr, p_scr):
    tm = x_ref.shape[1]
    hm = tm // ROW_SPLIT
    for part in range(ROW_SPLIT):
        rows = slice(part * hm, (part + 1) * hm)
        h = _rmsnorm(x_ref[0, rows, :], g_ref[...]) * (1.0 + sc_ref[0]) + sh_ref[0]
        h_scr[rows, :] = h.astype(BF16)
        hb = h_scr[rows, :]

        u_ref[0, rows, :] = _dot(hb, w_ref[:, 0:POOL_WIDTH]).astype(BF16)

        col = POOL_WIDTH
        qkv0_ref[0, 0, rows, :] = _dot(hb, w_ref[:, col:col + GROUP_QKV]).astype(BF16)
        for gi, (out_ref, d) in enumerate(((qkv1_ref, ATT_DILATIONS[1]), (qkv2_ref, ATT_DILATIONS[2]))):
            col += GROUP_QKV
            proj = _dot(hb, w_ref[:, col:col + GROUP_QKV])
            stage = p_scr.at[part, gi]
            for cb in range(GROUP_QKV // LANES):
                stage[cb] = proj[:, cb * LANES:(cb + 1) * LANES]
            sub = hm // d
            for r in range(d):
                out_ref[0, r, part * sub:(part + 1) * sub, :] = jnp.concatenate(
                    [stage[cb, pl.ds(r, sub, stride=d), :] for cb in range(GROUP_QKV // LANES)],
                    axis=1).astype(BF16)
        col += GROUP_QKV

        chunk = 512
        for j in range(2 * D_MODEL // chunk):
            g = _dot(hb, w_ref[:, col + j * chunk:col + (j + 1) * chunk])
            gate_ref[0, rows, j * chunk:(j + 1) * chunk] = _sigmoid(g).astype(BF16)


def _inproj(x, g_pre, sc, sh, w_perm):
    b, s, d = x.shape
    tm = ROW_TILE
    d1, d2 = ATT_DILATIONS[1], ATT_DILATIONS[2]
    grid = (b, s // tm)
    const = lambda bi, i: (0, 0)
    per_b = lambda bi, i: (bi, 0, 0)
    return pl.pallas_call(
        _inproj_kernel,
        grid=grid,
        in_specs=[pl.BlockSpec((1, tm, d), lambda bi, i: (bi, i, 0)),
                  pl.BlockSpec((1, d), const),
                  pl.BlockSpec((1, 1, d), per_b),
                  pl.BlockSpec((1, 1, d), per_b),
                  pl.BlockSpec((d, IN_WIDTH), const, pipeline_mode=pl.Buffered(1))],
        out_specs=[pl.BlockSpec((1, tm, POOL_WIDTH), lambda bi, i: (bi, i, 0)),
                   pl.BlockSpec((1, 1, tm, GROUP_QKV), lambda bi, i: (bi, 0, i, 0)),
                   pl.BlockSpec((1, d1, tm // d1, GROUP_QKV), lambda bi, i: (bi, 0, i, 0)),
                   pl.BlockSpec((1, d2, tm // d2, GROUP_QKV), lambda bi, i: (bi, 0, i, 0)),
                   pl.BlockSpec((1, tm, 2 * D_MODEL), lambda bi, i: (bi, i, 0))],
        out_shape=[jax.ShapeDtypeStruct((b, s, POOL_WIDTH), BF16),
                   jax.ShapeDtypeStruct((b, 1, s, GROUP_QKV), BF16),
                   jax.ShapeDtypeStruct((b, d1, s // d1, GROUP_QKV), BF16),
                   jax.ShapeDtypeStruct((b, d2, s // d2, GROUP_QKV), BF16),
                   jax.ShapeDtypeStruct((b, s, 2 * D_MODEL), BF16)],
        scratch_shapes=[pltpu.VMEM((tm, d), BF16),
                        pltpu.VMEM((ROW_SPLIT, 2, GROUP_QKV // LANES, tm // ROW_SPLIT, LANES), F32)],
        compiler_params=pltpu.CompilerParams(
            dimension_semantics=("arbitrary", "arbitrary"), vmem_limit_bytes=VMEM_LIMIT),
        name="inproj",
    )(x, g_pre, sc, sh, w_perm)


def _attn_kernel(slopes_ref, q0, k0, v0, q1, k1, v1, q2, k2, v2, o_ref,
                 bias_scr, acc, mst, lst, s_even, s_odd):
    pair = pl.program_id(1)
    seq = o_ref.shape[1]
    nblk = seq // ATT_BLOCK
    lane = lax.broadcasted_iota(I32, (ATT_BLOCK, LANES), 1)
    first_head = lane < HEAD_DIM
    half_lane = lax.broadcasted_iota(I32, (ATT_BLOCK // 2, LANES), 1)
    head_bits = (jnp.where(half_lane < HEAD_DIM, -1, 0), jnp.where(half_lane < HEAD_DIM, 0, -1))

    qi = lax.broadcasted_iota(I32, (ATT_BLOCK, 2 * ATT_BLOCK), 0)
    kj = lax.broadcasted_iota(I32, (ATT_BLOCK, 2 * ATT_BLOCK), 1)
    delta = ATT_BLOCK + qi - kj
    valid = (delta >= 0) & (delta <= ATT_REACH)
    delta0 = qi - kj
    valid0 = delta0 >= 0
    for g, d in enumerate(ATT_DILATIONS):
        for j in range(2):
            slope = slopes_ref[g * HEADS_PER_GROUP + 2 * pair + j]
            slope = slope * LOG2_E
            bias_scr[g, j, 0] = jnp.where(valid0, -slope * (delta0 * d).astype(F32), MASKED)
            bias_scr[g, j, 1] = jnp.where(valid, -slope * (delta * d).astype(F32), MASKED)

    def run_group(g, q_ref, k_ref, v_ref):
        d = ATT_DILATIONS[g]
        per_res = nblk // d
        n_iter = nblk // ATT_UNROLL

        def block_index(it, k):
            n = it * ATT_UNROLL + k
            return n // per_res, n % per_res

        def key_rows(i):
            lo = jnp.maximum(i - 1, 0)
            return pl.ds(pl.multiple_of(lo * ATT_BLOCK, ATT_BLOCK), 2 * ATT_BLOCK)

        def scores(it, s_ref):
            for k in range(ATT_UNROLL):
                r, i = block_index(it, k)
                q = q_ref[0, r, pl.ds(pl.multiple_of(i * ATT_BLOCK, ATT_BLOCK), ATT_BLOCK), :]
                kw = k_ref[0, r, key_rows(i), :]
                qbits = pltpu.bitcast(q, I32)
                for j in range(2):
                    qh = pltpu.bitcast(qbits & head_bits[j], BF16)
                    s_ref[2 * k + j] = _dot_nt(qh, kw) + bias_scr[g, j, jnp.minimum(i, 1)]

        def weighted_values(it, s_ref):
            for k in range(ATT_UNROLL):
                r, i = block_index(it, k)
                vw = v_ref[0, r, key_rows(i), :]
                outs = []
                for j in range(2):
                    sc = s_ref[2 * k + j]
                    mx = jnp.max(sc, axis=-1, keepdims=True)
                    p = jnp.exp2(sc - mx)
                    den = jnp.sum(p, axis=-1, keepdims=True)
                    outs.append((_dot(p.astype(BF16), vw), mx, den))
                (n0, m0, l0), (n1, m1, l1) = outs
                if d == 1:
                    rows = pl.ds(pl.multiple_of(i * ATT_BLOCK, ATT_BLOCK), ATT_BLOCK)
                else:
                    rows = pl.ds(i * (ATT_BLOCK * d) + r, ATT_BLOCK, stride=d)
                acc[g, rows, :] = jnp.where(first_head, n0, n1)
                mst[g, rows, :] = jnp.where(first_head, m0, m1)
                lst[g, rows, :] = jnp.where(first_head, l0, l1)

        scores(0, s_even)

        def body(h, carry):
            it = 2 * h
            scores(it + 1, s_odd)
            weighted_values(it, s_even)
            scores(jnp.minimum(it + 2, n_iter - 1), s_even)
            weighted_values(it + 1, s_odd)
            return carry

        lax.fori_loop(0, n_iter // 2, body, 0)

    run_group(0, q0, k0, v0)
    run_group(1, q1, k1, v1)
    run_group(2, q2, k2, v2)

    def finish(n, carry):
        rows = pl.ds(pl.multiple_of(n * ATT_BLOCK, ATT_BLOCK), ATT_BLOCK)
        ms = [mst[g, rows, :] for g in range(3)]
        top = jnp.maximum(jnp.maximum(ms[0], ms[1]), ms[2])
        scale = [jnp.exp2(m - top) for m in ms]
        num = scale[0] * acc[0, rows, :] + scale[1] * acc[1, rows, :] + scale[2] * acc[2, rows, :]
        den = scale[0] * lst[0, rows, :] + scale[1] * lst[1, rows, :] + scale[2] * lst[2, rows, :]
        o_ref[0, rows, :] = (num / den).astype(BF16)
        return carry

    lax.fori_loop(0, nblk, finish, 0)


def _attention(qkv0, qkv1, qkv2, slopes):
    b, _, s, _ = qkv0.shape
    pairs = HEADS_PER_GROUP // 2
    col_blocks = HEADS_PER_GROUP * HEAD_DIM // LANES

    def specs(arr):
        _, d, sub, _ = arr.shape
        return [pl.BlockSpec((1, d, sub, LANES),
                             functools.partial(lambda bi, p, sec: (bi, 0, 0, sec * col_blocks + p), sec=sec))
                for sec in range(3)]

    return pl.pallas_call(
        _attn_kernel,
        grid=(b, pairs),
        in_specs=[pl.BlockSpec(memory_space=pltpu.SMEM)] + specs(qkv0) + specs(qkv1) + specs(qkv2),
        out_specs=pl.BlockSpec((1, s, LANES), lambda bi, p: (bi, 0, p)),
        out_shape=jax.ShapeDtypeStruct((b, s, ATT_OUT_WIDTH), BF16),
        scratch_shapes=[pltpu.VMEM((3, 2, 2, ATT_BLOCK, 2 * ATT_BLOCK), F32),
                        pltpu.VMEM((3, s, LANES), F32),
                        pltpu.VMEM((3, s, LANES), F32),
                        pltpu.VMEM((3, s, LANES), F32),
                        pltpu.VMEM((2 * ATT_UNROLL, ATT_BLOCK, 2 * ATT_BLOCK), F32),
                        pltpu.VMEM((2 * ATT_UNROLL, ATT_BLOCK, 2 * ATT_BLOCK), F32)],
        compiler_params=pltpu.CompilerParams(
            dimension_semantics=("arbitrary", "arbitrary"), vmem_limit_bytes=VMEM_LIMIT),
        name="attention",
    )(slopes, qkv0, qkv0, qkv0, qkv1, qkv1, qkv1, qkv2, qkv2, qkv2)


def _mixtail_kernel(u_ref, halo_ref, att_ref, gate_ref, x_ref,
                    gt_m_ref, sc_f_ref, sh_f_ref, g_post_ref, g_pre_ref,
                    wpg_ref, pscale_ref, wbp_ref, wba_ref, wout_ref, wr_ref, br_ref,
                    xmid_ref, route_ref, meta_ref, counts_ref, xs_hbm,
                    pu, lv, xbuf, zbuf, fill, meta_s, sem, sem_s):
    i = pl.program_id(1)
    tm = x_ref.shape[1]
    step = pl.program_id(0) * pl.num_programs(1) + i
    last = pl.num_programs(0) * pl.num_programs(1) - 1
    slot = step % 2
    region = _region_rows(pl.num_programs(0) * pl.num_programs(1) * tm)

    @pl.when(step == 0)
    def _():
        fill[...] = jnp.zeros_like(fill)
        pu[0:POOL_PAD - POOL_HALO, :] = jnp.zeros((POOL_PAD - POOL_HALO, POOL_WIDTH), F32)
        lv[:, 0:POOL_PAD - POOL_HALO, :] = jnp.zeros((2, POOL_PAD - POOL_HALO, POOL_GROUP_DIM), F32)
        for e in range(N_EXPERTS):
            meta_s[1, 1, e] = 0
            meta_s[1, 0, e] = 0

    def send_tile(which):
        _tile_run_copies(lambda e: meta_s[which, 1, e], xbuf.at[which], None,
                         xs_hbm, lambda e: e * region + meta_s[which, 0, e], sem.at[which])

    send_tile(1 - slot)

    head = POOL_PAD - POOL_HALO
    halo = halo_ref[0].astype(F32)
    pu[head:POOL_PAD, :] = jnp.where(i > 0, halo, jnp.zeros_like(halo))
    pu[POOL_PAD:POOL_PAD + tm, :] = u_ref[0].astype(F32)
    t = i * tm + lax.broadcasted_iota(I32, (tm, 1), 0)
    pooled_groups = []
    for g, w in enumerate(POOL_WINDOWS):
        cols = slice(g * POOL_GROUP_DIM, (g + 1) * POOL_GROUP_DIM)
        read = lambda start, n: pu[pl.ds(start, n), cols]
        shift, level = 1, 0
        while 2 * shift < w:
            partial = read(head, tm + POOL_HALO) + read(head - shift, tm + POOL_HALO)
            buf = lv.at[level % 2]
            buf[pl.ds(head, tm + POOL_HALO), :] = partial
            read = lambda start, n, buf=buf: buf[pl.ds(start, n), :]
            shift, level = 2 * shift, level + 1
        win = read(POOL_PAD, tm) + read(POOL_PAD - shift, tm)
        count = jnp.minimum(t + 1, w).astype(F32)
        pooled_groups.append((win / count - pu[pl.ds(POOL_PAD, tm), cols]).astype(BF16))
    mixed = _dot(jnp.concatenate(pooled_groups, axis=1), wpg_ref[...]) * pscale_ref[...]
    y_pool = _dot(mixed.astype(BF16), wbp_ref[...])

    y_att = _dot(att_ref[0], wba_ref[...])
    merged = (gate_ref[0, :, 0:D_MODEL] * y_pool.astype(BF16)
              + gate_ref[0, :, D_MODEL:2 * D_MODEL] * y_att.astype(BF16))
    y = _dot(merged, wout_ref[...])
    x_mid = x_ref[0] + gt_m_ref[0] * _rmsnorm(y, g_post_ref[...])
    xmid_ref[0] = x_mid

    h2 = _rmsnorm(x_mid, g_pre_ref[...]) * (1.0 + sc_f_ref[0]) + sh_f_ref[0]

    h2b = h2.astype(BF16)
    h2lo = (h2 - h2b.astype(F32)).astype(BF16)
    by_hi = _dot_nt(wr_ref[...], h2b)
    logits = (by_hi[0:ROUTER_ROWS] + (by_hi[ROUTER_ROWS:2 * ROUTER_ROWS]
                                      + _dot_nt(wr_ref[0:ROUTER_ROWS, :], h2lo)) + br_ref[...])
    neg_inf = -jnp.inf
    far = float(LANES)
    gl = logits[N_EXPERTS:N_EXPERTS + N_EXPERT_GROUPS, :]
    grow = lax.broadcasted_iota(I32, gl.shape, 0).astype(F32)
    gmax = jnp.max(gl, axis=0, keepdims=True)
    gsel = jnp.min(jnp.where(gl == gmax, grow, far), axis=0, keepdims=True)
    p_group = 1.0 / jnp.sum(jnp.exp(gl - gmax), axis=0, keepdims=True)
    erow = lax.broadcasted_iota(I32, (N_EXPERTS, tm), 0).astype(F32)
    e_lo = gsel * float(EXPERTS_PER_GROUP)
    el = jnp.where((erow >= e_lo) & (erow < e_lo + float(EXPERTS_PER_GROUP)), logits[0:N_EXPERTS, :], neg_inf)
    v1 = jnp.max(el, axis=0, keepdims=True)
    i1 = jnp.min(jnp.where(el == v1, erow, far), axis=0, keepdims=True)
    el2 = jnp.where(erow == i1, neg_inf, el)
    v2 = jnp.max(el2, axis=0, keepdims=True)
    i2 = jnp.min(jnp.where(el2 == v2, erow, far), axis=0, keepdims=True)
    e21 = jnp.exp(v2 - v1)
    w1 = p_group / (1.0 + e21)
    w2 = p_group * e21 / (1.0 + e21)

    pick1 = erow == i1
    pick2 = erow == i2
    assign = jnp.where(pick1 | pick2, 1.0, 0.0)
    before = (lax.broadcasted_iota(I32, (tm, tm), 0) < lax.broadcasted_iota(I32, (tm, tm), 1))
    rank = _dot(assign.astype(BF16), jnp.where(before, 1.0, 0.0).astype(BF16))
    assign_pad = jnp.concatenate([assign, jnp.zeros((LANES - N_EXPERTS, tm), F32)], axis=0).astype(BF16)
    cnt_row = _dot_nt(jnp.ones((SUBLANES, tm), BF16), assign_pad)
    e_col = lax.broadcasted_iota(I32, (N_EXPERTS, LANES), 0)
    e_lane = lax.broadcasted_iota(I32, (N_EXPERTS, LANES), 1)
    run_start = jnp.sum(jnp.where(e_lane < e_col, cnt_row[0:1, :], 0.0), axis=1, keepdims=True)
    pos = rank + run_start
    key1 = jnp.sum(jnp.where(pick1, pos, 0.0), axis=0, keepdims=True)
    key2 = jnp.sum(jnp.where(pick2, pos, 0.0), axis=0, keepdims=True)

    row8 = lax.broadcasted_iota(I32, (SUBLANES, tm), 0)
    route_ref[...] = jnp.where(row8 == 0, key1, jnp.where(row8 == 1, key2,
                               jnp.where(row8 == 2, w1, jnp.where(row8 == 3, w2, 0.0))))

    filled = fill[...]
    mrow = lax.broadcasted_iota(I32, (SUBLANES, LANES), 0)
    meta_ref[0] = jnp.where(mrow == 0, filled, jnp.where(mrow == 1, cnt_row, 0.0)).astype(I32)
    fill[...] = filled + cnt_row
    counts_ref[...] = (filled + cnt_row).astype(I32)

    srow = lax.broadcasted_iota(I32, (SORT_ROWS, tm), 0).astype(F32)
    perm = jnp.where((srow == key1) | (srow == key2), 1.0, 0.0).astype(BF16)
    def wait_tile(which):
        pltpu.make_async_copy(xbuf.at[which], xs_hbm.at[pl.ds(0, SORT_ROWS * ROW_WORDS), :],
                              sem.at[which]).wait()

    @pl.when(step > 1)
    def _():
        wait_tile(slot)

    xslot = xbuf.at[slot]
    for w in range(ROW_WORDS):
        pair = _dot(perm, h2b[:, 2 * w * LANES:(2 * w + 2) * LANES])
        _pack_rows(xslot, w, SORT_ROWS, pair[:, :LANES], pair[:, LANES:])

    meta_copy = pltpu.make_async_copy(meta_ref.at[0], meta_s.at[slot], sem_s)
    meta_copy.start()
    meta_copy.wait()

    @pl.when(step == last)
    def _():
        send_tile(slot)

        @pl.when(step > 0)
        def _():
            wait_tile(1 - slot)

        wait_tile(slot)
        zbuf[...] = jnp.zeros_like(zbuf)

        def pad_copy(e):
            end = e * region + meta_s[slot, 0, e] + meta_s[slot, 1, e]
            return pltpu.make_async_copy(
                zbuf, xs_hbm.at[pl.ds(end * ROW_WORDS, MOE_BLOCK * ROW_WORDS), :], sem_s)

        def start_pad(e, carry):
            pad_copy(e).start()
            return carry

        def wait_pad(e, carry):
            pad_copy(e).wait()
            return carry

        lax.fori_loop(0, N_EXPERTS, start_pad, 0)
        lax.fori_loop(0, N_EXPERTS, wait_pad, 0)


def _mixtail(u, att, gates, x, gt_m, sc_f, sh_f, g_post, g_pre,
             wpg, pscale, wbp, wba, wout, wr, br):
    b, s, d = x.shape
    tm = ROW_TILE
    tiles = s // tm
    n_tiles = b * tiles
    halo_blocks = tm // POOL_HALO
    region = _region_rows(b * s)
    const2 = lambda bi, i: (0, 0)
    const3 = lambda bi, i: (0, 0, 0)
    per_b = lambda bi, i: (bi, 0, 0)
    tile = lambda bi, i: (bi, i, 0)
    single = dict(pipeline_mode=pl.Buffered(1))
    return pl.pallas_call(
        _mixtail_kernel,
        grid=(b, tiles),
        in_specs=[pl.BlockSpec((1, tm, POOL_WIDTH), tile),
                  pl.BlockSpec((1, POOL_HALO, POOL_WIDTH),
                               lambda bi, i: (bi, jnp.maximum(i * halo_blocks - 1, 0), 0)),
                  pl.BlockSpec((1, tm, ATT_OUT_WIDTH), tile),
                  pl.BlockSpec((1, tm, 2 * D_MODEL), tile),
                  pl.BlockSpec((1, tm, d), tile),
                  pl.BlockSpec((1, 1, d), per_b),
                  pl.BlockSpec((1, 1, d), per_b),
                  pl.BlockSpec((1, 1, d), per_b),
                  pl.BlockSpec((1, d), const2),
                  pl.BlockSpec((1, d), const2),
                  pl.BlockSpec(wpg.shape, const2, **single),
                  pl.BlockSpec((1, POOL_WIDTH), const2),
                  pl.BlockSpec(wbp.shape, const2, **single),
                  pl.BlockSpec(wba.shape, const2, **single),
                  pl.BlockSpec(wout.shape, const2, **single),
                  pl.BlockSpec(wr.shape, const2, **single),
                  pl.BlockSpec(br.shape, const2)],
        out_specs=[pl.BlockSpec((1, tm, d), tile),
                   pl.BlockSpec((SUBLANES, tm), lambda bi, i: (0, bi * tiles + i)),
                   pl.BlockSpec((1, SUBLANES, LANES), lambda bi, i: (bi * tiles + i, 0, 0)),
                   pl.BlockSpec((SUBLANES, LANES), const2),
                   pl.BlockSpec(memory_space=pl.ANY)],
        out_shape=[jax.ShapeDtypeStruct((b, s, d), F32),
                   jax.ShapeDtypeStruct((SUBLANES, b * s), F32),
                   jax.ShapeDtypeStruct((n_tiles, SUBLANES, LANES), I32),
                   jax.ShapeDtypeStruct((SUBLANES, LANES), I32),
                   jax.ShapeDtypeStruct((N_EXPERTS * region * ROW_WORDS, LANES), I32)],
        scratch_shapes=[pltpu.VMEM((POOL_PAD + tm, POOL_WIDTH), F32),
                        pltpu.VMEM((2, POOL_PAD + tm, POOL_GROUP_DIM), F32),
                        pltpu.VMEM((2, SORT_ROWS * ROW_WORDS, LANES), I32),
                        pltpu.VMEM((MOE_BLOCK * ROW_WORDS, LANES), I32),
                        pltpu.VMEM((SUBLANES, LANES), F32),
                        pltpu.SMEM((2, SUBLANES, LANES), I32),
                        pltpu.SemaphoreType.DMA((2,)),
                        pltpu.SemaphoreType.DMA(())],
        compiler_params=pltpu.CompilerParams(
            dimension_semantics=("arbitrary", "arbitrary"), vmem_limit_bytes=VMEM_LIMIT),
        name="mixtail",
    )(u, u, att, gates, x, gt_m, sc_f, sh_f, g_post, g_pre,
      wpg, pscale, wbp, wba, wout, wr, br)


def _expert_kernel(counts_ref, xs_hbm, wg_ref, wu_ref, wd_ref, ys_hbm,
                   wg_bf, wu_bf, wd_bf, xbuf, ybuf, state, semx, semy):
    e = pl.program_id(0)
    bm = MOE_BLOCK
    nx, ny = EXPERT_X_BUFFERS, EXPERT_Y_BUFFERS
    block_words = bm * ROW_WORDS
    region = xs_hbm.shape[0] // (N_EXPERTS * ROW_WORDS)

    def n_blocks(ex):
        return (counts_ref[ex] + (bm - 1)) // bm

    def block_rows(ref, ex, k):
        start = pl.multiple_of((ex * region + k * bm) * ROW_WORDS, block_words)
        return ref.at[pl.ds(start, block_words), :]

    def x_copy(ex, k, s):
        return pltpu.make_async_copy(block_rows(xs_hbm, ex, k), xbuf.at[s], semx.at[s])

    def y_copy(k, s):
        return pltpu.make_async_copy(ybuf.at[s], block_rows(ys_hbm, e, k), semy.at[s])

    @pl.when(e == 0)
    def _():
        for j in range(4):
            state[j] = 0

    def fetch_through(target):
        def more(c):
            pe, _, pg = c
            return (pg < target) & (pe < N_EXPERTS)

        def step(c):
            pe, pk, pg = c
            has = pk < n_blocks(pe)

            @pl.when(has)
            def _():
                x_copy(pe, pk, pg % nx).start(priority=BLOCK_DMA_PRIORITY)

            return (jnp.where(has, pe, pe + 1), jnp.where(has, pk + 1, 0), pg + has.astype(I32))

        pe, pk, pg = lax.while_loop(more, step, (state[1], state[2], state[3]))
        state[1] = pe
        state[2] = pk
        state[3] = pg

    wg_bf[...] = wg_ref[...].astype(BF16)
    wu_bf[...] = wu_ref[...].astype(BF16)
    wd_bf[...] = wd_ref[...].astype(BF16)

    nb = n_blocks(e)
    done = state[0]

    def blocks(k0, count):
        ks = [k0 + c for c in range(count)]
        gs = [done + k for k in ks]
        fetch_through(gs[0] + nx)
        for k, g in zip(ks, gs):
            x_copy(e, k, g % nx).wait()

            @pl.when(g >= ny)
            def _():
                y_copy(k, g % ny).wait()

        mids = []
        for g in gs:
            x = _unpack_rows(xbuf.at[g % nx], bm).astype(BF16)
            a = _dot(x, wg_bf[...])
            u = _dot(x, wu_bf[...])
            mids.append(((a * _sigmoid(a)) * u).astype(BF16))
        for k, g, mid in zip(ks, gs, mids):
            for w in range(ROW_WORDS):
                pair = _bf16_exact(_dot(mid, wd_bf[:, 2 * w * LANES:(2 * w + 2) * LANES]))
                _pack_rows(ybuf.at[g % ny], w, bm, pair[:, :LANES], pair[:, LANES:])
            y_copy(k, g % ny).start(priority=BLOCK_DMA_PRIORITY)

    def pair_of_blocks(p, carry):
        blocks(2 * p, 2)
        return carry

    lax.fori_loop(0, nb // 2, pair_of_blocks, 0)

    @pl.when(nb % 2 == 1)
    def _():
        blocks(nb - 1, 1)

    state[0] = done + nb

    @pl.when(e == N_EXPERTS - 1)
    def _():
        total = done + nb
        for j in range(ny):
            @pl.when(total > j)
            def _():
                y_copy(0, (total - 1 - j) % ny).wait()


def _experts(xs, counts, w_gate, w_up, w_down):
    bm = MOE_BLOCK
    w_in = pl.BlockSpec((None, D_MODEL, D_EXPERT), lambda e, cnt: (e, 0, 0))
    w_out = pl.BlockSpec((None, D_EXPERT, D_MODEL), lambda e, cnt: (e, 0, 0))
    grid_spec = pltpu.PrefetchScalarGridSpec(
        num_scalar_prefetch=1,
        grid=(N_EXPERTS,),
        in_specs=[pl.BlockSpec(memory_space=pl.ANY), w_in, w_in, w_out],
        out_specs=pl.BlockSpec(memory_space=pl.ANY),
        scratch_shapes=[pltpu.VMEM((D_MODEL, D_EXPERT), BF16),
                        pltpu.VMEM((D_MODEL, D_EXPERT), BF16),
                        pltpu.VMEM((D_EXPERT, D_MODEL), BF16),
                        pltpu.VMEM((EXPERT_X_BUFFERS, bm * ROW_WORDS, LANES), I32),
                        pltpu.VMEM((EXPERT_Y_BUFFERS, bm * ROW_WORDS, LANES), I32),
                        pltpu.SMEM((4,), I32),
                        pltpu.SemaphoreType.DMA((EXPERT_X_BUFFERS,)),
                        pltpu.SemaphoreType.DMA((EXPERT_Y_BUFFERS,))],
    )
    return pl.pallas_call(
        _expert_kernel,
        grid_spec=grid_spec,
        out_shape=jax.ShapeDtypeStruct(xs.shape, I32),
        compiler_params=pltpu.CompilerParams(
            dimension_semantics=("arbitrary",), vmem_limit_bytes=VMEM_LIMIT),
        name="experts",
    )(counts, xs, w_gate, w_up, w_down)


def _combine_kernel(meta_ref, y_hbm, route_ref, xmid_ref, gt_ref, g_ref, o_ref, ybuf, rt_scr, sem):
    step = pl.program_id(0)
    n_steps = pl.num_programs(0)
    tm = ROW_TILE
    slot = step % COMBINE_BUFFERS
    region = _region_rows(n_steps * tm)

    def fetch_tile(tile, live):
        tile = jnp.minimum(tile, n_steps - 1)
        which = tile % COMBINE_BUFFERS
        _tile_run_copies(lambda e: jnp.where(live, meta_ref[(tile * 2 + 1) * N_EXPERTS + e], 0),
                         y_hbm, lambda e: e * region + meta_ref[(tile * 2) * N_EXPERTS + e],
                         ybuf.at[which], None, sem.at[which])

    @pl.when(step == 0)
    def _():
        for ahead in range(COMBINE_BUFFERS - 1):
            fetch_tile(ahead, ahead < n_steps)

    pltpu.make_async_copy(y_hbm.at[pl.ds(0, SORT_ROWS * ROW_WORDS), :], ybuf.at[slot], sem.at[slot]).wait()
    yb = _unpack_rows(ybuf.at[slot], SORT_ROWS).astype(BF16)

    fetch_tile(step + COMBINE_BUFFERS - 1, step + COMBINE_BUFFERS - 1 < n_steps)

    rt_scr[...] = jnp.zeros_like(rt_scr)
    rt_scr[0:SUBLANES, :] = route_ref[...]
    cols = jnp.concatenate([rt_scr[:, c * LANES:(c + 1) * LANES].T for c in range(tm // LANES)], axis=0)
    key1, key2, w1, w2 = cols[:, 0:1], cols[:, 1:2], cols[:, 2:3], cols[:, 3:4]

    spos = lax.broadcasted_iota(I32, (tm, SORT_ROWS), 1).astype(F32)
    y1 = _dot(jnp.where(spos == key1, 1.0, 0.0).astype(BF16), yb)
    y2 = _dot(jnp.where(spos == key2, 1.0, 0.0).astype(BF16), yb)
    y = w1 * y1 + w2 * y2
    o_ref[...] = xmid_ref[...] + gt_ref[0] * _rmsnorm(y, g_ref[...])


def _combine(meta, ys, route, x_mid, gt_f, g_post, seq):
    t, d = x_mid.shape
    tm = ROW_TILE
    tiles_per_seq = seq // tm
    grid_spec = pltpu.PrefetchScalarGridSpec(
        num_scalar_prefetch=1,
        grid=(t // tm,),
        in_specs=[pl.BlockSpec(memory_space=pl.ANY),
                  pl.BlockSpec((SUBLANES, tm), lambda i, m: (0, i)),
                  pl.BlockSpec((tm, d), lambda i, m: (i, 0)),
                  pl.BlockSpec((1, 1, d), lambda i, m: (i // tiles_per_seq, 0, 0)),
                  pl.BlockSpec((1, d), lambda i, m: (0, 0))],
        out_specs=pl.BlockSpec((tm, d), lambda i, m: (i, 0)),
        scratch_shapes=[pltpu.VMEM((COMBINE_BUFFERS, SORT_ROWS * ROW_WORDS, LANES), I32),
                        pltpu.VMEM((LANES, tm), F32),
                        pltpu.SemaphoreType.DMA((COMBINE_BUFFERS,))],
    )
    return pl.pallas_call(
        _combine_kernel,
        grid_spec=grid_spec,
        out_shape=jax.ShapeDtypeStruct((t, d), F32),
        compiler_params=pltpu.CompilerParams(
            dimension_semantics=("arbitrary",), vmem_limit_bytes=VMEM_LIMIT),
        name="combine",
    )(meta, ys, route, x_mid, gt_f, g_post)


def kernel(x, c, w_ada, b_ada, g_pre_mix, g_post_mix, g_pre_ffn, g_post_ffn, w_in, w_pool_group, pool_scale, w_branch_pool, w_branch_att, w_out, w_group_router, b_group_router, w_expert_router, b_expert_router, w_exp_gate, w_exp_up, w_exp_down):
    b, s, d = x.shape
    assert d == D_MODEL and s % (ATT_BLOCK * 2 * ATT_DILATIONS[2]) == 0 and s % ROW_TILE == 0
    assert (b * s) % MOE_BLOCK == 0
    depth = w_ada.shape[0]
    slopes = jnp.exp2(-ALIBI_MAX_BIAS * jnp.arange(1, N_ATT_HEADS + 1, dtype=F32) / N_ATT_HEADS)
    q_scale = HEAD_DIM ** -0.5 * LOG2_E

    for layer in range(depth):
        mod = _adaln(c, w_ada[layer], b_ada[layer]).reshape(b, 6, 1, d)
        sh_m, sc_m, gt_m, sh_f, sc_f, gt_f = [mod[:, j] for j in range(6)]

        wl = w_in[layer]
        q_lo, k_lo, v_lo = POOL_WIDTH, POOL_WIDTH + 768, POOL_WIDTH + 2 * 768
        group_cols = []
        for g in range(3):
            sl = slice(g * 256, (g + 1) * 256)
            group_cols += [wl[:, q_lo:k_lo][:, sl] * q_scale, wl[:, k_lo:v_lo][:, sl],
                           wl[:, v_lo:v_lo + 768][:, sl]]
        w_perm = jnp.concatenate([wl[:, :POOL_WIDTH]] + group_cols + [wl[:, v_lo + 768:]],
                                 axis=1).astype(BF16)

        u, qkv0, qkv1, qkv2, gates = _inproj(x, g_pre_mix[layer].reshape(1, d), sc_m, sh_m, w_perm)
        att = _attention(qkv0, qkv1, qkv2, slopes)

        zero_block = jnp.zeros((POOL_GROUP_DIM, POOL_GROUP_DIM), F32)
        wpg_diag = jnp.block([[w_pool_group[layer, g] if g == h else zero_block
                               for h in range(len(POOL_WINDOWS))] for g in range(len(POOL_WINDOWS))])

        pad_rows = ROUTER_ROWS - N_EXPERTS - N_EXPERT_GROUPS
        wrt = jnp.concatenate([w_expert_router[layer].T, w_group_router[layer].T,
                               jnp.zeros((pad_rows, d), F32)], axis=0)
        wrt_hi = wrt.astype(BF16)
        wr = jnp.concatenate([wrt_hi, (wrt - wrt_hi.astype(F32)).astype(BF16)], axis=0)
        br = jnp.concatenate([b_expert_router[layer], b_group_router[layer],
                              jnp.zeros((pad_rows,), F32)]).reshape(ROUTER_ROWS, 1)

        x_mid, route, meta, counts, xs = _mixtail(
            u, att, gates, x, gt_m, sc_f, sh_f,
            g_post_mix[layer].reshape(1, d), g_pre_ffn[layer].reshape(1, d),
            wpg_diag.astype(BF16), pool_scale[layer].reshape(1, POOL_WIDTH),
            w_branch_pool[layer].astype(BF16), w_branch_att[layer].astype(BF16),
            w_out[layer].astype(BF16), wr, br)

        ys = _experts(xs, counts[0, :N_EXPERTS], w_exp_gate[layer], w_exp_up[layer], w_exp_down[layer])
        run_meta = meta[:, 0:2, 0:N_EXPERTS].reshape(-1)
        x = _combine(run_meta, ys, route, x_mid.reshape(b * s, d), gt_f,
                     g_post_ffn[layer].reshape(1, d), s).reshape(b, s, d)
    return x
```

```python
import functools

import jax
import jax.numpy as jnp
from jax import lax
from jax.experimental import pallas as pl
from jax.experimental.pallas import tpu as pltpu

F32 = jnp.float32
BF16 = jnp.bfloat16
I32 = jnp.int32
HIGHEST = lax.Precision.HIGHEST

D_MODEL = 1024
LANES = 128
SUBLANES = 8
ROW_WORDS = D_MODEL // (2 * LANES)
HI_HALF = -65536

POOL_WINDOWS = (2, 4, 8, 16)
POOL_GROUP_DIM = 128
POOL_WIDTH = 512
POOL_HALO = 16
POOL_PAD = POOL_HALO + 8

HEAD_DIM = 64
ATT_DILATIONS = (1, 4, 16)
ATT_REACH = 128
ATT_BLOCK = 128
ATT_UNROLL = 4
HEADS_PER_GROUP = 4
N_ATT_HEADS = 12
GROUP_QKV = 3 * HEADS_PER_GROUP * HEAD_DIM
ATT_OUT_WIDTH = 256
ALIBI_MAX_BIAS = 8.0
IN_WIDTH = POOL_WIDTH + 3 * GROUP_QKV + 2 * D_MODEL
MASKED = -1e30
LOG2_E = 1.4426950408889634

N_EXPERT_GROUPS = 4
EXPERTS_PER_GROUP = 8
N_EXPERTS = 32
ROUTER_ROWS = 48
D_EXPERT = 512
RMS_EPS = 1e-6

ROW_TILE = 512
ROW_SPLIT = 2
SORT_TOKENS = 256
SORT_GROUPS = ROW_TILE // SORT_TOKENS
SORT_ROWS = 2 * SORT_TOKENS
MOE_BLOCK = 256
BLOCK_DMA_PRIORITY = 1
EXPERT_X_BUFFERS = 6
EXPERT_Y_BUFFERS = 4
COMBINE_BUFFERS = 3
VMEM_LIMIT = 52 * 1024 * 1024


def _sigmoid(x):
    return 0.5 * jnp.tanh(0.5 * x) + 0.5


def _rmsnorm(x, g):
    return x * lax.rsqrt(jnp.mean(x * x, axis=-1, keepdims=True) + RMS_EPS) * g


def _dot(a, b):
    return jnp.dot(a, b, preferred_element_type=F32)


def _dot_nt(a, b, **kw):
    return lax.dot_general(a, b, (((1,), (1,)), ((), ())), preferred_element_type=F32, **kw)


def _pack_rows(ref, w, n, lo, hi):
    word = (lax.shift_right_logical(pltpu.bitcast(lo, I32), 16) | (pltpu.bitcast(hi, I32) & HI_HALF))
    ref[pl.ds(w, n, stride=ROW_WORDS), :] = word


def _unpack_rows(ref, n):
    cols = []
    for w in range(ROW_WORDS):
        word = ref[pl.ds(w, n, stride=ROW_WORDS), :]
        cols += [pltpu.bitcast(word << 16, F32), pltpu.bitcast(word & HI_HALF, F32)]
    return jnp.concatenate(cols, axis=1)


def _bf16_exact(x):
    return x.astype(BF16).astype(F32)


def _region_rows(n_tokens):
    return n_tokens + MOE_BLOCK


def _run_copies(n, src, src_row, dst, dst_row, sem):
    @pl.when(n > 0)
    def _():
        pltpu.make_async_copy(
            src.at[pl.ds(src_row * ROW_WORDS, n * ROW_WORDS), :],
            dst.at[pl.ds(dst_row * ROW_WORDS, n * ROW_WORDS), :], sem).start()


def _tile_run_copies(counts, src, src_rows, dst, dst_rows, sem):
    local = 0
    for e in range(N_EXPERTS):
        n = counts(e)
        _run_copies(n, src, local if src_rows is None else src_rows(e),
                    dst, local if dst_rows is None else dst_rows(e), sem)
        local = local + n


def _adaln_kernel(c_ref, w_ref, b_ref, o_ref):
    c = c_ref[...]
    a = c * _sigmoid(c)
    o_ref[...] = jnp.dot(a, w_ref[...], preferred_element_type=F32, precision=HIGHEST) + b_ref[...]


def _adaln(c, w_ada, b_ada):
    b, d = c.shape
    n = w_ada.shape[1]
    rows = -(-b // SUBLANES) * SUBLANES
    cp = jnp.pad(c, ((0, rows - b), (0, 0)))
    nt = 1536
    out = pl.pallas_call(
        _adaln_kernel,
        grid=(n // nt,),
        in_specs=[pl.BlockSpec((rows, d), lambda j: (0, 0)),
                  pl.BlockSpec((d, nt), lambda j: (0, j)),
                  pl.BlockSpec((1, nt), lambda j: (0, j))],
        out_specs=pl.BlockSpec((rows, nt), lambda j: (0, j)),
        out_shape=jax.ShapeDtypeStruct((rows, n), F32),
        compiler_params=pltpu.CompilerParams(vmem_limit_bytes=VMEM_LIMIT),
        name="adaln",
    )(cp, w_ada, b_ada.reshape(1, n))
    return out[:b]


def _inproj_kernel(x_ref, g_ref, sc_ref, sh_ref, w_ref,
                   u_ref, qkv0_ref, qkv1_ref, qkv2_ref, gate_ref, h_scr, p_scr):
    tm = x_ref.shape[1]
    hm = tm // ROW_SPLIT
    for part in range(ROW_SPLIT):
        rows = slice(part * hm, (part + 1) * hm)
        h = _rmsnorm(x_ref[0, rows, :], g_ref[...]) * (1.0 + sc_ref[0]) + sh_ref[0]
        h_scr[rows, :] = h.astype(BF16)
        hb = h_scr[rows, :]

        u_ref[0, rows, :] = _dot(hb, w_ref[:, 0:POOL_WIDTH]).astype(BF16)

        col = POOL_WIDTH
        qkv0_ref[0, 0, rows, :] = _dot(hb, w_ref[:, col:col + GROUP_QKV]).astype(BF16)
        for gi, (out_ref, d) in enumerate(((qkv1_ref, ATT_DILATIONS[1]), (qkv2_ref, ATT_DILATIONS[2]))):
            col += GROUP_QKV
            proj = _dot(hb, w_ref[:, col:col + GROUP_QKV])
            stage = p_scr.at[part, gi]
            for cb in range(GROUP_QKV // LANES):
                stage[cb] = proj[:, cb * LANES:(cb + 1) * LANES]
            sub = hm // d
            for r in range(d):
                out_ref[0, r, part * sub:(part + 1) * sub, :] = jnp.concatenate(
                    [stage[cb, pl.ds(r, sub, stride=d), :] for cb in range(GROUP_QKV // LANES)],
                    axis=1).astype(BF16)
        col += GROUP_QKV

        chunk = 512
        for j in range(2 * D_MODEL // chunk):
            g = _dot(hb, w_ref[:, col + j * chunk:col + (j + 1) * chunk])
            gate_ref[0, rows, j * chunk:(j + 1) * chunk] = _sigmoid(g).astype(BF16)


def _inproj(x, g_pre, sc, sh, w_perm):
    b, s, d = x.shape
    tm = ROW_TILE
    d1, d2 = ATT_DILATIONS[1], ATT_DILATIONS[2]
    grid = (b, s // tm)
    const = lambda bi, i: (0, 0)
    per_b = lambda bi, i: (bi, 0, 0)
    return pl.pallas_call(
        _inproj_kernel,
        grid=grid,
        in_specs=[pl.BlockSpec((1, tm, d), lambda bi, i: (bi, i, 0)),
                  pl.BlockSpec((1, d), const),
                  pl.BlockSpec((1, 1, d), per_b),
                  pl.BlockSpec((1, 1, d), per_b),
                  pl.BlockSpec((d, IN_WIDTH), const, pipeline_mode=pl.Buffered(1))],
        out_specs=[pl.BlockSpec((1, tm, POOL_WIDTH), lambda bi, i: (bi, i, 0)),
                   pl.BlockSpec((1, 1, tm, GROUP_QKV), lambda bi, i: (bi, 0, i, 0)),
                   pl.BlockSpec((1, d1, tm // d1, GROUP_QKV), lambda bi, i: (bi, 0, i, 0)),
                   pl.BlockSpec((1, d2, tm // d2, GROUP_QKV), lambda bi, i: (bi, 0, i, 0)),
                   pl.BlockSpec((1, tm, 2 * D_MODEL), lambda bi, i: (bi, i, 0))],
        out_shape=[jax.ShapeDtypeStruct((b, s, POOL_WIDTH), BF16),
                   jax.ShapeDtypeStruct((b, 1, s, GROUP_QKV), BF16),
                   jax.ShapeDtypeStruct((b, d1, s // d1, GROUP_QKV), BF16),
                   jax.ShapeDtypeStruct((b, d2, s // d2, GROUP_QKV), BF16),
                   jax.ShapeDtypeStruct((b, s, 2 * D_MODEL), BF16)],
        scratch_shapes=[pltpu.VMEM((tm, d), BF16),
                        pltpu.VMEM((ROW_SPLIT, 2, GROUP_QKV // LANES, tm // ROW_SPLIT, LANES), F32)],
        compiler_params=pltpu.CompilerParams(
            dimension_semantics=("arbitrary", "arbitrary"), vmem_limit_bytes=VMEM_LIMIT),
        name="inproj",
    )(x, g_pre, sc, sh, w_perm)


def _attn_kernel(slopes_ref, q0, k0, v0, q1, k1, v1, q2, k2, v2, o_ref,
                 bias_scr, acc, mst, lst, s_even, s_odd):
    pair = pl.program_id(1)
    seq = o_ref.shape[1]
    nblk = seq // ATT_BLOCK
    lane = lax.broadcasted_iota(I32, (ATT_BLOCK, LANES), 1)
    first_head = lane < HEAD_DIM
    half_lane = lax.broadcasted_iota(I32, (ATT_BLOCK // 2, LANES), 1)
    head_bits = (jnp.where(half_lane < HEAD_DIM, -1, 0), jnp.where(half_lane < HEAD_DIM, 0, -1))

    qi = lax.broadcasted_iota(I32, (ATT_BLOCK, 2 * ATT_BLOCK), 0)
    kj = lax.broadcasted_iota(I32, (ATT_BLOCK, 2 * ATT_BLOCK), 1)
    delta = ATT_BLOCK + qi - kj
    valid = (delta >= 0) & (delta <= ATT_REACH)
    delta0 = qi - kj
    valid0 = delta0 >= 0
    for g, d in enumerate(ATT_DILATIONS):
        for j in range(2):
            slope = slopes_ref[g * HEADS_PER_GROUP + 2 * pair + j]
            slope = slope * LOG2_E
            bias_scr[g, j, 0] = jnp.where(valid0, -slope * (delta0 * d).astype(F32), MASKED)
            bias_scr[g, j, 1] = jnp.where(valid, -slope * (delta * d).astype(F32), MASKED)

    def run_group(g, q_ref, k_ref, v_ref):
        d = ATT_DILATIONS[g]
        per_res = nblk // d
        n_iter = nblk // ATT_UNROLL

        def block_index(it, k):
            n = it * ATT_UNROLL + k
            return n // per_res, n % per_res

        def key_rows(i):
            lo = jnp.maximum(i - 1, 0)
            return pl.ds(pl.multiple_of(lo * ATT_BLOCK, ATT_BLOCK), 2 * ATT_BLOCK)

        def scores(it, s_ref):
            for k in range(ATT_UNROLL):
                r, i = block_index(it, k)
                q = q_ref[0, r, pl.ds(pl.multiple_of(i * ATT_BLOCK, ATT_BLOCK), ATT_BLOCK), :]
                kw = k_ref[0, r, key_rows(i), :]
                qbits = pltpu.bitcast(q, I32)
                for j in range(2):
                    qh = pltpu.bitcast(qbits & head_bits[j], BF16)
                    s_ref[2 * k + j] = _dot_nt(qh, kw) + bias_scr[g, j, jnp.minimum(i, 1)]

        def weighted_values(it, s_ref):
            for k in range(ATT_UNROLL):
                r, i = block_index(it, k)
                vw = v_ref[0, r, key_rows(i), :]
                outs = []
                for j in range(2):
                    sc = s_ref[2 * k + j]
                    mx = jnp.max(sc, axis=-1, keepdims=True)
                    p = jnp.exp2(sc - mx)
                    den = jnp.sum(p, axis=-1, keepdims=True)
                    outs.append((_dot(p.astype(BF16), vw), mx, den))
                (n0, m0, l0), (n1, m1, l1) = outs
                if d == 1:
                    rows = pl.ds(pl.multiple_of(i * ATT_BLOCK, ATT_BLOCK), ATT_BLOCK)
                else:
                    rows = pl.ds(i * (ATT_BLOCK * d) + r, ATT_BLOCK, stride=d)
                acc[g, rows, :] = jnp.where(first_head, n0, n1)
                mst[g, rows, :] = jnp.where(first_head, m0, m1)
                lst[g, rows, :] = jnp.where(first_head, l0, l1)

        scores(0, s_even)

        def body(h, carry):
            it = 2 * h
            scores(it + 1, s_odd)
            weighted_values(it, s_even)
            scores(jnp.minimum(it + 2, n_iter - 1), s_even)
            weighted_values(it + 1, s_odd)
            return carry

        lax.fori_loop(0, n_iter // 2, body, 0)

    run_group(0, q0, k0, v0)
    run_group(1, q1, k1, v1)
    run_group(2, q2, k2, v2)

    def finish(n, carry):
        rows = pl.ds(pl.multiple_of(n * ATT_BLOCK, ATT_BLOCK), ATT_BLOCK)
        ms = [mst[g, rows, :] for g in range(3)]
        top = jnp.maximum(jnp.maximum(ms[0], ms[1]), ms[2])
        scale = [jnp.exp2(m - top) for m in ms]
        num = scale[0] * acc[0, rows, :] + scale[1] * acc[1, rows, :] + scale[2] * acc[2, rows, :]
        den = scale[0] * lst[0, rows, :] + scale[1] * lst[1, rows, :] + scale[2] * lst[2, rows, :]
        o_ref[0, rows, :] = (num / den).astype(BF16)
        return carry

    lax.fori_loop(0, nblk, finish, 0)


def _attention(qkv0, qkv1, qkv2, slopes):
    b, _, s, _ = qkv0.shape
    pairs = HEADS_PER_GROUP // 2
    col_blocks = HEADS_PER_GROUP * HEAD_DIM // LANES

    def specs(arr):
        _, d, sub, _ = arr.shape
        return [pl.BlockSpec((1, d, sub, LANES),
                             functools.partial(lambda bi, p, sec: (bi, 0, 0, sec * col_blocks + p), sec=sec))
                for sec in range(3)]

    return pl.pallas_call(
        _attn_kernel,
        grid=(b, pairs),
        in_specs=[pl.BlockSpec(memory_space=pltpu.SMEM)] + specs(qkv0) + specs(qkv1) + specs(qkv2),
        out_specs=pl.BlockSpec((1, s, LANES), lambda bi, p: (bi, 0, p)),
        out_shape=jax.ShapeDtypeStruct((b, s, ATT_OUT_WIDTH), BF16),
        scratch_shapes=[pltpu.VMEM((3, 2, 2, ATT_BLOCK, 2 * ATT_BLOCK), F32),
                        pltpu.VMEM((3, s, LANES), F32),
                        pltpu.VMEM((3, s, LANES), F32),
                        pltpu.VMEM((3, s, LANES), F32),
                        pltpu.VMEM((2 * ATT_UNROLL, ATT_BLOCK, 2 * ATT_BLOCK), F32),
                        pltpu.VMEM((2 * ATT_UNROLL, ATT_BLOCK, 2 * ATT_BLOCK), F32)],
        compiler_params=pltpu.CompilerParams(
            dimension_semantics=("arbitrary", "arbitrary"), vmem_limit_bytes=VMEM_LIMIT),
        name="attention",
    )(slopes, qkv0, qkv0, qkv0, qkv1, qkv1, qkv1, qkv2, qkv2, qkv2)


def _mixtail_kernel(u_ref, halo_ref, att_ref, gate_ref, x_ref,
                    gt_m_ref, sc_f_ref, sh_f_ref, g_post_ref, g_pre_ref,
                    wpg_ref, pscale_ref, wbp_ref, wba_ref, wout_ref, wr_ref, br_ref,
                    xmid_ref, route_ref, meta_ref, counts_ref, xs_hbm,
                    pu, lv, xbuf, zbuf, fill, meta_s, sem, sem_s):
    i = pl.program_id(1)
    tm = x_ref.shape[1]
    step = pl.program_id(0) * pl.num_programs(1) + i
    last = pl.num_programs(0) * pl.num_programs(1) - 1
    slot = step % 2
    region = _region_rows(pl.num_programs(0) * pl.num_programs(1) * tm)

    @pl.when(step == 0)
    def _():
        fill[...] = jnp.zeros_like(fill)
        pu[0:POOL_PAD - POOL_HALO, :] = jnp.zeros((POOL_PAD - POOL_HALO, POOL_WIDTH), F32)
        lv[:, 0:POOL_PAD - POOL_HALO, :] = jnp.zeros((2, POOL_PAD - POOL_HALO, POOL_GROUP_DIM), F32)
        for sub in range(SORT_GROUPS):
            for e in range(N_EXPERTS):
                meta_s[SORT_GROUPS + sub, 1, e] = 0
                meta_s[SORT_GROUPS + sub, 0, e] = 0

    def send_tile(which):
        for sub in range(SORT_GROUPS):
            m = which * SORT_GROUPS + sub
            _tile_run_copies(lambda e: meta_s[m, 1, e], xbuf.at[which, sub], None,
                             xs_hbm, lambda e: e * region + meta_s[m, 0, e], sem.at[which])

    def wait_tile(which):
        for sub in range(SORT_GROUPS):
            pltpu.make_async_copy(xbuf.at[which, sub], xs_hbm.at[pl.ds(0, SORT_ROWS * ROW_WORDS), :],
                                  sem.at[which]).wait()

    @pl.when(step > 1)
    def _():
        wait_tile(slot)

    send_tile(1 - slot)

    head = POOL_PAD - POOL_HALO
    halo = halo_ref[0].astype(F32)
    pu[head:POOL_PAD, :] = jnp.where(i > 0, halo, jnp.zeros_like(halo))
    pu[POOL_PAD:POOL_PAD + tm, :] = u_ref[0].astype(F32)
    t = i * tm + lax.broadcasted_iota(I32, (tm, 1), 0)
    pooled_groups = []
    for g, w in enumerate(POOL_WINDOWS):
        cols = slice(g * POOL_GROUP_DIM, (g + 1) * POOL_GROUP_DIM)
        read = lambda start, n: pu[pl.ds(start, n), cols]
        shift, level = 1, 0
        while 2 * shift < w:
            partial = read(head, tm + POOL_HALO) + read(head - shift, tm + POOL_HALO)
            buf = lv.at[level % 2]
            buf[pl.ds(head, tm + POOL_HALO), :] = partial
            read = lambda start, n, buf=buf: buf[pl.ds(start, n), :]
            shift, level = 2 * shift, level + 1
        win = read(POOL_PAD, tm) + read(POOL_PAD - shift, tm)
        count = jnp.minimum(t + 1, w).astype(F32)
        pooled_groups.append((win / count - pu[pl.ds(POOL_PAD, tm), cols]).astype(BF16))
    pooled_all = jnp.concatenate(pooled_groups, axis=1)

    ts = SORT_TOKENS
    neg_inf = -jnp.inf
    far = float(LANES)
    before = (lax.broadcasted_iota(I32, (ts, ts), 0) < lax.broadcasted_iota(I32, (ts, ts), 1))
    earlier = jnp.where(before, 1.0, 0.0).astype(BF16)
    erow = lax.broadcasted_iota(I32, (N_EXPERTS, ts), 0).astype(F32)
    e_col = lax.broadcasted_iota(I32, (N_EXPERTS, LANES), 0)
    e_lane = lax.broadcasted_iota(I32, (N_EXPERTS, LANES), 1)
    row8 = lax.broadcasted_iota(I32, (SUBLANES, ts), 0)
    mrow = lax.broadcasted_iota(I32, (SUBLANES, LANES), 0)
    srow = lax.broadcasted_iota(I32, (SORT_ROWS, ts), 0).astype(F32)
    filled = fill[...]

    def rows_of(sub):
        return slice(sub * ts, (sub + 1) * ts)

    def branch_projections(sub, st):
        rows = rows_of(sub)
        mixed = _dot(pooled_all[rows], wpg_ref[...]) * pscale_ref[...]
        st["y_pool"] = _dot(mixed.astype(BF16), wbp_ref[...])
        st["y_att"] = _dot(att_ref[0, rows, :], wba_ref[...])

    def gated_sum(sub, st):
        rows = rows_of(sub)
        st["merged"] = (gate_ref[0, rows, 0:D_MODEL] * st.pop("y_pool").astype(BF16)
                        + gate_ref[0, rows, D_MODEL:2 * D_MODEL] * st.pop("y_att").astype(BF16))

    def output_projection(sub, st):
        st["y"] = _dot(st.pop("merged"), wout_ref[...])

    def residual_and_ffn_input(sub, st):
        rows = rows_of(sub)
        x_mid = x_ref[0, rows, :] + gt_m_ref[0] * _rmsnorm(st.pop("y"), g_post_ref[...])
        xmid_ref[0, rows, :] = x_mid
        h2 = _rmsnorm(x_mid, g_pre_ref[...]) * (1.0 + sc_f_ref[0]) + sh_f_ref[0]
        st["h2b"] = h2.astype(BF16)
        st["h2lo"] = (h2 - st["h2b"].astype(F32)).astype(BF16)

    def router_logits(sub, st):
        by_hi = _dot_nt(wr_ref[...], st["h2b"])
        st["logits"] = (by_hi[0:ROUTER_ROWS] + (by_hi[ROUTER_ROWS:2 * ROUTER_ROWS]
                        + _dot_nt(wr_ref[0:ROUTER_ROWS, :], st.pop("h2lo"))) + br_ref[...])

    def route(sub, st):
        logits = st.pop("logits")
        gl = logits[N_EXPERTS:N_EXPERTS + N_EXPERT_GROUPS, :]
        grow = lax.broadcasted_iota(I32, gl.shape, 0).astype(F32)
        gmax = jnp.max(gl, axis=0, keepdims=True)
        gsel = jnp.min(jnp.where(gl == gmax, grow, far), axis=0, keepdims=True)
        p_group = 1.0 / jnp.sum(jnp.exp(gl - gmax), axis=0, keepdims=True)
        e_lo = gsel * float(EXPERTS_PER_GROUP)
        el = jnp.where((erow >= e_lo) & (erow < e_lo + float(EXPERTS_PER_GROUP)),
                       logits[0:N_EXPERTS, :], neg_inf)
        v1 = jnp.max(el, axis=0, keepdims=True)
        i1 = jnp.min(jnp.where(el == v1, erow, far), axis=0, keepdims=True)
        el2 = jnp.where(erow == i1, neg_inf, el)
        v2 = jnp.max(el2, axis=0, keepdims=True)
        i2 = jnp.min(jnp.where(el2 == v2, erow, far), axis=0, keepdims=True)
        e21 = jnp.exp(v2 - v1)
        st["w1"] = p_group / (1.0 + e21)
        st["w2"] = p_group * e21 / (1.0 + e21)
        st["pick1"] = erow == i1
        st["pick2"] = erow == i2
        st["assign"] = jnp.where(st["pick1"] | st["pick2"], 1.0, 0.0)

    def count_and_rank(sub, st):
        assign = st.pop("assign")
        st["rank"] = _dot(assign.astype(BF16), earlier)
        assign_pad = jnp.concatenate([assign, jnp.zeros((LANES - N_EXPERTS, ts), F32)], axis=0).astype(BF16)
        st["cnt_row"] = _dot_nt(jnp.ones((SUBLANES, ts), BF16), assign_pad)

    def sorted_positions(sub, st):
        cnt_row = st["cnt_row"]
        run_start = jnp.sum(jnp.where(e_lane < e_col, cnt_row[0:1, :], 0.0), axis=1, keepdims=True)
        pos = st.pop("rank") + run_start
        key1 = jnp.sum(jnp.where(st.pop("pick1"), pos, 0.0), axis=0, keepdims=True)
        key2 = jnp.sum(jnp.where(st.pop("pick2"), pos, 0.0), axis=0, keepdims=True)
        route_ref[:, rows_of(sub)] = jnp.where(row8 == 0, key1, jnp.where(row8 == 1, key2,
                                               jnp.where(row8 == 2, st.pop("w1"),
                                                         jnp.where(row8 == 3, st.pop("w2"), 0.0))))
        st["perm"] = jnp.where((srow == key1) | (srow == key2), 1.0, 0.0).astype(BF16)

    def sort_rows(sub, st):
        perm, h2b = st.pop("perm"), st.pop("h2b")
        for w in range(ROW_WORDS):
            pair = _dot(perm, h2b[:, 2 * w * LANES:(2 * w + 2) * LANES])
            _pack_rows(xbuf.at[slot, sub], w, SORT_ROWS, pair[:, :LANES], pair[:, LANES:])

    state = [{} for _ in range(SORT_GROUPS)]
    for stage in (branch_projections, gated_sum, output_projection, residual_and_ffn_input,
                  router_logits, route, count_and_rank, sorted_positions, sort_rows):
        for sub in range(SORT_GROUPS):
            stage(sub, state[sub])

    for sub in range(SORT_GROUPS):
        cnt_row = state[sub]["cnt_row"]
        meta_ref[sub] = jnp.where(mrow == 0, filled, jnp.where(mrow == 1, cnt_row, 0.0)).astype(I32)
        filled = filled + cnt_row

    fill[...] = filled
    counts_ref[...] = filled.astype(I32)

    meta_copy = pltpu.make_async_copy(meta_ref, meta_s.at[pl.ds(slot * SORT_GROUPS, SORT_GROUPS)], sem_s)
    meta_copy.start()
    meta_copy.wait()

    @pl.when(step == last)
    def _():
        send_tile(slot)

        @pl.when(step > 0)
        def _():
            wait_tile(1 - slot)

        wait_tile(slot)
        zbuf[...] = jnp.zeros_like(zbuf)
        final = slot * SORT_GROUPS + SORT_GROUPS - 1

        def pad_copy(e):
            end = e * region + meta_s[final, 0, e] + meta_s[final, 1, e]
            return pltpu.make_async_copy(
                zbuf, xs_hbm.at[pl.ds(end * ROW_WORDS, MOE_BLOCK * ROW_WORDS), :], sem_s)

        def start_pad(e, carry):
            pad_copy(e).start()
            return carry

        def wait_pad(e, carry):
            pad_copy(e).wait()
            return carry

        lax.fori_loop(0, N_EXPERTS, start_pad, 0)
        lax.fori_loop(0, N_EXPERTS, wait_pad, 0)


def _mixtail(u, att, gates, x, gt_m, sc_f, sh_f, g_post, g_pre,
             wpg, pscale, wbp, wba, wout, wr, br):
    b, s, d = x.shape
    tm = ROW_TILE
    tiles = s // tm
    n_tiles = b * tiles
    halo_blocks = tm // POOL_HALO
    region = _region_rows(b * s)
    const2 = lambda bi, i: (0, 0)
    const3 = lambda bi, i: (0, 0, 0)
    per_b = lambda bi, i: (bi, 0, 0)
    tile = lambda bi, i: (bi, i, 0)
    single = dict(pipeline_mode=pl.Buffered(1))
    return pl.pallas_call(
        _mixtail_kernel,
        grid=(b, tiles),
        in_specs=[pl.BlockSpec((1, tm, POOL_WIDTH), tile),
                  pl.BlockSpec((1, POOL_HALO, POOL_WIDTH),
                               lambda bi, i: (bi, jnp.maximum(i * halo_blocks - 1, 0), 0)),
                  pl.BlockSpec((1, tm, ATT_OUT_WIDTH), tile),
                  pl.BlockSpec((1, tm, 2 * D_MODEL), tile),
                  pl.BlockSpec((1, tm, d), tile),
                  pl.BlockSpec((1, 1, d), per_b),
                  pl.BlockSpec((1, 1, d), per_b),
                  pl.BlockSpec((1, 1, d), per_b),
                  pl.BlockSpec((1, d), const2),
                  pl.BlockSpec((1, d), const2),
                  pl.BlockSpec(wpg.shape, const2, **single),
                  pl.BlockSpec((1, POOL_WIDTH), const2),
                  pl.BlockSpec(wbp.shape, const2, **single),
                  pl.BlockSpec(wba.shape, const2, **single),
                  pl.BlockSpec(wout.shape, const2, **single),
                  pl.BlockSpec(wr.shape, const2, **single),
                  pl.BlockSpec(br.shape, const2)],
        out_specs=[pl.BlockSpec((1, tm, d), tile),
                   pl.BlockSpec((SUBLANES, tm), lambda bi, i: (0, bi * tiles + i)),
                   pl.BlockSpec((SORT_GROUPS, SUBLANES, LANES), lambda bi, i: (bi * tiles + i, 0, 0)),
                   pl.BlockSpec((SUBLANES, LANES), const2),
                   pl.BlockSpec(memory_space=pl.ANY)],
        out_shape=[jax.ShapeDtypeStruct((b, s, d), F32),
                   jax.ShapeDtypeStruct((SUBLANES, b * s), F32),
                   jax.ShapeDtypeStruct((n_tiles * SORT_GROUPS, SUBLANES, LANES), I32),
                   jax.ShapeDtypeStruct((SUBLANES, LANES), I32),
                   jax.ShapeDtypeStruct((N_EXPERTS * region * ROW_WORDS, LANES), I32)],
        scratch_shapes=[pltpu.VMEM((POOL_PAD + tm, POOL_WIDTH), F32),
                        pltpu.VMEM((2, POOL_PAD + tm, POOL_GROUP_DIM), F32),
                        pltpu.VMEM((2, SORT_GROUPS, SORT_ROWS * ROW_WORDS, LANES), I32),
                        pltpu.VMEM((MOE_BLOCK * ROW_WORDS, LANES), I32),
                        pltpu.VMEM((SUBLANES, LANES), F32),
                        pltpu.SMEM((2 * SORT_GROUPS, SUBLANES, LANES), I32),
                        pltpu.SemaphoreType.DMA((2,)),
                        pltpu.SemaphoreType.DMA(())],
        compiler_params=pltpu.CompilerParams(
            dimension_semantics=("arbitrary", "arbitrary"), vmem_limit_bytes=VMEM_LIMIT),
        name="mixtail",
    )(u, u, att, gates, x, gt_m, sc_f, sh_f, g_post, g_pre,
      wpg, pscale, wbp, wba, wout, wr, br)


def _expert_kernel(counts_ref, xs_hbm, wg_ref, wu_ref, wd_ref, ys_hbm,
                   wg_bf, wu_bf, wd_bf, xbuf, ybuf, state, semx, semy):
    e = pl.program_id(0)
    bm = MOE_BLOCK
    nx, ny = EXPERT_X_BUFFERS, EXPERT_Y_BUFFERS
    block_words = bm * ROW_WORDS
    region = xs_hbm.shape[0] // (N_EXPERTS * ROW_WORDS)

    def n_blocks(ex):
        return (counts_ref[ex] + (bm - 1)) // bm

    def block_rows(ref, ex, k):
        start = pl.multiple_of((ex * region + k * bm) * ROW_WORDS, block_words)
        return ref.at[pl.ds(start, block_words), :]

    def x_copy(ex, k, s):
        return pltpu.make_async_copy(block_rows(xs_hbm, ex, k), xbuf.at[s], semx.at[s])

    def y_copy(k, s):
        return pltpu.make_async_copy(ybuf.at[s], block_rows(ys_hbm, e, k), semy.at[s])

    @pl.when(e == 0)
    def _():
        for j in range(4):
            state[j] = 0

    def fetch_through(target):
        def more(c):
            pe, _, pg = c
            return (pg < target) & (pe < N_EXPERTS)

        def step(c):
            pe, pk, pg = c
            has = pk < n_blocks(pe)

            @pl.when(has)
            def _():
                x_copy(pe, pk, pg % nx).start(priority=BLOCK_DMA_PRIORITY)

            return (jnp.where(has, pe, pe + 1), jnp.where(has, pk + 1, 0), pg + has.astype(I32))

        pe, pk, pg = lax.while_loop(more, step, (state[1], state[2], state[3]))
        state[1] = pe
        state[2] = pk
        state[3] = pg

    wg_bf[...] = wg_ref[...].astype(BF16)
    wu_bf[...] = wu_ref[...].astype(BF16)
    wd_bf[...] = wd_ref[...].astype(BF16)

    nb = n_blocks(e)
    done = state[0]

    def blocks(k0, count):
        ks = [k0 + c for c in range(count)]
        gs = [done + k for k in ks]
        fetch_through(gs[0] + nx)
        for k, g in zip(ks, gs):
            x_copy(e, k, g % nx).wait()

            @pl.when(g >= ny)
            def _():
                y_copy(k, g % ny).wait()

        mids = []
        for g in gs:
            x = _unpack_rows(xbuf.at[g % nx], bm).astype(BF16)
            a = _dot(x, wg_bf[...])
            u = _dot(x, wu_bf[...])
            mids.append(((a * _sigmoid(a)) * u).astype(BF16))
        for k, g, mid in zip(ks, gs, mids):
            for w in range(ROW_WORDS):
                pair = _bf16_exact(_dot(mid, wd_bf[:, 2 * w * LANES:(2 * w + 2) * LANES]))
                _pack_rows(ybuf.at[g % ny], w, bm, pair[:, :LANES], pair[:, LANES:])
            y_copy(k, g % ny).start(priority=BLOCK_DMA_PRIORITY)

    def pair_of_blocks(p, carry):
        blocks(2 * p, 2)
        return carry

    lax.fori_loop(0, nb // 2, pair_of_blocks, 0)

    @pl.when(nb % 2 == 1)
    def _():
        blocks(nb - 1, 1)

    state[0] = done + nb

    @pl.when(e == N_EXPERTS - 1)
    def _():
        total = done + nb
        for j in range(ny):
            @pl.when(total > j)
            def _():
                y_copy(0, (total - 1 - j) % ny).wait()


def _experts(xs, counts, w_gate, w_up, w_down):
    bm = MOE_BLOCK
    w_in = pl.BlockSpec((None, D_MODEL, D_EXPERT), lambda e, cnt: (e, 0, 0))
    w_out = pl.BlockSpec((None, D_EXPERT, D_MODEL), lambda e, cnt: (e, 0, 0))
    grid_spec = pltpu.PrefetchScalarGridSpec(
        num_scalar_prefetch=1,
        grid=(N_EXPERTS,),
        in_specs=[pl.BlockSpec(memory_space=pl.ANY), w_in, w_in, w_out],
        out_specs=pl.BlockSpec(memory_space=pl.ANY),
        scratch_shapes=[pltpu.VMEM((D_MODEL, D_EXPERT), BF16),
                        pltpu.VMEM((D_MODEL, D_EXPERT), BF16),
                        pltpu.VMEM((D_EXPERT, D_MODEL), BF16),
                        pltpu.VMEM((EXPERT_X_BUFFERS, bm * ROW_WORDS, LANES), I32),
                        pltpu.VMEM((EXPERT_Y_BUFFERS, bm * ROW_WORDS, LANES), I32),
                        pltpu.SMEM((4,), I32),
                        pltpu.SemaphoreType.DMA((EXPERT_X_BUFFERS,)),
                        pltpu.SemaphoreType.DMA((EXPERT_Y_BUFFERS,))],
    )
    return pl.pallas_call(
        _expert_kernel,
        grid_spec=grid_spec,
        out_shape=jax.ShapeDtypeStruct(xs.shape, I32),
        compiler_params=pltpu.CompilerParams(
            dimension_semantics=("arbitrary",), vmem_limit_bytes=VMEM_LIMIT),
        name="experts",
    )(counts, xs, w_gate, w_up, w_down)


def _combine_kernel(meta_ref, y_hbm, route_ref, xmid_ref, gt_ref, g_ref, o_ref, ybuf, rt_scr, sem):
    step = pl.program_id(0)
    n_steps = pl.num_programs(0)
    tm = ROW_TILE
    slot = step % COMBINE_BUFFERS
    region = _region_rows(n_steps * tm)

    def fetch_tile(tile, live):
        tile = jnp.minimum(tile, n_steps - 1)
        which = tile % COMBINE_BUFFERS
        for sub in range(SORT_GROUPS):
            m = (tile * SORT_GROUPS + sub) * 2
            _tile_run_copies(lambda e: jnp.where(live, meta_ref[(m + 1) * N_EXPERTS + e], 0),
                             y_hbm, lambda e: e * region + meta_ref[m * N_EXPERTS + e],
                             ybuf.at[which, sub], None, sem.at[which])

    @pl.when(step == 0)
    def _():
        for ahead in range(COMBINE_BUFFERS - 1):
            fetch_tile(ahead, ahead < n_steps)

    for sub in range(SORT_GROUPS):
        pltpu.make_async_copy(y_hbm.at[pl.ds(0, SORT_ROWS * ROW_WORDS), :], ybuf.at[slot, sub],
                              sem.at[slot]).wait()
    sorted_rows = [_unpack_rows(ybuf.at[slot, sub], SORT_ROWS).astype(BF16) for sub in range(SORT_GROUPS)]

    fetch_tile(step + COMBINE_BUFFERS - 1, step + COMBINE_BUFFERS - 1 < n_steps)

    rt_scr[...] = jnp.zeros_like(rt_scr)
    rt_scr[0:SUBLANES, :] = route_ref[...]
    cols = jnp.concatenate([rt_scr[:, c * LANES:(c + 1) * LANES].T for c in range(tm // LANES)], axis=0)

    ts = SORT_TOKENS
    spos = lax.broadcasted_iota(I32, (ts, SORT_ROWS), 1).astype(F32)
    def rows_of(sub):
        return slice(sub * ts, (sub + 1) * ts)

    def one_hots(sub, st):
        key1, key2 = (cols[rows_of(sub), c:c + 1] for c in range(2))
        st["pick1"] = jnp.where(spos == key1, 1.0, 0.0).astype(BF16)
        st["pick2"] = jnp.where(spos == key2, 1.0, 0.0).astype(BF16)

    def unsort(sub, st):
        st["y1"] = _dot(st.pop("pick1"), sorted_rows[sub])
        st["y2"] = _dot(st.pop("pick2"), sorted_rows[sub])

    def weigh_and_add(sub, st):
        rows = rows_of(sub)
        w1, w2 = (cols[rows, c:c + 1] for c in range(2, 4))
        y = w1 * st.pop("y1") + w2 * st.pop("y2")
        o_ref[rows, :] = xmid_ref[rows, :] + gt_ref[0] * _rmsnorm(y, g_ref[...])

    state = [{} for _ in range(SORT_GROUPS)]
    for stage in (one_hots, unsort, weigh_and_add):
        for sub in range(SORT_GROUPS):
            stage(sub, state[sub])


def _combine(meta, ys, route, x_mid, gt_f, g_post, seq):
    t, d = x_mid.shape
    tm = ROW_TILE
    tiles_per_seq = seq // tm
    grid_spec = pltpu.PrefetchScalarGridSpec(
        num_scalar_prefetch=1,
        grid=(t // tm,),
        in_specs=[pl.BlockSpec(memory_space=pl.ANY),
                  pl.BlockSpec((SUBLANES, tm), lambda i, m: (0, i)),
                  pl.BlockSpec((tm, d), lambda i, m: (i, 0)),
                  pl.BlockSpec((1, 1, d), lambda i, m: (i // tiles_per_seq, 0, 0)),
                  pl.BlockSpec((1, d), lambda i, m: (0, 0))],
        out_specs=pl.BlockSpec((tm, d), lambda i, m: (i, 0)),
        scratch_shapes=[pltpu.VMEM((COMBINE_BUFFERS, SORT_GROUPS, SORT_ROWS * ROW_WORDS, LANES), I32),
                        pltpu.VMEM((LANES, tm), F32),
                        pltpu.SemaphoreType.DMA((COMBINE_BUFFERS,))],
    )
    return pl.pallas_call(
        _combine_kernel,
        grid_spec=grid_spec,
        out_shape=jax.ShapeDtypeStruct((t, d), F32),
        compiler_params=pltpu.CompilerParams(
            dimension_semantics=("arbitrary",), vmem_limit_bytes=VMEM_LIMIT),
        name="combine",
    )(meta, ys, route, x_mid, gt_f, g_post)


def kernel(x, c, w_ada, b_ada, g_pre_mix, g_post_mix, g_pre_ffn, g_post_ffn, w_in, w_pool_group, pool_scale, w_branch_pool, w_branch_att, w_out, w_group_router, b_group_router, w_expert_router, b_expert_router, w_exp_gate, w_exp_up, w_exp_down):
    b, s, d = x.shape
    assert d == D_MODEL and s % (ATT_BLOCK * 2 * ATT_DILATIONS[2]) == 0 and s % ROW_TILE == 0
    assert (b * s) % MOE_BLOCK == 0
    depth = w_ada.shape[0]
    slopes = jnp.exp2(-ALIBI_MAX_BIAS * jnp.arange(1, N_ATT_HEADS + 1, dtype=F32) / N_ATT_HEADS)
    q_scale = HEAD_DIM ** -0.5 * LOG2_E

    for layer in range(depth):
        mod = _adaln(c, w_ada[layer], b_ada[layer]).reshape(b, 6, 1, d)
        sh_m, sc_m, gt_m, sh_f, sc_f, gt_f = [mod[:, j] for j in range(6)]

        wl = w_in[layer]
        q_lo, k_lo, v_lo = POOL_WIDTH, POOL_WIDTH + 768, POOL_WIDTH + 2 * 768
        group_cols = []
        for g in range(3):
            sl = slice(g * 256, (g + 1) * 256)
            group_cols += [wl[:, q_lo:k_lo][:, sl] * q_scale, wl[:, k_lo:v_lo][:, sl],
                           wl[:, v_lo:v_lo + 768][:, sl]]
        w_perm = jnp.concatenate([wl[:, :POOL_WIDTH]] + group_cols + [wl[:, v_lo + 768:]],
                                 axis=1).astype(BF16)

        u, qkv0, qkv1, qkv2, gates = _inproj(x, g_pre_mix[layer].reshape(1, d), sc_m, sh_m, w_perm)
        att = _attention(qkv0, qkv1, qkv2, slopes)

        zero_block = jnp.zeros((POOL_GROUP_DIM, POOL_GROUP_DIM), F32)
        wpg_diag = jnp.block([[w_pool_group[layer, g] if g == h else zero_block
                               for h in range(len(POOL_WINDOWS))] for g in range(len(POOL_WINDOWS))])

        pad_rows = ROUTER_ROWS - N_EXPERTS - N_EXPERT_GROUPS
        wrt = jnp.concatenate([w_expert_router[layer].T, w_group_router[layer].T,
                               jnp.zeros((pad_rows, d), F32)], axis=0)
        wrt_hi = wrt.astype(BF16)
        wr = jnp.concatenate([wrt_hi, (wrt - wrt_hi.astype(F32)).astype(BF16)], axis=0)
        br = jnp.concatenate([b_expert_router[layer], b_group_router[layer],
                              jnp.zeros((pad_rows,), F32)]).reshape(ROUTER_ROWS, 1)

        x_mid, route, meta, counts, xs = _mixtail(
            u, att, gates, x, gt_m, sc_f, sh_f,
            g_post_mix[layer].reshape(1, d), g_pre_ffn[layer].reshape(1, d),
            wpg_diag.astype(BF16), pool_scale[layer].reshape(1, POOL_WIDTH),
            w_branch_pool[layer].astype(BF16), w_branch_att[layer].astype(BF16),
            w_out[layer].astype(BF16), wr, br)

        ys = _experts(xs, counts[0, :N_EXPERTS], w_exp_gate[layer], w_exp_up[layer], w_exp_down[layer])
        run_meta = meta[:, 0:2, 0:N_EXPERTS].reshape(-1)
        x = _combine(run_meta, ys, route, x_mid.reshape(b * s, d), gt_f,
                     g_post_ffn[layer].reshape(1, d), s).reshape(b, s, d)
    return x
```

```python
import functools

import jax
import jax.numpy as jnp
from jax import lax
from jax.experimental import pallas as pl
from jax.experimental.pallas import tpu as pltpu

F32 = jnp.float32
BF16 = jnp.bfloat16
I32 = jnp.int32
HIGHEST = lax.Precision.HIGHEST

D_MODEL = 1024
LANES = 128
SUBLANES = 8
ROW_WORDS = D_MODEL // (2 * LANES)
HI_HALF = -65536

POOL_WINDOWS = (2, 4, 8, 16)
POOL_GROUP_DIM = 128
POOL_WIDTH = 512
POOL_HALO = 16
POOL_PAD = POOL_HALO + 8

HEAD_DIM = 64
ATT_DILATIONS = (1, 4, 16)
ATT_REACH = 128
ATT_BLOCK = 128
ATT_UNROLL = 4
HEADS_PER_GROUP = 4
N_ATT_HEADS = 12
GROUP_QKV = 3 * HEADS_PER_GROUP * HEAD_DIM
ATT_OUT_WIDTH = 256
ALIBI_MAX_BIAS = 8.0
IN_WIDTH = POOL_WIDTH + 3 * GROUP_QKV + 2 * D_MODEL
MASKED = -1e30
LOG2_E = 1.4426950408889634

N_EXPERT_GROUPS = 4
EXPERTS_PER_GROUP = 8
N_EXPERTS = 32
ROUTER_ROWS = 48
D_EXPERT = 512
RMS_EPS = 1e-6

ROW_TILE = 512
ROW_SPLIT = 2
SORT_TOKENS = 256
SORT_GROUPS = ROW_TILE // SORT_TOKENS
SORT_ROWS = 2 * SORT_TOKENS
MOE_BLOCK = 256
BLOCK_DMA_PRIORITY = 1
EXPERT_X_BUFFERS = 6
EXPERT_Y_BUFFERS = 4
COMBINE_BUFFERS = 3
VMEM_LIMIT = 52 * 1024 * 1024


def _sigmoid(x):
    return 0.5 * jnp.tanh(0.5 * x) + 0.5


def _rmsnorm(x, g):
    return x * lax.rsqrt(jnp.mean(x * x, axis=-1, keepdims=True) + RMS_EPS) * g


def _dot(a, b):
    return jnp.dot(a, b, preferred_element_type=F32)


def _dot_nt(a, b, **kw):
    return lax.dot_general(a, b, (((1,), (1,)), ((), ())), preferred_element_type=F32, **kw)


def _pack_rows(ref, w, n, lo, hi):
    word = (lax.shift_right_logical(pltpu.bitcast(lo, I32), 16) | (pltpu.bitcast(hi, I32) & HI_HALF))
    ref[pl.ds(w, n, stride=ROW_WORDS), :] = word


def _unpack_rows(ref, n):
    cols = []
    for w in range(ROW_WORDS):
        word = ref[pl.ds(w, n, stride=ROW_WORDS), :]
        cols += [pltpu.bitcast(word << 16, F32), pltpu.bitcast(word & HI_HALF, F32)]
    return jnp.concatenate(cols, axis=1)


def _bf16_exact(x):
    return x.astype(BF16).astype(F32)


def _run_staggered(stages, n_chains, lag):
    state = [{} for _ in range(n_chains)]
    for tick in range(len(stages) + lag * (n_chains - 1)):
        for chain in range(n_chains):
            if 0 <= tick - lag * chain < len(stages):
                stages[tick - lag * chain](chain, state[chain])
    return state


def _region_rows(n_tokens):
    return n_tokens + MOE_BLOCK


def _run_copies(n, src, src_row, dst, dst_row, sem):
    @pl.when(n > 0)
    def _():
        pltpu.make_async_copy(
            src.at[pl.ds(src_row * ROW_WORDS, n * ROW_WORDS), :],
            dst.at[pl.ds(dst_row * ROW_WORDS, n * ROW_WORDS), :], sem).start()


def _tile_run_copies(counts, src, src_rows, dst, dst_rows, sem):
    local = 0
    for e in range(N_EXPERTS):
        n = counts(e)
        _run_copies(n, src, local if src_rows is None else src_rows(e),
                    dst, local if dst_rows is None else dst_rows(e), sem)
        local = local + n


def _adaln_kernel(c_ref, w_ref, b_ref, o_ref):
    c = c_ref[...]
    a = c * _sigmoid(c)
    o_ref[...] = jnp.dot(a, w_ref[...], preferred_element_type=F32, precision=HIGHEST) + b_ref[...]


def _adaln(c, w_ada, b_ada):
    b, d = c.shape
    n = w_ada.shape[1]
    rows = -(-b // SUBLANES) * SUBLANES
    cp = jnp.pad(c, ((0, rows - b), (0, 0)))
    nt = 1536
    out = pl.pallas_call(
        _adaln_kernel,
        grid=(n // nt,),
        in_specs=[pl.BlockSpec((rows, d), lambda j: (0, 0)),
                  pl.BlockSpec((d, nt), lambda j: (0, j)),
                  pl.BlockSpec((1, nt), lambda j: (0, j))],
        out_specs=pl.BlockSpec((rows, nt), lambda j: (0, j)),
        out_shape=jax.ShapeDtypeStruct((rows, n), F32),
        compiler_params=pltpu.CompilerParams(vmem_limit_bytes=VMEM_LIMIT),
        name="adaln",
    )(cp, w_ada, b_ada.reshape(1, n))
    return out[:b]


def _inproj_kernel(x_ref, g_ref, sc_ref, sh_ref, w_ref,
                   u_ref, qkv0_ref, qkv1_ref, qkv2_ref, gate_ref, h_scr, p_scr):
    tm = x_ref.shape[1]
    hm = tm // ROW_SPLIT
    for part in range(ROW_SPLIT):
        rows = slice(part * hm, (part + 1) * hm)
        h = _rmsnorm(x_ref[0, rows, :], g_ref[...]) * (1.0 + sc_ref[0]) + sh_ref[0]
        h_scr[rows, :] = h.astype(BF16)
        hb = h_scr[rows, :]

        u_ref[0, rows, :] = _dot(hb, w_ref[:, 0:POOL_WIDTH]).astype(BF16)

        col = POOL_WIDTH
        qkv0_ref[0, 0, rows, :] = _dot(hb, w_ref[:, col:col + GROUP_QKV]).astype(BF16)
        for gi, (out_ref, d) in enumerate(((qkv1_ref, ATT_DILATIONS[1]), (qkv2_ref, ATT_DILATIONS[2]))):
            col += GROUP_QKV
            proj = _dot(hb, w_ref[:, col:col + GROUP_QKV])
            stage = p_scr.at[part, gi]
            for cb in range(GROUP_QKV // LANES):
                stage[cb] = proj[:, cb * LANES:(cb + 1) * LANES]
            sub = hm // d
            for r in range(d):
                out_ref[0, r, part * sub:(part + 1) * sub, :] = jnp.concatenate(
                    [stage[cb, pl.ds(r, sub, stride=d), :] for cb in range(GROUP_QKV // LANES)],
                    axis=1).astype(BF16)
        col += GROUP_QKV

        chunk = 512
        for j in range(2 * D_MODEL // chunk):
            g = _dot(hb, w_ref[:, col + j * chunk:col + (j + 1) * chunk])
            gate_ref[0, rows, j * chunk:(j + 1) * chunk] = _sigmoid(g).astype(BF16)


def _inproj(x, g_pre, sc, sh, w_perm):
    b, s, d = x.shape
    tm = ROW_TILE
    d1, d2 = ATT_DILATIONS[1], ATT_DILATIONS[2]
    grid = (b, s // tm)
    const = lambda bi, i: (0, 0)
    per_b = lambda bi, i: (bi, 0, 0)
    return pl.pallas_call(
        _inproj_kernel,
        grid=grid,
        in_specs=[pl.BlockSpec((1, tm, d), lambda bi, i: (bi, i, 0)),
                  pl.BlockSpec((1, d), const),
                  pl.BlockSpec((1, 1, d), per_b),
                  pl.BlockSpec((1, 1, d), per_b),
                  pl.BlockSpec((d, IN_WIDTH), const, pipeline_mode=pl.Buffered(1))],
        out_specs=[pl.BlockSpec((1, tm, POOL_WIDTH), lambda bi, i: (bi, i, 0)),
                   pl.BlockSpec((1, 1, tm, GROUP_QKV), lambda bi, i: (bi, 0, i, 0)),
                   pl.BlockSpec((1, d1, tm // d1, GROUP_QKV), lambda bi, i: (bi, 0, i, 0)),
                   pl.BlockSpec((1, d2, tm // d2, GROUP_QKV), lambda bi, i: (bi, 0, i, 0)),
                   pl.BlockSpec((1, tm, 2 * D_MODEL), lambda bi, i: (bi, i, 0))],
        out_shape=[jax.ShapeDtypeStruct((b, s, POOL_WIDTH), BF16),
                   jax.ShapeDtypeStruct((b, 1, s, GROUP_QKV), BF16),
                   jax.ShapeDtypeStruct((b, d1, s // d1, GROUP_QKV), BF16),
                   jax.ShapeDtypeStruct((b, d2, s // d2, GROUP_QKV), BF16),
                   jax.ShapeDtypeStruct((b, s, 2 * D_MODEL), BF16)],
        scratch_shapes=[pltpu.VMEM((tm, d), BF16),
                        pltpu.VMEM((ROW_SPLIT, 2, GROUP_QKV // LANES, tm // ROW_SPLIT, LANES), F32)],
        compiler_params=pltpu.CompilerParams(
            dimension_semantics=("arbitrary", "arbitrary"), vmem_limit_bytes=VMEM_LIMIT),
        name="inproj",
    )(x, g_pre, sc, sh, w_perm)


def _attn_kernel(slopes_ref, q0, k0, v0, q1, k1, v1, q2, k2, v2, o_ref,
                 bias_scr, acc, mst, lst, s_even, s_odd):
    pair = pl.program_id(1)
    seq = o_ref.shape[1]
    nblk = seq // ATT_BLOCK
    lane = lax.broadcasted_iota(I32, (ATT_BLOCK, LANES), 1)
    first_head = lane < HEAD_DIM
    half_lane = lax.broadcasted_iota(I32, (ATT_BLOCK // 2, LANES), 1)
    head_bits = (jnp.where(half_lane < HEAD_DIM, -1, 0), jnp.where(half_lane < HEAD_DIM, 0, -1))

    qi = lax.broadcasted_iota(I32, (ATT_BLOCK, 2 * ATT_BLOCK), 0)
    kj = lax.broadcasted_iota(I32, (ATT_BLOCK, 2 * ATT_BLOCK), 1)
    delta = ATT_BLOCK + qi - kj
    valid = (delta >= 0) & (delta <= ATT_REACH)
    delta0 = qi - kj
    valid0 = delta0 >= 0
    for g, d in enumerate(ATT_DILATIONS):
        for j in range(2):
            slope = slopes_ref[g * HEADS_PER_GROUP + 2 * pair + j]
            slope = slope * LOG2_E
            bias_scr[g, j, 0] = jnp.where(valid0, -slope * (delta0 * d).astype(F32), MASKED)
            bias_scr[g, j, 1] = jnp.where(valid, -slope * (delta * d).astype(F32), MASKED)

    def run_group(g, q_ref, k_ref, v_ref):
        d = ATT_DILATIONS[g]
        per_res = nblk // d
        n_iter = nblk // ATT_UNROLL

        def block_index(it, k):
            n = it * ATT_UNROLL + k
            return n // per_res, n % per_res

        def key_rows(i):
            lo = jnp.maximum(i - 1, 0)
            return pl.ds(pl.multiple_of(lo * ATT_BLOCK, ATT_BLOCK), 2 * ATT_BLOCK)

        def scores(it, s_ref):
            for k in range(ATT_UNROLL):
                r, i = block_index(it, k)
                q = q_ref[0, r, pl.ds(pl.multiple_of(i * ATT_BLOCK, ATT_BLOCK), ATT_BLOCK), :]
                kw = k_ref[0, r, key_rows(i), :]
                qbits = pltpu.bitcast(q, I32)
                for j in range(2):
                    qh = pltpu.bitcast(qbits & head_bits[j], BF16)
                    s_ref[2 * k + j] = _dot_nt(qh, kw) + bias_scr[g, j, jnp.minimum(i, 1)]

        def weighted_values(it, s_ref):
            for k in range(ATT_UNROLL):
                r, i = block_index(it, k)
                vw = v_ref[0, r, key_rows(i), :]
                outs = []
                for j in range(2):
                    sc = s_ref[2 * k + j]
                    mx = jnp.max(sc, axis=-1, keepdims=True)
                    p = jnp.exp2(sc - mx)
                    den = jnp.sum(p, axis=-1, keepdims=True)
                    outs.append((_dot(p.astype(BF16), vw), mx, den))
                (n0, m0, l0), (n1, m1, l1) = outs
                if d == 1:
                    rows = pl.ds(pl.multiple_of(i * ATT_BLOCK, ATT_BLOCK), ATT_BLOCK)
                else:
                    rows = pl.ds(i * (ATT_BLOCK * d) + r, ATT_BLOCK, stride=d)
                acc[g, rows, :] = jnp.where(first_head, n0, n1)
                mst[g, rows, :] = jnp.where(first_head, m0, m1)
                lst[g, rows, :] = jnp.where(first_head, l0, l1)

        scores(0, s_even)

        def body(h, carry):
            it = 2 * h
            scores(it + 1, s_odd)
            weighted_values(it, s_even)
            scores(jnp.minimum(it + 2, n_iter - 1), s_even)
            weighted_values(it + 1, s_odd)
            return carry

        lax.fori_loop(0, n_iter // 2, body, 0)

    run_group(0, q0, k0, v0)
    run_group(1, q1, k1, v1)
    run_group(2, q2, k2, v2)

    def finish(n, carry):
        rows = pl.ds(pl.multiple_of(n * ATT_BLOCK, ATT_BLOCK), ATT_BLOCK)
        ms = [mst[g, rows, :] for g in range(3)]
        top = jnp.maximum(jnp.maximum(ms[0], ms[1]), ms[2])
        scale = [jnp.exp2(m - top) for m in ms]
        num = scale[0] * acc[0, rows, :] + scale[1] * acc[1, rows, :] + scale[2] * acc[2, rows, :]
        den = scale[0] * lst[0, rows, :] + scale[1] * lst[1, rows, :] + scale[2] * lst[2, rows, :]
        o_ref[0, rows, :] = (num / den).astype(BF16)
        return carry

    lax.fori_loop(0, nblk, finish, 0)


def _attention(qkv0, qkv1, qkv2, slopes):
    b, _, s, _ = qkv0.shape
    pairs = HEADS_PER_GROUP // 2
    col_blocks = HEADS_PER_GROUP * HEAD_DIM // LANES

    def specs(arr):
        _, d, sub, _ = arr.shape
        return [pl.BlockSpec((1, d, sub, LANES),
                             functools.partial(lambda bi, p, sec: (bi, 0, 0, sec * col_blocks + p), sec=sec))
                for sec in range(3)]

    return pl.pallas_call(
        _attn_kernel,
        grid=(b, pairs),
        in_specs=[pl.BlockSpec(memory_space=pltpu.SMEM)] + specs(qkv0) + specs(qkv1) + specs(qkv2),
        out_specs=pl.BlockSpec((1, s, LANES), lambda bi, p: (bi, 0, p)),
        out_shape=jax.ShapeDtypeStruct((b, s, ATT_OUT_WIDTH), BF16),
        scratch_shapes=[pltpu.VMEM((3, 2, 2, ATT_BLOCK, 2 * ATT_BLOCK), F32),
                        pltpu.VMEM((3, s, LANES), F32),
                        pltpu.VMEM((3, s, LANES), F32),
                        pltpu.VMEM((3, s, LANES), F32),
                        pltpu.VMEM((2 * ATT_UNROLL, ATT_BLOCK, 2 * ATT_BLOCK), F32),
                        pltpu.VMEM((2 * ATT_UNROLL, ATT_BLOCK, 2 * ATT_BLOCK), F32)],
        compiler_params=pltpu.CompilerParams(
            dimension_semantics=("arbitrary", "arbitrary"), vmem_limit_bytes=VMEM_LIMIT),
        name="attention",
    )(slopes, qkv0, qkv0, qkv0, qkv1, qkv1, qkv1, qkv2, qkv2, qkv2)


def _mixtail_kernel(u_ref, halo_ref, att_ref, gate_ref, x_ref,
                    gt_m_ref, sc_f_ref, sh_f_ref, g_post_ref, g_pre_ref,
                    wpg_ref, pscale_ref, wbp_ref, wba_ref, wout_ref, wr_ref, br_ref,
                    xmid_ref, route_ref, meta_ref, counts_ref, xs_hbm,
                    pu, lv, xbuf, zbuf, fill, meta_s, sem, sem_s):
    i = pl.program_id(1)
    tm = x_ref.shape[1]
    step = pl.program_id(0) * pl.num_programs(1) + i
    last = pl.num_programs(0) * pl.num_programs(1) - 1
    slot = step % 2
    region = _region_rows(pl.num_programs(0) * pl.num_programs(1) * tm)

    @pl.when(step == 0)
    def _():
        fill[...] = jnp.zeros_like(fill)
        pu[0:POOL_PAD - POOL_HALO, :] = jnp.zeros((POOL_PAD - POOL_HALO, POOL_WIDTH), F32)
        lv[:, :, 0:POOL_PAD - POOL_HALO, :] = jnp.zeros(
            (SORT_GROUPS, 2, POOL_PAD - POOL_HALO, POOL_GROUP_DIM), F32)
        for sub in range(SORT_GROUPS):
            for e in range(N_EXPERTS):
                meta_s[SORT_GROUPS + sub, 1, e] = 0
                meta_s[SORT_GROUPS + sub, 0, e] = 0

    def send_tile(which):
        for sub in range(SORT_GROUPS):
            m = which * SORT_GROUPS + sub
            _tile_run_copies(lambda e: meta_s[m, 1, e], xbuf.at[which, sub], None,
                             xs_hbm, lambda e: e * region + meta_s[m, 0, e], sem.at[which])

    def wait_tile(which):
        for sub in range(SORT_GROUPS):
            pltpu.make_async_copy(xbuf.at[which, sub], xs_hbm.at[pl.ds(0, SORT_ROWS * ROW_WORDS), :],
                                  sem.at[which]).wait()

    @pl.when(step > 1)
    def _():
        wait_tile(slot)

    send_tile(1 - slot)

    head = POOL_PAD - POOL_HALO
    halo = halo_ref[0].astype(F32)
    pu[head:POOL_PAD, :] = jnp.where(i > 0, halo, jnp.zeros_like(halo))
    pu[POOL_PAD:POOL_PAD + tm, :] = u_ref[0].astype(F32)

    ts = SORT_TOKENS

    def pool(sub, st):
        base = sub * ts
        t = i * tm + base + lax.broadcasted_iota(I32, (ts, 1), 0)
        pooled_groups = []
        for g, w in enumerate(POOL_WINDOWS):
            cols = slice(g * POOL_GROUP_DIM, (g + 1) * POOL_GROUP_DIM)
            read = lambda start, n: pu[pl.ds(base + start, n), cols]
            shift, level = 1, 0
            while 2 * shift < w:
                partial = read(head, ts + POOL_HALO) + read(head - shift, ts + POOL_HALO)
                buf = lv.at[sub, level % 2]
                buf[pl.ds(head, ts + POOL_HALO), :] = partial
                read = lambda start, n, buf=buf: buf[pl.ds(start, n), :]
                shift, level = 2 * shift, level + 1
            win = read(POOL_PAD, ts) + read(POOL_PAD - shift, ts)
            count = jnp.minimum(t + 1, w).astype(F32)
            pooled_groups.append((win / count - pu[pl.ds(base + POOL_PAD, ts), cols]).astype(BF16))
        st["pooled"] = jnp.concatenate(pooled_groups, axis=1)

    neg_inf = -jnp.inf
    far = float(LANES)
    before = (lax.broadcasted_iota(I32, (ts, ts), 0) < lax.broadcasted_iota(I32, (ts, ts), 1))
    earlier = jnp.where(before, 1.0, 0.0).astype(BF16)
    erow = lax.broadcasted_iota(I32, (N_EXPERTS, ts), 0).astype(F32)
    e_col = lax.broadcasted_iota(I32, (N_EXPERTS, LANES), 0)
    e_lane = lax.broadcasted_iota(I32, (N_EXPERTS, LANES), 1)
    row8 = lax.broadcasted_iota(I32, (SUBLANES, ts), 0)
    mrow = lax.broadcasted_iota(I32, (SUBLANES, LANES), 0)
    srow = lax.broadcasted_iota(I32, (SORT_ROWS, ts), 0).astype(F32)
    filled = fill[...]

    def rows_of(sub):
        return slice(sub * ts, (sub + 1) * ts)

    def branch_projections(sub, st):
        rows = rows_of(sub)
        mixed = _dot(st.pop("pooled"), wpg_ref[...]) * pscale_ref[...]
        st["y_pool"] = _dot(mixed.astype(BF16), wbp_ref[...])
        st["y_att"] = _dot(att_ref[0, rows, :], wba_ref[...])

    def gated_sum(sub, st):
        rows = rows_of(sub)
        st["merged"] = (gate_ref[0, rows, 0:D_MODEL] * st.pop("y_pool").astype(BF16)
                        + gate_ref[0, rows, D_MODEL:2 * D_MODEL] * st.pop("y_att").astype(BF16))

    def output_projection(sub, st):
        st["y"] = _dot(st.pop("merged"), wout_ref[...])

    def residual_and_ffn_input(sub, st):
        rows = rows_of(sub)
        x_mid = x_ref[0, rows, :] + gt_m_ref[0] * _rmsnorm(st.pop("y"), g_post_ref[...])
        xmid_ref[0, rows, :] = x_mid
        h2 = _rmsnorm(x_mid, g_pre_ref[...]) * (1.0 + sc_f_ref[0]) + sh_f_ref[0]
        st["h2b"] = h2.astype(BF16)
        st["h2lo"] = (h2 - st["h2b"].astype(F32)).astype(BF16)

    def router_logits(sub, st):
        by_hi = _dot_nt(wr_ref[...], st["h2b"])
        st["logits"] = (by_hi[0:ROUTER_ROWS] + (by_hi[ROUTER_ROWS:2 * ROUTER_ROWS]
                        + _dot_nt(wr_ref[0:ROUTER_ROWS, :], st.pop("h2lo"))) + br_ref[...])

    def route(sub, st):
        logits = st.pop("logits")
        gl = logits[N_EXPERTS:N_EXPERTS + N_EXPERT_GROUPS, :]
        grow = lax.broadcasted_iota(I32, gl.shape, 0).astype(F32)
        gmax = jnp.max(gl, axis=0, keepdims=True)
        gsel = jnp.min(jnp.where(gl == gmax, grow, far), axis=0, keepdims=True)
        p_group = 1.0 / jnp.sum(jnp.exp(gl - gmax), axis=0, keepdims=True)
        e_lo = gsel * float(EXPERTS_PER_GROUP)
        el = jnp.where((erow >= e_lo) & (erow < e_lo + float(EXPERTS_PER_GROUP)),
                       logits[0:N_EXPERTS, :], neg_inf)
        v1 = jnp.max(el, axis=0, keepdims=True)
        i1 = jnp.min(jnp.where(el == v1, erow, far), axis=0, keepdims=True)
        el2 = jnp.where(erow == i1, neg_inf, el)
        v2 = jnp.max(el2, axis=0, keepdims=True)
        i2 = jnp.min(jnp.where(el2 == v2, erow, far), axis=0, keepdims=True)
        e21 = jnp.exp(v2 - v1)
        st["w1"] = p_group / (1.0 + e21)
        st["w2"] = p_group * e21 / (1.0 + e21)
        st["pick1"] = erow == i1
        st["pick2"] = erow == i2
        st["assign"] = jnp.where(st["pick1"] | st["pick2"], 1.0, 0.0)

    def count_and_rank(sub, st):
        assign = st.pop("assign")
        st["rank"] = _dot(assign.astype(BF16), earlier)
        assign_pad = jnp.concatenate([assign, jnp.zeros((LANES - N_EXPERTS, ts), F32)], axis=0).astype(BF16)
        st["cnt_row"] = _dot_nt(jnp.ones((SUBLANES, ts), BF16), assign_pad)

    def sorted_positions(sub, st):
        cnt_row = st["cnt_row"]
        run_start = jnp.sum(jnp.where(e_lane < e_col, cnt_row[0:1, :], 0.0), axis=1, keepdims=True)
        pos = st.pop("rank") + run_start
        key1 = jnp.sum(jnp.where(st.pop("pick1"), pos, 0.0), axis=0, keepdims=True)
        key2 = jnp.sum(jnp.where(st.pop("pick2"), pos, 0.0), axis=0, keepdims=True)
        route_ref[:, rows_of(sub)] = jnp.where(row8 == 0, key1, jnp.where(row8 == 1, key2,
                                               jnp.where(row8 == 2, st.pop("w1"),
                                                         jnp.where(row8 == 3, st.pop("w2"), 0.0))))
        st["perm"] = jnp.where((srow == key1) | (srow == key2), 1.0, 0.0).astype(BF16)

    def sort_rows(sub, st):
        perm, h2b = st.pop("perm"), st.pop("h2b")
        for w in range(ROW_WORDS):
            pair = _dot(perm, h2b[:, 2 * w * LANES:(2 * w + 2) * LANES])
            _pack_rows(xbuf.at[slot, sub], w, SORT_ROWS, pair[:, :LANES], pair[:, LANES:])

    state = _run_staggered((pool, branch_projections, gated_sum, output_projection,
                            residual_and_ffn_input, router_logits, route, count_and_rank,
                            sorted_positions, sort_rows), SORT_GROUPS, lag=0)

    for sub in range(SORT_GROUPS):
        cnt_row = state[sub]["cnt_row"]
        meta_ref[sub] = jnp.where(mrow == 0, filled, jnp.where(mrow == 1, cnt_row, 0.0)).astype(I32)
        filled = filled + cnt_row

    fill[...] = filled
    counts_ref[...] = filled.astype(I32)

    meta_copy = pltpu.make_async_copy(meta_ref, meta_s.at[pl.ds(slot * SORT_GROUPS, SORT_GROUPS)], sem_s)
    meta_copy.start()
    meta_copy.wait()

    @pl.when(step == last)
    def _():
        send_tile(slot)

        @pl.when(step > 0)
        def _():
            wait_tile(1 - slot)

        wait_tile(slot)
        zbuf[...] = jnp.zeros_like(zbuf)
        final = slot * SORT_GROUPS + SORT_GROUPS - 1

        def pad_copy(e):
            end = e * region + meta_s[final, 0, e] + meta_s[final, 1, e]
            return pltpu.make_async_copy(
                zbuf, xs_hbm.at[pl.ds(end * ROW_WORDS, MOE_BLOCK * ROW_WORDS), :], sem_s)

        def start_pad(e, carry):
            pad_copy(e).start()
            return carry

        def wait_pad(e, carry):
            pad_copy(e).wait()
            return carry

        lax.fori_loop(0, N_EXPERTS, start_pad, 0)
        lax.fori_loop(0, N_EXPERTS, wait_pad, 0)


def _mixtail(u, att, gates, x, gt_m, sc_f, sh_f, g_post, g_pre,
             wpg, pscale, wbp, wba, wout, wr, br):
    b, s, d = x.shape
    tm = ROW_TILE
    tiles = s // tm
    n_tiles = b * tiles
    halo_blocks = tm // POOL_HALO
    region = _region_rows(b * s)
    const2 = lambda bi, i: (0, 0)
    const3 = lambda bi, i: (0, 0, 0)
    per_b = lambda bi, i: (bi, 0, 0)
    tile = lambda bi, i: (bi, i, 0)
    single = dict(pipeline_mode=pl.Buffered(1))
    return pl.pallas_call(
        _mixtail_kernel,
        grid=(b, tiles),
        in_specs=[pl.BlockSpec((1, tm, POOL_WIDTH), tile),
                  pl.BlockSpec((1, POOL_HALO, POOL_WIDTH),
                               lambda bi, i: (bi, jnp.maximum(i * halo_blocks - 1, 0), 0)),
                  pl.BlockSpec((1, tm, ATT_OUT_WIDTH), tile),
                  pl.BlockSpec((1, tm, 2 * D_MODEL), tile),
                  pl.BlockSpec((1, tm, d), tile),
                  pl.BlockSpec((1, 1, d), per_b),
                  pl.BlockSpec((1, 1, d), per_b),
                  pl.BlockSpec((1, 1, d), per_b),
                  pl.BlockSpec((1, d), const2),
                  pl.BlockSpec((1, d), const2),
                  pl.BlockSpec(wpg.shape, const2, **single),
                  pl.BlockSpec((1, POOL_WIDTH), const2),
                  pl.BlockSpec(wbp.shape, const2, **single),
                  pl.BlockSpec(wba.shape, const2, **single),
                  pl.BlockSpec(wout.shape, const2, **single),
                  pl.BlockSpec(wr.shape, const2, **single),
                  pl.BlockSpec(br.shape, const2)],
        out_specs=[pl.BlockSpec((1, tm, d), tile),
                   pl.BlockSpec((SUBLANES, tm), lambda bi, i: (0, bi * tiles + i)),
                   pl.BlockSpec((SORT_GROUPS, SUBLANES, LANES), lambda bi, i: (bi * tiles + i, 0, 0)),
                   pl.BlockSpec((SUBLANES, LANES), const2),
                   pl.BlockSpec(memory_space=pl.ANY)],
        out_shape=[jax.ShapeDtypeStruct((b, s, d), F32),
                   jax.ShapeDtypeStruct((SUBLANES, b * s), F32),
                   jax.ShapeDtypeStruct((n_tiles * SORT_GROUPS, SUBLANES, LANES), I32),
                   jax.ShapeDtypeStruct((SUBLANES, LANES), I32),
                   jax.ShapeDtypeStruct((N_EXPERTS * region * ROW_WORDS, LANES), I32)],
        scratch_shapes=[pltpu.VMEM((POOL_PAD + tm, POOL_WIDTH), F32),
                        pltpu.VMEM((SORT_GROUPS, 2, POOL_PAD + SORT_TOKENS, POOL_GROUP_DIM), F32),
                        pltpu.VMEM((2, SORT_GROUPS, SORT_ROWS * ROW_WORDS, LANES), I32),
                        pltpu.VMEM((MOE_BLOCK * ROW_WORDS, LANES), I32),
                        pltpu.VMEM((SUBLANES, LANES), F32),
                        pltpu.SMEM((2 * SORT_GROUPS, SUBLANES, LANES), I32),
                        pltpu.SemaphoreType.DMA((2,)),
                        pltpu.SemaphoreType.DMA(())],
        compiler_params=pltpu.CompilerParams(
            dimension_semantics=("arbitrary", "arbitrary"), vmem_limit_bytes=VMEM_LIMIT),
        name="mixtail",
    )(u, u, att, gates, x, gt_m, sc_f, sh_f, g_post, g_pre,
      wpg, pscale, wbp, wba, wout, wr, br)


def _expert_kernel(counts_ref, xs_hbm, wg_ref, wu_ref, wd_ref, ys_hbm,
                   wg_bf, wu_bf, wd_bf, xbuf, ybuf, state, semx, semy):
    e = pl.program_id(0)
    bm = MOE_BLOCK
    nx, ny = EXPERT_X_BUFFERS, EXPERT_Y_BUFFERS
    block_words = bm * ROW_WORDS
    region = xs_hbm.shape[0] // (N_EXPERTS * ROW_WORDS)

    def n_blocks(ex):
        return (counts_ref[ex] + (bm - 1)) // bm

    def block_rows(ref, ex, k):
        start = pl.multiple_of((ex * region + k * bm) * ROW_WORDS, block_words)
        return ref.at[pl.ds(start, block_words), :]

    def x_copy(ex, k, s):
        return pltpu.make_async_copy(block_rows(xs_hbm, ex, k), xbuf.at[s], semx.at[s])

    def y_copy(k, s):
        return pltpu.make_async_copy(ybuf.at[s], block_rows(ys_hbm, e, k), semy.at[s])

    @pl.when(e == 0)
    def _():
        for j in range(4):
            state[j] = 0

    def fetch_through(target):
        def more(c):
            pe, _, pg = c
            return (pg < target) & (pe < N_EXPERTS)

        def step(c):
            pe, pk, pg = c
            has = pk < n_blocks(pe)

            @pl.when(has)
            def _():
                x_copy(pe, pk, pg % nx).start(priority=BLOCK_DMA_PRIORITY)

            return (jnp.where(has, pe, pe + 1), jnp.where(has, pk + 1, 0), pg + has.astype(I32))

        pe, pk, pg = lax.while_loop(more, step, (state[1], state[2], state[3]))
        state[1] = pe
        state[2] = pk
        state[3] = pg

    wg_bf[...] = wg_ref[...].astype(BF16)
    wu_bf[...] = wu_ref[...].astype(BF16)
    wd_bf[...] = wd_ref[...].astype(BF16)

    nb = n_blocks(e)
    done = state[0]

    def blocks(k0, count):
        ks = [k0 + c for c in range(count)]
        gs = [done + k for k in ks]
        fetch_through(gs[0] + nx)
        for k, g in zip(ks, gs):
            x_copy(e, k, g % nx).wait()

            @pl.when(g >= ny)
            def _():
                y_copy(k, g % ny).wait()

        mids = []
        for g in gs:
            x = _unpack_rows(xbuf.at[g % nx], bm).astype(BF16)
            a = _dot(x, wg_bf[...])
            u = _dot(x, wu_bf[...])
            mids.append(((a * _sigmoid(a)) * u).astype(BF16))
        for k, g, mid in zip(ks, gs, mids):
            for w in range(ROW_WORDS):
                pair = _bf16_exact(_dot(mid, wd_bf[:, 2 * w * LANES:(2 * w + 2) * LANES]))
                _pack_rows(ybuf.at[g % ny], w, bm, pair[:, :LANES], pair[:, LANES:])
            y_copy(k, g % ny).start(priority=BLOCK_DMA_PRIORITY)

    def pair_of_blocks(p, carry):
        blocks(2 * p, 2)
        return carry

    lax.fori_loop(0, nb // 2, pair_of_blocks, 0)

    @pl.when(nb % 2 == 1)
    def _():
        blocks(nb - 1, 1)

    state[0] = done + nb

    @pl.when(e == N_EXPERTS - 1)
    def _():
        total = done + nb
        for j in range(ny):
            @pl.when(total > j)
            def _():
                y_copy(0, (total - 1 - j) % ny).wait()


def _experts(xs, counts, w_gate, w_up, w_down):
    bm = MOE_BLOCK
    w_in = pl.BlockSpec((None, D_MODEL, D_EXPERT), lambda e, cnt: (e, 0, 0))
    w_out = pl.BlockSpec((None, D_EXPERT, D_MODEL), lambda e, cnt: (e, 0, 0))
    grid_spec = pltpu.PrefetchScalarGridSpec(
        num_scalar_prefetch=1,
        grid=(N_EXPERTS,),
        in_specs=[pl.BlockSpec(memory_space=pl.ANY), w_in, w_in, w_out],
        out_specs=pl.BlockSpec(memory_space=pl.ANY),
        scratch_shapes=[pltpu.VMEM((D_MODEL, D_EXPERT), BF16),
                        pltpu.VMEM((D_MODEL, D_EXPERT), BF16),
                        pltpu.VMEM((D_EXPERT, D_MODEL), BF16),
                        pltpu.VMEM((EXPERT_X_BUFFERS, bm * ROW_WORDS, LANES), I32),
                        pltpu.VMEM((EXPERT_Y_BUFFERS, bm * ROW_WORDS, LANES), I32),
                        pltpu.SMEM((4,), I32),
                        pltpu.SemaphoreType.DMA((EXPERT_X_BUFFERS,)),
                        pltpu.SemaphoreType.DMA((EXPERT_Y_BUFFERS,))],
    )
    return pl.pallas_call(
        _expert_kernel,
        grid_spec=grid_spec,
        out_shape=jax.ShapeDtypeStruct(xs.shape, I32),
        compiler_params=pltpu.CompilerParams(
            dimension_semantics=("arbitrary",), vmem_limit_bytes=VMEM_LIMIT),
        name="experts",
    )(counts, xs, w_gate, w_up, w_down)


def _combine_kernel(meta_ref, y_hbm, route_ref, xmid_ref, gt_ref, g_ref, o_ref, ybuf, rt_scr, sem):
    step = pl.program_id(0)
    n_steps = pl.num_programs(0)
    tm = ROW_TILE
    slot = step % COMBINE_BUFFERS
    region = _region_rows(n_steps * tm)

    def fetch_tile(tile, live):
        tile = jnp.minimum(tile, n_steps - 1)
        which = tile % COMBINE_BUFFERS
        for sub in range(SORT_GROUPS):
            m = (tile * SORT_GROUPS + sub) * 2
            _tile_run_copies(lambda e: jnp.where(live, meta_ref[(m + 1) * N_EXPERTS + e], 0),
                             y_hbm, lambda e: e * region + meta_ref[m * N_EXPERTS + e],
                             ybuf.at[which, sub], None, sem.at[which])

    @pl.when(step == 0)
    def _():
        for ahead in range(COMBINE_BUFFERS - 1):
            fetch_tile(ahead, ahead < n_steps)

    for sub in range(SORT_GROUPS):
        pltpu.make_async_copy(y_hbm.at[pl.ds(0, SORT_ROWS * ROW_WORDS), :], ybuf.at[slot, sub],
                              sem.at[slot]).wait()
    sorted_rows = [_unpack_rows(ybuf.at[slot, sub], SORT_ROWS).astype(BF16) for sub in range(SORT_GROUPS)]

    fetch_tile(step + COMBINE_BUFFERS - 1, step + COMBINE_BUFFERS - 1 < n_steps)

    rt_scr[...] = jnp.zeros_like(rt_scr)
    rt_scr[0:SUBLANES, :] = route_ref[...]
    cols = jnp.concatenate([rt_scr[:, c * LANES:(c + 1) * LANES].T for c in range(tm // LANES)], axis=0)

    ts = SORT_TOKENS
    spos = lax.broadcasted_iota(I32, (ts, SORT_ROWS), 1).astype(F32)
    def rows_of(sub):
        return slice(sub * ts, (sub + 1) * ts)

    def gate_matrix(sub, st):
        key1, key2, w1, w2 = (cols[rows_of(sub), c:c + 1] for c in range(4))
        st["gates"] = jnp.where(spos == key1, w1, jnp.where(spos == key2, w2, 0.0)).astype(BF16)

    def weighted_unsort(sub, st):
        st["y"] = _dot(st.pop("gates"), sorted_rows[sub])

    def norm_and_add(sub, st):
        rows = rows_of(sub)
        o_ref[rows, :] = xmid_ref[rows, :] + gt_ref[0] * _rmsnorm(st.pop("y"), g_ref[...])

    _run_staggered((gate_matrix, weighted_unsort, norm_and_add), SORT_GROUPS, lag=1)


def _combine(meta, ys, route, x_mid, gt_f, g_post, seq):
    t, d = x_mid.shape
    tm = ROW_TILE
    tiles_per_seq = seq // tm
    grid_spec = pltpu.PrefetchScalarGridSpec(
        num_scalar_prefetch=1,
        grid=(t // tm,),
        in_specs=[pl.BlockSpec(memory_space=pl.ANY),
                  pl.BlockSpec((SUBLANES, tm), lambda i, m: (0, i)),
                  pl.BlockSpec((tm, d), lambda i, m: (i, 0)),
                  pl.BlockSpec((1, 1, d), lambda i, m: (i // tiles_per_seq, 0, 0)),
                  pl.BlockSpec((1, d), lambda i, m: (0, 0))],
        out_specs=pl.BlockSpec((tm, d), lambda i, m: (i, 0)),
        scratch_shapes=[pltpu.VMEM((COMBINE_BUFFERS, SORT_GROUPS, SORT_ROWS * ROW_WORDS, LANES), I32),
                        pltpu.VMEM((LANES, tm), F32),
                        pltpu.SemaphoreType.DMA((COMBINE_BUFFERS,))],
    )
    return pl.pallas_call(
        _combine_kernel,
        grid_spec=grid_spec,
        out_shape=jax.ShapeDtypeStruct((t, d), F32),
        compiler_params=pltpu.CompilerParams(
            dimension_semantics=("arbitrary",), vmem_limit_bytes=VMEM_LIMIT),
        name="combine",
    )(meta, ys, route, x_mid, gt_f, g_post)


def kernel(x, c, w_ada, b_ada, g_pre_mix, g_post_mix, g_pre_ffn, g_post_ffn, w_in, w_pool_group, pool_scale, w_branch_pool, w_branch_att, w_out, w_group_router, b_group_router, w_expert_router, b_expert_router, w_exp_gate, w_exp_up, w_exp_down):
    b, s, d = x.shape
    assert d == D_MODEL and s % (ATT_BLOCK * 2 * ATT_DILATIONS[2]) == 0 and s % ROW_TILE == 0
    assert (b * s) % MOE_BLOCK == 0
    depth = w_ada.shape[0]
    slopes = jnp.exp2(-ALIBI_MAX_BIAS * jnp.arange(1, N_ATT_HEADS + 1, dtype=F32) / N_ATT_HEADS)
    q_scale = HEAD_DIM ** -0.5 * LOG2_E

    for layer in range(depth):
        mod = _adaln(c, w_ada[layer], b_ada[layer]).reshape(b, 6, 1, d)
        sh_m, sc_m, gt_m, sh_f, sc_f, gt_f = [mod[:, j] for j in range(6)]

        wl = w_in[layer]
        q_lo, k_lo, v_lo = POOL_WIDTH, POOL_WIDTH + 768, POOL_WIDTH + 2 * 768
        group_cols = []
        for g in range(3):
            sl = slice(g * 256, (g + 1) * 256)
            group_cols += [wl[:, q_lo:k_lo][:, sl] * q_scale, wl[:, k_lo:v_lo][:, sl],
                           wl[:, v_lo:v_lo + 768][:, sl]]
        w_perm = jnp.concatenate([wl[:, :POOL_WIDTH]] + group_cols + [wl[:, v_lo + 768:]],
                                 axis=1).astype(BF16)

        u, qkv0, qkv1, qkv2, gates = _inproj(x, g_pre_mix[layer].reshape(1, d), sc_m, sh_m, w_perm)
        att = _attention(qkv0, qkv1, qkv2, slopes)

        zero_block = jnp.zeros((POOL_GROUP_DIM, POOL_GROUP_DIM), F32)
        wpg_diag = jnp.block([[w_pool_group[layer, g] if g == h else zero_block
                               for h in range(len(POOL_WINDOWS))] for g in range(len(POOL_WINDOWS))])

        pad_rows = ROUTER_ROWS - N_EXPERTS - N_EXPERT_GROUPS
        wrt = jnp.concatenate([w_expert_router[layer].T, w_group_router[layer].T,
                               jnp.zeros((pad_rows, d), F32)], axis=0)
        wrt_hi = wrt.astype(BF16)
        wr = jnp.concatenate([wrt_hi, (wrt - wrt_hi.astype(F32)).astype(BF16)], axis=0)
        br = jnp.concatenate([b_expert_router[layer], b_group_router[layer],
                              jnp.zeros((pad_rows,), F32)]).reshape(ROUTER_ROWS, 1)

        x_mid, route, meta, counts, xs = _mixtail(
            u, att, gates, x, gt_m, sc_f, sh_f,
            g_post_mix[layer].reshape(1, d), g_pre_ffn[layer].reshape(1, d),
            wpg_diag.astype(BF16), pool_scale[layer].reshape(1, POOL_WIDTH),
            w_branch_pool[layer].astype(BF16), w_branch_att[layer].astype(BF16),
            w_out[layer].astype(BF16), wr, br)

        ys = _experts(xs, counts[0, :N_EXPERTS], w_exp_gate[layer], w_exp_up[layer], w_exp_down[layer])
        run_meta = meta[:, 0:2, 0:N_EXPERTS].reshape(-1)
        x = _combine(run_meta, ys, route, x_mid.reshape(b * s, d), gt_f,
                     g_post_ffn[layer].reshape(1, d), s).reshape(b, s, d)
    return x
```

```python
import functools

import jax
import jax.numpy as jnp
from jax import lax
from jax.experimental import pallas as pl
from jax.experimental.pallas import tpu as pltpu

F32 = jnp.float32
BF16 = jnp.bfloat16
I32 = jnp.int32
HIGHEST = lax.Precision.HIGHEST

D_MODEL = 1024
LANES = 128
SUBLANES = 8
ROW_WORDS = D_MODEL // (2 * LANES)
HI_HALF = -65536

POOL_WINDOWS = (2, 4, 8, 16)
POOL_GROUP_DIM = 128
POOL_WIDTH = 512
POOL_HALO = 16
POOL_PAD = POOL_HALO + 8

HEAD_DIM = 64
ATT_DILATIONS = (1, 4, 16)
ATT_REACH = 128
ATT_BLOCK = 128
ATT_UNROLL = 4
HEADS_PER_GROUP = 4
N_ATT_HEADS = 12
GROUP_QKV = 3 * HEADS_PER_GROUP * HEAD_DIM
ATT_OUT_WIDTH = 256
ALIBI_MAX_BIAS = 8.0
IN_WIDTH = POOL_WIDTH + 3 * GROUP_QKV + 2 * D_MODEL
MASKED = -1e30
LOG2_E = 1.4426950408889634

N_EXPERT_GROUPS = 4
EXPERTS_PER_GROUP = 8
N_EXPERTS = 32
ROUTER_ROWS = 48
D_EXPERT = 512
RMS_EPS = 1e-6

ROW_TILE = 1024
ROW_SPLIT = 2
SORT_TOKENS = 256
SORT_GROUPS = ROW_TILE // SORT_TOKENS
SORT_ROWS = 2 * SORT_TOKENS
MOE_BLOCK = 256
BLOCK_DMA_PRIORITY = 1
EXPERT_X_BUFFERS = 6
EXPERT_Y_BUFFERS = 4
COMBINE_BUFFERS = 3
VMEM_LIMIT = 52 * 1024 * 1024


def _sigmoid(x):
    return 0.5 * jnp.tanh(0.5 * x) + 0.5


def _rmsnorm(x, g):
    return x * lax.rsqrt(jnp.mean(x * x, axis=-1, keepdims=True) + RMS_EPS) * g


def _dot(a, b):
    return jnp.dot(a, b, preferred_element_type=F32)


def _dot_nt(a, b, **kw):
    return lax.dot_general(a, b, (((1,), (1,)), ((), ())), preferred_element_type=F32, **kw)


def _pack_rows(ref, w, n, lo, hi):
    word = (lax.shift_right_logical(pltpu.bitcast(lo, I32), 16) | (pltpu.bitcast(hi, I32) & HI_HALF))
    ref[pl.ds(w, n, stride=ROW_WORDS), :] = word


def _unpack_rows(ref, n):
    cols = []
    for w in range(ROW_WORDS):
        word = ref[pl.ds(w, n, stride=ROW_WORDS), :]
        cols += [pltpu.bitcast(word << 16, F32), pltpu.bitcast(word & HI_HALF, F32)]
    return jnp.concatenate(cols, axis=1)


def _bf16_exact(x):
    return x.astype(BF16).astype(F32)


def _run_staggered(stages, n_chains, lag):
    state = [{} for _ in range(n_chains)]
    for tick in range(len(stages) + lag * (n_chains - 1)):
        for chain in range(n_chains):
            if 0 <= tick - lag * chain < len(stages):
                stages[tick - lag * chain](chain, state[chain])
    return state


def _region_rows(n_tokens):
    return n_tokens + MOE_BLOCK


def _run_copies(n, src, src_row, dst, dst_row, sem):
    @pl.when(n > 0)
    def _():
        pltpu.make_async_copy(
            src.at[pl.ds(src_row * ROW_WORDS, n * ROW_WORDS), :],
            dst.at[pl.ds(dst_row * ROW_WORDS, n * ROW_WORDS), :], sem).start()


def _tile_run_copies(counts, src, src_rows, dst, dst_rows, sem):
    local = 0
    for e in range(N_EXPERTS):
        n = counts(e)
        _run_copies(n, src, local if src_rows is None else src_rows(e),
                    dst, local if dst_rows is None else dst_rows(e), sem)
        local = local + n


def _adaln_kernel(c_ref, w_ref, b_ref, o_ref):
    c = c_ref[...]
    a = c * _sigmoid(c)
    o_ref[...] = jnp.dot(a, w_ref[...], preferred_element_type=F32, precision=HIGHEST) + b_ref[...]


def _adaln(c, w_ada, b_ada):
    b, d = c.shape
    n = w_ada.shape[1]
    rows = -(-b // SUBLANES) * SUBLANES
    cp = jnp.pad(c, ((0, rows - b), (0, 0)))
    nt = 1536
    out = pl.pallas_call(
        _adaln_kernel,
        grid=(n // nt,),
        in_specs=[pl.BlockSpec((rows, d), lambda j: (0, 0)),
                  pl.BlockSpec((d, nt), lambda j: (0, j)),
                  pl.BlockSpec((1, nt), lambda j: (0, j))],
        out_specs=pl.BlockSpec((rows, nt), lambda j: (0, j)),
        out_shape=jax.ShapeDtypeStruct((rows, n), F32),
        compiler_params=pltpu.CompilerParams(vmem_limit_bytes=VMEM_LIMIT),
        name="adaln",
    )(cp, w_ada, b_ada.reshape(1, n))
    return out[:b]


def _inproj_kernel(x_ref, g_ref, sc_ref, sh_ref, w_ref,
                   u_ref, qkv0_ref, qkv1_ref, qkv2_ref, gate_ref, h_scr, p_scr):
    tm = x_ref.shape[1]
    hm = tm // ROW_SPLIT
    for part in range(ROW_SPLIT):
        rows = slice(part * hm, (part + 1) * hm)
        h = _rmsnorm(x_ref[0, rows, :], g_ref[...]) * (1.0 + sc_ref[0]) + sh_ref[0]
        h_scr[rows, :] = h.astype(BF16)
        hb = h_scr[rows, :]

        u_ref[0, rows, :] = _dot(hb, w_ref[:, 0:POOL_WIDTH]).astype(BF16)

        col = POOL_WIDTH
        qkv0_ref[0, 0, rows, :] = _dot(hb, w_ref[:, col:col + GROUP_QKV]).astype(BF16)
        for gi, (out_ref, d) in enumerate(((qkv1_ref, ATT_DILATIONS[1]), (qkv2_ref, ATT_DILATIONS[2]))):
            col += GROUP_QKV
            proj = _dot(hb, w_ref[:, col:col + GROUP_QKV])
            stage = p_scr.at[part, gi]
            for cb in range(GROUP_QKV // LANES):
                stage[cb] = proj[:, cb * LANES:(cb + 1) * LANES]
            sub = hm // d
            for r in range(d):
                out_ref[0, r, part * sub:(part + 1) * sub, :] = jnp.concatenate(
                    [stage[cb, pl.ds(r, sub, stride=d), :] for cb in range(GROUP_QKV // LANES)],
                    axis=1).astype(BF16)
        col += GROUP_QKV

        chunk = 512
        for j in range(2 * D_MODEL // chunk):
            g = _dot(hb, w_ref[:, col + j * chunk:col + (j + 1) * chunk])
            gate_ref[0, rows, j * chunk:(j + 1) * chunk] = _sigmoid(g).astype(BF16)


def _inproj(x, g_pre, sc, sh, w_perm):
    b, s, d = x.shape
    tm = ROW_TILE
    d1, d2 = ATT_DILATIONS[1], ATT_DILATIONS[2]
    grid = (b, s // tm)
    const = lambda bi, i: (0, 0)
    per_b = lambda bi, i: (bi, 0, 0)
    return pl.pallas_call(
        _inproj_kernel,
        grid=grid,
        in_specs=[pl.BlockSpec((1, tm, d), lambda bi, i: (bi, i, 0)),
                  pl.BlockSpec((1, d), const),
                  pl.BlockSpec((1, 1, d), per_b),
                  pl.BlockSpec((1, 1, d), per_b),
                  pl.BlockSpec((d, IN_WIDTH), const, pipeline_mode=pl.Buffered(1))],
        out_specs=[pl.BlockSpec((1, tm, POOL_WIDTH), lambda bi, i: (bi, i, 0)),
                   pl.BlockSpec((1, 1, tm, GROUP_QKV), lambda bi, i: (bi, 0, i, 0)),
                   pl.BlockSpec((1, d1, tm // d1, GROUP_QKV), lambda bi, i: (bi, 0, i, 0)),
                   pl.BlockSpec((1, d2, tm // d2, GROUP_QKV), lambda bi, i: (bi, 0, i, 0)),
                   pl.BlockSpec((1, tm, 2 * D_MODEL), lambda bi, i: (bi, i, 0))],
        out_shape=[jax.ShapeDtypeStruct((b, s, POOL_WIDTH), BF16),
                   jax.ShapeDtypeStruct((b, 1, s, GROUP_QKV), BF16),
                   jax.ShapeDtypeStruct((b, d1, s // d1, GROUP_QKV), BF16),
                   jax.ShapeDtypeStruct((b, d2, s // d2, GROUP_QKV), BF16),
                   jax.ShapeDtypeStruct((b, s, 2 * D_MODEL), BF16)],
        scratch_shapes=[pltpu.VMEM((tm, d), BF16),
                        pltpu.VMEM((ROW_SPLIT, 2, GROUP_QKV // LANES, tm // ROW_SPLIT, LANES), F32)],
        compiler_params=pltpu.CompilerParams(
            dimension_semantics=("arbitrary", "arbitrary"), vmem_limit_bytes=VMEM_LIMIT),
        name="inproj",
    )(x, g_pre, sc, sh, w_perm)


def _attn_kernel(slopes_ref, q0, k0, v0, q1, k1, v1, q2, k2, v2, o_ref,
                 bias_scr, acc, mst, lst, s_even, s_odd):
    pair = pl.program_id(1)
    seq = o_ref.shape[1]
    nblk = seq // ATT_BLOCK
    lane = lax.broadcasted_iota(I32, (ATT_BLOCK, LANES), 1)
    first_head = lane < HEAD_DIM
    half_lane = lax.broadcasted_iota(I32, (ATT_BLOCK // 2, LANES), 1)
    head_bits = (jnp.where(half_lane < HEAD_DIM, -1, 0), jnp.where(half_lane < HEAD_DIM, 0, -1))

    qi = lax.broadcasted_iota(I32, (ATT_BLOCK, 2 * ATT_BLOCK), 0)
    kj = lax.broadcasted_iota(I32, (ATT_BLOCK, 2 * ATT_BLOCK), 1)
    delta = ATT_BLOCK + qi - kj
    valid = (delta >= 0) & (delta <= ATT_REACH)
    delta0 = qi - kj
    valid0 = delta0 >= 0
    for g, d in enumerate(ATT_DILATIONS):
        for j in range(2):
            slope = slopes_ref[g * HEADS_PER_GROUP + 2 * pair + j]
            slope = slope * LOG2_E
            bias_scr[g, j, 0] = jnp.where(valid0, -slope * (delta0 * d).astype(F32), MASKED)
            bias_scr[g, j, 1] = jnp.where(valid, -slope * (delta * d).astype(F32), MASKED)

    def run_group(g, q_ref, k_ref, v_ref):
        d = ATT_DILATIONS[g]
        per_res = nblk // d
        n_iter = nblk // ATT_UNROLL

        def block_index(it, k):
            n = it * ATT_UNROLL + k
            return n // per_res, n % per_res

        def key_rows(i):
            lo = jnp.maximum(i - 1, 0)
            return pl.ds(pl.multiple_of(lo * ATT_BLOCK, ATT_BLOCK), 2 * ATT_BLOCK)

        def scores(it, s_ref):
            for k in range(ATT_UNROLL):
                r, i = block_index(it, k)
                q = q_ref[0, r, pl.ds(pl.multiple_of(i * ATT_BLOCK, ATT_BLOCK), ATT_BLOCK), :]
                kw = k_ref[0, r, key_rows(i), :]
                qbits = pltpu.bitcast(q, I32)
                for j in range(2):
                    qh = pltpu.bitcast(qbits & head_bits[j], BF16)
                    s_ref[2 * k + j] = _dot_nt(qh, kw) + bias_scr[g, j, jnp.minimum(i, 1)]

        def weighted_values(it, s_ref):
            for k in range(ATT_UNROLL):
                r, i = block_index(it, k)
                vw = v_ref[0, r, key_rows(i), :]
                outs = []
                for j in range(2):
                    sc = s_ref[2 * k + j]
                    mx = jnp.max(sc, axis=-1, keepdims=True)
                    p = jnp.exp2(sc - mx)
                    den = jnp.sum(p, axis=-1, keepdims=True)
                    outs.append((_dot(p.astype(BF16), vw), mx, den))
                (n0, m0, l0), (n1, m1, l1) = outs
                if d == 1:
                    rows = pl.ds(pl.multiple_of(i * ATT_BLOCK, ATT_BLOCK), ATT_BLOCK)
                else:
                    rows = pl.ds(i * (ATT_BLOCK * d) + r, ATT_BLOCK, stride=d)
                acc[g, rows, :] = jnp.where(first_head, n0, n1)
                mst[g, rows, :] = jnp.where(first_head, m0, m1)
                lst[g, rows, :] = jnp.where(first_head, l0, l1)

        scores(0, s_even)

        def body(h, carry):
            it = 2 * h
            scores(it + 1, s_odd)
            weighted_values(it, s_even)
            scores(jnp.minimum(it + 2, n_iter - 1), s_even)
            weighted_values(it + 1, s_odd)
            return carry

        lax.fori_loop(0, n_iter // 2, body, 0)

    run_group(0, q0, k0, v0)
    run_group(1, q1, k1, v1)
    run_group(2, q2, k2, v2)

    def finish(n, carry):
        rows = pl.ds(pl.multiple_of(n * ATT_BLOCK, ATT_BLOCK), ATT_BLOCK)
        ms = [mst[g, rows, :] for g in range(3)]
        top = jnp.maximum(jnp.maximum(ms[0], ms[1]), ms[2])
        scale = [jnp.exp2(m - top) for m in ms]
        num = scale[0] * acc[0, rows, :] + scale[1] * acc[1, rows, :] + scale[2] * acc[2, rows, :]
        den = scale[0] * lst[0, rows, :] + scale[1] * lst[1, rows, :] + scale[2] * lst[2, rows, :]
        o_ref[0, rows, :] = (num / den).astype(BF16)
        return carry

    lax.fori_loop(0, nblk, finish, 0)


def _attention(qkv0, qkv1, qkv2, slopes):
    b, _, s, _ = qkv0.shape
    pairs = HEADS_PER_GROUP // 2
    col_blocks = HEADS_PER_GROUP * HEAD_DIM // LANES

    def specs(arr):
        _, d, sub, _ = arr.shape
        return [pl.BlockSpec((1, d, sub, LANES),
                             functools.partial(lambda bi, p, sec: (bi, 0, 0, sec * col_blocks + p), sec=sec))
                for sec in range(3)]

    return pl.pallas_call(
        _attn_kernel,
        grid=(b, pairs),
        in_specs=[pl.BlockSpec(memory_space=pltpu.SMEM)] + specs(qkv0) + specs(qkv1) + specs(qkv2),
        out_specs=pl.BlockSpec((1, s, LANES), lambda bi, p: (bi, 0, p)),
        out_shape=jax.ShapeDtypeStruct((b, s, ATT_OUT_WIDTH), BF16),
        scratch_shapes=[pltpu.VMEM((3, 2, 2, ATT_BLOCK, 2 * ATT_BLOCK), F32),
                        pltpu.VMEM((3, s, LANES), F32),
                        pltpu.VMEM((3, s, LANES), F32),
                        pltpu.VMEM((3, s, LANES), F32),
                        pltpu.VMEM((2 * ATT_UNROLL, ATT_BLOCK, 2 * ATT_BLOCK), F32),
                        pltpu.VMEM((2 * ATT_UNROLL, ATT_BLOCK, 2 * ATT_BLOCK), F32)],
        compiler_params=pltpu.CompilerParams(
            dimension_semantics=("arbitrary", "arbitrary"), vmem_limit_bytes=VMEM_LIMIT),
        name="attention",
    )(slopes, qkv0, qkv0, qkv0, qkv1, qkv1, qkv1, qkv2, qkv2, qkv2)


def _mixtail_kernel(u_ref, halo_ref, att_ref, gate_ref, x_ref,
                    gt_m_ref, sc_f_ref, sh_f_ref, g_post_ref, g_pre_ref,
                    wpg_ref, pscale_ref, wbp_ref, wba_ref, wout_ref, wr_ref, br_ref,
                    xmid_ref, route_ref, meta_ref, counts_ref, xs_hbm,
                    pu, lv, xbuf, zbuf, fill, meta_s, sem, sem_s):
    i = pl.program_id(1)
    tm = x_ref.shape[1]
    step = pl.program_id(0) * pl.num_programs(1) + i
    last = pl.num_programs(0) * pl.num_programs(1) - 1
    slot = step % 2
    region = _region_rows(pl.num_programs(0) * pl.num_programs(1) * tm)

    @pl.when(step == 0)
    def _():
        fill[...] = jnp.zeros_like(fill)
        pu[0:POOL_PAD - POOL_HALO, :] = jnp.zeros((POOL_PAD - POOL_HALO, POOL_WIDTH), F32)
        lv[:, :, 0:POOL_PAD - POOL_HALO, :] = jnp.zeros(
            (SORT_GROUPS, 2, POOL_PAD - POOL_HALO, POOL_GROUP_DIM), F32)
        for sub in range(SORT_GROUPS):
            for e in range(N_EXPERTS):
                meta_s[SORT_GROUPS + sub, 1, e] = 0
                meta_s[SORT_GROUPS + sub, 0, e] = 0

    def send_tile(which):
        for sub in range(SORT_GROUPS):
            m = which * SORT_GROUPS + sub
            _tile_run_copies(lambda e: meta_s[m, 1, e], xbuf.at[which, sub], None,
                             xs_hbm, lambda e: e * region + meta_s[m, 0, e], sem.at[which])

    def wait_tile(which):
        for sub in range(SORT_GROUPS):
            pltpu.make_async_copy(xbuf.at[which, sub], xs_hbm.at[pl.ds(0, SORT_ROWS * ROW_WORDS), :],
                                  sem.at[which]).wait()

    @pl.when(step > 1)
    def _():
        wait_tile(slot)

    send_tile(1 - slot)

    head = POOL_PAD - POOL_HALO
    halo = halo_ref[0].astype(F32)
    pu[head:POOL_PAD, :] = jnp.where(i > 0, halo, jnp.zeros_like(halo))
    pu[POOL_PAD:POOL_PAD + tm, :] = u_ref[0].astype(F32)

    ts = SORT_TOKENS

    def pool(sub, st):
        base = sub * ts
        t = i * tm + base + lax.broadcasted_iota(I32, (ts, 1), 0)
        pooled_groups = []
        for g, w in enumerate(POOL_WINDOWS):
            cols = slice(g * POOL_GROUP_DIM, (g + 1) * POOL_GROUP_DIM)
            read = lambda start, n: pu[pl.ds(base + start, n), cols]
            shift, level = 1, 0
            while 2 * shift < w:
                partial = read(head, ts + POOL_HALO) + read(head - shift, ts + POOL_HALO)
                buf = lv.at[sub, level % 2]
                buf[pl.ds(head, ts + POOL_HALO), :] = partial
                read = lambda start, n, buf=buf: buf[pl.ds(start, n), :]
                shift, level = 2 * shift, level + 1
            win = read(POOL_PAD, ts) + read(POOL_PAD - shift, ts)
            count = jnp.minimum(t + 1, w).astype(F32)
            pooled_groups.append((win / count - pu[pl.ds(base + POOL_PAD, ts), cols]).astype(BF16))
        st["pooled"] = jnp.concatenate(pooled_groups, axis=1)

    neg_inf = -jnp.inf
    far = float(LANES)
    before = (lax.broadcasted_iota(I32, (ts, ts), 0) < lax.broadcasted_iota(I32, (ts, ts), 1))
    earlier = jnp.where(before, 1.0, 0.0).astype(BF16)
    erow = lax.broadcasted_iota(I32, (N_EXPERTS, ts), 0).astype(F32)
    e_col = lax.broadcasted_iota(I32, (N_EXPERTS, LANES), 0)
    e_lane = lax.broadcasted_iota(I32, (N_EXPERTS, LANES), 1)
    row8 = lax.broadcasted_iota(I32, (SUBLANES, ts), 0)
    mrow = lax.broadcasted_iota(I32, (SUBLANES, LANES), 0)
    srow = lax.broadcasted_iota(I32, (SORT_ROWS, ts), 0).astype(F32)
    filled = fill[...]

    def rows_of(sub):
        return slice(sub * ts, (sub + 1) * ts)

    def branch_projections(sub, st):
        rows = rows_of(sub)
        mixed = _dot(st.pop("pooled"), wpg_ref[...]) * pscale_ref[...]
        st["y_pool"] = _dot(mixed.astype(BF16), wbp_ref[...])
        st["y_att"] = _dot(att_ref[0, rows, :], wba_ref[...])

    def gated_sum(sub, st):
        rows = rows_of(sub)
        st["merged"] = (gate_ref[0, rows, 0:D_MODEL] * st.pop("y_pool").astype(BF16)
                        + gate_ref[0, rows, D_MODEL:2 * D_MODEL] * st.pop("y_att").astype(BF16))

    def output_projection(sub, st):
        st["y"] = _dot(st.pop("merged"), wout_ref[...])

    def residual_and_ffn_input(sub, st):
        rows = rows_of(sub)
        x_mid = x_ref[0, rows, :] + gt_m_ref[0] * _rmsnorm(st.pop("y"), g_post_ref[...])
        xmid_ref[0, rows, :] = x_mid
        h2 = _rmsnorm(x_mid, g_pre_ref[...]) * (1.0 + sc_f_ref[0]) + sh_f_ref[0]
        st["h2b"] = h2.astype(BF16)
        st["h2lo"] = (h2 - st["h2b"].astype(F32)).astype(BF16)

    def router_logits(sub, st):
        by_hi = _dot_nt(wr_ref[...], st["h2b"])
        st["logits"] = (by_hi[0:ROUTER_ROWS] + (by_hi[ROUTER_ROWS:2 * ROUTER_ROWS]
                        + _dot_nt(wr_ref[0:ROUTER_ROWS, :], st.pop("h2lo"))) + br_ref[...])

    def route(sub, st):
        logits = st.pop("logits")
        gl = logits[N_EXPERTS:N_EXPERTS + N_EXPERT_GROUPS, :]
        grow = lax.broadcasted_iota(I32, gl.shape, 0).astype(F32)
        gmax = jnp.max(gl, axis=0, keepdims=True)
        gsel = jnp.min(jnp.where(gl == gmax, grow, far), axis=0, keepdims=True)
        p_group = 1.0 / jnp.sum(jnp.exp(gl - gmax), axis=0, keepdims=True)
        e_lo = gsel * float(EXPERTS_PER_GROUP)
        el = jnp.where((erow >= e_lo) & (erow < e_lo + float(EXPERTS_PER_GROUP)),
                       logits[0:N_EXPERTS, :], neg_inf)
        v1 = jnp.max(el, axis=0, keepdims=True)
        i1 = jnp.min(jnp.where(el == v1, erow, far), axis=0, keepdims=True)
        el2 = jnp.where(erow == i1, neg_inf, el)
        v2 = jnp.max(el2, axis=0, keepdims=True)
        i2 = jnp.min(jnp.where(el2 == v2, erow, far), axis=0, keepdims=True)
        e21 = jnp.exp(v2 - v1)
        st["w1"] = p_group / (1.0 + e21)
        st["w2"] = p_group * e21 / (1.0 + e21)
        st["pick1"] = erow == i1
        st["pick2"] = erow == i2
        st["assign"] = jnp.where(st["pick1"] | st["pick2"], 1.0, 0.0)

    def count_and_rank(sub, st):
        assign = st.pop("assign")
        st["rank"] = _dot(assign.astype(BF16), earlier)
        assign_pad = jnp.concatenate([assign, jnp.zeros((LANES - N_EXPERTS, ts), F32)], axis=0).astype(BF16)
        st["cnt_row"] = _dot_nt(jnp.ones((SUBLANES, ts), BF16), assign_pad)

    def sorted_positions(sub, st):
        cnt_row = st["cnt_row"]
        run_start = jnp.sum(jnp.where(e_lane < e_col, cnt_row[0:1, :], 0.0), axis=1, keepdims=True)
        pos = st.pop("rank") + run_start
        key1 = jnp.sum(jnp.where(st.pop("pick1"), pos, 0.0), axis=0, keepdims=True)
        key2 = jnp.sum(jnp.where(st.pop("pick2"), pos, 0.0), axis=0, keepdims=True)
        route_ref[:, rows_of(sub)] = jnp.where(row8 == 0, key1, jnp.where(row8 == 1, key2,
                                               jnp.where(row8 == 2, st.pop("w1"),
                                                         jnp.where(row8 == 3, st.pop("w2"), 0.0))))
        st["perm"] = jnp.where((srow == key1) | (srow == key2), 1.0, 0.0).astype(BF16)

    def sort_rows(sub, st):
        perm, h2b = st.pop("perm"), st.pop("h2b")
        for w in range(ROW_WORDS):
            pair = _dot(perm, h2b[:, 2 * w * LANES:(2 * w + 2) * LANES])
            _pack_rows(xbuf.at[slot, sub], w, SORT_ROWS, pair[:, :LANES], pair[:, LANES:])

    state = _run_staggered((pool, branch_projections, gated_sum, output_projection,
                            residual_and_ffn_input, router_logits, route, count_and_rank,
                            sorted_positions, sort_rows), SORT_GROUPS, lag=0)

    for sub in range(SORT_GROUPS):
        cnt_row = state[sub]["cnt_row"]
        meta_ref[sub] = jnp.where(mrow == 0, filled, jnp.where(mrow == 1, cnt_row, 0.0)).astype(I32)
        filled = filled + cnt_row

    fill[...] = filled
    counts_ref[...] = filled.astype(I32)

    meta_copy = pltpu.make_async_copy(meta_ref, meta_s.at[pl.ds(slot * SORT_GROUPS, SORT_GROUPS)], sem_s)
    meta_copy.start()
    meta_copy.wait()

    @pl.when(step == last)
    def _():
        send_tile(slot)

        @pl.when(step > 0)
        def _():
            wait_tile(1 - slot)

        wait_tile(slot)
        zbuf[...] = jnp.zeros_like(zbuf)
        final = slot * SORT_GROUPS + SORT_GROUPS - 1

        def pad_copy(e):
            end = e * region + meta_s[final, 0, e] + meta_s[final, 1, e]
            return pltpu.make_async_copy(
                zbuf, xs_hbm.at[pl.ds(end * ROW_WORDS, MOE_BLOCK * ROW_WORDS), :], sem_s)

        def start_pad(e, carry):
            pad_copy(e).start()
            return carry

        def wait_pad(e, carry):
            pad_copy(e).wait()
            return carry

        lax.fori_loop(0, N_EXPERTS, start_pad, 0)
        lax.fori_loop(0, N_EXPERTS, wait_pad, 0)


def _mixtail(u, att, gates, x, gt_m, sc_f, sh_f, g_post, g_pre,
             wpg, pscale, wbp, wba, wout, wr, br):
    b, s, d = x.shape
    tm = ROW_TILE
    tiles = s // tm
    n_tiles = b * tiles
    halo_blocks = tm // POOL_HALO
    region = _region_rows(b * s)
    const2 = lambda bi, i: (0, 0)
    const3 = lambda bi, i: (0, 0, 0)
    per_b = lambda bi, i: (bi, 0, 0)
    tile = lambda bi, i: (bi, i, 0)
    single = dict(pipeline_mode=pl.Buffered(1))
    return pl.pallas_call(
        _mixtail_kernel,
        grid=(b, tiles),
        in_specs=[pl.BlockSpec((1, tm, POOL_WIDTH), tile),
                  pl.BlockSpec((1, POOL_HALO, POOL_WIDTH),
                               lambda bi, i: (bi, jnp.maximum(i * halo_blocks - 1, 0), 0)),
                  pl.BlockSpec((1, tm, ATT_OUT_WIDTH), tile),
                  pl.BlockSpec((1, tm, 2 * D_MODEL), tile),
                  pl.BlockSpec((1, tm, d), tile),
                  pl.BlockSpec((1, 1, d), per_b),
                  pl.BlockSpec((1, 1, d), per_b),
                  pl.BlockSpec((1, 1, d), per_b),
                  pl.BlockSpec((1, d), const2),
                  pl.BlockSpec((1, d), const2),
                  pl.BlockSpec(wpg.shape, const2, **single),
                  pl.BlockSpec((1, POOL_WIDTH), const2),
                  pl.BlockSpec(wbp.shape, const2, **single),
                  pl.BlockSpec(wba.shape, const2, **single),
                  pl.BlockSpec(wout.shape, const2, **single),
                  pl.BlockSpec(wr.shape, const2, **single),
                  pl.BlockSpec(br.shape, const2)],
        out_specs=[pl.BlockSpec((1, tm, d), tile),
                   pl.BlockSpec((SUBLANES, tm), lambda bi, i: (0, bi * tiles + i)),
                   pl.BlockSpec((SORT_GROUPS, SUBLANES, LANES), lambda bi, i: (bi * tiles + i, 0, 0)),
                   pl.BlockSpec((SUBLANES, LANES), const2),
                   pl.BlockSpec(memory_space=pl.ANY)],
        out_shape=[jax.ShapeDtypeStruct((b, s, d), F32),
                   jax.ShapeDtypeStruct((SUBLANES, b * s), F32),
                   jax.ShapeDtypeStruct((n_tiles * SORT_GROUPS, SUBLANES, LANES), I32),
                   jax.ShapeDtypeStruct((SUBLANES, LANES), I32),
                   jax.ShapeDtypeStruct((N_EXPERTS * region * ROW_WORDS, LANES), I32)],
        scratch_shapes=[pltpu.VMEM((POOL_PAD + tm, POOL_WIDTH), F32),
                        pltpu.VMEM((SORT_GROUPS, 2, POOL_PAD + SORT_TOKENS, POOL_GROUP_DIM), F32),
                        pltpu.VMEM((2, SORT_GROUPS, SORT_ROWS * ROW_WORDS, LANES), I32),
                        pltpu.VMEM((MOE_BLOCK * ROW_WORDS, LANES), I32),
                        pltpu.VMEM((SUBLANES, LANES), F32),
                        pltpu.SMEM((2 * SORT_GROUPS, SUBLANES, LANES), I32),
                        pltpu.SemaphoreType.DMA((2,)),
                        pltpu.SemaphoreType.DMA(())],
        compiler_params=pltpu.CompilerParams(
            dimension_semantics=("arbitrary", "arbitrary"), vmem_limit_bytes=VMEM_LIMIT),
        name="mixtail",
    )(u, u, att, gates, x, gt_m, sc_f, sh_f, g_post, g_pre,
      wpg, pscale, wbp, wba, wout, wr, br)


def _expert_kernel(counts_ref, xs_hbm, wg_ref, wu_ref, wd_ref, ys_hbm,
                   wg_bf, wu_bf, wd_bf, xbuf, ybuf, state, semx, semy):
    e = pl.program_id(0)
    bm = MOE_BLOCK
    nx, ny = EXPERT_X_BUFFERS, EXPERT_Y_BUFFERS
    block_words = bm * ROW_WORDS
    region = xs_hbm.shape[0] // (N_EXPERTS * ROW_WORDS)

    def n_blocks(ex):
        return (counts_ref[ex] + (bm - 1)) // bm

    def block_rows(ref, ex, k):
        start = pl.multiple_of((ex * region + k * bm) * ROW_WORDS, block_words)
        return ref.at[pl.ds(start, block_words), :]

    def x_copy(ex, k, s):
        return pltpu.make_async_copy(block_rows(xs_hbm, ex, k), xbuf.at[s], semx.at[s])

    def y_copy(k, s):
        return pltpu.make_async_copy(ybuf.at[s], block_rows(ys_hbm, e, k), semy.at[s])

    @pl.when(e == 0)
    def _():
        for j in range(4):
            state[j] = 0

    def fetch_through(target):
        def more(c):
            pe, _, pg = c
            return (pg < target) & (pe < N_EXPERTS)

        def step(c):
            pe, pk, pg = c
            has = pk < n_blocks(pe)

            @pl.when(has)
            def _():
                x_copy(pe, pk, pg % nx).start(priority=BLOCK_DMA_PRIORITY)

            return (jnp.where(has, pe, pe + 1), jnp.where(has, pk + 1, 0), pg + has.astype(I32))

        pe, pk, pg = lax.while_loop(more, step, (state[1], state[2], state[3]))
        state[1] = pe
        state[2] = pk
        state[3] = pg

    wg_bf[...] = wg_ref[...].astype(BF16)
    wu_bf[...] = wu_ref[...].astype(BF16)
    wd_bf[...] = wd_ref[...].astype(BF16)

    nb = n_blocks(e)
    done = state[0]

    def blocks(k0, count):
        ks = [k0 + c for c in range(count)]
        gs = [done + k for k in ks]
        fetch_through(gs[0] + nx)
        for k, g in zip(ks, gs):
            x_copy(e, k, g % nx).wait()

            @pl.when(g >= ny)
            def _():
                y_copy(k, g % ny).wait()

        mids = []
        for g in gs:
            x = _unpack_rows(xbuf.at[g % nx], bm).astype(BF16)
            a = _dot(x, wg_bf[...])
            u = _dot(x, wu_bf[...])
            mids.append(((a * _sigmoid(a)) * u).astype(BF16))
        for k, g, mid in zip(ks, gs, mids):
            for w in range(ROW_WORDS):
                pair = _bf16_exact(_dot(mid, wd_bf[:, 2 * w * LANES:(2 * w + 2) * LANES]))
                _pack_rows(ybuf.at[g % ny], w, bm, pair[:, :LANES], pair[:, LANES:])
            y_copy(k, g % ny).start(priority=BLOCK_DMA_PRIORITY)

    def pair_of_blocks(p, carry):
        blocks(2 * p, 2)
        return carry

    lax.fori_loop(0, nb // 2, pair_of_blocks, 0)

    @pl.when(nb % 2 == 1)
    def _():
        blocks(nb - 1, 1)

    state[0] = done + nb

    @pl.when(e == N_EXPERTS - 1)
    def _():
        total = done + nb
        for j in range(ny):
            @pl.when(total > j)
            def _():
                y_copy(0, (total - 1 - j) % ny).wait()


def _experts(xs, counts, w_gate, w_up, w_down):
    bm = MOE_BLOCK
    w_in = pl.BlockSpec((None, D_MODEL, D_EXPERT), lambda e, cnt: (e, 0, 0))
    w_out = pl.BlockSpec((None, D_EXPERT, D_MODEL), lambda e, cnt: (e, 0, 0))
    grid_spec = pltpu.PrefetchScalarGridSpec(
        num_scalar_prefetch=1,
        grid=(N_EXPERTS,),
        in_specs=[pl.BlockSpec(memory_space=pl.ANY), w_in, w_in, w_out],
        out_specs=pl.BlockSpec(memory_space=pl.ANY),
        scratch_shapes=[pltpu.VMEM((D_MODEL, D_EXPERT), BF16),
                        pltpu.VMEM((D_MODEL, D_EXPERT), BF16),
                        pltpu.VMEM((D_EXPERT, D_MODEL), BF16),
                        pltpu.VMEM((EXPERT_X_BUFFERS, bm * ROW_WORDS, LANES), I32),
                        pltpu.VMEM((EXPERT_Y_BUFFERS, bm * ROW_WORDS, LANES), I32),
                        pltpu.SMEM((4,), I32),
                        pltpu.SemaphoreType.DMA((EXPERT_X_BUFFERS,)),
                        pltpu.SemaphoreType.DMA((EXPERT_Y_BUFFERS,))],
    )
    return pl.pallas_call(
        _expert_kernel,
        grid_spec=grid_spec,
        out_shape=jax.ShapeDtypeStruct(xs.shape, I32),
        compiler_params=pltpu.CompilerParams(
            dimension_semantics=("arbitrary",), vmem_limit_bytes=VMEM_LIMIT),
        name="experts",
    )(counts, xs, w_gate, w_up, w_down)


def _combine_kernel(meta_ref, y_hbm, route_ref, xmid_ref, gt_ref, g_ref, o_ref, ybuf, rt_scr, sem):
    step = pl.program_id(0)
    n_steps = pl.num_programs(0)
    tm = ROW_TILE
    slot = step % COMBINE_BUFFERS
    region = _region_rows(n_steps * tm)

    def fetch_tile(tile, live):
        tile = jnp.minimum(tile, n_steps - 1)
        which = tile % COMBINE_BUFFERS
        for sub in range(SORT_GROUPS):
            m = (tile * SORT_GROUPS + sub) * 2
            _tile_run_copies(lambda e: jnp.where(live, meta_ref[(m + 1) * N_EXPERTS + e], 0),
                             y_hbm, lambda e: e * region + meta_ref[m * N_EXPERTS + e],
                             ybuf.at[which, sub], None, sem.at[which])

    @pl.when(step == 0)
    def _():
        for ahead in range(COMBINE_BUFFERS - 1):
            fetch_tile(ahead, ahead < n_steps)

    for sub in range(SORT_GROUPS):
        pltpu.make_async_copy(y_hbm.at[pl.ds(0, SORT_ROWS * ROW_WORDS), :], ybuf.at[slot, sub],
                              sem.at[slot]).wait()
    sorted_rows = [_unpack_rows(ybuf.at[slot, sub], SORT_ROWS).astype(BF16) for sub in range(SORT_GROUPS)]

    fetch_tile(step + COMBINE_BUFFERS - 1, step + COMBINE_BUFFERS - 1 < n_steps)

    rt_scr[...] = jnp.zeros_like(rt_scr)
    rt_scr[0:SUBLANES, :] = route_ref[...]
    cols = jnp.concatenate([rt_scr[:, c * LANES:(c + 1) * LANES].T for c in range(tm // LANES)], axis=0)

    ts = SORT_TOKENS
    spos = lax.broadcasted_iota(I32, (ts, SORT_ROWS), 1).astype(F32)
    def rows_of(sub):
        return slice(sub * ts, (sub + 1) * ts)

    def gate_matrix(sub, st):
        key1, key2, w1, w2 = (cols[rows_of(sub), c:c + 1] for c in range(4))
        st["gates"] = jnp.where(spos == key1, w1, jnp.where(spos == key2, w2, 0.0)).astype(BF16)

    def weighted_unsort(sub, st):
        st["y"] = _dot(st.pop("gates"), sorted_rows[sub])

    def norm_and_add(sub, st):
        rows = rows_of(sub)
        o_ref[rows, :] = xmid_ref[rows, :] + gt_ref[0] * _rmsnorm(st.pop("y"), g_ref[...])

    _run_staggered((gate_matrix, weighted_unsort, norm_and_add), SORT_GROUPS, lag=1)


def _combine(meta, ys, route, x_mid, gt_f, g_post, seq):
    t, d = x_mid.shape
    tm = ROW_TILE
    tiles_per_seq = seq // tm
    grid_spec = pltpu.PrefetchScalarGridSpec(
        num_scalar_prefetch=1,
        grid=(t // tm,),
        in_specs=[pl.BlockSpec(memory_space=pl.ANY),
                  pl.BlockSpec((SUBLANES, tm), lambda i, m: (0, i)),
                  pl.BlockSpec((tm, d), lambda i, m: (i, 0)),
                  pl.BlockSpec((1, 1, d), lambda i, m: (i // tiles_per_seq, 0, 0)),
                  pl.BlockSpec((1, d), lambda i, m: (0, 0))],
        out_specs=pl.BlockSpec((tm, d), lambda i, m: (i, 0)),
        scratch_shapes=[pltpu.VMEM((COMBINE_BUFFERS, SORT_GROUPS, SORT_ROWS * ROW_WORDS, LANES), I32),
                        pltpu.VMEM((LANES, tm), F32),
                        pltpu.SemaphoreType.DMA((COMBINE_BUFFERS,))],
    )
    return pl.pallas_call(
        _combine_kernel,
        grid_spec=grid_spec,
        out_shape=jax.ShapeDtypeStruct((t, d), F32),
        compiler_params=pltpu.CompilerParams(
            dimension_semantics=("arbitrary",), vmem_limit_bytes=VMEM_LIMIT),
        name="combine",
    )(meta, ys, route, x_mid, gt_f, g_post)


def kernel(x, c, w_ada, b_ada, g_pre_mix, g_post_mix, g_pre_ffn, g_post_ffn, w_in, w_pool_group, pool_scale, w_branch_pool, w_branch_att, w_out, w_group_router, b_group_router, w_expert_router, b_expert_router, w_exp_gate, w_exp_up, w_exp_down):
    b, s, d = x.shape
    assert d == D_MODEL and s % (ATT_BLOCK * 2 * ATT_DILATIONS[2]) == 0 and s % ROW_TILE == 0
    assert (b * s) % MOE_BLOCK == 0
    depth = w_ada.shape[0]
    slopes = jnp.exp2(-ALIBI_MAX_BIAS * jnp.arange(1, N_ATT_HEADS + 1, dtype=F32) / N_ATT_HEADS)
    q_scale = HEAD_DIM ** -0.5 * LOG2_E

    for layer in range(depth):
        mod = _adaln(c, w_ada[layer], b_ada[layer]).reshape(b, 6, 1, d)
        sh_m, sc_m, gt_m, sh_f, sc_f, gt_f = [mod[:, j] for j in range(6)]

        wl = w_in[layer]
        q_lo, k_lo, v_lo = POOL_WIDTH, POOL_WIDTH + 768, POOL_WIDTH + 2 * 768
        group_cols = []
        for g in range(3):
            sl = slice(g * 256, (g + 1) * 256)
            group_cols += [wl[:, q_lo:k_lo][:, sl] * q_scale, wl[:, k_lo:v_lo][:, sl],
                           wl[:, v_lo:v_lo + 768][:, sl]]
        w_perm = jnp.concatenate([wl[:, :POOL_WIDTH]] + group_cols + [wl[:, v_lo + 768:]],
                                 axis=1).astype(BF16)

        u, qkv0, qkv1, qkv2, gates = _inproj(x, g_pre_mix[layer].reshape(1, d), sc_m, sh_m, w_perm)
        att = _attention(qkv0, qkv1, qkv2, slopes)

        zero_block = jnp.zeros((POOL_GROUP_DIM, POOL_GROUP_DIM), F32)
        wpg_diag = jnp.block([[w_pool_group[layer, g] if g == h else zero_block
                               for h in range(len(POOL_WINDOWS))] for g in range(len(POOL_WINDOWS))])

        pad_rows = ROUTER_ROWS - N_EXPERTS - N_EXPERT_GROUPS
        wrt = jnp.concatenate([w_expert_router[layer].T, w_group_router[layer].T,
                               jnp.zeros((pad_rows, d), F32)], axis=0)
        wrt_hi = wrt.astype(BF16)
        wr = jnp.concatenate([wrt_hi, (wrt - wrt_hi.astype(F32)).astype(BF16)], axis=0)
        br = jnp.concatenate([b_expert_router[layer], b_group_router[layer],
                              jnp.zeros((pad_rows,), F32)]).reshape(ROUTER_ROWS, 1)

        x_mid, route, meta, counts, xs = _mixtail(
            u, att, gates, x, gt_m, sc_f, sh_f,
            g_post_mix[layer].reshape(1, d), g_pre_ffn[layer].reshape(1, d),
            wpg_diag.astype(BF16), pool_scale[layer].reshape(1, POOL_WIDTH),
            w_branch_pool[layer].astype(BF16), w_branch_att[layer].astype(BF16),
            w_out[layer].astype(BF16), wr, br)

        ys = _experts(xs, counts[0, :N_EXPERTS], w_exp_gate[layer], w_exp_up[layer], w_exp_down[layer])
        run_meta = meta[:, 0:2, 0:N_EXPERTS].reshape(-1)
        x = _combine(run_meta, ys, route, x_mid.reshape(b * s, d), gt_f,
                     g_post_ffn[layer].reshape(1, d), s).reshape(b, s, d)
    return x
```

```python
import functools

import jax
import jax.numpy as jnp
from jax import lax
from jax.experimental import pallas as pl
from jax.experimental.pallas import tpu as pltpu

F32 = jnp.float32
BF16 = jnp.bfloat16
I32 = jnp.int32
HIGHEST = lax.Precision.HIGHEST

D_MODEL = 1024
LANES = 128
SUBLANES = 8
ROW_WORDS = D_MODEL // (2 * LANES)
HI_HALF = -65536

POOL_WINDOWS = (2, 4, 8, 16)
POOL_GROUP_DIM = 128
POOL_WIDTH = 512
POOL_HALO = 16
POOL_PAD = POOL_HALO + 8

HEAD_DIM = 64
ATT_DILATIONS = (1, 4, 16)
ATT_REACH = 128
ATT_BLOCK = 128
ATT_UNROLL = 4
HEADS_PER_GROUP = 4
N_ATT_HEADS = 12
GROUP_QKV = 3 * HEADS_PER_GROUP * HEAD_DIM
ATT_OUT_WIDTH = 256
ALIBI_MAX_BIAS = 8.0
IN_WIDTH = POOL_WIDTH + 3 * GROUP_QKV + 2 * D_MODEL
MASKED = -1e30
LOG2_E = 1.4426950408889634

N_EXPERT_GROUPS = 4
EXPERTS_PER_GROUP = 8
N_EXPERTS = 32
ROUTER_ROWS = 48
D_EXPERT = 512
RMS_EPS = 1e-6

ROW_TILE = 1024
ROW_SPLIT = 2
SORT_TOKENS = 256
SORT_GROUPS = ROW_TILE // SORT_TOKENS
SORT_ROWS = 2 * SORT_TOKENS
MOE_BLOCK = 256
EXPERT_CHAINS = 2
BLOCK_DMA_PRIORITY = 1
EXPERT_X_BUFFERS = 6
EXPERT_Y_BUFFERS = 4
COMBINE_BUFFERS = 3
VMEM_LIMIT = 52 * 1024 * 1024


def _sigmoid(x):
    return 0.5 * jnp.tanh(0.5 * x) + 0.5


def _rmsnorm(x, g):
    return x * lax.rsqrt(jnp.mean(x * x, axis=-1, keepdims=True) + RMS_EPS) * g


def _dot(a, b):
    return jnp.dot(a, b, preferred_element_type=F32)


def _dot_nt(a, b, **kw):
    return lax.dot_general(a, b, (((1,), (1,)), ((), ())), preferred_element_type=F32, **kw)


def _pack_rows(ref, w, n, lo, hi):
    word = (lax.shift_right_logical(pltpu.bitcast(lo, I32), 16) | (pltpu.bitcast(hi, I32) & HI_HALF))
    ref[pl.ds(w, n, stride=ROW_WORDS), :] = word


def _unpack_rows(ref, n):
    cols = []
    for w in range(ROW_WORDS):
        word = ref[pl.ds(w, n, stride=ROW_WORDS), :]
        cols += [pltpu.bitcast(word << 16, F32), pltpu.bitcast(word & HI_HALF, F32)]
    return jnp.concatenate(cols, axis=1)


def _bf16_exact(x):
    return x.astype(BF16).astype(F32)


def _run_staggered(stages, n_chains, lag):
    state = [{} for _ in range(n_chains)]
    for tick in range(len(stages) + lag * (n_chains - 1)):
        for chain in range(n_chains):
            if 0 <= tick - lag * chain < len(stages):
                stages[tick - lag * chain](chain, state[chain])
    return state


def _region_rows(n_tokens):
    return n_tokens + MOE_BLOCK


def _run_copies(n, src, src_row, dst, dst_row, sem):
    @pl.when(n > 0)
    def _():
        pltpu.make_async_copy(
            src.at[pl.ds(src_row * ROW_WORDS, n * ROW_WORDS), :],
            dst.at[pl.ds(dst_row * ROW_WORDS, n * ROW_WORDS), :], sem).start()


def _tile_run_copies(counts, src, src_rows, dst, dst_rows, sem):
    local = 0
    for e in range(N_EXPERTS):
        n = counts(e)
        _run_copies(n, src, local if src_rows is None else src_rows(e),
                    dst, local if dst_rows is None else dst_rows(e), sem)
        local = local + n


def _adaln_kernel(c_ref, w_ref, b_ref, o_ref):
    c = c_ref[...]
    a = c * _sigmoid(c)
    o_ref[...] = jnp.dot(a, w_ref[...], preferred_element_type=F32, precision=HIGHEST) + b_ref[...]


def _adaln(c, w_ada, b_ada):
    b, d = c.shape
    n = w_ada.shape[1]
    rows = -(-b // SUBLANES) * SUBLANES
    cp = jnp.pad(c, ((0, rows - b), (0, 0)))
    nt = 1536
    out = pl.pallas_call(
        _adaln_kernel,
        grid=(n // nt,),
        in_specs=[pl.BlockSpec((rows, d), lambda j: (0, 0)),
                  pl.BlockSpec((d, nt), lambda j: (0, j)),
                  pl.BlockSpec((1, nt), lambda j: (0, j))],
        out_specs=pl.BlockSpec((rows, nt), lambda j: (0, j)),
        out_shape=jax.ShapeDtypeStruct((rows, n), F32),
        compiler_params=pltpu.CompilerParams(vmem_limit_bytes=VMEM_LIMIT),
        name="adaln",
    )(cp, w_ada, b_ada.reshape(1, n))
    return out[:b]


def _inproj_kernel(x_ref, g_ref, sc_ref, sh_ref, w_ref,
                   u_ref, qkv0_ref, qkv1_ref, qkv2_ref, gate_ref, h_scr, p_scr):
    tm = x_ref.shape[1]
    hm = tm // ROW_SPLIT
    for part in range(ROW_SPLIT):
        rows = slice(part * hm, (part + 1) * hm)
        h = _rmsnorm(x_ref[0, rows, :], g_ref[...]) * (1.0 + sc_ref[0]) + sh_ref[0]
        h_scr[rows, :] = h.astype(BF16)
        hb = h_scr[rows, :]

        u_ref[0, rows, :] = _dot(hb, w_ref[:, 0:POOL_WIDTH]).astype(BF16)

        col = POOL_WIDTH
        qkv0_ref[0, 0, rows, :] = _dot(hb, w_ref[:, col:col + GROUP_QKV]).astype(BF16)
        for gi, (out_ref, d) in enumerate(((qkv1_ref, ATT_DILATIONS[1]), (qkv2_ref, ATT_DILATIONS[2]))):
            col += GROUP_QKV
            proj = _dot(hb, w_ref[:, col:col + GROUP_QKV])
            stage = p_scr.at[part, gi]
            for cb in range(GROUP_QKV // LANES):
                stage[cb] = proj[:, cb * LANES:(cb + 1) * LANES]
            sub = hm // d
            for r in range(d):
                out_ref[0, r, part * sub:(part + 1) * sub, :] = jnp.concatenate(
                    [stage[cb, pl.ds(r, sub, stride=d), :] for cb in range(GROUP_QKV // LANES)],
                    axis=1).astype(BF16)
        col += GROUP_QKV

        chunk = 512
        for j in range(2 * D_MODEL // chunk):
            g = _dot(hb, w_ref[:, col + j * chunk:col + (j + 1) * chunk])
            gate_ref[0, rows, j * chunk:(j + 1) * chunk] = _sigmoid(g).astype(BF16)


def _inproj(x, g_pre, sc, sh, w_perm):
    b, s, d = x.shape
    tm = ROW_TILE
    d1, d2 = ATT_DILATIONS[1], ATT_DILATIONS[2]
    grid = (b, s // tm)
    const = lambda bi, i: (0, 0)
    per_b = lambda bi, i: (bi, 0, 0)
    return pl.pallas_call(
        _inproj_kernel,
        grid=grid,
        in_specs=[pl.BlockSpec((1, tm, d), lambda bi, i: (bi, i, 0)),
                  pl.BlockSpec((1, d), const),
                  pl.BlockSpec((1, 1, d), per_b),
                  pl.BlockSpec((1, 1, d), per_b),
                  pl.BlockSpec((d, IN_WIDTH), const, pipeline_mode=pl.Buffered(1))],
        out_specs=[pl.BlockSpec((1, tm, POOL_WIDTH), lambda bi, i: (bi, i, 0)),
                   pl.BlockSpec((1, 1, tm, GROUP_QKV), lambda bi, i: (bi, 0, i, 0)),
                   pl.BlockSpec((1, d1, tm // d1, GROUP_QKV), lambda bi, i: (bi, 0, i, 0)),
                   pl.BlockSpec((1, d2, tm // d2, GROUP_QKV), lambda bi, i: (bi, 0, i, 0)),
                   pl.BlockSpec((1, tm, 2 * D_MODEL), lambda bi, i: (bi, i, 0))],
        out_shape=[jax.ShapeDtypeStruct((b, s, POOL_WIDTH), BF16),
                   jax.ShapeDtypeStruct((b, 1, s, GROUP_QKV), BF16),
                   jax.ShapeDtypeStruct((b, d1, s // d1, GROUP_QKV), BF16),
                   jax.ShapeDtypeStruct((b, d2, s // d2, GROUP_QKV), BF16),
                   jax.ShapeDtypeStruct((b, s, 2 * D_MODEL), BF16)],
        scratch_shapes=[pltpu.VMEM((tm, d), BF16),
                        pltpu.VMEM((ROW_SPLIT, 2, GROUP_QKV // LANES, tm // ROW_SPLIT, LANES), F32)],
        compiler_params=pltpu.CompilerParams(
            dimension_semantics=("arbitrary", "arbitrary"), vmem_limit_bytes=VMEM_LIMIT),
        name="inproj",
    )(x, g_pre, sc, sh, w_perm)


def _attn_kernel(slopes_ref, q0, k0, v0, q1, k1, v1, q2, k2, v2, o_ref,
                 bias_scr, acc, mst, lst, s_even, s_odd):
    pair = pl.program_id(1)
    seq = o_ref.shape[1]
    nblk = seq // ATT_BLOCK
    lane = lax.broadcasted_iota(I32, (ATT_BLOCK, LANES), 1)
    first_head = lane < HEAD_DIM
    half_lane = lax.broadcasted_iota(I32, (ATT_BLOCK // 2, LANES), 1)
    head_bits = (jnp.where(half_lane < HEAD_DIM, -1, 0), jnp.where(half_lane < HEAD_DIM, 0, -1))
    ones_block = jnp.ones((2 * ATT_BLOCK, LANES), BF16)

    qi = lax.broadcasted_iota(I32, (ATT_BLOCK, 2 * ATT_BLOCK), 0)
    kj = lax.broadcasted_iota(I32, (ATT_BLOCK, 2 * ATT_BLOCK), 1)
    delta = ATT_BLOCK + qi - kj
    valid = (delta >= 0) & (delta <= ATT_REACH)
    delta0 = qi - kj
    valid0 = delta0 >= 0
    for g, d in enumerate(ATT_DILATIONS):
        for j in range(2):
            slope = slopes_ref[g * HEADS_PER_GROUP + 2 * pair + j]
            slope = slope * LOG2_E
            bias_scr[g, j, 0] = jnp.where(valid0, -slope * (delta0 * d).astype(F32), MASKED)
            bias_scr[g, j, 1] = jnp.where(valid, -slope * (delta * d).astype(F32), MASKED)

    def run_group(g, q_ref, k_ref, v_ref):
        d = ATT_DILATIONS[g]
        per_res = nblk // d
        n_iter = nblk // ATT_UNROLL

        def block_index(it, k):
            n = it * ATT_UNROLL + k
            return n // per_res, n % per_res

        def key_rows(i):
            lo = jnp.maximum(i - 1, 0)
            return pl.ds(pl.multiple_of(lo * ATT_BLOCK, ATT_BLOCK), 2 * ATT_BLOCK)

        def scores(it, s_ref):
            for k in range(ATT_UNROLL):
                r, i = block_index(it, k)
                q = q_ref[0, r, pl.ds(pl.multiple_of(i * ATT_BLOCK, ATT_BLOCK), ATT_BLOCK), :]
                kw = k_ref[0, r, key_rows(i), :]
                qbits = pltpu.bitcast(q, I32)
                for j in range(2):
                    qh = pltpu.bitcast(qbits & head_bits[j], BF16)
                    s_ref[2 * k + j] = _dot_nt(qh, kw) + bias_scr[g, j, jnp.minimum(i, 1)]

        def weighted_values(it, s_ref):
            for k in range(ATT_UNROLL):
                r, i = block_index(it, k)
                vw = jnp.concatenate([v_ref[0, r, key_rows(i), :], ones_block], axis=1)
                outs = []
                for j in range(2):
                    sc = s_ref[2 * k + j]
                    mx = jnp.max(sc, axis=-1, keepdims=True)
                    p = jnp.exp2(sc - mx)
                    num_den = _dot(p.astype(BF16), vw)
                    outs.append((num_den[:, :LANES], mx, num_den[:, LANES:]))
                (n0, m0, l0), (n1, m1, l1) = outs
                if d == 1:
                    rows = pl.ds(pl.multiple_of(i * ATT_BLOCK, ATT_BLOCK), ATT_BLOCK)
                else:
                    rows = pl.ds(i * (ATT_BLOCK * d) + r, ATT_BLOCK, stride=d)
                acc[g, rows, :] = jnp.where(first_head, n0, n1)
                mst[g, rows, :] = jnp.where(first_head, m0, m1)
                lst[g, rows, :] = jnp.where(first_head, l0, l1)

        scores(0, s_even)

        def body(h, carry):
            it = 2 * h
            scores(it + 1, s_odd)
            weighted_values(it, s_even)
            scores(jnp.minimum(it + 2, n_iter - 1), s_even)
            weighted_values(it + 1, s_odd)
            return carry

        lax.fori_loop(0, n_iter // 2, body, 0)

    run_group(0, q0, k0, v0)
    run_group(1, q1, k1, v1)
    run_group(2, q2, k2, v2)

    def finish(n, carry):
        rows = pl.ds(pl.multiple_of(n * ATT_BLOCK, ATT_BLOCK), ATT_BLOCK)
        ms = [mst[g, rows, :] for g in range(3)]
        top = jnp.maximum(jnp.maximum(ms[0], ms[1]), ms[2])
        scale = [jnp.exp2(m - top) for m in ms]
        num = scale[0] * acc[0, rows, :] + scale[1] * acc[1, rows, :] + scale[2] * acc[2, rows, :]
        den = scale[0] * lst[0, rows, :] + scale[1] * lst[1, rows, :] + scale[2] * lst[2, rows, :]
        o_ref[0, rows, :] = (num / den).astype(BF16)
        return carry

    lax.fori_loop(0, nblk, finish, 0)


def _attention(qkv0, qkv1, qkv2, slopes):
    b, _, s, _ = qkv0.shape
    pairs = HEADS_PER_GROUP // 2
    col_blocks = HEADS_PER_GROUP * HEAD_DIM // LANES

    def specs(arr):
        _, d, sub, _ = arr.shape
        return [pl.BlockSpec((1, d, sub, LANES),
                             functools.partial(lambda bi, p, sec: (bi, 0, 0, sec * col_blocks + p), sec=sec))
                for sec in range(3)]

    return pl.pallas_call(
        _attn_kernel,
        grid=(b, pairs),
        in_specs=[pl.BlockSpec(memory_space=pltpu.SMEM)] + specs(qkv0) + specs(qkv1) + specs(qkv2),
        out_specs=pl.BlockSpec((1, s, LANES), lambda bi, p: (bi, 0, p)),
        out_shape=jax.ShapeDtypeStruct((b, s, ATT_OUT_WIDTH), BF16),
        scratch_shapes=[pltpu.VMEM((3, 2, 2, ATT_BLOCK, 2 * ATT_BLOCK), F32),
                        pltpu.VMEM((3, s, LANES), F32),
                        pltpu.VMEM((3, s, LANES), F32),
                        pltpu.VMEM((3, s, LANES), F32),
                        pltpu.VMEM((2 * ATT_UNROLL, ATT_BLOCK, 2 * ATT_BLOCK), F32),
                        pltpu.VMEM((2 * ATT_UNROLL, ATT_BLOCK, 2 * ATT_BLOCK), F32)],
        compiler_params=pltpu.CompilerParams(
            dimension_semantics=("arbitrary", "arbitrary"), vmem_limit_bytes=VMEM_LIMIT),
        name="attention",
    )(slopes, qkv0, qkv0, qkv0, qkv1, qkv1, qkv1, qkv2, qkv2, qkv2)


def _mixtail_kernel(u_ref, halo_ref, att_ref, gate_ref, x_ref,
                    gt_m_ref, sc_f_ref, sh_f_ref, g_post_ref, g_pre_ref,
                    wpg_ref, pscale_ref, wbp_ref, wba_ref, wout_ref, wr_ref, br_ref,
                    xmid_ref, route_ref, meta_ref, counts_ref, xs_hbm,
                    pu, lv, xbuf, zbuf, fill, meta_s, sem, sem_s):
    i = pl.program_id(1)
    tm = x_ref.shape[1]
    step = pl.program_id(0) * pl.num_programs(1) + i
    last = pl.num_programs(0) * pl.num_programs(1) - 1
    slot = step % 2
    region = _region_rows(pl.num_programs(0) * pl.num_programs(1) * tm)

    @pl.when(step == 0)
    def _():
        fill[...] = jnp.zeros_like(fill)
        pu[0:POOL_PAD - POOL_HALO, :] = jnp.zeros((POOL_PAD - POOL_HALO, POOL_WIDTH), F32)
        lv[:, :, 0:POOL_PAD - POOL_HALO, :] = jnp.zeros(
            (SORT_GROUPS, 2, POOL_PAD - POOL_HALO, POOL_GROUP_DIM), F32)
        for sub in range(SORT_GROUPS):
            for e in range(N_EXPERTS):
                meta_s[SORT_GROUPS + sub, 1, e] = 0
                meta_s[SORT_GROUPS + sub, 0, e] = 0

    def send_tile(which):
        for sub in range(SORT_GROUPS):
            m = which * SORT_GROUPS + sub
            _tile_run_copies(lambda e: meta_s[m, 1, e], xbuf.at[which, sub], None,
                             xs_hbm, lambda e: e * region + meta_s[m, 0, e], sem.at[which])

    def wait_tile(which):
        for sub in range(SORT_GROUPS):
            pltpu.make_async_copy(xbuf.at[which, sub], xs_hbm.at[pl.ds(0, SORT_ROWS * ROW_WORDS), :],
                                  sem.at[which]).wait()

    @pl.when(step > 1)
    def _():
        wait_tile(slot)

    send_tile(1 - slot)

    head = POOL_PAD - POOL_HALO
    halo = halo_ref[0].astype(F32)
    pu[head:POOL_PAD, :] = jnp.where(i > 0, halo, jnp.zeros_like(halo))
    pu[POOL_PAD:POOL_PAD + tm, :] = u_ref[0].astype(F32)

    ts = SORT_TOKENS

    def pool(sub, st):
        base = sub * ts
        t = i * tm + base + lax.broadcasted_iota(I32, (ts, 1), 0)
        pooled_groups = []
        for g, w in enumerate(POOL_WINDOWS):
            cols = slice(g * POOL_GROUP_DIM, (g + 1) * POOL_GROUP_DIM)
            read = lambda start, n: pu[pl.ds(base + start, n), cols]
            shift, level = 1, 0
            while 2 * shift < w:
                partial = read(head, ts + POOL_HALO) + read(head - shift, ts + POOL_HALO)
                buf = lv.at[sub, level % 2]
                buf[pl.ds(head, ts + POOL_HALO), :] = partial
                read = lambda start, n, buf=buf: buf[pl.ds(start, n), :]
                shift, level = 2 * shift, level + 1
            win = read(POOL_PAD, ts) + read(POOL_PAD - shift, ts)
            count = jnp.minimum(t + 1, w).astype(F32)
            pooled_groups.append((win / count - pu[pl.ds(base + POOL_PAD, ts), cols]).astype(BF16))
        st["pooled"] = jnp.concatenate(pooled_groups, axis=1)

    neg_inf = -jnp.inf
    far = float(LANES)
    before = (lax.broadcasted_iota(I32, (ts, ts), 0) < lax.broadcasted_iota(I32, (ts, ts), 1))
    earlier = jnp.where(before, 1.0, 0.0).astype(BF16)
    erow = lax.broadcasted_iota(I32, (N_EXPERTS, ts), 0).astype(F32)
    e_col = lax.broadcasted_iota(I32, (N_EXPERTS, LANES), 0)
    e_lane = lax.broadcasted_iota(I32, (N_EXPERTS, LANES), 1)
    row8 = lax.broadcasted_iota(I32, (SUBLANES, ts), 0)
    mrow = lax.broadcasted_iota(I32, (SUBLANES, LANES), 0)
    srow = lax.broadcasted_iota(I32, (SORT_ROWS, ts), 0).astype(F32)
    filled = fill[...]

    def rows_of(sub):
        return slice(sub * ts, (sub + 1) * ts)

    def branch_projections(sub, st):
        rows = rows_of(sub)
        mixed = _dot(st.pop("pooled"), wpg_ref[...]) * pscale_ref[...]
        st["y_pool"] = _dot(mixed.astype(BF16), wbp_ref[...])
        st["y_att"] = _dot(att_ref[0, rows, :], wba_ref[...])

    def gated_sum(sub, st):
        rows = rows_of(sub)
        st["merged"] = (gate_ref[0, rows, 0:D_MODEL] * st.pop("y_pool").astype(BF16)
                        + gate_ref[0, rows, D_MODEL:2 * D_MODEL] * st.pop("y_att").astype(BF16))

    def output_projection(sub, st):
        st["y"] = _dot(st.pop("merged"), wout_ref[...])

    def residual_and_ffn_input(sub, st):
        rows = rows_of(sub)
        x_mid = x_ref[0, rows, :] + gt_m_ref[0] * _rmsnorm(st.pop("y"), g_post_ref[...])
        xmid_ref[0, rows, :] = x_mid
        h2 = _rmsnorm(x_mid, g_pre_ref[...]) * (1.0 + sc_f_ref[0]) + sh_f_ref[0]
        st["h2b"] = h2.astype(BF16)
        st["h2lo"] = (h2 - st["h2b"].astype(F32)).astype(BF16)

    def router_logits(sub, st):
        by_hi = _dot_nt(wr_ref[...], st["h2b"])
        st["logits"] = (by_hi[0:ROUTER_ROWS] + (by_hi[ROUTER_ROWS:2 * ROUTER_ROWS]
                        + _dot_nt(wr_ref[0:ROUTER_ROWS, :], st.pop("h2lo"))) + br_ref[...])

    def route(sub, st):
        logits = st.pop("logits")
        gl = logits[N_EXPERTS:N_EXPERTS + N_EXPERT_GROUPS, :]
        grow = lax.broadcasted_iota(I32, gl.shape, 0).astype(F32)
        gmax = jnp.max(gl, axis=0, keepdims=True)
        gsel = jnp.min(jnp.where(gl == gmax, grow, far), axis=0, keepdims=True)
        p_group = 1.0 / jnp.sum(jnp.exp(gl - gmax), axis=0, keepdims=True)
        e_lo = gsel * float(EXPERTS_PER_GROUP)
        el = jnp.where((erow >= e_lo) & (erow < e_lo + float(EXPERTS_PER_GROUP)),
                       logits[0:N_EXPERTS, :], neg_inf)
        v1 = jnp.max(el, axis=0, keepdims=True)
        i1 = jnp.min(jnp.where(el == v1, erow, far), axis=0, keepdims=True)
        el2 = jnp.where(erow == i1, neg_inf, el)
        v2 = jnp.max(el2, axis=0, keepdims=True)
        i2 = jnp.min(jnp.where(el2 == v2, erow, far), axis=0, keepdims=True)
        e21 = jnp.exp(v2 - v1)
        st["w1"] = p_group / (1.0 + e21)
        st["w2"] = p_group * e21 / (1.0 + e21)
        st["pick1"] = erow == i1
        st["pick2"] = erow == i2
        st["assign"] = jnp.where(st["pick1"] | st["pick2"], 1.0, 0.0)

    def count_and_rank(sub, st):
        assign = st.pop("assign")
        st["rank"] = _dot(assign.astype(BF16), earlier)
        assign_pad = jnp.concatenate([assign, jnp.zeros((LANES - N_EXPERTS, ts), F32)], axis=0).astype(BF16)
        st["cnt_row"] = _dot_nt(jnp.ones((SUBLANES, ts), BF16), assign_pad)

    def sorted_positions(sub, st):
        cnt_row = st["cnt_row"]
        run_start = jnp.sum(jnp.where(e_lane < e_col, cnt_row[0:1, :], 0.0), axis=1, keepdims=True)
        pos = st.pop("rank") + run_start
        key1 = jnp.sum(jnp.where(st.pop("pick1"), pos, 0.0), axis=0, keepdims=True)
        key2 = jnp.sum(jnp.where(st.pop("pick2"), pos, 0.0), axis=0, keepdims=True)
        route_ref[:, rows_of(sub)] = jnp.where(row8 == 0, key1, jnp.where(row8 == 1, key2,
                                               jnp.where(row8 == 2, st.pop("w1"),
                                                         jnp.where(row8 == 3, st.pop("w2"), 0.0))))
        st["perm"] = jnp.where((srow == key1) | (srow == key2), 1.0, 0.0).astype(BF16)

    def sort_rows(sub, st):
        perm, h2b = st.pop("perm"), st.pop("h2b")
        for w in range(ROW_WORDS):
            pair = _dot(perm, h2b[:, 2 * w * LANES:(2 * w + 2) * LANES])
            _pack_rows(xbuf.at[slot, sub], w, SORT_ROWS, pair[:, :LANES], pair[:, LANES:])

    state = _run_staggered((pool, branch_projections, gated_sum, output_projection,
                            residual_and_ffn_input, router_logits, route, count_and_rank,
                            sorted_positions, sort_rows), SORT_GROUPS, lag=0)

    for sub in range(SORT_GROUPS):
        cnt_row = state[sub]["cnt_row"]
        meta_ref[sub] = jnp.where(mrow == 0, filled, jnp.where(mrow == 1, cnt_row, 0.0)).astype(I32)
        filled = filled + cnt_row

    fill[...] = filled
    counts_ref[...] = filled.astype(I32)

    meta_copy = pltpu.make_async_copy(meta_ref, meta_s.at[pl.ds(slot * SORT_GROUPS, SORT_GROUPS)], sem_s)
    meta_copy.start()
    meta_copy.wait()

    @pl.when(step == last)
    def _():
        send_tile(slot)

        @pl.when(step > 0)
        def _():
            wait_tile(1 - slot)

        wait_tile(slot)
        zbuf[...] = jnp.zeros_like(zbuf)
        final = slot * SORT_GROUPS + SORT_GROUPS - 1

        def pad_copy(e):
            end = e * region + meta_s[final, 0, e] + meta_s[final, 1, e]
            return pltpu.make_async_copy(
                zbuf, xs_hbm.at[pl.ds(end * ROW_WORDS, MOE_BLOCK * ROW_WORDS), :], sem_s)

        def start_pad(e, carry):
            pad_copy(e).start()
            return carry

        def wait_pad(e, carry):
            pad_copy(e).wait()
            return carry

        lax.fori_loop(0, N_EXPERTS, start_pad, 0)
        lax.fori_loop(0, N_EXPERTS, wait_pad, 0)


def _mixtail(u, att, gates, x, gt_m, sc_f, sh_f, g_post, g_pre,
             wpg, pscale, wbp, wba, wout, wr, br):
    b, s, d = x.shape
    tm = ROW_TILE
    tiles = s // tm
    n_tiles = b * tiles
    halo_blocks = tm // POOL_HALO
    region = _region_rows(b * s)
    const2 = lambda bi, i: (0, 0)
    const3 = lambda bi, i: (0, 0, 0)
    per_b = lambda bi, i: (bi, 0, 0)
    tile = lambda bi, i: (bi, i, 0)
    single = dict(pipeline_mode=pl.Buffered(1))
    return pl.pallas_call(
        _mixtail_kernel,
        grid=(b, tiles),
        in_specs=[pl.BlockSpec((1, tm, POOL_WIDTH), tile),
                  pl.BlockSpec((1, POOL_HALO, POOL_WIDTH),
                               lambda bi, i: (bi, jnp.maximum(i * halo_blocks - 1, 0), 0)),
                  pl.BlockSpec((1, tm, ATT_OUT_WIDTH), tile),
                  pl.BlockSpec((1, tm, 2 * D_MODEL), tile),
                  pl.BlockSpec((1, tm, d), tile),
                  pl.BlockSpec((1, 1, d), per_b),
                  pl.BlockSpec((1, 1, d), per_b),
                  pl.BlockSpec((1, 1, d), per_b),
                  pl.BlockSpec((1, d), const2),
                  pl.BlockSpec((1, d), const2),
                  pl.BlockSpec(wpg.shape, const2, **single),
                  pl.BlockSpec((1, POOL_WIDTH), const2),
                  pl.BlockSpec(wbp.shape, const2, **single),
                  pl.BlockSpec(wba.shape, const2, **single),
                  pl.BlockSpec(wout.shape, const2, **single),
                  pl.BlockSpec(wr.shape, const2, **single),
                  pl.BlockSpec(br.shape, const2)],
        out_specs=[pl.BlockSpec((1, tm, d), tile),
                   pl.BlockSpec((SUBLANES, tm), lambda bi, i: (0, bi * tiles + i)),
                   pl.BlockSpec((SORT_GROUPS, SUBLANES, LANES), lambda bi, i: (bi * tiles + i, 0, 0)),
                   pl.BlockSpec((SUBLANES, LANES), const2),
                   pl.BlockSpec(memory_space=pl.ANY)],
        out_shape=[jax.ShapeDtypeStruct((b, s, d), F32),
                   jax.ShapeDtypeStruct((SUBLANES, b * s), F32),
                   jax.ShapeDtypeStruct((n_tiles * SORT_GROUPS, SUBLANES, LANES), I32),
                   jax.ShapeDtypeStruct((SUBLANES, LANES), I32),
                   jax.ShapeDtypeStruct((N_EXPERTS * region * ROW_WORDS, LANES), I32)],
        scratch_shapes=[pltpu.VMEM((POOL_PAD + tm, POOL_WIDTH), F32),
                        pltpu.VMEM((SORT_GROUPS, 2, POOL_PAD + SORT_TOKENS, POOL_GROUP_DIM), F32),
                        pltpu.VMEM((2, SORT_GROUPS, SORT_ROWS * ROW_WORDS, LANES), I32),
                        pltpu.VMEM((MOE_BLOCK * ROW_WORDS, LANES), I32),
                        pltpu.VMEM((SUBLANES, LANES), F32),
                        pltpu.SMEM((2 * SORT_GROUPS, SUBLANES, LANES), I32),
                        pltpu.SemaphoreType.DMA((2,)),
                        pltpu.SemaphoreType.DMA(())],
        compiler_params=pltpu.CompilerParams(
            dimension_semantics=("arbitrary", "arbitrary"), vmem_limit_bytes=VMEM_LIMIT),
        name="mixtail",
    )(u, u, att, gates, x, gt_m, sc_f, sh_f, g_post, g_pre,
      wpg, pscale, wbp, wba, wout, wr, br)


def _expert_kernel(counts_ref, xs_hbm, wg_ref, wu_ref, wd_ref, ys_hbm,
                   wg_bf, wu_bf, wd_bf, xbuf, ybuf, state, semx, semy):
    e = pl.program_id(0)
    bm = MOE_BLOCK
    nx, ny = EXPERT_X_BUFFERS, EXPERT_Y_BUFFERS
    block_words = bm * ROW_WORDS
    region = xs_hbm.shape[0] // (N_EXPERTS * ROW_WORDS)

    def n_blocks(ex):
        return (counts_ref[ex] + (bm - 1)) // bm

    def block_rows(ref, ex, k):
        start = pl.multiple_of((ex * region + k * bm) * ROW_WORDS, block_words)
        return ref.at[pl.ds(start, block_words), :]

    def x_copy(ex, k, s):
        return pltpu.make_async_copy(block_rows(xs_hbm, ex, k), xbuf.at[s], semx.at[s])

    def y_copy(k, s):
        return pltpu.make_async_copy(ybuf.at[s], block_rows(ys_hbm, e, k), semy.at[s])

    @pl.when(e == 0)
    def _():
        for j in range(4):
            state[j] = 0

    def fetch_through(target):
        def more(c):
            pe, _, pg = c
            return (pg < target) & (pe < N_EXPERTS)

        def step(c):
            pe, pk, pg = c
            has = pk < n_blocks(pe)

            @pl.when(has)
            def _():
                x_copy(pe, pk, pg % nx).start(priority=BLOCK_DMA_PRIORITY)

            return (jnp.where(has, pe, pe + 1), jnp.where(has, pk + 1, 0), pg + has.astype(I32))

        pe, pk, pg = lax.while_loop(more, step, (state[1], state[2], state[3]))
        state[1] = pe
        state[2] = pk
        state[3] = pg

    wg_bf[...] = wg_ref[...].astype(BF16)
    wu_bf[...] = wu_ref[...].astype(BF16)
    wd_bf[...] = wd_ref[...].astype(BF16)

    nb = n_blocks(e)
    done = state[0]

    def blocks(k0, count):
        ks = [k0 + c for c in range(count)]
        gs = [done + k for k in ks]
        fetch_through(gs[0] + nx)
        for k, g in zip(ks, gs):
            x_copy(e, k, g % nx).wait()

            @pl.when(g >= ny)
            def _():
                y_copy(k, g % ny).wait()

        mids = []
        for g in gs:
            x = _unpack_rows(xbuf.at[g % nx], bm).astype(BF16)
            a = _dot(x, wg_bf[...])
            u = _dot(x, wu_bf[...])
            mids.append(((a * _sigmoid(a)) * u).astype(BF16))
        for k, g, mid in zip(ks, gs, mids):
            for w in range(ROW_WORDS):
                pair = _bf16_exact(_dot(mid, wd_bf[:, 2 * w * LANES:(2 * w + 2) * LANES]))
                _pack_rows(ybuf.at[g % ny], w, bm, pair[:, :LANES], pair[:, LANES:])
            y_copy(k, g % ny).start(priority=BLOCK_DMA_PRIORITY)

    per = EXPERT_CHAINS

    def several_blocks(p, carry):
        blocks(per * p, per)
        return carry

    def single_block(k, carry):
        blocks(k, 1)
        return carry

    lax.fori_loop(0, nb // per, several_blocks, 0)
    lax.fori_loop((nb // per) * per, nb, single_block, 0)

    state[0] = done + nb

    @pl.when(e == N_EXPERTS - 1)
    def _():
        total = done + nb
        for j in range(ny):
            @pl.when(total > j)
            def _():
                y_copy(0, (total - 1 - j) % ny).wait()


def _experts(xs, counts, w_gate, w_up, w_down):
    bm = MOE_BLOCK
    w_in = pl.BlockSpec((None, D_MODEL, D_EXPERT), lambda e, cnt: (e, 0, 0))
    w_out = pl.BlockSpec((None, D_EXPERT, D_MODEL), lambda e, cnt: (e, 0, 0))
    grid_spec = pltpu.PrefetchScalarGridSpec(
        num_scalar_prefetch=1,
        grid=(N_EXPERTS,),
        in_specs=[pl.BlockSpec(memory_space=pl.ANY), w_in, w_in, w_out],
        out_specs=pl.BlockSpec(memory_space=pl.ANY),
        scratch_shapes=[pltpu.VMEM((D_MODEL, D_EXPERT), BF16),
                        pltpu.VMEM((D_MODEL, D_EXPERT), BF16),
                        pltpu.VMEM((D_EXPERT, D_MODEL), BF16),
                        pltpu.VMEM((EXPERT_X_BUFFERS, bm * ROW_WORDS, LANES), I32),
                        pltpu.VMEM((EXPERT_Y_BUFFERS, bm * ROW_WORDS, LANES), I32),
                        pltpu.SMEM((4,), I32),
                        pltpu.SemaphoreType.DMA((EXPERT_X_BUFFERS,)),
                        pltpu.SemaphoreType.DMA((EXPERT_Y_BUFFERS,))],
    )
    return pl.pallas_call(
        _expert_kernel,
        grid_spec=grid_spec,
        out_shape=jax.ShapeDtypeStruct(xs.shape, I32),
        compiler_params=pltpu.CompilerParams(
            dimension_semantics=("arbitrary",), vmem_limit_bytes=VMEM_LIMIT),
        name="experts",
    )(counts, xs, w_gate, w_up, w_down)


def _combine_kernel(meta_ref, y_hbm, route_ref, xmid_ref, gt_ref, g_ref, o_ref, ybuf, rt_scr, sem):
    step = pl.program_id(0)
    n_steps = pl.num_programs(0)
    tm = ROW_TILE
    slot = step % COMBINE_BUFFERS
    region = _region_rows(n_steps * tm)

    def fetch_tile(tile, live):
        tile = jnp.minimum(tile, n_steps - 1)
        which = tile % COMBINE_BUFFERS
        for sub in range(SORT_GROUPS):
            m = (tile * SORT_GROUPS + sub) * 2
            _tile_run_copies(lambda e: jnp.where(live, meta_ref[(m + 1) * N_EXPERTS + e], 0),
                             y_hbm, lambda e: e * region + meta_ref[m * N_EXPERTS + e],
                             ybuf.at[which, sub], None, sem.at[which])

    @pl.when(step == 0)
    def _():
        for ahead in range(COMBINE_BUFFERS - 1):
            fetch_tile(ahead, ahead < n_steps)

    for sub in range(SORT_GROUPS):
        pltpu.make_async_copy(y_hbm.at[pl.ds(0, SORT_ROWS * ROW_WORDS), :], ybuf.at[slot, sub],
                              sem.at[slot]).wait()
    sorted_rows = [_unpack_rows(ybuf.at[slot, sub], SORT_ROWS).astype(BF16) for sub in range(SORT_GROUPS)]

    fetch_tile(step + COMBINE_BUFFERS - 1, step + COMBINE_BUFFERS - 1 < n_steps)

    rt_scr[...] = jnp.zeros_like(rt_scr)
    rt_scr[0:SUBLANES, :] = route_ref[...]
    cols = jnp.concatenate([rt_scr[:, c * LANES:(c + 1) * LANES].T for c in range(tm // LANES)], axis=0)

    ts = SORT_TOKENS
    spos = lax.broadcasted_iota(I32, (ts, SORT_ROWS), 1).astype(F32)
    def rows_of(sub):
        return slice(sub * ts, (sub + 1) * ts)

    def gate_matrix(sub, st):
        key1, key2, w1, w2 = (cols[rows_of(sub), c:c + 1] for c in range(4))
        st["gates"] = jnp.where(spos == key1, w1, jnp.where(spos == key2, w2, 0.0)).astype(BF16)

    def weighted_unsort(sub, st):
        st["y"] = _dot(st.pop("gates"), sorted_rows[sub])

    def norm_and_add(sub, st):
        rows = rows_of(sub)
        o_ref[rows, :] = xmid_ref[rows, :] + gt_ref[0] * _rmsnorm(st.pop("y"), g_ref[...])

    _run_staggered((gate_matrix, weighted_unsort, norm_and_add), SORT_GROUPS, lag=1)


def _combine(meta, ys, route, x_mid, gt_f, g_post, seq):
    t, d = x_mid.shape
    tm = ROW_TILE
    tiles_per_seq = seq // tm
    grid_spec = pltpu.PrefetchScalarGridSpec(
        num_scalar_prefetch=1,
        grid=(t // tm,),
        in_specs=[pl.BlockSpec(memory_space=pl.ANY),
                  pl.BlockSpec((SUBLANES, tm), lambda i, m: (0, i)),
                  pl.BlockSpec((tm, d), lambda i, m: (i, 0)),
                  pl.BlockSpec((1, 1, d), lambda i, m: (i // tiles_per_seq, 0, 0)),
                  pl.BlockSpec((1, d), lambda i, m: (0, 0))],
        out_specs=pl.BlockSpec((tm, d), lambda i, m: (i, 0)),
        scratch_shapes=[pltpu.VMEM((COMBINE_BUFFERS, SORT_GROUPS, SORT_ROWS * ROW_WORDS, LANES), I32),
                        pltpu.VMEM((LANES, tm), F32),
                        pltpu.SemaphoreType.DMA((COMBINE_BUFFERS,))],
    )
    return pl.pallas_call(
        _combine_kernel,
        grid_spec=grid_spec,
        out_shape=jax.ShapeDtypeStruct((t, d), F32),
        compiler_params=pltpu.CompilerParams(
            dimension_semantics=("arbitrary",), vmem_limit_bytes=VMEM_LIMIT),
        name="combine",
    )(meta, ys, route, x_mid, gt_f, g_post)


def kernel(x, c, w_ada, b_ada, g_pre_mix, g_post_mix, g_pre_ffn, g_post_ffn, w_in, w_pool_group, pool_scale, w_branch_pool, w_branch_att, w_out, w_group_router, b_group_router, w_expert_router, b_expert_router, w_exp_gate, w_exp_up, w_exp_down):
    b, s, d = x.shape
    assert d == D_MODEL and s % (ATT_BLOCK * 2 * ATT_DILATIONS[2]) == 0 and s % ROW_TILE == 0
    assert (b * s) % MOE_BLOCK == 0
    depth = w_ada.shape[0]
    slopes = jnp.exp2(-ALIBI_MAX_BIAS * jnp.arange(1, N_ATT_HEADS + 1, dtype=F32) / N_ATT_HEADS)
    q_scale = HEAD_DIM ** -0.5 * LOG2_E

    for layer in range(depth):
        mod = _adaln(c, w_ada[layer], b_ada[layer]).reshape(b, 6, 1, d)
        sh_m, sc_m, gt_m, sh_f, sc_f, gt_f = [mod[:, j] for j in range(6)]

        wl = w_in[layer]
        q_lo, k_lo, v_lo = POOL_WIDTH, POOL_WIDTH + 768, POOL_WIDTH + 2 * 768
        group_cols = []
        for g in range(3):
            sl = slice(g * 256, (g + 1) * 256)
            group_cols += [wl[:, q_lo:k_lo][:, sl] * q_scale, wl[:, k_lo:v_lo][:, sl],
                           wl[:, v_lo:v_lo + 768][:, sl]]
        w_perm = jnp.concatenate([wl[:, :POOL_WIDTH]] + group_cols + [wl[:, v_lo + 768:]],
                                 axis=1).astype(BF16)

        u, qkv0, qkv1, qkv2, gates = _inproj(x, g_pre_mix[layer].reshape(1, d), sc_m, sh_m, w_perm)
        att = _attention(qkv0, qkv1, qkv2, slopes)

        zero_block = jnp.zeros((POOL_GROUP_DIM, POOL_GROUP_DIM), F32)
        wpg_diag = jnp.block([[w_pool_group[layer, g] if g == h else zero_block
                               for h in range(len(POOL_WINDOWS))] for g in range(len(POOL_WINDOWS))])

        pad_rows = ROUTER_ROWS - N_EXPERTS - N_EXPERT_GROUPS
        wrt = jnp.concatenate([w_expert_router[layer].T, w_group_router[layer].T,
                               jnp.zeros((pad_rows, d), F32)], axis=0)
        wrt_hi = wrt.astype(BF16)
        wr = jnp.concatenate([wrt_hi, (wrt - wrt_hi.astype(F32)).astype(BF16)], axis=0)
        br = jnp.concatenate([b_expert_router[layer], b_group_router[layer],
                              jnp.zeros((pad_rows,), F32)]).reshape(ROUTER_ROWS, 1)

        x_mid, route, meta, counts, xs = _mixtail(
            u, att, gates, x, gt_m, sc_f, sh_f,
            g_post_mix[layer].reshape(1, d), g_pre_ffn[layer].reshape(1, d),
            wpg_diag.astype(BF16), pool_scale[layer].reshape(1, POOL_WIDTH),
            w_branch_pool[layer].astype(BF16), w_branch_att[layer].astype(BF16),
            w_out[layer].astype(BF16), wr, br)

        ys = _experts(xs, counts[0, :N_EXPERTS], w_exp_gate[layer], w_exp_up[layer], w_exp_down[layer])
        run_meta = meta[:, 0:2, 0:N_EXPERTS].reshape(-1)
        x = _combine(run_meta, ys, route, x_mid.reshape(b * s, d), gt_f,
                     g_post_ffn[layer].reshape(1, d), s).reshape(b, s, d)
    return x
```

```python
import functools

import jax
import jax.numpy as jnp
from jax import lax
from jax.experimental import pallas as pl
from jax.experimental.pallas import tpu as pltpu

F32 = jnp.float32
BF16 = jnp.bfloat16
I32 = jnp.int32

D_MODEL = 1024
LANES = 128
SUBLANES = 8
ROW_WORDS = D_MODEL // (2 * LANES)
HI_HALF = -65536

POOL_WINDOWS = (2, 4, 8, 16)
POOL_GROUP_DIM = 128
POOL_WIDTH = 512
POOL_HALO = 16
POOL_PAD = POOL_HALO + 8

HEAD_DIM = 64
ATT_DILATIONS = (1, 4, 16)
ATT_REACH = 128
ATT_BLOCK = 128
ATT_UNROLL = 4
HEADS_PER_GROUP = 4
N_ATT_HEADS = 12
GROUP_QKV = 3 * HEADS_PER_GROUP * HEAD_DIM
ATT_OUT_WIDTH = 256
ALIBI_MAX_BIAS = 8.0
IN_WIDTH = POOL_WIDTH + 3 * GROUP_QKV + 2 * D_MODEL
MASKED = -1e30
LOG2_E = 1.4426950408889634

N_EXPERT_GROUPS = 4
EXPERTS_PER_GROUP = 8
N_EXPERTS = 32
ROUTER_ROWS = 48
D_EXPERT = 512
RMS_EPS = 1e-6

ROW_TILE = 1024
ROW_SPLIT = 4
SORT_TOKENS = 256
SORT_GROUPS = ROW_TILE // SORT_TOKENS
SORT_ROWS = 2 * SORT_TOKENS
MOE_BLOCK = 256
EXPERT_CHAINS = 2
BLOCK_DMA_PRIORITY = 1
EXPERT_X_BUFFERS = 6
EXPERT_Y_BUFFERS = 4
COMBINE_BUFFERS = 3
VMEM_LIMIT = 52 * 1024 * 1024


def _sigmoid(x):
    return 0.5 * jnp.tanh(0.5 * x) + 0.5


def _rmsnorm(x, g):
    return x * lax.rsqrt(jnp.mean(x * x, axis=-1, keepdims=True) + RMS_EPS) * g


def _dot(a, b):
    return jnp.dot(a, b, preferred_element_type=F32)


def _dot_nt(a, b, **kw):
    return lax.dot_general(a, b, (((1,), (1,)), ((), ())), preferred_element_type=F32, **kw)


def _pack_rows(ref, w, n, lo, hi):
    word = (lax.shift_right_logical(pltpu.bitcast(lo, I32), 16) | (pltpu.bitcast(hi, I32) & HI_HALF))
    ref[pl.ds(w, n, stride=ROW_WORDS), :] = word


def _unpack_rows(ref, n):
    cols = []
    for w in range(ROW_WORDS):
        word = ref[pl.ds(w, n, stride=ROW_WORDS), :]
        cols += [pltpu.bitcast(word << 16, F32), pltpu.bitcast(word & HI_HALF, F32)]
    return jnp.concatenate(cols, axis=1)


def _bf16_exact(x):
    return x.astype(BF16).astype(F32)


def _run_staggered(stages, n_chains, lag):
    state = [{} for _ in range(n_chains)]
    for tick in range(len(stages) + lag * (n_chains - 1)):
        for chain in range(n_chains):
            if 0 <= tick - lag * chain < len(stages):
                stages[tick - lag * chain](chain, state[chain])
    return state


def _region_rows(n_tokens):
    return n_tokens + MOE_BLOCK


def _run_copies(n, src, src_row, dst, dst_row, sem):
    @pl.when(n > 0)
    def _():
        pltpu.make_async_copy(
            src.at[pl.ds(src_row * ROW_WORDS, n * ROW_WORDS), :],
            dst.at[pl.ds(dst_row * ROW_WORDS, n * ROW_WORDS), :], sem).start()


def _tile_run_copies(counts, src, src_rows, dst, dst_rows, sem):
    local = 0
    for e in range(N_EXPERTS):
        n = counts(e)
        _run_copies(n, src, local if src_rows is None else src_rows(e),
                    dst, local if dst_rows is None else dst_rows(e), sem)
        local = local + n


def _adaln_kernel(c_ref, w_ref, b_ref, o_ref):
    c = c_ref[...]
    a = c * _sigmoid(c)
    w = w_ref[...]
    a_hi, w_hi = a.astype(BF16), w.astype(BF16)
    a_lo = (a - a_hi.astype(F32)).astype(BF16)
    w_lo = (w - w_hi.astype(F32)).astype(BF16)
    o_ref[...] = _dot(a_hi, w_hi) + (_dot(a_hi, w_lo) + _dot(a_lo, w_hi)) + b_ref[...]


def _adaln(c, w_ada, b_ada):
    b, d = c.shape
    n = w_ada.shape[1]
    rows = -(-b // SUBLANES) * SUBLANES
    cp = jnp.pad(c, ((0, rows - b), (0, 0)))
    nt = 1536
    out = pl.pallas_call(
        _adaln_kernel,
        grid=(n // nt,),
        in_specs=[pl.BlockSpec((rows, d), lambda j: (0, 0)),
                  pl.BlockSpec((d, nt), lambda j: (0, j)),
                  pl.BlockSpec((1, nt), lambda j: (0, j))],
        out_specs=pl.BlockSpec((rows, nt), lambda j: (0, j)),
        out_shape=jax.ShapeDtypeStruct((rows, n), F32),
        compiler_params=pltpu.CompilerParams(vmem_limit_bytes=VMEM_LIMIT),
        name="adaln",
    )(cp, w_ada, b_ada.reshape(1, n))
    return out[:b]


def _inproj_kernel(x_ref, g_ref, sc_ref, sh_ref, w_ref,
                   u_ref, qkv0_ref, qkv1_ref, qkv2_ref, gate_ref, h_scr, p_scr):
    tm = x_ref.shape[1]
    hm = tm // ROW_SPLIT
    for part in range(ROW_SPLIT):
        rows = slice(part * hm, (part + 1) * hm)
        h = _rmsnorm(x_ref[0, rows, :], g_ref[...]) * (1.0 + sc_ref[0]) + sh_ref[0]
        h_scr[rows, :] = h.astype(BF16)
        hb = h_scr[rows, :]

        u_ref[0, rows, :] = _dot(hb, w_ref[:, 0:POOL_WIDTH]).astype(BF16)

        col = POOL_WIDTH
        qkv0_ref[0, 0, rows, :] = _dot(hb, w_ref[:, col:col + GROUP_QKV]).astype(BF16)
        for gi, (out_ref, d) in enumerate(((qkv1_ref, ATT_DILATIONS[1]), (qkv2_ref, ATT_DILATIONS[2]))):
            col += GROUP_QKV
            proj = _dot(hb, w_ref[:, col:col + GROUP_QKV])
            stage = p_scr.at[part, gi]
            for cb in range(GROUP_QKV // LANES):
                stage[cb] = proj[:, cb * LANES:(cb + 1) * LANES]
            sub = hm // d
            for r in range(d):
                out_ref[0, r, part * sub:(part + 1) * sub, :] = jnp.concatenate(
                    [stage[cb, pl.ds(r, sub, stride=d), :] for cb in range(GROUP_QKV // LANES)],
                    axis=1).astype(BF16)
        col += GROUP_QKV

        chunk = 512
        for j in range(2 * D_MODEL // chunk):
            g = _dot(hb, w_ref[:, col + j * chunk:col + (j + 1) * chunk])
            gate_ref[0, rows, j * chunk:(j + 1) * chunk] = _sigmoid(g).astype(BF16)


def _inproj(x, g_pre, sc, sh, w_perm):
    b, s, d = x.shape
    tm = ROW_TILE
    d1, d2 = ATT_DILATIONS[1], ATT_DILATIONS[2]
    grid = (b, s // tm)
    const = lambda bi, i: (0, 0)
    per_b = lambda bi, i: (bi, 0, 0)
    return pl.pallas_call(
        _inproj_kernel,
        grid=grid,
        in_specs=[pl.BlockSpec((1, tm, d), lambda bi, i: (bi, i, 0)),
                  pl.BlockSpec((1, d), const),
                  pl.BlockSpec((1, 1, d), per_b),
                  pl.BlockSpec((1, 1, d), per_b),
                  pl.BlockSpec((d, IN_WIDTH), const, pipeline_mode=pl.Buffered(1))],
        out_specs=[pl.BlockSpec((1, tm, POOL_WIDTH), lambda bi, i: (bi, i, 0)),
                   pl.BlockSpec((1, 1, tm, GROUP_QKV), lambda bi, i: (bi, 0, i, 0)),
                   pl.BlockSpec((1, d1, tm // d1, GROUP_QKV), lambda bi, i: (bi, 0, i, 0)),
                   pl.BlockSpec((1, d2, tm // d2, GROUP_QKV), lambda bi, i: (bi, 0, i, 0)),
                   pl.BlockSpec((1, tm, 2 * D_MODEL), lambda bi, i: (bi, i, 0))],
        out_shape=[jax.ShapeDtypeStruct((b, s, POOL_WIDTH), BF16),
                   jax.ShapeDtypeStruct((b, 1, s, GROUP_QKV), BF16),
                   jax.ShapeDtypeStruct((b, d1, s // d1, GROUP_QKV), BF16),
                   jax.ShapeDtypeStruct((b, d2, s // d2, GROUP_QKV), BF16),
                   jax.ShapeDtypeStruct((b, s, 2 * D_MODEL), BF16)],
        scratch_shapes=[pltpu.VMEM((tm, d), BF16),
                        pltpu.VMEM((ROW_SPLIT, 2, GROUP_QKV // LANES, tm // ROW_SPLIT, LANES), F32)],
        compiler_params=pltpu.CompilerParams(
            dimension_semantics=("arbitrary", "arbitrary"), vmem_limit_bytes=VMEM_LIMIT),
        name="inproj",
    )(x, g_pre, sc, sh, w_perm)


def _attn_kernel(slopes_ref, q0, k0, v0, q1, k1, v1, q2, k2, v2, o_ref,
                 bias_scr, acc, mst, lst, s_even, s_odd):
    pair = pl.program_id(1)
    seq = o_ref.shape[1]
    nblk = seq // ATT_BLOCK
    lane = lax.broadcasted_iota(I32, (ATT_BLOCK, LANES), 1)
    first_head = lane < HEAD_DIM
    half_lane = lax.broadcasted_iota(I32, (ATT_BLOCK // 2, LANES), 1)
    head_bits = (jnp.where(half_lane < HEAD_DIM, -1, 0), jnp.where(half_lane < HEAD_DIM, 0, -1))
    ones_block = jnp.ones((2 * ATT_BLOCK, LANES), BF16)

    qi = lax.broadcasted_iota(I32, (ATT_BLOCK, 2 * ATT_BLOCK), 0)
    kj = lax.broadcasted_iota(I32, (ATT_BLOCK, 2 * ATT_BLOCK), 1)
    delta = ATT_BLOCK + qi - kj
    valid = (delta >= 0) & (delta <= ATT_REACH)
    delta0 = qi - kj
    valid0 = delta0 >= 0
    for g, d in enumerate(ATT_DILATIONS):
        for j in range(2):
            slope = slopes_ref[g * HEADS_PER_GROUP + 2 * pair + j]
            slope = slope * LOG2_E
            bias_scr[g, j, 0] = jnp.where(valid0, -slope * (delta0 * d).astype(F32), MASKED)
            bias_scr[g, j, 1] = jnp.where(valid, -slope * (delta * d).astype(F32), MASKED)

    def run_group(g, q_ref, k_ref, v_ref):
        d = ATT_DILATIONS[g]
        per_res = nblk // d
        n_iter = nblk // ATT_UNROLL

        def block_index(it, k):
            n = it * ATT_UNROLL + k
            return n // per_res, n % per_res

        def key_rows(i):
            lo = jnp.maximum(i - 1, 0)
            return pl.ds(pl.multiple_of(lo * ATT_BLOCK, ATT_BLOCK), 2 * ATT_BLOCK)

        def scores(it, s_ref):
            for k in range(ATT_UNROLL):
                r, i = block_index(it, k)
                q = q_ref[0, r, pl.ds(pl.multiple_of(i * ATT_BLOCK, ATT_BLOCK), ATT_BLOCK), :]
                kw = k_ref[0, r, key_rows(i), :]
                qbits = pltpu.bitcast(q, I32)
                for j in range(2):
                    qh = pltpu.bitcast(qbits & head_bits[j], BF16)
                    s_ref[2 * k + j] = _dot_nt(qh, kw) + bias_scr[g, j, jnp.minimum(i, 1)]

        def weighted_values(it, s_ref):
            for k in range(ATT_UNROLL):
                r, i = block_index(it, k)
                vw = jnp.concatenate([v_ref[0, r, key_rows(i), :], ones_block], axis=1)
                outs = []
                for j in range(2):
                    sc = s_ref[2 * k + j]
                    mx = jnp.max(sc, axis=-1, keepdims=True)
                    p = jnp.exp2(sc - mx)
                    num_den = _dot(p.astype(BF16), vw)
                    outs.append((num_den[:, :LANES], mx, num_den[:, LANES:]))
                (n0, m0, l0), (n1, m1, l1) = outs
                if d == 1:
                    rows = pl.ds(pl.multiple_of(i * ATT_BLOCK, ATT_BLOCK), ATT_BLOCK)
                else:
                    rows = pl.ds(i * (ATT_BLOCK * d) + r, ATT_BLOCK, stride=d)
                acc[g, rows, :] = jnp.where(first_head, n0, n1)
                mst[g, rows, :] = jnp.where(first_head, m0, m1)
                lst[g, rows, :] = jnp.where(first_head, l0, l1)

        scores(0, s_even)

        def body(h, carry):
            it = 2 * h
            scores(it + 1, s_odd)
            weighted_values(it, s_even)
            scores(jnp.minimum(it + 2, n_iter - 1), s_even)
            weighted_values(it + 1, s_odd)
            return carry

        lax.fori_loop(0, n_iter // 2, body, 0)

    run_group(0, q0, k0, v0)
    run_group(1, q1, k1, v1)
    run_group(2, q2, k2, v2)

    def finish(n, carry):
        rows = pl.ds(pl.multiple_of(n * ATT_BLOCK, ATT_BLOCK), ATT_BLOCK)
        ms = [mst[g, rows, :] for g in range(3)]
        top = jnp.maximum(jnp.maximum(ms[0], ms[1]), ms[2])
        scale = [jnp.exp2(m - top) for m in ms]
        num = scale[0] * acc[0, rows, :] + scale[1] * acc[1, rows, :] + scale[2] * acc[2, rows, :]
        den = scale[0] * lst[0, rows, :] + scale[1] * lst[1, rows, :] + scale[2] * lst[2, rows, :]
        o_ref[0, rows, :] = (num / den).astype(BF16)
        return carry

    lax.fori_loop(0, nblk, finish, 0)


def _attention(qkv0, qkv1, qkv2, slopes):
    b, _, s, _ = qkv0.shape
    pairs = HEADS_PER_GROUP // 2
    col_blocks = HEADS_PER_GROUP * HEAD_DIM // LANES

    def specs(arr):
        _, d, sub, _ = arr.shape
        return [pl.BlockSpec((1, d, sub, LANES),
                             functools.partial(lambda bi, p, sec: (bi, 0, 0, sec * col_blocks + p), sec=sec))
                for sec in range(3)]

    return pl.pallas_call(
        _attn_kernel,
        grid=(b, pairs),
        in_specs=[pl.BlockSpec(memory_space=pltpu.SMEM)] + specs(qkv0) + specs(qkv1) + specs(qkv2),
        out_specs=pl.BlockSpec((1, s, LANES), lambda bi, p: (bi, 0, p)),
        out_shape=jax.ShapeDtypeStruct((b, s, ATT_OUT_WIDTH), BF16),
        scratch_shapes=[pltpu.VMEM((3, 2, 2, ATT_BLOCK, 2 * ATT_BLOCK), F32),
                        pltpu.VMEM((3, s, LANES), F32),
                        pltpu.VMEM((3, s, LANES), F32),
                        pltpu.VMEM((3, s, LANES), F32),
                        pltpu.VMEM((2 * ATT_UNROLL, ATT_BLOCK, 2 * ATT_BLOCK), F32),
                        pltpu.VMEM((2 * ATT_UNROLL, ATT_BLOCK, 2 * ATT_BLOCK), F32)],
        compiler_params=pltpu.CompilerParams(
            dimension_semantics=("arbitrary", "arbitrary"), vmem_limit_bytes=VMEM_LIMIT),
        name="attention",
    )(slopes, qkv0, qkv0, qkv0, qkv1, qkv1, qkv1, qkv2, qkv2, qkv2)


def _mixtail_kernel(u_ref, halo_ref, att_ref, gate_ref, x_ref,
                    gt_m_ref, sc_f_ref, sh_f_ref, g_post_ref, g_pre_ref,
                    wpg_ref, pscale_ref, wbp_ref, wba_ref, wout_ref, wr_ref, br_ref,
                    xmid_ref, route_ref, meta_ref, counts_ref, xs_hbm,
                    pu, lv, xbuf, zbuf, fill, meta_s, sem, sem_s):
    i = pl.program_id(1)
    tm = x_ref.shape[1]
    step = pl.program_id(0) * pl.num_programs(1) + i
    last = pl.num_programs(0) * pl.num_programs(1) - 1
    slot = step % 2
    region = _region_rows(pl.num_programs(0) * pl.num_programs(1) * tm)

    @pl.when(step == 0)
    def _():
        fill[...] = jnp.zeros_like(fill)
        pu[0:POOL_PAD - POOL_HALO, :] = jnp.zeros((POOL_PAD - POOL_HALO, POOL_WIDTH), F32)
        lv[:, :, 0:POOL_PAD - POOL_HALO, :] = jnp.zeros(
            (SORT_GROUPS, 2, POOL_PAD - POOL_HALO, POOL_GROUP_DIM), F32)
        for sub in range(SORT_GROUPS):
            for e in range(N_EXPERTS):
                meta_s[SORT_GROUPS + sub, 1, e] = 0
                meta_s[SORT_GROUPS + sub, 0, e] = 0

    def send_tile(which):
        for sub in range(SORT_GROUPS):
            m = which * SORT_GROUPS + sub
            _tile_run_copies(lambda e: meta_s[m, 1, e], xbuf.at[which, sub], None,
                             xs_hbm, lambda e: e * region + meta_s[m, 0, e], sem.at[which])

    def wait_tile(which):
        for sub in range(SORT_GROUPS):
            pltpu.make_async_copy(xbuf.at[which, sub], xs_hbm.at[pl.ds(0, SORT_ROWS * ROW_WORDS), :],
                                  sem.at[which]).wait()

    @pl.when(step > 1)
    def _():
        wait_tile(slot)

    send_tile(1 - slot)

    head = POOL_PAD - POOL_HALO
    halo = halo_ref[0].astype(F32)
    pu[head:POOL_PAD, :] = jnp.where(i > 0, halo, jnp.zeros_like(halo))
    pu[POOL_PAD:POOL_PAD + tm, :] = u_ref[0].astype(F32)

    ts = SORT_TOKENS

    def pool(sub, st):
        base = sub * ts
        t = i * tm + base + lax.broadcasted_iota(I32, (ts, 1), 0)
        pooled_groups = []
        for g, w in enumerate(POOL_WINDOWS):
            cols = slice(g * POOL_GROUP_DIM, (g + 1) * POOL_GROUP_DIM)
            read = lambda start, n: pu[pl.ds(base + start, n), cols]
            shift, level = 1, 0
            while 2 * shift < w:
                partial = read(head, ts + POOL_HALO) + read(head - shift, ts + POOL_HALO)
                buf = lv.at[sub, level % 2]
                buf[pl.ds(head, ts + POOL_HALO), :] = partial
                read = lambda start, n, buf=buf: buf[pl.ds(start, n), :]
                shift, level = 2 * shift, level + 1
            win = read(POOL_PAD, ts) + read(POOL_PAD - shift, ts)
            count = jnp.minimum(t + 1, w).astype(F32)
            pooled_groups.append((win / count - pu[pl.ds(base + POOL_PAD, ts), cols]).astype(BF16))
        st["pooled"] = jnp.concatenate(pooled_groups, axis=1)

    neg_inf = -jnp.inf
    far = float(LANES)
    before = (lax.broadcasted_iota(I32, (ts, ts), 0) < lax.broadcasted_iota(I32, (ts, ts), 1))
    earlier = jnp.where(before, 1.0, 0.0).astype(BF16)
    erow = lax.broadcasted_iota(I32, (N_EXPERTS, ts), 0).astype(F32)
    e_col = lax.broadcasted_iota(I32, (N_EXPERTS, LANES), 0)
    e_lane = lax.broadcasted_iota(I32, (N_EXPERTS, LANES), 1)
    row8 = lax.broadcasted_iota(I32, (SUBLANES, ts), 0)
    mrow = lax.broadcasted_iota(I32, (SUBLANES, LANES), 0)
    srow = lax.broadcasted_iota(I32, (SORT_ROWS, ts), 0).astype(F32)
    filled = fill[...]

    def rows_of(sub):
        return slice(sub * ts, (sub + 1) * ts)

    def branch_projections(sub, st):
        rows = rows_of(sub)
        mixed = _dot(st.pop("pooled"), wpg_ref[...]) * pscale_ref[...]
        st["y_pool"] = _dot(mixed.astype(BF16), wbp_ref[...])
        st["y_att"] = _dot(att_ref[0, rows, :], wba_ref[...])

    def gated_sum(sub, st):
        rows = rows_of(sub)
        st["merged"] = (gate_ref[0, rows, 0:D_MODEL] * st.pop("y_pool").astype(BF16)
                        + gate_ref[0, rows, D_MODEL:2 * D_MODEL] * st.pop("y_att").astype(BF16))

    def output_projection(sub, st):
        st["y"] = _dot(st.pop("merged"), wout_ref[...])

    def residual_and_ffn_input(sub, st):
        rows = rows_of(sub)
        x_mid = x_ref[0, rows, :] + gt_m_ref[0] * _rmsnorm(st.pop("y"), g_post_ref[...])
        xmid_ref[0, rows, :] = x_mid
        h2 = _rmsnorm(x_mid, g_pre_ref[...]) * (1.0 + sc_f_ref[0]) + sh_f_ref[0]
        st["h2b"] = h2.astype(BF16)
        st["h2lo"] = (h2 - st["h2b"].astype(F32)).astype(BF16)

    def router_logits(sub, st):
        by_hi = _dot_nt(wr_ref[...], st["h2b"])
        st["logits"] = (by_hi[0:ROUTER_ROWS] + (by_hi[ROUTER_ROWS:2 * ROUTER_ROWS]
                        + _dot_nt(wr_ref[0:ROUTER_ROWS, :], st.pop("h2lo"))) + br_ref[...])

    def route(sub, st):
        logits = st.pop("logits")
        gl = logits[N_EXPERTS:N_EXPERTS + N_EXPERT_GROUPS, :]
        grow = lax.broadcasted_iota(I32, gl.shape, 0).astype(F32)
        gmax = jnp.max(gl, axis=0, keepdims=True)
        gsel = jnp.min(jnp.where(gl == gmax, grow, far), axis=0, keepdims=True)
        p_group = 1.0 / jnp.sum(jnp.exp(gl - gmax), axis=0, keepdims=True)
        e_lo = gsel * float(EXPERTS_PER_GROUP)
        el = jnp.where((erow >= e_lo) & (erow < e_lo + float(EXPERTS_PER_GROUP)),
                       logits[0:N_EXPERTS, :], neg_inf)
        v1 = jnp.max(el, axis=0, keepdims=True)
        i1 = jnp.min(jnp.where(el == v1, erow, far), axis=0, keepdims=True)
        el2 = jnp.where(erow == i1, neg_inf, el)
        v2 = jnp.max(el2, axis=0, keepdims=True)
        i2 = jnp.min(jnp.where(el2 == v2, erow, far), axis=0, keepdims=True)
        e21 = jnp.exp(v2 - v1)
        st["w1"] = p_group / (1.0 + e21)
        st["w2"] = p_group * e21 / (1.0 + e21)
        st["pick1"] = erow == i1
        st["pick2"] = erow == i2
        st["assign"] = jnp.where(st["pick1"] | st["pick2"], 1.0, 0.0)

    def count_and_rank(sub, st):
        assign = st.pop("assign")
        st["rank"] = _dot(assign.astype(BF16), earlier)
        assign_pad = jnp.concatenate([assign, jnp.zeros((LANES - N_EXPERTS, ts), F32)], axis=0).astype(BF16)
        st["cnt_row"] = _dot_nt(jnp.ones((SUBLANES, ts), BF16), assign_pad)

    def sorted_positions(sub, st):
        cnt_row = st["cnt_row"]
        run_start = jnp.sum(jnp.where(e_lane < e_col, cnt_row[0:1, :], 0.0), axis=1, keepdims=True)
        pos = st.pop("rank") + run_start
        key1 = jnp.sum(jnp.where(st.pop("pick1"), pos, 0.0), axis=0, keepdims=True)
        key2 = jnp.sum(jnp.where(st.pop("pick2"), pos, 0.0), axis=0, keepdims=True)
        route_ref[:, rows_of(sub)] = jnp.where(row8 == 0, key1, jnp.where(row8 == 1, key2,
                                               jnp.where(row8 == 2, st.pop("w1"),
                                                         jnp.where(row8 == 3, st.pop("w2"), 0.0))))
        st["perm"] = jnp.where((srow == key1) | (srow == key2), 1.0, 0.0).astype(BF16)

    def sort_rows(sub, st):
        perm, h2b = st.pop("perm"), st.pop("h2b")
        for w in range(ROW_WORDS):
            pair = _dot(perm, h2b[:, 2 * w * LANES:(2 * w + 2) * LANES])
            _pack_rows(xbuf.at[slot, sub], w, SORT_ROWS, pair[:, :LANES], pair[:, LANES:])

    state = _run_staggered((pool, branch_projections, gated_sum, output_projection,
                            residual_and_ffn_input, router_logits, route, count_and_rank,
                            sorted_positions, sort_rows), SORT_GROUPS, lag=0)

    for sub in range(SORT_GROUPS):
        cnt_row = state[sub]["cnt_row"]
        meta_ref[sub] = jnp.where(mrow == 0, filled, jnp.where(mrow == 1, cnt_row, 0.0)).astype(I32)
        filled = filled + cnt_row

    fill[...] = filled
    counts_ref[...] = filled.astype(I32)

    meta_copy = pltpu.make_async_copy(meta_ref, meta_s.at[pl.ds(slot * SORT_GROUPS, SORT_GROUPS)], sem_s)
    meta_copy.start()
    meta_copy.wait()

    @pl.when(step == last)
    def _():
        send_tile(slot)

        @pl.when(step > 0)
        def _():
            wait_tile(1 - slot)

        wait_tile(slot)
        zbuf[...] = jnp.zeros_like(zbuf)
        final = slot * SORT_GROUPS + SORT_GROUPS - 1

        def pad_copy(e):
            end = e * region + meta_s[final, 0, e] + meta_s[final, 1, e]
            return pltpu.make_async_copy(
                zbuf, xs_hbm.at[pl.ds(end * ROW_WORDS, MOE_BLOCK * ROW_WORDS), :], sem_s)

        def start_pad(e, carry):
            pad_copy(e).start()
            return carry

        def wait_pad(e, carry):
            pad_copy(e).wait()
            return carry

        lax.fori_loop(0, N_EXPERTS, start_pad, 0)
        lax.fori_loop(0, N_EXPERTS, wait_pad, 0)


def _mixtail(u, att, gates, x, gt_m, sc_f, sh_f, g_post, g_pre,
             wpg, pscale, wbp, wba, wout, wr, br):
    b, s, d = x.shape
    tm = ROW_TILE
    tiles = s // tm
    n_tiles = b * tiles
    halo_blocks = tm // POOL_HALO
    region = _region_rows(b * s)
    const2 = lambda bi, i: (0, 0)
    const3 = lambda bi, i: (0, 0, 0)
    per_b = lambda bi, i: (bi, 0, 0)
    tile = lambda bi, i: (bi, i, 0)
    single = dict(pipeline_mode=pl.Buffered(1))
    return pl.pallas_call(
        _mixtail_kernel,
        grid=(b, tiles),
        in_specs=[pl.BlockSpec((1, tm, POOL_WIDTH), tile),
                  pl.BlockSpec((1, POOL_HALO, POOL_WIDTH),
                               lambda bi, i: (bi, jnp.maximum(i * halo_blocks - 1, 0), 0)),
                  pl.BlockSpec((1, tm, ATT_OUT_WIDTH), tile),
                  pl.BlockSpec((1, tm, 2 * D_MODEL), tile),
                  pl.BlockSpec((1, tm, d), tile),
                  pl.BlockSpec((1, 1, d), per_b),
                  pl.BlockSpec((1, 1, d), per_b),
                  pl.BlockSpec((1, 1, d), per_b),
                  pl.BlockSpec((1, d), const2),
                  pl.BlockSpec((1, d), const2),
                  pl.BlockSpec(wpg.shape, const2, **single),
                  pl.BlockSpec((1, POOL_WIDTH), const2),
                  pl.BlockSpec(wbp.shape, const2, **single),
                  pl.BlockSpec(wba.shape, const2, **single),
                  pl.BlockSpec(wout.shape, const2, **single),
                  pl.BlockSpec(wr.shape, const2, **single),
                  pl.BlockSpec(br.shape, const2)],
        out_specs=[pl.BlockSpec((1, tm, d), tile),
                   pl.BlockSpec((SUBLANES, tm), lambda bi, i: (0, bi * tiles + i)),
                   pl.BlockSpec((SORT_GROUPS, SUBLANES, LANES), lambda bi, i: (bi * tiles + i, 0, 0)),
                   pl.BlockSpec((SUBLANES, LANES), const2),
                   pl.BlockSpec(memory_space=pl.ANY)],
        out_shape=[jax.ShapeDtypeStruct((b, s, d), F32),
                   jax.ShapeDtypeStruct((SUBLANES, b * s), F32),
                   jax.ShapeDtypeStruct((n_tiles * SORT_GROUPS, SUBLANES, LANES), I32),
                   jax.ShapeDtypeStruct((SUBLANES, LANES), I32),
                   jax.ShapeDtypeStruct((N_EXPERTS * region * ROW_WORDS, LANES), I32)],
        scratch_shapes=[pltpu.VMEM((POOL_PAD + tm, POOL_WIDTH), F32),
                        pltpu.VMEM((SORT_GROUPS, 2, POOL_PAD + SORT_TOKENS, POOL_GROUP_DIM), F32),
                        pltpu.VMEM((2, SORT_GROUPS, SORT_ROWS * ROW_WORDS, LANES), I32),
                        pltpu.VMEM((MOE_BLOCK * ROW_WORDS, LANES), I32),
                        pltpu.VMEM((SUBLANES, LANES), F32),
                        pltpu.SMEM((2 * SORT_GROUPS, SUBLANES, LANES), I32),
                        pltpu.SemaphoreType.DMA((2,)),
                        pltpu.SemaphoreType.DMA(())],
        compiler_params=pltpu.CompilerParams(
            dimension_semantics=("arbitrary", "arbitrary"), vmem_limit_bytes=VMEM_LIMIT),
        name="mixtail",
    )(u, u, att, gates, x, gt_m, sc_f, sh_f, g_post, g_pre,
      wpg, pscale, wbp, wba, wout, wr, br)


def _expert_kernel(counts_ref, xs_hbm, wg_ref, wu_ref, wd_ref, ys_hbm,
                   xbuf, ybuf, state, semx, semy):
    e = pl.program_id(0)
    bm = MOE_BLOCK
    nx, ny = EXPERT_X_BUFFERS, EXPERT_Y_BUFFERS
    block_words = bm * ROW_WORDS
    region = xs_hbm.shape[0] // (N_EXPERTS * ROW_WORDS)

    def n_blocks(ex):
        return (counts_ref[ex] + (bm - 1)) // bm

    def block_rows(ref, ex, k):
        start = pl.multiple_of((ex * region + k * bm) * ROW_WORDS, block_words)
        return ref.at[pl.ds(start, block_words), :]

    def x_copy(ex, k, s):
        return pltpu.make_async_copy(block_rows(xs_hbm, ex, k), xbuf.at[s], semx.at[s])

    def y_copy(k, s):
        return pltpu.make_async_copy(ybuf.at[s], block_rows(ys_hbm, e, k), semy.at[s])

    @pl.when(e == 0)
    def _():
        for j in range(4):
            state[j] = 0

    def fetch_through(target):
        def more(c):
            pe, _, pg = c
            return (pg < target) & (pe < N_EXPERTS)

        def step(c):
            pe, pk, pg = c
            has = pk < n_blocks(pe)

            @pl.when(has)
            def _():
                x_copy(pe, pk, pg % nx).start(priority=BLOCK_DMA_PRIORITY)

            return (jnp.where(has, pe, pe + 1), jnp.where(has, pk + 1, 0), pg + has.astype(I32))

        pe, pk, pg = lax.while_loop(more, step, (state[1], state[2], state[3]))
        state[1] = pe
        state[2] = pk
        state[3] = pg

    nb = n_blocks(e)
    done = state[0]

    def blocks(k0, count):
        ks = [k0 + c for c in range(count)]
        gs = [done + k for k in ks]
        fetch_through(gs[0] + nx)
        for k, g in zip(ks, gs):
            x_copy(e, k, g % nx).wait()

            @pl.when(g >= ny)
            def _():
                y_copy(k, g % ny).wait()

        mids = []
        for g in gs:
            x = _unpack_rows(xbuf.at[g % nx], bm).astype(BF16)
            a = _dot(x, wg_ref[...].astype(BF16))
            u = _dot(x, wu_ref[...].astype(BF16))
            mids.append(((a * _sigmoid(a)) * u).astype(BF16))
        for k, g, mid in zip(ks, gs, mids):
            for w in range(ROW_WORDS):
                pair = _bf16_exact(_dot(mid, wd_ref[:, 2 * w * LANES:(2 * w + 2) * LANES].astype(BF16)))
                _pack_rows(ybuf.at[g % ny], w, bm, pair[:, :LANES], pair[:, LANES:])
            y_copy(k, g % ny).start(priority=BLOCK_DMA_PRIORITY)

    per = EXPERT_CHAINS

    def several_blocks(p, carry):
        blocks(per * p, per)
        return carry

    def single_block(k, carry):
        blocks(k, 1)
        return carry

    lax.fori_loop(0, nb // per, several_blocks, 0)
    lax.fori_loop((nb // per) * per, nb, single_block, 0)

    state[0] = done + nb

    @pl.when(e == N_EXPERTS - 1)
    def _():
        total = done + nb
        for j in range(ny):
            @pl.when(total > j)
            def _():
                y_copy(0, (total - 1 - j) % ny).wait()


def _experts(xs, counts, w_gate, w_up, w_down):
    bm = MOE_BLOCK
    w_in = pl.BlockSpec((None, D_MODEL, D_EXPERT), lambda e, cnt: (e, 0, 0))
    w_out = pl.BlockSpec((None, D_EXPERT, D_MODEL), lambda e, cnt: (e, 0, 0))
    grid_spec = pltpu.PrefetchScalarGridSpec(
        num_scalar_prefetch=1,
        grid=(N_EXPERTS,),
        in_specs=[pl.BlockSpec(memory_space=pl.ANY), w_in, w_in, w_out],
        out_specs=pl.BlockSpec(memory_space=pl.ANY),
        scratch_shapes=[pltpu.VMEM((EXPERT_X_BUFFERS, bm * ROW_WORDS, LANES), I32),
                        pltpu.VMEM((EXPERT_Y_BUFFERS, bm * ROW_WORDS, LANES), I32),
                        pltpu.SMEM((4,), I32),
                        pltpu.SemaphoreType.DMA((EXPERT_X_BUFFERS,)),
                        pltpu.SemaphoreType.DMA((EXPERT_Y_BUFFERS,))],
    )
    return pl.pallas_call(
        _expert_kernel,
        grid_spec=grid_spec,
        out_shape=jax.ShapeDtypeStruct(xs.shape, I32),
        compiler_params=pltpu.CompilerParams(
            dimension_semantics=("arbitrary",), vmem_limit_bytes=VMEM_LIMIT),
        name="experts",
    )(counts, xs, w_gate, w_up, w_down)


def _combine_kernel(meta_ref, y_hbm, route_ref, xmid_ref, gt_ref, g_ref, o_ref, ybuf, rt_scr, sem):
    step = pl.program_id(0)
    n_steps = pl.num_programs(0)
    tm = ROW_TILE
    slot = step % COMBINE_BUFFERS
    region = _region_rows(n_steps * tm)

    def fetch_tile(tile, live):
        tile = jnp.minimum(tile, n_steps - 1)
        which = tile % COMBINE_BUFFERS
        for sub in range(SORT_GROUPS):
            m = (tile * SORT_GROUPS + sub) * 2
            _tile_run_copies(lambda e: jnp.where(live, meta_ref[(m + 1) * N_EXPERTS + e], 0),
                             y_hbm, lambda e: e * region + meta_ref[m * N_EXPERTS + e],
                             ybuf.at[which, sub], None, sem.at[which])

    @pl.when(step == 0)
    def _():
        for ahead in range(COMBINE_BUFFERS - 1):
            fetch_tile(ahead, ahead < n_steps)

    for sub in range(SORT_GROUPS):
        pltpu.make_async_copy(y_hbm.at[pl.ds(0, SORT_ROWS * ROW_WORDS), :], ybuf.at[slot, sub],
                              sem.at[slot]).wait()
    sorted_rows = [_unpack_rows(ybuf.at[slot, sub], SORT_ROWS).astype(BF16) for sub in range(SORT_GROUPS)]

    fetch_tile(step + COMBINE_BUFFERS - 1, step + COMBINE_BUFFERS - 1 < n_steps)

    rt_scr[...] = jnp.zeros_like(rt_scr)
    rt_scr[0:SUBLANES, :] = route_ref[...]
    cols = jnp.concatenate([rt_scr[:, c * LANES:(c + 1) * LANES].T for c in range(tm // LANES)], axis=0)

    ts = SORT_TOKENS
    spos = lax.broadcasted_iota(I32, (ts, SORT_ROWS), 1).astype(F32)
    def rows_of(sub):
        return slice(sub * ts, (sub + 1) * ts)

    def gate_matrix(sub, st):
        key1, key2, w1, w2 = (cols[rows_of(sub), c:c + 1] for c in range(4))
        st["gates"] = jnp.where(spos == key1, w1, jnp.where(spos == key2, w2, 0.0)).astype(BF16)

    def weighted_unsort(sub, st):
        st["y"] = _dot(st.pop("gates"), sorted_rows[sub])

    def norm_and_add(sub, st):
        rows = rows_of(sub)
        o_ref[rows, :] = xmid_ref[rows, :] + gt_ref[0] * _rmsnorm(st.pop("y"), g_ref[...])

    _run_staggered((gate_matrix, weighted_unsort, norm_and_add), SORT_GROUPS, lag=1)


def _combine(meta, ys, route, x_mid, gt_f, g_post, seq):
    t, d = x_mid.shape
    tm = ROW_TILE
    tiles_per_seq = seq // tm
    grid_spec = pltpu.PrefetchScalarGridSpec(
        num_scalar_prefetch=1,
        grid=(t // tm,),
        in_specs=[pl.BlockSpec(memory_space=pl.ANY),
                  pl.BlockSpec((SUBLANES, tm), lambda i, m: (0, i)),
                  pl.BlockSpec((tm, d), lambda i, m: (i, 0)),
                  pl.BlockSpec((1, 1, d), lambda i, m: (i // tiles_per_seq, 0, 0)),
                  pl.BlockSpec((1, d), lambda i, m: (0, 0))],
        out_specs=pl.BlockSpec((tm, d), lambda i, m: (i, 0)),
        scratch_shapes=[pltpu.VMEM((COMBINE_BUFFERS, SORT_GROUPS, SORT_ROWS * ROW_WORDS, LANES), I32),
                        pltpu.VMEM((LANES, tm), F32),
                        pltpu.SemaphoreType.DMA((COMBINE_BUFFERS,))],
    )
    return pl.pallas_call(
        _combine_kernel,
        grid_spec=grid_spec,
        out_shape=jax.ShapeDtypeStruct((t, d), F32),
        compiler_params=pltpu.CompilerParams(
            dimension_semantics=("arbitrary",), vmem_limit_bytes=VMEM_LIMIT),
        name="combine",
    )(meta, ys, route, x_mid, gt_f, g_post)


def kernel(x, c, w_ada, b_ada, g_pre_mix, g_post_mix, g_pre_ffn, g_post_ffn, w_in, w_pool_group, pool_scale, w_branch_pool, w_branch_att, w_out, w_group_router, b_group_router, w_expert_router, b_expert_router, w_exp_gate, w_exp_up, w_exp_down):
    b, s, d = x.shape
    assert d == D_MODEL and s % (ATT_BLOCK * 2 * ATT_DILATIONS[2]) == 0 and s % ROW_TILE == 0
    assert (b * s) % MOE_BLOCK == 0
    depth = w_ada.shape[0]
    slopes = jnp.exp2(-ALIBI_MAX_BIAS * jnp.arange(1, N_ATT_HEADS + 1, dtype=F32) / N_ATT_HEADS)
    q_scale = HEAD_DIM ** -0.5 * LOG2_E

    for layer in range(depth):
        mod = _adaln(c, w_ada[layer], b_ada[layer]).reshape(b, 6, 1, d)
        sh_m, sc_m, gt_m, sh_f, sc_f, gt_f = [mod[:, j] for j in range(6)]

        wl = w_in[layer]
        q_lo, k_lo, v_lo = POOL_WIDTH, POOL_WIDTH + 768, POOL_WIDTH + 2 * 768
        group_cols = []
        for g in range(3):
            sl = slice(g * 256, (g + 1) * 256)
            group_cols += [wl[:, q_lo:k_lo][:, sl] * q_scale, wl[:, k_lo:v_lo][:, sl],
                           wl[:, v_lo:v_lo + 768][:, sl]]
        w_perm = jnp.concatenate([wl[:, :POOL_WIDTH]] + group_cols + [wl[:, v_lo + 768:]],
                                 axis=1).astype(BF16)

        u, qkv0, qkv1, qkv2, gates = _inproj(x, g_pre_mix[layer].reshape(1, d), sc_m, sh_m, w_perm)
        att = _attention(qkv0, qkv1, qkv2, slopes)

        zero_block = jnp.zeros((POOL_GROUP_DIM, POOL_GROUP_DIM), F32)
        wpg_diag = jnp.block([[w_pool_group[layer, g] if g == h else zero_block
                               for h in range(len(POOL_WINDOWS))] for g in range(len(POOL_WINDOWS))])

        pad_rows = ROUTER_ROWS - N_EXPERTS - N_EXPERT_GROUPS
        wrt = jnp.concatenate([w_expert_router[layer].T, w_group_router[layer].T,
                               jnp.zeros((pad_rows, d), F32)], axis=0)
        wrt_hi = wrt.astype(BF16)
        wr = jnp.concatenate([wrt_hi, (wrt - wrt_hi.astype(F32)).astype(BF16)], axis=0)
        br = jnp.concatenate([b_expert_router[layer], b_group_router[layer],
                              jnp.zeros((pad_rows,), F32)]).reshape(ROUTER_ROWS, 1)

        x_mid, route, meta, counts, xs = _mixtail(
            u, att, gates, x, gt_m, sc_f, sh_f,
            g_post_mix[layer].reshape(1, d), g_pre_ffn[layer].reshape(1, d),
            wpg_diag.astype(BF16), pool_scale[layer].reshape(1, POOL_WIDTH),
            w_branch_pool[layer].astype(BF16), w_branch_att[layer].astype(BF16),
            w_out[layer].astype(BF16), wr, br)

        ys = _experts(xs, counts[0, :N_EXPERTS], w_exp_gate[layer], w_exp_up[layer], w_exp_down[layer])
        run_meta = meta[:, 0:2, 0:N_EXPERTS].reshape(-1)
        x = _combine(run_meta, ys, route, x_mid.reshape(b * s, d), gt_f,
                     g_post_ffn[layer].reshape(1, d), s).reshape(b, s, d)
    return x
```

```python
import functools

import jax
import jax.numpy as jnp
from jax import lax
from jax.experimental import pallas as pl
from jax.experimental.pallas import tpu as pltpu

F32 = jnp.float32
BF16 = jnp.bfloat16
I32 = jnp.int32

D_MODEL = 1024
LANES = 128
SUBLANES = 8
ROW_WORDS = D_MODEL // (2 * LANES)
HI_HALF = -65536

POOL_WINDOWS = (2, 4, 8, 16)
POOL_GROUP_DIM = 128
POOL_WIDTH = 512
POOL_HALO = 16
POOL_PAD = POOL_HALO + 8

HEAD_DIM = 64
ATT_DILATIONS = (1, 4, 16)
ATT_REACH = 128
ATT_BLOCK = 128
ATT_UNROLL = 4
HEADS_PER_GROUP = 4
N_ATT_HEADS = 12
GROUP_QKV = 3 * HEADS_PER_GROUP * HEAD_DIM
ATT_OUT_WIDTH = 256
ALIBI_MAX_BIAS = 8.0
IN_WIDTH = POOL_WIDTH + 3 * GROUP_QKV + 2 * D_MODEL
MASKED = -1e30
LOG2_E = 1.4426950408889634

N_EXPERT_GROUPS = 4
EXPERTS_PER_GROUP = 8
N_EXPERTS = 32
ROUTER_ROWS = 48
D_EXPERT = 512
RMS_EPS = 1e-6

ROW_TILE = 1024
ROW_SPLIT = 4
SORT_TOKENS = 256
SORT_GROUPS = ROW_TILE // SORT_TOKENS
SORT_ROWS = 2 * SORT_TOKENS
MOE_BLOCK = 256
EXPERT_CHAINS = 2
BLOCK_DMA_PRIORITY = 1
EXPERT_X_BUFFERS = 6
EXPERT_Y_BUFFERS = 4
COMBINE_BUFFERS = 3
VMEM_LIMIT = 52 * 1024 * 1024


def _sigmoid(x):
    return 0.5 * jnp.tanh(0.5 * x) + 0.5


def _rmsnorm(x, g):
    return x * lax.rsqrt(jnp.mean(x * x, axis=-1, keepdims=True) + RMS_EPS) * g


def _dot(a, b):
    return jnp.dot(a, b, preferred_element_type=F32)


def _dot_nt(a, b, **kw):
    return lax.dot_general(a, b, (((1,), (1,)), ((), ())), preferred_element_type=F32, **kw)


def _pack_rows(ref, w, n, lo, hi):
    word = (lax.shift_right_logical(pltpu.bitcast(lo, I32), 16) | (pltpu.bitcast(hi, I32) & HI_HALF))
    ref[pl.ds(w, n, stride=ROW_WORDS), :] = word


def _unpack_rows(ref, n):
    cols = []
    for w in range(ROW_WORDS):
        word = ref[pl.ds(w, n, stride=ROW_WORDS), :]
        cols += [pltpu.bitcast(word << 16, F32), pltpu.bitcast(word & HI_HALF, F32)]
    return jnp.concatenate(cols, axis=1)


def _bf16_exact(x):
    return x.astype(BF16).astype(F32)


def _run_staggered(stages, n_chains, lag):
    state = [{} for _ in range(n_chains)]
    for tick in range(len(stages) + lag * (n_chains - 1)):
        for chain in range(n_chains):
            if 0 <= tick - lag * chain < len(stages):
                stages[tick - lag * chain](chain, state[chain])
    return state


def _region_rows(n_tokens):
    return n_tokens + MOE_BLOCK


def _run_copies(n, src, src_row, dst, dst_row, sem):
    @pl.when(n > 0)
    def _():
        pltpu.make_async_copy(
            src.at[pl.ds(src_row * ROW_WORDS, n * ROW_WORDS), :],
            dst.at[pl.ds(dst_row * ROW_WORDS, n * ROW_WORDS), :], sem).start()


def _tile_run_copies(counts, src, src_rows, dst, dst_rows, sem):
    local = 0
    for e in range(N_EXPERTS):
        n = counts(e)
        _run_copies(n, src, local if src_rows is None else src_rows(e),
                    dst, local if dst_rows is None else dst_rows(e), sem)
        local = local + n


def _adaln_kernel(c_ref, w_ref, b_ref, o_ref):
    c = c_ref[...]
    a = c * _sigmoid(c)
    w = w_ref[...]
    a_hi, w_hi = a.astype(BF16), w.astype(BF16)
    a_lo = (a - a_hi.astype(F32)).astype(BF16)
    w_lo = (w - w_hi.astype(F32)).astype(BF16)
    o_ref[...] = _dot(a_hi, w_hi) + (_dot(a_hi, w_lo) + _dot(a_lo, w_hi)) + b_ref[...]


def _adaln(c, w_ada, b_ada):
    b, d = c.shape
    n = w_ada.shape[1]
    rows = -(-b // SUBLANES) * SUBLANES
    cp = jnp.pad(c, ((0, rows - b), (0, 0)))
    nt = 1536
    out = pl.pallas_call(
        _adaln_kernel,
        grid=(n // nt,),
        in_specs=[pl.BlockSpec((rows, d), lambda j: (0, 0)),
                  pl.BlockSpec((d, nt), lambda j: (0, j)),
                  pl.BlockSpec((1, nt), lambda j: (0, j))],
        out_specs=pl.BlockSpec((rows, nt), lambda j: (0, j)),
        out_shape=jax.ShapeDtypeStruct((rows, n), F32),
        compiler_params=pltpu.CompilerParams(vmem_limit_bytes=VMEM_LIMIT),
        name="adaln",
    )(cp, w_ada, b_ada.reshape(1, n))
    return out[:b]


def _inproj_kernel(x_ref, g_ref, sc_ref, sh_ref, w_ref,
                   u_ref, qkv0_ref, qkv1_ref, qkv2_ref, gate_ref, h_scr, p_scr):
    tm = x_ref.shape[1]
    hm = tm // ROW_SPLIT
    for part in range(ROW_SPLIT):
        rows = slice(part * hm, (part + 1) * hm)
        h = _rmsnorm(x_ref[0, rows, :], g_ref[...] * (1.0 + sc_ref[0])) + sh_ref[0]
        h_scr[rows, :] = h.astype(BF16)
        hb = h_scr[rows, :]

        u_ref[0, rows, :] = _dot(hb, w_ref[:, 0:POOL_WIDTH]).astype(BF16)

        col = POOL_WIDTH
        qkv0_ref[0, 0, rows, :] = _dot(hb, w_ref[:, col:col + GROUP_QKV]).astype(BF16)
        for gi, (out_ref, d) in enumerate(((qkv1_ref, ATT_DILATIONS[1]), (qkv2_ref, ATT_DILATIONS[2]))):
            col += GROUP_QKV
            proj = _dot(hb, w_ref[:, col:col + GROUP_QKV])
            stage = p_scr.at[part, gi]
            for cb in range(GROUP_QKV // LANES):
                stage[cb] = proj[:, cb * LANES:(cb + 1) * LANES]
            sub = hm // d
            for r in range(d):
                out_ref[0, r, part * sub:(part + 1) * sub, :] = jnp.concatenate(
                    [stage[cb, pl.ds(r, sub, stride=d), :] for cb in range(GROUP_QKV // LANES)],
                    axis=1).astype(BF16)
        col += GROUP_QKV

        chunk = 512
        for j in range(2 * D_MODEL // chunk):
            g = _dot(hb, w_ref[:, col + j * chunk:col + (j + 1) * chunk])
            gate_ref[0, rows, j * chunk:(j + 1) * chunk] = _sigmoid(g).astype(BF16)


def _inproj(x, g_pre, sc, sh, w_perm):
    b, s, d = x.shape
    tm = ROW_TILE
    d1, d2 = ATT_DILATIONS[1], ATT_DILATIONS[2]
    grid = (b, s // tm)
    const = lambda bi, i: (0, 0)
    per_b = lambda bi, i: (bi, 0, 0)
    return pl.pallas_call(
        _inproj_kernel,
        grid=grid,
        in_specs=[pl.BlockSpec((1, tm, d), lambda bi, i: (bi, i, 0)),
                  pl.BlockSpec((1, d), const),
                  pl.BlockSpec((1, 1, d), per_b),
                  pl.BlockSpec((1, 1, d), per_b),
                  pl.BlockSpec((d, IN_WIDTH), const, pipeline_mode=pl.Buffered(1))],
        out_specs=[pl.BlockSpec((1, tm, POOL_WIDTH), lambda bi, i: (bi, i, 0)),
                   pl.BlockSpec((1, 1, tm, GROUP_QKV), lambda bi, i: (bi, 0, i, 0)),
                   pl.BlockSpec((1, d1, tm // d1, GROUP_QKV), lambda bi, i: (bi, 0, i, 0)),
                   pl.BlockSpec((1, d2, tm // d2, GROUP_QKV), lambda bi, i: (bi, 0, i, 0)),
                   pl.BlockSpec((1, tm, 2 * D_MODEL), lambda bi, i: (bi, i, 0))],
        out_shape=[jax.ShapeDtypeStruct((b, s, POOL_WIDTH), BF16),
                   jax.ShapeDtypeStruct((b, 1, s, GROUP_QKV), BF16),
                   jax.ShapeDtypeStruct((b, d1, s // d1, GROUP_QKV), BF16),
                   jax.ShapeDtypeStruct((b, d2, s // d2, GROUP_QKV), BF16),
                   jax.ShapeDtypeStruct((b, s, 2 * D_MODEL), BF16)],
        scratch_shapes=[pltpu.VMEM((tm, d), BF16),
                        pltpu.VMEM((ROW_SPLIT, 2, GROUP_QKV // LANES, tm // ROW_SPLIT, LANES), F32)],
        compiler_params=pltpu.CompilerParams(
            dimension_semantics=("arbitrary", "arbitrary"), vmem_limit_bytes=VMEM_LIMIT),
        name="inproj",
    )(x, g_pre, sc, sh, w_perm)


def _attn_kernel(slopes_ref, q0, k0, v0, q1, k1, v1, q2, k2, v2, o_ref,
                 bias_scr, acc, mst, lst, s_even, s_odd):
    pair = pl.program_id(1)
    seq = o_ref.shape[1]
    nblk = seq // ATT_BLOCK
    lane = lax.broadcasted_iota(I32, (ATT_BLOCK, LANES), 1)
    first_head = lane < HEAD_DIM
    half_lane = lax.broadcasted_iota(I32, (ATT_BLOCK // 2, LANES), 1)
    head_bits = (jnp.where(half_lane < HEAD_DIM, -1, 0), jnp.where(half_lane < HEAD_DIM, 0, -1))
    ones_block = jnp.ones((2 * ATT_BLOCK, LANES), BF16)

    qi = lax.broadcasted_iota(I32, (ATT_BLOCK, 2 * ATT_BLOCK), 0)
    kj = lax.broadcasted_iota(I32, (ATT_BLOCK, 2 * ATT_BLOCK), 1)
    delta = ATT_BLOCK + qi - kj
    valid = (delta >= 0) & (delta <= ATT_REACH)
    delta0 = qi - kj
    valid0 = delta0 >= 0
    for g, d in enumerate(ATT_DILATIONS):
        for j in range(2):
            slope = slopes_ref[g * HEADS_PER_GROUP + 2 * pair + j]
            slope = slope * LOG2_E
            bias_scr[g, j, 0] = jnp.where(valid0, -slope * (delta0 * d).astype(F32), MASKED)
            bias_scr[g, j, 1] = jnp.where(valid, -slope * (delta * d).astype(F32), MASKED)

    def run_group(g, q_ref, k_ref, v_ref):
        d = ATT_DILATIONS[g]
        per_res = nblk // d
        n_iter = nblk // ATT_UNROLL

        def block_index(it, k):
            n = it * ATT_UNROLL + k
            return n // per_res, n % per_res

        def key_rows(i):
            lo = jnp.maximum(i - 1, 0)
            return pl.ds(pl.multiple_of(lo * ATT_BLOCK, ATT_BLOCK), 2 * ATT_BLOCK)

        def scores(it, s_ref):
            for k in range(ATT_UNROLL):
                r, i = block_index(it, k)
                q = q_ref[0, r, pl.ds(pl.multiple_of(i * ATT_BLOCK, ATT_BLOCK), ATT_BLOCK), :]
                kw = k_ref[0, r, key_rows(i), :]
                qbits = pltpu.bitcast(q, I32)
                for j in range(2):
                    qh = pltpu.bitcast(qbits & head_bits[j], BF16)
                    s_ref[2 * k + j] = _dot_nt(qh, kw) + bias_scr[g, j, jnp.minimum(i, 1)]

        def weighted_values(it, s_ref):
            for k in range(ATT_UNROLL):
                r, i = block_index(it, k)
                vw = jnp.concatenate([v_ref[0, r, key_rows(i), :], ones_block], axis=1)
                outs = []
                for j in range(2):
                    sc = s_ref[2 * k + j]
                    mx = jnp.max(sc, axis=-1, keepdims=True)
                    p = jnp.exp2(sc - mx)
                    num_den = _dot(p.astype(BF16), vw)
                    outs.append((num_den[:, :LANES], mx, num_den[:, LANES:]))
                (n0, m0, l0), (n1, m1, l1) = outs
                if d == 1:
                    rows = pl.ds(pl.multiple_of(i * ATT_BLOCK, ATT_BLOCK), ATT_BLOCK)
                else:
                    rows = pl.ds(i * (ATT_BLOCK * d) + r, ATT_BLOCK, stride=d)
                acc[g, rows, :] = jnp.where(first_head, n0, n1)
                mst[g, rows, :] = jnp.where(first_head, m0, m1)
                lst[g, rows, :] = jnp.where(first_head, l0, l1)

        scores(0, s_even)

        def body(h, carry):
            it = 2 * h
            scores(it + 1, s_odd)
            weighted_values(it, s_even)
            scores(jnp.minimum(it + 2, n_iter - 1), s_even)
            weighted_values(it + 1, s_odd)
            return carry

        lax.fori_loop(0, n_iter // 2, body, 0)

    run_group(0, q0, k0, v0)
    run_group(1, q1, k1, v1)
    run_group(2, q2, k2, v2)

    def finish(it, carry):
        for k in range(ATT_UNROLL):
            rows = pl.ds(pl.multiple_of((it * ATT_UNROLL + k) * ATT_BLOCK, ATT_BLOCK), ATT_BLOCK)
            ms = [mst[g, rows, :] for g in range(3)]
            top = jnp.maximum(jnp.maximum(ms[0], ms[1]), ms[2])
            scale = [jnp.exp2(m - top) for m in ms]
            num = scale[0] * acc[0, rows, :] + scale[1] * acc[1, rows, :] + scale[2] * acc[2, rows, :]
            den = scale[0] * lst[0, rows, :] + scale[1] * lst[1, rows, :] + scale[2] * lst[2, rows, :]
            o_ref[0, rows, :] = (num / den).astype(BF16)
        return carry

    lax.fori_loop(0, nblk // ATT_UNROLL, finish, 0)


def _attention(qkv0, qkv1, qkv2, slopes):
    b, _, s, _ = qkv0.shape
    pairs = HEADS_PER_GROUP // 2
    col_blocks = HEADS_PER_GROUP * HEAD_DIM // LANES

    def specs(arr):
        _, d, sub, _ = arr.shape
        return [pl.BlockSpec((1, d, sub, LANES),
                             functools.partial(lambda bi, p, sec: (bi, 0, 0, sec * col_blocks + p), sec=sec))
                for sec in range(3)]

    return pl.pallas_call(
        _attn_kernel,
        grid=(b, pairs),
        in_specs=[pl.BlockSpec(memory_space=pltpu.SMEM)] + specs(qkv0) + specs(qkv1) + specs(qkv2),
        out_specs=pl.BlockSpec((1, s, LANES), lambda bi, p: (bi, 0, p)),
        out_shape=jax.ShapeDtypeStruct((b, s, ATT_OUT_WIDTH), BF16),
        scratch_shapes=[pltpu.VMEM((3, 2, 2, ATT_BLOCK, 2 * ATT_BLOCK), F32),
                        pltpu.VMEM((3, s, LANES), F32),
                        pltpu.VMEM((3, s, LANES), F32),
                        pltpu.VMEM((3, s, LANES), F32),
                        pltpu.VMEM((2 * ATT_UNROLL, ATT_BLOCK, 2 * ATT_BLOCK), F32),
                        pltpu.VMEM((2 * ATT_UNROLL, ATT_BLOCK, 2 * ATT_BLOCK), F32)],
        compiler_params=pltpu.CompilerParams(
            dimension_semantics=("arbitrary", "arbitrary"), vmem_limit_bytes=VMEM_LIMIT),
        name="attention",
    )(slopes, qkv0, qkv0, qkv0, qkv1, qkv1, qkv1, qkv2, qkv2, qkv2)


def _mixtail_kernel(u_ref, halo_ref, att_ref, gate_ref, x_ref,
                    gt_m_ref, sc_f_ref, sh_f_ref, g_post_ref, g_pre_ref,
                    wpg_ref, pscale_ref, wbp_ref, wba_ref, wout_ref, wr_ref, br_ref,
                    xmid_ref, route_ref, meta_ref, counts_ref, xs_hbm,
                    pu, lv, xbuf, zbuf, fill, meta_s, sem, sem_s):
    i = pl.program_id(1)
    tm = x_ref.shape[1]
    step = pl.program_id(0) * pl.num_programs(1) + i
    last = pl.num_programs(0) * pl.num_programs(1) - 1
    slot = step % 2
    region = _region_rows(pl.num_programs(0) * pl.num_programs(1) * tm)

    @pl.when(step == 0)
    def _():
        fill[...] = jnp.zeros_like(fill)
        pu[0:POOL_PAD - POOL_HALO, :] = jnp.zeros((POOL_PAD - POOL_HALO, POOL_WIDTH), F32)
        lv[:, :, 0:POOL_PAD - POOL_HALO, :] = jnp.zeros(
            (SORT_GROUPS, 2, POOL_PAD - POOL_HALO, POOL_GROUP_DIM), F32)
        for sub in range(SORT_GROUPS):
            for e in range(N_EXPERTS):
                meta_s[SORT_GROUPS + sub, 1, e] = 0
                meta_s[SORT_GROUPS + sub, 0, e] = 0

    def send_tile(which):
        for sub in range(SORT_GROUPS):
            m = which * SORT_GROUPS + sub
            _tile_run_copies(lambda e: meta_s[m, 1, e], xbuf.at[which, sub], None,
                             xs_hbm, lambda e: e * region + meta_s[m, 0, e], sem.at[which])

    def wait_tile(which):
        for sub in range(SORT_GROUPS):
            pltpu.make_async_copy(xbuf.at[which, sub], xs_hbm.at[pl.ds(0, SORT_ROWS * ROW_WORDS), :],
                                  sem.at[which]).wait()

    @pl.when(step > 1)
    def _():
        wait_tile(slot)

    send_tile(1 - slot)

    head = POOL_PAD - POOL_HALO
    halo = halo_ref[0].astype(F32)
    pu[head:POOL_PAD, :] = jnp.where(i > 0, halo, jnp.zeros_like(halo))
    pu[POOL_PAD:POOL_PAD + tm, :] = u_ref[0].astype(F32)

    ts = SORT_TOKENS

    def pool(sub, st):
        base = sub * ts
        t = i * tm + base + lax.broadcasted_iota(I32, (ts, 1), 0)
        pooled_groups = []
        for g, w in enumerate(POOL_WINDOWS):
            cols = slice(g * POOL_GROUP_DIM, (g + 1) * POOL_GROUP_DIM)
            read = lambda start, n: pu[pl.ds(base + start, n), cols]
            shift, level = 1, 0
            while 2 * shift < w:
                partial = read(head, ts + POOL_HALO) + read(head - shift, ts + POOL_HALO)
                buf = lv.at[sub, level % 2]
                buf[pl.ds(head, ts + POOL_HALO), :] = partial
                read = lambda start, n, buf=buf: buf[pl.ds(start, n), :]
                shift, level = 2 * shift, level + 1
            win = read(POOL_PAD, ts) + read(POOL_PAD - shift, ts)
            count = jnp.minimum(t + 1, w).astype(F32)
            pooled_groups.append((win / count - pu[pl.ds(base + POOL_PAD, ts), cols]).astype(BF16))
        st["pooled"] = jnp.concatenate(pooled_groups, axis=1)

    neg_inf = -jnp.inf
    far = float(LANES)
    before = (lax.broadcasted_iota(I32, (ts, ts), 0) < lax.broadcasted_iota(I32, (ts, ts), 1))
    earlier = jnp.where(before, 1.0, 0.0).astype(BF16)
    erow = lax.broadcasted_iota(I32, (N_EXPERTS, ts), 0).astype(F32)
    e_col = lax.broadcasted_iota(I32, (N_EXPERTS, LANES), 0)
    e_lane = lax.broadcasted_iota(I32, (N_EXPERTS, LANES), 1)
    row8 = lax.broadcasted_iota(I32, (SUBLANES, ts), 0)
    mrow = lax.broadcasted_iota(I32, (SUBLANES, LANES), 0)
    srow = lax.broadcasted_iota(I32, (SORT_ROWS, ts), 0).astype(F32)
    filled = fill[...]

    def rows_of(sub):
        return slice(sub * ts, (sub + 1) * ts)

    def branch_projections(sub, st):
        rows = rows_of(sub)
        mixed = _dot(st.pop("pooled"), wpg_ref[...]) * pscale_ref[...]
        st["y_pool"] = _dot(mixed.astype(BF16), wbp_ref[...])
        st["y_att"] = _dot(att_ref[0, rows, :], wba_ref[...])

    def gated_sum(sub, st):
        rows = rows_of(sub)
        st["merged"] = (gate_ref[0, rows, 0:D_MODEL] * st.pop("y_pool").astype(BF16)
                        + gate_ref[0, rows, D_MODEL:2 * D_MODEL] * st.pop("y_att").astype(BF16))

    def output_projection(sub, st):
        st["y"] = _dot(st.pop("merged"), wout_ref[...])

    def residual_and_ffn_input(sub, st):
        rows = rows_of(sub)
        x_mid = x_ref[0, rows, :] + _rmsnorm(st.pop("y"), gt_m_ref[0] * g_post_ref[...])
        xmid_ref[0, rows, :] = x_mid
        h2 = _rmsnorm(x_mid, g_pre_ref[...] * (1.0 + sc_f_ref[0])) + sh_f_ref[0]
        st["h2b"] = h2.astype(BF16)
        st["h2lo"] = (h2 - st["h2b"].astype(F32)).astype(BF16)

    def router_logits(sub, st):
        by_hi = _dot_nt(wr_ref[...], st["h2b"])
        st["logits"] = (by_hi[0:ROUTER_ROWS] + (by_hi[ROUTER_ROWS:2 * ROUTER_ROWS]
                        + _dot_nt(wr_ref[0:ROUTER_ROWS, :], st.pop("h2lo"))) + br_ref[...])

    def route(sub, st):
        logits = st.pop("logits")
        gl = logits[N_EXPERTS:N_EXPERTS + N_EXPERT_GROUPS, :]
        grow = lax.broadcasted_iota(I32, gl.shape, 0).astype(F32)
        gmax = jnp.max(gl, axis=0, keepdims=True)
        gsel = jnp.min(jnp.where(gl == gmax, grow, far), axis=0, keepdims=True)
        p_group = 1.0 / jnp.sum(jnp.exp(gl - gmax), axis=0, keepdims=True)
        e_lo = gsel * float(EXPERTS_PER_GROUP)
        el = jnp.where((erow >= e_lo) & (erow < e_lo + float(EXPERTS_PER_GROUP)),
                       logits[0:N_EXPERTS, :], neg_inf)
        v1 = jnp.max(el, axis=0, keepdims=True)
        i1 = jnp.min(jnp.where(el == v1, erow, far), axis=0, keepdims=True)
        el2 = jnp.where(erow == i1, neg_inf, el)
        v2 = jnp.max(el2, axis=0, keepdims=True)
        i2 = jnp.min(jnp.where(el2 == v2, erow, far), axis=0, keepdims=True)
        e21 = jnp.exp(v2 - v1)
        st["w1"] = p_group / (1.0 + e21)
        st["w2"] = p_group * e21 / (1.0 + e21)
        st["pick1"] = erow == i1
        st["pick2"] = erow == i2
        st["assign"] = jnp.where(st["pick1"] | st["pick2"], 1.0, 0.0)

    def count_and_rank(sub, st):
        assign = st.pop("assign")
        st["rank"] = _dot(assign.astype(BF16), earlier)
        assign_pad = jnp.concatenate([assign, jnp.zeros((LANES - N_EXPERTS, ts), F32)], axis=0).astype(BF16)
        st["cnt_row"] = _dot_nt(jnp.ones((SUBLANES, ts), BF16), assign_pad)

    def sorted_positions(sub, st):
        cnt_row = st["cnt_row"]
        run_start = jnp.sum(jnp.where(e_lane < e_col, cnt_row[0:1, :], 0.0), axis=1, keepdims=True)
        pos = st.pop("rank") + run_start
        key1 = jnp.sum(jnp.where(st.pop("pick1"), pos, 0.0), axis=0, keepdims=True)
        key2 = jnp.sum(jnp.where(st.pop("pick2"), pos, 0.0), axis=0, keepdims=True)
        route_ref[:, rows_of(sub)] = jnp.where(row8 == 0, key1, jnp.where(row8 == 1, key2,
                                               jnp.where(row8 == 2, st.pop("w1"),
                                                         jnp.where(row8 == 3, st.pop("w2"), 0.0))))
        st["perm"] = jnp.where((srow == key1) | (srow == key2), 1.0, 0.0).astype(BF16)

    def sort_rows(sub, st):
        perm, h2b = st.pop("perm"), st.pop("h2b")
        for w in range(ROW_WORDS):
            pair = _dot(perm, h2b[:, 2 * w * LANES:(2 * w + 2) * LANES])
            _pack_rows(xbuf.at[slot, sub], w, SORT_ROWS, pair[:, :LANES], pair[:, LANES:])

    state = _run_staggered((pool, branch_projections, gated_sum, output_projection,
                            residual_and_ffn_input, router_logits, route, count_and_rank,
                            sorted_positions, sort_rows), SORT_GROUPS, lag=0)

    for sub in range(SORT_GROUPS):
        cnt_row = state[sub]["cnt_row"]
        meta_ref[sub] = jnp.where(mrow == 0, filled, jnp.where(mrow == 1, cnt_row, 0.0)).astype(I32)
        filled = filled + cnt_row

    fill[...] = filled
    counts_ref[...] = filled.astype(I32)

    meta_copy = pltpu.make_async_copy(meta_ref, meta_s.at[pl.ds(slot * SORT_GROUPS, SORT_GROUPS)], sem_s)
    meta_copy.start()
    meta_copy.wait()

    @pl.when(step == last)
    def _():
        send_tile(slot)

        @pl.when(step > 0)
        def _():
            wait_tile(1 - slot)

        wait_tile(slot)
        zbuf[...] = jnp.zeros_like(zbuf)
        final = slot * SORT_GROUPS + SORT_GROUPS - 1

        def pad_copy(e):
            end = e * region + meta_s[final, 0, e] + meta_s[final, 1, e]
            return pltpu.make_async_copy(
                zbuf, xs_hbm.at[pl.ds(end * ROW_WORDS, MOE_BLOCK * ROW_WORDS), :], sem_s)

        def start_pad(e, carry):
            pad_copy(e).start()
            return carry

        def wait_pad(e, carry):
            pad_copy(e).wait()
            return carry

        lax.fori_loop(0, N_EXPERTS, start_pad, 0)
        lax.fori_loop(0, N_EXPERTS, wait_pad, 0)


def _mixtail(u, att, gates, x, gt_m, sc_f, sh_f, g_post, g_pre,
             wpg, pscale, wbp, wba, wout, wr, br):
    b, s, d = x.shape
    tm = ROW_TILE
    tiles = s // tm
    n_tiles = b * tiles
    halo_blocks = tm // POOL_HALO
    region = _region_rows(b * s)
    const2 = lambda bi, i: (0, 0)
    const3 = lambda bi, i: (0, 0, 0)
    per_b = lambda bi, i: (bi, 0, 0)
    tile = lambda bi, i: (bi, i, 0)
    single = dict(pipeline_mode=pl.Buffered(1))
    return pl.pallas_call(
        _mixtail_kernel,
        grid=(b, tiles),
        in_specs=[pl.BlockSpec((1, tm, POOL_WIDTH), tile),
                  pl.BlockSpec((1, POOL_HALO, POOL_WIDTH),
                               lambda bi, i: (bi, jnp.maximum(i * halo_blocks - 1, 0), 0)),
                  pl.BlockSpec((1, tm, ATT_OUT_WIDTH), tile),
                  pl.BlockSpec((1, tm, 2 * D_MODEL), tile),
                  pl.BlockSpec((1, tm, d), tile),
                  pl.BlockSpec((1, 1, d), per_b),
                  pl.BlockSpec((1, 1, d), per_b),
                  pl.BlockSpec((1, 1, d), per_b),
                  pl.BlockSpec((1, d), const2),
                  pl.BlockSpec((1, d), const2),
                  pl.BlockSpec(wpg.shape, const2, **single),
                  pl.BlockSpec((1, POOL_WIDTH), const2),
                  pl.BlockSpec(wbp.shape, const2, **single),
                  pl.BlockSpec(wba.shape, const2, **single),
                  pl.BlockSpec(wout.shape, const2, **single),
                  pl.BlockSpec(wr.shape, const2, **single),
                  pl.BlockSpec(br.shape, const2)],
        out_specs=[pl.BlockSpec((1, tm, d), tile),
                   pl.BlockSpec((SUBLANES, tm), lambda bi, i: (0, bi * tiles + i)),
                   pl.BlockSpec((SORT_GROUPS, SUBLANES, LANES), lambda bi, i: (bi * tiles + i, 0, 0)),
                   pl.BlockSpec((SUBLANES, LANES), const2),
                   pl.BlockSpec(memory_space=pl.ANY)],
        out_shape=[jax.ShapeDtypeStruct((b, s, d), F32),
                   jax.ShapeDtypeStruct((SUBLANES, b * s), F32),
                   jax.ShapeDtypeStruct((n_tiles * SORT_GROUPS, SUBLANES, LANES), I32),
                   jax.ShapeDtypeStruct((SUBLANES, LANES), I32),
                   jax.ShapeDtypeStruct((N_EXPERTS * region * ROW_WORDS, LANES), I32)],
        scratch_shapes=[pltpu.VMEM((POOL_PAD + tm, POOL_WIDTH), F32),
                        pltpu.VMEM((SORT_GROUPS, 2, POOL_PAD + SORT_TOKENS, POOL_GROUP_DIM), F32),
                        pltpu.VMEM((2, SORT_GROUPS, SORT_ROWS * ROW_WORDS, LANES), I32),
                        pltpu.VMEM((MOE_BLOCK * ROW_WORDS, LANES), I32),
                        pltpu.VMEM((SUBLANES, LANES), F32),
                        pltpu.SMEM((2 * SORT_GROUPS, SUBLANES, LANES), I32),
                        pltpu.SemaphoreType.DMA((2,)),
                        pltpu.SemaphoreType.DMA(())],
        compiler_params=pltpu.CompilerParams(
            dimension_semantics=("arbitrary", "arbitrary"), vmem_limit_bytes=VMEM_LIMIT),
        name="mixtail",
    )(u, u, att, gates, x, gt_m, sc_f, sh_f, g_post, g_pre,
      wpg, pscale, wbp, wba, wout, wr, br)


def _expert_kernel(counts_ref, xs_hbm, wg_ref, wu_ref, wd_ref, ys_hbm,
                   xbuf, ybuf, state, semx, semy):
    e = pl.program_id(0)
    bm = MOE_BLOCK
    nx, ny = EXPERT_X_BUFFERS, EXPERT_Y_BUFFERS
    block_words = bm * ROW_WORDS
    region = xs_hbm.shape[0] // (N_EXPERTS * ROW_WORDS)

    def n_blocks(ex):
        return (counts_ref[ex] + (bm - 1)) // bm

    def block_rows(ref, ex, k):
        start = pl.multiple_of((ex * region + k * bm) * ROW_WORDS, block_words)
        return ref.at[pl.ds(start, block_words), :]

    def x_copy(ex, k, s):
        return pltpu.make_async_copy(block_rows(xs_hbm, ex, k), xbuf.at[s], semx.at[s])

    def y_copy(k, s):
        return pltpu.make_async_copy(ybuf.at[s], block_rows(ys_hbm, e, k), semy.at[s])

    @pl.when(e == 0)
    def _():
        for j in range(4):
            state[j] = 0

    def fetch_through(target):
        def more(c):
            pe, _, pg = c
            return (pg < target) & (pe < N_EXPERTS)

        def step(c):
            pe, pk, pg = c
            has = pk < n_blocks(pe)

            @pl.when(has)
            def _():
                x_copy(pe, pk, pg % nx).start(priority=BLOCK_DMA_PRIORITY)

            return (jnp.where(has, pe, pe + 1), jnp.where(has, pk + 1, 0), pg + has.astype(I32))

        pe, pk, pg = lax.while_loop(more, step, (state[1], state[2], state[3]))
        state[1] = pe
        state[2] = pk
        state[3] = pg

    nb = n_blocks(e)
    done = state[0]

    def blocks(k0, count):
        ks = [k0 + c for c in range(count)]
        gs = [done + k for k in ks]
        fetch_through(gs[0] + nx)
        for k, g in zip(ks, gs):
            x_copy(e, k, g % nx).wait()

            @pl.when(g >= ny)
            def _():
                y_copy(k, g % ny).wait()

        mids = []
        for g in gs:
            x = _unpack_rows(xbuf.at[g % nx], bm).astype(BF16)
            a = _dot(x, wg_ref[...].astype(BF16))
            u = _dot(x, wu_ref[...].astype(BF16))
            mids.append(((a * _sigmoid(a)) * u).astype(BF16))
        for k, g, mid in zip(ks, gs, mids):
            for w in range(ROW_WORDS):
                pair = _bf16_exact(_dot(mid, wd_ref[:, 2 * w * LANES:(2 * w + 2) * LANES].astype(BF16)))
                _pack_rows(ybuf.at[g % ny], w, bm, pair[:, :LANES], pair[:, LANES:])
            y_copy(k, g % ny).start(priority=BLOCK_DMA_PRIORITY)

    per = EXPERT_CHAINS

    def several_blocks(p, carry):
        blocks(per * p, per)
        return carry

    def single_block(k, carry):
        blocks(k, 1)
        return carry

    lax.fori_loop(0, nb // per, several_blocks, 0)
    lax.fori_loop((nb // per) * per, nb, single_block, 0)

    state[0] = done + nb

    @pl.when(e == N_EXPERTS - 1)
    def _():
        total = done + nb
        for j in range(ny):
            @pl.when(total > j)
            def _():
                y_copy(0, (total - 1 - j) % ny).wait()


def _experts(xs, counts, w_gate, w_up, w_down):
    bm = MOE_BLOCK
    w_in = pl.BlockSpec((None, D_MODEL, D_EXPERT), lambda e, cnt: (e, 0, 0))
    w_out = pl.BlockSpec((None, D_EXPERT, D_MODEL), lambda e, cnt: (e, 0, 0))
    grid_spec = pltpu.PrefetchScalarGridSpec(
        num_scalar_prefetch=1,
        grid=(N_EXPERTS,),
        in_specs=[pl.BlockSpec(memory_space=pl.ANY), w_in, w_in, w_out],
        out_specs=pl.BlockSpec(memory_space=pl.ANY),
        scratch_shapes=[pltpu.VMEM((EXPERT_X_BUFFERS, bm * ROW_WORDS, LANES), I32),
                        pltpu.VMEM((EXPERT_Y_BUFFERS, bm * ROW_WORDS, LANES), I32),
                        pltpu.SMEM((4,), I32),
                        pltpu.SemaphoreType.DMA((EXPERT_X_BUFFERS,)),
                        pltpu.SemaphoreType.DMA((EXPERT_Y_BUFFERS,))],
    )
    return pl.pallas_call(
        _expert_kernel,
        grid_spec=grid_spec,
        out_shape=jax.ShapeDtypeStruct(xs.shape, I32),
        compiler_params=pltpu.CompilerParams(
            dimension_semantics=("arbitrary",), vmem_limit_bytes=VMEM_LIMIT),
        name="experts",
    )(counts, xs, w_gate, w_up, w_down)


def _combine_kernel(meta_ref, y_hbm, route_ref, xmid_ref, gt_ref, g_ref, o_ref, ybuf, rt_scr, sem):
    step = pl.program_id(0)
    n_steps = pl.num_programs(0)
    tm = ROW_TILE
    slot = step % COMBINE_BUFFERS
    region = _region_rows(n_steps * tm)

    def fetch_tile(tile, live):
        tile = jnp.minimum(tile, n_steps - 1)
        which = tile % COMBINE_BUFFERS
        for sub in range(SORT_GROUPS):
            m = (tile * SORT_GROUPS + sub) * 2
            _tile_run_copies(lambda e: jnp.where(live, meta_ref[(m + 1) * N_EXPERTS + e], 0),
                             y_hbm, lambda e: e * region + meta_ref[m * N_EXPERTS + e],
                             ybuf.at[which, sub], None, sem.at[which])

    @pl.when(step == 0)
    def _():
        for ahead in range(COMBINE_BUFFERS - 1):
            fetch_tile(ahead, ahead < n_steps)

    for sub in range(SORT_GROUPS):
        pltpu.make_async_copy(y_hbm.at[pl.ds(0, SORT_ROWS * ROW_WORDS), :], ybuf.at[slot, sub],
                              sem.at[slot]).wait()
    sorted_rows = [_unpack_rows(ybuf.at[slot, sub], SORT_ROWS).astype(BF16) for sub in range(SORT_GROUPS)]

    fetch_tile(step + COMBINE_BUFFERS - 1, step + COMBINE_BUFFERS - 1 < n_steps)

    rt_scr[...] = jnp.zeros_like(rt_scr)
    rt_scr[0:SUBLANES, :] = route_ref[...]
    cols = jnp.concatenate([rt_scr[:, c * LANES:(c + 1) * LANES].T for c in range(tm // LANES)], axis=0)

    ts = SORT_TOKENS
    spos = lax.broadcasted_iota(I32, (ts, SORT_ROWS), 1).astype(F32)
    def rows_of(sub):
        return slice(sub * ts, (sub + 1) * ts)

    def gate_matrix(sub, st):
        key1, key2, w1, w2 = (cols[rows_of(sub), c:c + 1] for c in range(4))
        st["gates"] = jnp.where(spos == key1, w1, jnp.where(spos == key2, w2, 0.0)).astype(BF16)

    def weighted_unsort(sub, st):
        st["y"] = _dot(st.pop("gates"), sorted_rows[sub])

    def norm_and_add(sub, st):
        rows = rows_of(sub)
        o_ref[rows, :] = xmid_ref[rows, :] + _rmsnorm(st.pop("y"), gt_ref[0] * g_ref[...])

    _run_staggered((gate_matrix, weighted_unsort, norm_and_add), SORT_GROUPS, lag=1)


def _combine(meta, ys, route, x_mid, gt_f, g_post, seq):
    t, d = x_mid.shape
    tm = ROW_TILE
    tiles_per_seq = seq // tm
    grid_spec = pltpu.PrefetchScalarGridSpec(
        num_scalar_prefetch=1,
        grid=(t // tm,),
        in_specs=[pl.BlockSpec(memory_space=pl.ANY),
                  pl.BlockSpec((SUBLANES, tm), lambda i, m: (0, i)),
                  pl.BlockSpec((tm, d), lambda i, m: (i, 0)),
                  pl.BlockSpec((1, 1, d), lambda i, m: (i // tiles_per_seq, 0, 0)),
                  pl.BlockSpec((1, d), lambda i, m: (0, 0))],
        out_specs=pl.BlockSpec((tm, d), lambda i, m: (i, 0)),
        scratch_shapes=[pltpu.VMEM((COMBINE_BUFFERS, SORT_GROUPS, SORT_ROWS * ROW_WORDS, LANES), I32),
                        pltpu.VMEM((LANES, tm), F32),
                        pltpu.SemaphoreType.DMA((COMBINE_BUFFERS,))],
    )
    return pl.pallas_call(
        _combine_kernel,
        grid_spec=grid_spec,
        out_shape=jax.ShapeDtypeStruct((t, d), F32),
        compiler_params=pltpu.CompilerParams(
            dimension_semantics=("arbitrary",), vmem_limit_bytes=VMEM_LIMIT),
        name="combine",
    )(meta, ys, route, x_mid, gt_f, g_post)


def kernel(x, c, w_ada, b_ada, g_pre_mix, g_post_mix, g_pre_ffn, g_post_ffn, w_in, w_pool_group, pool_scale, w_branch_pool, w_branch_att, w_out, w_group_router, b_group_router, w_expert_router, b_expert_router, w_exp_gate, w_exp_up, w_exp_down):
    b, s, d = x.shape
    assert d == D_MODEL and s % (ATT_BLOCK * 2 * ATT_DILATIONS[2]) == 0 and s % ROW_TILE == 0
    assert (b * s) % MOE_BLOCK == 0
    depth = w_ada.shape[0]
    slopes = jnp.exp2(-ALIBI_MAX_BIAS * jnp.arange(1, N_ATT_HEADS + 1, dtype=F32) / N_ATT_HEADS)
    q_scale = HEAD_DIM ** -0.5 * LOG2_E

    for layer in range(depth):
        mod = _adaln(c, w_ada[layer], b_ada[layer]).reshape(b, 6, 1, d)
        sh_m, sc_m, gt_m, sh_f, sc_f, gt_f = [mod[:, j] for j in range(6)]

        wl = w_in[layer]
        q_lo, k_lo, v_lo = POOL_WIDTH, POOL_WIDTH + 768, POOL_WIDTH + 2 * 768
        group_cols = []
        for g in range(3):
            sl = slice(g * 256, (g + 1) * 256)
            group_cols += [wl[:, q_lo:k_lo][:, sl] * q_scale, wl[:, k_lo:v_lo][:, sl],
                           wl[:, v_lo:v_lo + 768][:, sl]]
        w_perm = jnp.concatenate([wl[:, :POOL_WIDTH]] + group_cols + [wl[:, v_lo + 768:]],
                                 axis=1).astype(BF16)

        u, qkv0, qkv1, qkv2, gates = _inproj(x, g_pre_mix[layer].reshape(1, d), sc_m, sh_m, w_perm)
        att = _attention(qkv0, qkv1, qkv2, slopes)

        zero_block = jnp.zeros((POOL_GROUP_DIM, POOL_GROUP_DIM), F32)
        wpg_diag = jnp.block([[w_pool_group[layer, g] if g == h else zero_block
                               for h in range(len(POOL_WINDOWS))] for g in range(len(POOL_WINDOWS))])

        pad_rows = ROUTER_ROWS - N_EXPERTS - N_EXPERT_GROUPS
        wrt = jnp.concatenate([w_expert_router[layer].T, w_group_router[layer].T,
                               jnp.zeros((pad_rows, d), F32)], axis=0)
        wrt_hi = wrt.astype(BF16)
        wr = jnp.concatenate([wrt_hi, (wrt - wrt_hi.astype(F32)).astype(BF16)], axis=0)
        br = jnp.concatenate([b_expert_router[layer], b_group_router[layer],
                              jnp.zeros((pad_rows,), F32)]).reshape(ROUTER_ROWS, 1)

        x_mid, route, meta, counts, xs = _mixtail(
            u, att, gates, x, gt_m, sc_f, sh_f,
            g_post_mix[layer].reshape(1, d), g_pre_ffn[layer].reshape(1, d),
            wpg_diag.astype(BF16), pool_scale[layer].reshape(1, POOL_WIDTH),
            w_branch_pool[layer].astype(BF16), w_branch_att[layer].astype(BF16),
            w_out[layer].astype(BF16), wr, br)

        ys = _experts(xs, counts[0, :N_EXPERTS], w_exp_gate[layer], w_exp_up[layer], w_exp_down[layer])
        run_meta = meta[:, 0:2, 0:N_EXPERTS].reshape(-1)
        x = _combine(run_meta, ys, route, x_mid.reshape(b * s, d), gt_f,
                     g_post_ffn[layer].reshape(1, d), s).reshape(b, s, d)
    return x
```

```python
import functools

import jax
import jax.numpy as jnp
from jax import lax
from jax.experimental import pallas as pl
from jax.experimental.pallas import tpu as pltpu

F32 = jnp.float32
BF16 = jnp.bfloat16
I32 = jnp.int32

D_MODEL = 1024
LANES = 128
SUBLANES = 8
ROW_WORDS = D_MODEL // (2 * LANES)
HI_HALF = -65536

POOL_WINDOWS = (2, 4, 8, 16)
POOL_GROUP_DIM = 128
POOL_WIDTH = 512
POOL_HALO = 16
POOL_PAD = POOL_HALO + 8

HEAD_DIM = 64
ATT_DILATIONS = (1, 4, 16)
ATT_REACH = 128
ATT_BLOCK = 128
ATT_UNROLL = 4
HEADS_PER_GROUP = 4
N_ATT_HEADS = 12
GROUP_QKV = 3 * HEADS_PER_GROUP * HEAD_DIM
ATT_OUT_WIDTH = 256
ALIBI_MAX_BIAS = 8.0
IN_WIDTH = POOL_WIDTH + 3 * GROUP_QKV + 2 * D_MODEL
MASKED = -1e30
LOG2_E = 1.4426950408889634

N_EXPERT_GROUPS = 4
EXPERTS_PER_GROUP = 8
N_EXPERTS = 32
ROUTER_ROWS = 48
D_EXPERT = 512
RMS_EPS = 1e-6

ROW_TILE = 1024
ROW_SPLIT = 4
SORT_TOKENS = 256
SORT_GROUPS = ROW_TILE // SORT_TOKENS
SORT_ROWS = 2 * SORT_TOKENS
MOE_BLOCK = 256
EXPERT_CHAINS = 2
EXPERT_X_BUFFERS = 6
EXPERT_Y_BUFFERS = 4
COMBINE_BUFFERS = 3
VMEM_LIMIT = 52 * 1024 * 1024


def _sigmoid(x):
    return 0.5 * jnp.tanh(0.5 * x) + 0.5


def _rmsnorm(x, g):
    return x * lax.rsqrt(jnp.mean(x * x, axis=-1, keepdims=True) + RMS_EPS) * g


def _dot(a, b):
    return jnp.dot(a, b, preferred_element_type=F32)


def _dot_nt(a, b):
    return lax.dot_general(a, b, (((1,), (1,)), ((), ())), preferred_element_type=F32)


def _pack_rows(ref, w, n, lo, hi):
    word = (lax.shift_right_logical(pltpu.bitcast(lo, I32), 16) | (pltpu.bitcast(hi, I32) & HI_HALF))
    ref[pl.ds(w, n, stride=ROW_WORDS), :] = word


def _unpack_rows(ref, n):
    cols = []
    for w in range(ROW_WORDS):
        word = ref[pl.ds(w, n, stride=ROW_WORDS), :]
        cols += [pltpu.bitcast(word << 16, F32), pltpu.bitcast(word & HI_HALF, F32)]
    return jnp.concatenate(cols, axis=1)


def _bf16_exact(x):
    return x.astype(BF16).astype(F32)


def _run_staggered(stages, n_chains, lag):
    state = [{} for _ in range(n_chains)]
    for tick in range(len(stages) + lag * (n_chains - 1)):
        for chain in range(n_chains):
            if 0 <= tick - lag * chain < len(stages):
                stages[tick - lag * chain](chain, state[chain])
    return state


def _region_rows(n_tokens):
    return n_tokens + MOE_BLOCK


def _run_copies(n, src, src_row, dst, dst_row, sem):
    @pl.when(n > 0)
    def _():
        pltpu.make_async_copy(
            src.at[pl.ds(src_row * ROW_WORDS, n * ROW_WORDS), :],
            dst.at[pl.ds(dst_row * ROW_WORDS, n * ROW_WORDS), :], sem).start()


def _tile_run_copies(counts, src, src_rows, dst, dst_rows, sem):
    local = 0
    for e in range(N_EXPERTS):
        n = counts(e)
        _run_copies(n, src, local if src_rows is None else src_rows(e),
                    dst, local if dst_rows is None else dst_rows(e), sem)
        local = local + n


def _adaln_kernel(c_ref, w_ref, b_ref, o_ref):
    c = c_ref[...]
    a = c * _sigmoid(c)
    w = w_ref[...]
    a_hi, w_hi = a.astype(BF16), w.astype(BF16)
    a_lo = (a - a_hi.astype(F32)).astype(BF16)
    w_lo = (w - w_hi.astype(F32)).astype(BF16)
    o_ref[...] = _dot(a_hi, w_hi) + (_dot(a_hi, w_lo) + _dot(a_lo, w_hi)) + b_ref[...]


def _adaln(c, w_ada, b_ada):
    b, d = c.shape
    n = w_ada.shape[1]
    rows = -(-b // SUBLANES) * SUBLANES
    cp = jnp.pad(c, ((0, rows - b), (0, 0)))
    nt = 1536
    out = pl.pallas_call(
        _adaln_kernel,
        grid=(n // nt,),
        in_specs=[pl.BlockSpec((rows, d), lambda j: (0, 0)),
                  pl.BlockSpec((d, nt), lambda j: (0, j)),
                  pl.BlockSpec((1, nt), lambda j: (0, j))],
        out_specs=pl.BlockSpec((rows, nt), lambda j: (0, j)),
        out_shape=jax.ShapeDtypeStruct((rows, n), F32),
        compiler_params=pltpu.CompilerParams(vmem_limit_bytes=VMEM_LIMIT),
        name="adaln",
    )(cp, w_ada, b_ada.reshape(1, n))
    return out[:b]


def _inproj_kernel(x_ref, g_ref, sc_ref, sh_ref, w_ref,
                   u_ref, qkv0_ref, qkv1_ref, qkv2_ref, gate_ref, h_scr, p_scr):
    tm = x_ref.shape[1]
    hm = tm // ROW_SPLIT
    for part in range(ROW_SPLIT):
        rows = slice(part * hm, (part + 1) * hm)
        h = _rmsnorm(x_ref[0, rows, :], g_ref[...] * (1.0 + sc_ref[0])) + sh_ref[0]
        h_scr[rows, :] = h.astype(BF16)
        hb = h_scr[rows, :]

        u_ref[0, rows, :] = _dot(hb, w_ref[:, 0:POOL_WIDTH]).astype(BF16)

        col = POOL_WIDTH
        qkv0_ref[0, 0, rows, :] = _dot(hb, w_ref[:, col:col + GROUP_QKV]).astype(BF16)
        for gi, (out_ref, d) in enumerate(((qkv1_ref, ATT_DILATIONS[1]), (qkv2_ref, ATT_DILATIONS[2]))):
            col += GROUP_QKV
            proj = _dot(hb, w_ref[:, col:col + GROUP_QKV])
            stage = p_scr.at[part, gi]
            for cb in range(GROUP_QKV // LANES):
                stage[cb] = proj[:, cb * LANES:(cb + 1) * LANES]
            sub = hm // d
            for r in range(d):
                out_ref[0, r, part * sub:(part + 1) * sub, :] = jnp.concatenate(
                    [stage[cb, pl.ds(r, sub, stride=d), :] for cb in range(GROUP_QKV // LANES)],
                    axis=1).astype(BF16)
        col += GROUP_QKV

        chunk = 512
        for j in range(2 * D_MODEL // chunk):
            g = _dot(hb, w_ref[:, col + j * chunk:col + (j + 1) * chunk])
            gate_ref[0, rows, j * chunk:(j + 1) * chunk] = _sigmoid(g).astype(BF16)


def _inproj(x, g_pre, sc, sh, w_perm):
    b, s, d = x.shape
    tm = ROW_TILE
    d1, d2 = ATT_DILATIONS[1], ATT_DILATIONS[2]
    grid = (b, s // tm)
    const = lambda bi, i: (0, 0)
    per_b = lambda bi, i: (bi, 0, 0)
    return pl.pallas_call(
        _inproj_kernel,
        grid=grid,
        in_specs=[pl.BlockSpec((1, tm, d), lambda bi, i: (bi, i, 0)),
                  pl.BlockSpec((1, d), const),
                  pl.BlockSpec((1, 1, d), per_b),
                  pl.BlockSpec((1, 1, d), per_b),
                  pl.BlockSpec((d, IN_WIDTH), const, pipeline_mode=pl.Buffered(1))],
        out_specs=[pl.BlockSpec((1, tm, POOL_WIDTH), lambda bi, i: (bi, i, 0)),
                   pl.BlockSpec((1, 1, tm, GROUP_QKV), lambda bi, i: (bi, 0, i, 0)),
                   pl.BlockSpec((1, d1, tm // d1, GROUP_QKV), lambda bi, i: (bi, 0, i, 0)),
                   pl.BlockSpec((1, d2, tm // d2, GROUP_QKV), lambda bi, i: (bi, 0, i, 0)),
                   pl.BlockSpec((1, tm, 2 * D_MODEL), lambda bi, i: (bi, i, 0))],
        out_shape=[jax.ShapeDtypeStruct((b, s, POOL_WIDTH), BF16),
                   jax.ShapeDtypeStruct((b, 1, s, GROUP_QKV), BF16),
                   jax.ShapeDtypeStruct((b, d1, s // d1, GROUP_QKV), BF16),
                   jax.ShapeDtypeStruct((b, d2, s // d2, GROUP_QKV), BF16),
                   jax.ShapeDtypeStruct((b, s, 2 * D_MODEL), BF16)],
        scratch_shapes=[pltpu.VMEM((tm, d), BF16),
                        pltpu.VMEM((ROW_SPLIT, 2, GROUP_QKV // LANES, tm // ROW_SPLIT, LANES), F32)],
        compiler_params=pltpu.CompilerParams(
            dimension_semantics=("arbitrary", "arbitrary"), vmem_limit_bytes=VMEM_LIMIT),
        name="inproj",
    )(x, g_pre, sc, sh, w_perm)


def _attn_kernel(slopes_ref, q0, k0, v0, q1, k1, v1, q2, k2, v2, o_ref,
                 bias_scr, acc, mst, lst, s_even, s_odd):
    pair = pl.program_id(1)
    seq = o_ref.shape[1]
    nblk = seq // ATT_BLOCK
    lane = lax.broadcasted_iota(I32, (ATT_BLOCK, LANES), 1)
    first_head = lane < HEAD_DIM
    half_lane = lax.broadcasted_iota(I32, (ATT_BLOCK // 2, LANES), 1)
    head_bits = (jnp.where(half_lane < HEAD_DIM, -1, 0), jnp.where(half_lane < HEAD_DIM, 0, -1))
    ones_block = jnp.ones((2 * ATT_BLOCK, LANES), BF16)

    qi = lax.broadcasted_iota(I32, (ATT_BLOCK, 2 * ATT_BLOCK), 0)
    kj = lax.broadcasted_iota(I32, (ATT_BLOCK, 2 * ATT_BLOCK), 1)
    delta = ATT_BLOCK + qi - kj
    valid = (delta >= 0) & (delta <= ATT_REACH)
    delta0 = qi - kj
    valid0 = delta0 >= 0
    for g, d in enumerate(ATT_DILATIONS):
        for j in range(2):
            slope = slopes_ref[g * HEADS_PER_GROUP + 2 * pair + j]
            slope = slope * LOG2_E
            bias_scr[g, j, 0] = jnp.where(valid0, -slope * (delta0 * d).astype(F32), MASKED)
            bias_scr[g, j, 1] = jnp.where(valid, -slope * (delta * d).astype(F32), MASKED)

    def run_group(g, q_ref, k_ref, v_ref):
        d = ATT_DILATIONS[g]
        per_res = nblk // d
        n_iter = nblk // ATT_UNROLL

        def block_index(it, k):
            n = it * ATT_UNROLL + k
            return n // per_res, n % per_res

        def key_rows(i):
            lo = jnp.maximum(i - 1, 0)
            return pl.ds(pl.multiple_of(lo * ATT_BLOCK, ATT_BLOCK), 2 * ATT_BLOCK)

        def scores(it, s_ref):
            for k in range(ATT_UNROLL):
                r, i = block_index(it, k)
                q = q_ref[0, r, pl.ds(pl.multiple_of(i * ATT_BLOCK, ATT_BLOCK), ATT_BLOCK), :]
                kw = k_ref[0, r, key_rows(i), :]
                qbits = pltpu.bitcast(q, I32)
                for j in range(2):
                    qh = pltpu.bitcast(qbits & head_bits[j], BF16)
                    s_ref[2 * k + j] = _dot_nt(qh, kw) + bias_scr[g, j, jnp.minimum(i, 1)]

        def weighted_values(it, s_ref):
            for k in range(ATT_UNROLL):
                r, i = block_index(it, k)
                vw = jnp.concatenate([v_ref[0, r, key_rows(i), :], ones_block], axis=1)
                outs = []
                for j in range(2):
                    sc = s_ref[2 * k + j]
                    mx = jnp.max(sc, axis=-1, keepdims=True)
                    p = jnp.exp2(sc - mx)
                    num_den = _dot(p.astype(BF16), vw)
                    outs.append((num_den[:, :LANES], mx, num_den[:, LANES:]))
                (n0, m0, l0), (n1, m1, l1) = outs
                if d == 1:
                    rows = pl.ds(pl.multiple_of(i * ATT_BLOCK, ATT_BLOCK), ATT_BLOCK)
                else:
                    rows = pl.ds(i * (ATT_BLOCK * d) + r, ATT_BLOCK, stride=d)
                acc[g, rows, :] = jnp.where(first_head, n0, n1)
                mst[g, rows, :] = jnp.where(first_head, m0, m1)
                lst[g, rows, :] = jnp.where(first_head, l0, l1)

        scores(0, s_even)

        def body(h, carry):
            it = 2 * h
            scores(it + 1, s_odd)
            weighted_values(it, s_even)
            scores(jnp.minimum(it + 2, n_iter - 1), s_even)
            weighted_values(it + 1, s_odd)
            return carry

        lax.fori_loop(0, n_iter // 2, body, 0)

    run_group(0, q0, k0, v0)
    run_group(1, q1, k1, v1)
    run_group(2, q2, k2, v2)

    def finish(it, carry):
        for k in range(ATT_UNROLL):
            rows = pl.ds(pl.multiple_of((it * ATT_UNROLL + k) * ATT_BLOCK, ATT_BLOCK), ATT_BLOCK)
            ms = [mst[g, rows, :] for g in range(3)]
            top = jnp.maximum(jnp.maximum(ms[0], ms[1]), ms[2])
            scale = [jnp.exp2(m - top) for m in ms]
            num = scale[0] * acc[0, rows, :] + scale[1] * acc[1, rows, :] + scale[2] * acc[2, rows, :]
            den = scale[0] * lst[0, rows, :] + scale[1] * lst[1, rows, :] + scale[2] * lst[2, rows, :]
            o_ref[0, rows, :] = (num / den).astype(BF16)
        return carry

    lax.fori_loop(0, nblk // ATT_UNROLL, finish, 0)


def _attention(qkv0, qkv1, qkv2, slopes):
    b, _, s, _ = qkv0.shape
    pairs = HEADS_PER_GROUP // 2
    col_blocks = HEADS_PER_GROUP * HEAD_DIM // LANES

    def specs(arr):
        _, d, sub, _ = arr.shape
        return [pl.BlockSpec((1, d, sub, LANES),
                             functools.partial(lambda bi, p, sec: (bi, 0, 0, sec * col_blocks + p), sec=sec))
                for sec in range(3)]

    return pl.pallas_call(
        _attn_kernel,
        grid=(b, pairs),
        in_specs=[pl.BlockSpec(memory_space=pltpu.SMEM)] + specs(qkv0) + specs(qkv1) + specs(qkv2),
        out_specs=pl.BlockSpec((1, s, LANES), lambda bi, p: (bi, 0, p)),
        out_shape=jax.ShapeDtypeStruct((b, s, ATT_OUT_WIDTH), BF16),
        scratch_shapes=[pltpu.VMEM((3, 2, 2, ATT_BLOCK, 2 * ATT_BLOCK), F32),
                        pltpu.VMEM((3, s, LANES), F32),
                        pltpu.VMEM((3, s, LANES), F32),
                        pltpu.VMEM((3, s, LANES), F32),
                        pltpu.VMEM((2 * ATT_UNROLL, ATT_BLOCK, 2 * ATT_BLOCK), F32),
                        pltpu.VMEM((2 * ATT_UNROLL, ATT_BLOCK, 2 * ATT_BLOCK), F32)],
        compiler_params=pltpu.CompilerParams(
            dimension_semantics=("arbitrary", "arbitrary"), vmem_limit_bytes=VMEM_LIMIT),
        name="attention",
    )(slopes, qkv0, qkv0, qkv0, qkv1, qkv1, qkv1, qkv2, qkv2, qkv2)


def _mixtail_kernel(u_ref, halo_ref, att_ref, gate_ref, x_ref,
                    gt_m_ref, sc_f_ref, sh_f_ref, g_post_ref, g_pre_ref,
                    wpg_ref, pscale_ref, wbp_ref, wba_ref, wout_ref, wr_ref, br_ref,
                    xmid_ref, route_ref, meta_ref, counts_ref, xs_hbm,
                    pu, lv, xbuf, zbuf, fill, meta_s, sem, sem_s):
    i = pl.program_id(1)
    tm = x_ref.shape[1]
    step = pl.program_id(0) * pl.num_programs(1) + i
    last = pl.num_programs(0) * pl.num_programs(1) - 1
    slot = step % 2
    region = _region_rows(pl.num_programs(0) * pl.num_programs(1) * tm)

    @pl.when(step == 0)
    def _():
        fill[...] = jnp.zeros_like(fill)
        pu[0:POOL_PAD - POOL_HALO, :] = jnp.zeros((POOL_PAD - POOL_HALO, POOL_WIDTH), F32)
        lv[:, :, 0:POOL_PAD - POOL_HALO, :] = jnp.zeros(
            (SORT_GROUPS, 2, POOL_PAD - POOL_HALO, POOL_GROUP_DIM), F32)
        for sub in range(SORT_GROUPS):
            for e in range(N_EXPERTS):
                meta_s[SORT_GROUPS + sub, 1, e] = 0
                meta_s[SORT_GROUPS + sub, 0, e] = 0

    def send_tile(which):
        for sub in range(SORT_GROUPS):
            m = which * SORT_GROUPS + sub
            _tile_run_copies(lambda e: meta_s[m, 1, e], xbuf.at[which, sub], None,
                             xs_hbm, lambda e: e * region + meta_s[m, 0, e], sem.at[which])

    def wait_tile(which):
        for sub in range(SORT_GROUPS):
            pltpu.make_async_copy(xbuf.at[which, sub], xs_hbm.at[pl.ds(0, SORT_ROWS * ROW_WORDS), :],
                                  sem.at[which]).wait()

    @pl.when(step > 1)
    def _():
        wait_tile(slot)

    send_tile(1 - slot)

    head = POOL_PAD - POOL_HALO
    halo = halo_ref[0].astype(F32)
    pu[head:POOL_PAD, :] = jnp.where(i > 0, halo, jnp.zeros_like(halo))
    pu[POOL_PAD:POOL_PAD + tm, :] = u_ref[0].astype(F32)

    ts = SORT_TOKENS

    def pool(sub, st):
        base = sub * ts
        t = i * tm + base + lax.broadcasted_iota(I32, (ts, 1), 0)
        pooled_groups = []
        for g, w in enumerate(POOL_WINDOWS):
            cols = slice(g * POOL_GROUP_DIM, (g + 1) * POOL_GROUP_DIM)
            read = lambda start, n: pu[pl.ds(base + start, n), cols]
            shift, level = 1, 0
            while 2 * shift < w:
                partial = read(head, ts + POOL_HALO) + read(head - shift, ts + POOL_HALO)
                buf = lv.at[sub, level % 2]
                buf[pl.ds(head, ts + POOL_HALO), :] = partial
                read = lambda start, n, buf=buf: buf[pl.ds(start, n), :]
                shift, level = 2 * shift, level + 1
            win = read(POOL_PAD, ts) + read(POOL_PAD - shift, ts)
            count = jnp.minimum(t + 1, w).astype(F32)
            pooled_groups.append((win / count - pu[pl.ds(base + POOL_PAD, ts), cols]).astype(BF16))
        st["pooled"] = jnp.concatenate(pooled_groups, axis=1)

    neg_inf = -jnp.inf
    far = float(LANES)
    before = (lax.broadcasted_iota(I32, (ts, ts), 0) < lax.broadcasted_iota(I32, (ts, ts), 1))
    earlier = jnp.where(before, 1.0, 0.0).astype(BF16)
    erow = lax.broadcasted_iota(I32, (N_EXPERTS, ts), 0).astype(F32)
    e_col = lax.broadcasted_iota(I32, (N_EXPERTS, LANES), 0)
    e_lane = lax.broadcasted_iota(I32, (N_EXPERTS, LANES), 1)
    row8 = lax.broadcasted_iota(I32, (SUBLANES, ts), 0)
    mrow = lax.broadcasted_iota(I32, (SUBLANES, LANES), 0)
    srow = lax.broadcasted_iota(I32, (SORT_ROWS, ts), 0).astype(F32)
    filled = fill[...]

    def rows_of(sub):
        return slice(sub * ts, (sub + 1) * ts)

    def branch_projections(sub, st):
        rows = rows_of(sub)
        mixed = _dot(st.pop("pooled"), wpg_ref[...]) * pscale_ref[...]
        st["y_pool"] = _dot(mixed.astype(BF16), wbp_ref[...])
        st["y_att"] = _dot(att_ref[0, rows, :], wba_ref[...])

    def gated_sum(sub, st):
        rows = rows_of(sub)
        st["merged"] = (gate_ref[0, rows, 0:D_MODEL] * st.pop("y_pool").astype(BF16)
                        + gate_ref[0, rows, D_MODEL:2 * D_MODEL] * st.pop("y_att").astype(BF16))

    def output_projection(sub, st):
        st["y"] = _dot(st.pop("merged"), wout_ref[...])

    def residual_and_ffn_input(sub, st):
        rows = rows_of(sub)
        x_mid = x_ref[0, rows, :] + _rmsnorm(st.pop("y"), gt_m_ref[0] * g_post_ref[...])
        xmid_ref[0, rows, :] = x_mid
        h2 = _rmsnorm(x_mid, g_pre_ref[...] * (1.0 + sc_f_ref[0])) + sh_f_ref[0]
        st["h2b"] = h2.astype(BF16)
        st["h2lo"] = (h2 - st["h2b"].astype(F32)).astype(BF16)

    def router_logits(sub, st):
        by_hi = _dot_nt(wr_ref[...], st["h2b"])
        st["logits"] = (by_hi[0:ROUTER_ROWS] + (by_hi[ROUTER_ROWS:2 * ROUTER_ROWS]
                        + _dot_nt(wr_ref[0:ROUTER_ROWS, :], st.pop("h2lo"))) + br_ref[...])

    def route(sub, st):
        logits = st.pop("logits")
        gl = logits[N_EXPERTS:N_EXPERTS + N_EXPERT_GROUPS, :]
        grow = lax.broadcasted_iota(I32, gl.shape, 0).astype(F32)
        gmax = jnp.max(gl, axis=0, keepdims=True)
        gsel = jnp.min(jnp.where(gl == gmax, grow, far), axis=0, keepdims=True)
        p_group = 1.0 / jnp.sum(jnp.exp(gl - gmax), axis=0, keepdims=True)
        e_lo = gsel * float(EXPERTS_PER_GROUP)
        el = jnp.where((erow >= e_lo) & (erow < e_lo + float(EXPERTS_PER_GROUP)),
                       logits[0:N_EXPERTS, :], neg_inf)
        v1 = jnp.max(el, axis=0, keepdims=True)
        i1 = jnp.min(jnp.where(el == v1, erow, far), axis=0, keepdims=True)
        el2 = jnp.where(erow == i1, neg_inf, el)
        v2 = jnp.max(el2, axis=0, keepdims=True)
        i2 = jnp.min(jnp.where(el2 == v2, erow, far), axis=0, keepdims=True)
        e21 = jnp.exp(v2 - v1)
        st["w1"] = p_group / (1.0 + e21)
        st["w2"] = p_group * e21 / (1.0 + e21)
        st["pick1"] = erow == i1
        st["pick2"] = erow == i2
        st["assign"] = jnp.where(st["pick1"] | st["pick2"], 1.0, 0.0)

    def count_and_rank(sub, st):
        assign = st.pop("assign")
        st["rank"] = _dot(assign.astype(BF16), earlier)
        assign_pad = jnp.concatenate([assign, jnp.zeros((LANES - N_EXPERTS, ts), F32)], axis=0).astype(BF16)
        st["cnt_row"] = _dot_nt(jnp.ones((SUBLANES, ts), BF16), assign_pad)

    def sorted_positions(sub, st):
        cnt_row = st["cnt_row"]
        run_start = jnp.sum(jnp.where(e_lane < e_col, cnt_row[0:1, :], 0.0), axis=1, keepdims=True)
        pos = st.pop("rank") + run_start
        key1 = jnp.sum(jnp.where(st.pop("pick1"), pos, 0.0), axis=0, keepdims=True)
        key2 = jnp.sum(jnp.where(st.pop("pick2"), pos, 0.0), axis=0, keepdims=True)
        route_ref[:, rows_of(sub)] = jnp.where(row8 == 0, key1, jnp.where(row8 == 1, key2,
                                               jnp.where(row8 == 2, st.pop("w1"),
                                                         jnp.where(row8 == 3, st.pop("w2"), 0.0))))
        st["perm"] = jnp.where((srow == key1) | (srow == key2), 1.0, 0.0).astype(BF16)

    def sort_rows(sub, st):
        perm, h2b = st.pop("perm"), st.pop("h2b")
        for w in range(ROW_WORDS):
            pair = _dot(perm, h2b[:, 2 * w * LANES:(2 * w + 2) * LANES])
            _pack_rows(xbuf.at[slot, sub], w, SORT_ROWS, pair[:, :LANES], pair[:, LANES:])

    state = _run_staggered((pool, branch_projections, gated_sum, output_projection,
                            residual_and_ffn_input, router_logits, route, count_and_rank,
                            sorted_positions, sort_rows), SORT_GROUPS, lag=0)

    for sub in range(SORT_GROUPS):
        cnt_row = state[sub]["cnt_row"]
        meta_ref[sub] = jnp.where(mrow == 0, filled, jnp.where(mrow == 1, cnt_row, 0.0)).astype(I32)
        filled = filled + cnt_row

    fill[...] = filled
    counts_ref[...] = filled.astype(I32)

    meta_copy = pltpu.make_async_copy(meta_ref, meta_s.at[pl.ds(slot * SORT_GROUPS, SORT_GROUPS)], sem_s)
    meta_copy.start()
    meta_copy.wait()

    @pl.when(step == last)
    def _():
        send_tile(slot)

        @pl.when(step > 0)
        def _():
            wait_tile(1 - slot)

        wait_tile(slot)
        zbuf[...] = jnp.zeros_like(zbuf)
        final = slot * SORT_GROUPS + SORT_GROUPS - 1

        def pad_copy(e):
            end = e * region + meta_s[final, 0, e] + meta_s[final, 1, e]
            return pltpu.make_async_copy(
                zbuf, xs_hbm.at[pl.ds(end * ROW_WORDS, MOE_BLOCK * ROW_WORDS), :], sem_s)

        def start_pad(e, carry):
            pad_copy(e).start()
            return carry

        def wait_pad(e, carry):
            pad_copy(e).wait()
            return carry

        lax.fori_loop(0, N_EXPERTS, start_pad, 0)
        lax.fori_loop(0, N_EXPERTS, wait_pad, 0)


def _mixtail(u, att, gates, x, gt_m, sc_f, sh_f, g_post, g_pre,
             wpg, pscale, wbp, wba, wout, wr, br):
    b, s, d = x.shape
    tm = ROW_TILE
    tiles = s // tm
    n_tiles = b * tiles
    halo_blocks = tm // POOL_HALO
    region = _region_rows(b * s)
    const2 = lambda bi, i: (0, 0)
    const3 = lambda bi, i: (0, 0, 0)
    per_b = lambda bi, i: (bi, 0, 0)
    tile = lambda bi, i: (bi, i, 0)
    single = dict(pipeline_mode=pl.Buffered(1))
    return pl.pallas_call(
        _mixtail_kernel,
        grid=(b, tiles),
        in_specs=[pl.BlockSpec((1, tm, POOL_WIDTH), tile),
                  pl.BlockSpec((1, POOL_HALO, POOL_WIDTH),
                               lambda bi, i: (bi, jnp.maximum(i * halo_blocks - 1, 0), 0)),
                  pl.BlockSpec((1, tm, ATT_OUT_WIDTH), tile),
                  pl.BlockSpec((1, tm, 2 * D_MODEL), tile),
                  pl.BlockSpec((1, tm, d), tile),
                  pl.BlockSpec((1, 1, d), per_b),
                  pl.BlockSpec((1, 1, d), per_b),
                  pl.BlockSpec((1, 1, d), per_b),
                  pl.BlockSpec((1, d), const2),
                  pl.BlockSpec((1, d), const2),
                  pl.BlockSpec(wpg.shape, const2, **single),
                  pl.BlockSpec((1, POOL_WIDTH), const2),
                  pl.BlockSpec(wbp.shape, const2, **single),
                  pl.BlockSpec(wba.shape, const2, **single),
                  pl.BlockSpec(wout.shape, const2, **single),
                  pl.BlockSpec(wr.shape, const2, **single),
                  pl.BlockSpec(br.shape, const2)],
        out_specs=[pl.BlockSpec((1, tm, d), tile),
                   pl.BlockSpec((SUBLANES, tm), lambda bi, i: (0, bi * tiles + i)),
                   pl.BlockSpec((SORT_GROUPS, SUBLANES, LANES), lambda bi, i: (bi * tiles + i, 0, 0)),
                   pl.BlockSpec((SUBLANES, LANES), const2),
                   pl.BlockSpec(memory_space=pl.ANY)],
        out_shape=[jax.ShapeDtypeStruct((b, s, d), F32),
                   jax.ShapeDtypeStruct((SUBLANES, b * s), F32),
                   jax.ShapeDtypeStruct((n_tiles * SORT_GROUPS, SUBLANES, LANES), I32),
                   jax.ShapeDtypeStruct((SUBLANES, LANES), I32),
                   jax.ShapeDtypeStruct((N_EXPERTS * region * ROW_WORDS, LANES), I32)],
        scratch_shapes=[pltpu.VMEM((POOL_PAD + tm, POOL_WIDTH), F32),
                        pltpu.VMEM((SORT_GROUPS, 2, POOL_PAD + SORT_TOKENS, POOL_GROUP_DIM), F32),
                        pltpu.VMEM((2, SORT_GROUPS, SORT_ROWS * ROW_WORDS, LANES), I32),
                        pltpu.VMEM((MOE_BLOCK * ROW_WORDS, LANES), I32),
                        pltpu.VMEM((SUBLANES, LANES), F32),
                        pltpu.SMEM((2 * SORT_GROUPS, SUBLANES, LANES), I32),
                        pltpu.SemaphoreType.DMA((2,)),
                        pltpu.SemaphoreType.DMA(())],
        compiler_params=pltpu.CompilerParams(
            dimension_semantics=("arbitrary", "arbitrary"), vmem_limit_bytes=VMEM_LIMIT),
        name="mixtail",
    )(u, u, att, gates, x, gt_m, sc_f, sh_f, g_post, g_pre,
      wpg, pscale, wbp, wba, wout, wr, br)


def _expert_kernel(counts_ref, xs_hbm, wg_ref, wu_ref, wd_ref, ys_hbm,
                   xbuf, ybuf, state, semx, semy):
    e = pl.program_id(0)
    bm = MOE_BLOCK
    nx, ny = EXPERT_X_BUFFERS, EXPERT_Y_BUFFERS
    block_words = bm * ROW_WORDS
    region = xs_hbm.shape[0] // (N_EXPERTS * ROW_WORDS)

    def n_blocks(ex):
        return (counts_ref[ex] + (bm - 1)) // bm

    def block_rows(ref, ex, k):
        start = pl.multiple_of((ex * region + k * bm) * ROW_WORDS, block_words)
        return ref.at[pl.ds(start, block_words), :]

    def x_copy(ex, k, s):
        return pltpu.make_async_copy(block_rows(xs_hbm, ex, k), xbuf.at[s], semx.at[s])

    def y_copy(k, s):
        return pltpu.make_async_copy(ybuf.at[s], block_rows(ys_hbm, e, k), semy.at[s])

    @pl.when(e == 0)
    def _():
        for j in range(4):
            state[j] = 0

    def fetch_through(target):
        def more(c):
            pe, _, pg = c
            return (pg < target) & (pe < N_EXPERTS)

        def step(c):
            pe, pk, pg = c
            has = pk < n_blocks(pe)

            @pl.when(has)
            def _():
                x_copy(pe, pk, pg % nx).start()

            return (jnp.where(has, pe, pe + 1), jnp.where(has, pk + 1, 0), pg + has.astype(I32))

        pe, pk, pg = lax.while_loop(more, step, (state[1], state[2], state[3]))
        state[1] = pe
        state[2] = pk
        state[3] = pg

    nb = n_blocks(e)
    done = state[0]

    def blocks(k0, count, rows=bm):
        ks = [k0 + c for c in range(count)]
        gs = [done + k for k in ks]
        fetch_through(gs[0] + nx)
        for k, g in zip(ks, gs):
            x_copy(e, k, g % nx).wait()

            @pl.when(g >= ny)
            def _():
                y_copy(k, g % ny).wait()

        mids = []
        for g in gs:
            x = _unpack_rows(xbuf.at[g % nx], rows).astype(BF16)
            a = _dot(x, wg_ref[...].astype(BF16))
            u = _dot(x, wu_ref[...].astype(BF16))
            mids.append(((a * _sigmoid(a)) * u).astype(BF16))
        for k, g, mid in zip(ks, gs, mids):
            out = ybuf.at[g % ny]
            for w in range(ROW_WORDS):
                pair = _bf16_exact(_dot(mid, wd_ref[:, 2 * w * LANES:(2 * w + 2) * LANES].astype(BF16)))
                _pack_rows(out, w, rows, pair[:, :LANES], pair[:, LANES:])
            if rows < bm:
                out[pl.ds(rows * ROW_WORDS, (bm - rows) * ROW_WORDS), :] = jnp.zeros(
                    ((bm - rows) * ROW_WORDS, LANES), I32)
            y_copy(k, g % ny).start()

    per = EXPERT_CHAINS
    tail = counts_ref[e] - (nb - 1) * bm
    half_tail = (nb > 0) & (tail <= bm // 2)
    n_main = nb - half_tail.astype(I32)

    def several_blocks(p, carry):
        blocks(per * p, per)
        return carry

    def single_block(k, carry):
        blocks(k, 1)
        return carry

    lax.fori_loop(0, n_main // per, several_blocks, 0)
    lax.fori_loop((n_main // per) * per, n_main, single_block, 0)

    @pl.when(half_tail)
    def _():
        blocks(nb - 1, 1, rows=bm // 2)

    state[0] = done + nb

    @pl.when(e == N_EXPERTS - 1)
    def _():
        total = done + nb
        for j in range(ny):
            @pl.when(total > j)
            def _():
                y_copy(0, (total - 1 - j) % ny).wait()


def _experts(xs, counts, w_gate, w_up, w_down):
    bm = MOE_BLOCK
    w_in = pl.BlockSpec((None, D_MODEL, D_EXPERT), lambda e, cnt: (e, 0, 0))
    w_out = pl.BlockSpec((None, D_EXPERT, D_MODEL), lambda e, cnt: (e, 0, 0))
    grid_spec = pltpu.PrefetchScalarGridSpec(
        num_scalar_prefetch=1,
        grid=(N_EXPERTS,),
        in_specs=[pl.BlockSpec(memory_space=pl.ANY), w_in, w_in, w_out],
        out_specs=pl.BlockSpec(memory_space=pl.ANY),
        scratch_shapes=[pltpu.VMEM((EXPERT_X_BUFFERS, bm * ROW_WORDS, LANES), I32),
                        pltpu.VMEM((EXPERT_Y_BUFFERS, bm * ROW_WORDS, LANES), I32),
                        pltpu.SMEM((4,), I32),
                        pltpu.SemaphoreType.DMA((EXPERT_X_BUFFERS,)),
                        pltpu.SemaphoreType.DMA((EXPERT_Y_BUFFERS,))],
    )
    return pl.pallas_call(
        _expert_kernel,
        grid_spec=grid_spec,
        out_shape=jax.ShapeDtypeStruct(xs.shape, I32),
        compiler_params=pltpu.CompilerParams(
            dimension_semantics=("arbitrary",), vmem_limit_bytes=VMEM_LIMIT),
        name="experts",
    )(counts, xs, w_gate, w_up, w_down)


def _combine_kernel(meta_ref, y_hbm, route_ref, xmid_ref, gt_ref, g_ref, o_ref, ybuf, rt_scr, sem):
    step = pl.program_id(0)
    n_steps = pl.num_programs(0)
    tm = ROW_TILE
    slot = step % COMBINE_BUFFERS
    region = _region_rows(n_steps * tm)

    def fetch_tile(tile, live):
        tile = jnp.minimum(tile, n_steps - 1)
        which = tile % COMBINE_BUFFERS
        for sub in range(SORT_GROUPS):
            m = (tile * SORT_GROUPS + sub) * 2
            _tile_run_copies(lambda e: jnp.where(live, meta_ref[(m + 1) * N_EXPERTS + e], 0),
                             y_hbm, lambda e: e * region + meta_ref[m * N_EXPERTS + e],
                             ybuf.at[which, sub], None, sem.at[which])

    @pl.when(step == 0)
    def _():
        for ahead in range(COMBINE_BUFFERS - 1):
            fetch_tile(ahead, ahead < n_steps)

    for sub in range(SORT_GROUPS):
        pltpu.make_async_copy(y_hbm.at[pl.ds(0, SORT_ROWS * ROW_WORDS), :], ybuf.at[slot, sub],
                              sem.at[slot]).wait()
    sorted_rows = [_unpack_rows(ybuf.at[slot, sub], SORT_ROWS).astype(BF16) for sub in range(SORT_GROUPS)]

    fetch_tile(step + COMBINE_BUFFERS - 1, step + COMBINE_BUFFERS - 1 < n_steps)

    rt_scr[...] = jnp.zeros_like(rt_scr)
    rt_scr[0:SUBLANES, :] = route_ref[...]
    cols = jnp.concatenate([rt_scr[:, c * LANES:(c + 1) * LANES].T for c in range(tm // LANES)], axis=0)

    ts = SORT_TOKENS
    spos = lax.broadcasted_iota(I32, (ts, SORT_ROWS), 1).astype(F32)
    def rows_of(sub):
        return slice(sub * ts, (sub + 1) * ts)

    def gate_matrix(sub, st):
        key1, key2, w1, w2 = (cols[rows_of(sub), c:c + 1] for c in range(4))
        st["gates"] = jnp.where(spos == key1, w1, jnp.where(spos == key2, w2, 0.0)).astype(BF16)

    def weighted_unsort(sub, st):
        st["y"] = _dot(st.pop("gates"), sorted_rows[sub])

    def norm_and_add(sub, st):
        rows = rows_of(sub)
        o_ref[rows, :] = xmid_ref[rows, :] + _rmsnorm(st.pop("y"), gt_ref[0] * g_ref[...])

    _run_staggered((gate_matrix, weighted_unsort, norm_and_add), SORT_GROUPS, lag=1)


def _combine(meta, ys, route, x_mid, gt_f, g_post, seq):
    t, d = x_mid.shape
    tm = ROW_TILE
    tiles_per_seq = seq // tm
    grid_spec = pltpu.PrefetchScalarGridSpec(
        num_scalar_prefetch=1,
        grid=(t // tm,),
        in_specs=[pl.BlockSpec(memory_space=pl.ANY),
                  pl.BlockSpec((SUBLANES, tm), lambda i, m: (0, i)),
                  pl.BlockSpec((tm, d), lambda i, m: (i, 0)),
                  pl.BlockSpec((1, 1, d), lambda i, m: (i // tiles_per_seq, 0, 0)),
                  pl.BlockSpec((1, d), lambda i, m: (0, 0))],
        out_specs=pl.BlockSpec((tm, d), lambda i, m: (i, 0)),
        scratch_shapes=[pltpu.VMEM((COMBINE_BUFFERS, SORT_GROUPS, SORT_ROWS * ROW_WORDS, LANES), I32),
                        pltpu.VMEM((LANES, tm), F32),
                        pltpu.SemaphoreType.DMA((COMBINE_BUFFERS,))],
    )
    return pl.pallas_call(
        _combine_kernel,
        grid_spec=grid_spec,
        out_shape=jax.ShapeDtypeStruct((t, d), F32),
        compiler_params=pltpu.CompilerParams(
            dimension_semantics=("arbitrary",), vmem_limit_bytes=VMEM_LIMIT),
        name="combine",
    )(meta, ys, route, x_mid, gt_f, g_post)


def kernel(x, c, w_ada, b_ada, g_pre_mix, g_post_mix, g_pre_ffn, g_post_ffn, w_in, w_pool_group, pool_scale, w_branch_pool, w_branch_att, w_out, w_group_router, b_group_router, w_expert_router, b_expert_router, w_exp_gate, w_exp_up, w_exp_down):
    b, s, d = x.shape
    assert d == D_MODEL and s % (ATT_BLOCK * 2 * ATT_DILATIONS[2]) == 0 and s % ROW_TILE == 0
    assert (b * s) % MOE_BLOCK == 0
    depth = w_ada.shape[0]
    slopes = jnp.exp2(-ALIBI_MAX_BIAS * jnp.arange(1, N_ATT_HEADS + 1, dtype=F32) / N_ATT_HEADS)
    q_scale = HEAD_DIM ** -0.5 * LOG2_E

    for layer in range(depth):
        mod = _adaln(c, w_ada[layer], b_ada[layer]).reshape(b, 6, 1, d)
        sh_m, sc_m, gt_m, sh_f, sc_f, gt_f = [mod[:, j] for j in range(6)]

        wl = w_in[layer]
        q_lo, k_lo, v_lo = POOL_WIDTH, POOL_WIDTH + 768, POOL_WIDTH + 2 * 768
        group_cols = []
        for g in range(3):
            sl = slice(g * 256, (g + 1) * 256)
            group_cols += [wl[:, q_lo:k_lo][:, sl] * q_scale, wl[:, k_lo:v_lo][:, sl],
                           wl[:, v_lo:v_lo + 768][:, sl]]
        w_perm = jnp.concatenate([wl[:, :POOL_WIDTH]] + group_cols + [wl[:, v_lo + 768:]],
                                 axis=1).astype(BF16)

        u, qkv0, qkv1, qkv2, gates = _inproj(x, g_pre_mix[layer].reshape(1, d), sc_m, sh_m, w_perm)
        att = _attention(qkv0, qkv1, qkv2, slopes)

        zero_block = jnp.zeros((POOL_GROUP_DIM, POOL_GROUP_DIM), F32)
        wpg_diag = jnp.block([[w_pool_group[layer, g] if g == h else zero_block
                               for h in range(len(POOL_WINDOWS))] for g in range(len(POOL_WINDOWS))])

        pad_rows = ROUTER_ROWS - N_EXPERTS - N_EXPERT_GROUPS
        wrt = jnp.concatenate([w_expert_router[layer].T, w_group_router[layer].T,
                               jnp.zeros((pad_rows, d), F32)], axis=0)
        wrt_hi = wrt.astype(BF16)
        wr = jnp.concatenate([wrt_hi, (wrt - wrt_hi.astype(F32)).astype(BF16)], axis=0)
        br = jnp.concatenate([b_expert_router[layer], b_group_router[layer],
                              jnp.zeros((pad_rows,), F32)]).reshape(ROUTER_ROWS, 1)

        x_mid, route, meta, counts, xs = _mixtail(
            u, att, gates, x, gt_m, sc_f, sh_f,
            g_post_mix[layer].reshape(1, d), g_pre_ffn[layer].reshape(1, d),
            wpg_diag.astype(BF16), pool_scale[layer].reshape(1, POOL_WIDTH),
            w_branch_pool[layer].astype(BF16), w_branch_att[layer].astype(BF16),
            w_out[layer].astype(BF16), wr, br)

        ys = _experts(xs, counts[0, :N_EXPERTS], w_exp_gate[layer], w_exp_up[layer], w_exp_down[layer])
        run_meta = meta[:, 0:2, 0:N_EXPERTS].reshape(-1)
        x = _combine(run_meta, ys, route, x_mid.reshape(b * s, d), gt_f,
                     g_post_ffn[layer].reshape(1, d), s).reshape(b, s, d)
    return x
```

```python
import functools

import jax
import jax.numpy as jnp
from jax import lax
from jax.experimental import pallas as pl
from jax.experimental.pallas import tpu as pltpu

F32 = jnp.float32
BF16 = jnp.bfloat16
I32 = jnp.int32

D_MODEL = 1024
LANES = 128
SUBLANES = 8
ROW_WORDS = D_MODEL // (2 * LANES)
HI_HALF = -65536

POOL_WINDOWS = (2, 4, 8, 16)
POOL_GROUP_DIM = 128
POOL_WIDTH = 512
POOL_HALO = 16
POOL_PAD = POOL_HALO + 8

HEAD_DIM = 64
ATT_DILATIONS = (1, 4, 16)
ATT_REACH = 128
ATT_BLOCK = 128
ATT_UNROLL = 4
HEADS_PER_GROUP = 4
N_ATT_HEADS = 12
GROUP_QKV = 3 * HEADS_PER_GROUP * HEAD_DIM
ATT_OUT_WIDTH = 256
ALIBI_MAX_BIAS = 8.0
IN_WIDTH = POOL_WIDTH + 3 * GROUP_QKV + 2 * D_MODEL
MASKED = -1e30
LOG2_E = 1.4426950408889634

N_EXPERT_GROUPS = 4
EXPERTS_PER_GROUP = 8
N_EXPERTS = 32
ROUTER_ROWS = 48
D_EXPERT = 512
RMS_EPS = 1e-6

ROW_TILE = 1024
ROW_SPLIT = 4
SORT_TOKENS = 256
SORT_GROUPS = ROW_TILE // SORT_TOKENS
SORT_ROWS = 2 * SORT_TOKENS
MOE_BLOCK = 256
EXPERT_CHAINS = 2
EXPERT_X_BUFFERS = 6
EXPERT_Y_BUFFERS = 4
COMBINE_BUFFERS = 3
VMEM_LIMIT = 52 * 1024 * 1024


def _sigmoid(x):
    return 0.5 * jnp.tanh(0.5 * x) + 0.5


def _rmsnorm(x, g):
    return x * lax.rsqrt(jnp.mean(x * x, axis=-1, keepdims=True) + RMS_EPS) * g


def _dot(a, b):
    return jnp.dot(a, b, preferred_element_type=F32)


def _dot_nt(a, b):
    return lax.dot_general(a, b, (((1,), (1,)), ((), ())), preferred_element_type=F32)


def _pack_rows(ref, w, n, lo, hi):
    word = (lax.shift_right_logical(pltpu.bitcast(lo, I32), 16) | (pltpu.bitcast(hi, I32) & HI_HALF))
    ref[pl.ds(w, n, stride=ROW_WORDS), :] = word


def _unpack_rows(ref, n):
    cols = []
    for w in range(ROW_WORDS):
        word = ref[pl.ds(w, n, stride=ROW_WORDS), :]
        cols += [pltpu.bitcast(word << 16, F32), pltpu.bitcast(word & HI_HALF, F32)]
    return jnp.concatenate(cols, axis=1)


def _bf16_exact(x):
    return x.astype(BF16).astype(F32)


def _run_staggered(stages, n_chains, lag):
    state = [{} for _ in range(n_chains)]
    for tick in range(len(stages) + lag * (n_chains - 1)):
        for chain in range(n_chains):
            if 0 <= tick - lag * chain < len(stages):
                stages[tick - lag * chain](chain, state[chain])
    return state


def _region_rows(n_tokens):
    return n_tokens + MOE_BLOCK


def _run_copies(n, src, src_row, dst, dst_row, sem):
    @pl.when(n > 0)
    def _():
        pltpu.make_async_copy(
            src.at[pl.ds(src_row * ROW_WORDS, n * ROW_WORDS), :],
            dst.at[pl.ds(dst_row * ROW_WORDS, n * ROW_WORDS), :], sem).start()


def _tile_run_copies(counts, src, src_rows, dst, dst_rows, sem):
    local = 0
    for e in range(N_EXPERTS):
        n = counts(e)
        _run_copies(n, src, local if src_rows is None else src_rows(e),
                    dst, local if dst_rows is None else dst_rows(e), sem)
        local = local + n


def _adaln_kernel(c_ref, w_ref, b_ref, o_ref):
    c = c_ref[...]
    a = c * _sigmoid(c)
    w = w_ref[...]
    a_hi, w_hi = a.astype(BF16), w.astype(BF16)
    a_lo = (a - a_hi.astype(F32)).astype(BF16)
    w_lo = (w - w_hi.astype(F32)).astype(BF16)
    o_ref[...] = _dot(a_hi, w_hi) + (_dot(a_hi, w_lo) + _dot(a_lo, w_hi)) + b_ref[...]


def _adaln(c, w_ada, b_ada):
    b, d = c.shape
    n = w_ada.shape[1]
    rows = -(-b // SUBLANES) * SUBLANES
    cp = jnp.pad(c, ((0, rows - b), (0, 0)))
    nt = 1536
    out = pl.pallas_call(
        _adaln_kernel,
        grid=(n // nt,),
        in_specs=[pl.BlockSpec((rows, d), lambda j: (0, 0)),
                  pl.BlockSpec((d, nt), lambda j: (0, j)),
                  pl.BlockSpec((1, nt), lambda j: (0, j))],
        out_specs=pl.BlockSpec((rows, nt), lambda j: (0, j)),
        out_shape=jax.ShapeDtypeStruct((rows, n), F32),
        compiler_params=pltpu.CompilerParams(vmem_limit_bytes=VMEM_LIMIT),
        name="adaln",
    )(cp, w_ada, b_ada.reshape(1, n))
    return out[:b]


def _inproj_kernel(x_ref, g_ref, sc_ref, sh_ref, w_ref,
                   u_ref, qkv0_ref, qkv1_ref, qkv2_ref, gate_ref, h_scr, p_scr):
    tm = x_ref.shape[1]
    hm = tm // ROW_SPLIT
    for part in range(ROW_SPLIT):
        rows = slice(part * hm, (part + 1) * hm)
        h = _rmsnorm(x_ref[0, rows, :], g_ref[...] * (1.0 + sc_ref[0])) + sh_ref[0]
        h_scr[rows, :] = h.astype(BF16)
        hb = h_scr[rows, :]

        u_ref[0, rows, :] = _dot(hb, w_ref[:, 0:POOL_WIDTH]).astype(BF16)

        col = POOL_WIDTH
        qkv0_ref[0, 0, rows, :] = _dot(hb, w_ref[:, col:col + GROUP_QKV]).astype(BF16)
        for gi, (out_ref, d) in enumerate(((qkv1_ref, ATT_DILATIONS[1]), (qkv2_ref, ATT_DILATIONS[2]))):
            col += GROUP_QKV
            proj = _dot(hb, w_ref[:, col:col + GROUP_QKV])
            stage = p_scr.at[part, gi]
            for cb in range(GROUP_QKV // LANES):
                stage[cb] = proj[:, cb * LANES:(cb + 1) * LANES]
            sub = hm // d
            for r in range(d):
                out_ref[0, r, part * sub:(part + 1) * sub, :] = jnp.concatenate(
                    [stage[cb, pl.ds(r, sub, stride=d), :] for cb in range(GROUP_QKV // LANES)],
                    axis=1).astype(BF16)
        col += GROUP_QKV

        chunk = 512
        for j in range(2 * D_MODEL // chunk):
            g = _dot(hb, w_ref[:, col + j * chunk:col + (j + 1) * chunk])
            gate_ref[0, rows, j * chunk:(j + 1) * chunk] = _sigmoid(g).astype(BF16)


def _inproj(x, g_pre, sc, sh, w_perm):
    b, s, d = x.shape
    tm = ROW_TILE
    d1, d2 = ATT_DILATIONS[1], ATT_DILATIONS[2]
    grid = (b, s // tm)
    const = lambda bi, i: (0, 0)
    per_b = lambda bi, i: (bi, 0, 0)
    return pl.pallas_call(
        _inproj_kernel,
        grid=grid,
        in_specs=[pl.BlockSpec((1, tm, d), lambda bi, i: (bi, i, 0)),
                  pl.BlockSpec((1, d), const),
                  pl.BlockSpec((1, 1, d), per_b),
                  pl.BlockSpec((1, 1, d), per_b),
                  pl.BlockSpec((d, IN_WIDTH), const, pipeline_mode=pl.Buffered(1))],
        out_specs=[pl.BlockSpec((1, tm, POOL_WIDTH), lambda bi, i: (bi, i, 0)),
                   pl.BlockSpec((1, 1, tm, GROUP_QKV), lambda bi, i: (bi, 0, i, 0)),
                   pl.BlockSpec((1, d1, tm // d1, GROUP_QKV), lambda bi, i: (bi, 0, i, 0)),
                   pl.BlockSpec((1, d2, tm // d2, GROUP_QKV), lambda bi, i: (bi, 0, i, 0)),
                   pl.BlockSpec((1, tm, 2 * D_MODEL), lambda bi, i: (bi, i, 0))],
        out_shape=[jax.ShapeDtypeStruct((b, s, POOL_WIDTH), BF16),
                   jax.ShapeDtypeStruct((b, 1, s, GROUP_QKV), BF16),
                   jax.ShapeDtypeStruct((b, d1, s // d1, GROUP_QKV), BF16),
                   jax.ShapeDtypeStruct((b, d2, s // d2, GROUP_QKV), BF16),
                   jax.ShapeDtypeStruct((b, s, 2 * D_MODEL), BF16)],
        scratch_shapes=[pltpu.VMEM((tm, d), BF16),
                        pltpu.VMEM((ROW_SPLIT, 2, GROUP_QKV // LANES, tm // ROW_SPLIT, LANES), F32)],
        compiler_params=pltpu.CompilerParams(
            dimension_semantics=("arbitrary", "arbitrary"), vmem_limit_bytes=VMEM_LIMIT),
        name="inproj",
    )(x, g_pre, sc, sh, w_perm)


def _attn_kernel(slopes_ref, q0, k0, v0, q1, k1, v1, q2, k2, v2, o_ref,
                 bias_scr, acc, mst, lst, s_even, s_odd):
    pair = pl.program_id(1)
    seq = o_ref.shape[1]
    nblk = seq // ATT_BLOCK
    lane = lax.broadcasted_iota(I32, (ATT_BLOCK, LANES), 1)
    first_head = lane < HEAD_DIM
    half_lane = lax.broadcasted_iota(I32, (ATT_BLOCK // 2, LANES), 1)
    head_bits = (jnp.where(half_lane < HEAD_DIM, -1, 0), jnp.where(half_lane < HEAD_DIM, 0, -1))
    ones_block = jnp.ones((2 * ATT_BLOCK, LANES), BF16)

    qi = lax.broadcasted_iota(I32, (ATT_BLOCK, 2 * ATT_BLOCK), 0)
    kj = lax.broadcasted_iota(I32, (ATT_BLOCK, 2 * ATT_BLOCK), 1)
    delta = ATT_BLOCK + qi - kj
    valid = (delta >= 0) & (delta <= ATT_REACH)
    delta0 = qi - kj
    valid0 = delta0 >= 0
    for g, d in enumerate(ATT_DILATIONS):
        for j in range(2):
            slope = slopes_ref[g * HEADS_PER_GROUP + 2 * pair + j]
            slope = slope * LOG2_E
            bias_scr[g, j, 0] = jnp.where(valid0, -slope * (delta0 * d).astype(F32), MASKED)
            bias_scr[g, j, 1] = jnp.where(valid, -slope * (delta * d).astype(F32), MASKED)

    def run_group(g, q_ref, k_ref, v_ref):
        d = ATT_DILATIONS[g]
        per_res = nblk // d
        n_iter = nblk // ATT_UNROLL

        def block_index(it, k):
            n = it * ATT_UNROLL + k
            return n // per_res, n % per_res

        def key_rows(i):
            lo = jnp.maximum(i - 1, 0)
            return pl.ds(pl.multiple_of(lo * ATT_BLOCK, ATT_BLOCK), 2 * ATT_BLOCK)

        def scores(it, s_ref):
            for k in range(ATT_UNROLL):
                r, i = block_index(it, k)
                q = q_ref[0, r, pl.ds(pl.multiple_of(i * ATT_BLOCK, ATT_BLOCK), ATT_BLOCK), :]
                kw = k_ref[0, r, key_rows(i), :]
                qbits = pltpu.bitcast(q, I32)
                for j in range(2):
                    qh = pltpu.bitcast(qbits & head_bits[j], BF16)
                    s_ref[2 * k + j] = _dot_nt(qh, kw) + bias_scr[g, j, jnp.minimum(i, 1)]

        def weighted_values(it, s_ref):
            for k in range(ATT_UNROLL):
                r, i = block_index(it, k)
                vw = jnp.concatenate([v_ref[0, r, key_rows(i), :], ones_block], axis=1)
                outs = []
                for j in range(2):
                    sc = s_ref[2 * k + j]
                    mx = jnp.max(sc, axis=-1, keepdims=True)
                    p = jnp.exp2(sc - mx)
                    num_den = _dot(p.astype(BF16), vw)
                    outs.append((num_den[:, :LANES], mx, num_den[:, LANES:]))
                (n0, m0, l0), (n1, m1, l1) = outs
                if d == 1:
                    rows = pl.ds(pl.multiple_of(i * ATT_BLOCK, ATT_BLOCK), ATT_BLOCK)
                else:
                    rows = pl.ds(i * (ATT_BLOCK * d) + r, ATT_BLOCK, stride=d)
                acc[g, rows, :] = jnp.where(first_head, n0, n1)
                mst[g, rows, :] = jnp.where(first_head, m0, m1)
                lst[g, rows, :] = jnp.where(first_head, l0, l1)

        scores(0, s_even)

        def body(h, carry):
            it = 2 * h
            scores(it + 1, s_odd)
            weighted_values(it, s_even)
            scores(jnp.minimum(it + 2, n_iter - 1), s_even)
            weighted_values(it + 1, s_odd)
            return carry

        lax.fori_loop(0, n_iter // 2, body, 0)

    run_group(0, q0, k0, v0)
    run_group(1, q1, k1, v1)
    run_group(2, q2, k2, v2)

    def finish(it, carry):
        for k in range(ATT_UNROLL):
            rows = pl.ds(pl.multiple_of((it * ATT_UNROLL + k) * ATT_BLOCK, ATT_BLOCK), ATT_BLOCK)
            ms = [mst[g, rows, :] for g in range(3)]
            top = jnp.maximum(jnp.maximum(ms[0], ms[1]), ms[2])
            scale = [jnp.exp2(m - top) for m in ms]
            num = scale[0] * acc[0, rows, :] + scale[1] * acc[1, rows, :] + scale[2] * acc[2, rows, :]
            den = scale[0] * lst[0, rows, :] + scale[1] * lst[1, rows, :] + scale[2] * lst[2, rows, :]
            o_ref[0, rows, :] = (num / den).astype(BF16)
        return carry

    lax.fori_loop(0, nblk // ATT_UNROLL, finish, 0)


def _attention(qkv0, qkv1, qkv2, slopes):
    b, _, s, _ = qkv0.shape
    pairs = HEADS_PER_GROUP // 2
    col_blocks = HEADS_PER_GROUP * HEAD_DIM // LANES

    def specs(arr):
        _, d, sub, _ = arr.shape
        return [pl.BlockSpec((1, d, sub, LANES),
                             functools.partial(lambda bi, p, sec: (bi, 0, 0, sec * col_blocks + p), sec=sec))
                for sec in range(3)]

    return pl.pallas_call(
        _attn_kernel,
        grid=(b, pairs),
        in_specs=[pl.BlockSpec(memory_space=pltpu.SMEM)] + specs(qkv0) + specs(qkv1) + specs(qkv2),
        out_specs=pl.BlockSpec((1, s, LANES), lambda bi, p: (bi, 0, p)),
        out_shape=jax.ShapeDtypeStruct((b, s, ATT_OUT_WIDTH), BF16),
        scratch_shapes=[pltpu.VMEM((3, 2, 2, ATT_BLOCK, 2 * ATT_BLOCK), F32),
                        pltpu.VMEM((3, s, LANES), F32),
                        pltpu.VMEM((3, s, LANES), F32),
                        pltpu.VMEM((3, s, LANES), F32),
                        pltpu.VMEM((2 * ATT_UNROLL, ATT_BLOCK, 2 * ATT_BLOCK), F32),
                        pltpu.VMEM((2 * ATT_UNROLL, ATT_BLOCK, 2 * ATT_BLOCK), F32)],
        compiler_params=pltpu.CompilerParams(
            dimension_semantics=("arbitrary", "arbitrary"), vmem_limit_bytes=VMEM_LIMIT),
        name="attention",
    )(slopes, qkv0, qkv0, qkv0, qkv1, qkv1, qkv1, qkv2, qkv2, qkv2)


def _mixtail_kernel(u_ref, halo_ref, att_ref, gate_ref, x_ref,
                    gt_m_ref, sc_f_ref, sh_f_ref, g_post_ref, g_pre_ref,
                    wpg_ref, pscale_ref, wbp_ref, wba_ref, wout_ref, wr_ref, br_ref,
                    xmid_ref, route_ref, meta_ref, counts_ref, xs_hbm,
                    pu, lv, xbuf, zbuf, fill, meta_s, sem, sem_s):
    i = pl.program_id(1)
    tm = x_ref.shape[1]
    step = pl.program_id(0) * pl.num_programs(1) + i
    last = pl.num_programs(0) * pl.num_programs(1) - 1
    slot = step % 2
    region = _region_rows(pl.num_programs(0) * pl.num_programs(1) * tm)

    @pl.when(step == 0)
    def _():
        fill[...] = jnp.zeros_like(fill)
        pu[0:POOL_PAD - POOL_HALO, :] = jnp.zeros((POOL_PAD - POOL_HALO, POOL_WIDTH), F32)
        lv[:, :, 0:POOL_PAD - POOL_HALO, :] = jnp.zeros(
            (SORT_GROUPS, 2, POOL_PAD - POOL_HALO, POOL_GROUP_DIM), F32)
        for sub in range(SORT_GROUPS):
            for e in range(N_EXPERTS):
                meta_s[SORT_GROUPS + sub, 1, e] = 0
                meta_s[SORT_GROUPS + sub, 0, e] = 0

    def send_tile(which):
        for sub in range(SORT_GROUPS):
            m = which * SORT_GROUPS + sub
            _tile_run_copies(lambda e: meta_s[m, 1, e], xbuf.at[which, sub], None,
                             xs_hbm, lambda e: e * region + meta_s[m, 0, e], sem.at[which])

    def wait_tile(which):
        for sub in range(SORT_GROUPS):
            pltpu.make_async_copy(xbuf.at[which, sub], xs_hbm.at[pl.ds(0, SORT_ROWS * ROW_WORDS), :],
                                  sem.at[which]).wait()

    @pl.when(step > 1)
    def _():
        wait_tile(slot)

    send_tile(1 - slot)

    head = POOL_PAD - POOL_HALO
    halo = halo_ref[0].astype(F32)
    pu[head:POOL_PAD, :] = jnp.where(i > 0, halo, jnp.zeros_like(halo))
    pu[POOL_PAD:POOL_PAD + tm, :] = u_ref[0].astype(F32)

    ts = SORT_TOKENS

    def pool(sub, st):
        base = sub * ts
        t = i * tm + base + lax.broadcasted_iota(I32, (ts, 1), 0)
        pooled_groups = []
        for g, w in enumerate(POOL_WINDOWS):
            cols = slice(g * POOL_GROUP_DIM, (g + 1) * POOL_GROUP_DIM)
            read = lambda start, n: pu[pl.ds(base + start, n), cols]
            shift, level = 1, 0
            while 2 * shift < w:
                partial = read(head, ts + POOL_HALO) + read(head - shift, ts + POOL_HALO)
                buf = lv.at[sub, level % 2]
                buf[pl.ds(head, ts + POOL_HALO), :] = partial
                read = lambda start, n, buf=buf: buf[pl.ds(start, n), :]
                shift, level = 2 * shift, level + 1
            win = read(POOL_PAD, ts) + read(POOL_PAD - shift, ts)
            count = jnp.minimum(t + 1, w).astype(F32)
            pooled_groups.append((win / count - pu[pl.ds(base + POOL_PAD, ts), cols]).astype(BF16))
        st["pooled"] = jnp.concatenate(pooled_groups, axis=1)

    neg_inf = -jnp.inf
    far = float(LANES)
    before = (lax.broadcasted_iota(I32, (ts, ts), 0) < lax.broadcasted_iota(I32, (ts, ts), 1))
    earlier = jnp.where(before, 1.0, 0.0).astype(BF16)
    erow = lax.broadcasted_iota(I32, (N_EXPERTS, ts), 0).astype(F32)
    e_col = lax.broadcasted_iota(I32, (N_EXPERTS, LANES), 0)
    e_lane = lax.broadcasted_iota(I32, (N_EXPERTS, LANES), 1)
    row8 = lax.broadcasted_iota(I32, (SUBLANES, ts), 0)
    mrow = lax.broadcasted_iota(I32, (SUBLANES, LANES), 0)
    srow = lax.broadcasted_iota(I32, (SORT_ROWS, ts), 0).astype(F32)
    filled = fill[...]

    def rows_of(sub):
        return slice(sub * ts, (sub + 1) * ts)

    def branch_projections(sub, st):
        rows = rows_of(sub)
        mixed = _dot(st.pop("pooled"), wpg_ref[...]) * pscale_ref[...]
        st["y_pool"] = _dot(mixed.astype(BF16), wbp_ref[...])
        st["y_att"] = _dot(att_ref[0, rows, :], wba_ref[...])

    def gated_sum(sub, st):
        rows = rows_of(sub)
        st["merged"] = (gate_ref[0, rows, 0:D_MODEL] * st.pop("y_pool").astype(BF16)
                        + gate_ref[0, rows, D_MODEL:2 * D_MODEL] * st.pop("y_att").astype(BF16))

    def output_projection(sub, st):
        st["y"] = _dot(st.pop("merged"), wout_ref[...])

    def residual_and_ffn_input(sub, st):
        rows = rows_of(sub)
        x_mid = x_ref[0, rows, :] + _rmsnorm(st.pop("y"), gt_m_ref[0] * g_post_ref[...])
        xmid_ref[0, rows, :] = x_mid
        h2 = _rmsnorm(x_mid, g_pre_ref[...] * (1.0 + sc_f_ref[0])) + sh_f_ref[0]
        st["h2b"] = h2.astype(BF16)
        st["h2lo"] = (h2 - st["h2b"].astype(F32)).astype(BF16)

    def router_logits(sub, st):
        by_hi = _dot_nt(wr_ref[...], st["h2b"])
        st["logits"] = (by_hi[0:ROUTER_ROWS] + (by_hi[ROUTER_ROWS:2 * ROUTER_ROWS]
                        + _dot_nt(wr_ref[0:ROUTER_ROWS, :], st.pop("h2lo"))) + br_ref[...])

    def route(sub, st):
        logits = st.pop("logits")
        gl = logits[N_EXPERTS:N_EXPERTS + N_EXPERT_GROUPS, :]
        grow = lax.broadcasted_iota(I32, gl.shape, 0).astype(F32)
        gmax = jnp.max(gl, axis=0, keepdims=True)
        gsel = jnp.min(jnp.where(gl == gmax, grow, far), axis=0, keepdims=True)
        p_group = 1.0 / jnp.sum(jnp.exp(gl - gmax), axis=0, keepdims=True)
        e_lo = gsel * float(EXPERTS_PER_GROUP)
        el = jnp.where((erow >= e_lo) & (erow < e_lo + float(EXPERTS_PER_GROUP)),
                       logits[0:N_EXPERTS, :], neg_inf)
        v1 = jnp.max(el, axis=0, keepdims=True)
        i1 = jnp.min(jnp.where(el == v1, erow, far), axis=0, keepdims=True)
        el2 = jnp.where(erow == i1, neg_inf, el)
        v2 = jnp.max(el2, axis=0, keepdims=True)
        i2 = jnp.min(jnp.where(el2 == v2, erow, far), axis=0, keepdims=True)
        e21 = jnp.exp(v2 - v1)
        st["w1"] = p_group / (1.0 + e21)
        st["w2"] = p_group * e21 / (1.0 + e21)
        st["pick1"] = erow == i1
        st["pick2"] = erow == i2
        st["assign"] = jnp.where(st["pick1"] | st["pick2"], 1.0, 0.0)

    def count_and_rank(sub, st):
        assign = st.pop("assign")
        st["rank"] = _dot(assign.astype(BF16), earlier)
        assign_pad = jnp.concatenate([assign, jnp.zeros((LANES - N_EXPERTS, ts), F32)], axis=0).astype(BF16)
        st["cnt_row"] = _dot_nt(jnp.ones((SUBLANES, ts), BF16), assign_pad)

    def sorted_positions(sub, st):
        cnt_row = st["cnt_row"]
        run_start = jnp.sum(jnp.where(e_lane < e_col, cnt_row[0:1, :], 0.0), axis=1, keepdims=True)
        pos = st.pop("rank") + run_start
        key1 = jnp.sum(jnp.where(st.pop("pick1"), pos, 0.0), axis=0, keepdims=True)
        key2 = jnp.sum(jnp.where(st.pop("pick2"), pos, 0.0), axis=0, keepdims=True)
        route_ref[:, rows_of(sub)] = jnp.where(row8 == 0, key1, jnp.where(row8 == 1, key2,
                                               jnp.where(row8 == 2, st.pop("w1"),
                                                         jnp.where(row8 == 3, st.pop("w2"), 0.0))))
        st["perm"] = jnp.where((srow == key1) | (srow == key2), 1.0, 0.0).astype(BF16)

    def sort_rows(sub, st):
        perm, h2b = st.pop("perm"), st.pop("h2b")
        for w in range(ROW_WORDS):
            pair = _dot(perm, h2b[:, 2 * w * LANES:(2 * w + 2) * LANES])
            _pack_rows(xbuf.at[slot, sub], w, SORT_ROWS, pair[:, :LANES], pair[:, LANES:])

    state = _run_staggered((pool, branch_projections, gated_sum, output_projection,
                            residual_and_ffn_input, router_logits, route, count_and_rank,
                            sorted_positions, sort_rows), SORT_GROUPS, lag=0)

    for sub in range(SORT_GROUPS):
        cnt_row = state[sub]["cnt_row"]
        meta_ref[sub] = jnp.where(mrow == 0, filled, jnp.where(mrow == 1, cnt_row, 0.0)).astype(I32)
        filled = filled + cnt_row

    fill[...] = filled
    counts_ref[...] = filled.astype(I32)

    meta_copy = pltpu.make_async_copy(meta_ref, meta_s.at[pl.ds(slot * SORT_GROUPS, SORT_GROUPS)], sem_s)
    meta_copy.start()
    meta_copy.wait()

    @pl.when(step == last)
    def _():
        send_tile(slot)

        @pl.when(step > 0)
        def _():
            wait_tile(1 - slot)

        wait_tile(slot)
        zbuf[...] = jnp.zeros_like(zbuf)
        final = slot * SORT_GROUPS + SORT_GROUPS - 1

        def pad_copy(e):
            end = e * region + meta_s[final, 0, e] + meta_s[final, 1, e]
            return pltpu.make_async_copy(
                zbuf, xs_hbm.at[pl.ds(end * ROW_WORDS, MOE_BLOCK * ROW_WORDS), :], sem_s)

        def start_pad(e, carry):
            pad_copy(e).start()
            return carry

        def wait_pad(e, carry):
            pad_copy(e).wait()
            return carry

        lax.fori_loop(0, N_EXPERTS, start_pad, 0)
        lax.fori_loop(0, N_EXPERTS, wait_pad, 0)


def _mixtail(u, att, gates, x, gt_m, sc_f, sh_f, g_post, g_pre,
             wpg, pscale, wbp, wba, wout, wr, br):
    b, s, d = x.shape
    tm = ROW_TILE
    tiles = s // tm
    n_tiles = b * tiles
    halo_blocks = tm // POOL_HALO
    region = _region_rows(b * s)
    const2 = lambda bi, i: (0, 0)
    const3 = lambda bi, i: (0, 0, 0)
    per_b = lambda bi, i: (bi, 0, 0)
    tile = lambda bi, i: (bi, i, 0)
    single = dict(pipeline_mode=pl.Buffered(1))
    return pl.pallas_call(
        _mixtail_kernel,
        grid=(b, tiles),
        in_specs=[pl.BlockSpec((1, tm, POOL_WIDTH), tile),
                  pl.BlockSpec((1, POOL_HALO, POOL_WIDTH),
                               lambda bi, i: (bi, jnp.maximum(i * halo_blocks - 1, 0), 0)),
                  pl.BlockSpec((1, tm, ATT_OUT_WIDTH), tile),
                  pl.BlockSpec((1, tm, 2 * D_MODEL), tile),
                  pl.BlockSpec((1, tm, d), tile),
                  pl.BlockSpec((1, 1, d), per_b),
                  pl.BlockSpec((1, 1, d), per_b),
                  pl.BlockSpec((1, 1, d), per_b),
                  pl.BlockSpec((1, d), const2),
                  pl.BlockSpec((1, d), const2),
                  pl.BlockSpec(wpg.shape, const2, **single),
                  pl.BlockSpec((1, POOL_WIDTH), const2),
                  pl.BlockSpec(wbp.shape, const2, **single),
                  pl.BlockSpec(wba.shape, const2, **single),
                  pl.BlockSpec(wout.shape, const2, **single),
                  pl.BlockSpec(wr.shape, const2, **single),
                  pl.BlockSpec(br.shape, const2)],
        out_specs=[pl.BlockSpec((1, tm, d), tile),
                   pl.BlockSpec((SUBLANES, tm), lambda bi, i: (0, bi * tiles + i)),
                   pl.BlockSpec((SORT_GROUPS, SUBLANES, LANES), lambda bi, i: (bi * tiles + i, 0, 0)),
                   pl.BlockSpec((SUBLANES, LANES), const2),
                   pl.BlockSpec(memory_space=pl.ANY)],
        out_shape=[jax.ShapeDtypeStruct((b, s, d), F32),
                   jax.ShapeDtypeStruct((SUBLANES, b * s), F32),
                   jax.ShapeDtypeStruct((n_tiles * SORT_GROUPS, SUBLANES, LANES), I32),
                   jax.ShapeDtypeStruct((SUBLANES, LANES), I32),
                   jax.ShapeDtypeStruct((N_EXPERTS * region * ROW_WORDS, LANES), I32)],
        scratch_shapes=[pltpu.VMEM((POOL_PAD + tm, POOL_WIDTH), F32),
                        pltpu.VMEM((SORT_GROUPS, 2, POOL_PAD + SORT_TOKENS, POOL_GROUP_DIM), F32),
                        pltpu.VMEM((2, SORT_GROUPS, SORT_ROWS * ROW_WORDS, LANES), I32),
                        pltpu.VMEM((MOE_BLOCK * ROW_WORDS, LANES), I32),
                        pltpu.VMEM((SUBLANES, LANES), F32),
                        pltpu.SMEM((2 * SORT_GROUPS, SUBLANES, LANES), I32),
                        pltpu.SemaphoreType.DMA((2,)),
                        pltpu.SemaphoreType.DMA(())],
        compiler_params=pltpu.CompilerParams(
            dimension_semantics=("arbitrary", "arbitrary"), vmem_limit_bytes=VMEM_LIMIT),
        name="mixtail",
    )(u, u, att, gates, x, gt_m, sc_f, sh_f, g_post, g_pre,
      wpg, pscale, wbp, wba, wout, wr, br)


def _expert_kernel(counts_ref, xs_hbm, wg_ref, wu_ref, wd_ref, ys_hbm,
                   xbuf, ybuf, state, semx, semy):
    e = pl.program_id(0)
    bm = MOE_BLOCK
    nx, ny = EXPERT_X_BUFFERS, EXPERT_Y_BUFFERS
    block_words = bm * ROW_WORDS
    region = xs_hbm.shape[0] // (N_EXPERTS * ROW_WORDS)

    def n_blocks(ex):
        return (counts_ref[ex] + (bm - 1)) // bm

    def block_rows(ref, ex, k):
        start = pl.multiple_of((ex * region + k * bm) * ROW_WORDS, block_words)
        return ref.at[pl.ds(start, block_words), :]

    def x_copy(ex, k, s):
        return pltpu.make_async_copy(block_rows(xs_hbm, ex, k), xbuf.at[s], semx.at[s])

    def y_copy(k, s):
        return pltpu.make_async_copy(ybuf.at[s], block_rows(ys_hbm, e, k), semy.at[s])

    @pl.when(e == 0)
    def _():
        for j in range(4):
            state[j] = 0

    def fetch_through(target):
        def more(c):
            pe, _, pg = c
            return (pg < target) & (pe < N_EXPERTS)

        def step(c):
            pe, pk, pg = c
            has = pk < n_blocks(pe)

            @pl.when(has)
            def _():
                x_copy(pe, pk, pg % nx).start()

            return (jnp.where(has, pe, pe + 1), jnp.where(has, pk + 1, 0), pg + has.astype(I32))

        pe, pk, pg = lax.while_loop(more, step, (state[1], state[2], state[3]))
        state[1] = pe
        state[2] = pk
        state[3] = pg

    nb = n_blocks(e)
    done = state[0]

    def blocks(k0, count):
        ks = [k0 + c for c in range(count)]
        gs = [done + k for k in ks]
        fetch_through(gs[0] + nx)
        for k, g in zip(ks, gs):
            x_copy(e, k, g % nx).wait()

            @pl.when(g >= ny)
            def _():
                y_copy(k, g % ny).wait()

        mids = []
        for g in gs:
            x = _unpack_rows(xbuf.at[g % nx], bm).astype(BF16)
            a = _dot(x, wg_ref[...].astype(BF16))
            u = _dot(x, wu_ref[...].astype(BF16))
            mids.append(((a * _sigmoid(a)) * u).astype(BF16))
        for k, g, mid in zip(ks, gs, mids):
            for w in range(ROW_WORDS):
                pair = _bf16_exact(_dot(mid, wd_ref[:, 2 * w * LANES:(2 * w + 2) * LANES].astype(BF16)))
                _pack_rows(ybuf.at[g % ny], w, bm, pair[:, :LANES], pair[:, LANES:])
            y_copy(k, g % ny).start()

    per = EXPERT_CHAINS

    def several_blocks(p, carry):
        blocks(per * p, per)
        return carry

    def single_block(k, carry):
        blocks(k, 1)
        return carry

    lax.fori_loop(0, nb // per, several_blocks, 0)
    lax.fori_loop((nb // per) * per, nb, single_block, 0)

    state[0] = done + nb

    @pl.when(e == N_EXPERTS - 1)
    def _():
        total = done + nb
        for j in range(ny):
            @pl.when(total > j)
            def _():
                y_copy(0, (total - 1 - j) % ny).wait()


def _experts(xs, counts, w_gate, w_up, w_down):
    bm = MOE_BLOCK
    w_in = pl.BlockSpec((None, D_MODEL, D_EXPERT), lambda e, cnt: (e, 0, 0))
    w_out = pl.BlockSpec((None, D_EXPERT, D_MODEL), lambda e, cnt: (e, 0, 0))
    grid_spec = pltpu.PrefetchScalarGridSpec(
        num_scalar_prefetch=1,
        grid=(N_EXPERTS,),
        in_specs=[pl.BlockSpec(memory_space=pl.ANY), w_in, w_in, w_out],
        out_specs=pl.BlockSpec(memory_space=pl.ANY),
        scratch_shapes=[pltpu.VMEM((EXPERT_X_BUFFERS, bm * ROW_WORDS, LANES), I32),
                        pltpu.VMEM((EXPERT_Y_BUFFERS, bm * ROW_WORDS, LANES), I32),
                        pltpu.SMEM((4,), I32),
                        pltpu.SemaphoreType.DMA((EXPERT_X_BUFFERS,)),
                        pltpu.SemaphoreType.DMA((EXPERT_Y_BUFFERS,))],
    )
    return pl.pallas_call(
        _expert_kernel,
        grid_spec=grid_spec,
        out_shape=jax.ShapeDtypeStruct(xs.shape, I32),
        compiler_params=pltpu.CompilerParams(
            dimension_semantics=("arbitrary",), vmem_limit_bytes=VMEM_LIMIT),
        name="experts",
    )(counts, xs, w_gate, w_up, w_down)


def _combine_kernel(meta_ref, y_hbm, route_ref, xmid_ref, gt_ref, g_ref, o_ref, ybuf, rt_scr, sem):
    step = pl.program_id(0)
    n_steps = pl.num_programs(0)
    tm = ROW_TILE
    slot = step % COMBINE_BUFFERS
    region = _region_rows(n_steps * tm)

    def fetch_tile(tile, live):
        tile = jnp.minimum(tile, n_steps - 1)
        which = tile % COMBINE_BUFFERS
        for sub in range(SORT_GROUPS):
            m = (tile * SORT_GROUPS + sub) * 2
            _tile_run_copies(lambda e: jnp.where(live, meta_ref[(m + 1) * N_EXPERTS + e], 0),
                             y_hbm, lambda e: e * region + meta_ref[m * N_EXPERTS + e],
                             ybuf.at[which, sub], None, sem.at[which])

    @pl.when(step == 0)
    def _():
        for ahead in range(COMBINE_BUFFERS - 1):
            fetch_tile(ahead, ahead < n_steps)

    for sub in range(SORT_GROUPS):
        pltpu.make_async_copy(y_hbm.at[pl.ds(0, SORT_ROWS * ROW_WORDS), :], ybuf.at[slot, sub],
                              sem.at[slot]).wait()
    sorted_rows = [_unpack_rows(ybuf.at[slot, sub], SORT_ROWS).astype(BF16) for sub in range(SORT_GROUPS)]

    fetch_tile(step + COMBINE_BUFFERS - 1, step + COMBINE_BUFFERS - 1 < n_steps)

    rt_scr[...] = jnp.zeros_like(rt_scr)
    rt_scr[0:SUBLANES, :] = route_ref[...]
    cols = jnp.concatenate([rt_scr[:, c * LANES:(c + 1) * LANES].T for c in range(tm // LANES)], axis=0)

    ts = SORT_TOKENS
    spos = lax.broadcasted_iota(I32, (ts, SORT_ROWS), 1).astype(F32)
    def rows_of(sub):
        return slice(sub * ts, (sub + 1) * ts)

    def gate_matrix(sub, st):
        key1, key2, w1, w2 = (cols[rows_of(sub), c:c + 1] for c in range(4))
        st["gates"] = jnp.where(spos == key1, w1, jnp.where(spos == key2, w2, 0.0)).astype(BF16)

    def weighted_unsort(sub, st):
        st["y"] = _dot(st.pop("gates"), sorted_rows[sub])

    def norm_and_add(sub, st):
        rows = rows_of(sub)
        o_ref[rows, :] = xmid_ref[rows, :] + _rmsnorm(st.pop("y"), gt_ref[0] * g_ref[...])

    _run_staggered((gate_matrix, weighted_unsort, norm_and_add), SORT_GROUPS, lag=1)


def _combine(meta, ys, route, x_mid, gt_f, g_post, seq):
    t, d = x_mid.shape
    tm = ROW_TILE
    tiles_per_seq = seq // tm
    grid_spec = pltpu.PrefetchScalarGridSpec(
        num_scalar_prefetch=1,
        grid=(t // tm,),
        in_specs=[pl.BlockSpec(memory_space=pl.ANY),
                  pl.BlockSpec((SUBLANES, tm), lambda i, m: (0, i)),
                  pl.BlockSpec((tm, d), lambda i, m: (i, 0)),
                  pl.BlockSpec((1, 1, d), lambda i, m: (i // tiles_per_seq, 0, 0)),
                  pl.BlockSpec((1, d), lambda i, m: (0, 0))],
        out_specs=pl.BlockSpec((tm, d), lambda i, m: (i, 0)),
        scratch_shapes=[pltpu.VMEM((COMBINE_BUFFERS, SORT_GROUPS, SORT_ROWS * ROW_WORDS, LANES), I32),
                        pltpu.VMEM((LANES, tm), F32),
                        pltpu.SemaphoreType.DMA((COMBINE_BUFFERS,))],
    )
    return pl.pallas_call(
        _combine_kernel,
        grid_spec=grid_spec,
        out_shape=jax.ShapeDtypeStruct((t, d), F32),
        compiler_params=pltpu.CompilerParams(
            dimension_semantics=("arbitrary",), vmem_limit_bytes=VMEM_LIMIT),
        name="combine",
    )(meta, ys, route, x_mid, gt_f, g_post)


def kernel(x, c, w_ada, b_ada, g_pre_mix, g_post_mix, g_pre_ffn, g_post_ffn, w_in, w_pool_group, pool_scale, w_branch_pool, w_branch_att, w_out, w_group_router, b_group_router, w_expert_router, b_expert_router, w_exp_gate, w_exp_up, w_exp_down):
    b, s, d = x.shape
    assert d == D_MODEL and s % (ATT_BLOCK * 2 * ATT_DILATIONS[2]) == 0 and s % ROW_TILE == 0
    assert (b * s) % MOE_BLOCK == 0
    depth = w_ada.shape[0]
    slopes = jnp.exp2(-ALIBI_MAX_BIAS * jnp.arange(1, N_ATT_HEADS + 1, dtype=F32) / N_ATT_HEADS)
    q_scale = HEAD_DIM ** -0.5 * LOG2_E

    for layer in range(depth):
        mod = _adaln(c, w_ada[layer], b_ada[layer]).reshape(b, 6, 1, d)
        sh_m, sc_m, gt_m, sh_f, sc_f, gt_f = [mod[:, j] for j in range(6)]

        wl = w_in[layer]
        q_lo, k_lo, v_lo = POOL_WIDTH, POOL_WIDTH + 768, POOL_WIDTH + 2 * 768
        group_cols = []
        for g in range(3):
            sl = slice(g * 256, (g + 1) * 256)
            group_cols += [wl[:, q_lo:k_lo][:, sl] * q_scale, wl[:, k_lo:v_lo][:, sl],
                           wl[:, v_lo:v_lo + 768][:, sl]]
        w_perm = jnp.concatenate([wl[:, :POOL_WIDTH]] + group_cols + [wl[:, v_lo + 768:]],
                                 axis=1).astype(BF16)

        u, qkv0, qkv1, qkv2, gates = _inproj(x, g_pre_mix[layer].reshape(1, d), sc_m, sh_m, w_perm)
        att = _attention(qkv0, qkv1, qkv2, slopes)

        zero_block = jnp.zeros((POOL_GROUP_DIM, POOL_GROUP_DIM), F32)
        wpg_diag = jnp.block([[w_pool_group[layer, g] if g == h else zero_block
                               for h in range(len(POOL_WINDOWS))] for g in range(len(POOL_WINDOWS))])

        pad_rows = ROUTER_ROWS - N_EXPERTS - N_EXPERT_GROUPS
        wrt = jnp.concatenate([w_expert_router[layer].T, w_group_router[layer].T,
                               jnp.zeros((pad_rows, d), F32)], axis=0)
        wrt_hi = wrt.astype(BF16)
        wr = jnp.concatenate([wrt_hi, (wrt - wrt_hi.astype(F32)).astype(BF16)], axis=0)
        br = jnp.concatenate([b_expert_router[layer], b_group_router[layer],
                              jnp.zeros((pad_rows,), F32)]).reshape(ROUTER_ROWS, 1)

        x_mid, route, meta, counts, xs = _mixtail(
            u, att, gates, x, gt_m, sc_f, sh_f,
            g_post_mix[layer].reshape(1, d), g_pre_ffn[layer].reshape(1, d),
            wpg_diag.astype(BF16), pool_scale[layer].reshape(1, POOL_WIDTH),
            w_branch_pool[layer].astype(BF16), w_branch_att[layer].astype(BF16),
            w_out[layer].astype(BF16), wr, br)

        ys = _experts(xs, counts[0, :N_EXPERTS], w_exp_gate[layer], w_exp_up[layer], w_exp_down[layer])
        run_meta = meta[:, 0:2, 0:N_EXPERTS].reshape(-1)
        x = _combine(run_meta, ys, route, x_mid.reshape(b * s, d), gt_f,
                     g_post_ffn[layer].reshape(1, d), s).reshape(b, s, d)
    return x
```

```python
import functools

import jax
import jax.numpy as jnp
from jax import lax
from jax.experimental import pallas as pl
from jax.experimental.pallas import tpu as pltpu

F32 = jnp.float32
BF16 = jnp.bfloat16
I32 = jnp.int32

D_MODEL = 1024
LANES = 128
SUBLANES = 8
ROW_WORDS = D_MODEL // (2 * LANES)
HI_HALF = -65536

POOL_WINDOWS = (2, 4, 8, 16)
POOL_GROUP_DIM = 128
POOL_WIDTH = 512
POOL_HALO = 16
POOL_PAD = POOL_HALO + 8

HEAD_DIM = 64
ATT_DILATIONS = (1, 4, 16)
ATT_REACH = 128
ATT_BLOCK = 128
ATT_UNROLL = 4
HEADS_PER_GROUP = 4
N_ATT_HEADS = 12
GROUP_QKV = 3 * HEADS_PER_GROUP * HEAD_DIM
ATT_OUT_WIDTH = 256
ALIBI_MAX_BIAS = 8.0
IN_WIDTH = POOL_WIDTH + 3 * GROUP_QKV + 2 * D_MODEL
MASKED = -1e30
LOG2_E = 1.4426950408889634

N_EXPERT_GROUPS = 4
EXPERTS_PER_GROUP = 8
N_EXPERTS = 32
ROUTER_ROWS = 48
D_EXPERT = 512
RMS_EPS = 1e-6

ROW_TILE = 1024
ROW_SPLIT = 4
SORT_TOKENS = 256
SORT_GROUPS = ROW_TILE // SORT_TOKENS
SORT_ROWS = 2 * SORT_TOKENS
MOE_BLOCK = 256
EXPERT_CHAINS = 2
EXPERT_X_BUFFERS = 6
EXPERT_Y_BUFFERS = 4
COMBINE_BUFFERS = 3
VMEM_LIMIT = 52 * 1024 * 1024


def _sigmoid(x):
    return 0.5 * jnp.tanh(0.5 * x) + 0.5


def _rmsnorm(x, g):
    return x * lax.rsqrt(jnp.mean(x * x, axis=-1, keepdims=True) + RMS_EPS) * g


def _dot(a, b):
    return jnp.dot(a, b, preferred_element_type=F32)


def _dot_nt(a, b):
    return lax.dot_general(a, b, (((1,), (1,)), ((), ())), preferred_element_type=F32)


def _pack_rows(ref, w, n, lo, hi):
    word = (lax.shift_right_logical(pltpu.bitcast(lo, I32), 16) | (pltpu.bitcast(hi, I32) & HI_HALF))
    ref[pl.ds(w, n, stride=ROW_WORDS), :] = word


def _unpack_rows(ref, n):
    cols = []
    for w in range(ROW_WORDS):
        word = ref[pl.ds(w, n, stride=ROW_WORDS), :]
        cols += [pltpu.bitcast(word << 16, F32), pltpu.bitcast(word & HI_HALF, F32)]
    return jnp.concatenate(cols, axis=1)


def _bf16_exact(x):
    return x.astype(BF16).astype(F32)


def _run_staggered(stages, n_chains, lag):
    state = [{} for _ in range(n_chains)]
    for tick in range(len(stages) + lag * (n_chains - 1)):
        for chain in range(n_chains):
            if 0 <= tick - lag * chain < len(stages):
                stages[tick - lag * chain](chain, state[chain])
    return state


def _region_rows(n_tokens):
    return n_tokens + MOE_BLOCK


def _run_copies(n, src, src_row, dst, dst_row, sem):
    @pl.when(n > 0)
    def _():
        pltpu.make_async_copy(
            src.at[pl.ds(src_row * ROW_WORDS, n * ROW_WORDS), :],
            dst.at[pl.ds(dst_row * ROW_WORDS, n * ROW_WORDS), :], sem).start()


def _tile_run_copies(counts, src, src_rows, dst, dst_rows, sem):
    local = 0
    for e in range(N_EXPERTS):
        n = counts(e)
        _run_copies(n, src, local if src_rows is None else src_rows(e),
                    dst, local if dst_rows is None else dst_rows(e), sem)
        local = local + n


def _adaln_kernel(c_ref, w_ref, b_ref, o_ref):
    c = c_ref[...]
    a = c * _sigmoid(c)
    w = w_ref[...]
    a_hi, w_hi = a.astype(BF16), w.astype(BF16)
    a_lo = (a - a_hi.astype(F32)).astype(BF16)
    w_lo = (w - w_hi.astype(F32)).astype(BF16)
    o_ref[...] = _dot(a_hi, w_hi) + (_dot(a_hi, w_lo) + _dot(a_lo, w_hi)) + b_ref[...]


def _adaln(c, w_ada, b_ada):
    b, d = c.shape
    n = w_ada.shape[1]
    rows = -(-b // SUBLANES) * SUBLANES
    cp = jnp.pad(c, ((0, rows - b), (0, 0)))
    nt = 1536
    out = pl.pallas_call(
        _adaln_kernel,
        grid=(n // nt,),
        in_specs=[pl.BlockSpec((rows, d), lambda j: (0, 0)),
                  pl.BlockSpec((d, nt), lambda j: (0, j)),
                  pl.BlockSpec((1, nt), lambda j: (0, j))],
        out_specs=pl.BlockSpec((rows, nt), lambda j: (0, j)),
        out_shape=jax.ShapeDtypeStruct((rows, n), F32),
        compiler_params=pltpu.CompilerParams(vmem_limit_bytes=VMEM_LIMIT),
        name="adaln",
    )(cp, w_ada, b_ada.reshape(1, n))
    return out[:b]


def _inproj_kernel(x_ref, g_ref, sc_ref, sh_ref, w_ref,
                   u_ref, qkv0_ref, qkv1_ref, qkv2_ref, gate_ref, h_scr, p_scr):
    tm = x_ref.shape[1]
    hm = tm // ROW_SPLIT
    for part in range(ROW_SPLIT):
        rows = slice(part * hm, (part + 1) * hm)
        h = _rmsnorm(x_ref[0, rows, :], g_ref[...] * (1.0 + sc_ref[0])) + sh_ref[0]
        h_scr[rows, :] = h.astype(BF16)
        hb = h_scr[rows, :]

        u_ref[0, rows, :] = _dot(hb, w_ref[:, 0:POOL_WIDTH]).astype(BF16)

        col = POOL_WIDTH
        qkv0_ref[0, 0, rows, :] = _dot(hb, w_ref[:, col:col + GROUP_QKV]).astype(BF16)
        for gi, (out_ref, d) in enumerate(((qkv1_ref, ATT_DILATIONS[1]), (qkv2_ref, ATT_DILATIONS[2]))):
            col += GROUP_QKV
            proj = _dot(hb, w_ref[:, col:col + GROUP_QKV])
            stage = p_scr.at[part, gi]
            for cb in range(GROUP_QKV // LANES):
                stage[cb] = proj[:, cb * LANES:(cb + 1) * LANES]
            sub = hm // d
            for r in range(d):
                out_ref[0, r, part * sub:(part + 1) * sub, :] = jnp.concatenate(
                    [stage[cb, pl.ds(r, sub, stride=d), :] for cb in range(GROUP_QKV // LANES)],
                    axis=1).astype(BF16)
        col += GROUP_QKV

        chunk = 512
        for j in range(2 * D_MODEL // chunk):
            g = _dot(hb, w_ref[:, col + j * chunk:col + (j + 1) * chunk])
            gate_ref[0, rows, j * chunk:(j + 1) * chunk] = _sigmoid(g).astype(BF16)


def _inproj(x, g_pre, sc, sh, w_perm):
    b, s, d = x.shape
    tm = ROW_TILE
    d1, d2 = ATT_DILATIONS[1], ATT_DILATIONS[2]
    grid = (b, s // tm)
    const = lambda bi, i: (0, 0)
    per_b = lambda bi, i: (bi, 0, 0)
    return pl.pallas_call(
        _inproj_kernel,
        grid=grid,
        in_specs=[pl.BlockSpec((1, tm, d), lambda bi, i: (bi, i, 0)),
                  pl.BlockSpec((1, d), const),
                  pl.BlockSpec((1, 1, d), per_b),
                  pl.BlockSpec((1, 1, d), per_b),
                  pl.BlockSpec((d, IN_WIDTH), const, pipeline_mode=pl.Buffered(1))],
        out_specs=[pl.BlockSpec((1, tm, POOL_WIDTH), lambda bi, i: (bi, i, 0)),
                   pl.BlockSpec((1, 1, tm, GROUP_QKV), lambda bi, i: (bi, 0, i, 0)),
                   pl.BlockSpec((1, d1, tm // d1, GROUP_QKV), lambda bi, i: (bi, 0, i, 0)),
                   pl.BlockSpec((1, d2, tm // d2, GROUP_QKV), lambda bi, i: (bi, 0, i, 0)),
                   pl.BlockSpec((1, tm, 2 * D_MODEL), lambda bi, i: (bi, i, 0))],
        out_shape=[jax.ShapeDtypeStruct((b, s, POOL_WIDTH), BF16),
                   jax.ShapeDtypeStruct((b, 1, s, GROUP_QKV), BF16),
                   jax.ShapeDtypeStruct((b, d1, s // d1, GROUP_QKV), BF16),
                   jax.ShapeDtypeStruct((b, d2, s // d2, GROUP_QKV), BF16),
                   jax.ShapeDtypeStruct((b, s, 2 * D_MODEL), BF16)],
        scratch_shapes=[pltpu.VMEM((tm, d), BF16),
                        pltpu.VMEM((ROW_SPLIT, 2, GROUP_QKV // LANES, tm // ROW_SPLIT, LANES), F32)],
        compiler_params=pltpu.CompilerParams(
            dimension_semantics=("arbitrary", "arbitrary"), vmem_limit_bytes=VMEM_LIMIT),
        name="inproj",
    )(x, g_pre, sc, sh, w_perm)


def _attn_kernel(slopes_ref, q0, k0, v0, q1, k1, v1, q2, k2, v2, o_ref,
                 bias_scr, acc, mst, lst, s_even, s_odd):
    pair = pl.program_id(1)
    seq = o_ref.shape[1]
    nblk = seq // ATT_BLOCK
    lane = lax.broadcasted_iota(I32, (ATT_BLOCK, LANES), 1)
    first_head = lane < HEAD_DIM
    half_lane = lax.broadcasted_iota(I32, (ATT_BLOCK // 2, LANES), 1)
    head_bits = (jnp.where(half_lane < HEAD_DIM, -1, 0), jnp.where(half_lane < HEAD_DIM, 0, -1))
    ones_block = jnp.ones((2 * ATT_BLOCK, LANES), BF16)

    qi = lax.broadcasted_iota(I32, (ATT_BLOCK, 2 * ATT_BLOCK), 0)
    kj = lax.broadcasted_iota(I32, (ATT_BLOCK, 2 * ATT_BLOCK), 1)
    delta = ATT_BLOCK + qi - kj
    valid = (delta >= 0) & (delta <= ATT_REACH)
    delta0 = qi - kj
    valid0 = delta0 >= 0
    for g, d in enumerate(ATT_DILATIONS):
        for j in range(2):
            slope = slopes_ref[g * HEADS_PER_GROUP + 2 * pair + j]
            slope = slope * LOG2_E
            bias_scr[g, j, 0] = jnp.where(valid0, -slope * (delta0 * d).astype(F32), MASKED)
            bias_scr[g, j, 1] = jnp.where(valid, -slope * (delta * d).astype(F32), MASKED)

    n_iter = nblk // ATT_UNROLL

    def group_stages(g, q_ref, k_ref, v_ref):
        d = ATT_DILATIONS[g]
        per_res = nblk // d

        def block_index(it, k):
            n = it * ATT_UNROLL + k
            return n // per_res, n % per_res

        def key_rows(i):
            lo = jnp.maximum(i - 1, 0)
            return pl.ds(pl.multiple_of(lo * ATT_BLOCK, ATT_BLOCK), 2 * ATT_BLOCK)

        def scores(it, s_ref):
            for k in range(ATT_UNROLL):
                r, i = block_index(it, k)
                q = q_ref[0, r, pl.ds(pl.multiple_of(i * ATT_BLOCK, ATT_BLOCK), ATT_BLOCK), :]
                kw = k_ref[0, r, key_rows(i), :]
                qbits = pltpu.bitcast(q, I32)
                for j in range(2):
                    qh = pltpu.bitcast(qbits & head_bits[j], BF16)
                    s_ref[2 * k + j] = _dot_nt(qh, kw) + bias_scr[g, j, jnp.minimum(i, 1)]

        def weighted_values(it, s_ref):
            for k in range(ATT_UNROLL):
                r, i = block_index(it, k)
                vw = jnp.concatenate([v_ref[0, r, key_rows(i), :], ones_block], axis=1)
                outs = []
                for j in range(2):
                    sc = s_ref[2 * k + j]
                    mx = jnp.max(sc, axis=-1, keepdims=True)
                    p = jnp.exp2(sc - mx)
                    num_den = _dot(p.astype(BF16), vw)
                    outs.append((num_den[:, :LANES], mx, num_den[:, LANES:]))
                (n0, m0, l0), (n1, m1, l1) = outs
                if d == 1:
                    rows = pl.ds(pl.multiple_of(i * ATT_BLOCK, ATT_BLOCK), ATT_BLOCK)
                else:
                    rows = pl.ds(i * (ATT_BLOCK * d) + r, ATT_BLOCK, stride=d)
                acc[g, rows, :] = jnp.where(first_head, n0, n1)
                mst[g, rows, :] = jnp.where(first_head, m0, m1)
                lst[g, rows, :] = jnp.where(first_head, l0, l1)

        return scores, weighted_values

    stages = [group_stages(0, q0, k0, v0), group_stages(1, q1, k1, v1), group_stages(2, q2, k2, v2)]
    stages[0][0](jnp.int32(0), s_even)
    for g, (scores, weighted_values) in enumerate(stages):
        def pair_of_iterations(it, then_scores):
            scores(it + 1, s_odd)
            weighted_values(it, s_even)
            then_scores()
            weighted_values(it + 1, s_odd)

        def body(h, carry, pair_of_iterations=pair_of_iterations, scores=scores):
            it = 2 * h
            pair_of_iterations(it, lambda: scores(it + 2, s_even))
            return carry

        lax.fori_loop(0, n_iter // 2 - 1, body, 0)
        if g + 1 < len(stages):
            next_scores = stages[g + 1][0]
            pair_of_iterations(jnp.int32(n_iter - 2), lambda: next_scores(jnp.int32(0), s_even))
        else:
            pair_of_iterations(jnp.int32(n_iter - 2), lambda: None)

    def finish(it, carry):
        for k in range(ATT_UNROLL):
            rows = pl.ds(pl.multiple_of((it * ATT_UNROLL + k) * ATT_BLOCK, ATT_BLOCK), ATT_BLOCK)
            ms = [mst[g, rows, :] for g in range(3)]
            top = jnp.maximum(jnp.maximum(ms[0], ms[1]), ms[2])
            scale = [jnp.exp2(m - top) for m in ms]
            num = scale[0] * acc[0, rows, :] + scale[1] * acc[1, rows, :] + scale[2] * acc[2, rows, :]
            den = scale[0] * lst[0, rows, :] + scale[1] * lst[1, rows, :] + scale[2] * lst[2, rows, :]
            o_ref[0, rows, :] = (num / den).astype(BF16)
        return carry

    lax.fori_loop(0, nblk // ATT_UNROLL, finish, 0)


def _attention(qkv0, qkv1, qkv2, slopes):
    b, _, s, _ = qkv0.shape
    pairs = HEADS_PER_GROUP // 2
    col_blocks = HEADS_PER_GROUP * HEAD_DIM // LANES

    def specs(arr):
        _, d, sub, _ = arr.shape
        return [pl.BlockSpec((1, d, sub, LANES),
                             functools.partial(lambda bi, p, sec: (bi, 0, 0, sec * col_blocks + p), sec=sec))
                for sec in range(3)]

    return pl.pallas_call(
        _attn_kernel,
        grid=(b, pairs),
        in_specs=[pl.BlockSpec(memory_space=pltpu.SMEM)] + specs(qkv0) + specs(qkv1) + specs(qkv2),
        out_specs=pl.BlockSpec((1, s, LANES), lambda bi, p: (bi, 0, p)),
        out_shape=jax.ShapeDtypeStruct((b, s, ATT_OUT_WIDTH), BF16),
        scratch_shapes=[pltpu.VMEM((3, 2, 2, ATT_BLOCK, 2 * ATT_BLOCK), F32),
                        pltpu.VMEM((3, s, LANES), F32),
                        pltpu.VMEM((3, s, LANES), F32),
                        pltpu.VMEM((3, s, LANES), F32),
                        pltpu.VMEM((2 * ATT_UNROLL, ATT_BLOCK, 2 * ATT_BLOCK), F32),
                        pltpu.VMEM((2 * ATT_UNROLL, ATT_BLOCK, 2 * ATT_BLOCK), F32)],
        compiler_params=pltpu.CompilerParams(
            dimension_semantics=("arbitrary", "arbitrary"), vmem_limit_bytes=VMEM_LIMIT),
        name="attention",
    )(slopes, qkv0, qkv0, qkv0, qkv1, qkv1, qkv1, qkv2, qkv2, qkv2)


def _mixtail_kernel(u_ref, halo_ref, att_ref, gate_ref, x_ref,
                    gt_m_ref, sc_f_ref, sh_f_ref, g_post_ref, g_pre_ref,
                    wpg_ref, pscale_ref, wbp_ref, wba_ref, wout_ref, wr_ref, br_ref,
                    xmid_ref, route_ref, meta_ref, counts_ref, xs_hbm,
                    pu, lv, xbuf, zbuf, fill, meta_s, sem, sem_s):
    i = pl.program_id(1)
    tm = x_ref.shape[1]
    step = pl.program_id(0) * pl.num_programs(1) + i
    last = pl.num_programs(0) * pl.num_programs(1) - 1
    slot = step % 2
    region = _region_rows(pl.num_programs(0) * pl.num_programs(1) * tm)

    @pl.when(step == 0)
    def _():
        fill[...] = jnp.zeros_like(fill)
        pu[0:POOL_PAD - POOL_HALO, :] = jnp.zeros((POOL_PAD - POOL_HALO, POOL_WIDTH), F32)
        lv[:, :, 0:POOL_PAD - POOL_HALO, :] = jnp.zeros(
            (SORT_GROUPS, 2, POOL_PAD - POOL_HALO, POOL_GROUP_DIM), F32)
        for sub in range(SORT_GROUPS):
            for e in range(N_EXPERTS):
                meta_s[SORT_GROUPS + sub, 1, e] = 0
                meta_s[SORT_GROUPS + sub, 0, e] = 0

    def send_tile(which):
        for sub in range(SORT_GROUPS):
            m = which * SORT_GROUPS + sub
            _tile_run_copies(lambda e: meta_s[m, 1, e], xbuf.at[which, sub], None,
                             xs_hbm, lambda e: e * region + meta_s[m, 0, e], sem.at[which])

    def wait_tile(which):
        for sub in range(SORT_GROUPS):
            pltpu.make_async_copy(xbuf.at[which, sub], xs_hbm.at[pl.ds(0, SORT_ROWS * ROW_WORDS), :],
                                  sem.at[which]).wait()

    @pl.when(step > 1)
    def _():
        wait_tile(slot)

    send_tile(1 - slot)

    head = POOL_PAD - POOL_HALO
    halo = halo_ref[0].astype(F32)
    pu[head:POOL_PAD, :] = jnp.where(i > 0, halo, jnp.zeros_like(halo))
    pu[POOL_PAD:POOL_PAD + tm, :] = u_ref[0].astype(F32)

    ts = SORT_TOKENS

    def pool(sub, st):
        base = sub * ts
        t = i * tm + base + lax.broadcasted_iota(I32, (ts, 1), 0)
        pooled_groups = []
        for g, w in enumerate(POOL_WINDOWS):
            cols = slice(g * POOL_GROUP_DIM, (g + 1) * POOL_GROUP_DIM)
            read = lambda start, n: pu[pl.ds(base + start, n), cols]
            shift, level = 1, 0
            while 2 * shift < w:
                partial = read(head, ts + POOL_HALO) + read(head - shift, ts + POOL_HALO)
                buf = lv.at[sub, level % 2]
                buf[pl.ds(head, ts + POOL_HALO), :] = partial
                read = lambda start, n, buf=buf: buf[pl.ds(start, n), :]
                shift, level = 2 * shift, level + 1
            win = read(POOL_PAD, ts) + read(POOL_PAD - shift, ts)
            count = jnp.minimum(t + 1, w).astype(F32)
            pooled_groups.append((win / count - pu[pl.ds(base + POOL_PAD, ts), cols]).astype(BF16))
        st["pooled"] = jnp.concatenate(pooled_groups, axis=1)

    neg_inf = -jnp.inf
    far = float(LANES)
    before = (lax.broadcasted_iota(I32, (ts, ts), 0) < lax.broadcasted_iota(I32, (ts, ts), 1))
    earlier = jnp.where(before, 1.0, 0.0).astype(BF16)
    erow = lax.broadcasted_iota(I32, (N_EXPERTS, ts), 0).astype(F32)
    e_col = lax.broadcasted_iota(I32, (N_EXPERTS, LANES), 0)
    e_lane = lax.broadcasted_iota(I32, (N_EXPERTS, LANES), 1)
    row8 = lax.broadcasted_iota(I32, (SUBLANES, ts), 0)
    mrow = lax.broadcasted_iota(I32, (SUBLANES, LANES), 0)
    srow = lax.broadcasted_iota(I32, (SORT_ROWS, ts), 0).astype(F32)
    filled = fill[...]

    def rows_of(sub):
        return slice(sub * ts, (sub + 1) * ts)

    def branch_projections(sub, st):
        rows = rows_of(sub)
        mixed = _dot(st.pop("pooled"), wpg_ref[...]) * pscale_ref[...]
        st["y_pool"] = _dot(mixed.astype(BF16), wbp_ref[...])
        st["y_att"] = _dot(att_ref[0, rows, :], wba_ref[...])

    def gated_sum(sub, st):
        rows = rows_of(sub)
        st["merged"] = (gate_ref[0, rows, 0:D_MODEL] * st.pop("y_pool").astype(BF16)
                        + gate_ref[0, rows, D_MODEL:2 * D_MODEL] * st.pop("y_att").astype(BF16))

    def output_projection(sub, st):
        st["y"] = _dot(st.pop("merged"), wout_ref[...])

    def residual_and_ffn_input(sub, st):
        rows = rows_of(sub)
        x_mid = x_ref[0, rows, :] + _rmsnorm(st.pop("y"), gt_m_ref[0] * g_post_ref[...])
        xmid_ref[0, rows, :] = x_mid
        h2 = _rmsnorm(x_mid, g_pre_ref[...] * (1.0 + sc_f_ref[0])) + sh_f_ref[0]
        st["h2b"] = h2.astype(BF16)
        st["h2lo"] = (h2 - st["h2b"].astype(F32)).astype(BF16)

    def router_logits(sub, st):
        by_hi = _dot_nt(wr_ref[...], st["h2b"])
        st["logits"] = (by_hi[0:ROUTER_ROWS] + (by_hi[ROUTER_ROWS:2 * ROUTER_ROWS]
                        + _dot_nt(wr_ref[0:ROUTER_ROWS, :], st.pop("h2lo"))) + br_ref[...])

    def route(sub, st):
        logits = st.pop("logits")
        gl = logits[N_EXPERTS:N_EXPERTS + N_EXPERT_GROUPS, :]
        grow = lax.broadcasted_iota(I32, gl.shape, 0).astype(F32)
        gmax = jnp.max(gl, axis=0, keepdims=True)
        gsel = jnp.min(jnp.where(gl == gmax, grow, far), axis=0, keepdims=True)
        p_group = 1.0 / jnp.sum(jnp.exp(gl - gmax), axis=0, keepdims=True)
        e_lo = gsel * float(EXPERTS_PER_GROUP)
        el = jnp.where((erow >= e_lo) & (erow < e_lo + float(EXPERTS_PER_GROUP)),
                       logits[0:N_EXPERTS, :], neg_inf)
        v1 = jnp.max(el, axis=0, keepdims=True)
        i1 = jnp.min(jnp.where(el == v1, erow, far), axis=0, keepdims=True)
        el2 = jnp.where(erow == i1, neg_inf, el)
        v2 = jnp.max(el2, axis=0, keepdims=True)
        i2 = jnp.min(jnp.where(el2 == v2, erow, far), axis=0, keepdims=True)
        e21 = jnp.exp(v2 - v1)
        st["w1"] = p_group / (1.0 + e21)
        st["w2"] = p_group * e21 / (1.0 + e21)
        st["pick1"] = erow == i1
        st["pick2"] = erow == i2
        st["assign"] = jnp.where(st["pick1"] | st["pick2"], 1.0, 0.0)

    def count_and_rank(sub, st):
        assign = st.pop("assign")
        st["rank"] = _dot(assign.astype(BF16), earlier)
        assign_pad = jnp.concatenate([assign, jnp.zeros((LANES - N_EXPERTS, ts), F32)], axis=0).astype(BF16)
        st["cnt_row"] = _dot_nt(jnp.ones((SUBLANES, ts), BF16), assign_pad)

    def sorted_positions(sub, st):
        cnt_row = st["cnt_row"]
        run_start = jnp.sum(jnp.where(e_lane < e_col, cnt_row[0:1, :], 0.0), axis=1, keepdims=True)
        pos = st.pop("rank") + run_start
        key1 = jnp.sum(jnp.where(st.pop("pick1"), pos, 0.0), axis=0, keepdims=True)
        key2 = jnp.sum(jnp.where(st.pop("pick2"), pos, 0.0), axis=0, keepdims=True)
        route_ref[:, rows_of(sub)] = jnp.where(row8 == 0, key1, jnp.where(row8 == 1, key2,
                                               jnp.where(row8 == 2, st.pop("w1"),
                                                         jnp.where(row8 == 3, st.pop("w2"), 0.0))))
        st["perm"] = jnp.where((srow == key1) | (srow == key2), 1.0, 0.0).astype(BF16)

    def sort_rows(sub, st):
        perm, h2b = st.pop("perm"), st.pop("h2b")
        for w in range(ROW_WORDS):
            pair = _dot(perm, h2b[:, 2 * w * LANES:(2 * w + 2) * LANES])
            _pack_rows(xbuf.at[slot, sub], w, SORT_ROWS, pair[:, :LANES], pair[:, LANES:])

    state = _run_staggered((pool, branch_projections, gated_sum, output_projection,
                            residual_and_ffn_input, router_logits, route, count_and_rank,
                            sorted_positions, sort_rows), SORT_GROUPS, lag=0)

    for sub in range(SORT_GROUPS):
        cnt_row = state[sub]["cnt_row"]
        meta_ref[sub] = jnp.where(mrow == 0, filled, jnp.where(mrow == 1, cnt_row, 0.0)).astype(I32)
        filled = filled + cnt_row

    fill[...] = filled
    counts_ref[...] = filled.astype(I32)

    meta_copy = pltpu.make_async_copy(meta_ref, meta_s.at[pl.ds(slot * SORT_GROUPS, SORT_GROUPS)], sem_s)
    meta_copy.start()
    meta_copy.wait()

    @pl.when(step == last)
    def _():
        send_tile(slot)

        @pl.when(step > 0)
        def _():
            wait_tile(1 - slot)

        wait_tile(slot)
        zbuf[...] = jnp.zeros_like(zbuf)
        final = slot * SORT_GROUPS + SORT_GROUPS - 1

        def pad_copy(e):
            end = e * region + meta_s[final, 0, e] + meta_s[final, 1, e]
            return pltpu.make_async_copy(
                zbuf, xs_hbm.at[pl.ds(end * ROW_WORDS, MOE_BLOCK * ROW_WORDS), :], sem_s)

        def start_pad(e, carry):
            pad_copy(e).start()
            return carry

        def wait_pad(e, carry):
            pad_copy(e).wait()
            return carry

        lax.fori_loop(0, N_EXPERTS, start_pad, 0)
        lax.fori_loop(0, N_EXPERTS, wait_pad, 0)


def _mixtail(u, att, gates, x, gt_m, sc_f, sh_f, g_post, g_pre,
             wpg, pscale, wbp, wba, wout, wr, br):
    b, s, d = x.shape
    tm = ROW_TILE
    tiles = s // tm
    n_tiles = b * tiles
    halo_blocks = tm // POOL_HALO
    region = _region_rows(b * s)
    const2 = lambda bi, i: (0, 0)
    const3 = lambda bi, i: (0, 0, 0)
    per_b = lambda bi, i: (bi, 0, 0)
    tile = lambda bi, i: (bi, i, 0)
    single = dict(pipeline_mode=pl.Buffered(1))
    return pl.pallas_call(
        _mixtail_kernel,
        grid=(b, tiles),
        in_specs=[pl.BlockSpec((1, tm, POOL_WIDTH), tile),
                  pl.BlockSpec((1, POOL_HALO, POOL_WIDTH),
                               lambda bi, i: (bi, jnp.maximum(i * halo_blocks - 1, 0), 0)),
                  pl.BlockSpec((1, tm, ATT_OUT_WIDTH), tile),
                  pl.BlockSpec((1, tm, 2 * D_MODEL), tile),
                  pl.BlockSpec((1, tm, d), tile),
                  pl.BlockSpec((1, 1, d), per_b),
                  pl.BlockSpec((1, 1, d), per_b),
                  pl.BlockSpec((1, 1, d), per_b),
                  pl.BlockSpec((1, d), const2),
                  pl.BlockSpec((1, d), const2),
                  pl.BlockSpec(wpg.shape, const2, **single),
                  pl.BlockSpec((1, POOL_WIDTH), const2),
                  pl.BlockSpec(wbp.shape, const2, **single),
                  pl.BlockSpec(wba.shape, const2, **single),
                  pl.BlockSpec(wout.shape, const2, **single),
                  pl.BlockSpec(wr.shape, const2, **single),
                  pl.BlockSpec(br.shape, const2)],
        out_specs=[pl.BlockSpec((1, tm, d), tile),
                   pl.BlockSpec((SUBLANES, tm), lambda bi, i: (0, bi * tiles + i)),
                   pl.BlockSpec((SORT_GROUPS, SUBLANES, LANES), lambda bi, i: (bi * tiles + i, 0, 0)),
                   pl.BlockSpec((SUBLANES, LANES), const2),
                   pl.BlockSpec(memory_space=pl.ANY)],
        out_shape=[jax.ShapeDtypeStruct((b, s, d), F32),
                   jax.ShapeDtypeStruct((SUBLANES, b * s), F32),
                   jax.ShapeDtypeStruct((n_tiles * SORT_GROUPS, SUBLANES, LANES), I32),
                   jax.ShapeDtypeStruct((SUBLANES, LANES), I32),
                   jax.ShapeDtypeStruct((N_EXPERTS * region * ROW_WORDS, LANES), I32)],
        scratch_shapes=[pltpu.VMEM((POOL_PAD + tm, POOL_WIDTH), F32),
                        pltpu.VMEM((SORT_GROUPS, 2, POOL_PAD + SORT_TOKENS, POOL_GROUP_DIM), F32),
                        pltpu.VMEM((2, SORT_GROUPS, SORT_ROWS * ROW_WORDS, LANES), I32),
                        pltpu.VMEM((MOE_BLOCK * ROW_WORDS, LANES), I32),
                        pltpu.VMEM((SUBLANES, LANES), F32),
                        pltpu.SMEM((2 * SORT_GROUPS, SUBLANES, LANES), I32),
                        pltpu.SemaphoreType.DMA((2,)),
                        pltpu.SemaphoreType.DMA(())],
        compiler_params=pltpu.CompilerParams(
            dimension_semantics=("arbitrary", "arbitrary"), vmem_limit_bytes=VMEM_LIMIT),
        name="mixtail",
    )(u, u, att, gates, x, gt_m, sc_f, sh_f, g_post, g_pre,
      wpg, pscale, wbp, wba, wout, wr, br)


def _expert_kernel(counts_ref, xs_hbm, wg_ref, wu_ref, wd_ref, ys_hbm,
                   xbuf, ybuf, state, semx, semy):
    e = pl.program_id(0)
    bm = MOE_BLOCK
    nx, ny = EXPERT_X_BUFFERS, EXPERT_Y_BUFFERS
    block_words = bm * ROW_WORDS
    region = xs_hbm.shape[0] // (N_EXPERTS * ROW_WORDS)

    def n_blocks(ex):
        return (counts_ref[ex] + (bm - 1)) // bm

    def block_rows(ref, ex, k):
        start = pl.multiple_of((ex * region + k * bm) * ROW_WORDS, block_words)
        return ref.at[pl.ds(start, block_words), :]

    def x_copy(ex, k, s):
        return pltpu.make_async_copy(block_rows(xs_hbm, ex, k), xbuf.at[s], semx.at[s])

    def y_copy(k, s):
        return pltpu.make_async_copy(ybuf.at[s], block_rows(ys_hbm, e, k), semy.at[s])

    @pl.when(e == 0)
    def _():
        for j in range(4):
            state[j] = 0

    def fetch_through(target):
        def more(c):
            pe, _, pg = c
            return (pg < target) & (pe < N_EXPERTS)

        def step(c):
            pe, pk, pg = c
            has = pk < n_blocks(pe)

            @pl.when(has)
            def _():
                x_copy(pe, pk, pg % nx).start()

            return (jnp.where(has, pe, pe + 1), jnp.where(has, pk + 1, 0), pg + has.astype(I32))

        pe, pk, pg = lax.while_loop(more, step, (state[1], state[2], state[3]))
        state[1] = pe
        state[2] = pk
        state[3] = pg

    nb = n_blocks(e)
    done = state[0]

    def blocks(k0, count):
        ks = [k0 + c for c in range(count)]
        gs = [done + k for k in ks]
        fetch_through(gs[0] + nx)
        for k, g in zip(ks, gs):
            x_copy(e, k, g % nx).wait()

            @pl.when(g >= ny)
            def _():
                y_copy(k, g % ny).wait()

        mids = []
        for g in gs:
            x = _unpack_rows(xbuf.at[g % nx], bm).astype(BF16)
            a = _dot(x, wg_ref[...].astype(BF16))
            u = _dot(x, wu_ref[...].astype(BF16))
            mids.append(((a * _sigmoid(a)) * u).astype(BF16))
        for k, g, mid in zip(ks, gs, mids):
            for w in range(ROW_WORDS):
                pair = _bf16_exact(_dot(mid, wd_ref[:, 2 * w * LANES:(2 * w + 2) * LANES].astype(BF16)))
                _pack_rows(ybuf.at[g % ny], w, bm, pair[:, :LANES], pair[:, LANES:])
            y_copy(k, g % ny).start()

    per = EXPERT_CHAINS

    def several_blocks(p, carry):
        blocks(per * p, per)
        return carry

    def single_block(k, carry):
        blocks(k, 1)
        return carry

    lax.fori_loop(0, nb // per, several_blocks, 0)
    lax.fori_loop((nb // per) * per, nb, single_block, 0)

    state[0] = done + nb

    @pl.when(e == N_EXPERTS - 1)
    def _():
        total = done + nb
        for j in range(ny):
            @pl.when(total > j)
            def _():
                y_copy(0, (total - 1 - j) % ny).wait()


def _experts(xs, counts, w_gate, w_up, w_down):
    bm = MOE_BLOCK
    w_in = pl.BlockSpec((None, D_MODEL, D_EXPERT), lambda e, cnt: (e, 0, 0))
    w_out = pl.BlockSpec((None, D_EXPERT, D_MODEL), lambda e, cnt: (e, 0, 0))
    grid_spec = pltpu.PrefetchScalarGridSpec(
        num_scalar_prefetch=1,
        grid=(N_EXPERTS,),
        in_specs=[pl.BlockSpec(memory_space=pl.ANY), w_in, w_in, w_out],
        out_specs=pl.BlockSpec(memory_space=pl.ANY),
        scratch_shapes=[pltpu.VMEM((EXPERT_X_BUFFERS, bm * ROW_WORDS, LANES), I32),
                        pltpu.VMEM((EXPERT_Y_BUFFERS, bm * ROW_WORDS, LANES), I32),
                        pltpu.SMEM((4,), I32),
                        pltpu.SemaphoreType.DMA((EXPERT_X_BUFFERS,)),
                        pltpu.SemaphoreType.DMA((EXPERT_Y_BUFFERS,))],
    )
    return pl.pallas_call(
        _expert_kernel,
        grid_spec=grid_spec,
        out_shape=jax.ShapeDtypeStruct(xs.shape, I32),
        compiler_params=pltpu.CompilerParams(
            dimension_semantics=("arbitrary",), vmem_limit_bytes=VMEM_LIMIT),
        name="experts",
    )(counts, xs, w_gate, w_up, w_down)


def _combine_kernel(meta_ref, y_hbm, route_ref, xmid_ref, gt_ref, g_ref, o_ref, ybuf, rt_scr, sem):
    step = pl.program_id(0)
    n_steps = pl.num_programs(0)
    tm = ROW_TILE
    slot = step % COMBINE_BUFFERS
    region = _region_rows(n_steps * tm)

    def fetch_tile(tile, live):
        tile = jnp.minimum(tile, n_steps - 1)
        which = tile % COMBINE_BUFFERS
        for sub in range(SORT_GROUPS):
            m = (tile * SORT_GROUPS + sub) * 2
            _tile_run_copies(lambda e: jnp.where(live, meta_ref[(m + 1) * N_EXPERTS + e], 0),
                             y_hbm, lambda e: e * region + meta_ref[m * N_EXPERTS + e],
                             ybuf.at[which, sub], None, sem.at[which])

    @pl.when(step == 0)
    def _():
        for ahead in range(COMBINE_BUFFERS - 1):
            fetch_tile(ahead, ahead < n_steps)

    for sub in range(SORT_GROUPS):
        pltpu.make_async_copy(y_hbm.at[pl.ds(0, SORT_ROWS * ROW_WORDS), :], ybuf.at[slot, sub],
                              sem.at[slot]).wait()
    sorted_rows = [_unpack_rows(ybuf.at[slot, sub], SORT_ROWS).astype(BF16) for sub in range(SORT_GROUPS)]

    fetch_tile(step + COMBINE_BUFFERS - 1, step + COMBINE_BUFFERS - 1 < n_steps)

    rt_scr[...] = jnp.zeros_like(rt_scr)
    rt_scr[0:SUBLANES, :] = route_ref[...]
    cols = jnp.concatenate([rt_scr[:, c * LANES:(c + 1) * LANES].T for c in range(tm // LANES)], axis=0)

    ts = SORT_TOKENS
    spos = lax.broadcasted_iota(I32, (ts, SORT_ROWS), 1).astype(F32)
    def rows_of(sub):
        return slice(sub * ts, (sub + 1) * ts)

    def gate_matrix(sub, st):
        key1, key2, w1, w2 = (cols[rows_of(sub), c:c + 1] for c in range(4))
        st["gates"] = jnp.where(spos == key1, w1, jnp.where(spos == key2, w2, 0.0)).astype(BF16)

    def weighted_unsort(sub, st):
        st["y"] = _dot(st.pop("gates"), sorted_rows[sub])

    def norm_and_add(sub, st):
        rows = rows_of(sub)
        o_ref[rows, :] = xmid_ref[rows, :] + _rmsnorm(st.pop("y"), gt_ref[0] * g_ref[...])

    _run_staggered((gate_matrix, weighted_unsort, norm_and_add), SORT_GROUPS, lag=1)


def _combine(meta, ys, route, x_mid, gt_f, g_post, seq):
    t, d = x_mid.shape
    tm = ROW_TILE
    tiles_per_seq = seq // tm
    grid_spec = pltpu.PrefetchScalarGridSpec(
        num_scalar_prefetch=1,
        grid=(t // tm,),
        in_specs=[pl.BlockSpec(memory_space=pl.ANY),
                  pl.BlockSpec((SUBLANES, tm), lambda i, m: (0, i)),
                  pl.BlockSpec((tm, d), lambda i, m: (i, 0)),
                  pl.BlockSpec((1, 1, d), lambda i, m: (i // tiles_per_seq, 0, 0)),
                  pl.BlockSpec((1, d), lambda i, m: (0, 0))],
        out_specs=pl.BlockSpec((tm, d), lambda i, m: (i, 0)),
        scratch_shapes=[pltpu.VMEM((COMBINE_BUFFERS, SORT_GROUPS, SORT_ROWS * ROW_WORDS, LANES), I32),
                        pltpu.VMEM((LANES, tm), F32),
                        pltpu.SemaphoreType.DMA((COMBINE_BUFFERS,))],
    )
    return pl.pallas_call(
        _combine_kernel,
        grid_spec=grid_spec,
        out_shape=jax.ShapeDtypeStruct((t, d), F32),
        compiler_params=pltpu.CompilerParams(
            dimension_semantics=("arbitrary",), vmem_limit_bytes=VMEM_LIMIT),
        name="combine",
    )(meta, ys, route, x_mid, gt_f, g_post)


def kernel(x, c, w_ada, b_ada, g_pre_mix, g_post_mix, g_pre_ffn, g_post_ffn, w_in, w_pool_group, pool_scale, w_branch_pool, w_branch_att, w_out, w_group_router, b_group_router, w_expert_router, b_expert_router, w_exp_gate, w_exp_up, w_exp_down):
    b, s, d = x.shape
    assert d == D_MODEL and s % (ATT_BLOCK * 2 * ATT_DILATIONS[2]) == 0 and s % ROW_TILE == 0
    assert (b * s) % MOE_BLOCK == 0
    depth = w_ada.shape[0]
    slopes = jnp.exp2(-ALIBI_MAX_BIAS * jnp.arange(1, N_ATT_HEADS + 1, dtype=F32) / N_ATT_HEADS)
    q_scale = HEAD_DIM ** -0.5 * LOG2_E

    for layer in range(depth):
        mod = _adaln(c, w_ada[layer], b_ada[layer]).reshape(b, 6, 1, d)
        sh_m, sc_m, gt_m, sh_f, sc_f, gt_f = [mod[:, j] for j in range(6)]

        wl = w_in[layer]
        q_lo, k_lo, v_lo = POOL_WIDTH, POOL_WIDTH + 768, POOL_WIDTH + 2 * 768
        group_cols = []
        for g in range(3):
            sl = slice(g * 256, (g + 1) * 256)
            group_cols += [wl[:, q_lo:k_lo][:, sl] * q_scale, wl[:, k_lo:v_lo][:, sl],
                           wl[:, v_lo:v_lo + 768][:, sl]]
        w_perm = jnp.concatenate([wl[:, :POOL_WIDTH]] + group_cols + [wl[:, v_lo + 768:]],
                                 axis=1).astype(BF16)

        u, qkv0, qkv1, qkv2, gates = _inproj(x, g_pre_mix[layer].reshape(1, d), sc_m, sh_m, w_perm)
        att = _attention(qkv0, qkv1, qkv2, slopes)

        zero_block = jnp.zeros((POOL_GROUP_DIM, POOL_GROUP_DIM), F32)
        wpg_diag = jnp.block([[w_pool_group[layer, g] if g == h else zero_block
                               for h in range(len(POOL_WINDOWS))] for g in range(len(POOL_WINDOWS))])

        pad_rows = ROUTER_ROWS - N_EXPERTS - N_EXPERT_GROUPS
        wrt = jnp.concatenate([w_expert_router[layer].T, w_group_router[layer].T,
                               jnp.zeros((pad_rows, d), F32)], axis=0)
        wrt_hi = wrt.astype(BF16)
        wr = jnp.concatenate([wrt_hi, (wrt - wrt_hi.astype(F32)).astype(BF16)], axis=0)
        br = jnp.concatenate([b_expert_router[layer], b_group_router[layer],
                              jnp.zeros((pad_rows,), F32)]).reshape(ROUTER_ROWS, 1)

        x_mid, route, meta, counts, xs = _mixtail(
            u, att, gates, x, gt_m, sc_f, sh_f,
            g_post_mix[layer].reshape(1, d), g_pre_ffn[layer].reshape(1, d),
            wpg_diag.astype(BF16), pool_scale[layer].reshape(1, POOL_WIDTH),
            w_branch_pool[layer].astype(BF16), w_branch_att[layer].astype(BF16),
            w_out[layer].astype(BF16), wr, br)

        ys = _experts(xs, counts[0, :N_EXPERTS], w_exp_gate[layer], w_exp_up[layer], w_exp_down[layer])
        run_meta = meta[:, 0:2, 0:N_EXPERTS].reshape(-1)
        x = _combine(run_meta, ys, route, x_mid.reshape(b * s, d), gt_f,
                     g_post_ffn[layer].reshape(1, d), s).reshape(b, s, d)
    return x
```

```python
import functools

import jax
import jax.numpy as jnp
from jax import lax
from jax.experimental import pallas as pl
from jax.experimental.pallas import tpu as pltpu

F32 = jnp.float32
BF16 = jnp.bfloat16
I32 = jnp.int32

D_MODEL = 1024
LANES = 128
SUBLANES = 8
ROW_WORDS = D_MODEL // (2 * LANES)
HI_HALF = -65536

POOL_WINDOWS = (2, 4, 8, 16)
POOL_GROUP_DIM = 128
POOL_WIDTH = 512
POOL_HALO = 16
POOL_PAD = POOL_HALO + 8

HEAD_DIM = 64
ATT_DILATIONS = (1, 4, 16)
ATT_REACH = 128
ATT_BLOCK = 128
ATT_UNROLL = 4
HEADS_PER_GROUP = 4
N_ATT_HEADS = 12
GROUP_QKV = 3 * HEADS_PER_GROUP * HEAD_DIM
ATT_OUT_WIDTH = 256
ALIBI_MAX_BIAS = 8.0
IN_WIDTH = POOL_WIDTH + 3 * GROUP_QKV + 2 * D_MODEL
MASKED = -1e30
LOG2_E = 1.4426950408889634

N_EXPERT_GROUPS = 4
EXPERTS_PER_GROUP = 8
N_EXPERTS = 32
ROUTER_ROWS = 48
D_EXPERT = 512
RMS_EPS = 1e-6

ROW_TILE = 1024
ROW_SPLIT = 4
SORT_TOKENS = 256
SORT_GROUPS = ROW_TILE // SORT_TOKENS
SORT_ROWS = 2 * SORT_TOKENS
MOE_BLOCK = 256
EXPERT_CHAINS = 4
EXPERT_X_BUFFERS = 10
EXPERT_Y_BUFFERS = 8
COMBINE_BUFFERS = 3
VMEM_LIMIT = 52 * 1024 * 1024


def _sigmoid(x):
    return 0.5 * jnp.tanh(0.5 * x) + 0.5


def _rmsnorm(x, g):
    return x * lax.rsqrt(jnp.mean(x * x, axis=-1, keepdims=True) + RMS_EPS) * g


def _dot(a, b):
    return jnp.dot(a, b, preferred_element_type=F32)


def _dot_nt(a, b):
    return lax.dot_general(a, b, (((1,), (1,)), ((), ())), preferred_element_type=F32)


def _pack_rows(ref, w, n, lo, hi):
    word = (lax.shift_right_logical(pltpu.bitcast(lo, I32), 16) | (pltpu.bitcast(hi, I32) & HI_HALF))
    ref[pl.ds(w, n, stride=ROW_WORDS), :] = word


def _unpack_rows(ref, n):
    cols = []
    for w in range(ROW_WORDS):
        word = ref[pl.ds(w, n, stride=ROW_WORDS), :]
        cols += [pltpu.bitcast(word << 16, F32), pltpu.bitcast(word & HI_HALF, F32)]
    return jnp.concatenate(cols, axis=1)


def _bf16_exact(x):
    return x.astype(BF16).astype(F32)


def _run_staggered(stages, n_chains, lag):
    state = [{} for _ in range(n_chains)]
    for tick in range(len(stages) + lag * (n_chains - 1)):
        for chain in range(n_chains):
            if 0 <= tick - lag * chain < len(stages):
                stages[tick - lag * chain](chain, state[chain])
    return state


def _region_rows(n_tokens):
    return n_tokens + MOE_BLOCK


def _run_copies(n, src, src_row, dst, dst_row, sem):
    @pl.when(n > 0)
    def _():
        pltpu.make_async_copy(
            src.at[pl.ds(src_row * ROW_WORDS, n * ROW_WORDS), :],
            dst.at[pl.ds(dst_row * ROW_WORDS, n * ROW_WORDS), :], sem).start()


def _tile_run_copies(counts, src, src_rows, dst, dst_rows, sem):
    local = 0
    for e in range(N_EXPERTS):
        n = counts(e)
        _run_copies(n, src, local if src_rows is None else src_rows(e),
                    dst, local if dst_rows is None else dst_rows(e), sem)
        local = local + n


def _adaln_kernel(c_ref, w_ref, b_ref, o_ref):
    c = c_ref[...]
    a = c * _sigmoid(c)
    w = w_ref[...]
    a_hi, w_hi = a.astype(BF16), w.astype(BF16)
    a_lo = (a - a_hi.astype(F32)).astype(BF16)
    w_lo = (w - w_hi.astype(F32)).astype(BF16)
    o_ref[...] = _dot(a_hi, w_hi) + (_dot(a_hi, w_lo) + _dot(a_lo, w_hi)) + b_ref[...]


def _adaln(c, w_ada, b_ada):
    b, d = c.shape
    n = w_ada.shape[1]
    rows = -(-b // SUBLANES) * SUBLANES
    cp = jnp.pad(c, ((0, rows - b), (0, 0)))
    nt = 1536
    out = pl.pallas_call(
        _adaln_kernel,
        grid=(n // nt,),
        in_specs=[pl.BlockSpec((rows, d), lambda j: (0, 0)),
                  pl.BlockSpec((d, nt), lambda j: (0, j)),
                  pl.BlockSpec((1, nt), lambda j: (0, j))],
        out_specs=pl.BlockSpec((rows, nt), lambda j: (0, j)),
        out_shape=jax.ShapeDtypeStruct((rows, n), F32),
        compiler_params=pltpu.CompilerParams(vmem_limit_bytes=VMEM_LIMIT),
        name="adaln",
    )(cp, w_ada, b_ada.reshape(1, n))
    return out[:b]


def _inproj_kernel(x_ref, g_ref, sc_ref, sh_ref, w_ref,
                   u_ref, qkv0_ref, qkv1_ref, qkv2_ref, gate_ref, h_scr, p_scr):
    tm = x_ref.shape[1]
    hm = tm // ROW_SPLIT
    for part in range(ROW_SPLIT):
        rows = slice(part * hm, (part + 1) * hm)
        h = _rmsnorm(x_ref[0, rows, :], g_ref[...] * (1.0 + sc_ref[0])) + sh_ref[0]
        h_scr[rows, :] = h.astype(BF16)
        hb = h_scr[rows, :]

        u_ref[0, rows, :] = _dot(hb, w_ref[:, 0:POOL_WIDTH]).astype(BF16)

        col = POOL_WIDTH
        qkv0_ref[0, 0, rows, :] = _dot(hb, w_ref[:, col:col + GROUP_QKV]).astype(BF16)
        for gi, (out_ref, d) in enumerate(((qkv1_ref, ATT_DILATIONS[1]), (qkv2_ref, ATT_DILATIONS[2]))):
            col += GROUP_QKV
            proj = _dot(hb, w_ref[:, col:col + GROUP_QKV])
            stage = p_scr.at[part, gi]
            for cb in range(GROUP_QKV // LANES):
                stage[cb] = proj[:, cb * LANES:(cb + 1) * LANES]
            sub = hm // d
            for r in range(d):
                out_ref[0, r, part * sub:(part + 1) * sub, :] = jnp.concatenate(
                    [stage[cb, pl.ds(r, sub, stride=d), :] for cb in range(GROUP_QKV // LANES)],
                    axis=1).astype(BF16)
        col += GROUP_QKV

        chunk = 512
        for j in range(2 * D_MODEL // chunk):
            g = _dot(hb, w_ref[:, col + j * chunk:col + (j + 1) * chunk])
            gate_ref[0, rows, j * chunk:(j + 1) * chunk] = _sigmoid(g).astype(BF16)


def _inproj(x, g_pre, sc, sh, w_perm):
    b, s, d = x.shape
    tm = ROW_TILE
    d1, d2 = ATT_DILATIONS[1], ATT_DILATIONS[2]
    grid = (b, s // tm)
    const = lambda bi, i: (0, 0)
    per_b = lambda bi, i: (bi, 0, 0)
    return pl.pallas_call(
        _inproj_kernel,
        grid=grid,
        in_specs=[pl.BlockSpec((1, tm, d), lambda bi, i: (bi, i, 0)),
                  pl.BlockSpec((1, d), const),
                  pl.BlockSpec((1, 1, d), per_b),
                  pl.BlockSpec((1, 1, d), per_b),
                  pl.BlockSpec((d, IN_WIDTH), const, pipeline_mode=pl.Buffered(1))],
        out_specs=[pl.BlockSpec((1, tm, POOL_WIDTH), lambda bi, i: (bi, i, 0)),
                   pl.BlockSpec((1, 1, tm, GROUP_QKV), lambda bi, i: (bi, 0, i, 0)),
                   pl.BlockSpec((1, d1, tm // d1, GROUP_QKV), lambda bi, i: (bi, 0, i, 0)),
                   pl.BlockSpec((1, d2, tm // d2, GROUP_QKV), lambda bi, i: (bi, 0, i, 0)),
                   pl.BlockSpec((1, tm, 2 * D_MODEL), lambda bi, i: (bi, i, 0))],
        out_shape=[jax.ShapeDtypeStruct((b, s, POOL_WIDTH), BF16),
                   jax.ShapeDtypeStruct((b, 1, s, GROUP_QKV), BF16),
                   jax.ShapeDtypeStruct((b, d1, s // d1, GROUP_QKV), BF16),
                   jax.ShapeDtypeStruct((b, d2, s // d2, GROUP_QKV), BF16),
                   jax.ShapeDtypeStruct((b, s, 2 * D_MODEL), BF16)],
        scratch_shapes=[pltpu.VMEM((tm, d), BF16),
                        pltpu.VMEM((ROW_SPLIT, 2, GROUP_QKV // LANES, tm // ROW_SPLIT, LANES), F32)],
        compiler_params=pltpu.CompilerParams(
            dimension_semantics=("arbitrary", "arbitrary"), vmem_limit_bytes=VMEM_LIMIT),
        name="inproj",
    )(x, g_pre, sc, sh, w_perm)


def _attn_kernel(slopes_ref, q0, k0, v0, q1, k1, v1, q2, k2, v2, o_ref,
                 bias_scr, acc, mst, lst, s_even, s_odd):
    pair = pl.program_id(1)
    seq = o_ref.shape[1]
    nblk = seq // ATT_BLOCK
    lane = lax.broadcasted_iota(I32, (ATT_BLOCK, LANES), 1)
    first_head = lane < HEAD_DIM
    half_lane = lax.broadcasted_iota(I32, (ATT_BLOCK // 2, LANES), 1)
    head_bits = (jnp.where(half_lane < HEAD_DIM, -1, 0), jnp.where(half_lane < HEAD_DIM, 0, -1))
    ones_block = jnp.ones((2 * ATT_BLOCK, LANES), BF16)

    qi = lax.broadcasted_iota(I32, (ATT_BLOCK, 2 * ATT_BLOCK), 0)
    kj = lax.broadcasted_iota(I32, (ATT_BLOCK, 2 * ATT_BLOCK), 1)
    delta = ATT_BLOCK + qi - kj
    valid = (delta >= 0) & (delta <= ATT_REACH)
    delta0 = qi - kj
    valid0 = delta0 >= 0
    for g, d in enumerate(ATT_DILATIONS):
        for j in range(2):
            slope = slopes_ref[g * HEADS_PER_GROUP + 2 * pair + j]
            slope = slope * LOG2_E
            bias_scr[g, j, 0] = jnp.where(valid0, -slope * (delta0 * d).astype(F32), MASKED)
            bias_scr[g, j, 1] = jnp.where(valid, -slope * (delta * d).astype(F32), MASKED)

    n_iter = nblk // ATT_UNROLL

    def group_stages(g, q_ref, k_ref, v_ref):
        d = ATT_DILATIONS[g]
        per_res = nblk // d

        def block_index(it, k):
            n = it * ATT_UNROLL + k
            return n // per_res, n % per_res

        def key_rows(i):
            lo = jnp.maximum(i - 1, 0)
            return pl.ds(pl.multiple_of(lo * ATT_BLOCK, ATT_BLOCK), 2 * ATT_BLOCK)

        def scores(it, s_ref):
            for k in range(ATT_UNROLL):
                r, i = block_index(it, k)
                q = q_ref[0, r, pl.ds(pl.multiple_of(i * ATT_BLOCK, ATT_BLOCK), ATT_BLOCK), :]
                kw = k_ref[0, r, key_rows(i), :]
                qbits = pltpu.bitcast(q, I32)
                for j in range(2):
                    qh = pltpu.bitcast(qbits & head_bits[j], BF16)
                    s_ref[2 * k + j] = _dot_nt(qh, kw) + bias_scr[g, j, jnp.minimum(i, 1)]

        def weighted_values(it, s_ref):
            for k in range(ATT_UNROLL):
                r, i = block_index(it, k)
                vw = jnp.concatenate([v_ref[0, r, key_rows(i), :], ones_block], axis=1)
                outs = []
                for j in range(2):
                    sc = s_ref[2 * k + j]
                    mx = jnp.max(sc, axis=-1, keepdims=True)
                    p = jnp.exp2(sc - mx)
                    num_den = _dot(p.astype(BF16), vw)
                    outs.append((num_den[:, :LANES], mx, num_den[:, LANES:]))
                (n0, m0, l0), (n1, m1, l1) = outs
                if d == 1:
                    rows = pl.ds(pl.multiple_of(i * ATT_BLOCK, ATT_BLOCK), ATT_BLOCK)
                else:
                    rows = pl.ds(i * (ATT_BLOCK * d) + r, ATT_BLOCK, stride=d)
                acc[g, rows, :] = jnp.where(first_head, n0, n1)
                mst[g, rows, :] = jnp.where(first_head, m0, m1)
                lst[g, rows, :] = jnp.where(first_head, l0, l1)

        return scores, weighted_values

    stages = [group_stages(0, q0, k0, v0), group_stages(1, q1, k1, v1), group_stages(2, q2, k2, v2)]
    stages[0][0](jnp.int32(0), s_even)
    for g, (scores, weighted_values) in enumerate(stages):
        def pair_of_iterations(it, then_scores):
            scores(it + 1, s_odd)
            weighted_values(it, s_even)
            then_scores()
            weighted_values(it + 1, s_odd)

        def body(h, carry, pair_of_iterations=pair_of_iterations, scores=scores):
            it = 2 * h
            pair_of_iterations(it, lambda: scores(it + 2, s_even))
            return carry

        lax.fori_loop(0, n_iter // 2 - 1, body, 0)
        if g + 1 < len(stages):
            next_scores = stages[g + 1][0]
            pair_of_iterations(jnp.int32(n_iter - 2), lambda: next_scores(jnp.int32(0), s_even))
        else:
            pair_of_iterations(jnp.int32(n_iter - 2), lambda: None)

    def finish(it, carry):
        for k in range(ATT_UNROLL):
            rows = pl.ds(pl.multiple_of((it * ATT_UNROLL + k) * ATT_BLOCK, ATT_BLOCK), ATT_BLOCK)
            ms = [mst[g, rows, :] for g in range(3)]
            top = jnp.maximum(jnp.maximum(ms[0], ms[1]), ms[2])
            scale = [jnp.exp2(m - top) for m in ms]
            num = scale[0] * acc[0, rows, :] + scale[1] * acc[1, rows, :] + scale[2] * acc[2, rows, :]
            den = scale[0] * lst[0, rows, :] + scale[1] * lst[1, rows, :] + scale[2] * lst[2, rows, :]
            o_ref[0, rows, :] = (num / den).astype(BF16)
        return carry

    lax.fori_loop(0, nblk // ATT_UNROLL, finish, 0)


def _attention(qkv0, qkv1, qkv2, slopes):
    b, _, s, _ = qkv0.shape
    pairs = HEADS_PER_GROUP // 2
    col_blocks = HEADS_PER_GROUP * HEAD_DIM // LANES

    def specs(arr):
        _, d, sub, _ = arr.shape
        return [pl.BlockSpec((1, d, sub, LANES),
                             functools.partial(lambda bi, p, sec: (bi, 0, 0, sec * col_blocks + p), sec=sec))
                for sec in range(3)]

    return pl.pallas_call(
        _attn_kernel,
        grid=(b, pairs),
        in_specs=[pl.BlockSpec(memory_space=pltpu.SMEM)] + specs(qkv0) + specs(qkv1) + specs(qkv2),
        out_specs=pl.BlockSpec((1, s, LANES), lambda bi, p: (bi, 0, p)),
        out_shape=jax.ShapeDtypeStruct((b, s, ATT_OUT_WIDTH), BF16),
        scratch_shapes=[pltpu.VMEM((3, 2, 2, ATT_BLOCK, 2 * ATT_BLOCK), F32),
                        pltpu.VMEM((3, s, LANES), F32),
                        pltpu.VMEM((3, s, LANES), F32),
                        pltpu.VMEM((3, s, LANES), F32),
                        pltpu.VMEM((2 * ATT_UNROLL, ATT_BLOCK, 2 * ATT_BLOCK), F32),
                        pltpu.VMEM((2 * ATT_UNROLL, ATT_BLOCK, 2 * ATT_BLOCK), F32)],
        compiler_params=pltpu.CompilerParams(
            dimension_semantics=("arbitrary", "arbitrary"), vmem_limit_bytes=VMEM_LIMIT),
        name="attention",
    )(slopes, qkv0, qkv0, qkv0, qkv1, qkv1, qkv1, qkv2, qkv2, qkv2)


def _mixtail_kernel(u_ref, halo_ref, att_ref, gate_ref, x_ref,
                    gt_m_ref, sc_f_ref, sh_f_ref, g_post_ref, g_pre_ref,
                    wpg_ref, pscale_ref, wbp_ref, wba_ref, wout_ref, wr_ref, br_ref,
                    xmid_ref, route_ref, meta_ref, counts_ref, xs_hbm,
                    pu, lv, xbuf, zbuf, fill, meta_s, sem, sem_s):
    i = pl.program_id(1)
    tm = x_ref.shape[1]
    step = pl.program_id(0) * pl.num_programs(1) + i
    last = pl.num_programs(0) * pl.num_programs(1) - 1
    slot = step % 2
    region = _region_rows(pl.num_programs(0) * pl.num_programs(1) * tm)

    @pl.when(step == 0)
    def _():
        fill[...] = jnp.zeros_like(fill)
        pu[0:POOL_PAD - POOL_HALO, :] = jnp.zeros((POOL_PAD - POOL_HALO, POOL_WIDTH), F32)
        lv[:, :, 0:POOL_PAD - POOL_HALO, :] = jnp.zeros(
            (SORT_GROUPS, 2, POOL_PAD - POOL_HALO, POOL_GROUP_DIM), F32)
        for sub in range(SORT_GROUPS):
            for e in range(N_EXPERTS):
                meta_s[SORT_GROUPS + sub, 1, e] = 0
                meta_s[SORT_GROUPS + sub, 0, e] = 0

    def send_tile(which):
        for sub in range(SORT_GROUPS):
            m = which * SORT_GROUPS + sub
            _tile_run_copies(lambda e: meta_s[m, 1, e], xbuf.at[which, sub], None,
                             xs_hbm, lambda e: e * region + meta_s[m, 0, e], sem.at[which])

    def wait_tile(which):
        for sub in range(SORT_GROUPS):
            pltpu.make_async_copy(xbuf.at[which, sub], xs_hbm.at[pl.ds(0, SORT_ROWS * ROW_WORDS), :],
                                  sem.at[which]).wait()

    @pl.when(step > 1)
    def _():
        wait_tile(slot)

    send_tile(1 - slot)

    head = POOL_PAD - POOL_HALO
    halo = halo_ref[0].astype(F32)
    pu[head:POOL_PAD, :] = jnp.where(i > 0, halo, jnp.zeros_like(halo))
    pu[POOL_PAD:POOL_PAD + tm, :] = u_ref[0].astype(F32)

    ts = SORT_TOKENS

    def pool(sub, st):
        base = sub * ts
        t = i * tm + base + lax.broadcasted_iota(I32, (ts, 1), 0)
        pooled_groups = []
        for g, w in enumerate(POOL_WINDOWS):
            cols = slice(g * POOL_GROUP_DIM, (g + 1) * POOL_GROUP_DIM)
            read = lambda start, n: pu[pl.ds(base + start, n), cols]
            shift, level = 1, 0
            while 2 * shift < w:
                partial = read(head, ts + POOL_HALO) + read(head - shift, ts + POOL_HALO)
                buf = lv.at[sub, level % 2]
                buf[pl.ds(head, ts + POOL_HALO), :] = partial
                read = lambda start, n, buf=buf: buf[pl.ds(start, n), :]
                shift, level = 2 * shift, level + 1
            win = read(POOL_PAD, ts) + read(POOL_PAD - shift, ts)
            count = jnp.minimum(t + 1, w).astype(F32)
            pooled_groups.append((win / count - pu[pl.ds(base + POOL_PAD, ts), cols]).astype(BF16))
        st["pooled"] = jnp.concatenate(pooled_groups, axis=1)

    neg_inf = -jnp.inf
    far = float(LANES)
    before = (lax.broadcasted_iota(I32, (ts, ts), 0) < lax.broadcasted_iota(I32, (ts, ts), 1))
    earlier = jnp.where(before, 1.0, 0.0).astype(BF16)
    erow = lax.broadcasted_iota(I32, (N_EXPERTS, ts), 0).astype(F32)
    e_col = lax.broadcasted_iota(I32, (N_EXPERTS, LANES), 0)
    e_lane = lax.broadcasted_iota(I32, (N_EXPERTS, LANES), 1)
    row8 = lax.broadcasted_iota(I32, (SUBLANES, ts), 0)
    mrow = lax.broadcasted_iota(I32, (SUBLANES, LANES), 0)
    srow = lax.broadcasted_iota(I32, (SORT_ROWS, ts), 0).astype(F32)
    filled = fill[...]

    def rows_of(sub):
        return slice(sub * ts, (sub + 1) * ts)

    def branch_projections(sub, st):
        rows = rows_of(sub)
        mixed = _dot(st.pop("pooled"), wpg_ref[...]) * pscale_ref[...]
        st["y_pool"] = _dot(mixed.astype(BF16), wbp_ref[...])
        st["y_att"] = _dot(att_ref[0, rows, :], wba_ref[...])

    def gated_sum(sub, st):
        rows = rows_of(sub)
        st["merged"] = (gate_ref[0, rows, 0:D_MODEL] * st.pop("y_pool").astype(BF16)
                        + gate_ref[0, rows, D_MODEL:2 * D_MODEL] * st.pop("y_att").astype(BF16))

    def output_projection(sub, st):
        st["y"] = _dot(st.pop("merged"), wout_ref[...])

    def residual_and_ffn_input(sub, st):
        rows = rows_of(sub)
        x_mid = x_ref[0, rows, :] + _rmsnorm(st.pop("y"), gt_m_ref[0] * g_post_ref[...])
        xmid_ref[0, rows, :] = x_mid
        h2 = _rmsnorm(x_mid, g_pre_ref[...] * (1.0 + sc_f_ref[0])) + sh_f_ref[0]
        st["h2b"] = h2.astype(BF16)
        st["h2lo"] = (h2 - st["h2b"].astype(F32)).astype(BF16)

    def router_logits(sub, st):
        by_hi = _dot_nt(wr_ref[...], st["h2b"])
        st["logits"] = (by_hi[0:ROUTER_ROWS] + (by_hi[ROUTER_ROWS:2 * ROUTER_ROWS]
                        + _dot_nt(wr_ref[0:ROUTER_ROWS, :], st.pop("h2lo"))) + br_ref[...])

    def route(sub, st):
        logits = st.pop("logits")
        gl = logits[N_EXPERTS:N_EXPERTS + N_EXPERT_GROUPS, :]
        grow = lax.broadcasted_iota(I32, gl.shape, 0).astype(F32)
        gmax = jnp.max(gl, axis=0, keepdims=True)
        gsel = jnp.min(jnp.where(gl == gmax, grow, far), axis=0, keepdims=True)
        p_group = 1.0 / jnp.sum(jnp.exp(gl - gmax), axis=0, keepdims=True)
        e_lo = gsel * float(EXPERTS_PER_GROUP)
        el = jnp.where((erow >= e_lo) & (erow < e_lo + float(EXPERTS_PER_GROUP)),
                       logits[0:N_EXPERTS, :], neg_inf)
        v1 = jnp.max(el, axis=0, keepdims=True)
        i1 = jnp.min(jnp.where(el == v1, erow, far), axis=0, keepdims=True)
        el2 = jnp.where(erow == i1, neg_inf, el)
        v2 = jnp.max(el2, axis=0, keepdims=True)
        i2 = jnp.min(jnp.where(el2 == v2, erow, far), axis=0, keepdims=True)
        e21 = jnp.exp(v2 - v1)
        st["w1"] = p_group / (1.0 + e21)
        st["w2"] = p_group * e21 / (1.0 + e21)
        st["pick1"] = erow == i1
        st["pick2"] = erow == i2
        st["assign"] = jnp.where(st["pick1"] | st["pick2"], 1.0, 0.0)

    def count_and_rank(sub, st):
        assign = st.pop("assign")
        st["rank"] = _dot(assign.astype(BF16), earlier)
        assign_pad = jnp.concatenate([assign, jnp.zeros((LANES - N_EXPERTS, ts), F32)], axis=0).astype(BF16)
        st["cnt_row"] = _dot_nt(jnp.ones((SUBLANES, ts), BF16), assign_pad)

    def sorted_positions(sub, st):
        cnt_row = st["cnt_row"]
        run_start = jnp.sum(jnp.where(e_lane < e_col, cnt_row[0:1, :], 0.0), axis=1, keepdims=True)
        pos = st.pop("rank") + run_start
        key1 = jnp.sum(jnp.where(st.pop("pick1"), pos, 0.0), axis=0, keepdims=True)
        key2 = jnp.sum(jnp.where(st.pop("pick2"), pos, 0.0), axis=0, keepdims=True)
        route_ref[:, rows_of(sub)] = jnp.where(row8 == 0, key1, jnp.where(row8 == 1, key2,
                                               jnp.where(row8 == 2, st.pop("w1"),
                                                         jnp.where(row8 == 3, st.pop("w2"), 0.0))))
        st["perm"] = jnp.where((srow == key1) | (srow == key2), 1.0, 0.0).astype(BF16)

    def sort_rows(sub, st):
        perm, h2b = st.pop("perm"), st.pop("h2b")
        for w in range(ROW_WORDS):
            pair = _dot(perm, h2b[:, 2 * w * LANES:(2 * w + 2) * LANES])
            _pack_rows(xbuf.at[slot, sub], w, SORT_ROWS, pair[:, :LANES], pair[:, LANES:])

    state = _run_staggered((pool, branch_projections, gated_sum, output_projection,
                            residual_and_ffn_input, router_logits, route, count_and_rank,
                            sorted_positions, sort_rows), SORT_GROUPS, lag=0)

    for sub in range(SORT_GROUPS):
        cnt_row = state[sub]["cnt_row"]
        meta_ref[sub] = jnp.where(mrow == 0, filled, jnp.where(mrow == 1, cnt_row, 0.0)).astype(I32)
        filled = filled + cnt_row

    fill[...] = filled
    counts_ref[...] = filled.astype(I32)

    meta_copy = pltpu.make_async_copy(meta_ref, meta_s.at[pl.ds(slot * SORT_GROUPS, SORT_GROUPS)], sem_s)
    meta_copy.start()
    meta_copy.wait()

    @pl.when(step == last)
    def _():
        send_tile(slot)

        @pl.when(step > 0)
        def _():
            wait_tile(1 - slot)

        wait_tile(slot)
        zbuf[...] = jnp.zeros_like(zbuf)
        final = slot * SORT_GROUPS + SORT_GROUPS - 1

        def pad_copy(e):
            end = e * region + meta_s[final, 0, e] + meta_s[final, 1, e]
            return pltpu.make_async_copy(
                zbuf, xs_hbm.at[pl.ds(end * ROW_WORDS, MOE_BLOCK * ROW_WORDS), :], sem_s)

        def start_pad(e, carry):
            pad_copy(e).start()
            return carry

        def wait_pad(e, carry):
            pad_copy(e).wait()
            return carry

        lax.fori_loop(0, N_EXPERTS, start_pad, 0)
        lax.fori_loop(0, N_EXPERTS, wait_pad, 0)


def _mixtail(u, att, gates, x, gt_m, sc_f, sh_f, g_post, g_pre,
             wpg, pscale, wbp, wba, wout, wr, br):
    b, s, d = x.shape
    tm = ROW_TILE
    tiles = s // tm
    n_tiles = b * tiles
    halo_blocks = tm // POOL_HALO
    region = _region_rows(b * s)
    const2 = lambda bi, i: (0, 0)
    const3 = lambda bi, i: (0, 0, 0)
    per_b = lambda bi, i: (bi, 0, 0)
    tile = lambda bi, i: (bi, i, 0)
    single = dict(pipeline_mode=pl.Buffered(1))
    return pl.pallas_call(
        _mixtail_kernel,
        grid=(b, tiles),
        in_specs=[pl.BlockSpec((1, tm, POOL_WIDTH), tile),
                  pl.BlockSpec((1, POOL_HALO, POOL_WIDTH),
                               lambda bi, i: (bi, jnp.maximum(i * halo_blocks - 1, 0), 0)),
                  pl.BlockSpec((1, tm, ATT_OUT_WIDTH), tile),
                  pl.BlockSpec((1, tm, 2 * D_MODEL), tile),
                  pl.BlockSpec((1, tm, d), tile),
                  pl.BlockSpec((1, 1, d), per_b),
                  pl.BlockSpec((1, 1, d), per_b),
                  pl.BlockSpec((1, 1, d), per_b),
                  pl.BlockSpec((1, d), const2),
                  pl.BlockSpec((1, d), const2),
                  pl.BlockSpec(wpg.shape, const2, **single),
                  pl.BlockSpec((1, POOL_WIDTH), const2),
                  pl.BlockSpec(wbp.shape, const2, **single),
                  pl.BlockSpec(wba.shape, const2, **single),
                  pl.BlockSpec(wout.shape, const2, **single),
                  pl.BlockSpec(wr.shape, const2, **single),
                  pl.BlockSpec(br.shape, const2)],
        out_specs=[pl.BlockSpec((1, tm, d), tile),
                   pl.BlockSpec((SUBLANES, tm), lambda bi, i: (0, bi * tiles + i)),
                   pl.BlockSpec((SORT_GROUPS, SUBLANES, LANES), lambda bi, i: (bi * tiles + i, 0, 0)),
                   pl.BlockSpec((SUBLANES, LANES), const2),
                   pl.BlockSpec(memory_space=pl.ANY)],
        out_shape=[jax.ShapeDtypeStruct((b, s, d), F32),
                   jax.ShapeDtypeStruct((SUBLANES, b * s), F32),
                   jax.ShapeDtypeStruct((n_tiles * SORT_GROUPS, SUBLANES, LANES), I32),
                   jax.ShapeDtypeStruct((SUBLANES, LANES), I32),
                   jax.ShapeDtypeStruct((N_EXPERTS * region * ROW_WORDS, LANES), I32)],
        scratch_shapes=[pltpu.VMEM((POOL_PAD + tm, POOL_WIDTH), F32),
                        pltpu.VMEM((SORT_GROUPS, 2, POOL_PAD + SORT_TOKENS, POOL_GROUP_DIM), F32),
                        pltpu.VMEM((2, SORT_GROUPS, SORT_ROWS * ROW_WORDS, LANES), I32),
                        pltpu.VMEM((MOE_BLOCK * ROW_WORDS, LANES), I32),
                        pltpu.VMEM((SUBLANES, LANES), F32),
                        pltpu.SMEM((2 * SORT_GROUPS, SUBLANES, LANES), I32),
                        pltpu.SemaphoreType.DMA((2,)),
                        pltpu.SemaphoreType.DMA(())],
        compiler_params=pltpu.CompilerParams(
            dimension_semantics=("arbitrary", "arbitrary"), vmem_limit_bytes=VMEM_LIMIT),
        name="mixtail",
    )(u, u, att, gates, x, gt_m, sc_f, sh_f, g_post, g_pre,
      wpg, pscale, wbp, wba, wout, wr, br)


def _expert_kernel(counts_ref, xs_hbm, wg_ref, wu_ref, wd_ref, ys_hbm,
                   xbuf, ybuf, state, semx, semy):
    e = pl.program_id(0)
    bm = MOE_BLOCK
    nx, ny = EXPERT_X_BUFFERS, EXPERT_Y_BUFFERS
    block_words = bm * ROW_WORDS
    region = xs_hbm.shape[0] // (N_EXPERTS * ROW_WORDS)

    def n_blocks(ex):
        return (counts_ref[ex] + (bm - 1)) // bm

    def block_rows(ref, ex, k):
        start = pl.multiple_of((ex * region + k * bm) * ROW_WORDS, block_words)
        return ref.at[pl.ds(start, block_words), :]

    def x_copy(ex, k, s):
        return pltpu.make_async_copy(block_rows(xs_hbm, ex, k), xbuf.at[s], semx.at[s])

    def y_copy(k, s):
        return pltpu.make_async_copy(ybuf.at[s], block_rows(ys_hbm, e, k), semy.at[s])

    @pl.when(e == 0)
    def _():
        for j in range(4):
            state[j] = 0

    def fetch_through(target):
        def more(c):
            pe, _, pg = c
            return (pg < target) & (pe < N_EXPERTS)

        def step(c):
            pe, pk, pg = c
            has = pk < n_blocks(pe)

            @pl.when(has)
            def _():
                x_copy(pe, pk, pg % nx).start()

            return (jnp.where(has, pe, pe + 1), jnp.where(has, pk + 1, 0), pg + has.astype(I32))

        pe, pk, pg = lax.while_loop(more, step, (state[1], state[2], state[3]))
        state[1] = pe
        state[2] = pk
        state[3] = pg

    nb = n_blocks(e)
    done = state[0]

    def blocks(k0, count):
        ks = [k0 + c for c in range(count)]
        gs = [done + k for k in ks]
        fetch_through(gs[0] + nx)
        for k, g in zip(ks, gs):
            x_copy(e, k, g % nx).wait()

            @pl.when(g >= ny)
            def _():
                y_copy(k, g % ny).wait()

        mids = []
        for g in gs:
            x = _unpack_rows(xbuf.at[g % nx], bm).astype(BF16)
            a = _dot(x, wg_ref[...].astype(BF16))
            u = _dot(x, wu_ref[...].astype(BF16))
            mids.append(((a * _sigmoid(a)) * u).astype(BF16))
        for k, g, mid in zip(ks, gs, mids):
            for w in range(ROW_WORDS):
                pair = _bf16_exact(_dot(mid, wd_ref[:, 2 * w * LANES:(2 * w + 2) * LANES].astype(BF16)))
                _pack_rows(ybuf.at[g % ny], w, bm, pair[:, :LANES], pair[:, LANES:])
            y_copy(k, g % ny).start()

    per = EXPERT_CHAINS

    def several_blocks(p, carry):
        blocks(per * p, per)
        return carry

    lax.fori_loop(0, nb // per, several_blocks, 0)
    k_next = (nb // per) * per
    count = per // 2
    while count >= 1:
        take = ((nb - k_next) >= count)

        @pl.when(take)
        def _(k_next=k_next, count=count):
            blocks(k_next, count)

        k_next = k_next + jnp.where(take, count, 0)
        count //= 2

    state[0] = done + nb

    @pl.when(e == N_EXPERTS - 1)
    def _():
        total = done + nb
        for j in range(ny):
            @pl.when(total > j)
            def _():
                y_copy(0, (total - 1 - j) % ny).wait()


def _experts(xs, counts, w_gate, w_up, w_down):
    bm = MOE_BLOCK
    w_in = pl.BlockSpec((None, D_MODEL, D_EXPERT), lambda e, cnt: (e, 0, 0))
    w_out = pl.BlockSpec((None, D_EXPERT, D_MODEL), lambda e, cnt: (e, 0, 0))
    grid_spec = pltpu.PrefetchScalarGridSpec(
        num_scalar_prefetch=1,
        grid=(N_EXPERTS,),
        in_specs=[pl.BlockSpec(memory_space=pl.ANY), w_in, w_in, w_out],
        out_specs=pl.BlockSpec(memory_space=pl.ANY),
        scratch_shapes=[pltpu.VMEM((EXPERT_X_BUFFERS, bm * ROW_WORDS, LANES), I32),
                        pltpu.VMEM((EXPERT_Y_BUFFERS, bm * ROW_WORDS, LANES), I32),
                        pltpu.SMEM((4,), I32),
                        pltpu.SemaphoreType.DMA((EXPERT_X_BUFFERS,)),
                        pltpu.SemaphoreType.DMA((EXPERT_Y_BUFFERS,))],
    )
    return pl.pallas_call(
        _expert_kernel,
        grid_spec=grid_spec,
        out_shape=jax.ShapeDtypeStruct(xs.shape, I32),
        compiler_params=pltpu.CompilerParams(
            dimension_semantics=("arbitrary",), vmem_limit_bytes=VMEM_LIMIT),
        name="experts",
    )(counts, xs, w_gate, w_up, w_down)


def _combine_kernel(meta_ref, y_hbm, route_ref, xmid_ref, gt_ref, g_ref, o_ref, ybuf, rt_scr, sem):
    step = pl.program_id(0)
    n_steps = pl.num_programs(0)
    tm = ROW_TILE
    slot = step % COMBINE_BUFFERS
    region = _region_rows(n_steps * tm)

    def fetch_tile(tile, live):
        tile = jnp.minimum(tile, n_steps - 1)
        which = tile % COMBINE_BUFFERS
        for sub in range(SORT_GROUPS):
            m = (tile * SORT_GROUPS + sub) * 2
            _tile_run_copies(lambda e: jnp.where(live, meta_ref[(m + 1) * N_EXPERTS + e], 0),
                             y_hbm, lambda e: e * region + meta_ref[m * N_EXPERTS + e],
                             ybuf.at[which, sub], None, sem.at[which])

    @pl.when(step == 0)
    def _():
        for ahead in range(COMBINE_BUFFERS - 1):
            fetch_tile(ahead, ahead < n_steps)

    for sub in range(SORT_GROUPS):
        pltpu.make_async_copy(y_hbm.at[pl.ds(0, SORT_ROWS * ROW_WORDS), :], ybuf.at[slot, sub],
                              sem.at[slot]).wait()
    sorted_rows = [_unpack_rows(ybuf.at[slot, sub], SORT_ROWS).astype(BF16) for sub in range(SORT_GROUPS)]

    fetch_tile(step + COMBINE_BUFFERS - 1, step + COMBINE_BUFFERS - 1 < n_steps)

    rt_scr[...] = jnp.zeros_like(rt_scr)
    rt_scr[0:SUBLANES, :] = route_ref[...]
    cols = jnp.concatenate([rt_scr[:, c * LANES:(c + 1) * LANES].T for c in range(tm // LANES)], axis=0)

    ts = SORT_TOKENS
    spos = lax.broadcasted_iota(I32, (ts, SORT_ROWS), 1).astype(F32)
    def rows_of(sub):
        return slice(sub * ts, (sub + 1) * ts)

    def gate_matrix(sub, st):
        key1, key2, w1, w2 = (cols[rows_of(sub), c:c + 1] for c in range(4))
        st["gates"] = jnp.where(spos == key1, w1, jnp.where(spos == key2, w2, 0.0)).astype(BF16)

    def weighted_unsort(sub, st):
        st["y"] = _dot(st.pop("gates"), sorted_rows[sub])

    def norm_and_add(sub, st):
        rows = rows_of(sub)
        o_ref[rows, :] = xmid_ref[rows, :] + _rmsnorm(st.pop("y"), gt_ref[0] * g_ref[...])

    _run_staggered((gate_matrix, weighted_unsort, norm_and_add), SORT_GROUPS, lag=1)


def _combine(meta, ys, route, x_mid, gt_f, g_post, seq):
    t, d = x_mid.shape
    tm = ROW_TILE
    tiles_per_seq = seq // tm
    grid_spec = pltpu.PrefetchScalarGridSpec(
        num_scalar_prefetch=1,
        grid=(t // tm,),
        in_specs=[pl.BlockSpec(memory_space=pl.ANY),
                  pl.BlockSpec((SUBLANES, tm), lambda i, m: (0, i)),
                  pl.BlockSpec((tm, d), lambda i, m: (i, 0)),
                  pl.BlockSpec((1, 1, d), lambda i, m: (i // tiles_per_seq, 0, 0)),
                  pl.BlockSpec((1, d), lambda i, m: (0, 0))],
        out_specs=pl.BlockSpec((tm, d), lambda i, m: (i, 0)),
        scratch_shapes=[pltpu.VMEM((COMBINE_BUFFERS, SORT_GROUPS, SORT_ROWS * ROW_WORDS, LANES), I32),
                        pltpu.VMEM((LANES, tm), F32),
                        pltpu.SemaphoreType.DMA((COMBINE_BUFFERS,))],
    )
    return pl.pallas_call(
        _combine_kernel,
        grid_spec=grid_spec,
        out_shape=jax.ShapeDtypeStruct((t, d), F32),
        compiler_params=pltpu.CompilerParams(
            dimension_semantics=("arbitrary",), vmem_limit_bytes=VMEM_LIMIT),
        name="combine",
    )(meta, ys, route, x_mid, gt_f, g_post)


def kernel(x, c, w_ada, b_ada, g_pre_mix, g_post_mix, g_pre_ffn, g_post_ffn, w_in, w_pool_group, pool_scale, w_branch_pool, w_branch_att, w_out, w_group_router, b_group_router, w_expert_router, b_expert_router, w_exp_gate, w_exp_up, w_exp_down):
    b, s, d = x.shape
    assert d == D_MODEL and s % (ATT_BLOCK * 2 * ATT_DILATIONS[2]) == 0 and s % ROW_TILE == 0
    assert (b * s) % MOE_BLOCK == 0
    depth = w_ada.shape[0]
    slopes = jnp.exp2(-ALIBI_MAX_BIAS * jnp.arange(1, N_ATT_HEADS + 1, dtype=F32) / N_ATT_HEADS)
    q_scale = HEAD_DIM ** -0.5 * LOG2_E

    for layer in range(depth):
        mod = _adaln(c, w_ada[layer], b_ada[layer]).reshape(b, 6, 1, d)
        sh_m, sc_m, gt_m, sh_f, sc_f, gt_f = [mod[:, j] for j in range(6)]

        wl = w_in[layer]
        q_lo, k_lo, v_lo = POOL_WIDTH, POOL_WIDTH + 768, POOL_WIDTH + 2 * 768
        group_cols = []
        for g in range(3):
            sl = slice(g * 256, (g + 1) * 256)
            group_cols += [wl[:, q_lo:k_lo][:, sl] * q_scale, wl[:, k_lo:v_lo][:, sl],
                           wl[:, v_lo:v_lo + 768][:, sl]]
        w_perm = jnp.concatenate([wl[:, :POOL_WIDTH]] + group_cols + [wl[:, v_lo + 768:]],
                                 axis=1).astype(BF16)

        u, qkv0, qkv1, qkv2, gates = _inproj(x, g_pre_mix[layer].reshape(1, d), sc_m, sh_m, w_perm)
        att = _attention(qkv0, qkv1, qkv2, slopes)

        zero_block = jnp.zeros((POOL_GROUP_DIM, POOL_GROUP_DIM), F32)
        wpg_diag = jnp.block([[w_pool_group[layer, g] if g == h else zero_block
                               for h in range(len(POOL_WINDOWS))] for g in range(len(POOL_WINDOWS))])

        pad_rows = ROUTER_ROWS - N_EXPERTS - N_EXPERT_GROUPS
        wrt = jnp.concatenate([w_expert_router[layer].T, w_group_router[layer].T,
                               jnp.zeros((pad_rows, d), F32)], axis=0)
        wrt_hi = wrt.astype(BF16)
        wr = jnp.concatenate([wrt_hi, (wrt - wrt_hi.astype(F32)).astype(BF16)], axis=0)
        br = jnp.concatenate([b_expert_router[layer], b_group_router[layer],
                              jnp.zeros((pad_rows,), F32)]).reshape(ROUTER_ROWS, 1)

        x_mid, route, meta, counts, xs = _mixtail(
            u, att, gates, x, gt_m, sc_f, sh_f,
            g_post_mix[layer].reshape(1, d), g_pre_ffn[layer].reshape(1, d),
            wpg_diag.astype(BF16), pool_scale[layer].reshape(1, POOL_WIDTH),
            w_branch_pool[layer].astype(BF16), w_branch_att[layer].astype(BF16),
            w_out[layer].astype(BF16), wr, br)

        ys = _experts(xs, counts[0, :N_EXPERTS], w_exp_gate[layer], w_exp_up[layer], w_exp_down[layer])
        run_meta = meta[:, 0:2, 0:N_EXPERTS].reshape(-1)
        x = _combine(run_meta, ys, route, x_mid.reshape(b * s, d), gt_f,
                     g_post_ffn[layer].reshape(1, d), s).reshape(b, s, d)
    return x
```

```python
import functools

import jax
import jax.numpy as jnp
from jax import lax
from jax.experimental import pallas as pl
from jax.experimental.pallas import tpu as pltpu

F32 = jnp.float32
BF16 = jnp.bfloat16
I32 = jnp.int32

D_MODEL = 1024
LANES = 128
SUBLANES = 8
ROW_WORDS = D_MODEL // (2 * LANES)
HI_HALF = -65536

POOL_WINDOWS = (2, 4, 8, 16)
POOL_GROUP_DIM = 128
POOL_WIDTH = 512
POOL_HALO = 16
POOL_PAD = POOL_HALO + 8

HEAD_DIM = 64
ATT_DILATIONS = (1, 4, 16)
ATT_REACH = 128
ATT_BLOCK = 128
ATT_UNROLL = 8
HEADS_PER_GROUP = 4
N_ATT_HEADS = 12
GROUP_QKV = 3 * HEADS_PER_GROUP * HEAD_DIM
ATT_OUT_WIDTH = 256
ALIBI_MAX_BIAS = 8.0
IN_WIDTH = POOL_WIDTH + 3 * GROUP_QKV + 2 * D_MODEL
MASKED = -1e30
LOG2_E = 1.4426950408889634

N_EXPERT_GROUPS = 4
EXPERTS_PER_GROUP = 8
N_EXPERTS = 32
ROUTER_ROWS = 48
D_EXPERT = 512
RMS_EPS = 1e-6

ROW_TILE = 1024
ROW_SPLIT = 4
SORT_TOKENS = 256
SORT_GROUPS = ROW_TILE // SORT_TOKENS
SORT_ROWS = 2 * SORT_TOKENS
MOE_BLOCK = 256
EXPERT_CHAINS = 4
EXPERT_X_BUFFERS = 10
EXPERT_Y_BUFFERS = 8
COMBINE_BUFFERS = 3
VMEM_LIMIT = 52 * 1024 * 1024


def _sigmoid(x):
    return 0.5 * jnp.tanh(0.5 * x) + 0.5


def _rmsnorm(x, g):
    return x * lax.rsqrt(jnp.mean(x * x, axis=-1, keepdims=True) + RMS_EPS) * g


def _dot(a, b):
    return jnp.dot(a, b, preferred_element_type=F32)


def _dot_nt(a, b):
    return lax.dot_general(a, b, (((1,), (1,)), ((), ())), preferred_element_type=F32)


def _pack_rows(ref, w, n, lo, hi):
    word = (lax.shift_right_logical(pltpu.bitcast(lo, I32), 16) | (pltpu.bitcast(hi, I32) & HI_HALF))
    ref[pl.ds(w, n, stride=ROW_WORDS), :] = word


def _unpack_rows(ref, n):
    cols = []
    for w in range(ROW_WORDS):
        word = ref[pl.ds(w, n, stride=ROW_WORDS), :]
        cols += [pltpu.bitcast(word << 16, F32), pltpu.bitcast(word & HI_HALF, F32)]
    return jnp.concatenate(cols, axis=1)


def _bf16_exact(x):
    return x.astype(BF16).astype(F32)


def _run_staggered(stages, n_chains, lag):
    state = [{} for _ in range(n_chains)]
    for tick in range(len(stages) + lag * (n_chains - 1)):
        for chain in range(n_chains):
            if 0 <= tick - lag * chain < len(stages):
                stages[tick - lag * chain](chain, state[chain])
    return state


def _region_rows(n_tokens):
    return n_tokens + MOE_BLOCK


def _run_copies(n, src, src_row, dst, dst_row, sem):
    @pl.when(n > 0)
    def _():
        pltpu.make_async_copy(
            src.at[pl.ds(src_row * ROW_WORDS, n * ROW_WORDS), :],
            dst.at[pl.ds(dst_row * ROW_WORDS, n * ROW_WORDS), :], sem).start()


def _tile_run_copies(counts, src, src_rows, dst, dst_rows, sem):
    local = 0
    for e in range(N_EXPERTS):
        n = counts(e)
        _run_copies(n, src, local if src_rows is None else src_rows(e),
                    dst, local if dst_rows is None else dst_rows(e), sem)
        local = local + n


def _adaln_kernel(c_ref, w_ref, b_ref, o_ref):
    c = c_ref[...]
    a = c * _sigmoid(c)
    w = w_ref[...]
    a_hi, w_hi = a.astype(BF16), w.astype(BF16)
    a_lo = (a - a_hi.astype(F32)).astype(BF16)
    w_lo = (w - w_hi.astype(F32)).astype(BF16)
    o_ref[...] = _dot(a_hi, w_hi) + (_dot(a_hi, w_lo) + _dot(a_lo, w_hi)) + b_ref[...]


def _adaln(c, w_ada, b_ada):
    b, d = c.shape
    n = w_ada.shape[1]
    rows = -(-b // SUBLANES) * SUBLANES
    cp = jnp.pad(c, ((0, rows - b), (0, 0)))
    nt = 1536
    out = pl.pallas_call(
        _adaln_kernel,
        grid=(n // nt,),
        in_specs=[pl.BlockSpec((rows, d), lambda j: (0, 0)),
                  pl.BlockSpec((d, nt), lambda j: (0, j)),
                  pl.BlockSpec((1, nt), lambda j: (0, j))],
        out_specs=pl.BlockSpec((rows, nt), lambda j: (0, j)),
        out_shape=jax.ShapeDtypeStruct((rows, n), F32),
        compiler_params=pltpu.CompilerParams(vmem_limit_bytes=VMEM_LIMIT),
        name="adaln",
    )(cp, w_ada, b_ada.reshape(1, n))
    return out[:b]


def _inproj_kernel(x_ref, g_ref, sc_ref, sh_ref, w_ref,
                   u_ref, qkv0_ref, qkv1_ref, qkv2_ref, gate_ref, h_scr, p_scr):
    tm = x_ref.shape[1]
    hm = tm // ROW_SPLIT
    for part in range(ROW_SPLIT):
        rows = slice(part * hm, (part + 1) * hm)
        h = _rmsnorm(x_ref[0, rows, :], g_ref[...] * (1.0 + sc_ref[0])) + sh_ref[0]
        h_scr[rows, :] = h.astype(BF16)
        hb = h_scr[rows, :]

        u_ref[0, rows, :] = _dot(hb, w_ref[:, 0:POOL_WIDTH]).astype(BF16)

        col = POOL_WIDTH
        qkv0_ref[0, 0, rows, :] = _dot(hb, w_ref[:, col:col + GROUP_QKV]).astype(BF16)
        for gi, (out_ref, d) in enumerate(((qkv1_ref, ATT_DILATIONS[1]), (qkv2_ref, ATT_DILATIONS[2]))):
            col += GROUP_QKV
            proj = _dot(hb, w_ref[:, col:col + GROUP_QKV])
            stage = p_scr.at[part, gi]
            for cb in range(GROUP_QKV // LANES):
                stage[cb] = proj[:, cb * LANES:(cb + 1) * LANES]
            sub = hm // d
            for r in range(d):
                out_ref[0, r, part * sub:(part + 1) * sub, :] = jnp.concatenate(
                    [stage[cb, pl.ds(r, sub, stride=d), :] for cb in range(GROUP_QKV // LANES)],
                    axis=1).astype(BF16)
        col += GROUP_QKV

        chunk = 512
        for j in range(2 * D_MODEL // chunk):
            g = _dot(hb, w_ref[:, col + j * chunk:col + (j + 1) * chunk])
            gate_ref[0, rows, j * chunk:(j + 1) * chunk] = _sigmoid(g).astype(BF16)


def _inproj(x, g_pre, sc, sh, w_perm):
    b, s, d = x.shape
    tm = ROW_TILE
    d1, d2 = ATT_DILATIONS[1], ATT_DILATIONS[2]
    grid = (b, s // tm)
    const = lambda bi, i: (0, 0)
    per_b = lambda bi, i: (bi, 0, 0)
    return pl.pallas_call(
        _inproj_kernel,
        grid=grid,
        in_specs=[pl.BlockSpec((1, tm, d), lambda bi, i: (bi, i, 0)),
                  pl.BlockSpec((1, d), const),
                  pl.BlockSpec((1, 1, d), per_b),
                  pl.BlockSpec((1, 1, d), per_b),
                  pl.BlockSpec((d, IN_WIDTH), const, pipeline_mode=pl.Buffered(1))],
        out_specs=[pl.BlockSpec((1, tm, POOL_WIDTH), lambda bi, i: (bi, i, 0)),
                   pl.BlockSpec((1, 1, tm, GROUP_QKV), lambda bi, i: (bi, 0, i, 0)),
                   pl.BlockSpec((1, d1, tm // d1, GROUP_QKV), lambda bi, i: (bi, 0, i, 0)),
                   pl.BlockSpec((1, d2, tm // d2, GROUP_QKV), lambda bi, i: (bi, 0, i, 0)),
                   pl.BlockSpec((1, tm, 2 * D_MODEL), lambda bi, i: (bi, i, 0))],
        out_shape=[jax.ShapeDtypeStruct((b, s, POOL_WIDTH), BF16),
                   jax.ShapeDtypeStruct((b, 1, s, GROUP_QKV), BF16),
                   jax.ShapeDtypeStruct((b, d1, s // d1, GROUP_QKV), BF16),
                   jax.ShapeDtypeStruct((b, d2, s // d2, GROUP_QKV), BF16),
                   jax.ShapeDtypeStruct((b, s, 2 * D_MODEL), BF16)],
        scratch_shapes=[pltpu.VMEM((tm, d), BF16),
                        pltpu.VMEM((ROW_SPLIT, 2, GROUP_QKV // LANES, tm // ROW_SPLIT, LANES), F32)],
        compiler_params=pltpu.CompilerParams(
            dimension_semantics=("arbitrary", "arbitrary"), vmem_limit_bytes=VMEM_LIMIT),
        name="inproj",
    )(x, g_pre, sc, sh, w_perm)


def _attn_kernel(slopes_ref, q0, k0, v0, q1, k1, v1, q2, k2, v2, o_ref,
                 bias_scr, acc, mst, lst, s_even, s_odd):
    pair = pl.program_id(1)
    seq = o_ref.shape[1]
    nblk = seq // ATT_BLOCK
    lane = lax.broadcasted_iota(I32, (ATT_BLOCK, LANES), 1)
    first_head = lane < HEAD_DIM
    half_lane = lax.broadcasted_iota(I32, (ATT_BLOCK // 2, LANES), 1)
    head_bits = (jnp.where(half_lane < HEAD_DIM, -1, 0), jnp.where(half_lane < HEAD_DIM, 0, -1))
    ones_block = jnp.ones((2 * ATT_BLOCK, LANES), BF16)

    qi = lax.broadcasted_iota(I32, (ATT_BLOCK, 2 * ATT_BLOCK), 0)
    kj = lax.broadcasted_iota(I32, (ATT_BLOCK, 2 * ATT_BLOCK), 1)
    delta = ATT_BLOCK + qi - kj
    valid = (delta >= 0) & (delta <= ATT_REACH)
    delta0 = qi - kj
    valid0 = delta0 >= 0
    for g, d in enumerate(ATT_DILATIONS):
        for j in range(2):
            slope = slopes_ref[g * HEADS_PER_GROUP + 2 * pair + j]
            slope = slope * LOG2_E
            bias_scr[g, j, 0] = jnp.where(valid0, -slope * (delta0 * d).astype(F32), MASKED)
            bias_scr[g, j, 1] = jnp.where(valid, -slope * (delta * d).astype(F32), MASKED)

    n_iter = nblk // ATT_UNROLL

    def group_stages(g, q_ref, k_ref, v_ref):
        d = ATT_DILATIONS[g]
        per_res = nblk // d

        def block_index(it, k):
            n = it * ATT_UNROLL + k
            return n // per_res, n % per_res

        def key_rows(i):
            lo = jnp.maximum(i - 1, 0)
            return pl.ds(pl.multiple_of(lo * ATT_BLOCK, ATT_BLOCK), 2 * ATT_BLOCK)

        def scores(it, s_ref):
            for k in range(ATT_UNROLL):
                r, i = block_index(it, k)
                q = q_ref[0, r, pl.ds(pl.multiple_of(i * ATT_BLOCK, ATT_BLOCK), ATT_BLOCK), :]
                kw = k_ref[0, r, key_rows(i), :]
                qbits = pltpu.bitcast(q, I32)
                for j in range(2):
                    qh = pltpu.bitcast(qbits & head_bits[j], BF16)
                    s_ref[2 * k + j] = _dot_nt(qh, kw) + bias_scr[g, j, jnp.minimum(i, 1)]

        def weighted_values(it, s_ref):
            for k in range(ATT_UNROLL):
                r, i = block_index(it, k)
                vw = jnp.concatenate([v_ref[0, r, key_rows(i), :], ones_block], axis=1)
                outs = []
                for j in range(2):
                    sc = s_ref[2 * k + j]
                    mx = jnp.max(sc, axis=-1, keepdims=True)
                    p = jnp.exp2(sc - mx)
                    num_den = _dot(p.astype(BF16), vw)
                    outs.append((num_den[:, :LANES], mx, num_den[:, LANES:]))
                (n0, m0, l0), (n1, m1, l1) = outs
                if d == 1:
                    rows = pl.ds(pl.multiple_of(i * ATT_BLOCK, ATT_BLOCK), ATT_BLOCK)
                else:
                    rows = pl.ds(i * (ATT_BLOCK * d) + r, ATT_BLOCK, stride=d)
                acc[g, rows, :] = jnp.where(first_head, n0, n1)
                mst[g, rows, :] = jnp.where(first_head, m0, m1)
                lst[g, rows, :] = jnp.where(first_head, l0, l1)

        return scores, weighted_values

    stages = [group_stages(0, q0, k0, v0), group_stages(1, q1, k1, v1), group_stages(2, q2, k2, v2)]
    stages[0][0](jnp.int32(0), s_even)
    for g, (scores, weighted_values) in enumerate(stages):
        def pair_of_iterations(it, then_scores):
            scores(it + 1, s_odd)
            weighted_values(it, s_even)
            then_scores()
            weighted_values(it + 1, s_odd)

        def body(h, carry, pair_of_iterations=pair_of_iterations, scores=scores):
            it = 2 * h
            pair_of_iterations(it, lambda: scores(it + 2, s_even))
            return carry

        lax.fori_loop(0, n_iter // 2 - 1, body, 0)
        if g + 1 < len(stages):
            next_scores = stages[g + 1][0]
            pair_of_iterations(jnp.int32(n_iter - 2), lambda: next_scores(jnp.int32(0), s_even))
        else:
            pair_of_iterations(jnp.int32(n_iter - 2), lambda: None)

    def finish(it, carry):
        for k in range(ATT_UNROLL):
            rows = pl.ds(pl.multiple_of((it * ATT_UNROLL + k) * ATT_BLOCK, ATT_BLOCK), ATT_BLOCK)
            ms = [mst[g, rows, :] for g in range(3)]
            top = jnp.maximum(jnp.maximum(ms[0], ms[1]), ms[2])
            scale = [jnp.exp2(m - top) for m in ms]
            num = scale[0] * acc[0, rows, :] + scale[1] * acc[1, rows, :] + scale[2] * acc[2, rows, :]
            den = scale[0] * lst[0, rows, :] + scale[1] * lst[1, rows, :] + scale[2] * lst[2, rows, :]
            o_ref[0, rows, :] = (num / den).astype(BF16)
        return carry

    lax.fori_loop(0, nblk // ATT_UNROLL, finish, 0)


def _attention(qkv0, qkv1, qkv2, slopes):
    b, _, s, _ = qkv0.shape
    pairs = HEADS_PER_GROUP // 2
    col_blocks = HEADS_PER_GROUP * HEAD_DIM // LANES

    def specs(arr):
        _, d, sub, _ = arr.shape
        return [pl.BlockSpec((1, d, sub, LANES),
                             functools.partial(lambda bi, p, sec: (bi, 0, 0, sec * col_blocks + p), sec=sec))
                for sec in range(3)]

    return pl.pallas_call(
        _attn_kernel,
        grid=(b, pairs),
        in_specs=[pl.BlockSpec(memory_space=pltpu.SMEM)] + specs(qkv0) + specs(qkv1) + specs(qkv2),
        out_specs=pl.BlockSpec((1, s, LANES), lambda bi, p: (bi, 0, p)),
        out_shape=jax.ShapeDtypeStruct((b, s, ATT_OUT_WIDTH), BF16),
        scratch_shapes=[pltpu.VMEM((3, 2, 2, ATT_BLOCK, 2 * ATT_BLOCK), F32),
                        pltpu.VMEM((3, s, LANES), F32),
                        pltpu.VMEM((3, s, LANES), F32),
                        pltpu.VMEM((3, s, LANES), F32),
                        pltpu.VMEM((2 * ATT_UNROLL, ATT_BLOCK, 2 * ATT_BLOCK), F32),
                        pltpu.VMEM((2 * ATT_UNROLL, ATT_BLOCK, 2 * ATT_BLOCK), F32)],
        compiler_params=pltpu.CompilerParams(
            dimension_semantics=("arbitrary", "arbitrary"), vmem_limit_bytes=VMEM_LIMIT),
        name="attention",
    )(slopes, qkv0, qkv0, qkv0, qkv1, qkv1, qkv1, qkv2, qkv2, qkv2)


def _mixtail_kernel(u_ref, halo_ref, att_ref, gate_ref, x_ref,
                    gt_m_ref, sc_f_ref, sh_f_ref, g_post_ref, g_pre_ref,
                    wpg_ref, pscale_ref, wbp_ref, wba_ref, wout_ref, wr_ref, br_ref,
                    xmid_ref, route_ref, meta_ref, counts_ref, xs_hbm,
                    pu, lv, xbuf, zbuf, fill, meta_s, sem, sem_s):
    i = pl.program_id(1)
    tm = x_ref.shape[1]
    step = pl.program_id(0) * pl.num_programs(1) + i
    last = pl.num_programs(0) * pl.num_programs(1) - 1
    slot = step % 2
    region = _region_rows(pl.num_programs(0) * pl.num_programs(1) * tm)

    @pl.when(step == 0)
    def _():
        fill[...] = jnp.zeros_like(fill)
        pu[0:POOL_PAD - POOL_HALO, :] = jnp.zeros((POOL_PAD - POOL_HALO, POOL_WIDTH), F32)
        lv[:, :, 0:POOL_PAD - POOL_HALO, :] = jnp.zeros(
            (SORT_GROUPS, 2, POOL_PAD - POOL_HALO, POOL_GROUP_DIM), F32)
        for sub in range(SORT_GROUPS):
            for e in range(N_EXPERTS):
                meta_s[SORT_GROUPS + sub, 1, e] = 0
                meta_s[SORT_GROUPS + sub, 0, e] = 0

    def send_tile(which):
        for sub in range(SORT_GROUPS):
            m = which * SORT_GROUPS + sub
            _tile_run_copies(lambda e: meta_s[m, 1, e], xbuf.at[which, sub], None,
                             xs_hbm, lambda e: e * region + meta_s[m, 0, e], sem.at[which])

    def wait_tile(which):
        for sub in range(SORT_GROUPS):
            pltpu.make_async_copy(xbuf.at[which, sub], xs_hbm.at[pl.ds(0, SORT_ROWS * ROW_WORDS), :],
                                  sem.at[which]).wait()

    @pl.when(step > 1)
    def _():
        wait_tile(slot)

    send_tile(1 - slot)

    head = POOL_PAD - POOL_HALO
    halo = halo_ref[0].astype(F32)
    pu[head:POOL_PAD, :] = jnp.where(i > 0, halo, jnp.zeros_like(halo))
    pu[POOL_PAD:POOL_PAD + tm, :] = u_ref[0].astype(F32)

    ts = SORT_TOKENS

    def pool(sub, st):
        base = sub * ts
        t = i * tm + base + lax.broadcasted_iota(I32, (ts, 1), 0)
        pooled_groups = []
        for g, w in enumerate(POOL_WINDOWS):
            cols = slice(g * POOL_GROUP_DIM, (g + 1) * POOL_GROUP_DIM)
            read = lambda start, n: pu[pl.ds(base + start, n), cols]
            shift, level = 1, 0
            while 2 * shift < w:
                partial = read(head, ts + POOL_HALO) + read(head - shift, ts + POOL_HALO)
                buf = lv.at[sub, level % 2]
                buf[pl.ds(head, ts + POOL_HALO), :] = partial
                read = lambda start, n, buf=buf: buf[pl.ds(start, n), :]
                shift, level = 2 * shift, level + 1
            win = read(POOL_PAD, ts) + read(POOL_PAD - shift, ts)
            count = jnp.minimum(t + 1, w).astype(F32)
            pooled_groups.append((win / count - pu[pl.ds(base + POOL_PAD, ts), cols]).astype(BF16))
        st["pooled"] = jnp.concatenate(pooled_groups, axis=1)

    neg_inf = -jnp.inf
    far = float(LANES)
    before = (lax.broadcasted_iota(I32, (ts, ts), 0) < lax.broadcasted_iota(I32, (ts, ts), 1))
    earlier = jnp.where(before, 1.0, 0.0).astype(BF16)
    erow = lax.broadcasted_iota(I32, (N_EXPERTS, ts), 0).astype(F32)
    e_col = lax.broadcasted_iota(I32, (N_EXPERTS, LANES), 0)
    e_lane = lax.broadcasted_iota(I32, (N_EXPERTS, LANES), 1)
    row8 = lax.broadcasted_iota(I32, (SUBLANES, ts), 0)
    mrow = lax.broadcasted_iota(I32, (SUBLANES, LANES), 0)
    srow = lax.broadcasted_iota(I32, (SORT_ROWS, ts), 0).astype(F32)
    filled = fill[...]

    def rows_of(sub):
        return slice(sub * ts, (sub + 1) * ts)

    def branch_projections(sub, st):
        rows = rows_of(sub)
        mixed = _dot(st.pop("pooled"), wpg_ref[...]) * pscale_ref[...]
        st["y_pool"] = _dot(mixed.astype(BF16), wbp_ref[...])
        st["y_att"] = _dot(att_ref[0, rows, :], wba_ref[...])

    def gated_sum(sub, st):
        rows = rows_of(sub)
        st["merged"] = (gate_ref[0, rows, 0:D_MODEL] * st.pop("y_pool").astype(BF16)
                        + gate_ref[0, rows, D_MODEL:2 * D_MODEL] * st.pop("y_att").astype(BF16))

    def output_projection(sub, st):
        st["y"] = _dot(st.pop("merged"), wout_ref[...])

    def residual_and_ffn_input(sub, st):
        rows = rows_of(sub)
        x_mid = x_ref[0, rows, :] + _rmsnorm(st.pop("y"), gt_m_ref[0] * g_post_ref[...])
        xmid_ref[0, rows, :] = x_mid
        h2 = _rmsnorm(x_mid, g_pre_ref[...] * (1.0 + sc_f_ref[0])) + sh_f_ref[0]
        st["h2b"] = h2.astype(BF16)
        st["h2lo"] = (h2 - st["h2b"].astype(F32)).astype(BF16)

    def router_logits(sub, st):
        by_hi = _dot_nt(wr_ref[...], st["h2b"])
        st["logits"] = (by_hi[0:ROUTER_ROWS] + (by_hi[ROUTER_ROWS:2 * ROUTER_ROWS]
                        + _dot_nt(wr_ref[0:ROUTER_ROWS, :], st.pop("h2lo"))) + br_ref[...])

    def route(sub, st):
        logits = st.pop("logits")
        gl = logits[N_EXPERTS:N_EXPERTS + N_EXPERT_GROUPS, :]
        grow = lax.broadcasted_iota(I32, gl.shape, 0).astype(F32)
        gmax = jnp.max(gl, axis=0, keepdims=True)
        gsel = jnp.min(jnp.where(gl == gmax, grow, far), axis=0, keepdims=True)
        p_group = 1.0 / jnp.sum(jnp.exp(gl - gmax), axis=0, keepdims=True)
        e_lo = gsel * float(EXPERTS_PER_GROUP)
        el = jnp.where((erow >= e_lo) & (erow < e_lo + float(EXPERTS_PER_GROUP)),
                       logits[0:N_EXPERTS, :], neg_inf)
        v1 = jnp.max(el, axis=0, keepdims=True)
        i1 = jnp.min(jnp.where(el == v1, erow, far), axis=0, keepdims=True)
        el2 = jnp.where(erow == i1, neg_inf, el)
        v2 = jnp.max(el2, axis=0, keepdims=True)
        i2 = jnp.min(jnp.where(el2 == v2, erow, far), axis=0, keepdims=True)
        e21 = jnp.exp(v2 - v1)
        st["w1"] = p_group / (1.0 + e21)
        st["w2"] = p_group * e21 / (1.0 + e21)
        st["pick1"] = erow == i1
        st["pick2"] = erow == i2
        st["assign"] = jnp.where(st["pick1"] | st["pick2"], 1.0, 0.0)

    def count_and_rank(sub, st):
        assign = st.pop("assign")
        st["rank"] = _dot(assign.astype(BF16), earlier)
        assign_pad = jnp.concatenate([assign, jnp.zeros((LANES - N_EXPERTS, ts), F32)], axis=0).astype(BF16)
        st["cnt_row"] = _dot_nt(jnp.ones((SUBLANES, ts), BF16), assign_pad)

    def sorted_positions(sub, st):
        cnt_row = st["cnt_row"]
        run_start = jnp.sum(jnp.where(e_lane < e_col, cnt_row[0:1, :], 0.0), axis=1, keepdims=True)
        pos = st.pop("rank") + run_start
        key1 = jnp.sum(jnp.where(st.pop("pick1"), pos, 0.0), axis=0, keepdims=True)
        key2 = jnp.sum(jnp.where(st.pop("pick2"), pos, 0.0), axis=0, keepdims=True)
        route_ref[:, rows_of(sub)] = jnp.where(row8 == 0, key1, jnp.where(row8 == 1, key2,
                                               jnp.where(row8 == 2, st.pop("w1"),
                                                         jnp.where(row8 == 3, st.pop("w2"), 0.0))))
        st["perm"] = jnp.where((srow == key1) | (srow == key2), 1.0, 0.0).astype(BF16)

    def sort_rows(sub, st):
        perm, h2b = st.pop("perm"), st.pop("h2b")
        for w in range(ROW_WORDS):
            pair = _dot(perm, h2b[:, 2 * w * LANES:(2 * w + 2) * LANES])
            _pack_rows(xbuf.at[slot, sub], w, SORT_ROWS, pair[:, :LANES], pair[:, LANES:])

    state = _run_staggered((pool, branch_projections, gated_sum, output_projection,
                            residual_and_ffn_input, router_logits, route, count_and_rank,
                            sorted_positions, sort_rows), SORT_GROUPS, lag=0)

    for sub in range(SORT_GROUPS):
        cnt_row = state[sub]["cnt_row"]
        meta_ref[sub] = jnp.where(mrow == 0, filled, jnp.where(mrow == 1, cnt_row, 0.0)).astype(I32)
        filled = filled + cnt_row

    fill[...] = filled
    counts_ref[...] = filled.astype(I32)

    meta_copy = pltpu.make_async_copy(meta_ref, meta_s.at[pl.ds(slot * SORT_GROUPS, SORT_GROUPS)], sem_s)
    meta_copy.start()
    meta_copy.wait()

    @pl.when(step == last)
    def _():
        send_tile(slot)

        @pl.when(step > 0)
        def _():
            wait_tile(1 - slot)

        wait_tile(slot)
        zbuf[...] = jnp.zeros_like(zbuf)
        final = slot * SORT_GROUPS + SORT_GROUPS - 1

        def pad_copy(e):
            end = e * region + meta_s[final, 0, e] + meta_s[final, 1, e]
            return pltpu.make_async_copy(
                zbuf, xs_hbm.at[pl.ds(end * ROW_WORDS, MOE_BLOCK * ROW_WORDS), :], sem_s)

        def start_pad(e, carry):
            pad_copy(e).start()
            return carry

        def wait_pad(e, carry):
            pad_copy(e).wait()
            return carry

        lax.fori_loop(0, N_EXPERTS, start_pad, 0)
        lax.fori_loop(0, N_EXPERTS, wait_pad, 0)


def _mixtail(u, att, gates, x, gt_m, sc_f, sh_f, g_post, g_pre,
             wpg, pscale, wbp, wba, wout, wr, br):
    b, s, d = x.shape
    tm = ROW_TILE
    tiles = s // tm
    n_tiles = b * tiles
    halo_blocks = tm // POOL_HALO
    region = _region_rows(b * s)
    const2 = lambda bi, i: (0, 0)
    const3 = lambda bi, i: (0, 0, 0)
    per_b = lambda bi, i: (bi, 0, 0)
    tile = lambda bi, i: (bi, i, 0)
    single = dict(pipeline_mode=pl.Buffered(1))
    return pl.pallas_call(
        _mixtail_kernel,
        grid=(b, tiles),
        in_specs=[pl.BlockSpec((1, tm, POOL_WIDTH), tile),
                  pl.BlockSpec((1, POOL_HALO, POOL_WIDTH),
                               lambda bi, i: (bi, jnp.maximum(i * halo_blocks - 1, 0), 0)),
                  pl.BlockSpec((1, tm, ATT_OUT_WIDTH), tile),
                  pl.BlockSpec((1, tm, 2 * D_MODEL), tile),
                  pl.BlockSpec((1, tm, d), tile),
                  pl.BlockSpec((1, 1, d), per_b),
                  pl.BlockSpec((1, 1, d), per_b),
                  pl.BlockSpec((1, 1, d), per_b),
                  pl.BlockSpec((1, d), const2),
                  pl.BlockSpec((1, d), const2),
                  pl.BlockSpec(wpg.shape, const2, **single),
                  pl.BlockSpec((1, POOL_WIDTH), const2),
                  pl.BlockSpec(wbp.shape, const2, **single),
                  pl.BlockSpec(wba.shape, const2, **single),
                  pl.BlockSpec(wout.shape, const2, **single),
                  pl.BlockSpec(wr.shape, const2, **single),
                  pl.BlockSpec(br.shape, const2)],
        out_specs=[pl.BlockSpec((1, tm, d), tile),
                   pl.BlockSpec((SUBLANES, tm), lambda bi, i: (0, bi * tiles + i)),
                   pl.BlockSpec((SORT_GROUPS, SUBLANES, LANES), lambda bi, i: (bi * tiles + i, 0, 0)),
                   pl.BlockSpec((SUBLANES, LANES), const2),
                   pl.BlockSpec(memory_space=pl.ANY)],
        out_shape=[jax.ShapeDtypeStruct((b, s, d), F32),
                   jax.ShapeDtypeStruct((SUBLANES, b * s), F32),
                   jax.ShapeDtypeStruct((n_tiles * SORT_GROUPS, SUBLANES, LANES), I32),
                   jax.ShapeDtypeStruct((SUBLANES, LANES), I32),
                   jax.ShapeDtypeStruct((N_EXPERTS * region * ROW_WORDS, LANES), I32)],
        scratch_shapes=[pltpu.VMEM((POOL_PAD + tm, POOL_WIDTH), F32),
                        pltpu.VMEM((SORT_GROUPS, 2, POOL_PAD + SORT_TOKENS, POOL_GROUP_DIM), F32),
                        pltpu.VMEM((2, SORT_GROUPS, SORT_ROWS * ROW_WORDS, LANES), I32),
                        pltpu.VMEM((MOE_BLOCK * ROW_WORDS, LANES), I32),
                        pltpu.VMEM((SUBLANES, LANES), F32),
                        pltpu.SMEM((2 * SORT_GROUPS, SUBLANES, LANES), I32),
                        pltpu.SemaphoreType.DMA((2,)),
                        pltpu.SemaphoreType.DMA(())],
        compiler_params=pltpu.CompilerParams(
            dimension_semantics=("arbitrary", "arbitrary"), vmem_limit_bytes=VMEM_LIMIT),
        name="mixtail",
    )(u, u, att, gates, x, gt_m, sc_f, sh_f, g_post, g_pre,
      wpg, pscale, wbp, wba, wout, wr, br)


def _expert_kernel(counts_ref, xs_hbm, wg_ref, wu_ref, wd_ref, ys_hbm,
                   xbuf, ybuf, state, semx, semy):
    e = pl.program_id(0)
    bm = MOE_BLOCK
    nx, ny = EXPERT_X_BUFFERS, EXPERT_Y_BUFFERS
    block_words = bm * ROW_WORDS
    region = xs_hbm.shape[0] // (N_EXPERTS * ROW_WORDS)

    def n_blocks(ex):
        return (counts_ref[ex] + (bm - 1)) // bm

    def block_rows(ref, ex, k):
        start = pl.multiple_of((ex * region + k * bm) * ROW_WORDS, block_words)
        return ref.at[pl.ds(start, block_words), :]

    def x_copy(ex, k, s):
        return pltpu.make_async_copy(block_rows(xs_hbm, ex, k), xbuf.at[s], semx.at[s])

    def y_copy(k, s):
        return pltpu.make_async_copy(ybuf.at[s], block_rows(ys_hbm, e, k), semy.at[s])

    @pl.when(e == 0)
    def _():
        for j in range(4):
            state[j] = 0

    def fetch_through(target):
        def more(c):
            pe, _, pg = c
            return (pg < target) & (pe < N_EXPERTS)

        def step(c):
            pe, pk, pg = c
            has = pk < n_blocks(pe)

            @pl.when(has)
            def _():
                x_copy(pe, pk, pg % nx).start()

            return (jnp.where(has, pe, pe + 1), jnp.where(has, pk + 1, 0), pg + has.astype(I32))

        pe, pk, pg = lax.while_loop(more, step, (state[1], state[2], state[3]))
        state[1] = pe
        state[2] = pk
        state[3] = pg

    nb = n_blocks(e)
    done = state[0]

    def blocks(k0, count):
        ks = [k0 + c for c in range(count)]
        gs = [done + k for k in ks]
        fetch_through(gs[0] + nx)
        for k, g in zip(ks, gs):
            x_copy(e, k, g % nx).wait()

            @pl.when(g >= ny)
            def _():
                y_copy(k, g % ny).wait()

        mids = []
        for g in gs:
            x = _unpack_rows(xbuf.at[g % nx], bm).astype(BF16)
            a = _dot(x, wg_ref[...].astype(BF16))
            u = _dot(x, wu_ref[...].astype(BF16))
            mids.append(((a * _sigmoid(a)) * u).astype(BF16))
        for k, g, mid in zip(ks, gs, mids):
            for w in range(ROW_WORDS):
                pair = _bf16_exact(_dot(mid, wd_ref[:, 2 * w * LANES:(2 * w + 2) * LANES].astype(BF16)))
                _pack_rows(ybuf.at[g % ny], w, bm, pair[:, :LANES], pair[:, LANES:])
            y_copy(k, g % ny).start()

    per = EXPERT_CHAINS

    def several_blocks(p, carry):
        blocks(per * p, per)
        return carry

    lax.fori_loop(0, nb // per, several_blocks, 0)
    k_next = (nb // per) * per
    count = per // 2
    while count >= 1:
        take = ((nb - k_next) >= count)

        @pl.when(take)
        def _(k_next=k_next, count=count):
            blocks(k_next, count)

        k_next = k_next + jnp.where(take, count, 0)
        count //= 2

    state[0] = done + nb

    @pl.when(e == N_EXPERTS - 1)
    def _():
        total = done + nb
        for j in range(ny):
            @pl.when(total > j)
            def _():
                y_copy(0, (total - 1 - j) % ny).wait()


def _experts(xs, counts, w_gate, w_up, w_down):
    bm = MOE_BLOCK
    w_in = pl.BlockSpec((None, D_MODEL, D_EXPERT), lambda e, cnt: (e, 0, 0))
    w_out = pl.BlockSpec((None, D_EXPERT, D_MODEL), lambda e, cnt: (e, 0, 0))
    grid_spec = pltpu.PrefetchScalarGridSpec(
        num_scalar_prefetch=1,
        grid=(N_EXPERTS,),
        in_specs=[pl.BlockSpec(memory_space=pl.ANY), w_in, w_in, w_out],
        out_specs=pl.BlockSpec(memory_space=pl.ANY),
        scratch_shapes=[pltpu.VMEM((EXPERT_X_BUFFERS, bm * ROW_WORDS, LANES), I32),
                        pltpu.VMEM((EXPERT_Y_BUFFERS, bm * ROW_WORDS, LANES), I32),
                        pltpu.SMEM((4,), I32),
                        pltpu.SemaphoreType.DMA((EXPERT_X_BUFFERS,)),
                        pltpu.SemaphoreType.DMA((EXPERT_Y_BUFFERS,))],
    )
    return pl.pallas_call(
        _expert_kernel,
        grid_spec=grid_spec,
        out_shape=jax.ShapeDtypeStruct(xs.shape, I32),
        compiler_params=pltpu.CompilerParams(
            dimension_semantics=("arbitrary",), vmem_limit_bytes=VMEM_LIMIT),
        name="experts",
    )(counts, xs, w_gate, w_up, w_down)


def _combine_kernel(meta_ref, y_hbm, route_ref, xmid_ref, gt_ref, g_ref, o_ref, ybuf, rt_scr, sem):
    step = pl.program_id(0)
    n_steps = pl.num_programs(0)
    tm = ROW_TILE
    slot = step % COMBINE_BUFFERS
    region = _region_rows(n_steps * tm)

    def fetch_tile(tile, live):
        tile = jnp.minimum(tile, n_steps - 1)
        which = tile % COMBINE_BUFFERS
        for sub in range(SORT_GROUPS):
            m = (tile * SORT_GROUPS + sub) * 2
            _tile_run_copies(lambda e: jnp.where(live, meta_ref[(m + 1) * N_EXPERTS + e], 0),
                             y_hbm, lambda e: e * region + meta_ref[m * N_EXPERTS + e],
                             ybuf.at[which, sub], None, sem.at[which])

    @pl.when(step == 0)
    def _():
        for ahead in range(COMBINE_BUFFERS - 1):
            fetch_tile(ahead, ahead < n_steps)

    for sub in range(SORT_GROUPS):
        pltpu.make_async_copy(y_hbm.at[pl.ds(0, SORT_ROWS * ROW_WORDS), :], ybuf.at[slot, sub],
                              sem.at[slot]).wait()
    sorted_rows = [_unpack_rows(ybuf.at[slot, sub], SORT_ROWS).astype(BF16) for sub in range(SORT_GROUPS)]

    fetch_tile(step + COMBINE_BUFFERS - 1, step + COMBINE_BUFFERS - 1 < n_steps)

    rt_scr[...] = jnp.zeros_like(rt_scr)
    rt_scr[0:SUBLANES, :] = route_ref[...]
    cols = jnp.concatenate([rt_scr[:, c * LANES:(c + 1) * LANES].T for c in range(tm // LANES)], axis=0)

    ts = SORT_TOKENS
    spos = lax.broadcasted_iota(I32, (ts, SORT_ROWS), 1).astype(F32)
    def rows_of(sub):
        return slice(sub * ts, (sub + 1) * ts)

    def gate_matrix(sub, st):
        key1, key2, w1, w2 = (cols[rows_of(sub), c:c + 1] for c in range(4))
        st["gates"] = jnp.where(spos == key1, w1, jnp.where(spos == key2, w2, 0.0)).astype(BF16)

    def weighted_unsort(sub, st):
        st["y"] = _dot(st.pop("gates"), sorted_rows[sub])

    def norm_and_add(sub, st):
        rows = rows_of(sub)
        o_ref[rows, :] = xmid_ref[rows, :] + _rmsnorm(st.pop("y"), gt_ref[0] * g_ref[...])

    _run_staggered((gate_matrix, weighted_unsort, norm_and_add), SORT_GROUPS, lag=1)


def _combine(meta, ys, route, x_mid, gt_f, g_post, seq):
    t, d = x_mid.shape
    tm = ROW_TILE
    tiles_per_seq = seq // tm
    grid_spec = pltpu.PrefetchScalarGridSpec(
        num_scalar_prefetch=1,
        grid=(t // tm,),
        in_specs=[pl.BlockSpec(memory_space=pl.ANY),
                  pl.BlockSpec((SUBLANES, tm), lambda i, m: (0, i)),
                  pl.BlockSpec((tm, d), lambda i, m: (i, 0)),
                  pl.BlockSpec((1, 1, d), lambda i, m: (i // tiles_per_seq, 0, 0)),
                  pl.BlockSpec((1, d), lambda i, m: (0, 0))],
        out_specs=pl.BlockSpec((tm, d), lambda i, m: (i, 0)),
        scratch_shapes=[pltpu.VMEM((COMBINE_BUFFERS, SORT_GROUPS, SORT_ROWS * ROW_WORDS, LANES), I32),
                        pltpu.VMEM((LANES, tm), F32),
                        pltpu.SemaphoreType.DMA((COMBINE_BUFFERS,))],
    )
    return pl.pallas_call(
        _combine_kernel,
        grid_spec=grid_spec,
        out_shape=jax.ShapeDtypeStruct((t, d), F32),
        compiler_params=pltpu.CompilerParams(
            dimension_semantics=("arbitrary",), vmem_limit_bytes=VMEM_LIMIT),
        name="combine",
    )(meta, ys, route, x_mid, gt_f, g_post)


def kernel(x, c, w_ada, b_ada, g_pre_mix, g_post_mix, g_pre_ffn, g_post_ffn, w_in, w_pool_group, pool_scale, w_branch_pool, w_branch_att, w_out, w_group_router, b_group_router, w_expert_router, b_expert_router, w_exp_gate, w_exp_up, w_exp_down):
    b, s, d = x.shape
    assert d == D_MODEL and s % (ATT_BLOCK * 2 * ATT_DILATIONS[2]) == 0 and s % ROW_TILE == 0
    assert (b * s) % MOE_BLOCK == 0
    depth = w_ada.shape[0]
    slopes = jnp.exp2(-ALIBI_MAX_BIAS * jnp.arange(1, N_ATT_HEADS + 1, dtype=F32) / N_ATT_HEADS)
    q_scale = HEAD_DIM ** -0.5 * LOG2_E

    for layer in range(depth):
        mod = _adaln(c, w_ada[layer], b_ada[layer]).reshape(b, 6, 1, d)
        sh_m, sc_m, gt_m, sh_f, sc_f, gt_f = [mod[:, j] for j in range(6)]

        wl = w_in[layer]
        q_lo, k_lo, v_lo = POOL_WIDTH, POOL_WIDTH + 768, POOL_WIDTH + 2 * 768
        group_cols = []
        for g in range(3):
            sl = slice(g * 256, (g + 1) * 256)
            group_cols += [wl[:, q_lo:k_lo][:, sl] * q_scale, wl[:, k_lo:v_lo][:, sl],
                           wl[:, v_lo:v_lo + 768][:, sl]]
        w_perm = jnp.concatenate([wl[:, :POOL_WIDTH]] + group_cols + [wl[:, v_lo + 768:]],
                                 axis=1).astype(BF16)

        u, qkv0, qkv1, qkv2, gates = _inproj(x, g_pre_mix[layer].reshape(1, d), sc_m, sh_m, w_perm)
        att = _attention(qkv0, qkv1, qkv2, slopes)

        zero_block = jnp.zeros((POOL_GROUP_DIM, POOL_GROUP_DIM), F32)
        wpg_diag = jnp.block([[w_pool_group[layer, g] if g == h else zero_block
                               for h in range(len(POOL_WINDOWS))] for g in range(len(POOL_WINDOWS))])

        pad_rows = ROUTER_ROWS - N_EXPERTS - N_EXPERT_GROUPS
        wrt = jnp.concatenate([w_expert_router[layer].T, w_group_router[layer].T,
                               jnp.zeros((pad_rows, d), F32)], axis=0)
        wrt_hi = wrt.astype(BF16)
        wr = jnp.concatenate([wrt_hi, (wrt - wrt_hi.astype(F32)).astype(BF16)], axis=0)
        br = jnp.concatenate([b_expert_router[layer], b_group_router[layer],
                              jnp.zeros((pad_rows,), F32)]).reshape(ROUTER_ROWS, 1)

        x_mid, route, meta, counts, xs = _mixtail(
            u, att, gates, x, gt_m, sc_f, sh_f,
            g_post_mix[layer].reshape(1, d), g_pre_ffn[layer].reshape(1, d),
            wpg_diag.astype(BF16), pool_scale[layer].reshape(1, POOL_WIDTH),
            w_branch_pool[layer].astype(BF16), w_branch_att[layer].astype(BF16),
            w_out[layer].astype(BF16), wr, br)

        ys = _experts(xs, counts[0, :N_EXPERTS], w_exp_gate[layer], w_exp_up[layer], w_exp_down[layer])
        run_meta = meta[:, 0:2, 0:N_EXPERTS].reshape(-1)
        x = _combine(run_meta, ys, route, x_mid.reshape(b * s, d), gt_f,
                     g_post_ffn[layer].reshape(1, d), s).reshape(b, s, d)
    return x
```

```python
import functools

import jax
import jax.numpy as jnp
from jax import lax
from jax.experimental import pallas as pl
from jax.experimental.pallas import tpu as pltpu

F32 = jnp.float32
BF16 = jnp.bfloat16
I32 = jnp.int32

D_MODEL = 1024
LANES = 128
SUBLANES = 8
ROW_WORDS = D_MODEL // (2 * LANES)
HI_HALF = -65536

POOL_WINDOWS = (2, 4, 8, 16)
POOL_GROUP_DIM = 128
POOL_WIDTH = 512
POOL_HALO = 16
POOL_PAD = POOL_HALO + 8

HEAD_DIM = 64
ATT_DILATIONS = (1, 4, 16)
ATT_REACH = 128
ATT_BLOCK = 128
ATT_UNROLL = 16
HEADS_PER_GROUP = 4
N_ATT_HEADS = 12
GROUP_QKV = 3 * HEADS_PER_GROUP * HEAD_DIM
ATT_OUT_WIDTH = 256
ALIBI_MAX_BIAS = 8.0
IN_WIDTH = POOL_WIDTH + 3 * GROUP_QKV + 2 * D_MODEL
MASKED = -1e30
LOG2_E = 1.4426950408889634

N_EXPERT_GROUPS = 4
EXPERTS_PER_GROUP = 8
N_EXPERTS = 32
ROUTER_ROWS = 48
D_EXPERT = 512
RMS_EPS = 1e-6

ROW_TILE = 1024
ROW_SPLIT = 4
SORT_TOKENS = 256
SORT_GROUPS = ROW_TILE // SORT_TOKENS
SORT_ROWS = 2 * SORT_TOKENS
MOE_BLOCK = 256
EXPERT_CHAINS = 4
EXPERT_X_BUFFERS = 10
EXPERT_Y_BUFFERS = 8
COMBINE_BUFFERS = 3
VMEM_LIMIT = 52 * 1024 * 1024


def _sigmoid(x):
    return 0.5 * jnp.tanh(0.5 * x) + 0.5


def _rmsnorm(x, g):
    return x * lax.rsqrt(jnp.mean(x * x, axis=-1, keepdims=True) + RMS_EPS) * g


def _dot(a, b):
    return jnp.dot(a, b, preferred_element_type=F32)


def _dot_nt(a, b):
    return lax.dot_general(a, b, (((1,), (1,)), ((), ())), preferred_element_type=F32)


def _pack_rows(ref, w, n, lo, hi):
    word = (lax.shift_right_logical(pltpu.bitcast(lo, I32), 16) | (pltpu.bitcast(hi, I32) & HI_HALF))
    ref[pl.ds(w, n, stride=ROW_WORDS), :] = word


def _unpack_rows(ref, n):
    cols = []
    for w in range(ROW_WORDS):
        word = ref[pl.ds(w, n, stride=ROW_WORDS), :]
        cols += [pltpu.bitcast(word << 16, F32), pltpu.bitcast(word & HI_HALF, F32)]
    return jnp.concatenate(cols, axis=1)


def _bf16_exact(x):
    return x.astype(BF16).astype(F32)


def _run_staggered(stages, n_chains, lag):
    state = [{} for _ in range(n_chains)]
    for tick in range(len(stages) + lag * (n_chains - 1)):
        for chain in range(n_chains):
            if 0 <= tick - lag * chain < len(stages):
                stages[tick - lag * chain](chain, state[chain])
    return state


def _region_rows(n_tokens):
    return n_tokens + MOE_BLOCK


def _run_copies(n, src, src_row, dst, dst_row, sem):
    @pl.when(n > 0)
    def _():
        pltpu.make_async_copy(
            src.at[pl.ds(src_row * ROW_WORDS, n * ROW_WORDS), :],
            dst.at[pl.ds(dst_row * ROW_WORDS, n * ROW_WORDS), :], sem).start()


def _tile_run_copies(counts, src, src_rows, dst, dst_rows, sem):
    local = 0
    for e in range(N_EXPERTS):
        n = counts(e)
        _run_copies(n, src, local if src_rows is None else src_rows(e),
                    dst, local if dst_rows is None else dst_rows(e), sem)
        local = local + n


def _adaln_kernel(c_ref, w_ref, b_ref, o_ref):
    c = c_ref[...]
    a = c * _sigmoid(c)
    w = w_ref[...]
    a_hi, w_hi = a.astype(BF16), w.astype(BF16)
    a_lo = (a - a_hi.astype(F32)).astype(BF16)
    w_lo = (w - w_hi.astype(F32)).astype(BF16)
    o_ref[...] = _dot(a_hi, w_hi) + (_dot(a_hi, w_lo) + _dot(a_lo, w_hi)) + b_ref[...]


def _adaln(c, w_ada, b_ada):
    b, d = c.shape
    n = w_ada.shape[1]
    rows = -(-b // SUBLANES) * SUBLANES
    cp = jnp.pad(c, ((0, rows - b), (0, 0)))
    nt = 1536
    out = pl.pallas_call(
        _adaln_kernel,
        grid=(n // nt,),
        in_specs=[pl.BlockSpec((rows, d), lambda j: (0, 0)),
                  pl.BlockSpec((d, nt), lambda j: (0, j)),
                  pl.BlockSpec((1, nt), lambda j: (0, j))],
        out_specs=pl.BlockSpec((rows, nt), lambda j: (0, j)),
        out_shape=jax.ShapeDtypeStruct((rows, n), F32),
        compiler_params=pltpu.CompilerParams(vmem_limit_bytes=VMEM_LIMIT),
        name="adaln",
    )(cp, w_ada, b_ada.reshape(1, n))
    return out[:b]


def _inproj_kernel(x_ref, g_ref, sc_ref, sh_ref, w_ref,
                   u_ref, qkv0_ref, qkv1_ref, qkv2_ref, gate_ref, h_scr, p_scr):
    tm = x_ref.shape[1]
    hm = tm // ROW_SPLIT
    for part in range(ROW_SPLIT):
        rows = slice(part * hm, (part + 1) * hm)
        h = _rmsnorm(x_ref[0, rows, :], g_ref[...] * (1.0 + sc_ref[0])) + sh_ref[0]
        h_scr[rows, :] = h.astype(BF16)
        hb = h_scr[rows, :]

        u_ref[0, rows, :] = _dot(hb, w_ref[:, 0:POOL_WIDTH]).astype(BF16)

        col = POOL_WIDTH
        qkv0_ref[0, 0, rows, :] = _dot(hb, w_ref[:, col:col + GROUP_QKV]).astype(BF16)
        for gi, (out_ref, d) in enumerate(((qkv1_ref, ATT_DILATIONS[1]), (qkv2_ref, ATT_DILATIONS[2]))):
            col += GROUP_QKV
            proj = _dot(hb, w_ref[:, col:col + GROUP_QKV])
            stage = p_scr.at[part, gi]
            for cb in range(GROUP_QKV // LANES):
                stage[cb] = proj[:, cb * LANES:(cb + 1) * LANES]
            sub = hm // d
            for r in range(d):
                out_ref[0, r, part * sub:(part + 1) * sub, :] = jnp.concatenate(
                    [stage[cb, pl.ds(r, sub, stride=d), :] for cb in range(GROUP_QKV // LANES)],
                    axis=1).astype(BF16)
        col += GROUP_QKV

        chunk = 512
        for j in range(2 * D_MODEL // chunk):
            g = _dot(hb, w_ref[:, col + j * chunk:col + (j + 1) * chunk])
            gate_ref[0, rows, j * chunk:(j + 1) * chunk] = _sigmoid(g).astype(BF16)


def _inproj(x, g_pre, sc, sh, w_perm):
    b, s, d = x.shape
    tm = ROW_TILE
    d1, d2 = ATT_DILATIONS[1], ATT_DILATIONS[2]
    grid = (b, s // tm)
    const = lambda bi, i: (0, 0)
    per_b = lambda bi, i: (bi, 0, 0)
    return pl.pallas_call(
        _inproj_kernel,
        grid=grid,
        in_specs=[pl.BlockSpec((1, tm, d), lambda bi, i: (bi, i, 0)),
                  pl.BlockSpec((1, d), const),
                  pl.BlockSpec((1, 1, d), per_b),
                  pl.BlockSpec((1, 1, d), per_b),
                  pl.BlockSpec((d, IN_WIDTH), const, pipeline_mode=pl.Buffered(1))],
        out_specs=[pl.BlockSpec((1, tm, POOL_WIDTH), lambda bi, i: (bi, i, 0)),
                   pl.BlockSpec((1, 1, tm, GROUP_QKV), lambda bi, i: (bi, 0, i, 0)),
                   pl.BlockSpec((1, d1, tm // d1, GROUP_QKV), lambda bi, i: (bi, 0, i, 0)),
                   pl.BlockSpec((1, d2, tm // d2, GROUP_QKV), lambda bi, i: (bi, 0, i, 0)),
                   pl.BlockSpec((1, tm, 2 * D_MODEL), lambda bi, i: (bi, i, 0))],
        out_shape=[jax.ShapeDtypeStruct((b, s, POOL_WIDTH), BF16),
                   jax.ShapeDtypeStruct((b, 1, s, GROUP_QKV), BF16),
                   jax.ShapeDtypeStruct((b, d1, s // d1, GROUP_QKV), BF16),
                   jax.ShapeDtypeStruct((b, d2, s // d2, GROUP_QKV), BF16),
                   jax.ShapeDtypeStruct((b, s, 2 * D_MODEL), BF16)],
        scratch_shapes=[pltpu.VMEM((tm, d), BF16),
                        pltpu.VMEM((ROW_SPLIT, 2, GROUP_QKV // LANES, tm // ROW_SPLIT, LANES), F32)],
        compiler_params=pltpu.CompilerParams(
            dimension_semantics=("arbitrary", "arbitrary"), vmem_limit_bytes=VMEM_LIMIT),
        name="inproj",
    )(x, g_pre, sc, sh, w_perm)


def _attn_kernel(slopes_ref, q0, k0, v0, q1, k1, v1, q2, k2, v2, o_ref,
                 bias_scr, acc, mst, lst, s_even, s_odd):
    pair = pl.program_id(1)
    seq = o_ref.shape[1]
    nblk = seq // ATT_BLOCK
    lane = lax.broadcasted_iota(I32, (ATT_BLOCK, LANES), 1)
    first_head = lane < HEAD_DIM
    half_lane = lax.broadcasted_iota(I32, (ATT_BLOCK // 2, LANES), 1)
    head_bits = (jnp.where(half_lane < HEAD_DIM, -1, 0), jnp.where(half_lane < HEAD_DIM, 0, -1))
    ones_block = jnp.ones((2 * ATT_BLOCK, LANES), BF16)

    qi = lax.broadcasted_iota(I32, (ATT_BLOCK, 2 * ATT_BLOCK), 0)
    kj = lax.broadcasted_iota(I32, (ATT_BLOCK, 2 * ATT_BLOCK), 1)
    delta = ATT_BLOCK + qi - kj
    valid = (delta >= 0) & (delta <= ATT_REACH)
    delta0 = qi - kj
    valid0 = delta0 >= 0
    for g, d in enumerate(ATT_DILATIONS):
        for j in range(2):
            slope = slopes_ref[g * HEADS_PER_GROUP + 2 * pair + j]
            slope = slope * LOG2_E
            bias_scr[g, j, 0] = jnp.where(valid0, -slope * (delta0 * d).astype(F32), MASKED)
            bias_scr[g, j, 1] = jnp.where(valid, -slope * (delta * d).astype(F32), MASKED)

    n_iter = nblk // ATT_UNROLL

    def group_stages(g, q_ref, k_ref, v_ref):
        d = ATT_DILATIONS[g]
        per_res = nblk // d

        def block_index(it, k):
            n = it * ATT_UNROLL + k
            return n // per_res, n % per_res

        def key_rows(i):
            lo = jnp.maximum(i - 1, 0)
            return pl.ds(pl.multiple_of(lo * ATT_BLOCK, ATT_BLOCK), 2 * ATT_BLOCK)

        def scores(it, s_ref):
            for k in range(ATT_UNROLL):
                r, i = block_index(it, k)
                q = q_ref[0, r, pl.ds(pl.multiple_of(i * ATT_BLOCK, ATT_BLOCK), ATT_BLOCK), :]
                kw = k_ref[0, r, key_rows(i), :]
                qbits = pltpu.bitcast(q, I32)
                for j in range(2):
                    qh = pltpu.bitcast(qbits & head_bits[j], BF16)
                    s_ref[2 * k + j] = _dot_nt(qh, kw) + bias_scr[g, j, jnp.minimum(i, 1)]

        def weighted_values(it, s_ref):
            for k in range(ATT_UNROLL):
                r, i = block_index(it, k)
                vw = jnp.concatenate([v_ref[0, r, key_rows(i), :], ones_block], axis=1)
                outs = []
                for j in range(2):
                    sc = s_ref[2 * k + j]
                    mx = jnp.max(sc, axis=-1, keepdims=True)
                    p = jnp.exp2(sc - mx)
                    num_den = _dot(p.astype(BF16), vw)
                    outs.append((num_den[:, :LANES], mx, num_den[:, LANES:]))
                (n0, m0, l0), (n1, m1, l1) = outs
                if d == 1:
                    rows = pl.ds(pl.multiple_of(i * ATT_BLOCK, ATT_BLOCK), ATT_BLOCK)
                else:
                    rows = pl.ds(i * (ATT_BLOCK * d) + r, ATT_BLOCK, stride=d)
                acc[g, rows, :] = jnp.where(first_head, n0, n1)
                mst[g, rows, :] = jnp.where(first_head, m0, m1)
                lst[g, rows, :] = jnp.where(first_head, l0, l1)

        return scores, weighted_values

    stages = [group_stages(0, q0, k0, v0), group_stages(1, q1, k1, v1), group_stages(2, q2, k2, v2)]
    stages[0][0](jnp.int32(0), s_even)
    for g, (scores, weighted_values) in enumerate(stages):
        def pair_of_iterations(it, then_scores):
            scores(it + 1, s_odd)
            weighted_values(it, s_even)
            then_scores()
            weighted_values(it + 1, s_odd)

        def body(h, carry, pair_of_iterations=pair_of_iterations, scores=scores):
            it = 2 * h
            pair_of_iterations(it, lambda: scores(it + 2, s_even))
            return carry

        lax.fori_loop(0, n_iter // 2 - 1, body, 0)
        if g + 1 < len(stages):
            next_scores = stages[g + 1][0]
            pair_of_iterations(jnp.int32(n_iter - 2), lambda: next_scores(jnp.int32(0), s_even))
        else:
            pair_of_iterations(jnp.int32(n_iter - 2), lambda: None)

    def finish(it, carry):
        for k in range(ATT_UNROLL):
            rows = pl.ds(pl.multiple_of((it * ATT_UNROLL + k) * ATT_BLOCK, ATT_BLOCK), ATT_BLOCK)
            ms = [mst[g, rows, :] for g in range(3)]
            top = jnp.maximum(jnp.maximum(ms[0], ms[1]), ms[2])
            scale = [jnp.exp2(m - top) for m in ms]
            num = scale[0] * acc[0, rows, :] + scale[1] * acc[1, rows, :] + scale[2] * acc[2, rows, :]
            den = scale[0] * lst[0, rows, :] + scale[1] * lst[1, rows, :] + scale[2] * lst[2, rows, :]
            o_ref[0, rows, :] = (num / den).astype(BF16)
        return carry

    lax.fori_loop(0, nblk // ATT_UNROLL, finish, 0)


def _attention(qkv0, qkv1, qkv2, slopes):
    b, _, s, _ = qkv0.shape
    pairs = HEADS_PER_GROUP // 2
    col_blocks = HEADS_PER_GROUP * HEAD_DIM // LANES

    def specs(arr):
        _, d, sub, _ = arr.shape
        return [pl.BlockSpec((1, d, sub, LANES),
                             functools.partial(lambda bi, p, sec: (bi, 0, 0, sec * col_blocks + p), sec=sec))
                for sec in range(3)]

    return pl.pallas_call(
        _attn_kernel,
        grid=(b, pairs),
        in_specs=[pl.BlockSpec(memory_space=pltpu.SMEM)] + specs(qkv0) + specs(qkv1) + specs(qkv2),
        out_specs=pl.BlockSpec((1, s, LANES), lambda bi, p: (bi, 0, p)),
        out_shape=jax.ShapeDtypeStruct((b, s, ATT_OUT_WIDTH), BF16),
        scratch_shapes=[pltpu.VMEM((3, 2, 2, ATT_BLOCK, 2 * ATT_BLOCK), F32),
                        pltpu.VMEM((3, s, LANES), F32),
                        pltpu.VMEM((3, s, LANES), F32),
                        pltpu.VMEM((3, s, LANES), F32),
                        pltpu.VMEM((2 * ATT_UNROLL, ATT_BLOCK, 2 * ATT_BLOCK), F32),
                        pltpu.VMEM((2 * ATT_UNROLL, ATT_BLOCK, 2 * ATT_BLOCK), F32)],
        compiler_params=pltpu.CompilerParams(
            dimension_semantics=("arbitrary", "arbitrary"), vmem_limit_bytes=VMEM_LIMIT),
        name="attention",
    )(slopes, qkv0, qkv0, qkv0, qkv1, qkv1, qkv1, qkv2, qkv2, qkv2)


def _mixtail_kernel(u_ref, halo_ref, att_ref, gate_ref, x_ref,
                    gt_m_ref, sc_f_ref, sh_f_ref, g_post_ref, g_pre_ref,
                    wpg_ref, pscale_ref, wbp_ref, wba_ref, wout_ref, wr_ref, br_ref,
                    xmid_ref, route_ref, meta_ref, counts_ref, xs_hbm,
                    pu, lv, xbuf, zbuf, fill, meta_s, sem, sem_s):
    i = pl.program_id(1)
    tm = x_ref.shape[1]
    step = pl.program_id(0) * pl.num_programs(1) + i
    last = pl.num_programs(0) * pl.num_programs(1) - 1
    slot = step % 2
    region = _region_rows(pl.num_programs(0) * pl.num_programs(1) * tm)

    @pl.when(step == 0)
    def _():
        fill[...] = jnp.zeros_like(fill)
        pu[0:POOL_PAD - POOL_HALO, :] = jnp.zeros((POOL_PAD - POOL_HALO, POOL_WIDTH), F32)
        lv[:, :, 0:POOL_PAD - POOL_HALO, :] = jnp.zeros(
            (SORT_GROUPS, 2, POOL_PAD - POOL_HALO, POOL_GROUP_DIM), F32)
        for sub in range(SORT_GROUPS):
            for e in range(N_EXPERTS):
                meta_s[SORT_GROUPS + sub, 1, e] = 0
                meta_s[SORT_GROUPS + sub, 0, e] = 0

    def send_tile(which):
        for sub in range(SORT_GROUPS):
            m = which * SORT_GROUPS + sub
            _tile_run_copies(lambda e: meta_s[m, 1, e], xbuf.at[which, sub], None,
                             xs_hbm, lambda e: e * region + meta_s[m, 0, e], sem.at[which])

    def wait_tile(which):
        for sub in range(SORT_GROUPS):
            pltpu.make_async_copy(xbuf.at[which, sub], xs_hbm.at[pl.ds(0, SORT_ROWS * ROW_WORDS), :],
                                  sem.at[which]).wait()

    @pl.when(step > 1)
    def _():
        wait_tile(slot)

    send_tile(1 - slot)

    head = POOL_PAD - POOL_HALO
    halo = halo_ref[0].astype(F32)
    pu[head:POOL_PAD, :] = jnp.where(i > 0, halo, jnp.zeros_like(halo))
    pu[POOL_PAD:POOL_PAD + tm, :] = u_ref[0].astype(F32)

    ts = SORT_TOKENS

    def pool(sub, st):
        base = sub * ts
        t = i * tm + base + lax.broadcasted_iota(I32, (ts, 1), 0)
        pooled_groups = []
        for g, w in enumerate(POOL_WINDOWS):
            cols = slice(g * POOL_GROUP_DIM, (g + 1) * POOL_GROUP_DIM)
            read = lambda start, n: pu[pl.ds(base + start, n), cols]
            shift, level = 1, 0
            while 2 * shift < w:
                partial = read(head, ts + POOL_HALO) + read(head - shift, ts + POOL_HALO)
                buf = lv.at[sub, level % 2]
                buf[pl.ds(head, ts + POOL_HALO), :] = partial
                read = lambda start, n, buf=buf: buf[pl.ds(start, n), :]
                shift, level = 2 * shift, level + 1
            win = read(POOL_PAD, ts) + read(POOL_PAD - shift, ts)
            count = jnp.minimum(t + 1, w).astype(F32)
            pooled_groups.append((win / count - pu[pl.ds(base + POOL_PAD, ts), cols]).astype(BF16))
        st["pooled"] = jnp.concatenate(pooled_groups, axis=1)

    neg_inf = -jnp.inf
    far = float(LANES)
    before = (lax.broadcasted_iota(I32, (ts, ts), 0) < lax.broadcasted_iota(I32, (ts, ts), 1))
    earlier = jnp.where(before, 1.0, 0.0).astype(BF16)
    erow = lax.broadcasted_iota(I32, (N_EXPERTS, ts), 0).astype(F32)
    e_col = lax.broadcasted_iota(I32, (N_EXPERTS, LANES), 0)
    e_lane = lax.broadcasted_iota(I32, (N_EXPERTS, LANES), 1)
    row8 = lax.broadcasted_iota(I32, (SUBLANES, ts), 0)
    mrow = lax.broadcasted_iota(I32, (SUBLANES, LANES), 0)
    srow = lax.broadcasted_iota(I32, (SORT_ROWS, ts), 0).astype(F32)
    filled = fill[...]

    def rows_of(sub):
        return slice(sub * ts, (sub + 1) * ts)

    def branch_projections(sub, st):
        rows = rows_of(sub)
        mixed = _dot(st.pop("pooled"), wpg_ref[...]) * pscale_ref[...]
        st["y_pool"] = _dot(mixed.astype(BF16), wbp_ref[...])
        st["y_att"] = _dot(att_ref[0, rows, :], wba_ref[...])

    def gated_sum(sub, st):
        rows = rows_of(sub)
        st["merged"] = (gate_ref[0, rows, 0:D_MODEL] * st.pop("y_pool").astype(BF16)
                        + gate_ref[0, rows, D_MODEL:2 * D_MODEL] * st.pop("y_att").astype(BF16))

    def output_projection(sub, st):
        st["y"] = _dot(st.pop("merged"), wout_ref[...])

    def residual_and_ffn_input(sub, st):
        rows = rows_of(sub)
        x_mid = x_ref[0, rows, :] + _rmsnorm(st.pop("y"), gt_m_ref[0] * g_post_ref[...])
        xmid_ref[0, rows, :] = x_mid
        h2 = _rmsnorm(x_mid, g_pre_ref[...] * (1.0 + sc_f_ref[0])) + sh_f_ref[0]
        st["h2b"] = h2.astype(BF16)
        st["h2lo"] = (h2 - st["h2b"].astype(F32)).astype(BF16)

    def router_logits(sub, st):
        by_hi = _dot_nt(wr_ref[...], st["h2b"])
        st["logits"] = (by_hi[0:ROUTER_ROWS] + (by_hi[ROUTER_ROWS:2 * ROUTER_ROWS]
                        + _dot_nt(wr_ref[0:ROUTER_ROWS, :], st.pop("h2lo"))) + br_ref[...])

    def route(sub, st):
        logits = st.pop("logits")
        gl = logits[N_EXPERTS:N_EXPERTS + N_EXPERT_GROUPS, :]
        grow = lax.broadcasted_iota(I32, gl.shape, 0).astype(F32)
        gmax = jnp.max(gl, axis=0, keepdims=True)
        gsel = jnp.min(jnp.where(gl == gmax, grow, far), axis=0, keepdims=True)
        p_group = 1.0 / jnp.sum(jnp.exp(gl - gmax), axis=0, keepdims=True)
        e_lo = gsel * float(EXPERTS_PER_GROUP)
        el = jnp.where((erow >= e_lo) & (erow < e_lo + float(EXPERTS_PER_GROUP)),
                       logits[0:N_EXPERTS, :], neg_inf)
        v1 = jnp.max(el, axis=0, keepdims=True)
        i1 = jnp.min(jnp.where(el == v1, erow, far), axis=0, keepdims=True)
        el2 = jnp.where(erow == i1, neg_inf, el)
        v2 = jnp.max(el2, axis=0, keepdims=True)
        i2 = jnp.min(jnp.where(el2 == v2, erow, far), axis=0, keepdims=True)
        e21 = jnp.exp(v2 - v1)
        st["w1"] = p_group / (1.0 + e21)
        st["w2"] = p_group * e21 / (1.0 + e21)
        st["pick1"] = erow == i1
        st["pick2"] = erow == i2
        st["assign"] = jnp.where(st["pick1"] | st["pick2"], 1.0, 0.0)

    def count_and_rank(sub, st):
        assign = st.pop("assign")
        st["rank"] = _dot(assign.astype(BF16), earlier)
        assign_pad = jnp.concatenate([assign, jnp.zeros((LANES - N_EXPERTS, ts), F32)], axis=0).astype(BF16)
        st["cnt_row"] = _dot_nt(jnp.ones((SUBLANES, ts), BF16), assign_pad)

    def sorted_positions(sub, st):
        cnt_row = st["cnt_row"]
        run_start = jnp.sum(jnp.where(e_lane < e_col, cnt_row[0:1, :], 0.0), axis=1, keepdims=True)
        pos = st.pop("rank") + run_start
        key1 = jnp.sum(jnp.where(st.pop("pick1"), pos, 0.0), axis=0, keepdims=True)
        key2 = jnp.sum(jnp.where(st.pop("pick2"), pos, 0.0), axis=0, keepdims=True)
        route_ref[:, rows_of(sub)] = jnp.where(row8 == 0, key1, jnp.where(row8 == 1, key2,
                                               jnp.where(row8 == 2, st.pop("w1"),
                                                         jnp.where(row8 == 3, st.pop("w2"), 0.0))))
        st["perm"] = jnp.where((srow == key1) | (srow == key2), 1.0, 0.0).astype(BF16)

    def sort_rows(sub, st):
        perm, h2b = st.pop("perm"), st.pop("h2b")
        for w in range(ROW_WORDS):
            pair = _dot(perm, h2b[:, 2 * w * LANES:(2 * w + 2) * LANES])
            _pack_rows(xbuf.at[slot, sub], w, SORT_ROWS, pair[:, :LANES], pair[:, LANES:])

    state = _run_staggered((pool, branch_projections, gated_sum, output_projection,
                            residual_and_ffn_input, router_logits, route, count_and_rank,
                            sorted_positions, sort_rows), SORT_GROUPS, lag=0)

    for sub in range(SORT_GROUPS):
        cnt_row = state[sub]["cnt_row"]
        meta_ref[sub] = jnp.where(mrow == 0, filled, jnp.where(mrow == 1, cnt_row, 0.0)).astype(I32)
        filled = filled + cnt_row

    fill[...] = filled
    counts_ref[...] = filled.astype(I32)

    meta_copy = pltpu.make_async_copy(meta_ref, meta_s.at[pl.ds(slot * SORT_GROUPS, SORT_GROUPS)], sem_s)
    meta_copy.start()
    meta_copy.wait()

    @pl.when(step == last)
    def _():
        send_tile(slot)

        @pl.when(step > 0)
        def _():
            wait_tile(1 - slot)

        wait_tile(slot)
        zbuf[...] = jnp.zeros_like(zbuf)
        final = slot * SORT_GROUPS + SORT_GROUPS - 1

        def pad_copy(e):
            end = e * region + meta_s[final, 0, e] + meta_s[final, 1, e]
            return pltpu.make_async_copy(
                zbuf, xs_hbm.at[pl.ds(end * ROW_WORDS, MOE_BLOCK * ROW_WORDS), :], sem_s)

        def start_pad(e, carry):
            pad_copy(e).start()
            return carry

        def wait_pad(e, carry):
            pad_copy(e).wait()
            return carry

        lax.fori_loop(0, N_EXPERTS, start_pad, 0)
        lax.fori_loop(0, N_EXPERTS, wait_pad, 0)


def _mixtail(u, att, gates, x, gt_m, sc_f, sh_f, g_post, g_pre,
             wpg, pscale, wbp, wba, wout, wr, br):
    b, s, d = x.shape
    tm = ROW_TILE
    tiles = s // tm
    n_tiles = b * tiles
    halo_blocks = tm // POOL_HALO
    region = _region_rows(b * s)
    const2 = lambda bi, i: (0, 0)
    const3 = lambda bi, i: (0, 0, 0)
    per_b = lambda bi, i: (bi, 0, 0)
    tile = lambda bi, i: (bi, i, 0)
    single = dict(pipeline_mode=pl.Buffered(1))
    return pl.pallas_call(
        _mixtail_kernel,
        grid=(b, tiles),
        in_specs=[pl.BlockSpec((1, tm, POOL_WIDTH), tile),
                  pl.BlockSpec((1, POOL_HALO, POOL_WIDTH),
                               lambda bi, i: (bi, jnp.maximum(i * halo_blocks - 1, 0), 0)),
                  pl.BlockSpec((1, tm, ATT_OUT_WIDTH), tile),
                  pl.BlockSpec((1, tm, 2 * D_MODEL), tile),
                  pl.BlockSpec((1, tm, d), tile),
                  pl.BlockSpec((1, 1, d), per_b),
                  pl.BlockSpec((1, 1, d), per_b),
                  pl.BlockSpec((1, 1, d), per_b),
                  pl.BlockSpec((1, d), const2),
                  pl.BlockSpec((1, d), const2),
                  pl.BlockSpec(wpg.shape, const2, **single),
                  pl.BlockSpec((1, POOL_WIDTH), const2),
                  pl.BlockSpec(wbp.shape, const2, **single),
                  pl.BlockSpec(wba.shape, const2, **single),
                  pl.BlockSpec(wout.shape, const2, **single),
                  pl.BlockSpec(wr.shape, const2, **single),
                  pl.BlockSpec(br.shape, const2)],
        out_specs=[pl.BlockSpec((1, tm, d), tile),
                   pl.BlockSpec((SUBLANES, tm), lambda bi, i: (0, bi * tiles + i)),
                   pl.BlockSpec((SORT_GROUPS, SUBLANES, LANES), lambda bi, i: (bi * tiles + i, 0, 0)),
                   pl.BlockSpec((SUBLANES, LANES), const2),
                   pl.BlockSpec(memory_space=pl.ANY)],
        out_shape=[jax.ShapeDtypeStruct((b, s, d), F32),
                   jax.ShapeDtypeStruct((SUBLANES, b * s), F32),
                   jax.ShapeDtypeStruct((n_tiles * SORT_GROUPS, SUBLANES, LANES), I32),
                   jax.ShapeDtypeStruct((SUBLANES, LANES), I32),
                   jax.ShapeDtypeStruct((N_EXPERTS * region * ROW_WORDS, LANES), I32)],
        scratch_shapes=[pltpu.VMEM((POOL_PAD + tm, POOL_WIDTH), F32),
                        pltpu.VMEM((SORT_GROUPS, 2, POOL_PAD + SORT_TOKENS, POOL_GROUP_DIM), F32),
                        pltpu.VMEM((2, SORT_GROUPS, SORT_ROWS * ROW_WORDS, LANES), I32),
                        pltpu.VMEM((MOE_BLOCK * ROW_WORDS, LANES), I32),
                        pltpu.VMEM((SUBLANES, LANES), F32),
                        pltpu.SMEM((2 * SORT_GROUPS, SUBLANES, LANES), I32),
                        pltpu.SemaphoreType.DMA((2,)),
                        pltpu.SemaphoreType.DMA(())],
        compiler_params=pltpu.CompilerParams(
            dimension_semantics=("arbitrary", "arbitrary"), vmem_limit_bytes=VMEM_LIMIT),
        name="mixtail",
    )(u, u, att, gates, x, gt_m, sc_f, sh_f, g_post, g_pre,
      wpg, pscale, wbp, wba, wout, wr, br)


def _expert_kernel(counts_ref, xs_hbm, wg_ref, wu_ref, wd_ref, ys_hbm,
                   xbuf, ybuf, state, semx, semy):
    e = pl.program_id(0)
    bm = MOE_BLOCK
    nx, ny = EXPERT_X_BUFFERS, EXPERT_Y_BUFFERS
    block_words = bm * ROW_WORDS
    region = xs_hbm.shape[0] // (N_EXPERTS * ROW_WORDS)

    def n_blocks(ex):
        return (counts_ref[ex] + (bm - 1)) // bm

    def block_rows(ref, ex, k):
        start = pl.multiple_of((ex * region + k * bm) * ROW_WORDS, block_words)
        return ref.at[pl.ds(start, block_words), :]

    def x_copy(ex, k, s):
        return pltpu.make_async_copy(block_rows(xs_hbm, ex, k), xbuf.at[s], semx.at[s])

    def y_copy(k, s):
        return pltpu.make_async_copy(ybuf.at[s], block_rows(ys_hbm, e, k), semy.at[s])

    @pl.when(e == 0)
    def _():
        for j in range(4):
            state[j] = 0

    def fetch_through(target):
        def more(c):
            pe, _, pg = c
            return (pg < target) & (pe < N_EXPERTS)

        def step(c):
            pe, pk, pg = c
            has = pk < n_blocks(pe)

            @pl.when(has)
            def _():
                x_copy(pe, pk, pg % nx).start()

            return (jnp.where(has, pe, pe + 1), jnp.where(has, pk + 1, 0), pg + has.astype(I32))

        pe, pk, pg = lax.while_loop(more, step, (state[1], state[2], state[3]))
        state[1] = pe
        state[2] = pk
        state[3] = pg

    nb = n_blocks(e)
    done = state[0]

    def blocks(k0, count):
        ks = [k0 + c for c in range(count)]
        gs = [done + k for k in ks]
        fetch_through(gs[0] + nx)
        for k, g in zip(ks, gs):
            x_copy(e, k, g % nx).wait()

            @pl.when(g >= ny)
            def _():
                y_copy(k, g % ny).wait()

        mids = []
        for g in gs:
            x = _unpack_rows(xbuf.at[g % nx], bm).astype(BF16)
            a = _dot(x, wg_ref[...].astype(BF16))
            u = _dot(x, wu_ref[...].astype(BF16))
            mids.append(((a * _sigmoid(a)) * u).astype(BF16))
        for k, g, mid in zip(ks, gs, mids):
            for w in range(ROW_WORDS):
                pair = _bf16_exact(_dot(mid, wd_ref[:, 2 * w * LANES:(2 * w + 2) * LANES].astype(BF16)))
                _pack_rows(ybuf.at[g % ny], w, bm, pair[:, :LANES], pair[:, LANES:])
            y_copy(k, g % ny).start()

    per = EXPERT_CHAINS

    def several_blocks(p, carry):
        blocks(per * p, per)
        return carry

    lax.fori_loop(0, nb // per, several_blocks, 0)
    k_next = (nb // per) * per
    count = per // 2
    while count >= 1:
        take = ((nb - k_next) >= count)

        @pl.when(take)
        def _(k_next=k_next, count=count):
            blocks(k_next, count)

        k_next = k_next + jnp.where(take, count, 0)
        count //= 2

    state[0] = done + nb

    @pl.when(e == N_EXPERTS - 1)
    def _():
        total = done + nb
        for j in range(ny):
            @pl.when(total > j)
            def _():
                y_copy(0, (total - 1 - j) % ny).wait()


def _experts(xs, counts, w_gate, w_up, w_down):
    bm = MOE_BLOCK
    w_in = pl.BlockSpec((None, D_MODEL, D_EXPERT), lambda e, cnt: (e, 0, 0))
    w_out = pl.BlockSpec((None, D_EXPERT, D_MODEL), lambda e, cnt: (e, 0, 0))
    grid_spec = pltpu.PrefetchScalarGridSpec(
        num_scalar_prefetch=1,
        grid=(N_EXPERTS,),
        in_specs=[pl.BlockSpec(memory_space=pl.ANY), w_in, w_in, w_out],
        out_specs=pl.BlockSpec(memory_space=pl.ANY),
        scratch_shapes=[pltpu.VMEM((EXPERT_X_BUFFERS, bm * ROW_WORDS, LANES), I32),
                        pltpu.VMEM((EXPERT_Y_BUFFERS, bm * ROW_WORDS, LANES), I32),
                        pltpu.SMEM((4,), I32),
                        pltpu.SemaphoreType.DMA((EXPERT_X_BUFFERS,)),
                        pltpu.SemaphoreType.DMA((EXPERT_Y_BUFFERS,))],
    )
    return pl.pallas_call(
        _expert_kernel,
        grid_spec=grid_spec,
        out_shape=jax.ShapeDtypeStruct(xs.shape, I32),
        compiler_params=pltpu.CompilerParams(
            dimension_semantics=("arbitrary",), vmem_limit_bytes=VMEM_LIMIT),
        name="experts",
    )(counts, xs, w_gate, w_up, w_down)


def _combine_kernel(meta_ref, y_hbm, route_ref, xmid_ref, gt_ref, g_ref, o_ref, ybuf, rt_scr, sem):
    step = pl.program_id(0)
    n_steps = pl.num_programs(0)
    tm = ROW_TILE
    slot = step % COMBINE_BUFFERS
    region = _region_rows(n_steps * tm)

    def fetch_tile(tile, live):
        tile = jnp.minimum(tile, n_steps - 1)
        which = tile % COMBINE_BUFFERS
        for sub in range(SORT_GROUPS):
            m = (tile * SORT_GROUPS + sub) * 2
            _tile_run_copies(lambda e: jnp.where(live, meta_ref[(m + 1) * N_EXPERTS + e], 0),
                             y_hbm, lambda e: e * region + meta_ref[m * N_EXPERTS + e],
                             ybuf.at[which, sub], None, sem.at[which])

    @pl.when(step == 0)
    def _():
        for ahead in range(COMBINE_BUFFERS - 1):
            fetch_tile(ahead, ahead < n_steps)

    for sub in range(SORT_GROUPS):
        pltpu.make_async_copy(y_hbm.at[pl.ds(0, SORT_ROWS * ROW_WORDS), :], ybuf.at[slot, sub],
                              sem.at[slot]).wait()
    sorted_rows = [_unpack_rows(ybuf.at[slot, sub], SORT_ROWS).astype(BF16) for sub in range(SORT_GROUPS)]

    fetch_tile(step + COMBINE_BUFFERS - 1, step + COMBINE_BUFFERS - 1 < n_steps)

    rt_scr[...] = jnp.zeros_like(rt_scr)
    rt_scr[0:SUBLANES, :] = route_ref[...]
    cols = jnp.concatenate([rt_scr[:, c * LANES:(c + 1) * LANES].T for c in range(tm // LANES)], axis=0)

    ts = SORT_TOKENS
    spos = lax.broadcasted_iota(I32, (ts, SORT_ROWS), 1).astype(F32)
    def rows_of(sub):
        return slice(sub * ts, (sub + 1) * ts)

    def gate_matrix(sub, st):
        key1, key2, w1, w2 = (cols[rows_of(sub), c:c + 1] for c in range(4))
        st["gates"] = jnp.where(spos == key1, w1, jnp.where(spos == key2, w2, 0.0)).astype(BF16)

    def weighted_unsort(sub, st):
        st["y"] = _dot(st.pop("gates"), sorted_rows[sub])

    def norm_and_add(sub, st):
        rows = rows_of(sub)
        o_ref[rows, :] = xmid_ref[rows, :] + _rmsnorm(st.pop("y"), gt_ref[0] * g_ref[...])

    _run_staggered((gate_matrix, weighted_unsort, norm_and_add), SORT_GROUPS, lag=1)


def _combine(meta, ys, route, x_mid, gt_f, g_post, seq):
    t, d = x_mid.shape
    tm = ROW_TILE
    tiles_per_seq = seq // tm
    grid_spec = pltpu.PrefetchScalarGridSpec(
        num_scalar_prefetch=1,
        grid=(t // tm,),
        in_specs=[pl.BlockSpec(memory_space=pl.ANY),
                  pl.BlockSpec((SUBLANES, tm), lambda i, m: (0, i)),
                  pl.BlockSpec((tm, d), lambda i, m: (i, 0)),
                  pl.BlockSpec((1, 1, d), lambda i, m: (i // tiles_per_seq, 0, 0)),
                  pl.BlockSpec((1, d), lambda i, m: (0, 0))],
        out_specs=pl.BlockSpec((tm, d), lambda i, m: (i, 0)),
        scratch_shapes=[pltpu.VMEM((COMBINE_BUFFERS, SORT_GROUPS, SORT_ROWS * ROW_WORDS, LANES), I32),
                        pltpu.VMEM((LANES, tm), F32),
                        pltpu.SemaphoreType.DMA((COMBINE_BUFFERS,))],
    )
    return pl.pallas_call(
        _combine_kernel,
        grid_spec=grid_spec,
        out_shape=jax.ShapeDtypeStruct((t, d), F32),
        compiler_params=pltpu.CompilerParams(
            dimension_semantics=("arbitrary",), vmem_limit_bytes=VMEM_LIMIT),
        name="combine",
    )(meta, ys, route, x_mid, gt_f, g_post)


def kernel(x, c, w_ada, b_ada, g_pre_mix, g_post_mix, g_pre_ffn, g_post_ffn, w_in, w_pool_group, pool_scale, w_branch_pool, w_branch_att, w_out, w_group_router, b_group_router, w_expert_router, b_expert_router, w_exp_gate, w_exp_up, w_exp_down):
    b, s, d = x.shape
    assert d == D_MODEL and s % (ATT_BLOCK * 2 * ATT_DILATIONS[2]) == 0 and s % ROW_TILE == 0
    assert (b * s) % MOE_BLOCK == 0
    depth = w_ada.shape[0]
    slopes = jnp.exp2(-ALIBI_MAX_BIAS * jnp.arange(1, N_ATT_HEADS + 1, dtype=F32) / N_ATT_HEADS)
    q_scale = HEAD_DIM ** -0.5 * LOG2_E

    for layer in range(depth):
        mod = _adaln(c, w_ada[layer], b_ada[layer]).reshape(b, 6, 1, d)
        sh_m, sc_m, gt_m, sh_f, sc_f, gt_f = [mod[:, j] for j in range(6)]

        wl = w_in[layer]
        q_lo, k_lo, v_lo = POOL_WIDTH, POOL_WIDTH + 768, POOL_WIDTH + 2 * 768
        group_cols = []
        for g in range(3):
            sl = slice(g * 256, (g + 1) * 256)
            group_cols += [wl[:, q_lo:k_lo][:, sl] * q_scale, wl[:, k_lo:v_lo][:, sl],
                           wl[:, v_lo:v_lo + 768][:, sl]]
        w_perm = jnp.concatenate([wl[:, :POOL_WIDTH]] + group_cols + [wl[:, v_lo + 768:]],
                                 axis=1).astype(BF16)

        u, qkv0, qkv1, qkv2, gates = _inproj(x, g_pre_mix[layer].reshape(1, d), sc_m, sh_m, w_perm)
        att = _attention(qkv0, qkv1, qkv2, slopes)

        zero_block = jnp.zeros((POOL_GROUP_DIM, POOL_GROUP_DIM), F32)
        wpg_diag = jnp.block([[w_pool_group[layer, g] if g == h else zero_block
                               for h in range(len(POOL_WINDOWS))] for g in range(len(POOL_WINDOWS))])

        pad_rows = ROUTER_ROWS - N_EXPERTS - N_EXPERT_GROUPS
        wrt = jnp.concatenate([w_expert_router[layer].T, w_group_router[layer].T,
                               jnp.zeros((pad_rows, d), F32)], axis=0)
        wrt_hi = wrt.astype(BF16)
        wr = jnp.concatenate([wrt_hi, (wrt - wrt_hi.astype(F32)).astype(BF16)], axis=0)
        br = jnp.concatenate([b_expert_router[layer], b_group_router[layer],
                              jnp.zeros((pad_rows,), F32)]).reshape(ROUTER_ROWS, 1)

        x_mid, route, meta, counts, xs = _mixtail(
            u, att, gates, x, gt_m, sc_f, sh_f,
            g_post_mix[layer].reshape(1, d), g_pre_ffn[layer].reshape(1, d),
            wpg_diag.astype(BF16), pool_scale[layer].reshape(1, POOL_WIDTH),
            w_branch_pool[layer].astype(BF16), w_branch_att[layer].astype(BF16),
            w_out[layer].astype(BF16), wr, br)

        ys = _experts(xs, counts[0, :N_EXPERTS], w_exp_gate[layer], w_exp_up[layer], w_exp_down[layer])
        run_meta = meta[:, 0:2, 0:N_EXPERTS].reshape(-1)
        x = _combine(run_meta, ys, route, x_mid.reshape(b * s, d), gt_f,
                     g_post_ffn[layer].reshape(1, d), s).reshape(b, s, d)
    return x
```

```python
import functools

import jax
import jax.numpy as jnp
from jax import lax
from jax.experimental import pallas as pl
from jax.experimental.pallas import tpu as pltpu

F32 = jnp.float32
BF16 = jnp.bfloat16
I32 = jnp.int32

D_MODEL = 1024
LANES = 128
SUBLANES = 8
ROW_WORDS = D_MODEL // (2 * LANES)
HI_HALF = -65536

POOL_WINDOWS = (2, 4, 8, 16)
POOL_GROUP_DIM = 128
POOL_WIDTH = 512
POOL_HALO = 16
POOL_PAD = POOL_HALO + 8

HEAD_DIM = 64
ATT_DILATIONS = (1, 4, 16)
ATT_REACH = 128
ATT_BLOCK = 128
ATT_UNROLL = 16
HEADS_PER_GROUP = 4
N_ATT_HEADS = 12
GROUP_QKV = 3 * HEADS_PER_GROUP * HEAD_DIM
ATT_OUT_WIDTH = 256
ALIBI_MAX_BIAS = 8.0
IN_WIDTH = POOL_WIDTH + 3 * GROUP_QKV + 2 * D_MODEL
MASKED = -1e30
LOG2_E = 1.4426950408889634
Q_SCALE = HEAD_DIM ** -0.5 * LOG2_E

N_EXPERT_GROUPS = 4
EXPERTS_PER_GROUP = 8
N_EXPERTS = 32
ROUTER_ROWS = 48
D_EXPERT = 512
RMS_EPS = 1e-6

ROW_TILE = 1024
ROW_SPLIT = 4
SORT_TOKENS = 256
SORT_GROUPS = ROW_TILE // SORT_TOKENS
SORT_ROWS = 2 * SORT_TOKENS
MOE_BLOCK = 256
EXPERT_CHAINS = 4
EXPERT_X_BUFFERS = 10
EXPERT_Y_BUFFERS = 8
COMBINE_BUFFERS = 3
VMEM_LIMIT = 52 * 1024 * 1024


def _sigmoid(x):
    return 0.5 * jnp.tanh(0.5 * x) + 0.5


def _rmsnorm(x, g):
    return x * lax.rsqrt(jnp.mean(x * x, axis=-1, keepdims=True) + RMS_EPS) * g


def _dot(a, b):
    return jnp.dot(a, b, preferred_element_type=F32)


def _dot_nt(a, b):
    return lax.dot_general(a, b, (((1,), (1,)), ((), ())), preferred_element_type=F32)


def _pack_rows(ref, w, n, lo, hi):
    word = (lax.shift_right_logical(pltpu.bitcast(lo, I32), 16) | (pltpu.bitcast(hi, I32) & HI_HALF))
    ref[pl.ds(w, n, stride=ROW_WORDS), :] = word


def _unpack_rows(ref, n):
    cols = []
    for w in range(ROW_WORDS):
        word = ref[pl.ds(w, n, stride=ROW_WORDS), :]
        cols += [pltpu.bitcast(word << 16, F32), pltpu.bitcast(word & HI_HALF, F32)]
    return jnp.concatenate(cols, axis=1)


def _bf16_exact(x):
    return x.astype(BF16).astype(F32)


def _run_staggered(stages, n_chains, lag):
    state = [{} for _ in range(n_chains)]
    for tick in range(len(stages) + lag * (n_chains - 1)):
        for chain in range(n_chains):
            if 0 <= tick - lag * chain < len(stages):
                stages[tick - lag * chain](chain, state[chain])
    return state


def _region_rows(n_tokens):
    return n_tokens + MOE_BLOCK


def _run_copies(n, src, src_row, dst, dst_row, sem):
    @pl.when(n > 0)
    def _():
        pltpu.make_async_copy(
            src.at[pl.ds(src_row * ROW_WORDS, n * ROW_WORDS), :],
            dst.at[pl.ds(dst_row * ROW_WORDS, n * ROW_WORDS), :], sem).start()


def _tile_run_copies(counts, src, src_rows, dst, dst_rows, sem):
    local = 0
    for e in range(N_EXPERTS):
        n = counts(e)
        _run_copies(n, src, local if src_rows is None else src_rows(e),
                    dst, local if dst_rows is None else dst_rows(e), sem)
        local = local + n


def _adaln_kernel(c_ref, w_ref, b_ref, o_ref):
    c = c_ref[...]
    a = c * _sigmoid(c)
    w = w_ref[...]
    a_hi, w_hi = a.astype(BF16), w.astype(BF16)
    a_lo = (a - a_hi.astype(F32)).astype(BF16)
    w_lo = (w - w_hi.astype(F32)).astype(BF16)
    o_ref[...] = _dot(a_hi, w_hi) + (_dot(a_hi, w_lo) + _dot(a_lo, w_hi)) + b_ref[...]


def _adaln(c, w_ada, b_ada):
    b, d = c.shape
    n = w_ada.shape[1]
    rows = -(-b // SUBLANES) * SUBLANES
    cp = jnp.pad(c, ((0, rows - b), (0, 0)))
    nt = 1536
    out = pl.pallas_call(
        _adaln_kernel,
        grid=(n // nt,),
        in_specs=[pl.BlockSpec((rows, d), lambda j: (0, 0)),
                  pl.BlockSpec((d, nt), lambda j: (0, j)),
                  pl.BlockSpec((1, nt), lambda j: (0, j))],
        out_specs=pl.BlockSpec((rows, nt), lambda j: (0, j)),
        out_shape=jax.ShapeDtypeStruct((rows, n), F32),
        compiler_params=pltpu.CompilerParams(vmem_limit_bytes=VMEM_LIMIT),
        name="adaln",
    )(cp, w_ada, b_ada.reshape(1, n))
    return out[:b]


def _inproj_kernel(x_ref, g_ref, sc_ref, sh_ref, w_ref,
                   u_ref, qkv0_ref, qkv1_ref, qkv2_ref, gate_ref, h_scr, p_scr):
    tm = x_ref.shape[1]
    hm = tm // ROW_SPLIT
    for part in range(ROW_SPLIT):
        rows = slice(part * hm, (part + 1) * hm)
        h = _rmsnorm(x_ref[0, rows, :], g_ref[...] * (1.0 + sc_ref[0])) + sh_ref[0]
        h_scr[rows, :] = h.astype(BF16)
        hb = h_scr[rows, :]

        u_ref[0, rows, :] = _dot(hb, w_ref[:, 0:POOL_WIDTH]).astype(BF16)

        def group_qkv(g):
            width = HEADS_PER_GROUP * HEAD_DIM
            sections = []
            for sec in range(3):
                lo = POOL_WIDTH + sec * N_ATT_HEADS * HEAD_DIM + g * width
                part_proj = _dot(hb, w_ref[:, lo:lo + width])
                sections.append(part_proj * Q_SCALE if sec == 0 else part_proj)
            return jnp.concatenate(sections, axis=1)

        col = POOL_WIDTH + 3 * GROUP_QKV
        qkv0_ref[0, 0, rows, :] = group_qkv(0).astype(BF16)
        for gi, (out_ref, d) in enumerate(((qkv1_ref, ATT_DILATIONS[1]), (qkv2_ref, ATT_DILATIONS[2]))):
            proj = group_qkv(gi + 1)
            stage = p_scr.at[part, gi]
            for cb in range(GROUP_QKV // LANES):
                stage[cb] = proj[:, cb * LANES:(cb + 1) * LANES]
            sub = hm // d
            for r in range(d):
                out_ref[0, r, part * sub:(part + 1) * sub, :] = jnp.concatenate(
                    [stage[cb, pl.ds(r, sub, stride=d), :] for cb in range(GROUP_QKV // LANES)],
                    axis=1).astype(BF16)

        chunk = 512
        for j in range(2 * D_MODEL // chunk):
            g = _dot(hb, w_ref[:, col + j * chunk:col + (j + 1) * chunk])
            gate_ref[0, rows, j * chunk:(j + 1) * chunk] = _sigmoid(g).astype(BF16)


def _inproj(x, g_pre, sc, sh, w_in):
    b, s, d = x.shape
    tm = ROW_TILE
    d1, d2 = ATT_DILATIONS[1], ATT_DILATIONS[2]
    grid = (b, s // tm)
    const = lambda bi, i: (0, 0)
    per_b = lambda bi, i: (bi, 0, 0)
    return pl.pallas_call(
        _inproj_kernel,
        grid=grid,
        in_specs=[pl.BlockSpec((1, tm, d), lambda bi, i: (bi, i, 0)),
                  pl.BlockSpec((1, d), const),
                  pl.BlockSpec((1, 1, d), per_b),
                  pl.BlockSpec((1, 1, d), per_b),
                  pl.BlockSpec((d, IN_WIDTH), const, pipeline_mode=pl.Buffered(1))],
        out_specs=[pl.BlockSpec((1, tm, POOL_WIDTH), lambda bi, i: (bi, i, 0)),
                   pl.BlockSpec((1, 1, tm, GROUP_QKV), lambda bi, i: (bi, 0, i, 0)),
                   pl.BlockSpec((1, d1, tm // d1, GROUP_QKV), lambda bi, i: (bi, 0, i, 0)),
                   pl.BlockSpec((1, d2, tm // d2, GROUP_QKV), lambda bi, i: (bi, 0, i, 0)),
                   pl.BlockSpec((1, tm, 2 * D_MODEL), lambda bi, i: (bi, i, 0))],
        out_shape=[jax.ShapeDtypeStruct((b, s, POOL_WIDTH), BF16),
                   jax.ShapeDtypeStruct((b, 1, s, GROUP_QKV), BF16),
                   jax.ShapeDtypeStruct((b, d1, s // d1, GROUP_QKV), BF16),
                   jax.ShapeDtypeStruct((b, d2, s // d2, GROUP_QKV), BF16),
                   jax.ShapeDtypeStruct((b, s, 2 * D_MODEL), BF16)],
        scratch_shapes=[pltpu.VMEM((tm, d), BF16),
                        pltpu.VMEM((ROW_SPLIT, 2, GROUP_QKV // LANES, tm // ROW_SPLIT, LANES), F32)],
        compiler_params=pltpu.CompilerParams(
            dimension_semantics=("arbitrary", "arbitrary"), vmem_limit_bytes=VMEM_LIMIT),
        name="inproj",
    )(x, g_pre, sc, sh, w_in)


def _attn_kernel(slopes_ref, q0, k0, v0, q1, k1, v1, q2, k2, v2, o_ref,
                 bias_scr, acc, mst, lst, s_even, s_odd):
    pair = pl.program_id(1)
    seq = o_ref.shape[1]
    nblk = seq // ATT_BLOCK
    lane = lax.broadcasted_iota(I32, (ATT_BLOCK, LANES), 1)
    first_head = lane < HEAD_DIM
    half_lane = lax.broadcasted_iota(I32, (ATT_BLOCK // 2, LANES), 1)
    head_bits = (jnp.where(half_lane < HEAD_DIM, -1, 0), jnp.where(half_lane < HEAD_DIM, 0, -1))
    ones_block = jnp.ones((2 * ATT_BLOCK, LANES), BF16)

    qi = lax.broadcasted_iota(I32, (ATT_BLOCK, 2 * ATT_BLOCK), 0)
    kj = lax.broadcasted_iota(I32, (ATT_BLOCK, 2 * ATT_BLOCK), 1)
    delta = ATT_BLOCK + qi - kj
    valid = (delta >= 0) & (delta <= ATT_REACH)
    delta0 = qi - kj
    valid0 = delta0 >= 0
    for g, d in enumerate(ATT_DILATIONS):
        for j in range(2):
            slope = slopes_ref[g * HEADS_PER_GROUP + 2 * pair + j]
            slope = slope * LOG2_E
            bias_scr[g, j, 0] = jnp.where(valid0, -slope * (delta0 * d).astype(F32), MASKED)
            bias_scr[g, j, 1] = jnp.where(valid, -slope * (delta * d).astype(F32), MASKED)

    n_iter = nblk // ATT_UNROLL

    def group_stages(g, q_ref, k_ref, v_ref):
        d = ATT_DILATIONS[g]
        per_res = nblk // d

        def block_index(it, k):
            n = it * ATT_UNROLL + k
            return n // per_res, n % per_res

        def key_rows(i):
            lo = jnp.maximum(i - 1, 0)
            return pl.ds(pl.multiple_of(lo * ATT_BLOCK, ATT_BLOCK), 2 * ATT_BLOCK)

        def scores(it, s_ref):
            for k in range(ATT_UNROLL):
                r, i = block_index(it, k)
                q = q_ref[0, r, pl.ds(pl.multiple_of(i * ATT_BLOCK, ATT_BLOCK), ATT_BLOCK), :]
                kw = k_ref[0, r, key_rows(i), :]
                qbits = pltpu.bitcast(q, I32)
                for j in range(2):
                    qh = pltpu.bitcast(qbits & head_bits[j], BF16)
                    s_ref[2 * k + j] = _dot_nt(qh, kw) + bias_scr[g, j, jnp.minimum(i, 1)]

        def weighted_values(it, s_ref):
            for k in range(ATT_UNROLL):
                r, i = block_index(it, k)
                vw = jnp.concatenate([v_ref[0, r, key_rows(i), :], ones_block], axis=1)
                outs = []
                for j in range(2):
                    sc = s_ref[2 * k + j]
                    mx = jnp.max(sc, axis=-1, keepdims=True)
                    p = jnp.exp2(sc - mx)
                    num_den = _dot(p.astype(BF16), vw)
                    outs.append((num_den[:, :LANES], mx, num_den[:, LANES:]))
                (n0, m0, l0), (n1, m1, l1) = outs
                if d == 1:
                    rows = pl.ds(pl.multiple_of(i * ATT_BLOCK, ATT_BLOCK), ATT_BLOCK)
                else:
                    rows = pl.ds(i * (ATT_BLOCK * d) + r, ATT_BLOCK, stride=d)
                acc[g, rows, :] = jnp.where(first_head, n0, n1)
                mst[g, rows, :] = jnp.where(first_head, m0, m1)
                lst[g, rows, :] = jnp.where(first_head, l0, l1)

        return scores, weighted_values

    stages = [group_stages(0, q0, k0, v0), group_stages(1, q1, k1, v1), group_stages(2, q2, k2, v2)]
    stages[0][0](jnp.int32(0), s_even)
    for g, (scores, weighted_values) in enumerate(stages):
        def pair_of_iterations(it, then_scores):
            scores(it + 1, s_odd)
            weighted_values(it, s_even)
            then_scores()
            weighted_values(it + 1, s_odd)

        def body(h, carry, pair_of_iterations=pair_of_iterations, scores=scores):
            it = 2 * h
            pair_of_iterations(it, lambda: scores(it + 2, s_even))
            return carry

        lax.fori_loop(0, n_iter // 2 - 1, body, 0)
        if g + 1 < len(stages):
            next_scores = stages[g + 1][0]
            pair_of_iterations(jnp.int32(n_iter - 2), lambda: next_scores(jnp.int32(0), s_even))
        else:
            pair_of_iterations(jnp.int32(n_iter - 2), lambda: None)

    def finish(it, carry):
        for k in range(ATT_UNROLL):
            rows = pl.ds(pl.multiple_of((it * ATT_UNROLL + k) * ATT_BLOCK, ATT_BLOCK), ATT_BLOCK)
            ms = [mst[g, rows, :] for g in range(3)]
            top = jnp.maximum(jnp.maximum(ms[0], ms[1]), ms[2])
            scale = [jnp.exp2(m - top) for m in ms]
            num = scale[0] * acc[0, rows, :] + scale[1] * acc[1, rows, :] + scale[2] * acc[2, rows, :]
            den = scale[0] * lst[0, rows, :] + scale[1] * lst[1, rows, :] + scale[2] * lst[2, rows, :]
            o_ref[0, rows, :] = (num / den).astype(BF16)
        return carry

    lax.fori_loop(0, nblk // ATT_UNROLL, finish, 0)


def _attention(qkv0, qkv1, qkv2, slopes):
    b, _, s, _ = qkv0.shape
    pairs = HEADS_PER_GROUP // 2
    col_blocks = HEADS_PER_GROUP * HEAD_DIM // LANES

    def specs(arr):
        _, d, sub, _ = arr.shape
        return [pl.BlockSpec((1, d, sub, LANES),
                             functools.partial(lambda bi, p, sec: (bi, 0, 0, sec * col_blocks + p), sec=sec))
                for sec in range(3)]

    return pl.pallas_call(
        _attn_kernel,
        grid=(b, pairs),
        in_specs=[pl.BlockSpec(memory_space=pltpu.SMEM)] + specs(qkv0) + specs(qkv1) + specs(qkv2),
        out_specs=pl.BlockSpec((1, s, LANES), lambda bi, p: (bi, 0, p)),
        out_shape=jax.ShapeDtypeStruct((b, s, ATT_OUT_WIDTH), BF16),
        scratch_shapes=[pltpu.VMEM((3, 2, 2, ATT_BLOCK, 2 * ATT_BLOCK), F32),
                        pltpu.VMEM((3, s, LANES), F32),
                        pltpu.VMEM((3, s, LANES), F32),
                        pltpu.VMEM((3, s, LANES), F32),
                        pltpu.VMEM((2 * ATT_UNROLL, ATT_BLOCK, 2 * ATT_BLOCK), F32),
                        pltpu.VMEM((2 * ATT_UNROLL, ATT_BLOCK, 2 * ATT_BLOCK), F32)],
        compiler_params=pltpu.CompilerParams(
            dimension_semantics=("arbitrary", "arbitrary"), vmem_limit_bytes=VMEM_LIMIT),
        name="attention",
    )(slopes, qkv0, qkv0, qkv0, qkv1, qkv1, qkv1, qkv2, qkv2, qkv2)


def _mixtail_kernel(u_ref, halo_ref, att_ref, gate_ref, x_ref,
                    gt_m_ref, sc_f_ref, sh_f_ref, g_post_ref, g_pre_ref,
                    wpg_ref, pscale_ref, wbp_ref, wba_ref, wout_ref, wr_ref, br_ref,
                    xmid_ref, route_ref, meta_ref, counts_ref, xs_hbm,
                    pu, lv, xbuf, zbuf, fill, meta_s, sem, sem_s):
    i = pl.program_id(1)
    tm = x_ref.shape[1]
    step = pl.program_id(0) * pl.num_programs(1) + i
    last = pl.num_programs(0) * pl.num_programs(1) - 1
    slot = step % 2
    region = _region_rows(pl.num_programs(0) * pl.num_programs(1) * tm)

    @pl.when(step == 0)
    def _():
        fill[...] = jnp.zeros_like(fill)
        pu[0:POOL_PAD - POOL_HALO, :] = jnp.zeros((POOL_PAD - POOL_HALO, POOL_WIDTH), F32)
        lv[:, :, 0:POOL_PAD - POOL_HALO, :] = jnp.zeros(
            (SORT_GROUPS, 2, POOL_PAD - POOL_HALO, POOL_GROUP_DIM), F32)
        for sub in range(SORT_GROUPS):
            for e in range(N_EXPERTS):
                meta_s[SORT_GROUPS + sub, 1, e] = 0
                meta_s[SORT_GROUPS + sub, 0, e] = 0

    def send_tile(which):
        for sub in range(SORT_GROUPS):
            m = which * SORT_GROUPS + sub
            _tile_run_copies(lambda e: meta_s[m, 1, e], xbuf.at[which, sub], None,
                             xs_hbm, lambda e: e * region + meta_s[m, 0, e], sem.at[which])

    def wait_tile(which):
        for sub in range(SORT_GROUPS):
            pltpu.make_async_copy(xbuf.at[which, sub], xs_hbm.at[pl.ds(0, SORT_ROWS * ROW_WORDS), :],
                                  sem.at[which]).wait()

    @pl.when(step > 1)
    def _():
        wait_tile(slot)

    send_tile(1 - slot)

    head = POOL_PAD - POOL_HALO
    halo = halo_ref[0].astype(F32)
    pu[head:POOL_PAD, :] = jnp.where(i > 0, halo, jnp.zeros_like(halo))
    pu[POOL_PAD:POOL_PAD + tm, :] = u_ref[0].astype(F32)

    ts = SORT_TOKENS

    def pool(sub, st):
        base = sub * ts
        t = i * tm + base + lax.broadcasted_iota(I32, (ts, 1), 0)
        pooled_groups = []
        for g, w in enumerate(POOL_WINDOWS):
            cols = slice(g * POOL_GROUP_DIM, (g + 1) * POOL_GROUP_DIM)
            read = lambda start, n: pu[pl.ds(base + start, n), cols]
            shift, level = 1, 0
            while 2 * shift < w:
                partial = read(head, ts + POOL_HALO) + read(head - shift, ts + POOL_HALO)
                buf = lv.at[sub, level % 2]
                buf[pl.ds(head, ts + POOL_HALO), :] = partial
                read = lambda start, n, buf=buf: buf[pl.ds(start, n), :]
                shift, level = 2 * shift, level + 1
            win = read(POOL_PAD, ts) + read(POOL_PAD - shift, ts)
            count = jnp.minimum(t + 1, w).astype(F32)
            pooled_groups.append((win / count - pu[pl.ds(base + POOL_PAD, ts), cols]).astype(BF16))
        st["pooled"] = jnp.concatenate(pooled_groups, axis=1)

    neg_inf = -jnp.inf
    far = float(LANES)
    before = (lax.broadcasted_iota(I32, (ts, ts), 0) < lax.broadcasted_iota(I32, (ts, ts), 1))
    earlier = jnp.where(before, 1.0, 0.0).astype(BF16)
    erow = lax.broadcasted_iota(I32, (N_EXPERTS, ts), 0).astype(F32)
    e_col = lax.broadcasted_iota(I32, (N_EXPERTS, LANES), 0)
    e_lane = lax.broadcasted_iota(I32, (N_EXPERTS, LANES), 1)
    row8 = lax.broadcasted_iota(I32, (SUBLANES, ts), 0)
    mrow = lax.broadcasted_iota(I32, (SUBLANES, LANES), 0)
    srow = lax.broadcasted_iota(I32, (SORT_ROWS, ts), 0).astype(F32)
    filled = fill[...]

    def rows_of(sub):
        return slice(sub * ts, (sub + 1) * ts)

    def branch_projections(sub, st):
        rows = rows_of(sub)
        mixed = _dot(st.pop("pooled"), wpg_ref[...]) * pscale_ref[...]
        st["y_pool"] = _dot(mixed.astype(BF16), wbp_ref[...])
        st["y_att"] = _dot(att_ref[0, rows, :], wba_ref[...])

    def gated_sum(sub, st):
        rows = rows_of(sub)
        st["merged"] = (gate_ref[0, rows, 0:D_MODEL] * st.pop("y_pool").astype(BF16)
                        + gate_ref[0, rows, D_MODEL:2 * D_MODEL] * st.pop("y_att").astype(BF16))

    def output_projection(sub, st):
        st["y"] = _dot(st.pop("merged"), wout_ref[...])

    def residual_and_ffn_input(sub, st):
        rows = rows_of(sub)
        x_mid = x_ref[0, rows, :] + _rmsnorm(st.pop("y"), gt_m_ref[0] * g_post_ref[...])
        xmid_ref[0, rows, :] = x_mid
        h2 = _rmsnorm(x_mid, g_pre_ref[...] * (1.0 + sc_f_ref[0])) + sh_f_ref[0]
        st["h2b"] = h2.astype(BF16)
        st["h2lo"] = (h2 - st["h2b"].astype(F32)).astype(BF16)

    def router_logits(sub, st):
        by_hi = _dot_nt(wr_ref[...], st["h2b"])
        st["logits"] = (by_hi[0:ROUTER_ROWS] + (by_hi[ROUTER_ROWS:2 * ROUTER_ROWS]
                        + _dot_nt(wr_ref[0:ROUTER_ROWS, :], st.pop("h2lo"))) + br_ref[...])

    def route(sub, st):
        logits = st.pop("logits")
        gl = logits[N_EXPERTS:N_EXPERTS + N_EXPERT_GROUPS, :]
        grow = lax.broadcasted_iota(I32, gl.shape, 0).astype(F32)
        gmax = jnp.max(gl, axis=0, keepdims=True)
        gsel = jnp.min(jnp.where(gl == gmax, grow, far), axis=0, keepdims=True)
        p_group = 1.0 / jnp.sum(jnp.exp(gl - gmax), axis=0, keepdims=True)
        e_lo = gsel * float(EXPERTS_PER_GROUP)
        el = jnp.where((erow >= e_lo) & (erow < e_lo + float(EXPERTS_PER_GROUP)),
                       logits[0:N_EXPERTS, :], neg_inf)
        v1 = jnp.max(el, axis=0, keepdims=True)
        i1 = jnp.min(jnp.where(el == v1, erow, far), axis=0, keepdims=True)
        el2 = jnp.where(erow == i1, neg_inf, el)
        v2 = jnp.max(el2, axis=0, keepdims=True)
        i2 = jnp.min(jnp.where(el2 == v2, erow, far), axis=0, keepdims=True)
        e21 = jnp.exp(v2 - v1)
        st["w1"] = p_group / (1.0 + e21)
        st["w2"] = p_group * e21 / (1.0 + e21)
        st["pick1"] = erow == i1
        st["pick2"] = erow == i2
        st["assign"] = jnp.where(st["pick1"] | st["pick2"], 1.0, 0.0)

    def count_and_rank(sub, st):
        assign = st.pop("assign")
        st["rank"] = _dot(assign.astype(BF16), earlier)
        assign_pad = jnp.concatenate([assign, jnp.zeros((LANES - N_EXPERTS, ts), F32)], axis=0).astype(BF16)
        st["cnt_row"] = _dot_nt(jnp.ones((SUBLANES, ts), BF16), assign_pad)

    def sorted_positions(sub, st):
        cnt_row = st["cnt_row"]
        run_start = jnp.sum(jnp.where(e_lane < e_col, cnt_row[0:1, :], 0.0), axis=1, keepdims=True)
        pos = st.pop("rank") + run_start
        key1 = jnp.sum(jnp.where(st.pop("pick1"), pos, 0.0), axis=0, keepdims=True)
        key2 = jnp.sum(jnp.where(st.pop("pick2"), pos, 0.0), axis=0, keepdims=True)
        route_ref[:, rows_of(sub)] = jnp.where(row8 == 0, key1, jnp.where(row8 == 1, key2,
                                               jnp.where(row8 == 2, st.pop("w1"),
                                                         jnp.where(row8 == 3, st.pop("w2"), 0.0))))
        st["perm"] = jnp.where((srow == key1) | (srow == key2), 1.0, 0.0).astype(BF16)

    def sort_rows(sub, st):
        perm, h2b = st.pop("perm"), st.pop("h2b")
        for w in range(ROW_WORDS):
            pair = _dot(perm, h2b[:, 2 * w * LANES:(2 * w + 2) * LANES])
            _pack_rows(xbuf.at[slot, sub], w, SORT_ROWS, pair[:, :LANES], pair[:, LANES:])

    state = _run_staggered((pool, branch_projections, gated_sum, output_projection,
                            residual_and_ffn_input, router_logits, route, count_and_rank,
                            sorted_positions, sort_rows), SORT_GROUPS, lag=0)

    for sub in range(SORT_GROUPS):
        cnt_row = state[sub]["cnt_row"]
        meta_ref[sub] = jnp.where(mrow == 0, filled, jnp.where(mrow == 1, cnt_row, 0.0)).astype(I32)
        filled = filled + cnt_row

    fill[...] = filled
    counts_ref[...] = filled.astype(I32)

    meta_copy = pltpu.make_async_copy(meta_ref, meta_s.at[pl.ds(slot * SORT_GROUPS, SORT_GROUPS)], sem_s)
    meta_copy.start()
    meta_copy.wait()

    @pl.when(step == last)
    def _():
        send_tile(slot)

        @pl.when(step > 0)
        def _():
            wait_tile(1 - slot)

        wait_tile(slot)
        zbuf[...] = jnp.zeros_like(zbuf)
        final = slot * SORT_GROUPS + SORT_GROUPS - 1

        def pad_copy(e):
            end = e * region + meta_s[final, 0, e] + meta_s[final, 1, e]
            return pltpu.make_async_copy(
                zbuf, xs_hbm.at[pl.ds(end * ROW_WORDS, MOE_BLOCK * ROW_WORDS), :], sem_s)

        def start_pad(e, carry):
            pad_copy(e).start()
            return carry

        def wait_pad(e, carry):
            pad_copy(e).wait()
            return carry

        lax.fori_loop(0, N_EXPERTS, start_pad, 0)
        lax.fori_loop(0, N_EXPERTS, wait_pad, 0)


def _mixtail(u, att, gates, x, gt_m, sc_f, sh_f, g_post, g_pre,
             wpg, pscale, wbp, wba, wout, wr, br):
    b, s, d = x.shape
    tm = ROW_TILE
    tiles = s // tm
    n_tiles = b * tiles
    halo_blocks = tm // POOL_HALO
    region = _region_rows(b * s)
    const2 = lambda bi, i: (0, 0)
    const3 = lambda bi, i: (0, 0, 0)
    per_b = lambda bi, i: (bi, 0, 0)
    tile = lambda bi, i: (bi, i, 0)
    single = dict(pipeline_mode=pl.Buffered(1))
    return pl.pallas_call(
        _mixtail_kernel,
        grid=(b, tiles),
        in_specs=[pl.BlockSpec((1, tm, POOL_WIDTH), tile),
                  pl.BlockSpec((1, POOL_HALO, POOL_WIDTH),
                               lambda bi, i: (bi, jnp.maximum(i * halo_blocks - 1, 0), 0)),
                  pl.BlockSpec((1, tm, ATT_OUT_WIDTH), tile),
                  pl.BlockSpec((1, tm, 2 * D_MODEL), tile),
                  pl.BlockSpec((1, tm, d), tile),
                  pl.BlockSpec((1, 1, d), per_b),
                  pl.BlockSpec((1, 1, d), per_b),
                  pl.BlockSpec((1, 1, d), per_b),
                  pl.BlockSpec((1, d), const2),
                  pl.BlockSpec((1, d), const2),
                  pl.BlockSpec(wpg.shape, const2, **single),
                  pl.BlockSpec((1, POOL_WIDTH), const2),
                  pl.BlockSpec(wbp.shape, const2, **single),
                  pl.BlockSpec(wba.shape, const2, **single),
                  pl.BlockSpec(wout.shape, const2, **single),
                  pl.BlockSpec(wr.shape, const2, **single),
                  pl.BlockSpec(br.shape, const2)],
        out_specs=[pl.BlockSpec((1, tm, d), tile),
                   pl.BlockSpec((SUBLANES, tm), lambda bi, i: (0, bi * tiles + i)),
                   pl.BlockSpec((SORT_GROUPS, SUBLANES, LANES), lambda bi, i: (bi * tiles + i, 0, 0)),
                   pl.BlockSpec((SUBLANES, LANES), const2),
                   pl.BlockSpec(memory_space=pl.ANY)],
        out_shape=[jax.ShapeDtypeStruct((b, s, d), F32),
                   jax.ShapeDtypeStruct((SUBLANES, b * s), F32),
                   jax.ShapeDtypeStruct((n_tiles * SORT_GROUPS, SUBLANES, LANES), I32),
                   jax.ShapeDtypeStruct((SUBLANES, LANES), I32),
                   jax.ShapeDtypeStruct((N_EXPERTS * region * ROW_WORDS, LANES), I32)],
        scratch_shapes=[pltpu.VMEM((POOL_PAD + tm, POOL_WIDTH), F32),
                        pltpu.VMEM((SORT_GROUPS, 2, POOL_PAD + SORT_TOKENS, POOL_GROUP_DIM), F32),
                        pltpu.VMEM((2, SORT_GROUPS, SORT_ROWS * ROW_WORDS, LANES), I32),
                        pltpu.VMEM((MOE_BLOCK * ROW_WORDS, LANES), I32),
                        pltpu.VMEM((SUBLANES, LANES), F32),
                        pltpu.SMEM((2 * SORT_GROUPS, SUBLANES, LANES), I32),
                        pltpu.SemaphoreType.DMA((2,)),
                        pltpu.SemaphoreType.DMA(())],
        compiler_params=pltpu.CompilerParams(
            dimension_semantics=("arbitrary", "arbitrary"), vmem_limit_bytes=VMEM_LIMIT),
        name="mixtail",
    )(u, u, att, gates, x, gt_m, sc_f, sh_f, g_post, g_pre,
      wpg, pscale, wbp, wba, wout, wr, br)


def _expert_kernel(counts_ref, xs_hbm, wg_ref, wu_ref, wd_ref, ys_hbm,
                   xbuf, ybuf, state, semx, semy):
    e = pl.program_id(0)
    bm = MOE_BLOCK
    nx, ny = EXPERT_X_BUFFERS, EXPERT_Y_BUFFERS
    block_words = bm * ROW_WORDS
    region = xs_hbm.shape[0] // (N_EXPERTS * ROW_WORDS)

    def n_blocks(ex):
        return (counts_ref[ex] + (bm - 1)) // bm

    def block_rows(ref, ex, k):
        start = pl.multiple_of((ex * region + k * bm) * ROW_WORDS, block_words)
        return ref.at[pl.ds(start, block_words), :]

    def x_copy(ex, k, s):
        return pltpu.make_async_copy(block_rows(xs_hbm, ex, k), xbuf.at[s], semx.at[s])

    def y_copy(k, s):
        return pltpu.make_async_copy(ybuf.at[s], block_rows(ys_hbm, e, k), semy.at[s])

    @pl.when(e == 0)
    def _():
        for j in range(4):
            state[j] = 0

    def fetch_through(target):
        def more(c):
            pe, _, pg = c
            return (pg < target) & (pe < N_EXPERTS)

        def step(c):
            pe, pk, pg = c
            has = pk < n_blocks(pe)

            @pl.when(has)
            def _():
                x_copy(pe, pk, pg % nx).start()

            return (jnp.where(has, pe, pe + 1), jnp.where(has, pk + 1, 0), pg + has.astype(I32))

        pe, pk, pg = lax.while_loop(more, step, (state[1], state[2], state[3]))
        state[1] = pe
        state[2] = pk
        state[3] = pg

    nb = n_blocks(e)
    done = state[0]

    def blocks(k0, count):
        ks = [k0 + c for c in range(count)]
        gs = [done + k for k in ks]
        fetch_through(gs[0] + nx)
        for k, g in zip(ks, gs):
            x_copy(e, k, g % nx).wait()

            @pl.when(g >= ny)
            def _():
                y_copy(k, g % ny).wait()

        mids = []
        for g in gs:
            x = _unpack_rows(xbuf.at[g % nx], bm).astype(BF16)
            a = _dot(x, wg_ref[...].astype(BF16))
            u = _dot(x, wu_ref[...].astype(BF16))
            mids.append(((a * _sigmoid(a)) * u).astype(BF16))
        for k, g, mid in zip(ks, gs, mids):
            for w in range(ROW_WORDS):
                pair = _bf16_exact(_dot(mid, wd_ref[:, 2 * w * LANES:(2 * w + 2) * LANES].astype(BF16)))
                _pack_rows(ybuf.at[g % ny], w, bm, pair[:, :LANES], pair[:, LANES:])
            y_copy(k, g % ny).start()

    per = EXPERT_CHAINS

    def several_blocks(p, carry):
        blocks(per * p, per)
        return carry

    lax.fori_loop(0, nb // per, several_blocks, 0)
    k_next = (nb // per) * per
    count = per // 2
    while count >= 1:
        take = ((nb - k_next) >= count)

        @pl.when(take)
        def _(k_next=k_next, count=count):
            blocks(k_next, count)

        k_next = k_next + jnp.where(take, count, 0)
        count //= 2

    state[0] = done + nb

    @pl.when(e == N_EXPERTS - 1)
    def _():
        total = done + nb
        for j in range(ny):
            @pl.when(total > j)
            def _():
                y_copy(0, (total - 1 - j) % ny).wait()


def _experts(xs, counts, w_gate, w_up, w_down):
    bm = MOE_BLOCK
    w_in = pl.BlockSpec((None, D_MODEL, D_EXPERT), lambda e, cnt: (e, 0, 0))
    w_out = pl.BlockSpec((None, D_EXPERT, D_MODEL), lambda e, cnt: (e, 0, 0))
    grid_spec = pltpu.PrefetchScalarGridSpec(
        num_scalar_prefetch=1,
        grid=(N_EXPERTS,),
        in_specs=[pl.BlockSpec(memory_space=pl.ANY), w_in, w_in, w_out],
        out_specs=pl.BlockSpec(memory_space=pl.ANY),
        scratch_shapes=[pltpu.VMEM((EXPERT_X_BUFFERS, bm * ROW_WORDS, LANES), I32),
                        pltpu.VMEM((EXPERT_Y_BUFFERS, bm * ROW_WORDS, LANES), I32),
                        pltpu.SMEM((4,), I32),
                        pltpu.SemaphoreType.DMA((EXPERT_X_BUFFERS,)),
                        pltpu.SemaphoreType.DMA((EXPERT_Y_BUFFERS,))],
    )
    return pl.pallas_call(
        _expert_kernel,
        grid_spec=grid_spec,
        out_shape=jax.ShapeDtypeStruct(xs.shape, I32),
        compiler_params=pltpu.CompilerParams(
            dimension_semantics=("arbitrary",), vmem_limit_bytes=VMEM_LIMIT),
        name="experts",
    )(counts, xs, w_gate, w_up, w_down)


def _combine_kernel(meta_ref, y_hbm, route_ref, xmid_ref, gt_ref, g_ref, o_ref, ybuf, rt_scr, sem):
    step = pl.program_id(0)
    n_steps = pl.num_programs(0)
    tm = ROW_TILE
    slot = step % COMBINE_BUFFERS
    region = _region_rows(n_steps * tm)

    def fetch_tile(tile, live):
        tile = jnp.minimum(tile, n_steps - 1)
        which = tile % COMBINE_BUFFERS
        for sub in range(SORT_GROUPS):
            m = (tile * SORT_GROUPS + sub) * 2
            _tile_run_copies(lambda e: jnp.where(live, meta_ref[(m + 1) * N_EXPERTS + e], 0),
                             y_hbm, lambda e: e * region + meta_ref[m * N_EXPERTS + e],
                             ybuf.at[which, sub], None, sem.at[which])

    @pl.when(step == 0)
    def _():
        for ahead in range(COMBINE_BUFFERS - 1):
            fetch_tile(ahead, ahead < n_steps)

    for sub in range(SORT_GROUPS):
        pltpu.make_async_copy(y_hbm.at[pl.ds(0, SORT_ROWS * ROW_WORDS), :], ybuf.at[slot, sub],
                              sem.at[slot]).wait()
    sorted_rows = [_unpack_rows(ybuf.at[slot, sub], SORT_ROWS).astype(BF16) for sub in range(SORT_GROUPS)]

    fetch_tile(step + COMBINE_BUFFERS - 1, step + COMBINE_BUFFERS - 1 < n_steps)

    rt_scr[...] = jnp.zeros_like(rt_scr)
    rt_scr[0:SUBLANES, :] = route_ref[...]
    cols = jnp.concatenate([rt_scr[:, c * LANES:(c + 1) * LANES].T for c in range(tm // LANES)], axis=0)

    ts = SORT_TOKENS
    spos = lax.broadcasted_iota(I32, (ts, SORT_ROWS), 1).astype(F32)
    def rows_of(sub):
        return slice(sub * ts, (sub + 1) * ts)

    def gate_matrix(sub, st):
        key1, key2, w1, w2 = (cols[rows_of(sub), c:c + 1] for c in range(4))
        st["gates"] = jnp.where(spos == key1, w1, jnp.where(spos == key2, w2, 0.0)).astype(BF16)

    def weighted_unsort(sub, st):
        st["y"] = _dot(st.pop("gates"), sorted_rows[sub])

    def norm_and_add(sub, st):
        rows = rows_of(sub)
        o_ref[rows, :] = xmid_ref[rows, :] + _rmsnorm(st.pop("y"), gt_ref[0] * g_ref[...])

    _run_staggered((gate_matrix, weighted_unsort, norm_and_add), SORT_GROUPS, lag=1)


def _combine(meta, ys, route, x_mid, gt_f, g_post, seq):
    t, d = x_mid.shape
    tm = ROW_TILE
    tiles_per_seq = seq // tm
    grid_spec = pltpu.PrefetchScalarGridSpec(
        num_scalar_prefetch=1,
        grid=(t // tm,),
        in_specs=[pl.BlockSpec(memory_space=pl.ANY),
                  pl.BlockSpec((SUBLANES, tm), lambda i, m: (0, i)),
                  pl.BlockSpec((tm, d), lambda i, m: (i, 0)),
                  pl.BlockSpec((1, 1, d), lambda i, m: (i // tiles_per_seq, 0, 0)),
                  pl.BlockSpec((1, d), lambda i, m: (0, 0))],
        out_specs=pl.BlockSpec((tm, d), lambda i, m: (i, 0)),
        scratch_shapes=[pltpu.VMEM((COMBINE_BUFFERS, SORT_GROUPS, SORT_ROWS * ROW_WORDS, LANES), I32),
                        pltpu.VMEM((LANES, tm), F32),
                        pltpu.SemaphoreType.DMA((COMBINE_BUFFERS,))],
    )
    return pl.pallas_call(
        _combine_kernel,
        grid_spec=grid_spec,
        out_shape=jax.ShapeDtypeStruct((t, d), F32),
        compiler_params=pltpu.CompilerParams(
            dimension_semantics=("arbitrary",), vmem_limit_bytes=VMEM_LIMIT),
        name="combine",
    )(meta, ys, route, x_mid, gt_f, g_post)


def kernel(x, c, w_ada, b_ada, g_pre_mix, g_post_mix, g_pre_ffn, g_post_ffn, w_in, w_pool_group, pool_scale, w_branch_pool, w_branch_att, w_out, w_group_router, b_group_router, w_expert_router, b_expert_router, w_exp_gate, w_exp_up, w_exp_down):
    b, s, d = x.shape
    assert d == D_MODEL and s % (ATT_BLOCK * 2 * ATT_DILATIONS[2]) == 0 and s % ROW_TILE == 0
    assert (b * s) % MOE_BLOCK == 0
    depth = w_ada.shape[0]
    slopes = jnp.exp2(-ALIBI_MAX_BIAS * jnp.arange(1, N_ATT_HEADS + 1, dtype=F32) / N_ATT_HEADS)

    for layer in range(depth):
        mod = _adaln(c, w_ada[layer], b_ada[layer]).reshape(b, 6, 1, d)
        sh_m, sc_m, gt_m, sh_f, sc_f, gt_f = [mod[:, j] for j in range(6)]

        u, qkv0, qkv1, qkv2, gates = _inproj(x, g_pre_mix[layer].reshape(1, d), sc_m, sh_m,
                                             w_in[layer].astype(BF16))
        att = _attention(qkv0, qkv1, qkv2, slopes)

        zero_block = jnp.zeros((POOL_GROUP_DIM, POOL_GROUP_DIM), F32)
        wpg_diag = jnp.block([[w_pool_group[layer, g] if g == h else zero_block
                               for h in range(len(POOL_WINDOWS))] for g in range(len(POOL_WINDOWS))])

        pad_rows = ROUTER_ROWS - N_EXPERTS - N_EXPERT_GROUPS
        wrt = jnp.concatenate([w_expert_router[layer].T, w_group_router[layer].T,
                               jnp.zeros((pad_rows, d), F32)], axis=0)
        wrt_hi = wrt.astype(BF16)
        wr = jnp.concatenate([wrt_hi, (wrt - wrt_hi.astype(F32)).astype(BF16)], axis=0)
        br = jnp.concatenate([b_expert_router[layer], b_group_router[layer],
                              jnp.zeros((pad_rows,), F32)]).reshape(ROUTER_ROWS, 1)

        x_mid, route, meta, counts, xs = _mixtail(
            u, att, gates, x, gt_m, sc_f, sh_f,
            g_post_mix[layer].reshape(1, d), g_pre_ffn[layer].reshape(1, d),
            wpg_diag.astype(BF16), pool_scale[layer].reshape(1, POOL_WIDTH),
            w_branch_pool[layer].astype(BF16), w_branch_att[layer].astype(BF16),
            w_out[layer].astype(BF16), wr, br)

        ys = _experts(xs, counts[0, :N_EXPERTS], w_exp_gate[layer], w_exp_up[layer], w_exp_down[layer])
        run_meta = meta[:, 0:2, 0:N_EXPERTS].reshape(-1)
        x = _combine(run_meta, ys, route, x_mid.reshape(b * s, d), gt_f,
                     g_post_ffn[layer].reshape(1, d), s).reshape(b, s, d)
    return x
```

```python
import functools

import jax
import jax.numpy as jnp
from jax import lax
from jax.experimental import pallas as pl
from jax.experimental.pallas import tpu as pltpu

F32 = jnp.float32
BF16 = jnp.bfloat16
I32 = jnp.int32

D_MODEL = 1024
LANES = 128
SUBLANES = 8
ROW_WORDS = D_MODEL // (2 * LANES)
HI_HALF = -65536

POOL_WINDOWS = (2, 4, 8, 16)
POOL_GROUP_DIM = 128
POOL_WIDTH = 512
POOL_HALO = 16
POOL_PAD = POOL_HALO + 8

HEAD_DIM = 64
ATT_DILATIONS = (1, 4, 16)
ATT_REACH = 128
ATT_BLOCK = 128
ATT_UNROLL = 16
HEADS_PER_GROUP = 4
N_ATT_HEADS = 12
GROUP_QKV = 3 * HEADS_PER_GROUP * HEAD_DIM
ATT_OUT_WIDTH = 256
ALIBI_MAX_BIAS = 8.0
IN_WIDTH = POOL_WIDTH + 3 * GROUP_QKV + 2 * D_MODEL
MASKED = -1e30
LOG2_E = 1.4426950408889634
Q_SCALE = HEAD_DIM ** -0.5 * LOG2_E

N_EXPERT_GROUPS = 4
EXPERTS_PER_GROUP = 8
N_EXPERTS = 32
ROUTER_ROWS = 48
D_EXPERT = 512
RMS_EPS = 1e-6

ROW_TILE = 1024
ROW_SPLIT = 4
SORT_TOKENS = 256
SORT_GROUPS = ROW_TILE // SORT_TOKENS
SORT_ROWS = 2 * SORT_TOKENS
MOE_BLOCK = 256
EXPERT_CHAINS = 4
EXPERT_X_BUFFERS = 10
EXPERT_Y_BUFFERS = 8
COMBINE_BUFFERS = 3
VMEM_LIMIT = 52 * 1024 * 1024


def _sigmoid(x):
    return 0.5 * jnp.tanh(0.5 * x) + 0.5


def _rmsnorm(x, g):
    return x * lax.rsqrt(jnp.mean(x * x, axis=-1, keepdims=True) + RMS_EPS) * g


def _dot(a, b):
    return jnp.dot(a, b, preferred_element_type=F32)


def _dot_nt(a, b):
    return lax.dot_general(a, b, (((1,), (1,)), ((), ())), preferred_element_type=F32)


def _pack_rows(ref, w, n, lo, hi):
    word = (lax.shift_right_logical(pltpu.bitcast(lo, I32), 16) | (pltpu.bitcast(hi, I32) & HI_HALF))
    ref[pl.ds(w, n, stride=ROW_WORDS), :] = word


def _unpack_rows(ref, n):
    cols = []
    for w in range(ROW_WORDS):
        word = ref[pl.ds(w, n, stride=ROW_WORDS), :]
        cols += [pltpu.bitcast(word << 16, F32), pltpu.bitcast(word & HI_HALF, F32)]
    return jnp.concatenate(cols, axis=1)


def _bf16_exact(x):
    return x.astype(BF16).astype(F32)


def _run_staggered(stages, n_chains, lag):
    state = [{} for _ in range(n_chains)]
    for tick in range(len(stages) + lag * (n_chains - 1)):
        for chain in range(n_chains):
            if 0 <= tick - lag * chain < len(stages):
                stages[tick - lag * chain](chain, state[chain])
    return state


def _region_rows(n_tokens):
    return n_tokens + MOE_BLOCK


def _run_copies(n, src, src_row, dst, dst_row, sem):
    @pl.when(n > 0)
    def _():
        pltpu.make_async_copy(
            src.at[pl.ds(src_row * ROW_WORDS, n * ROW_WORDS), :],
            dst.at[pl.ds(dst_row * ROW_WORDS, n * ROW_WORDS), :], sem).start()


def _tile_run_copies(counts, src, src_rows, dst, dst_rows, sem):
    local = 0
    for e in range(N_EXPERTS):
        n = counts(e)
        _run_copies(n, src, local if src_rows is None else src_rows(e),
                    dst, local if dst_rows is None else dst_rows(e), sem)
        local = local + n


def _adaln_kernel(c_ref, w_ref, b_ref, o_ref):
    c = c_ref[...]
    a = c * _sigmoid(c)
    w = w_ref[...]
    a_hi, w_hi = a.astype(BF16), w.astype(BF16)
    a_lo = (a - a_hi.astype(F32)).astype(BF16)
    w_lo = (w - w_hi.astype(F32)).astype(BF16)
    o_ref[...] = _dot(a_hi, w_hi) + (_dot(a_hi, w_lo) + _dot(a_lo, w_hi)) + b_ref[...]


def _adaln(c, w_ada, b_ada):
    b, d = c.shape
    n = w_ada.shape[1]
    rows = -(-b // SUBLANES) * SUBLANES
    cp = jnp.pad(c, ((0, rows - b), (0, 0)))
    nt = 1536
    out = pl.pallas_call(
        _adaln_kernel,
        grid=(n // nt,),
        in_specs=[pl.BlockSpec((rows, d), lambda j: (0, 0)),
                  pl.BlockSpec((d, nt), lambda j: (0, j)),
                  pl.BlockSpec((1, nt), lambda j: (0, j))],
        out_specs=pl.BlockSpec((rows, nt), lambda j: (0, j)),
        out_shape=jax.ShapeDtypeStruct((rows, n), F32),
        compiler_params=pltpu.CompilerParams(vmem_limit_bytes=VMEM_LIMIT),
        name="adaln",
    )(cp, w_ada, b_ada.reshape(1, n))
    return out[:b]


def _inproj_kernel(x_ref, g_ref, sc_ref, sh_ref, w_ref,
                   u_ref, qkv0_ref, qkv1_ref, qkv2_ref, gate_ref, h_scr, p_scr, pu, lv, tail):
    tm = x_ref.shape[1]
    hm = tm // ROW_SPLIT

    @pl.when((pl.program_id(0) == 0) & (pl.program_id(1) == 0))
    def _():
        pu[0:POOL_PAD - POOL_HALO, :] = jnp.zeros((POOL_PAD - POOL_HALO, POOL_WIDTH), F32)
        lv[:, 0:POOL_PAD - POOL_HALO, :] = jnp.zeros((2, POOL_PAD - POOL_HALO, POOL_GROUP_DIM), F32)
        tail[...] = jnp.zeros_like(tail)
    for part in range(ROW_SPLIT):
        rows = slice(part * hm, (part + 1) * hm)
        h = _rmsnorm(x_ref[0, rows, :], g_ref[...] * (1.0 + sc_ref[0])) + sh_ref[0]
        h_scr[rows, :] = h.astype(BF16)
        hb = h_scr[rows, :]

        head = POOL_PAD - POOL_HALO
        pu[head:POOL_PAD, :] = tail[...] if part else jnp.where(pl.program_id(1) > 0, tail[...], 0.0)
        pu[POOL_PAD:POOL_PAD + hm, :] = _dot(hb, w_ref[:, 0:POOL_WIDTH])
        tail[...] = pu[POOL_PAD + hm - POOL_HALO:POOL_PAD + hm, :]
        t = pl.program_id(1) * tm + part * hm + lax.broadcasted_iota(I32, (hm, 1), 0)
        for g, w in enumerate(POOL_WINDOWS):
            cols = slice(g * POOL_GROUP_DIM, (g + 1) * POOL_GROUP_DIM)
            read = lambda start, n: pu[pl.ds(start, n), cols]
            shift, level = 1, 0
            while 2 * shift < w:
                partial = read(head, hm + POOL_HALO) + read(head - shift, hm + POOL_HALO)
                buf = lv.at[level % 2]
                buf[pl.ds(head, hm + POOL_HALO), :] = partial
                read = lambda start, n, buf=buf: buf[pl.ds(start, n), :]
                shift, level = 2 * shift, level + 1
            win = read(POOL_PAD, hm) + read(POOL_PAD - shift, hm)
            count = jnp.minimum(t + 1, w).astype(F32)
            u_ref[0, rows, cols] = (win / count - pu[pl.ds(POOL_PAD, hm), cols]).astype(BF16)

        def group_qkv(g):
            width = HEADS_PER_GROUP * HEAD_DIM
            sections = []
            for sec in range(3):
                lo = POOL_WIDTH + sec * N_ATT_HEADS * HEAD_DIM + g * width
                part_proj = _dot(hb, w_ref[:, lo:lo + width])
                sections.append(part_proj * Q_SCALE if sec == 0 else part_proj)
            return jnp.concatenate(sections, axis=1)

        col = POOL_WIDTH + 3 * GROUP_QKV
        qkv0_ref[0, 0, rows, :] = group_qkv(0).astype(BF16)
        for gi, (out_ref, d) in enumerate(((qkv1_ref, ATT_DILATIONS[1]), (qkv2_ref, ATT_DILATIONS[2]))):
            proj = group_qkv(gi + 1)
            stage = p_scr.at[part, gi]
            for cb in range(GROUP_QKV // LANES):
                stage[cb] = proj[:, cb * LANES:(cb + 1) * LANES]
            sub = hm // d
            for r in range(d):
                out_ref[0, r, part * sub:(part + 1) * sub, :] = jnp.concatenate(
                    [stage[cb, pl.ds(r, sub, stride=d), :] for cb in range(GROUP_QKV // LANES)],
                    axis=1).astype(BF16)

        chunk = 512
        for j in range(2 * D_MODEL // chunk):
            g = _dot(hb, w_ref[:, col + j * chunk:col + (j + 1) * chunk])
            gate_ref[0, rows, j * chunk:(j + 1) * chunk] = _sigmoid(g).astype(BF16)


def _inproj(x, g_pre, sc, sh, w_in):
    b, s, d = x.shape
    tm = ROW_TILE
    d1, d2 = ATT_DILATIONS[1], ATT_DILATIONS[2]
    grid = (b, s // tm)
    const = lambda bi, i: (0, 0)
    per_b = lambda bi, i: (bi, 0, 0)
    return pl.pallas_call(
        _inproj_kernel,
        grid=grid,
        in_specs=[pl.BlockSpec((1, tm, d), lambda bi, i: (bi, i, 0)),
                  pl.BlockSpec((1, d), const),
                  pl.BlockSpec((1, 1, d), per_b),
                  pl.BlockSpec((1, 1, d), per_b),
                  pl.BlockSpec((d, IN_WIDTH), const, pipeline_mode=pl.Buffered(1))],
        out_specs=[pl.BlockSpec((1, tm, POOL_WIDTH), lambda bi, i: (bi, i, 0)),
                   pl.BlockSpec((1, 1, tm, GROUP_QKV), lambda bi, i: (bi, 0, i, 0)),
                   pl.BlockSpec((1, d1, tm // d1, GROUP_QKV), lambda bi, i: (bi, 0, i, 0)),
                   pl.BlockSpec((1, d2, tm // d2, GROUP_QKV), lambda bi, i: (bi, 0, i, 0)),
                   pl.BlockSpec((1, tm, 2 * D_MODEL), lambda bi, i: (bi, i, 0))],
        out_shape=[jax.ShapeDtypeStruct((b, s, POOL_WIDTH), BF16),
                   jax.ShapeDtypeStruct((b, 1, s, GROUP_QKV), BF16),
                   jax.ShapeDtypeStruct((b, d1, s // d1, GROUP_QKV), BF16),
                   jax.ShapeDtypeStruct((b, d2, s // d2, GROUP_QKV), BF16),
                   jax.ShapeDtypeStruct((b, s, 2 * D_MODEL), BF16)],
        scratch_shapes=[pltpu.VMEM((tm, d), BF16),
                        pltpu.VMEM((ROW_SPLIT, 2, GROUP_QKV // LANES, tm // ROW_SPLIT, LANES), F32),
                        pltpu.VMEM((POOL_PAD + tm // ROW_SPLIT, POOL_WIDTH), F32),
                        pltpu.VMEM((2, POOL_PAD + tm // ROW_SPLIT, POOL_GROUP_DIM), F32),
                        pltpu.VMEM((POOL_HALO, POOL_WIDTH), F32)],
        compiler_params=pltpu.CompilerParams(
            dimension_semantics=("arbitrary", "arbitrary"), vmem_limit_bytes=VMEM_LIMIT),
        name="inproj",
    )(x, g_pre, sc, sh, w_in)


def _attn_kernel(slopes_ref, q0, k0, v0, q1, k1, v1, q2, k2, v2, o_ref,
                 bias_scr, acc, mst, lst, s_even, s_odd):
    pair = pl.program_id(1)
    seq = o_ref.shape[1]
    nblk = seq // ATT_BLOCK
    lane = lax.broadcasted_iota(I32, (ATT_BLOCK, LANES), 1)
    first_head = lane < HEAD_DIM
    half_lane = lax.broadcasted_iota(I32, (ATT_BLOCK // 2, LANES), 1)
    head_bits = (jnp.where(half_lane < HEAD_DIM, -1, 0), jnp.where(half_lane < HEAD_DIM, 0, -1))
    ones_block = jnp.ones((2 * ATT_BLOCK, LANES), BF16)

    qi = lax.broadcasted_iota(I32, (ATT_BLOCK, 2 * ATT_BLOCK), 0)
    kj = lax.broadcasted_iota(I32, (ATT_BLOCK, 2 * ATT_BLOCK), 1)
    delta = ATT_BLOCK + qi - kj
    valid = (delta >= 0) & (delta <= ATT_REACH)
    delta0 = qi - kj
    valid0 = delta0 >= 0
    for g, d in enumerate(ATT_DILATIONS):
        for j in range(2):
            slope = slopes_ref[g * HEADS_PER_GROUP + 2 * pair + j]
            slope = slope * LOG2_E
            bias_scr[g, j, 0] = jnp.where(valid0, -slope * (delta0 * d).astype(F32), MASKED)
            bias_scr[g, j, 1] = jnp.where(valid, -slope * (delta * d).astype(F32), MASKED)

    n_iter = nblk // ATT_UNROLL

    def group_stages(g, q_ref, k_ref, v_ref):
        d = ATT_DILATIONS[g]
        per_res = nblk // d

        def block_index(it, k):
            n = it * ATT_UNROLL + k
            return n // per_res, n % per_res

        def key_rows(i):
            lo = jnp.maximum(i - 1, 0)
            return pl.ds(pl.multiple_of(lo * ATT_BLOCK, ATT_BLOCK), 2 * ATT_BLOCK)

        def scores(it, s_ref):
            for k in range(ATT_UNROLL):
                r, i = block_index(it, k)
                q = q_ref[0, r, pl.ds(pl.multiple_of(i * ATT_BLOCK, ATT_BLOCK), ATT_BLOCK), :]
                kw = k_ref[0, r, key_rows(i), :]
                qbits = pltpu.bitcast(q, I32)
                for j in range(2):
                    qh = pltpu.bitcast(qbits & head_bits[j], BF16)
                    s_ref[2 * k + j] = _dot_nt(qh, kw) + bias_scr[g, j, jnp.minimum(i, 1)]

        def weighted_values(it, s_ref):
            for k in range(ATT_UNROLL):
                r, i = block_index(it, k)
                vw = jnp.concatenate([v_ref[0, r, key_rows(i), :], ones_block], axis=1)
                outs = []
                for j in range(2):
                    sc = s_ref[2 * k + j]
                    mx = jnp.max(sc, axis=-1, keepdims=True)
                    p = jnp.exp2(sc - mx)
                    num_den = _dot(p.astype(BF16), vw)
                    outs.append((num_den[:, :LANES], mx, num_den[:, LANES:]))
                (n0, m0, l0), (n1, m1, l1) = outs
                if d == 1:
                    rows = pl.ds(pl.multiple_of(i * ATT_BLOCK, ATT_BLOCK), ATT_BLOCK)
                else:
                    rows = pl.ds(i * (ATT_BLOCK * d) + r, ATT_BLOCK, stride=d)
                acc[g, rows, :] = jnp.where(first_head, n0, n1)
                mst[g, rows, :] = jnp.where(first_head, m0, m1)
                lst[g, rows, :] = jnp.where(first_head, l0, l1)

        return scores, weighted_values

    stages = [group_stages(0, q0, k0, v0), group_stages(1, q1, k1, v1), group_stages(2, q2, k2, v2)]
    stages[0][0](jnp.int32(0), s_even)
    for g, (scores, weighted_values) in enumerate(stages):
        def pair_of_iterations(it, then_scores):
            scores(it + 1, s_odd)
            weighted_values(it, s_even)
            then_scores()
            weighted_values(it + 1, s_odd)

        def body(h, carry, pair_of_iterations=pair_of_iterations, scores=scores):
            it = 2 * h
            pair_of_iterations(it, lambda: scores(it + 2, s_even))
            return carry

        lax.fori_loop(0, n_iter // 2 - 1, body, 0)
        if g + 1 < len(stages):
            next_scores = stages[g + 1][0]
            pair_of_iterations(jnp.int32(n_iter - 2), lambda: next_scores(jnp.int32(0), s_even))
        else:
            pair_of_iterations(jnp.int32(n_iter - 2), lambda: None)

    def finish(it, carry):
        for k in range(ATT_UNROLL):
            rows = pl.ds(pl.multiple_of((it * ATT_UNROLL + k) * ATT_BLOCK, ATT_BLOCK), ATT_BLOCK)
            ms = [mst[g, rows, :] for g in range(3)]
            top = jnp.maximum(jnp.maximum(ms[0], ms[1]), ms[2])
            scale = [jnp.exp2(m - top) for m in ms]
            num = scale[0] * acc[0, rows, :] + scale[1] * acc[1, rows, :] + scale[2] * acc[2, rows, :]
            den = scale[0] * lst[0, rows, :] + scale[1] * lst[1, rows, :] + scale[2] * lst[2, rows, :]
            o_ref[0, rows, :] = (num / den).astype(BF16)
        return carry

    lax.fori_loop(0, nblk // ATT_UNROLL, finish, 0)


def _attention(qkv0, qkv1, qkv2, slopes):
    b, _, s, _ = qkv0.shape
    pairs = HEADS_PER_GROUP // 2
    col_blocks = HEADS_PER_GROUP * HEAD_DIM // LANES

    def specs(arr):
        _, d, sub, _ = arr.shape
        return [pl.BlockSpec((1, d, sub, LANES),
                             functools.partial(lambda bi, p, sec: (bi, 0, 0, sec * col_blocks + p), sec=sec))
                for sec in range(3)]

    return pl.pallas_call(
        _attn_kernel,
        grid=(b, pairs),
        in_specs=[pl.BlockSpec(memory_space=pltpu.SMEM)] + specs(qkv0) + specs(qkv1) + specs(qkv2),
        out_specs=pl.BlockSpec((1, s, LANES), lambda bi, p: (bi, 0, p)),
        out_shape=jax.ShapeDtypeStruct((b, s, ATT_OUT_WIDTH), BF16),
        scratch_shapes=[pltpu.VMEM((3, 2, 2, ATT_BLOCK, 2 * ATT_BLOCK), F32),
                        pltpu.VMEM((3, s, LANES), F32),
                        pltpu.VMEM((3, s, LANES), F32),
                        pltpu.VMEM((3, s, LANES), F32),
                        pltpu.VMEM((2 * ATT_UNROLL, ATT_BLOCK, 2 * ATT_BLOCK), F32),
                        pltpu.VMEM((2 * ATT_UNROLL, ATT_BLOCK, 2 * ATT_BLOCK), F32)],
        compiler_params=pltpu.CompilerParams(
            dimension_semantics=("arbitrary", "arbitrary"), vmem_limit_bytes=VMEM_LIMIT),
        name="attention",
    )(slopes, qkv0, qkv0, qkv0, qkv1, qkv1, qkv1, qkv2, qkv2, qkv2)


def _mixtail_kernel(u_ref, halo_ref, att_ref, gate_ref, x_ref,
                    gt_m_ref, sc_f_ref, sh_f_ref, g_post_ref, g_pre_ref,
                    wpg_ref, pscale_ref, wbp_ref, wba_ref, wout_ref, wr_ref, br_ref,
                    xmid_ref, route_ref, meta_ref, counts_ref, xs_hbm,
                    pu, lv, xbuf, zbuf, fill, meta_s, sem, sem_s):
    i = pl.program_id(1)
    tm = x_ref.shape[1]
    step = pl.program_id(0) * pl.num_programs(1) + i
    last = pl.num_programs(0) * pl.num_programs(1) - 1
    slot = step % 2
    region = _region_rows(pl.num_programs(0) * pl.num_programs(1) * tm)

    @pl.when(step == 0)
    def _():
        fill[...] = jnp.zeros_like(fill)
        pu[0:POOL_PAD - POOL_HALO, :] = jnp.zeros((POOL_PAD - POOL_HALO, POOL_WIDTH), F32)
        lv[:, :, 0:POOL_PAD - POOL_HALO, :] = jnp.zeros(
            (SORT_GROUPS, 2, POOL_PAD - POOL_HALO, POOL_GROUP_DIM), F32)
        for sub in range(SORT_GROUPS):
            for e in range(N_EXPERTS):
                meta_s[SORT_GROUPS + sub, 1, e] = 0
                meta_s[SORT_GROUPS + sub, 0, e] = 0

    def send_tile(which):
        for sub in range(SORT_GROUPS):
            m = which * SORT_GROUPS + sub
            _tile_run_copies(lambda e: meta_s[m, 1, e], xbuf.at[which, sub], None,
                             xs_hbm, lambda e: e * region + meta_s[m, 0, e], sem.at[which])

    def wait_tile(which):
        for sub in range(SORT_GROUPS):
            pltpu.make_async_copy(xbuf.at[which, sub], xs_hbm.at[pl.ds(0, SORT_ROWS * ROW_WORDS), :],
                                  sem.at[which]).wait()

    @pl.when(step > 1)
    def _():
        wait_tile(slot)

    send_tile(1 - slot)

    ts = SORT_TOKENS

    def pool(sub, st):
        st["pooled"] = u_ref[0, sub * ts:(sub + 1) * ts, :]

    neg_inf = -jnp.inf
    far = float(LANES)
    before = (lax.broadcasted_iota(I32, (ts, ts), 0) < lax.broadcasted_iota(I32, (ts, ts), 1))
    earlier = jnp.where(before, 1.0, 0.0).astype(BF16)
    erow = lax.broadcasted_iota(I32, (N_EXPERTS, ts), 0).astype(F32)
    e_col = lax.broadcasted_iota(I32, (N_EXPERTS, LANES), 0)
    e_lane = lax.broadcasted_iota(I32, (N_EXPERTS, LANES), 1)
    row8 = lax.broadcasted_iota(I32, (SUBLANES, ts), 0)
    mrow = lax.broadcasted_iota(I32, (SUBLANES, LANES), 0)
    srow = lax.broadcasted_iota(I32, (SORT_ROWS, ts), 0).astype(F32)
    filled = fill[...]

    def rows_of(sub):
        return slice(sub * ts, (sub + 1) * ts)

    def branch_projections(sub, st):
        rows = rows_of(sub)
        mixed = _dot(st.pop("pooled"), wpg_ref[...]) * pscale_ref[...]
        st["y_pool"] = _dot(mixed.astype(BF16), wbp_ref[...])
        st["y_att"] = _dot(att_ref[0, rows, :], wba_ref[...])

    def gated_sum(sub, st):
        rows = rows_of(sub)
        st["merged"] = (gate_ref[0, rows, 0:D_MODEL] * st.pop("y_pool").astype(BF16)
                        + gate_ref[0, rows, D_MODEL:2 * D_MODEL] * st.pop("y_att").astype(BF16))

    def output_projection(sub, st):
        st["y"] = _dot(st.pop("merged"), wout_ref[...])

    def residual_and_ffn_input(sub, st):
        rows = rows_of(sub)
        x_mid = x_ref[0, rows, :] + _rmsnorm(st.pop("y"), gt_m_ref[0] * g_post_ref[...])
        xmid_ref[0, rows, :] = x_mid
        h2 = _rmsnorm(x_mid, g_pre_ref[...] * (1.0 + sc_f_ref[0])) + sh_f_ref[0]
        st["h2b"] = h2.astype(BF16)
        st["h2lo"] = (h2 - st["h2b"].astype(F32)).astype(BF16)

    def router_logits(sub, st):
        by_hi = _dot_nt(wr_ref[...], st["h2b"])
        st["logits"] = (by_hi[0:ROUTER_ROWS] + (by_hi[ROUTER_ROWS:2 * ROUTER_ROWS]
                        + _dot_nt(wr_ref[0:ROUTER_ROWS, :], st.pop("h2lo"))) + br_ref[...])

    def route(sub, st):
        logits = st.pop("logits")
        gl = logits[N_EXPERTS:N_EXPERTS + N_EXPERT_GROUPS, :]
        grow = lax.broadcasted_iota(I32, gl.shape, 0).astype(F32)
        gmax = jnp.max(gl, axis=0, keepdims=True)
        gsel = jnp.min(jnp.where(gl == gmax, grow, far), axis=0, keepdims=True)
        p_group = 1.0 / jnp.sum(jnp.exp(gl - gmax), axis=0, keepdims=True)
        e_lo = gsel * float(EXPERTS_PER_GROUP)
        el = jnp.where((erow >= e_lo) & (erow < e_lo + float(EXPERTS_PER_GROUP)),
                       logits[0:N_EXPERTS, :], neg_inf)
        v1 = jnp.max(el, axis=0, keepdims=True)
        i1 = jnp.min(jnp.where(el == v1, erow, far), axis=0, keepdims=True)
        el2 = jnp.where(erow == i1, neg_inf, el)
        v2 = jnp.max(el2, axis=0, keepdims=True)
        i2 = jnp.min(jnp.where(el2 == v2, erow, far), axis=0, keepdims=True)
        e21 = jnp.exp(v2 - v1)
        st["w1"] = p_group / (1.0 + e21)
        st["w2"] = p_group * e21 / (1.0 + e21)
        st["pick1"] = erow == i1
        st["pick2"] = erow == i2
        st["assign"] = jnp.where(st["pick1"] | st["pick2"], 1.0, 0.0)

    def count_and_rank(sub, st):
        assign = st.pop("assign")
        st["rank"] = _dot(assign.astype(BF16), earlier)
        assign_pad = jnp.concatenate([assign, jnp.zeros((LANES - N_EXPERTS, ts), F32)], axis=0).astype(BF16)
        st["cnt_row"] = _dot_nt(jnp.ones((SUBLANES, ts), BF16), assign_pad)

    def sorted_positions(sub, st):
        cnt_row = st["cnt_row"]
        run_start = jnp.sum(jnp.where(e_lane < e_col, cnt_row[0:1, :], 0.0), axis=1, keepdims=True)
        pos = st.pop("rank") + run_start
        key1 = jnp.sum(jnp.where(st.pop("pick1"), pos, 0.0), axis=0, keepdims=True)
        key2 = jnp.sum(jnp.where(st.pop("pick2"), pos, 0.0), axis=0, keepdims=True)
        route_ref[:, rows_of(sub)] = jnp.where(row8 == 0, key1, jnp.where(row8 == 1, key2,
                                               jnp.where(row8 == 2, st.pop("w1"),
                                                         jnp.where(row8 == 3, st.pop("w2"), 0.0))))
        st["perm"] = jnp.where((srow == key1) | (srow == key2), 1.0, 0.0).astype(BF16)

    def sort_rows(sub, st):
        perm, h2b = st.pop("perm"), st.pop("h2b")
        for w in range(ROW_WORDS):
            pair = _dot(perm, h2b[:, 2 * w * LANES:(2 * w + 2) * LANES])
            _pack_rows(xbuf.at[slot, sub], w, SORT_ROWS, pair[:, :LANES], pair[:, LANES:])

    state = _run_staggered((pool, branch_projections, gated_sum, output_projection,
                            residual_and_ffn_input, router_logits, route, count_and_rank,
                            sorted_positions, sort_rows), SORT_GROUPS, lag=0)

    for sub in range(SORT_GROUPS):
        cnt_row = state[sub]["cnt_row"]
        meta_ref[sub] = jnp.where(mrow == 0, filled, jnp.where(mrow == 1, cnt_row, 0.0)).astype(I32)
        filled = filled + cnt_row

    fill[...] = filled
    counts_ref[...] = filled.astype(I32)

    meta_copy = pltpu.make_async_copy(meta_ref, meta_s.at[pl.ds(slot * SORT_GROUPS, SORT_GROUPS)], sem_s)
    meta_copy.start()
    meta_copy.wait()

    @pl.when(step == last)
    def _():
        send_tile(slot)

        @pl.when(step > 0)
        def _():
            wait_tile(1 - slot)

        wait_tile(slot)
        zbuf[...] = jnp.zeros_like(zbuf)
        final = slot * SORT_GROUPS + SORT_GROUPS - 1

        def pad_copy(e):
            end = e * region + meta_s[final, 0, e] + meta_s[final, 1, e]
            return pltpu.make_async_copy(
                zbuf, xs_hbm.at[pl.ds(end * ROW_WORDS, MOE_BLOCK * ROW_WORDS), :], sem_s)

        def start_pad(e, carry):
            pad_copy(e).start()
            return carry

        def wait_pad(e, carry):
            pad_copy(e).wait()
            return carry

        lax.fori_loop(0, N_EXPERTS, start_pad, 0)
        lax.fori_loop(0, N_EXPERTS, wait_pad, 0)


def _mixtail(u, att, gates, x, gt_m, sc_f, sh_f, g_post, g_pre,
             wpg, pscale, wbp, wba, wout, wr, br):
    b, s, d = x.shape
    tm = ROW_TILE
    tiles = s // tm
    n_tiles = b * tiles
    halo_blocks = tm // POOL_HALO
    region = _region_rows(b * s)
    const2 = lambda bi, i: (0, 0)
    const3 = lambda bi, i: (0, 0, 0)
    per_b = lambda bi, i: (bi, 0, 0)
    tile = lambda bi, i: (bi, i, 0)
    single = dict(pipeline_mode=pl.Buffered(1))
    return pl.pallas_call(
        _mixtail_kernel,
        grid=(b, tiles),
        in_specs=[pl.BlockSpec((1, tm, POOL_WIDTH), tile),
                  pl.BlockSpec((1, POOL_HALO, POOL_WIDTH),
                               lambda bi, i: (bi, jnp.maximum(i * halo_blocks - 1, 0), 0)),
                  pl.BlockSpec((1, tm, ATT_OUT_WIDTH), tile),
                  pl.BlockSpec((1, tm, 2 * D_MODEL), tile),
                  pl.BlockSpec((1, tm, d), tile),
                  pl.BlockSpec((1, 1, d), per_b),
                  pl.BlockSpec((1, 1, d), per_b),
                  pl.BlockSpec((1, 1, d), per_b),
                  pl.BlockSpec((1, d), const2),
                  pl.BlockSpec((1, d), const2),
                  pl.BlockSpec(wpg.shape, const2, **single),
                  pl.BlockSpec((1, POOL_WIDTH), const2),
                  pl.BlockSpec(wbp.shape, const2, **single),
                  pl.BlockSpec(wba.shape, const2, **single),
                  pl.BlockSpec(wout.shape, const2, **single),
                  pl.BlockSpec(wr.shape, const2, **single),
                  pl.BlockSpec(br.shape, const2)],
        out_specs=[pl.BlockSpec((1, tm, d), tile),
                   pl.BlockSpec((SUBLANES, tm), lambda bi, i: (0, bi * tiles + i)),
                   pl.BlockSpec((SORT_GROUPS, SUBLANES, LANES), lambda bi, i: (bi * tiles + i, 0, 0)),
                   pl.BlockSpec((SUBLANES, LANES), const2),
                   pl.BlockSpec(memory_space=pl.ANY)],
        out_shape=[jax.ShapeDtypeStruct((b, s, d), F32),
                   jax.ShapeDtypeStruct((SUBLANES, b * s), F32),
                   jax.ShapeDtypeStruct((n_tiles * SORT_GROUPS, SUBLANES, LANES), I32),
                   jax.ShapeDtypeStruct((SUBLANES, LANES), I32),
                   jax.ShapeDtypeStruct((N_EXPERTS * region * ROW_WORDS, LANES), I32)],
        scratch_shapes=[pltpu.VMEM((POOL_PAD + tm, POOL_WIDTH), F32),
                        pltpu.VMEM((SORT_GROUPS, 2, POOL_PAD + SORT_TOKENS, POOL_GROUP_DIM), F32),
                        pltpu.VMEM((2, SORT_GROUPS, SORT_ROWS * ROW_WORDS, LANES), I32),
                        pltpu.VMEM((MOE_BLOCK * ROW_WORDS, LANES), I32),
                        pltpu.VMEM((SUBLANES, LANES), F32),
                        pltpu.SMEM((2 * SORT_GROUPS, SUBLANES, LANES), I32),
                        pltpu.SemaphoreType.DMA((2,)),
                        pltpu.SemaphoreType.DMA(())],
        compiler_params=pltpu.CompilerParams(
            dimension_semantics=("arbitrary", "arbitrary"), vmem_limit_bytes=VMEM_LIMIT),
        name="mixtail",
    )(u, u, att, gates, x, gt_m, sc_f, sh_f, g_post, g_pre,
      wpg, pscale, wbp, wba, wout, wr, br)


def _expert_kernel(counts_ref, xs_hbm, wg_ref, wu_ref, wd_ref, ys_hbm,
                   xbuf, ybuf, state, semx, semy):
    e = pl.program_id(0)
    bm = MOE_BLOCK
    nx, ny = EXPERT_X_BUFFERS, EXPERT_Y_BUFFERS
    block_words = bm * ROW_WORDS
    region = xs_hbm.shape[0] // (N_EXPERTS * ROW_WORDS)

    def n_blocks(ex):
        return (counts_ref[ex] + (bm - 1)) // bm

    def block_rows(ref, ex, k):
        start = pl.multiple_of((ex * region + k * bm) * ROW_WORDS, block_words)
        return ref.at[pl.ds(start, block_words), :]

    def x_copy(ex, k, s):
        return pltpu.make_async_copy(block_rows(xs_hbm, ex, k), xbuf.at[s], semx.at[s])

    def y_copy(k, s):
        return pltpu.make_async_copy(ybuf.at[s], block_rows(ys_hbm, e, k), semy.at[s])

    @pl.when(e == 0)
    def _():
        for j in range(4):
            state[j] = 0

    def fetch_through(target):
        def more(c):
            pe, _, pg = c
            return (pg < target) & (pe < N_EXPERTS)

        def step(c):
            pe, pk, pg = c
            has = pk < n_blocks(pe)

            @pl.when(has)
            def _():
                x_copy(pe, pk, pg % nx).start()

            return (jnp.where(has, pe, pe + 1), jnp.where(has, pk + 1, 0), pg + has.astype(I32))

        pe, pk, pg = lax.while_loop(more, step, (state[1], state[2], state[3]))
        state[1] = pe
        state[2] = pk
        state[3] = pg

    nb = n_blocks(e)
    done = state[0]

    def blocks(k0, count):
        ks = [k0 + c for c in range(count)]
        gs = [done + k for k in ks]
        fetch_through(gs[0] + nx)
        for k, g in zip(ks, gs):
            x_copy(e, k, g % nx).wait()

            @pl.when(g >= ny)
            def _():
                y_copy(k, g % ny).wait()

        mids = []
        for g in gs:
            x = _unpack_rows(xbuf.at[g % nx], bm).astype(BF16)
            a = _dot(x, wg_ref[...].astype(BF16))
            u = _dot(x, wu_ref[...].astype(BF16))
            mids.append(((a * _sigmoid(a)) * u).astype(BF16))
        for k, g, mid in zip(ks, gs, mids):
            for w in range(ROW_WORDS):
                pair = _bf16_exact(_dot(mid, wd_ref[:, 2 * w * LANES:(2 * w + 2) * LANES].astype(BF16)))
                _pack_rows(ybuf.at[g % ny], w, bm, pair[:, :LANES], pair[:, LANES:])
            y_copy(k, g % ny).start()

    per = EXPERT_CHAINS

    def several_blocks(p, carry):
        blocks(per * p, per)
        return carry

    lax.fori_loop(0, nb // per, several_blocks, 0)
    k_next = (nb // per) * per
    count = per // 2
    while count >= 1:
        take = ((nb - k_next) >= count)

        @pl.when(take)
        def _(k_next=k_next, count=count):
            blocks(k_next, count)

        k_next = k_next + jnp.where(take, count, 0)
        count //= 2

    state[0] = done + nb

    @pl.when(e == N_EXPERTS - 1)
    def _():
        total = done + nb
        for j in range(ny):
            @pl.when(total > j)
            def _():
                y_copy(0, (total - 1 - j) % ny).wait()


def _experts(xs, counts, w_gate, w_up, w_down):
    bm = MOE_BLOCK
    w_in = pl.BlockSpec((None, D_MODEL, D_EXPERT), lambda e, cnt: (e, 0, 0))
    w_out = pl.BlockSpec((None, D_EXPERT, D_MODEL), lambda e, cnt: (e, 0, 0))
    grid_spec = pltpu.PrefetchScalarGridSpec(
        num_scalar_prefetch=1,
        grid=(N_EXPERTS,),
        in_specs=[pl.BlockSpec(memory_space=pl.ANY), w_in, w_in, w_out],
        out_specs=pl.BlockSpec(memory_space=pl.ANY),
        scratch_shapes=[pltpu.VMEM((EXPERT_X_BUFFERS, bm * ROW_WORDS, LANES), I32),
                        pltpu.VMEM((EXPERT_Y_BUFFERS, bm * ROW_WORDS, LANES), I32),
                        pltpu.SMEM((4,), I32),
                        pltpu.SemaphoreType.DMA((EXPERT_X_BUFFERS,)),
                        pltpu.SemaphoreType.DMA((EXPERT_Y_BUFFERS,))],
    )
    return pl.pallas_call(
        _expert_kernel,
        grid_spec=grid_spec,
        out_shape=jax.ShapeDtypeStruct(xs.shape, I32),
        compiler_params=pltpu.CompilerParams(
            dimension_semantics=("arbitrary",), vmem_limit_bytes=VMEM_LIMIT),
        name="experts",
    )(counts, xs, w_gate, w_up, w_down)


def _combine_kernel(meta_ref, y_hbm, route_ref, xmid_ref, gt_ref, g_ref, o_ref, ybuf, rt_scr, sem):
    step = pl.program_id(0)
    n_steps = pl.num_programs(0)
    tm = ROW_TILE
    slot = step % COMBINE_BUFFERS
    region = _region_rows(n_steps * tm)

    def fetch_tile(tile, live):
        tile = jnp.minimum(tile, n_steps - 1)
        which = tile % COMBINE_BUFFERS
        for sub in range(SORT_GROUPS):
            m = (tile * SORT_GROUPS + sub) * 2
            _tile_run_copies(lambda e: jnp.where(live, meta_ref[(m + 1) * N_EXPERTS + e], 0),
                             y_hbm, lambda e: e * region + meta_ref[m * N_EXPERTS + e],
                             ybuf.at[which, sub], None, sem.at[which])

    @pl.when(step == 0)
    def _():
        for ahead in range(COMBINE_BUFFERS - 1):
            fetch_tile(ahead, ahead < n_steps)

    for sub in range(SORT_GROUPS):
        pltpu.make_async_copy(y_hbm.at[pl.ds(0, SORT_ROWS * ROW_WORDS), :], ybuf.at[slot, sub],
                              sem.at[slot]).wait()
    sorted_rows = [_unpack_rows(ybuf.at[slot, sub], SORT_ROWS).astype(BF16) for sub in range(SORT_GROUPS)]

    fetch_tile(step + COMBINE_BUFFERS - 1, step + COMBINE_BUFFERS - 1 < n_steps)

    rt_scr[...] = jnp.zeros_like(rt_scr)
    rt_scr[0:SUBLANES, :] = route_ref[...]
    cols = jnp.concatenate([rt_scr[:, c * LANES:(c + 1) * LANES].T for c in range(tm // LANES)], axis=0)

    ts = SORT_TOKENS
    spos = lax.broadcasted_iota(I32, (ts, SORT_ROWS), 1).astype(F32)
    def rows_of(sub):
        return slice(sub * ts, (sub + 1) * ts)

    def gate_matrix(sub, st):
        key1, key2, w1, w2 = (cols[rows_of(sub), c:c + 1] for c in range(4))
        st["gates"] = jnp.where(spos == key1, w1, jnp.where(spos == key2, w2, 0.0)).astype(BF16)

    def weighted_unsort(sub, st):
        st["y"] = _dot(st.pop("gates"), sorted_rows[sub])

    def norm_and_add(sub, st):
        rows = rows_of(sub)
        o_ref[rows, :] = xmid_ref[rows, :] + _rmsnorm(st.pop("y"), gt_ref[0] * g_ref[...])

    _run_staggered((gate_matrix, weighted_unsort, norm_and_add), SORT_GROUPS, lag=1)


def _combine(meta, ys, route, x_mid, gt_f, g_post, seq):
    t, d = x_mid.shape
    tm = ROW_TILE
    tiles_per_seq = seq // tm
    grid_spec = pltpu.PrefetchScalarGridSpec(
        num_scalar_prefetch=1,
        grid=(t // tm,),
        in_specs=[pl.BlockSpec(memory_space=pl.ANY),
                  pl.BlockSpec((SUBLANES, tm), lambda i, m: (0, i)),
                  pl.BlockSpec((tm, d), lambda i, m: (i, 0)),
                  pl.BlockSpec((1, 1, d), lambda i, m: (i // tiles_per_seq, 0, 0)),
                  pl.BlockSpec((1, d), lambda i, m: (0, 0))],
        out_specs=pl.BlockSpec((tm, d), lambda i, m: (i, 0)),
        scratch_shapes=[pltpu.VMEM((COMBINE_BUFFERS, SORT_GROUPS, SORT_ROWS * ROW_WORDS, LANES), I32),
                        pltpu.VMEM((LANES, tm), F32),
                        pltpu.SemaphoreType.DMA((COMBINE_BUFFERS,))],
    )
    return pl.pallas_call(
        _combine_kernel,
        grid_spec=grid_spec,
        out_shape=jax.ShapeDtypeStruct((t, d), F32),
        compiler_params=pltpu.CompilerParams(
            dimension_semantics=("arbitrary",), vmem_limit_bytes=VMEM_LIMIT),
        name="combine",
    )(meta, ys, route, x_mid, gt_f, g_post)


def kernel(x, c, w_ada, b_ada, g_pre_mix, g_post_mix, g_pre_ffn, g_post_ffn, w_in, w_pool_group, pool_scale, w_branch_pool, w_branch_att, w_out, w_group_router, b_group_router, w_expert_router, b_expert_router, w_exp_gate, w_exp_up, w_exp_down):
    b, s, d = x.shape
    assert d == D_MODEL and s % (ATT_BLOCK * 2 * ATT_DILATIONS[2]) == 0 and s % ROW_TILE == 0
    assert (b * s) % MOE_BLOCK == 0
    depth = w_ada.shape[0]
    slopes = jnp.exp2(-ALIBI_MAX_BIAS * jnp.arange(1, N_ATT_HEADS + 1, dtype=F32) / N_ATT_HEADS)

    for layer in range(depth):
        mod = _adaln(c, w_ada[layer], b_ada[layer]).reshape(b, 6, 1, d)
        sh_m, sc_m, gt_m, sh_f, sc_f, gt_f = [mod[:, j] for j in range(6)]

        u, qkv0, qkv1, qkv2, gates = _inproj(x, g_pre_mix[layer].reshape(1, d), sc_m, sh_m,
                                             w_in[layer].astype(BF16))
        att = _attention(qkv0, qkv1, qkv2, slopes)

        zero_block = jnp.zeros((POOL_GROUP_DIM, POOL_GROUP_DIM), F32)
        wpg_diag = jnp.block([[w_pool_group[layer, g] if g == h else zero_block
                               for h in range(len(POOL_WINDOWS))] for g in range(len(POOL_WINDOWS))])

        pad_rows = ROUTER_ROWS - N_EXPERTS - N_EXPERT_GROUPS
        wrt = jnp.concatenate([w_expert_router[layer].T, w_group_router[layer].T,
                               jnp.zeros((pad_rows, d), F32)], axis=0)
        wrt_hi = wrt.astype(BF16)
        wr = jnp.concatenate([wrt_hi, (wrt - wrt_hi.astype(F32)).astype(BF16)], axis=0)
        br = jnp.concatenate([b_expert_router[layer], b_group_router[layer],
                              jnp.zeros((pad_rows,), F32)]).reshape(ROUTER_ROWS, 1)

        x_mid, route, meta, counts, xs = _mixtail(
            u, att, gates, x, gt_m, sc_f, sh_f,
            g_post_mix[layer].reshape(1, d), g_pre_ffn[layer].reshape(1, d),
            wpg_diag.astype(BF16), pool_scale[layer].reshape(1, POOL_WIDTH),
            w_branch_pool[layer].astype(BF16), w_branch_att[layer].astype(BF16),
            w_out[layer].astype(BF16), wr, br)

        ys = _experts(xs, counts[0, :N_EXPERTS], w_exp_gate[layer], w_exp_up[layer], w_exp_down[layer])
        run_meta = meta[:, 0:2, 0:N_EXPERTS].reshape(-1)
        x = _combine(run_meta, ys, route, x_mid.reshape(b * s, d), gt_f,
                     g_post_ffn[layer].reshape(1, d), s).reshape(b, s, d)
    return x
```

```python
import functools

import jax
import jax.numpy as jnp
from jax import lax
from jax.experimental import pallas as pl
from jax.experimental.pallas import tpu as pltpu

F32 = jnp.float32
BF16 = jnp.bfloat16
I32 = jnp.int32

D_MODEL = 1024
LANES = 128
SUBLANES = 8
ROW_WORDS = D_MODEL // (2 * LANES)
HI_HALF = -65536

POOL_WINDOWS = (2, 4, 8, 16)
POOL_GROUP_DIM = 128
POOL_WIDTH = 512
POOL_HALO = 16
POOL_PAD = POOL_HALO + 8

HEAD_DIM = 64
ATT_DILATIONS = (1, 4, 16)
ATT_REACH = 128
ATT_BLOCK = 128
ATT_UNROLL = 16
HEADS_PER_GROUP = 4
N_ATT_HEADS = 12
GROUP_QKV = 3 * HEADS_PER_GROUP * HEAD_DIM
ATT_OUT_WIDTH = 256
ALIBI_MAX_BIAS = 8.0
IN_WIDTH = POOL_WIDTH + 3 * GROUP_QKV + 2 * D_MODEL
MASKED = -1e30
LOG2_E = 1.4426950408889634
Q_SCALE = HEAD_DIM ** -0.5 * LOG2_E

N_EXPERT_GROUPS = 4
EXPERTS_PER_GROUP = 8
N_EXPERTS = 32
ROUTER_ROWS = 48
D_EXPERT = 512
RMS_EPS = 1e-6

ROW_TILE = 1024
ROW_SPLIT = 4
SORT_TOKENS = 256
SORT_GROUPS = ROW_TILE // SORT_TOKENS
SORT_ROWS = 2 * SORT_TOKENS
MOE_BLOCK = 256
EXPERT_CHAINS = 4
EXPERT_X_BUFFERS = 10
EXPERT_Y_BUFFERS = 8
COMBINE_BUFFERS = 3
VMEM_LIMIT = 52 * 1024 * 1024


def _sigmoid(x):
    return 0.5 * jnp.tanh(0.5 * x) + 0.5


def _rmsnorm(x, g):
    return x * lax.rsqrt(jnp.mean(x * x, axis=-1, keepdims=True) + RMS_EPS) * g


def _dot(a, b):
    return jnp.dot(a, b, preferred_element_type=F32)


def _dot_nt(a, b):
    return lax.dot_general(a, b, (((1,), (1,)), ((), ())), preferred_element_type=F32)


def _pack_rows(ref, w, n, lo, hi):
    word = (lax.shift_right_logical(pltpu.bitcast(lo, I32), 16) | (pltpu.bitcast(hi, I32) & HI_HALF))
    ref[pl.ds(w, n, stride=ROW_WORDS), :] = word


def _unpack_rows(ref, n):
    cols = []
    for w in range(ROW_WORDS):
        word = ref[pl.ds(w, n, stride=ROW_WORDS), :]
        cols += [pltpu.bitcast(word << 16, F32), pltpu.bitcast(word & HI_HALF, F32)]
    return jnp.concatenate(cols, axis=1)


def _bf16_exact(x):
    return x.astype(BF16).astype(F32)


def _run_staggered(stages, n_chains, lag):
    state = [{} for _ in range(n_chains)]
    for tick in range(len(stages) + lag * (n_chains - 1)):
        for chain in range(n_chains):
            if 0 <= tick - lag * chain < len(stages):
                stages[tick - lag * chain](chain, state[chain])
    return state


def _region_rows(n_tokens):
    return n_tokens + MOE_BLOCK


def _run_copies(n, src, src_row, dst, dst_row, sem):
    @pl.when(n > 0)
    def _():
        pltpu.make_async_copy(
            src.at[pl.ds(src_row * ROW_WORDS, n * ROW_WORDS), :],
            dst.at[pl.ds(dst_row * ROW_WORDS, n * ROW_WORDS), :], sem).start()


def _tile_run_copies(counts, src, src_rows, dst, dst_rows, sem):
    local = 0
    for e in range(N_EXPERTS):
        n = counts(e)
        _run_copies(n, src, local if src_rows is None else src_rows(e),
                    dst, local if dst_rows is None else dst_rows(e), sem)
        local = local + n


def _adaln_kernel(c_ref, w_ref, b_ref, o_ref):
    c = c_ref[...]
    a = c * _sigmoid(c)
    w = w_ref[...]
    a_hi, w_hi = a.astype(BF16), w.astype(BF16)
    a_lo = (a - a_hi.astype(F32)).astype(BF16)
    w_lo = (w - w_hi.astype(F32)).astype(BF16)
    o_ref[...] = _dot(a_hi, w_hi) + (_dot(a_hi, w_lo) + _dot(a_lo, w_hi)) + b_ref[...]


def _adaln(c, w_ada, b_ada):
    b, d = c.shape
    n = w_ada.shape[1]
    rows = -(-b // SUBLANES) * SUBLANES
    cp = jnp.pad(c, ((0, rows - b), (0, 0)))
    nt = 1536
    out = pl.pallas_call(
        _adaln_kernel,
        grid=(n // nt,),
        in_specs=[pl.BlockSpec((rows, d), lambda j: (0, 0)),
                  pl.BlockSpec((d, nt), lambda j: (0, j)),
                  pl.BlockSpec((1, nt), lambda j: (0, j))],
        out_specs=pl.BlockSpec((rows, nt), lambda j: (0, j)),
        out_shape=jax.ShapeDtypeStruct((rows, n), F32),
        compiler_params=pltpu.CompilerParams(vmem_limit_bytes=VMEM_LIMIT),
        name="adaln",
    )(cp, w_ada, b_ada.reshape(1, n))
    return out[:b]


def _inproj_kernel(x_ref, g_ref, sc_ref, sh_ref, w_ref,
                   u_ref, qkv0_ref, qkv1_ref, qkv2_ref, gate_ref, h_scr, p_scr, pu, lv, tail):
    tm = x_ref.shape[1]
    hm = tm // ROW_SPLIT

    @pl.when((pl.program_id(0) == 0) & (pl.program_id(1) == 0))
    def _():
        pu[0:POOL_PAD - POOL_HALO, :] = jnp.zeros((POOL_PAD - POOL_HALO, POOL_WIDTH), F32)
        lv[:, 0:POOL_PAD - POOL_HALO, :] = jnp.zeros((2, POOL_PAD - POOL_HALO, POOL_GROUP_DIM), F32)
        tail[...] = jnp.zeros_like(tail)
    for part in range(ROW_SPLIT):
        rows = slice(part * hm, (part + 1) * hm)
        h = _rmsnorm(x_ref[0, rows, :], g_ref[...] * (1.0 + sc_ref[0])) + sh_ref[0]
        h_scr[rows, :] = h.astype(BF16)
        hb = h_scr[rows, :]

        head = POOL_PAD - POOL_HALO
        pu[head:POOL_PAD, :] = tail[...] if part else jnp.where(pl.program_id(1) > 0, tail[...], 0.0)
        pu[POOL_PAD:POOL_PAD + hm, :] = _dot(hb, w_ref[:, 0:POOL_WIDTH])
        tail[...] = pu[POOL_PAD + hm - POOL_HALO:POOL_PAD + hm, :]
        t = pl.program_id(1) * tm + part * hm + lax.broadcasted_iota(I32, (hm, 1), 0)
        for g, w in enumerate(POOL_WINDOWS):
            cols = slice(g * POOL_GROUP_DIM, (g + 1) * POOL_GROUP_DIM)
            read = lambda start, n: pu[pl.ds(start, n), cols]
            shift, level = 1, 0
            while 2 * shift < w:
                partial = read(head, hm + POOL_HALO) + read(head - shift, hm + POOL_HALO)
                buf = lv.at[level % 2]
                buf[pl.ds(head, hm + POOL_HALO), :] = partial
                read = lambda start, n, buf=buf: buf[pl.ds(start, n), :]
                shift, level = 2 * shift, level + 1
            win = read(POOL_PAD, hm) + read(POOL_PAD - shift, hm)
            count = jnp.minimum(t + 1, w).astype(F32)
            u_ref[0, rows, cols] = (win / count - pu[pl.ds(POOL_PAD, hm), cols]).astype(BF16)

        def group_qkv(g):
            width = HEADS_PER_GROUP * HEAD_DIM
            sections = []
            for sec in range(3):
                lo = POOL_WIDTH + sec * N_ATT_HEADS * HEAD_DIM + g * width
                part_proj = _dot(hb, w_ref[:, lo:lo + width])
                sections.append(part_proj * Q_SCALE if sec == 0 else part_proj)
            return jnp.concatenate(sections, axis=1)

        col = POOL_WIDTH + 3 * GROUP_QKV
        qkv0_ref[0, 0, rows, :] = group_qkv(0).astype(BF16)
        for gi, (out_ref, d) in enumerate(((qkv1_ref, ATT_DILATIONS[1]), (qkv2_ref, ATT_DILATIONS[2]))):
            proj = group_qkv(gi + 1)
            stage = p_scr.at[part, gi]
            for cb in range(GROUP_QKV // LANES):
                stage[cb] = proj[:, cb * LANES:(cb + 1) * LANES]
            sub = hm // d
            for r in range(d):
                out_ref[0, r, part * sub:(part + 1) * sub, :] = jnp.concatenate(
                    [stage[cb, pl.ds(r, sub, stride=d), :] for cb in range(GROUP_QKV // LANES)],
                    axis=1).astype(BF16)

        chunk = 512
        for j in range(2 * D_MODEL // chunk):
            g = _dot(hb, w_ref[:, col + j * chunk:col + (j + 1) * chunk])
            gate_ref[0, rows, j * chunk:(j + 1) * chunk] = _sigmoid(g).astype(BF16)


def _inproj(x, g_pre, sc, sh, w_in):
    b, s, d = x.shape
    tm = ROW_TILE
    d1, d2 = ATT_DILATIONS[1], ATT_DILATIONS[2]
    grid = (b, s // tm)
    const = lambda bi, i: (0, 0)
    per_b = lambda bi, i: (bi, 0, 0)
    return pl.pallas_call(
        _inproj_kernel,
        grid=grid,
        in_specs=[pl.BlockSpec((1, tm, d), lambda bi, i: (bi, i, 0)),
                  pl.BlockSpec((1, d), const),
                  pl.BlockSpec((1, 1, d), per_b),
                  pl.BlockSpec((1, 1, d), per_b),
                  pl.BlockSpec((d, IN_WIDTH), const, pipeline_mode=pl.Buffered(1))],
        out_specs=[pl.BlockSpec((1, tm, POOL_WIDTH), lambda bi, i: (bi, i, 0)),
                   pl.BlockSpec((1, 1, tm, GROUP_QKV), lambda bi, i: (bi, 0, i, 0)),
                   pl.BlockSpec((1, d1, tm // d1, GROUP_QKV), lambda bi, i: (bi, 0, i, 0)),
                   pl.BlockSpec((1, d2, tm // d2, GROUP_QKV), lambda bi, i: (bi, 0, i, 0)),
                   pl.BlockSpec((1, tm, 2 * D_MODEL), lambda bi, i: (bi, i, 0))],
        out_shape=[jax.ShapeDtypeStruct((b, s, POOL_WIDTH), BF16),
                   jax.ShapeDtypeStruct((b, 1, s, GROUP_QKV), BF16),
                   jax.ShapeDtypeStruct((b, d1, s // d1, GROUP_QKV), BF16),
                   jax.ShapeDtypeStruct((b, d2, s // d2, GROUP_QKV), BF16),
                   jax.ShapeDtypeStruct((b, s, 2 * D_MODEL), BF16)],
        scratch_shapes=[pltpu.VMEM((tm, d), BF16),
                        pltpu.VMEM((ROW_SPLIT, 2, GROUP_QKV // LANES, tm // ROW_SPLIT, LANES), F32),
                        pltpu.VMEM((POOL_PAD + tm // ROW_SPLIT, POOL_WIDTH), F32),
                        pltpu.VMEM((2, POOL_PAD + tm // ROW_SPLIT, POOL_GROUP_DIM), F32),
                        pltpu.VMEM((POOL_HALO, POOL_WIDTH), F32)],
        compiler_params=pltpu.CompilerParams(
            dimension_semantics=("arbitrary", "arbitrary"), vmem_limit_bytes=VMEM_LIMIT),
        name="inproj",
    )(x, g_pre, sc, sh, w_in)


def _attn_kernel(slopes_ref, q0, k0, v0, q1, k1, v1, q2, k2, v2, o_ref,
                 bias_scr, acc, mst, lst, s_even, s_odd):
    pair = pl.program_id(1)
    seq = o_ref.shape[1]
    nblk = seq // ATT_BLOCK
    lane = lax.broadcasted_iota(I32, (ATT_BLOCK, LANES), 1)
    first_head = lane < HEAD_DIM
    half_lane = lax.broadcasted_iota(I32, (ATT_BLOCK // 2, LANES), 1)
    head_bits = (jnp.where(half_lane < HEAD_DIM, -1, 0), jnp.where(half_lane < HEAD_DIM, 0, -1))
    ones_block = jnp.ones((2 * ATT_BLOCK, LANES), BF16)

    qi = lax.broadcasted_iota(I32, (ATT_BLOCK, 2 * ATT_BLOCK), 0)
    kj = lax.broadcasted_iota(I32, (ATT_BLOCK, 2 * ATT_BLOCK), 1)
    delta = ATT_BLOCK + qi - kj
    valid = (delta >= 0) & (delta <= ATT_REACH)
    delta0 = qi - kj
    valid0 = delta0 >= 0
    for g, d in enumerate(ATT_DILATIONS):
        for j in range(2):
            slope = slopes_ref[g * HEADS_PER_GROUP + 2 * pair + j]
            slope = slope * LOG2_E
            bias_scr[g, j, 0] = jnp.where(valid0, -slope * (delta0 * d).astype(F32), MASKED)
            bias_scr[g, j, 1] = jnp.where(valid, -slope * (delta * d).astype(F32), MASKED)

    n_iter = nblk // ATT_UNROLL

    def group_stages(g, q_ref, k_ref, v_ref):
        d = ATT_DILATIONS[g]
        per_res = nblk // d

        def block_index(it, k):
            n = it * ATT_UNROLL + k
            return n // per_res, n % per_res

        def key_rows(i):
            lo = jnp.maximum(i - 1, 0)
            return pl.ds(pl.multiple_of(lo * ATT_BLOCK, ATT_BLOCK), 2 * ATT_BLOCK)

        def scores(it, s_ref):
            for k in range(ATT_UNROLL):
                r, i = block_index(it, k)
                q = q_ref[0, r, pl.ds(pl.multiple_of(i * ATT_BLOCK, ATT_BLOCK), ATT_BLOCK), :]
                kw = k_ref[0, r, key_rows(i), :]
                qbits = pltpu.bitcast(q, I32)
                for j in range(2):
                    qh = pltpu.bitcast(qbits & head_bits[j], BF16)
                    s_ref[2 * k + j] = _dot_nt(qh, kw) + bias_scr[g, j, jnp.minimum(i, 1)]

        def weighted_values(it, s_ref):
            for k in range(ATT_UNROLL):
                r, i = block_index(it, k)
                vw = jnp.concatenate([v_ref[0, r, key_rows(i), :], ones_block], axis=1)
                outs = []
                for j in range(2):
                    sc = s_ref[2 * k + j]
                    mx = jnp.max(sc, axis=-1, keepdims=True)
                    p = jnp.exp2(sc - mx)
                    num_den = _dot(p.astype(BF16), vw)
                    outs.append((num_den[:, :LANES], mx, num_den[:, LANES:]))
                (n0, m0, l0), (n1, m1, l1) = outs
                if d == 1:
                    rows = pl.ds(pl.multiple_of(i * ATT_BLOCK, ATT_BLOCK), ATT_BLOCK)
                else:
                    rows = pl.ds(i * (ATT_BLOCK * d) + r, ATT_BLOCK, stride=d)
                acc[g, rows, :] = jnp.where(first_head, n0, n1)
                mst[g, rows, :] = jnp.where(first_head, m0, m1)
                lst[g, rows, :] = jnp.where(first_head, l0, l1)

        return scores, weighted_values

    stages = [group_stages(0, q0, k0, v0), group_stages(1, q1, k1, v1), group_stages(2, q2, k2, v2)]
    stages[0][0](jnp.int32(0), s_even)
    for g, (scores, weighted_values) in enumerate(stages):
        def pair_of_iterations(it, then_scores):
            scores(it + 1, s_odd)
            weighted_values(it, s_even)
            then_scores()
            weighted_values(it + 1, s_odd)

        def body(h, carry, pair_of_iterations=pair_of_iterations, scores=scores):
            it = 2 * h
            pair_of_iterations(it, lambda: scores(it + 2, s_even))
            return carry

        lax.fori_loop(0, n_iter // 2 - 1, body, 0)
        if g + 1 < len(stages):
            next_scores = stages[g + 1][0]
            pair_of_iterations(jnp.int32(n_iter - 2), lambda: next_scores(jnp.int32(0), s_even))
        else:
            pair_of_iterations(jnp.int32(n_iter - 2), lambda: None)

    def finish(it, carry):
        for k in range(ATT_UNROLL):
            rows = pl.ds(pl.multiple_of((it * ATT_UNROLL + k) * ATT_BLOCK, ATT_BLOCK), ATT_BLOCK)
            ms = [mst[g, rows, :] for g in range(3)]
            top = jnp.maximum(jnp.maximum(ms[0], ms[1]), ms[2])
            scale = [jnp.exp2(m - top) for m in ms]
            num = scale[0] * acc[0, rows, :] + scale[1] * acc[1, rows, :] + scale[2] * acc[2, rows, :]
            den = scale[0] * lst[0, rows, :] + scale[1] * lst[1, rows, :] + scale[2] * lst[2, rows, :]
            o_ref[0, rows, :] = (num / den).astype(BF16)
        return carry

    lax.fori_loop(0, nblk // ATT_UNROLL, finish, 0)


def _attention(qkv0, qkv1, qkv2, slopes):
    b, _, s, _ = qkv0.shape
    pairs = HEADS_PER_GROUP // 2
    col_blocks = HEADS_PER_GROUP * HEAD_DIM // LANES

    def specs(arr):
        _, d, sub, _ = arr.shape
        return [pl.BlockSpec((1, d, sub, LANES),
                             functools.partial(lambda bi, p, sec: (bi, 0, 0, sec * col_blocks + p), sec=sec))
                for sec in range(3)]

    return pl.pallas_call(
        _attn_kernel,
        grid=(b, pairs),
        in_specs=[pl.BlockSpec(memory_space=pltpu.SMEM)] + specs(qkv0) + specs(qkv1) + specs(qkv2),
        out_specs=pl.BlockSpec((1, s, LANES), lambda bi, p: (bi, 0, p)),
        out_shape=jax.ShapeDtypeStruct((b, s, ATT_OUT_WIDTH), BF16),
        scratch_shapes=[pltpu.VMEM((3, 2, 2, ATT_BLOCK, 2 * ATT_BLOCK), F32),
                        pltpu.VMEM((3, s, LANES), F32),
                        pltpu.VMEM((3, s, LANES), F32),
                        pltpu.VMEM((3, s, LANES), F32),
                        pltpu.VMEM((2 * ATT_UNROLL, ATT_BLOCK, 2 * ATT_BLOCK), F32),
                        pltpu.VMEM((2 * ATT_UNROLL, ATT_BLOCK, 2 * ATT_BLOCK), F32)],
        compiler_params=pltpu.CompilerParams(
            dimension_semantics=("arbitrary", "arbitrary"), vmem_limit_bytes=VMEM_LIMIT),
        name="attention",
    )(slopes, qkv0, qkv0, qkv0, qkv1, qkv1, qkv1, qkv2, qkv2, qkv2)


def _mixtail_kernel(u_ref, att_ref, gate_ref, x_ref,
                    gt_m_ref, sc_f_ref, sh_f_ref, g_post_ref, g_pre_ref,
                    wpg_ref, pscale_ref, wbp_ref, wba_ref, wout_ref, wr_ref, br_ref,
                    xmid_ref, route_ref, meta_ref, counts_ref, xs_hbm,
                    xbuf, zbuf, fill, meta_s, sem, sem_s):
    i = pl.program_id(1)
    tm = x_ref.shape[1]
    step = pl.program_id(0) * pl.num_programs(1) + i
    last = pl.num_programs(0) * pl.num_programs(1) - 1
    slot = step % 2
    region = _region_rows(pl.num_programs(0) * pl.num_programs(1) * tm)

    @pl.when(step == 0)
    def _():
        fill[...] = jnp.zeros_like(fill)
        for sub in range(SORT_GROUPS):
            for e in range(N_EXPERTS):
                meta_s[SORT_GROUPS + sub, 1, e] = 0
                meta_s[SORT_GROUPS + sub, 0, e] = 0

    def send_tile(which):
        for sub in range(SORT_GROUPS):
            m = which * SORT_GROUPS + sub
            _tile_run_copies(lambda e: meta_s[m, 1, e], xbuf.at[which, sub], None,
                             xs_hbm, lambda e: e * region + meta_s[m, 0, e], sem.at[which])

    def wait_tile(which):
        for sub in range(SORT_GROUPS):
            pltpu.make_async_copy(xbuf.at[which, sub], xs_hbm.at[pl.ds(0, SORT_ROWS * ROW_WORDS), :],
                                  sem.at[which]).wait()

    @pl.when(step > 1)
    def _():
        wait_tile(slot)

    send_tile(1 - slot)

    ts = SORT_TOKENS

    def pool(sub, st):
        st["pooled"] = u_ref[0, sub * ts:(sub + 1) * ts, :]

    neg_inf = -jnp.inf
    far = float(LANES)
    before = (lax.broadcasted_iota(I32, (ts, ts), 0) < lax.broadcasted_iota(I32, (ts, ts), 1))
    earlier = jnp.where(before, 1.0, 0.0).astype(BF16)
    erow = lax.broadcasted_iota(I32, (N_EXPERTS, ts), 0).astype(F32)
    e_col = lax.broadcasted_iota(I32, (N_EXPERTS, LANES), 0)
    e_lane = lax.broadcasted_iota(I32, (N_EXPERTS, LANES), 1)
    row8 = lax.broadcasted_iota(I32, (SUBLANES, ts), 0)
    mrow = lax.broadcasted_iota(I32, (SUBLANES, LANES), 0)
    srow = lax.broadcasted_iota(I32, (SORT_ROWS, ts), 0).astype(F32)
    filled = fill[...]

    def rows_of(sub):
        return slice(sub * ts, (sub + 1) * ts)

    def branch_projections(sub, st):
        rows = rows_of(sub)
        mixed = _dot(st.pop("pooled"), wpg_ref[...]) * pscale_ref[...]
        st["y_pool"] = _dot(mixed.astype(BF16), wbp_ref[...])
        st["y_att"] = _dot(att_ref[0, rows, :], wba_ref[...])

    def gated_sum(sub, st):
        rows = rows_of(sub)
        st["merged"] = (gate_ref[0, rows, 0:D_MODEL] * st.pop("y_pool").astype(BF16)
                        + gate_ref[0, rows, D_MODEL:2 * D_MODEL] * st.pop("y_att").astype(BF16))

    def output_projection(sub, st):
        st["y"] = _dot(st.pop("merged"), wout_ref[...])

    def residual_and_ffn_input(sub, st):
        rows = rows_of(sub)
        x_mid = x_ref[0, rows, :] + _rmsnorm(st.pop("y"), gt_m_ref[0] * g_post_ref[...])
        xmid_ref[0, rows, :] = x_mid
        h2 = _rmsnorm(x_mid, g_pre_ref[...] * (1.0 + sc_f_ref[0])) + sh_f_ref[0]
        st["h2b"] = h2.astype(BF16)
        st["h2lo"] = (h2 - st["h2b"].astype(F32)).astype(BF16)

    def router_logits(sub, st):
        by_hi = _dot_nt(wr_ref[...], st["h2b"])
        st["logits"] = (by_hi[0:ROUTER_ROWS] + (by_hi[ROUTER_ROWS:2 * ROUTER_ROWS]
                        + _dot_nt(wr_ref[0:ROUTER_ROWS, :], st.pop("h2lo"))) + br_ref[...])

    def route(sub, st):
        logits = st.pop("logits")
        gl = logits[N_EXPERTS:N_EXPERTS + N_EXPERT_GROUPS, :]
        grow = lax.broadcasted_iota(I32, gl.shape, 0).astype(F32)
        gmax = jnp.max(gl, axis=0, keepdims=True)
        gsel = jnp.min(jnp.where(gl == gmax, grow, far), axis=0, keepdims=True)
        p_group = 1.0 / jnp.sum(jnp.exp(gl - gmax), axis=0, keepdims=True)
        e_lo = gsel * float(EXPERTS_PER_GROUP)
        el = jnp.where((erow >= e_lo) & (erow < e_lo + float(EXPERTS_PER_GROUP)),
                       logits[0:N_EXPERTS, :], neg_inf)
        v1 = jnp.max(el, axis=0, keepdims=True)
        i1 = jnp.min(jnp.where(el == v1, erow, far), axis=0, keepdims=True)
        el2 = jnp.where(erow == i1, neg_inf, el)
        v2 = jnp.max(el2, axis=0, keepdims=True)
        i2 = jnp.min(jnp.where(el2 == v2, erow, far), axis=0, keepdims=True)
        e21 = jnp.exp(v2 - v1)
        st["w1"] = p_group / (1.0 + e21)
        st["w2"] = p_group * e21 / (1.0 + e21)
        st["pick1"] = erow == i1
        st["pick2"] = erow == i2
        st["assign"] = jnp.where(st["pick1"] | st["pick2"], 1.0, 0.0)

    def count_and_rank(sub, st):
        assign = st.pop("assign")
        st["rank"] = _dot(assign.astype(BF16), earlier)
        assign_pad = jnp.concatenate([assign, jnp.zeros((LANES - N_EXPERTS, ts), F32)], axis=0).astype(BF16)
        st["cnt_row"] = _dot_nt(jnp.ones((SUBLANES, ts), BF16), assign_pad)

    def sorted_positions(sub, st):
        cnt_row = st["cnt_row"]
        run_start = jnp.sum(jnp.where(e_lane < e_col, cnt_row[0:1, :], 0.0), axis=1, keepdims=True)
        pos = st.pop("rank") + run_start
        key1 = jnp.sum(jnp.where(st.pop("pick1"), pos, 0.0), axis=0, keepdims=True)
        key2 = jnp.sum(jnp.where(st.pop("pick2"), pos, 0.0), axis=0, keepdims=True)
        route_ref[:, rows_of(sub)] = jnp.where(row8 == 0, key1, jnp.where(row8 == 1, key2,
                                               jnp.where(row8 == 2, st.pop("w1"),
                                                         jnp.where(row8 == 3, st.pop("w2"), 0.0))))
        st["perm"] = jnp.where((srow == key1) | (srow == key2), 1.0, 0.0).astype(BF16)

    def sort_rows(sub, st):
        perm, h2b = st.pop("perm"), st.pop("h2b")
        for w in range(ROW_WORDS):
            pair = _dot(perm, h2b[:, 2 * w * LANES:(2 * w + 2) * LANES])
            _pack_rows(xbuf.at[slot, sub], w, SORT_ROWS, pair[:, :LANES], pair[:, LANES:])

    state = _run_staggered((pool, branch_projections, gated_sum, output_projection,
                            residual_and_ffn_input, router_logits, route, count_and_rank,
                            sorted_positions, sort_rows), SORT_GROUPS, lag=0)

    for sub in range(SORT_GROUPS):
        cnt_row = state[sub]["cnt_row"]
        meta_ref[sub] = jnp.where(mrow == 0, filled, jnp.where(mrow == 1, cnt_row, 0.0)).astype(I32)
        filled = filled + cnt_row

    fill[...] = filled
    counts_ref[...] = filled.astype(I32)

    meta_copy = pltpu.make_async_copy(meta_ref, meta_s.at[pl.ds(slot * SORT_GROUPS, SORT_GROUPS)], sem_s)
    meta_copy.start()
    meta_copy.wait()

    @pl.when(step == last)
    def _():
        send_tile(slot)

        @pl.when(step > 0)
        def _():
            wait_tile(1 - slot)

        wait_tile(slot)
        zbuf[...] = jnp.zeros_like(zbuf)
        final = slot * SORT_GROUPS + SORT_GROUPS - 1

        def pad_copy(e):
            end = e * region + meta_s[final, 0, e] + meta_s[final, 1, e]
            return pltpu.make_async_copy(
                zbuf, xs_hbm.at[pl.ds(end * ROW_WORDS, MOE_BLOCK * ROW_WORDS), :], sem_s)

        def start_pad(e, carry):
            pad_copy(e).start()
            return carry

        def wait_pad(e, carry):
            pad_copy(e).wait()
            return carry

        lax.fori_loop(0, N_EXPERTS, start_pad, 0)
        lax.fori_loop(0, N_EXPERTS, wait_pad, 0)


def _mixtail(u, att, gates, x, gt_m, sc_f, sh_f, g_post, g_pre,
             wpg, pscale, wbp, wba, wout, wr, br):
    b, s, d = x.shape
    tm = ROW_TILE
    tiles = s // tm
    n_tiles = b * tiles
    region = _region_rows(b * s)
    const2 = lambda bi, i: (0, 0)
    const3 = lambda bi, i: (0, 0, 0)
    per_b = lambda bi, i: (bi, 0, 0)
    tile = lambda bi, i: (bi, i, 0)
    single = dict(pipeline_mode=pl.Buffered(1))
    return pl.pallas_call(
        _mixtail_kernel,
        grid=(b, tiles),
        in_specs=[pl.BlockSpec((1, tm, POOL_WIDTH), tile),
                  pl.BlockSpec((1, tm, ATT_OUT_WIDTH), tile),
                  pl.BlockSpec((1, tm, 2 * D_MODEL), tile),
                  pl.BlockSpec((1, tm, d), tile),
                  pl.BlockSpec((1, 1, d), per_b),
                  pl.BlockSpec((1, 1, d), per_b),
                  pl.BlockSpec((1, 1, d), per_b),
                  pl.BlockSpec((1, d), const2),
                  pl.BlockSpec((1, d), const2),
                  pl.BlockSpec(wpg.shape, const2, **single),
                  pl.BlockSpec((1, POOL_WIDTH), const2),
                  pl.BlockSpec(wbp.shape, const2, **single),
                  pl.BlockSpec(wba.shape, const2, **single),
                  pl.BlockSpec(wout.shape, const2, **single),
                  pl.BlockSpec(wr.shape, const2, **single),
                  pl.BlockSpec(br.shape, const2)],
        out_specs=[pl.BlockSpec((1, tm, d), tile),
                   pl.BlockSpec((SUBLANES, tm), lambda bi, i: (0, bi * tiles + i)),
                   pl.BlockSpec((SORT_GROUPS, SUBLANES, LANES), lambda bi, i: (bi * tiles + i, 0, 0)),
                   pl.BlockSpec((SUBLANES, LANES), const2),
                   pl.BlockSpec(memory_space=pl.ANY)],
        out_shape=[jax.ShapeDtypeStruct((b, s, d), F32),
                   jax.ShapeDtypeStruct((SUBLANES, b * s), F32),
                   jax.ShapeDtypeStruct((n_tiles * SORT_GROUPS, SUBLANES, LANES), I32),
                   jax.ShapeDtypeStruct((SUBLANES, LANES), I32),
                   jax.ShapeDtypeStruct((N_EXPERTS * region * ROW_WORDS, LANES), I32)],
        scratch_shapes=[pltpu.VMEM((2, SORT_GROUPS, SORT_ROWS * ROW_WORDS, LANES), I32),
                        pltpu.VMEM((MOE_BLOCK * ROW_WORDS, LANES), I32),
                        pltpu.VMEM((SUBLANES, LANES), F32),
                        pltpu.SMEM((2 * SORT_GROUPS, SUBLANES, LANES), I32),
                        pltpu.SemaphoreType.DMA((2,)),
                        pltpu.SemaphoreType.DMA(())],
        compiler_params=pltpu.CompilerParams(
            dimension_semantics=("arbitrary", "arbitrary"), vmem_limit_bytes=VMEM_LIMIT),
        name="mixtail",
    )(u, att, gates, x, gt_m, sc_f, sh_f, g_post, g_pre,
      wpg, pscale, wbp, wba, wout, wr, br)


def _expert_kernel(counts_ref, xs_hbm, wg_ref, wu_ref, wd_ref, ys_hbm,
                   xbuf, ybuf, state, semx, semy):
    e = pl.program_id(0)
    bm = MOE_BLOCK
    nx, ny = EXPERT_X_BUFFERS, EXPERT_Y_BUFFERS
    block_words = bm * ROW_WORDS
    region = xs_hbm.shape[0] // (N_EXPERTS * ROW_WORDS)

    def n_blocks(ex):
        return (counts_ref[ex] + (bm - 1)) // bm

    def block_rows(ref, ex, k):
        start = pl.multiple_of((ex * region + k * bm) * ROW_WORDS, block_words)
        return ref.at[pl.ds(start, block_words), :]

    def x_copy(ex, k, s):
        return pltpu.make_async_copy(block_rows(xs_hbm, ex, k), xbuf.at[s], semx.at[s])

    def y_copy(k, s):
        return pltpu.make_async_copy(ybuf.at[s], block_rows(ys_hbm, e, k), semy.at[s])

    @pl.when(e == 0)
    def _():
        for j in range(4):
            state[j] = 0

    def fetch_through(target):
        def more(c):
            pe, _, pg = c
            return (pg < target) & (pe < N_EXPERTS)

        def step(c):
            pe, pk, pg = c
            has = pk < n_blocks(pe)

            @pl.when(has)
            def _():
                x_copy(pe, pk, pg % nx).start()

            return (jnp.where(has, pe, pe + 1), jnp.where(has, pk + 1, 0), pg + has.astype(I32))

        pe, pk, pg = lax.while_loop(more, step, (state[1], state[2], state[3]))
        state[1] = pe
        state[2] = pk
        state[3] = pg

    nb = n_blocks(e)
    done = state[0]

    def blocks(k0, count):
        ks = [k0 + c for c in range(count)]
        gs = [done + k for k in ks]
        fetch_through(gs[0] + nx)
        for k, g in zip(ks, gs):
            x_copy(e, k, g % nx).wait()

            @pl.when(g >= ny)
            def _():
                y_copy(k, g % ny).wait()

        mids = []
        for g in gs:
            x = _unpack_rows(xbuf.at[g % nx], bm).astype(BF16)
            a = _dot(x, wg_ref[...].astype(BF16))
            u = _dot(x, wu_ref[...].astype(BF16))
            mids.append(((a * _sigmoid(a)) * u).astype(BF16))
        for k, g, mid in zip(ks, gs, mids):
            for w in range(ROW_WORDS):
                pair = _bf16_exact(_dot(mid, wd_ref[:, 2 * w * LANES:(2 * w + 2) * LANES].astype(BF16)))
                _pack_rows(ybuf.at[g % ny], w, bm, pair[:, :LANES], pair[:, LANES:])
            y_copy(k, g % ny).start()

    per = EXPERT_CHAINS

    def several_blocks(p, carry):
        blocks(per * p, per)
        return carry

    lax.fori_loop(0, nb // per, several_blocks, 0)
    k_next = (nb // per) * per
    count = per // 2
    while count >= 1:
        take = ((nb - k_next) >= count)

        @pl.when(take)
        def _(k_next=k_next, count=count):
            blocks(k_next, count)

        k_next = k_next + jnp.where(take, count, 0)
        count //= 2

    state[0] = done + nb

    @pl.when(e == N_EXPERTS - 1)
    def _():
        total = done + nb
        for j in range(ny):
            @pl.when(total > j)
            def _():
                y_copy(0, (total - 1 - j) % ny).wait()


def _experts(xs, counts, w_gate, w_up, w_down):
    bm = MOE_BLOCK
    w_in = pl.BlockSpec((None, D_MODEL, D_EXPERT), lambda e, cnt: (e, 0, 0))
    w_out = pl.BlockSpec((None, D_EXPERT, D_MODEL), lambda e, cnt: (e, 0, 0))
    grid_spec = pltpu.PrefetchScalarGridSpec(
        num_scalar_prefetch=1,
        grid=(N_EXPERTS,),
        in_specs=[pl.BlockSpec(memory_space=pl.ANY), w_in, w_in, w_out],
        out_specs=pl.BlockSpec(memory_space=pl.ANY),
        scratch_shapes=[pltpu.VMEM((EXPERT_X_BUFFERS, bm * ROW_WORDS, LANES), I32),
                        pltpu.VMEM((EXPERT_Y_BUFFERS, bm * ROW_WORDS, LANES), I32),
                        pltpu.SMEM((4,), I32),
                        pltpu.SemaphoreType.DMA((EXPERT_X_BUFFERS,)),
                        pltpu.SemaphoreType.DMA((EXPERT_Y_BUFFERS,))],
    )
    return pl.pallas_call(
        _expert_kernel,
        grid_spec=grid_spec,
        out_shape=jax.ShapeDtypeStruct(xs.shape, I32),
        compiler_params=pltpu.CompilerParams(
            dimension_semantics=("arbitrary",), vmem_limit_bytes=VMEM_LIMIT),
        name="experts",
    )(counts, xs, w_gate, w_up, w_down)


def _combine_kernel(meta_ref, y_hbm, route_ref, xmid_ref, gt_ref, g_ref, o_ref, ybuf, rt_scr, sem):
    step = pl.program_id(0)
    n_steps = pl.num_programs(0)
    tm = ROW_TILE
    slot = step % COMBINE_BUFFERS
    region = _region_rows(n_steps * tm)

    def fetch_tile(tile, live):
        tile = jnp.minimum(tile, n_steps - 1)
        which = tile % COMBINE_BUFFERS
        for sub in range(SORT_GROUPS):
            m = (tile * SORT_GROUPS + sub) * 2
            _tile_run_copies(lambda e: jnp.where(live, meta_ref[(m + 1) * N_EXPERTS + e], 0),
                             y_hbm, lambda e: e * region + meta_ref[m * N_EXPERTS + e],
                             ybuf.at[which, sub], None, sem.at[which])

    @pl.when(step == 0)
    def _():
        for ahead in range(COMBINE_BUFFERS - 1):
            fetch_tile(ahead, ahead < n_steps)

    for sub in range(SORT_GROUPS):
        pltpu.make_async_copy(y_hbm.at[pl.ds(0, SORT_ROWS * ROW_WORDS), :], ybuf.at[slot, sub],
                              sem.at[slot]).wait()
    sorted_rows = [_unpack_rows(ybuf.at[slot, sub], SORT_ROWS).astype(BF16) for sub in range(SORT_GROUPS)]

    fetch_tile(step + COMBINE_BUFFERS - 1, step + COMBINE_BUFFERS - 1 < n_steps)

    rt_scr[...] = jnp.zeros_like(rt_scr)
    rt_scr[0:SUBLANES, :] = route_ref[...]
    cols = jnp.concatenate([rt_scr[:, c * LANES:(c + 1) * LANES].T for c in range(tm // LANES)], axis=0)

    ts = SORT_TOKENS
    spos = lax.broadcasted_iota(I32, (ts, SORT_ROWS), 1).astype(F32)
    def rows_of(sub):
        return slice(sub * ts, (sub + 1) * ts)

    def gate_matrix(sub, st):
        key1, key2, w1, w2 = (cols[rows_of(sub), c:c + 1] for c in range(4))
        st["gates"] = jnp.where(spos == key1, w1, jnp.where(spos == key2, w2, 0.0)).astype(BF16)

    def weighted_unsort(sub, st):
        st["y"] = _dot(st.pop("gates"), sorted_rows[sub])

    def norm_and_add(sub, st):
        rows = rows_of(sub)
        o_ref[rows, :] = xmid_ref[rows, :] + _rmsnorm(st.pop("y"), gt_ref[0] * g_ref[...])

    _run_staggered((gate_matrix, weighted_unsort, norm_and_add), SORT_GROUPS, lag=1)


def _combine(meta, ys, route, x_mid, gt_f, g_post, seq):
    t, d = x_mid.shape
    tm = ROW_TILE
    tiles_per_seq = seq // tm
    grid_spec = pltpu.PrefetchScalarGridSpec(
        num_scalar_prefetch=1,
        grid=(t // tm,),
        in_specs=[pl.BlockSpec(memory_space=pl.ANY),
                  pl.BlockSpec((SUBLANES, tm), lambda i, m: (0, i)),
                  pl.BlockSpec((tm, d), lambda i, m: (i, 0)),
                  pl.BlockSpec((1, 1, d), lambda i, m: (i // tiles_per_seq, 0, 0)),
                  pl.BlockSpec((1, d), lambda i, m: (0, 0))],
        out_specs=pl.BlockSpec((tm, d), lambda i, m: (i, 0)),
        scratch_shapes=[pltpu.VMEM((COMBINE_BUFFERS, SORT_GROUPS, SORT_ROWS * ROW_WORDS, LANES), I32),
                        pltpu.VMEM((LANES, tm), F32),
                        pltpu.SemaphoreType.DMA((COMBINE_BUFFERS,))],
    )
    return pl.pallas_call(
        _combine_kernel,
        grid_spec=grid_spec,
        out_shape=jax.ShapeDtypeStruct((t, d), F32),
        compiler_params=pltpu.CompilerParams(
            dimension_semantics=("arbitrary",), vmem_limit_bytes=VMEM_LIMIT),
        name="combine",
    )(meta, ys, route, x_mid, gt_f, g_post)


def kernel(x, c, w_ada, b_ada, g_pre_mix, g_post_mix, g_pre_ffn, g_post_ffn, w_in, w_pool_group, pool_scale, w_branch_pool, w_branch_att, w_out, w_group_router, b_group_router, w_expert_router, b_expert_router, w_exp_gate, w_exp_up, w_exp_down):
    b, s, d = x.shape
    assert d == D_MODEL and s % (ATT_BLOCK * 2 * ATT_DILATIONS[2]) == 0 and s % ROW_TILE == 0
    assert (b * s) % MOE_BLOCK == 0
    depth = w_ada.shape[0]
    slopes = jnp.exp2(-ALIBI_MAX_BIAS * jnp.arange(1, N_ATT_HEADS + 1, dtype=F32) / N_ATT_HEADS)

    for layer in range(depth):
        mod = _adaln(c, w_ada[layer], b_ada[layer]).reshape(b, 6, 1, d)
        sh_m, sc_m, gt_m, sh_f, sc_f, gt_f = [mod[:, j] for j in range(6)]

        u, qkv0, qkv1, qkv2, gates = _inproj(x, g_pre_mix[layer].reshape(1, d), sc_m, sh_m,
                                             w_in[layer].astype(BF16))
        att = _attention(qkv0, qkv1, qkv2, slopes)

        zero_block = jnp.zeros((POOL_GROUP_DIM, POOL_GROUP_DIM), F32)
        wpg_diag = jnp.block([[w_pool_group[layer, g] if g == h else zero_block
                               for h in range(len(POOL_WINDOWS))] for g in range(len(POOL_WINDOWS))])

        pad_rows = ROUTER_ROWS - N_EXPERTS - N_EXPERT_GROUPS
        wrt = jnp.concatenate([w_expert_router[layer].T, w_group_router[layer].T,
                               jnp.zeros((pad_rows, d), F32)], axis=0)
        wrt_hi = wrt.astype(BF16)
        wr = jnp.concatenate([wrt_hi, (wrt - wrt_hi.astype(F32)).astype(BF16)], axis=0)
        br = jnp.concatenate([b_expert_router[layer], b_group_router[layer],
                              jnp.zeros((pad_rows,), F32)]).reshape(ROUTER_ROWS, 1)

        x_mid, route, meta, counts, xs = _mixtail(
            u, att, gates, x, gt_m, sc_f, sh_f,
            g_post_mix[layer].reshape(1, d), g_pre_ffn[layer].reshape(1, d),
            wpg_diag.astype(BF16), pool_scale[layer].reshape(1, POOL_WIDTH),
            w_branch_pool[layer].astype(BF16), w_branch_att[layer].astype(BF16),
            w_out[layer].astype(BF16), wr, br)

        ys = _experts(xs, counts[0, :N_EXPERTS], w_exp_gate[layer], w_exp_up[layer], w_exp_down[layer])
        run_meta = meta[:, 0:2, 0:N_EXPERTS].reshape(-1)
        x = _combine(run_meta, ys, route, x_mid.reshape(b * s, d), gt_f,
                     g_post_ffn[layer].reshape(1, d), s).reshape(b, s, d)
    return x
```
